```python
import math
import jax, jax.numpy as jnp
from jax import lax
import numpy as np

D_MODEL = 1024
BATCH = 1
SEQ = 16384
DEPTH = 2

N_EVEN = (DEPTH + 1) // 2
N_ODD = DEPTH // 2
A_HEADS = 8
A_HEAD_DIM = 64
IDX_HEADS = 4
IDX_DIM = 64
TOPK_MAX = 256
Q_BLOCK = 128
B_HEADS = 4
B_DK = 64
B_DV = 128
GLA_RANK = 16
GLA_TAU = 16.0
GLA_CHUNK = 64
C_WIDTH = D_MODEL
CONV_WIDTH = 31
REL_BUCKETS = 32
REL_MAX_EXACT = 16
REL_MAX_DIST = 128
EPS = 1e-6

A_W = A_HEADS * A_HEAD_DIM
B_KW = B_HEADS * B_DK
B_VW = B_HEADS * B_DV
EVEN_SPLIT = [A_W, A_W, A_W, A_W,
              IDX_HEADS * IDX_DIM, IDX_DIM, IDX_HEADS,
              B_KW, B_KW, B_VW, B_VW, GLA_RANK]
EVEN_COLS = int(sum(EVEN_SPLIT))
ODD_SPLIT = [C_WIDTH, C_WIDTH, C_WIDTH]
ODD_COLS = int(sum(ODD_SPLIT))
MIX_W = A_W + B_VW

kernel_name = "hybrid_dsa_gla_conformer_block"


def _split(x, sizes):
    return jnp.split(x, [int(v) for v in np.cumsum(sizes)[:-1]], axis=-1)


def rmsnorm(x, g):
    xf = x.astype(jnp.float32)
    y = xf * lax.rsqrt(jnp.mean(xf * xf, axis=-1, keepdims=True) + EPS)
    return (y * g.astype(jnp.float32)).astype(x.dtype)


def layernorm(x, g, b):
    xf = x.astype(jnp.float32)
    mu = jnp.mean(xf, axis=-1, keepdims=True)
    var = jnp.mean(jnp.square(xf - mu), axis=-1, keepdims=True)
    y = (xf - mu) * lax.rsqrt(var + EPS)
    return (y * g.astype(jnp.float32) + b.astype(jnp.float32)).astype(x.dtype)


def t5_bucket(dist):
    dist = jnp.maximum(dist, 0)
    is_small = dist < REL_MAX_EXACT
    d = jnp.maximum(dist, 1).astype(jnp.float32)
    large = REL_MAX_EXACT + (jnp.log(d / REL_MAX_EXACT)
                             / math.log(REL_MAX_DIST / REL_MAX_EXACT)
                             * (REL_BUCKETS - REL_MAX_EXACT)).astype(jnp.int32)
    large = jnp.minimum(large, REL_BUCKETS - 1)
    return jnp.where(is_small, dist, large)


def dsa_attention(q, k, v, q_idx, k_idx, w_idx, rel_bias, top_k):
    B, S, H, Dh = q.shape
    n_blocks = S // Q_BLOCK
    s_pos = jnp.arange(S, dtype=jnp.int32)
    scale = Dh ** -0.5
    idx_scale = (IDX_DIM ** -0.5) * (IDX_HEADS ** -0.5)

    def block(i):
        t0 = i * Q_BLOCK
        t = t0 + jnp.arange(Q_BLOCK, dtype=jnp.int32)
        qi = lax.dynamic_slice_in_dim(q_idx, t0, Q_BLOCK, axis=1)
        wi = lax.dynamic_slice_in_dim(w_idx, t0, Q_BLOCK, axis=1)
        qa = lax.dynamic_slice_in_dim(q, t0, Q_BLOCK, axis=1)
        sc = jax.nn.relu(jnp.einsum('bqhd,bsd->bqhs', qi, k_idx).astype(jnp.float32))
        score = jnp.einsum('bqh,bqhs->bqs', wi.astype(jnp.float32), sc) * idx_scale
        score = jnp.where(s_pos[None, None, :] <= t[None, :, None], score, -jnp.inf)
        _, sel = lax.top_k(score, top_k)
        valid = sel <= t[None, :, None]
        ks = jax.vmap(lambda a, ii: a[ii])(k, sel)
        vs = jax.vmap(lambda a, ii: a[ii])(v, sel)
        logits = jnp.einsum('bqhd,bqkhd->bhqk', qa, ks).astype(jnp.float32) * scale
        bias = rel_bias.astype(jnp.float32)[t5_bucket(t[None, :, None] - sel)]
        logits = logits + jnp.transpose(bias, (0, 3, 1, 2))
        logits = jnp.where(valid[:, None, :, :], logits, -jnp.inf)
        p = jax.nn.softmax(logits, axis=-1).astype(vs.dtype)
        return jnp.einsum('bhqk,bqkhd->bqhd', p, vs)

    out = lax.map(block, jnp.arange(n_blocks, dtype=jnp.int32))
    return jnp.transpose(out, (1, 0, 2, 3, 4)).reshape(B, S, H, Dh)


def gla(q, k, v, log_a):
    B, S, H, DK = q.shape
    DV = v.shape[-1]
    C = GLA_CHUNK
    N = S // C

    def to_chunks(t):
        return jnp.transpose(t.astype(jnp.float32).reshape(B, N, C, H, t.shape[-1]),
                             (1, 0, 3, 2, 4))

    qc = to_chunks(q) * (DK ** -0.5)
    kc, vc = to_chunks(k), to_chunks(v)
    bc = jnp.cumsum(to_chunks(log_a), axis=-2)
    causal = jnp.tril(jnp.ones((C, C), dtype=bool))

    def step(state, inp):
        qi, ki, vi, bi = inp
        o_inter = jnp.einsum('bhcd,bhde->bhce', qi * jnp.exp(bi), state)
        diff = bi[:, :, :, None, :] - bi[:, :, None, :, :]
        decay = jnp.exp(jnp.where(causal[None, None, :, :, None], diff, -jnp.inf))
        att = jnp.einsum('bhid,bhjd,bhijd->bhij', qi, ki, decay)
        o_intra = jnp.einsum('bhij,bhje->bhie', att, vi)
        b_last = bi[:, :, -1:, :]
        new_state = (jnp.exp(b_last[:, :, 0, :])[..., None] * state
                     + jnp.einsum('bhjd,bhje->bhde', ki * jnp.exp(b_last - bi), vi))
        return new_state, o_inter + o_intra

    s0 = jnp.zeros((B, H, DK, DV), jnp.float32)
    _, out = lax.scan(step, s0, (qc, kc, vc, bc))
    return jnp.transpose(out, (1, 0, 3, 2, 4)).reshape(B, S, H, DV)


def causal_depthwise_conv(u, w, b):
    Cc = u.shape[-1]
    y = lax.conv_general_dilated(
        u, w.astype(u.dtype)[:, None, :], window_strides=(1,),
        padding=[(CONV_WIDTH - 1, 0)],
        dimension_numbers=('NWC', 'WIO', 'NWC'), feature_group_count=Cc)
    return y + b.astype(u.dtype)


def setup_inputs(seed: int = 0) -> dict:
    key = jax.random.key(seed)
    ks = jax.random.split(key, 20)
    f32 = jnp.float32
    nrm = lambda k, shape, s: jax.random.normal(k, shape, f32) * s
    return {
        "x": nrm(ks[0], (BATCH, SEQ, D_MODEL), 1.0),
        "rel_bias": nrm(ks[1], (REL_BUCKETS, A_HEADS), 0.5),
        "norm_even": 1.0 + nrm(ks[2], (N_EVEN, D_MODEL), 0.02),
        "w_in_even": nrm(ks[3], (N_EVEN, D_MODEL, EVEN_COLS), D_MODEL ** -0.5),
        "w_gla_lr": nrm(ks[4], (N_EVEN, GLA_RANK, B_KW), GLA_RANK ** -0.5),
        "b_gla_lr": nrm(ks[5], (N_EVEN, B_KW), 0.1),
        "gla_norm": 1.0 + nrm(ks[6], (N_EVEN, B_DV), 0.02),
        "w_out_even": nrm(ks[7], (N_EVEN, MIX_W, D_MODEL), MIX_W ** -0.5),
        "norm_odd": 1.0 + nrm(ks[8], (N_ODD, D_MODEL), 0.02),
        "w_in_odd": nrm(ks[9], (N_ODD, D_MODEL, ODD_COLS), D_MODEL ** -0.5),
        "conv_w": nrm(ks[10], (N_ODD, CONV_WIDTH, C_WIDTH), CONV_WIDTH ** -0.5),
        "conv_b": nrm(ks[11], (N_ODD, C_WIDTH), 0.02),
        "conv_ln_g": 1.0 + nrm(ks[12], (N_ODD, C_WIDTH), 0.02),
        "conv_ln_b": nrm(ks[13], (N_ODD, C_WIDTH), 0.02),
        "w_out_odd": nrm(ks[14], (N_ODD, C_WIDTH, D_MODEL), C_WIDTH ** -0.5),
        "norm_f": 1.0 + nrm(ks[15], (D_MODEL,), 0.02),
    }


def reference(x, rel_bias, norm_even, w_in_even, w_gla_lr, b_gla_lr, gla_norm,
              w_out_even, norm_odd, w_in_odd, conv_w, conv_b, conv_ln_g,
              conv_ln_b, w_out_odd, norm_f):
    B, S, _ = x.shape
    top_k = min(TOPK_MAX, S // 4)
    for layer in range(DEPTH):
        if layer % 2 == 0:
            i = layer // 2
            h = rmsnorm(x, norm_even[i])
            proj = h @ w_in_even[i]
            (qa, ka, va, ga, qi, ki, wi, qb, kb, vb, gb, lr) = _split(proj, EVEN_SPLIT)
            o_a = dsa_attention(
                qa.reshape(B, S, A_HEADS, A_HEAD_DIM),
                ka.reshape(B, S, A_HEADS, A_HEAD_DIM),
                va.reshape(B, S, A_HEADS, A_HEAD_DIM),
                qi.reshape(B, S, IDX_HEADS, IDX_DIM), ki, wi, rel_bias, top_k)
            o_a = o_a.reshape(B, S, A_W) * jax.nn.silu(ga)
            z = lr @ w_gla_lr[i] + b_gla_lr[i]
            log_a = jax.nn.log_sigmoid(z.astype(jnp.float32)) / GLA_TAU
            o_b = gla(qb.reshape(B, S, B_HEADS, B_DK),
                      kb.reshape(B, S, B_HEADS, B_DK),
                      vb.reshape(B, S, B_HEADS, B_DV),
                      log_a.reshape(B, S, B_HEADS, B_DK))
            o_b = o_b * lax.rsqrt(jnp.mean(o_b * o_b, axis=-1, keepdims=True) + EPS)
            o_b = (o_b * gla_norm[i].astype(jnp.float32)).astype(x.dtype)
            o_b = o_b.reshape(B, S, B_VW) * jax.nn.silu(gb)
            x = x + jnp.concatenate([o_a, o_b], axis=-1) @ w_out_even[i]
        else:
            j = layer // 2
            h = rmsnorm(x, norm_odd[j])
            proj = h @ w_in_odd[j]
            val, glu_g, gate = _split(proj, ODD_SPLIT)
            u = val * jax.nn.sigmoid(glu_g)
            u = causal_depthwise_conv(u, conv_w[j], conv_b[j])
            u = jax.nn.silu(layernorm(u, conv_ln_g[j], conv_ln_b[j]))
            x = x + (u * jax.nn.silu(gate)) @ w_out_odd[j]
    return rmsnorm(x, norm_f)
```

```python
import functools
import math

import jax
import jax.numpy as jnp
import numpy as np
from jax import lax
from jax.experimental import pallas as pl
from jax.experimental.pallas import tpu as pltpu

D_MODEL = 1024
A_HEADS = 8
A_HEAD_DIM = 64
IDX_HEADS = 4
IDX_DIM = 64
TOPK_MAX = 256
B_HEADS = 4
B_DK = 64
B_DV = 128
GLA_RANK = 16
GLA_TAU = 16.0
GLA_CHUNK = 64
GLA_SUB = 16
CONV_WIDTH = 31
REL_BUCKETS = 32
REL_MAX_EXACT = 16
REL_MAX_DIST = 128
EPS = 1e-6

A_W = A_HEADS * A_HEAD_DIM
B_KW = B_HEADS * B_DK
B_VW = B_HEADS * B_DV

LANES = 128
VMEM_LIMIT = 56 * 1024 * 1024

F32 = jnp.float32
BF16 = jnp.bfloat16
NEG_INF = float("-inf")
INT_MIN = -(2 ** 31)
NEG_INF_KEY = int(np.array(-np.inf, np.float32).view(np.int32)) ^ 0x7FFFFFFF
M_INIT = -1e30

C_QA = 0
C_KA = C_QA + A_HEADS * LANES
C_VA = C_KA + A_W
C_GA = C_VA + A_W
C_QI = C_GA + A_W
C_KI = C_QI + IDX_HEADS * LANES
C_WI = C_KI + LANES
C_QB = C_WI + LANES
C_KB = C_QB + B_HEADS * LANES
C_VB = C_KB + B_HEADS * LANES
C_GB = C_VB + B_VW
C_LR = C_GB + B_VW
C_END = C_LR + LANES


def _t5_bucket_np(dist):
    dist = np.maximum(dist, 0)
    d = np.maximum(dist, 1).astype(np.float32)
    large = REL_MAX_EXACT + (np.log(d / REL_MAX_EXACT) / math.log(REL_MAX_DIST / REL_MAX_EXACT)
                             * (REL_BUCKETS - REL_MAX_EXACT)).astype(np.int32)
    large = np.minimum(large, REL_BUCKETS - 1)
    return np.where(dist < REL_MAX_EXACT, dist, large)


def _silu(x):
    return x * jax.nn.sigmoid(x)


def _even_in_kernel(x_ref, g_ref, w_ref, qa_ref, ka_ref, va_ref, ga_ref, qi_ref, ki_ref, wi_ref,
                    qb_ref, kb_ref, vb_ref, gb_ref, lr_ref):
    x = x_ref[...]
    h = (x * lax.rsqrt(jnp.mean(x * x, axis=-1, keepdims=True) + EPS) * g_ref[...]).astype(BF16)

    def proj(lo, hi):
        return jnp.dot(h, w_ref[:, lo:hi], preferred_element_type=F32)

    qa_ref[...] = proj(C_QA, C_KA).astype(BF16)
    ka_ref[...] = proj(C_KA, C_VA).astype(BF16)
    va_ref[...] = proj(C_VA, C_GA).astype(BF16)
    ga_ref[...] = proj(C_GA, C_QI)
    qi_ref[...] = proj(C_QI, C_KI).astype(BF16)
    ki_ref[...] = proj(C_KI, C_WI).astype(BF16)
    wi_ref[...] = proj(C_WI, C_QB)
    qb_ref[...] = proj(C_QB, C_KB)
    kb_ref[...] = proj(C_KB, C_VB)
    vb_ref[...] = proj(C_VB, C_GB)
    gb_ref[...] = proj(C_GB, C_LR)
    lr_ref[...] = proj(C_LR, C_END)


def _even_in_weights(w):
    cols = np.cumsum([0, A_W, A_W, A_W, A_W, IDX_HEADS * IDX_DIM, IDX_DIM, IDX_HEADS,
                      B_KW, B_KW, B_VW, B_VW, GLA_RANK])
    (s_qa, s_ka, s_va, s_ga, s_qi, s_ki, s_wi, s_qb, s_kb, s_vb, s_gb, s_lr, _) = [int(c) for c in cols]
    out = jnp.zeros((D_MODEL, C_END), F32)
    a_scale = A_HEAD_DIM ** -0.5
    idx_scale = (IDX_DIM ** -0.5) * (IDX_HEADS ** -0.5)
    b_scale = B_DK ** -0.5
    for h in range(A_HEADS):
        dst = C_QA + h * LANES + (h % 2) * A_HEAD_DIM
        out = out.at[:, dst:dst + A_HEAD_DIM].set(w[:, s_qa + h * A_HEAD_DIM:s_qa + (h + 1) * A_HEAD_DIM] * a_scale)
    out = out.at[:, C_KA:C_KA + A_W].set(w[:, s_ka:s_ka + A_W])
    out = out.at[:, C_VA:C_VA + A_W].set(w[:, s_va:s_va + A_W])
    out = out.at[:, C_GA:C_GA + A_W].set(w[:, s_ga:s_ga + A_W])
    for h in range(IDX_HEADS):
        dst = C_QI + h * LANES
        out = out.at[:, dst:dst + IDX_DIM].set(w[:, s_qi + h * IDX_DIM:s_qi + (h + 1) * IDX_DIM])
    out = out.at[:, C_KI:C_KI + IDX_DIM].set(w[:, s_ki:s_ki + IDX_DIM])
    out = out.at[:, C_WI:C_WI + IDX_HEADS].set(w[:, s_wi:s_wi + IDX_HEADS] * idx_scale)
    for h in range(B_HEADS):
        dq = C_QB + h * LANES
        dk = C_KB + h * LANES
        out = out.at[:, dq:dq + B_DK].set(w[:, s_qb + h * B_DK:s_qb + (h + 1) * B_DK] * b_scale)
        out = out.at[:, dk:dk + B_DK].set(w[:, s_kb + h * B_DK:s_kb + (h + 1) * B_DK])
    out = out.at[:, C_VB:C_VB + B_VW].set(w[:, s_vb:s_vb + B_VW])
    out = out.at[:, C_GB:C_GB + B_VW].set(w[:, s_gb:s_gb + B_VW])
    out = out.at[:, C_LR:C_LR + GLA_RANK].set(w[:, s_lr:s_lr + GLA_RANK])
    return out.astype(BF16)


def _even_in(x2d, g, w_pad, tm):
    s = x2d.shape[0]
    widths = [(C_KA - C_QA, BF16), (A_W, BF16), (A_W, BF16), (A_W, F32), (C_KI - C_QI, BF16), (LANES, BF16),
              (LANES, F32), (C_KB - C_QB, F32), (C_VB - C_KB, F32), (B_VW, F32), (B_VW, F32), (LANES, F32)]
    return pl.pallas_call(
        _even_in_kernel,
        grid=(s // tm,),
        in_specs=[pl.BlockSpec((tm, D_MODEL), lambda i: (i, 0)),
                  pl.BlockSpec((1, D_MODEL), lambda i: (0, 0)),
                  pl.BlockSpec((D_MODEL, C_END), lambda i: (0, 0))],
        out_specs=[pl.BlockSpec((tm, wd), lambda i: (i, 0)) for wd, _ in widths],
        out_shape=[jax.ShapeDtypeStruct((s, wd), dt) for wd, dt in widths],
        compiler_params=pltpu.CompilerParams(dimension_semantics=("arbitrary",), vmem_limit_bytes=VMEM_LIMIT),
        name="even_in",
    )(x2d, g, w_pad)


DSA_Q = 128
DSA_KB = 128


def _f32_key(x):
    bits = lax.bitcast_convert_type(x, jnp.int32)
    return bits ^ (lax.shift_right_arithmetic(bits, 31) & 0x7FFFFFFF)


def _dsa_kernel(qa_ref, qi_ref, wi_ref, ga_ref, ka_ref, va_ref, ki_ref, btab_ref, tri_ref, o_ref,
                slab_ref, acc_ref, m_ref, l_ref, *, top_k):
    q, kb = DSA_Q, DSA_KB
    i = pl.program_id(0)
    nkb = i + 1
    t_col = i * q + lax.broadcasted_iota(jnp.int32, (q, 1), 0)
    lane = lax.broadcasted_iota(jnp.int32, (1, kb), 1)
    dn_t = (((1,), (1,)), ((), ()))

    w = wi_ref[...]

    def score_tile(j, c):
        kblk = ki_ref[pl.ds(pl.multiple_of(j * kb, kb), kb), :]
        sc = jnp.zeros((q, kb), F32)
        for h in range(IDX_HEADS):
            s = lax.dot_general(qi_ref[:, h * LANES:(h + 1) * LANES], kblk, dn_t, preferred_element_type=F32)
            sc = sc + w[:, h:h + 1] * jnp.maximum(s, 0.0)
        sc = jnp.where(j * kb + lane <= t_col, sc, NEG_INF)
        slab_ref[j] = _f32_key(sc)
        return c

    lax.fori_loop(0, nkb, score_tile, 0)

    def count_ge(trial):
        tb = jnp.broadcast_to(trial, (q, kb))

        def body(j, acc):
            return acc + jnp.where(slab_ref[j] >= tb, 1.0, 0.0)

        acc = lax.fori_loop(0, nkb, body, jnp.zeros((q, kb), F32))
        return jnp.sum(acc, axis=-1, keepdims=True).astype(jnp.int32)

    def bisect(b, c):
        trial = c + lax.shift_left(jnp.int32(1), 31 - b)
        return jnp.where(count_ge(trial) >= top_k, trial, c)

    thr = lax.fori_loop(0, 32, bisect, jnp.full((q, 1), INT_MIN, jnp.int32))
    cnt_ge = count_ge(thr)
    cnt_gt = count_ge(thr + 1)
    need = top_k - cnt_gt
    tied = jnp.logical_and(cnt_ge > top_k, thr > NEG_INF_KEY)
    any_tie = jnp.max(jnp.where(tied, 1, 0)) > 0

    neg_inf_bits = jnp.int32(int(np.array(-np.inf, np.float32).view(np.int32)))
    thr_valid = jnp.maximum(thr, NEG_INF_KEY + 1)

    def mask_fast():
        def body(j, c):
            slab_ref[j] = jnp.where(slab_ref[j] >= thr_valid, 0, neg_inf_bits)
            return c
        lax.fori_loop(0, nkb, body, 0)

    def mask_tied():
        def body(j, seen):
            key = slab_ref[j]
            eq = key == thr
            eq_f = jnp.where(eq, 1.0, 0.0)
            before = jnp.dot(eq_f.astype(BF16), tri_ref[...], preferred_element_type=F32)
            rank = seen + before
            sel = jnp.logical_or(key > thr, jnp.logical_and(eq, rank < need.astype(F32)))
            sel = jnp.logical_and(sel, key > NEG_INF_KEY)
            slab_ref[j] = jnp.where(sel, 0, neg_inf_bits)
            return seen + jnp.sum(eq_f, axis=-1, keepdims=True)
        lax.fori_loop(0, nkb, body, jnp.zeros((q, 1), F32))

    lax.cond(any_tie, mask_tied, mask_fast)

    m_ref[...] = jnp.full(m_ref.shape, M_INIT, F32)
    l_ref[...] = jnp.zeros(l_ref.shape, F32)
    acc_ref[...] = jnp.zeros(acc_ref.shape, F32)

    def attn_tile(j, c):
        mb = lax.bitcast_convert_type(slab_ref[j], F32)
        rel = jnp.minimum(i - j, 2)
        row0 = pl.multiple_of(j * kb, kb)
        for p in range(A_HEADS // 2):
            kp = ka_ref[pl.ds(row0, kb), p * LANES:(p + 1) * LANES]
            vp = va_ref[pl.ds(row0, kb), p * LANES:(p + 1) * LANES]
            for e in range(2):
                h = 2 * p + e
                s = lax.dot_general(qa_ref[:, h * LANES:(h + 1) * LANES], kp, dn_t, preferred_element_type=F32)
                s = s + btab_ref[rel, h] + mb
                m_old = m_ref[h]
                m_new = jnp.maximum(m_old, jnp.max(s, axis=-1, keepdims=True))
                alpha = jnp.exp(m_old - m_new)
                pexp = jnp.exp(s - m_new)
                m_ref[h] = m_new
                l_ref[h] = alpha * l_ref[h] + pexp
                acc_ref[h] = alpha * acc_ref[h] + jnp.dot(pexp.astype(BF16), vp, preferred_element_type=F32)
        return c

    lax.fori_loop(0, nkb, attn_tile, 0)

    for p in range(A_HEADS // 2):
        halves = []
        for e in range(2):
            h = 2 * p + e
            denom = jnp.sum(l_ref[h], axis=-1, keepdims=True)
            halves.append(acc_ref[h] / denom)
        o_pair = jnp.where(lane < A_HEAD_DIM, halves[0], halves[1])
        o_ref[:, p * LANES:(p + 1) * LANES] = (o_pair * _silu(ga_ref[:, p * LANES:(p + 1) * LANES])).astype(o_ref.dtype)


def _dsa_bias_table(rel_bias):
    ii = np.arange(DSA_Q)[:, None]
    jj = np.arange(DSA_KB)[None, :]
    buckets = np.stack([_t5_bucket_np(ii - jj), _t5_bucket_np(DSA_KB + ii - jj),
                        np.full((DSA_Q, DSA_KB), REL_BUCKETS - 1)])
    assert (_t5_bucket_np(np.arange(DSA_KB + 1, 4 * DSA_KB)) == REL_BUCKETS - 1).all()
    rb = rel_bias.astype(F32)
    tab = jnp.transpose(rb[buckets], (0, 3, 1, 2))
    return tab - rb[REL_BUCKETS - 1][None, :, None, None]


def _dsa(qa, qi, wi, ga, ka, va, ki, btab, top_k):
    s = qa.shape[0]
    q, kb = DSA_Q, DSA_KB
    tri = jnp.asarray(np.triu(np.ones((kb, kb), np.float32), 1), BF16)
    whole = lambda shape: pl.BlockSpec(shape, lambda i: (0,) * len(shape), pipeline_mode=pl.Buffered(1))
    return pl.pallas_call(
        functools.partial(_dsa_kernel, top_k=top_k),
        grid=(s // q,),
        in_specs=[pl.BlockSpec((q, A_HEADS * LANES), lambda i: (i, 0)),
                  pl.BlockSpec((q, IDX_HEADS * LANES), lambda i: (i, 0)),
                  pl.BlockSpec((q, LANES), lambda i: (i, 0)),
                  pl.BlockSpec((q, A_W), lambda i: (i, 0)),
                  whole((s, A_W)), whole((s, A_W)), whole((s, LANES)),
                  whole((3, A_HEADS, q, kb)), whole((kb, kb))],
        out_specs=pl.BlockSpec((q, A_W), lambda i: (i, 0)),
        out_shape=jax.ShapeDtypeStruct((s, A_W), BF16),
        scratch_shapes=[pltpu.VMEM((s // kb, q, kb), jnp.int32),
                        pltpu.VMEM((A_HEADS, q, LANES), F32),
                        pltpu.VMEM((A_HEADS, q, LANES), F32),
                        pltpu.VMEM((A_HEADS, q, LANES), F32)],
        compiler_params=pltpu.CompilerParams(dimension_semantics=("arbitrary",), vmem_limit_bytes=VMEM_LIMIT),
        name="dsa",
    )(qa, qi, wi, ga, ka, va, ki, btab, tri)


GLA_BLOCK = 512


def _gla_kernel(qb_ref, kb_ref, vb_ref, gb_ref, lr_ref, wlr_ref, blr_ref, gn_ref, tril_ref, o_ref,
                state_ref, b_ref, att_ref):
    c, sub = GLA_CHUNK, GLA_SUB
    nsub = c // sub

    @pl.when(pl.program_id(0) == 0)
    def _():
        state_ref[...] = jnp.zeros(state_ref.shape, F32)

    dn_t = (((1,), (1,)), ((), ()))
    dn_0 = (((0,), (0,)), ((), ()))
    hp = lax.Precision.HIGHEST
    row_i = lax.broadcasted_iota(jnp.int32, (sub, sub), 0)
    col_j = lax.broadcasted_iota(jnp.int32, (sub, sub), 1)

    def chunk(ci, carry):
        r0 = pl.multiple_of(ci * c, c)
        z = jnp.dot(lr_ref[pl.ds(r0, c), :], wlr_ref[...], precision=hp, preferred_element_type=F32) + blr_ref[...]
        log_a = (jnp.minimum(z, 0.0) - jnp.log1p(jnp.exp(-jnp.abs(z)))) * (1.0 / GLA_TAU)
        b_ref[...] = jnp.dot(tril_ref[...], log_a, precision=hp, preferred_element_type=F32)
        for h in range(B_HEADS):
            ls = slice(h * LANES, (h + 1) * LANES)
            b = b_ref[:, ls]
            qh = qb_ref[pl.ds(r0, c), ls]
            kh = kb_ref[pl.ds(r0, c), ls]
            vh = vb_ref[pl.ds(r0, c), ls]
            state_t = state_ref[h]
            b_last = b[c - 1:c, :]
            o = lax.dot_general((qh * jnp.exp(b)).astype(BF16), state_t.astype(BF16), dn_t,
                                preferred_element_type=F32)
            for si in range(nsub):
                rs = slice(si * sub, (si + 1) * sub)
                b_i, q_i = b[rs], qh[rs]
                if si > 0:
                    ref_row = b[si * sub:si * sub + 1, :]
                    q_t = (q_i * jnp.exp(b_i - ref_row)).astype(BF16)
                    k_t = (kh[:si * sub] * jnp.exp(ref_row - b[:si * sub])).astype(BF16)
                    att_ref[rs, :si * sub] = lax.dot_general(q_t, k_t, dn_t, preferred_element_type=F32)
                k_i = kh[rs]
                diag = jnp.zeros((sub, sub), F32)
                for jl in range(sub):
                    diff = jnp.minimum(b_i - b_i[jl:jl + 1, :], 0.0)
                    col = jnp.sum(q_i * k_i[jl:jl + 1, :] * jnp.exp(diff), axis=-1, keepdims=True)
                    diag = jnp.where(col_j == jl, col, diag)
                att_ref[rs, rs] = jnp.where(col_j <= row_i, diag, 0.0)
                if si + 1 < nsub:
                    att_ref[rs, (si + 1) * sub:] = jnp.zeros((sub, c - (si + 1) * sub), F32)
            o = o + jnp.dot(att_ref[...].astype(BF16), vh.astype(BF16), preferred_element_type=F32)
            k_dec = (kh * jnp.exp(b_last - b)).astype(BF16)
            state_ref[h] = (jnp.exp(b_last) * state_t
                            + lax.dot_general(vh.astype(BF16), k_dec, dn_0, preferred_element_type=F32))
            o = o * lax.rsqrt(jnp.mean(o * o, axis=-1, keepdims=True) + EPS) * gn_ref[...]
            o_ref[pl.ds(r0, c), ls] = (o * _silu(gb_ref[pl.ds(r0, c), ls])).astype(o_ref.dtype)
        return carry

    lax.fori_loop(0, GLA_BLOCK // c, chunk, 0)


def _gla(qb, kb, vb, gb, lr, wlr_pad, blr_pad, gn):
    s = qb.shape[0]
    blk = min(GLA_BLOCK, s)
    assert blk == GLA_BLOCK and s % blk == 0
    tril = jnp.asarray(np.tril(np.ones((GLA_CHUNK, GLA_CHUNK), np.float32)))
    row = lambda wd: pl.BlockSpec((blk, wd), lambda i: (i, 0))
    const = lambda shape: pl.BlockSpec(shape, lambda i: (0,) * len(shape))
    return pl.pallas_call(
        _gla_kernel,
        grid=(s // blk,),
        in_specs=[row(B_HEADS * LANES), row(B_HEADS * LANES), row(B_VW), row(B_VW), row(LANES),
                  const((LANES, B_HEADS * LANES)), const((1, B_HEADS * LANES)), const((1, B_DV)),
                  const((GLA_CHUNK, GLA_CHUNK))],
        out_specs=row(B_VW),
        out_shape=jax.ShapeDtypeStruct((s, B_VW), BF16),
        scratch_shapes=[pltpu.VMEM((B_HEADS, LANES, B_DV), F32),
                        pltpu.VMEM((GLA_CHUNK, B_HEADS * LANES), F32),
                        pltpu.VMEM((GLA_CHUNK, GLA_CHUNK), F32)],
        compiler_params=pltpu.CompilerParams(dimension_semantics=("arbitrary",), vmem_limit_bytes=VMEM_LIMIT),
        name="gla",
    )(qb, kb, vb, gb, lr, wlr_pad, blr_pad, gn, tril)


def _mid_kernel(x_ref, ma_ref, mb_ref, wo_ref, g_ref, wi_ref, x1_ref, u_ref, sg_ref):
    x1 = (x_ref[...]
          + jnp.dot(ma_ref[...], wo_ref[:A_W, :], preferred_element_type=F32)
          + jnp.dot(mb_ref[...], wo_ref[A_W:, :], preferred_element_type=F32))
    x1_ref[...] = x1
    h = (x1 * lax.rsqrt(jnp.mean(x1 * x1, axis=-1, keepdims=True) + EPS) * g_ref[...]).astype(BF16)
    val = jnp.dot(h, wi_ref[:, :D_MODEL], preferred_element_type=F32)
    glu = jnp.dot(h, wi_ref[:, D_MODEL:2 * D_MODEL], preferred_element_type=F32)
    gate = jnp.dot(h, wi_ref[:, 2 * D_MODEL:], preferred_element_type=F32)
    u_ref[...] = val * jax.nn.sigmoid(glu)
    sg_ref[...] = _silu(gate)


def _mid(x2d, mix_a, mix_b, w_out, g, w_in, tm):
    s = x2d.shape[0]
    row = lambda wd: pl.BlockSpec((tm, wd), lambda i: (i, 0))
    const = lambda shape: pl.BlockSpec(shape, lambda i: (0,) * len(shape))
    return pl.pallas_call(
        _mid_kernel,
        grid=(s // tm,),
        in_specs=[row(D_MODEL), row(A_W), row(B_VW), const((A_W + B_VW, D_MODEL)), const((1, D_MODEL)),
                  const((D_MODEL, 3 * D_MODEL))],
        out_specs=[row(D_MODEL), row(D_MODEL), row(D_MODEL)],
        out_shape=[jax.ShapeDtypeStruct((s, D_MODEL), F32)] * 3,
        compiler_params=pltpu.CompilerParams(dimension_semantics=("arbitrary",), vmem_limit_bytes=VMEM_LIMIT),
        name="mid",
    )(x2d, mix_a, mix_b, w_out, g, w_in)


HALO = 32


def _odd_out_kernel(u_ref, halo_ref, sg_ref, x1_ref, cw_ref, cb_ref, lg_ref, lb_ref, wo_ref, nf_ref, o_ref,
                    ubuf_ref, y_ref, *, tm):
    i = pl.program_id(0)
    ubuf_ref[HALO:, :] = u_ref[...]
    ubuf_ref[:HALO, :] = jnp.where(i > 0, halo_ref[...], 0.0)
    base = HALO - (CONV_WIDTH - 1)
    for cs in range(D_MODEL // LANES):
        ls = slice(cs * LANES, (cs + 1) * LANES)
        acc = jnp.zeros((tm, LANES), F32)
        for k in range(CONV_WIDTH):
            acc = acc + ubuf_ref[base + k:base + k + tm, ls] * cw_ref[k:k + 1, ls]
        y_ref[:, ls] = acc + cb_ref[:, ls]
    y = y_ref[...]
    mu = jnp.mean(y, axis=-1, keepdims=True)
    yc = y - mu
    var = jnp.mean(yc * yc, axis=-1, keepdims=True)
    yn = yc * lax.rsqrt(var + EPS) * lg_ref[...] + lb_ref[...]
    mixed = (_silu(yn) * sg_ref[...]).astype(BF16)
    x2 = x1_ref[...] + jnp.dot(mixed, wo_ref[...], preferred_element_type=F32)
    o_ref[...] = x2 * lax.rsqrt(jnp.mean(x2 * x2, axis=-1, keepdims=True) + EPS) * nf_ref[...]


def _odd_out(u, sg, x1, cw_pad, cb, lg, lb, w_out, nf, tm):
    s = u.shape[0]
    row = pl.BlockSpec((tm, D_MODEL), lambda i: (i, 0))
    const = lambda shape: pl.BlockSpec(shape, lambda i: (0,) * len(shape))
    halo = pl.BlockSpec((HALO, D_MODEL), lambda i: (jnp.maximum(i * (tm // HALO) - 1, 0), 0))
    return pl.pallas_call(
        functools.partial(_odd_out_kernel, tm=tm),
        grid=(s // tm,),
        in_specs=[row, halo, row, row, const((HALO, D_MODEL)), const((1, D_MODEL)), const((1, D_MODEL)),
                  const((1, D_MODEL)), const((D_MODEL, D_MODEL)), const((1, D_MODEL))],
        out_specs=row,
        out_shape=jax.ShapeDtypeStruct((s, D_MODEL), F32),
        scratch_shapes=[pltpu.VMEM((HALO + tm, D_MODEL), F32), pltpu.VMEM((tm, D_MODEL), F32)],
        compiler_params=pltpu.CompilerParams(dimension_semantics=("arbitrary",), vmem_limit_bytes=VMEM_LIMIT),
        name="odd_out",
    )(u, u, sg, x1, cw_pad, cb, lg, lb, w_out, nf)


def kernel(x, rel_bias, norm_even, w_in_even, w_gla_lr, b_gla_lr, gla_norm, w_out_even, norm_odd, w_in_odd,
           conv_w, conv_b, conv_ln_g, conv_ln_b, w_out_odd, norm_f):
    bsz, s, d = x.shape
    assert bsz == 1 and d == D_MODEL and s % GLA_BLOCK == 0
    assert norm_even.shape[0] == 1 and norm_odd.shape[0] == 1
    top_k = min(TOPK_MAX, s // 4)
    tm = 256
    x2d = x.reshape(s, d)

    w_pad = _even_in_weights(w_in_even[0])
    qa, ka, va, ga, qi, ki, wi, qb, kb, vb, gb, lr = _even_in(x2d, norm_even[0].reshape(1, d), w_pad, tm)
    mix_a = _dsa(qa, qi, wi, ga, ka, va, ki, _dsa_bias_table(rel_bias), top_k)

    wlr_pad = jnp.zeros((LANES, B_HEADS * LANES), F32)
    blr_pad = jnp.zeros((1, B_HEADS * LANES), F32)
    for h in range(B_HEADS):
        wlr_pad = wlr_pad.at[:GLA_RANK, h * LANES:h * LANES + B_DK].set(w_gla_lr[0][:, h * B_DK:(h + 1) * B_DK])
        blr_pad = blr_pad.at[0, h * LANES:h * LANES + B_DK].set(b_gla_lr[0][h * B_DK:(h + 1) * B_DK])
    mix_b = _gla(qb, kb, vb, gb, lr, wlr_pad, blr_pad, gla_norm[0].reshape(1, B_DV))

    x1, u, sg = _mid(x2d, mix_a, mix_b, w_out_even[0].astype(BF16), norm_odd[0].reshape(1, d),
                     w_in_odd[0].astype(BF16), tm)

    cw_pad = jnp.zeros((HALO, d), F32).at[:CONV_WIDTH].set(conv_w[0])
    out = _odd_out(u, sg, x1, cw_pad, conv_b[0].reshape(1, d), conv_ln_g[0].reshape(1, d),
                   conv_ln_b[0].reshape(1, d), w_out_odd[0].astype(BF16), norm_f.reshape(1, d), tm)
    return out.reshape(bsz, s, d)
```

```python
import functools
import math

import jax
import jax.numpy as jnp
import numpy as np
from jax import lax
from jax.experimental import pallas as pl
from jax.experimental.pallas import tpu as pltpu

D_MODEL = 1024
A_HEADS = 8
A_HEAD_DIM = 64
IDX_HEADS = 4
IDX_DIM = 64
TOPK_MAX = 256
B_HEADS = 4
B_DK = 64
B_DV = 128
GLA_RANK = 16
GLA_TAU = 16.0
GLA_CHUNK = 64
GLA_SUB = 16
CONV_WIDTH = 31
REL_BUCKETS = 32
REL_MAX_EXACT = 16
REL_MAX_DIST = 128
EPS = 1e-6

A_W = A_HEADS * A_HEAD_DIM
B_KW = B_HEADS * B_DK
B_VW = B_HEADS * B_DV

LANES = 128
VMEM_LIMIT = 56 * 1024 * 1024

F32 = jnp.float32
BF16 = jnp.bfloat16
NEG_INF = float("-inf")
INT_MIN = -(2 ** 31)
NEG_INF_KEY = int(np.array(-np.inf, np.float32).view(np.int32)) ^ 0x7FFFFFFF
M_INIT = -1e30

C_QA = 0
C_KA = C_QA + A_HEADS * LANES
C_VA = C_KA + A_W
C_GA = C_VA + A_W
C_QI = C_GA + A_W
C_KI = C_QI + IDX_HEADS * LANES
C_WI = C_KI + LANES
C_QB = C_WI + LANES
C_KB = C_QB + B_HEADS * LANES
C_VB = C_KB + B_HEADS * LANES
C_GB = C_VB + B_VW
C_LR = C_GB + B_VW
C_END = C_LR + LANES


def _t5_bucket_np(dist):
    dist = np.maximum(dist, 0)
    d = np.maximum(dist, 1).astype(np.float32)
    large = REL_MAX_EXACT + (np.log(d / REL_MAX_EXACT) / math.log(REL_MAX_DIST / REL_MAX_EXACT)
                             * (REL_BUCKETS - REL_MAX_EXACT)).astype(np.int32)
    large = np.minimum(large, REL_BUCKETS - 1)
    return np.where(dist < REL_MAX_EXACT, dist, large)


def _silu(x):
    return x * jax.nn.sigmoid(x)


def _even_in_kernel(x_ref, g_ref, w_ref, qa_ref, ka_ref, va_ref, ga_ref, qi_ref, ki_ref, wi_ref,
                    qb_ref, kb_ref, vb_ref, gb_ref, lr_ref):
    x = x_ref[...]
    h = (x * lax.rsqrt(jnp.mean(x * x, axis=-1, keepdims=True) + EPS) * g_ref[...]).astype(BF16)

    def proj(lo, hi):
        return jnp.dot(h, w_ref[:, lo:hi], preferred_element_type=F32)

    qa_ref[...] = proj(C_QA, C_KA).astype(BF16)
    ka_ref[...] = proj(C_KA, C_VA).astype(BF16)
    va_ref[...] = proj(C_VA, C_GA).astype(BF16)
    ga_ref[...] = proj(C_GA, C_QI)
    qi_ref[...] = proj(C_QI, C_KI).astype(BF16)
    ki_ref[...] = proj(C_KI, C_WI).astype(BF16)
    wi_ref[...] = proj(C_WI, C_QB)
    qb_ref[...] = proj(C_QB, C_KB)
    kb_ref[...] = proj(C_KB, C_VB)
    vb_ref[...] = proj(C_VB, C_GB)
    gb_ref[...] = proj(C_GB, C_LR)
    lr_ref[...] = proj(C_LR, C_END)


def _even_in_weights(w):
    cols = np.cumsum([0, A_W, A_W, A_W, A_W, IDX_HEADS * IDX_DIM, IDX_DIM, IDX_HEADS,
                      B_KW, B_KW, B_VW, B_VW, GLA_RANK])
    (s_qa, s_ka, s_va, s_ga, s_qi, s_ki, s_wi, s_qb, s_kb, s_vb, s_gb, s_lr, _) = [int(c) for c in cols]
    out = jnp.zeros((D_MODEL, C_END), F32)
    a_scale = A_HEAD_DIM ** -0.5
    idx_scale = (IDX_DIM ** -0.5) * (IDX_HEADS ** -0.5)
    b_scale = B_DK ** -0.5
    for h in range(A_HEADS):
        dst = C_QA + h * LANES + (h % 2) * A_HEAD_DIM
        out = out.at[:, dst:dst + A_HEAD_DIM].set(w[:, s_qa + h * A_HEAD_DIM:s_qa + (h + 1) * A_HEAD_DIM] * a_scale)
    out = out.at[:, C_KA:C_KA + A_W].set(w[:, s_ka:s_ka + A_W])
    out = out.at[:, C_VA:C_VA + A_W].set(w[:, s_va:s_va + A_W])
    out = out.at[:, C_GA:C_GA + A_W].set(w[:, s_ga:s_ga + A_W])
    for h in range(IDX_HEADS):
        dst = C_QI + h * LANES
        out = out.at[:, dst:dst + IDX_DIM].set(w[:, s_qi + h * IDX_DIM:s_qi + (h + 1) * IDX_DIM])
    out = out.at[:, C_KI:C_KI + IDX_DIM].set(w[:, s_ki:s_ki + IDX_DIM])
    out = out.at[:, C_WI:C_WI + IDX_HEADS].set(w[:, s_wi:s_wi + IDX_HEADS] * idx_scale)
    for h in range(B_HEADS):
        dq = C_QB + h * LANES
        dk = C_KB + h * LANES
        out = out.at[:, dq:dq + B_DK].set(w[:, s_qb + h * B_DK:s_qb + (h + 1) * B_DK] * b_scale)
        out = out.at[:, dk:dk + B_DK].set(w[:, s_kb + h * B_DK:s_kb + (h + 1) * B_DK])
    out = out.at[:, C_VB:C_VB + B_VW].set(w[:, s_vb:s_vb + B_VW])
    out = out.at[:, C_GB:C_GB + B_VW].set(w[:, s_gb:s_gb + B_VW])
    out = out.at[:, C_LR:C_LR + GLA_RANK].set(w[:, s_lr:s_lr + GLA_RANK])
    return out.astype(BF16)


def _even_in(x2d, g, w_pad, tm):
    s = x2d.shape[0]
    widths = [(C_KA - C_QA, BF16), (A_W, BF16), (A_W, BF16), (A_W, F32), (C_KI - C_QI, BF16), (LANES, BF16),
              (LANES, F32), (C_KB - C_QB, F32), (C_VB - C_KB, F32), (B_VW, F32), (B_VW, F32), (LANES, F32)]
    return pl.pallas_call(
        _even_in_kernel,
        grid=(s // tm,),
        in_specs=[pl.BlockSpec((tm, D_MODEL), lambda i: (i, 0)),
                  pl.BlockSpec((1, D_MODEL), lambda i: (0, 0)),
                  pl.BlockSpec((D_MODEL, C_END), lambda i: (0, 0))],
        out_specs=[pl.BlockSpec((tm, wd), lambda i: (i, 0)) for wd, _ in widths],
        out_shape=[jax.ShapeDtypeStruct((s, wd), dt) for wd, dt in widths],
        compiler_params=pltpu.CompilerParams(dimension_semantics=("arbitrary",), vmem_limit_bytes=VMEM_LIMIT),
        name="even_in",
    )(x2d, g, w_pad)


DSA_Q = 128
DSA_KB = 128
DSA_WIDE = 4


def _f32_key(x):
    bits = lax.bitcast_convert_type(x, jnp.int32)
    return bits ^ (lax.shift_right_arithmetic(bits, 31) & 0x7FFFFFFF)


def _dsa_kernel(qa_ref, qi_ref, wi_ref, ga_ref, ka_ref, va_ref, ki_ref, btab_ref, tri_ref, o_ref,
                slab_ref, acc_ref, m_ref, l_ref, *, top_k):
    q, kb, wide = DSA_Q, DSA_KB, DSA_WIDE * DSA_KB
    i = pl.program_id(0)
    nkb = i + 1
    nwide = jnp.maximum(i - 1, 0) // DSA_WIDE
    n0 = nwide * DSA_WIDE
    t_col = i * q + lax.broadcasted_iota(jnp.int32, (q, 1), 0)
    lane = lax.broadcasted_iota(jnp.int32, (1, kb), 1)
    dn_t = (((1,), (1,)), ((), ()))
    npair = A_HEADS // 2

    def wide_cols(jw):
        return pl.ds(pl.multiple_of(jw * wide, wide), wide)

    def narrow_cols(j):
        return pl.ds(pl.multiple_of(j * kb, kb), kb)

    qi_stack = jnp.concatenate([qi_ref[:, h * LANES:(h + 1) * LANES] for h in range(IDX_HEADS)], axis=0)
    w = wi_ref[...]
    w_b = [jnp.broadcast_to(w[:, h:h + 1], (q, kb)) for h in range(IDX_HEADS)]

    def scores(cols, width):
        s = lax.dot_general(qi_stack, ki_ref[cols, :], dn_t, preferred_element_type=F32)
        out = []
        for g in range(width // kb):
            sc = jnp.zeros((q, kb), F32)
            for h in range(IDX_HEADS):
                sc = sc + w_b[h] * jnp.maximum(s[h * q:(h + 1) * q, g * kb:(g + 1) * kb], 0.0)
            out.append(sc)
        return out

    def score_wide(jw, c):
        cols = wide_cols(jw)
        slab_ref[:, cols] = _f32_key(jnp.concatenate(scores(cols, wide), axis=-1))
        return c

    def score_narrow(j, c):
        cols = narrow_cols(j)
        sc = jnp.where(j * kb + lane <= t_col, scores(cols, kb)[0], NEG_INF)
        slab_ref[:, cols] = _f32_key(sc)
        return c

    lax.fori_loop(0, nwide, score_wide, 0)
    lax.fori_loop(n0, nkb, score_narrow, 0)

    def count_ge(trial):
        tb = jnp.broadcast_to(trial, (q, kb))

        def body_wide(jw, acc):
            blk = slab_ref[:, wide_cols(jw)]
            for g in range(DSA_WIDE):
                acc = acc + jnp.where(blk[:, g * kb:(g + 1) * kb] >= tb, 1.0, 0.0)
            return acc

        def body_narrow(j, acc):
            return acc + jnp.where(slab_ref[:, narrow_cols(j)] >= tb, 1.0, 0.0)

        acc = lax.fori_loop(0, nwide, body_wide, jnp.zeros((q, kb), F32))
        acc = lax.fori_loop(n0, nkb, body_narrow, acc)
        return jnp.sum(acc, axis=-1, keepdims=True).astype(jnp.int32)

    def bisect(b, c):
        trial = c + lax.shift_left(jnp.int32(1), 31 - b)
        return jnp.where(count_ge(trial) >= top_k, trial, c)

    thr = lax.fori_loop(0, 32, bisect, jnp.full((q, 1), INT_MIN, jnp.int32))
    cnt_ge = count_ge(thr)
    cnt_gt = count_ge(thr + 1)
    need = top_k - cnt_gt
    tied = jnp.logical_and(cnt_ge > top_k, thr > NEG_INF_KEY)
    any_tie = jnp.max(jnp.where(tied, 1, 0)) > 0

    neg_inf_bits = jnp.int32(int(np.array(-np.inf, np.float32).view(np.int32)))
    thr_valid = jnp.maximum(thr, NEG_INF_KEY + 1)

    def mask_fast():
        def body_wide(jw, c):
            cols = wide_cols(jw)
            slab_ref[:, cols] = jnp.where(slab_ref[:, cols] >= thr_valid, 0, neg_inf_bits)
            return c

        def body_narrow(j, c):
            cols = narrow_cols(j)
            slab_ref[:, cols] = jnp.where(slab_ref[:, cols] >= thr_valid, 0, neg_inf_bits)
            return c

        lax.fori_loop(0, nwide, body_wide, 0)
        lax.fori_loop(n0, nkb, body_narrow, 0)

    def mask_tied():
        def body(j, seen):
            cols = narrow_cols(j)
            key = slab_ref[:, cols]
            eq = key == thr
            eq_f = jnp.where(eq, 1.0, 0.0)
            before = jnp.dot(eq_f.astype(BF16), tri_ref[...], preferred_element_type=F32)
            rank = seen + before
            sel = jnp.logical_or(key > thr, jnp.logical_and(eq, rank < need.astype(F32)))
            sel = jnp.logical_and(sel, key > NEG_INF_KEY)
            slab_ref[:, cols] = jnp.where(sel, 0, neg_inf_bits)
            return seen + jnp.sum(eq_f, axis=-1, keepdims=True)
        lax.fori_loop(0, nkb, body, jnp.zeros((q, 1), F32))

    lax.cond(any_tie, mask_tied, mask_fast)

    m_ref[...] = jnp.full(m_ref.shape, M_INIT, F32)
    l_ref[...] = jnp.zeros(l_ref.shape, F32)
    acc_ref[...] = jnp.zeros(acc_ref.shape, F32)

    def attend(p, rows, width, bias):
        ls = slice(p * LANES, (p + 1) * LANES)
        q_pair = jnp.concatenate([qa_ref[:, (2 * p) * LANES:(2 * p + 1) * LANES],
                                  qa_ref[:, (2 * p + 1) * LANES:(2 * p + 2) * LANES]], axis=0)
        s = lax.dot_general(q_pair, ka_ref[rows, ls], dn_t, preferred_element_type=F32) + bias
        m_old = m_ref[p][:, 0:1]
        m_new = jnp.maximum(m_old, jnp.max(s, axis=-1, keepdims=True))
        alpha = jnp.exp(m_old - m_new)
        pexp = jnp.exp(s - m_new)
        psum = pexp[:, 0:kb]
        for g in range(1, width // kb):
            psum = psum + pexp[:, g * kb:(g + 1) * kb]
        m_ref[p] = jnp.broadcast_to(m_new, (2 * q, LANES))
        l_ref[p] = alpha * l_ref[p] + psum
        acc_ref[p] = alpha * acc_ref[p] + jnp.dot(pexp.astype(BF16), va_ref[rows, ls], preferred_element_type=F32)

    def attn_wide(jw, c):
        mb = lax.bitcast_convert_type(slab_ref[:, wide_cols(jw)], F32)
        mb2 = jnp.concatenate([mb, mb], axis=0)
        for p in range(npair):
            attend(p, wide_cols(jw), wide, mb2)
        return c

    def attn_narrow(j, c):
        mb = lax.bitcast_convert_type(slab_ref[:, narrow_cols(j)], F32)
        mb2 = jnp.concatenate([mb, mb], axis=0)
        rel = jnp.minimum(i - j, 2)
        for p in range(npair):
            attend(p, narrow_cols(j), kb, mb2 + btab_ref[rel, p])
        return c

    lax.fori_loop(0, nwide, attn_wide, 0)
    lax.fori_loop(n0, nkb, attn_narrow, 0)

    for p in range(npair):
        o2 = acc_ref[p] / jnp.sum(l_ref[p], axis=-1, keepdims=True)
        o_pair = jnp.where(lane < A_HEAD_DIM, o2[:q], o2[q:])
        o_ref[:, p * LANES:(p + 1) * LANES] = (o_pair * _silu(ga_ref[:, p * LANES:(p + 1) * LANES])).astype(o_ref.dtype)


def _dsa_bias_table(rel_bias):
    ii = np.arange(DSA_Q)[:, None]
    jj = np.arange(DSA_KB)[None, :]
    buckets = np.stack([_t5_bucket_np(ii - jj), _t5_bucket_np(DSA_KB + ii - jj),
                        np.full((DSA_Q, DSA_KB), REL_BUCKETS - 1)])
    assert (_t5_bucket_np(np.arange(DSA_KB + 1, 4 * DSA_KB)) == REL_BUCKETS - 1).all()
    rb = rel_bias.astype(F32)
    tab = jnp.transpose(rb[buckets], (0, 3, 1, 2))
    tab = tab - rb[REL_BUCKETS - 1][None, :, None, None]
    return tab.reshape(3, A_HEADS // 2, 2 * DSA_Q, DSA_KB)


def _dsa(qa, qi, wi, ga, ka, va, ki, btab, top_k):
    s = qa.shape[0]
    q, kb = DSA_Q, DSA_KB
    tri = jnp.asarray(np.triu(np.ones((kb, kb), np.float32), 1), BF16)
    whole = lambda shape: pl.BlockSpec(shape, lambda i: (0,) * len(shape), pipeline_mode=pl.Buffered(1))
    return pl.pallas_call(
        functools.partial(_dsa_kernel, top_k=top_k),
        grid=(s // q,),
        in_specs=[pl.BlockSpec((q, A_HEADS * LANES), lambda i: (i, 0)),
                  pl.BlockSpec((q, IDX_HEADS * LANES), lambda i: (i, 0)),
                  pl.BlockSpec((q, LANES), lambda i: (i, 0)),
                  pl.BlockSpec((q, A_W), lambda i: (i, 0)),
                  whole((s, A_W)), whole((s, A_W)), whole((s, LANES)),
                  whole((3, A_HEADS // 2, 2 * q, kb)), whole((kb, kb))],
        out_specs=pl.BlockSpec((q, A_W), lambda i: (i, 0)),
        out_shape=jax.ShapeDtypeStruct((s, A_W), BF16),
        scratch_shapes=[pltpu.VMEM((q, s), jnp.int32),
                        pltpu.VMEM((A_HEADS // 2, 2 * q, LANES), F32),
                        pltpu.VMEM((A_HEADS // 2, 2 * q, LANES), F32),
                        pltpu.VMEM((A_HEADS // 2, 2 * q, LANES), F32)],
        compiler_params=pltpu.CompilerParams(dimension_semantics=("arbitrary",), vmem_limit_bytes=VMEM_LIMIT),
        name="dsa",
    )(qa, qi, wi, ga, ka, va, ki, btab, tri)


GLA_BLOCK = 512


def _gla_kernel(qb_ref, kb_ref, vb_ref, gb_ref, lr_ref, wlr_ref, blr_ref, gn_ref, tril_ref, o_ref,
                state_ref, b_ref, att_ref):
    c, sub = GLA_CHUNK, GLA_SUB
    nsub = c // sub

    @pl.when(pl.program_id(0) == 0)
    def _():
        state_ref[...] = jnp.zeros(state_ref.shape, F32)

    dn_t = (((1,), (1,)), ((), ()))
    dn_0 = (((0,), (0,)), ((), ()))
    hp = lax.Precision.HIGHEST
    row_i = lax.broadcasted_iota(jnp.int32, (sub, sub), 0)
    col_j = lax.broadcasted_iota(jnp.int32, (sub, sub), 1)

    def chunk(ci, carry):
        r0 = pl.multiple_of(ci * c, c)
        z = jnp.dot(lr_ref[pl.ds(r0, c), :], wlr_ref[...], precision=hp, preferred_element_type=F32) + blr_ref[...]
        log_a = (jnp.minimum(z, 0.0) - jnp.log1p(jnp.exp(-jnp.abs(z)))) * (1.0 / GLA_TAU)
        b_ref[...] = jnp.dot(tril_ref[...], log_a, precision=hp, preferred_element_type=F32)
        for h in range(B_HEADS):
            ls = slice(h * LANES, (h + 1) * LANES)
            b = b_ref[:, ls]
            qh = qb_ref[pl.ds(r0, c), ls]
            kh = kb_ref[pl.ds(r0, c), ls]
            vh = vb_ref[pl.ds(r0, c), ls]
            state_t = state_ref[h]
            b_last = b[c - 1:c, :]
            o = lax.dot_general((qh * jnp.exp(b)).astype(BF16), state_t.astype(BF16), dn_t,
                                preferred_element_type=F32)
            for si in range(nsub):
                rs = slice(si * sub, (si + 1) * sub)
                b_i, q_i = b[rs], qh[rs]
                if si > 0:
                    ref_row = b[si * sub:si * sub + 1, :]
                    q_t = (q_i * jnp.exp(b_i - ref_row)).astype(BF16)
                    k_t = (kh[:si * sub] * jnp.exp(ref_row - b[:si * sub])).astype(BF16)
                    att_ref[rs, :si * sub] = lax.dot_general(q_t, k_t, dn_t, preferred_element_type=F32)
                k_i = kh[rs]
                diag = jnp.zeros((sub, sub), F32)
                for jl in range(sub):
                    diff = jnp.minimum(b_i - b_i[jl:jl + 1, :], 0.0)
                    col = jnp.sum(q_i * k_i[jl:jl + 1, :] * jnp.exp(diff), axis=-1, keepdims=True)
                    diag = jnp.where(col_j == jl, col, diag)
                att_ref[rs, rs] = jnp.where(col_j <= row_i, diag, 0.0)
                if si + 1 < nsub:
                    att_ref[rs, (si + 1) * sub:] = jnp.zeros((sub, c - (si + 1) * sub), F32)
            o = o + jnp.dot(att_ref[...].astype(BF16), vh.astype(BF16), preferred_element_type=F32)
            k_dec = (kh * jnp.exp(b_last - b)).astype(BF16)
            state_ref[h] = (jnp.exp(b_last) * state_t
                            + lax.dot_general(vh.astype(BF16), k_dec, dn_0, preferred_element_type=F32))
            o = o * lax.rsqrt(jnp.mean(o * o, axis=-1, keepdims=True) + EPS) * gn_ref[...]
            o_ref[pl.ds(r0, c), ls] = (o * _silu(gb_ref[pl.ds(r0, c), ls])).astype(o_ref.dtype)
        return carry

    lax.fori_loop(0, GLA_BLOCK // c, chunk, 0)


def _gla(qb, kb, vb, gb, lr, wlr_pad, blr_pad, gn):
    s = qb.shape[0]
    blk = min(GLA_BLOCK, s)
    assert blk == GLA_BLOCK and s % blk == 0
    tril = jnp.asarray(np.tril(np.ones((GLA_CHUNK, GLA_CHUNK), np.float32)))
    row = lambda wd: pl.BlockSpec((blk, wd), lambda i: (i, 0))
    const = lambda shape: pl.BlockSpec(shape, lambda i: (0,) * len(shape))
    return pl.pallas_call(
        _gla_kernel,
        grid=(s // blk,),
        in_specs=[row(B_HEADS * LANES), row(B_HEADS * LANES), row(B_VW), row(B_VW), row(LANES),
                  const((LANES, B_HEADS * LANES)), const((1, B_HEADS * LANES)), const((1, B_DV)),
                  const((GLA_CHUNK, GLA_CHUNK))],
        out_specs=row(B_VW),
        out_shape=jax.ShapeDtypeStruct((s, B_VW), BF16),
        scratch_shapes=[pltpu.VMEM((B_HEADS, LANES, B_DV), F32),
                        pltpu.VMEM((GLA_CHUNK, B_HEADS * LANES), F32),
                        pltpu.VMEM((GLA_CHUNK, GLA_CHUNK), F32)],
        compiler_params=pltpu.CompilerParams(dimension_semantics=("arbitrary",), vmem_limit_bytes=VMEM_LIMIT),
        name="gla",
    )(qb, kb, vb, gb, lr, wlr_pad, blr_pad, gn, tril)


def _mid_kernel(x_ref, ma_ref, mb_ref, wo_ref, g_ref, wi_ref, x1_ref, u_ref, sg_ref):
    x1 = (x_ref[...]
          + jnp.dot(ma_ref[...], wo_ref[:A_W, :], preferred_element_type=F32)
          + jnp.dot(mb_ref[...], wo_ref[A_W:, :], preferred_element_type=F32))
    x1_ref[...] = x1
    h = (x1 * lax.rsqrt(jnp.mean(x1 * x1, axis=-1, keepdims=True) + EPS) * g_ref[...]).astype(BF16)
    val = jnp.dot(h, wi_ref[:, :D_MODEL], preferred_element_type=F32)
    glu = jnp.dot(h, wi_ref[:, D_MODEL:2 * D_MODEL], preferred_element_type=F32)
    gate = jnp.dot(h, wi_ref[:, 2 * D_MODEL:], preferred_element_type=F32)
    u_ref[...] = val * jax.nn.sigmoid(glu)
    sg_ref[...] = _silu(gate)


def _mid(x2d, mix_a, mix_b, w_out, g, w_in, tm):
    s = x2d.shape[0]
    row = lambda wd: pl.BlockSpec((tm, wd), lambda i: (i, 0))
    const = lambda shape: pl.BlockSpec(shape, lambda i: (0,) * len(shape))
    return pl.pallas_call(
        _mid_kernel,
        grid=(s // tm,),
        in_specs=[row(D_MODEL), row(A_W), row(B_VW), const((A_W + B_VW, D_MODEL)), const((1, D_MODEL)),
                  const((D_MODEL, 3 * D_MODEL))],
        out_specs=[row(D_MODEL), row(D_MODEL), row(D_MODEL)],
        out_shape=[jax.ShapeDtypeStruct((s, D_MODEL), F32)] * 3,
        compiler_params=pltpu.CompilerParams(dimension_semantics=("arbitrary",), vmem_limit_bytes=VMEM_LIMIT),
        name="mid",
    )(x2d, mix_a, mix_b, w_out, g, w_in)


HALO = 32


def _odd_out_kernel(u_ref, halo_ref, sg_ref, x1_ref, cw_ref, cb_ref, lg_ref, lb_ref, wo_ref, nf_ref, o_ref,
                    ubuf_ref, y_ref, *, tm):
    i = pl.program_id(0)
    ubuf_ref[HALO:, :] = u_ref[...]
    ubuf_ref[:HALO, :] = jnp.where(i > 0, halo_ref[...], 0.0)
    base = HALO - (CONV_WIDTH - 1)
    for cs in range(D_MODEL // LANES):
        ls = slice(cs * LANES, (cs + 1) * LANES)
        acc = jnp.zeros((tm, LANES), F32)
        for k in range(CONV_WIDTH):
            acc = acc + ubuf_ref[base + k:base + k + tm, ls] * cw_ref[k:k + 1, ls]
        y_ref[:, ls] = acc + cb_ref[:, ls]
    y = y_ref[...]
    mu = jnp.mean(y, axis=-1, keepdims=True)
    yc = y - mu
    var = jnp.mean(yc * yc, axis=-1, keepdims=True)
    yn = yc * lax.rsqrt(var + EPS) * lg_ref[...] + lb_ref[...]
    mixed = (_silu(yn) * sg_ref[...]).astype(BF16)
    x2 = x1_ref[...] + jnp.dot(mixed, wo_ref[...], preferred_element_type=F32)
    o_ref[...] = x2 * lax.rsqrt(jnp.mean(x2 * x2, axis=-1, keepdims=True) + EPS) * nf_ref[...]


def _odd_out(u, sg, x1, cw_pad, cb, lg, lb, w_out, nf, tm):
    s = u.shape[0]
    row = pl.BlockSpec((tm, D_MODEL), lambda i: (i, 0))
    const = lambda shape: pl.BlockSpec(shape, lambda i: (0,) * len(shape))
    halo = pl.BlockSpec((HALO, D_MODEL), lambda i: (jnp.maximum(i * (tm // HALO) - 1, 0), 0))
    return pl.pallas_call(
        functools.partial(_odd_out_kernel, tm=tm),
        grid=(s // tm,),
        in_specs=[row, halo, row, row, const((HALO, D_MODEL)), const((1, D_MODEL)), const((1, D_MODEL)),
                  const((1, D_MODEL)), const((D_MODEL, D_MODEL)), const((1, D_MODEL))],
        out_specs=row,
        out_shape=jax.ShapeDtypeStruct((s, D_MODEL), F32),
        scratch_shapes=[pltpu.VMEM((HALO + tm, D_MODEL), F32), pltpu.VMEM((tm, D_MODEL), F32)],
        compiler_params=pltpu.CompilerParams(dimension_semantics=("arbitrary",), vmem_limit_bytes=VMEM_LIMIT),
        name="odd_out",
    )(u, u, sg, x1, cw_pad, cb, lg, lb, w_out, nf)


def kernel(x, rel_bias, norm_even, w_in_even, w_gla_lr, b_gla_lr, gla_norm, w_out_even, norm_odd, w_in_odd,
           conv_w, conv_b, conv_ln_g, conv_ln_b, w_out_odd, norm_f):
    bsz, s, d = x.shape
    assert bsz == 1 and d == D_MODEL and s % GLA_BLOCK == 0
    assert norm_even.shape[0] == 1 and norm_odd.shape[0] == 1
    top_k = min(TOPK_MAX, s // 4)
    tm = 256
    x2d = x.reshape(s, d)

    w_pad = _even_in_weights(w_in_even[0])
    qa, ka, va, ga, qi, ki, wi, qb, kb, vb, gb, lr = _even_in(x2d, norm_even[0].reshape(1, d), w_pad, tm)
    mix_a = _dsa(qa, qi, wi, ga, ka, va, ki, _dsa_bias_table(rel_bias), top_k)

    wlr_pad = jnp.zeros((LANES, B_HEADS * LANES), F32)
    blr_pad = jnp.zeros((1, B_HEADS * LANES), F32)
    for h in range(B_HEADS):
        wlr_pad = wlr_pad.at[:GLA_RANK, h * LANES:h * LANES + B_DK].set(w_gla_lr[0][:, h * B_DK:(h + 1) * B_DK])
        blr_pad = blr_pad.at[0, h * LANES:h * LANES + B_DK].set(b_gla_lr[0][h * B_DK:(h + 1) * B_DK])
    mix_b = _gla(qb, kb, vb, gb, lr, wlr_pad, blr_pad, gla_norm[0].reshape(1, B_DV))

    x1, u, sg = _mid(x2d, mix_a, mix_b, w_out_even[0].astype(BF16), norm_odd[0].reshape(1, d),
                     w_in_odd[0].astype(BF16), tm)

    cw_pad = jnp.zeros((HALO, d), F32).at[:CONV_WIDTH].set(conv_w[0])
    out = _odd_out(u, sg, x1, cw_pad, conv_b[0].reshape(1, d), conv_ln_g[0].reshape(1, d),
                   conv_ln_b[0].reshape(1, d), w_out_odd[0].astype(BF16), norm_f.reshape(1, d), tm)
    return out.reshape(bsz, s, d)
```

```python
import functools
import math

import jax
import jax.numpy as jnp
import numpy as np
from jax import lax
from jax.experimental import pallas as pl
from jax.experimental.pallas import tpu as pltpu

D_MODEL = 1024
A_HEADS = 8
A_HEAD_DIM = 64
IDX_HEADS = 4
IDX_DIM = 64
TOPK_MAX = 256
B_HEADS = 4
B_DK = 64
B_DV = 128
GLA_RANK = 16
GLA_TAU = 16.0
GLA_CHUNK = 64
GLA_SUB = 16
CONV_WIDTH = 31
REL_BUCKETS = 32
REL_MAX_EXACT = 16
REL_MAX_DIST = 128
EPS = 1e-6

A_W = A_HEADS * A_HEAD_DIM
B_KW = B_HEADS * B_DK
B_VW = B_HEADS * B_DV

LANES = 128
SUBLANES = 8
VMEM_LIMIT = 56 * 1024 * 1024

F32 = jnp.float32
BF16 = jnp.bfloat16
NEG_INF = float("-inf")
INT_MIN = -(2 ** 31)
NEG_INF_BITS = int(np.array(-np.inf, np.float32).view(np.int32))
NEG_INF_KEY = NEG_INF_BITS ^ 0x7FFFFFFF
M_INIT = -1e30
LOG2E = math.log2(math.e)

C_QA = 0
C_KA = C_QA + A_HEADS * LANES
C_VA = C_KA + A_W
C_GA = C_VA + A_W
C_QI = C_GA + A_W
C_KI = C_QI + IDX_HEADS * LANES
C_WI = C_KI + LANES
C_QB = C_WI + LANES
C_KB = C_QB + B_HEADS * LANES
C_VB = C_KB + B_HEADS * LANES
C_GB = C_VB + B_VW
C_LR = C_GB + B_VW
C_END = C_LR + LANES


def _t5_bucket_np(dist):
    dist = np.maximum(dist, 0)
    d = np.maximum(dist, 1).astype(np.float32)
    large = REL_MAX_EXACT + (np.log(d / REL_MAX_EXACT) / math.log(REL_MAX_DIST / REL_MAX_EXACT)
                             * (REL_BUCKETS - REL_MAX_EXACT)).astype(np.int32)
    large = np.minimum(large, REL_BUCKETS - 1)
    return np.where(dist < REL_MAX_EXACT, dist, large)


def _silu(x):
    return x * jax.nn.sigmoid(x)


def _even_in_kernel(x_ref, g_ref, w_ref, qa_ref, ka_ref, va_ref, ga_ref, qi_ref, ki_ref, wi_ref,
                    qb_ref, kb_ref, vb_ref, gb_ref, lr_ref):
    x = x_ref[...]
    h = (x * lax.rsqrt(jnp.mean(x * x, axis=-1, keepdims=True) + EPS) * g_ref[...]).astype(BF16)

    def proj(lo, hi):
        return jnp.dot(h, w_ref[:, lo:hi], preferred_element_type=F32)

    qa_ref[...] = proj(C_QA, C_KA).astype(BF16)
    ka_ref[...] = proj(C_KA, C_VA).astype(BF16)
    va_ref[...] = proj(C_VA, C_GA).astype(BF16)
    ga_ref[...] = proj(C_GA, C_QI)
    qi_ref[...] = proj(C_QI, C_KI).astype(BF16)
    ki_ref[...] = proj(C_KI, C_WI).astype(BF16)
    wi_ref[...] = proj(C_WI, C_QB)
    qb_ref[...] = proj(C_QB, C_KB)
    kb_ref[...] = proj(C_KB, C_VB)
    vb_ref[...] = proj(C_VB, C_GB)
    gb_ref[...] = proj(C_GB, C_LR)
    lr_ref[...] = proj(C_LR, C_END)


def _even_in_weights(w):
    cols = np.cumsum([0, A_W, A_W, A_W, A_W, IDX_HEADS * IDX_DIM, IDX_DIM, IDX_HEADS,
                      B_KW, B_KW, B_VW, B_VW, GLA_RANK])
    (s_qa, s_ka, s_va, s_ga, s_qi, s_ki, s_wi, s_qb, s_kb, s_vb, s_gb, s_lr, _) = [int(c) for c in cols]
    out = jnp.zeros((D_MODEL, C_END), F32)
    a_scale = (A_HEAD_DIM ** -0.5) * LOG2E
    idx_scale = (IDX_DIM ** -0.5) * (IDX_HEADS ** -0.5)
    b_scale = B_DK ** -0.5
    for h in range(A_HEADS):
        dst = C_QA + h * LANES + (h % 2) * A_HEAD_DIM
        out = out.at[:, dst:dst + A_HEAD_DIM].set(w[:, s_qa + h * A_HEAD_DIM:s_qa + (h + 1) * A_HEAD_DIM] * a_scale)
    out = out.at[:, C_KA:C_KA + A_W].set(w[:, s_ka:s_ka + A_W])
    out = out.at[:, C_VA:C_VA + A_W].set(w[:, s_va:s_va + A_W])
    out = out.at[:, C_GA:C_GA + A_W].set(w[:, s_ga:s_ga + A_W])
    for h in range(IDX_HEADS):
        dst = C_QI + h * LANES
        out = out.at[:, dst:dst + IDX_DIM].set(w[:, s_qi + h * IDX_DIM:s_qi + (h + 1) * IDX_DIM])
    out = out.at[:, C_KI:C_KI + IDX_DIM].set(w[:, s_ki:s_ki + IDX_DIM])
    out = out.at[:, C_WI:C_WI + IDX_HEADS].set(w[:, s_wi:s_wi + IDX_HEADS] * idx_scale)
    for h in range(B_HEADS):
        dq = C_QB + h * LANES
        dk = C_KB + h * LANES
        out = out.at[:, dq:dq + B_DK].set(w[:, s_qb + h * B_DK:s_qb + (h + 1) * B_DK] * b_scale)
        out = out.at[:, dk:dk + B_DK].set(w[:, s_kb + h * B_DK:s_kb + (h + 1) * B_DK])
    out = out.at[:, C_VB:C_VB + B_VW].set(w[:, s_vb:s_vb + B_VW])
    out = out.at[:, C_GB:C_GB + B_VW].set(w[:, s_gb:s_gb + B_VW])
    out = out.at[:, C_LR:C_LR + GLA_RANK].set(w[:, s_lr:s_lr + GLA_RANK])
    return out.astype(BF16)


def _even_in(x2d, g, w_pad, tm):
    s = x2d.shape[0]
    widths = [(C_KA - C_QA, BF16), (A_W, BF16), (A_W, BF16), (A_W, F32), (C_KI - C_QI, BF16), (LANES, BF16),
              (LANES, F32), (C_KB - C_QB, F32), (C_VB - C_KB, F32), (B_VW, F32), (B_VW, F32), (LANES, F32)]
    return pl.pallas_call(
        _even_in_kernel,
        grid=(s // tm,),
        in_specs=[pl.BlockSpec((tm, D_MODEL), lambda i: (i, 0)),
                  pl.BlockSpec((1, D_MODEL), lambda i: (0, 0)),
                  pl.BlockSpec((D_MODEL, C_END), lambda i: (0, 0))],
        out_specs=[pl.BlockSpec((tm, wd), lambda i: (i, 0)) for wd, _ in widths],
        out_shape=[jax.ShapeDtypeStruct((s, wd), dt) for wd, dt in widths],
        compiler_params=pltpu.CompilerParams(dimension_semantics=("arbitrary",), vmem_limit_bytes=VMEM_LIMIT),
        name="even_in",
    )(x2d, g, w_pad)


DSA_Q = 128
DSA_KB = 128
DSA_WIDE = 4
DSA_NACC = 4


def _f32_key(x):
    bits = lax.bitcast_convert_type(x, jnp.int32)
    return bits ^ (lax.shift_right_arithmetic(bits, 31) & 0x7FFFFFFF)


def _fold_keys(x, op):
    n, w = x.shape
    group = DSA_NACC * SUBLANES
    if n % group == 0 and n > group:
        x = op(x.reshape(n // group, DSA_NACC, SUBLANES, w), axis=0)
        return op(x, axis=0)
    return op(x.reshape(n // SUBLANES, SUBLANES, w), axis=0)


def _dsa_kernel(qat_ref, qit_ref, wit_ref, ga_ref, ka_ref, vat_ref, ki_ref, btab_ref, tri_ref, o_ref,
                slab_ref, acc_ref, m_ref, l_ref, *, top_k):
    q, kb, wide = DSA_Q, DSA_KB, DSA_WIDE * DSA_KB
    i = pl.program_id(0)
    nkb = i + 1
    nwide = jnp.maximum(i - 1, 0) // DSA_WIDE
    n0 = nwide * DSA_WIDE
    t_row = i * q + lax.broadcasted_iota(jnp.int32, (1, q), 1)
    key_off = lax.broadcasted_iota(jnp.int32, (kb, 1), 0)
    npair = A_HEADS // 2

    def wide_rows(jw):
        return pl.ds(pl.multiple_of(jw * wide, wide), wide)

    def narrow_rows(j):
        return pl.ds(pl.multiple_of(j * kb, kb), kb)

    w_idx = jnp.concatenate([qit_ref[h * LANES:(h + 1) * LANES, :] for h in range(IDX_HEADS)], axis=1)
    w_head = [wit_ref[h:h + 1, :] for h in range(IDX_HEADS)]

    def scores(rows):
        s = jnp.dot(ki_ref[rows, :], w_idx, preferred_element_type=F32)
        sc = w_head[0] * jnp.maximum(s[:, 0:q], 0.0)
        for h in range(1, IDX_HEADS):
            sc = sc + w_head[h] * jnp.maximum(s[:, h * q:(h + 1) * q], 0.0)
        return sc

    def score_wide(jw, c):
        rows = wide_rows(jw)
        slab_ref[rows, :] = _f32_key(scores(rows))
        return c

    def score_narrow(j, c):
        rows = narrow_rows(j)
        sc = jnp.where(j * kb + key_off <= t_row, scores(rows), NEG_INF)
        slab_ref[rows, :] = _f32_key(sc)
        return c

    lax.fori_loop(0, nwide, score_wide, 0)
    lax.fori_loop(n0, nkb, score_narrow, 0)

    def count_ge(trial):
        def cnt(blk):
            return _fold_keys(jnp.where(blk >= trial, 1.0, 0.0), jnp.sum)

        acc = lax.fori_loop(0, nwide, lambda jw, a: a + cnt(slab_ref[wide_rows(jw), :]),
                            jnp.zeros((SUBLANES, q), F32))
        acc = lax.fori_loop(n0, nkb, lambda j, a: a + cnt(slab_ref[narrow_rows(j), :]), acc)
        return jnp.sum(acc, axis=0, keepdims=True).astype(jnp.int32)

    def bisect(b, c):
        trial = c + lax.shift_left(jnp.int32(1), 31 - b)
        return jnp.where(count_ge(trial) >= top_k, trial, c)

    thr = lax.fori_loop(0, 32, bisect, jnp.full((1, q), INT_MIN, jnp.int32))
    cnt_ge = count_ge(thr)
    cnt_gt = count_ge(thr + 1)
    need = (top_k - cnt_gt).astype(F32)
    tied = jnp.logical_and(cnt_ge > top_k, thr > NEG_INF_KEY)
    any_tie = jnp.max(jnp.where(tied, 1, 0)) > 0

    thr_valid = jnp.maximum(thr, NEG_INF_KEY + 1)

    def mask_fast():
        def body(rows):
            slab_ref[rows, :] = jnp.where(slab_ref[rows, :] >= thr_valid, 0, NEG_INF_BITS)

        lax.fori_loop(0, nwide, lambda jw, c: body(wide_rows(jw)), None)
        lax.fori_loop(n0, nkb, lambda j, c: body(narrow_rows(j)), None)

    def mask_tied():
        def body(j, seen):
            rows = narrow_rows(j)
            key = slab_ref[rows, :]
            eq = key == thr
            eq_f = jnp.where(eq, 1.0, 0.0)
            before = jnp.dot(tri_ref[...], eq_f.astype(BF16), preferred_element_type=F32)
            sel = jnp.logical_or(key > thr, jnp.logical_and(eq, seen + before < need))
            sel = jnp.logical_and(sel, key > NEG_INF_KEY)
            slab_ref[rows, :] = jnp.where(sel, 0, NEG_INF_BITS)
            return seen + jnp.sum(eq_f, axis=0, keepdims=True)

        lax.fori_loop(0, nkb, body, jnp.zeros((1, q), F32))

    lax.cond(any_tie, mask_tied, mask_fast)

    m_ref[...] = jnp.full(m_ref.shape, M_INIT, F32)
    l_ref[...] = jnp.zeros(l_ref.shape, F32)
    acc_ref[...] = jnp.zeros(acc_ref.shape, F32)

    def attend(p, rows, mask2, bias):
        ls = slice(p * LANES, (p + 1) * LANES)
        w_q = jnp.concatenate([qat_ref[(2 * p) * LANES:(2 * p + 1) * LANES, :],
                               qat_ref[(2 * p + 1) * LANES:(2 * p + 2) * LANES, :]], axis=1)
        s = jnp.dot(ka_ref[rows, ls], w_q, preferred_element_type=F32) + mask2
        if bias is not None:
            s = s + bias
        m_old = m_ref[p]
        m_new = jnp.maximum(m_old, jnp.max(_fold_keys(s, jnp.max), axis=0, keepdims=True))
        alpha = jnp.exp2(m_old - m_new)
        pexp = jnp.exp2(s - m_new[0:1])
        m_ref[p] = m_new
        l_ref[p] = alpha * l_ref[p] + _fold_keys(pexp, jnp.sum)
        acc_ref[p] = alpha[0:1] * acc_ref[p] + jnp.dot(vat_ref[ls, rows], pexp.astype(BF16),
                                                       preferred_element_type=F32)

    def attn_wide(jw, c):
        rows = wide_rows(jw)
        mask = lax.bitcast_convert_type(slab_ref[rows, :], F32)
        mask2 = jnp.concatenate([mask, mask], axis=1)
        for p in range(npair):
            attend(p, rows, mask2, None)
        return c

    def attn_narrow(j, c):
        rows = narrow_rows(j)
        mask = lax.bitcast_convert_type(slab_ref[rows, :], F32)
        mask2 = jnp.concatenate([mask, mask], axis=1)
        rel = jnp.minimum(i - j, 2)
        for p in range(npair):
            attend(p, rows, mask2, btab_ref[rel, p])
        return c

    lax.fori_loop(0, nwide, attn_wide, 0)
    lax.fori_loop(n0, nkb, attn_narrow, 0)

    for p in range(npair):
        o_t = acc_ref[p] / jnp.sum(l_ref[p], axis=0, keepdims=True)
        o_pair = jnp.concatenate([o_t[:A_HEAD_DIM, :q], o_t[A_HEAD_DIM:, q:]], axis=0).T
        ls = slice(p * LANES, (p + 1) * LANES)
        o_ref[:, ls] = (o_pair * _silu(ga_ref[:, ls])).astype(o_ref.dtype)


def _dsa_bias_table(rel_bias):
    ii = np.arange(DSA_Q)[:, None]
    jj = np.arange(DSA_KB)[None, :]
    buckets = np.stack([_t5_bucket_np(ii - jj), _t5_bucket_np(DSA_KB + ii - jj),
                        np.full((DSA_Q, DSA_KB), REL_BUCKETS - 1)])
    assert (_t5_bucket_np(np.arange(DSA_KB + 1, 4 * DSA_KB)) == REL_BUCKETS - 1).all()
    rb = rel_bias.astype(F32) * LOG2E
    tab = jnp.transpose(rb[buckets], (0, 3, 2, 1))
    tab = tab - rb[REL_BUCKETS - 1][None, :, None, None]
    tab = tab.reshape(3, A_HEADS // 2, 2, DSA_KB, DSA_Q)
    return jnp.transpose(tab, (0, 1, 3, 2, 4)).reshape(3, A_HEADS // 2, DSA_KB, 2 * DSA_Q)


def _dsa(qa, qi, wi, ga, ka, va, ki, btab, top_k):
    s = qa.shape[0]
    q, kb = DSA_Q, DSA_KB
    tri = jnp.asarray(np.tril(np.ones((kb, kb), np.float32), -1), BF16)
    qa_t, qi_t, va_t = qa.T, qi.T, va.T
    wi_t = wi[:, :SUBLANES].T
    whole = lambda shape: pl.BlockSpec(shape, lambda i: (0,) * len(shape), pipeline_mode=pl.Buffered(1))
    return pl.pallas_call(
        functools.partial(_dsa_kernel, top_k=top_k),
        grid=(s // q,),
        in_specs=[pl.BlockSpec((A_HEADS * LANES, q), lambda i: (0, i)),
                  pl.BlockSpec((IDX_HEADS * LANES, q), lambda i: (0, i)),
                  pl.BlockSpec((SUBLANES, q), lambda i: (0, i)),
                  pl.BlockSpec((q, A_W), lambda i: (i, 0)),
                  whole((s, A_W)), whole((A_W, s)), whole((s, LANES)),
                  whole((3, A_HEADS // 2, kb, 2 * q)), whole((kb, kb))],
        out_specs=pl.BlockSpec((q, A_W), lambda i: (i, 0)),
        out_shape=jax.ShapeDtypeStruct((s, A_W), BF16),
        scratch_shapes=[pltpu.VMEM((s, q), jnp.int32),
                        pltpu.VMEM((A_HEADS // 2, LANES, 2 * q), F32),
                        pltpu.VMEM((A_HEADS // 2, SUBLANES, 2 * q), F32),
                        pltpu.VMEM((A_HEADS // 2, SUBLANES, 2 * q), F32)],
        compiler_params=pltpu.CompilerParams(dimension_semantics=("arbitrary",), vmem_limit_bytes=VMEM_LIMIT),
        name="dsa",
    )(qa_t, qi_t, wi_t, ga, ka, va_t, ki, btab, tri)


GLA_BLOCK = 512


def _gla_kernel(qb_ref, kb_ref, vb_ref, gb_ref, lr_ref, wlr_ref, blr_ref, gn_ref, tril_ref, o_ref,
                state_ref, b_ref, att_ref):
    c, sub = GLA_CHUNK, GLA_SUB
    nsub = c // sub

    @pl.when(pl.program_id(0) == 0)
    def _():
        state_ref[...] = jnp.zeros(state_ref.shape, F32)

    dn_t = (((1,), (1,)), ((), ()))
    dn_0 = (((0,), (0,)), ((), ()))
    hp = lax.Precision.HIGHEST
    row_i = lax.broadcasted_iota(jnp.int32, (sub, sub), 0)
    col_j = lax.broadcasted_iota(jnp.int32, (sub, sub), 1)

    def chunk(ci, carry):
        r0 = pl.multiple_of(ci * c, c)
        z = jnp.dot(lr_ref[pl.ds(r0, c), :], wlr_ref[...], precision=hp, preferred_element_type=F32) + blr_ref[...]
        log_a = (jnp.minimum(z, 0.0) - jnp.log1p(jnp.exp(-jnp.abs(z)))) * (1.0 / GLA_TAU)
        b_ref[...] = jnp.dot(tril_ref[...], log_a, precision=hp, preferred_element_type=F32)
        for h in range(B_HEADS):
            ls = slice(h * LANES, (h + 1) * LANES)
            b = b_ref[:, ls]
            qh = qb_ref[pl.ds(r0, c), ls]
            kh = kb_ref[pl.ds(r0, c), ls]
            vh = vb_ref[pl.ds(r0, c), ls]
            state_t = state_ref[h]
            b_last = b[c - 1:c, :]
            o = lax.dot_general((qh * jnp.exp(b)).astype(BF16), state_t.astype(BF16), dn_t,
                                preferred_element_type=F32)
            for si in range(nsub):
                rs = slice(si * sub, (si + 1) * sub)
                b_i, q_i = b[rs], qh[rs]
                if si > 0:
                    ref_row = b[si * sub:si * sub + 1, :]
                    q_t = (q_i * jnp.exp(b_i - ref_row)).astype(BF16)
                    k_t = (kh[:si * sub] * jnp.exp(ref_row - b[:si * sub])).astype(BF16)
                    att_ref[rs, :si * sub] = lax.dot_general(q_t, k_t, dn_t, preferred_element_type=F32)
                k_i = kh[rs]
                diag = jnp.zeros((sub, sub), F32)
                for jl in range(sub):
                    diff = jnp.minimum(b_i - b_i[jl:jl + 1, :], 0.0)
                    col = jnp.sum(q_i * k_i[jl:jl + 1, :] * jnp.exp(diff), axis=-1, keepdims=True)
                    diag = jnp.where(col_j == jl, col, diag)
                att_ref[rs, rs] = jnp.where(col_j <= row_i, diag, 0.0)
                if si + 1 < nsub:
                    att_ref[rs, (si + 1) * sub:] = jnp.zeros((sub, c - (si + 1) * sub), F32)
            o = o + jnp.dot(att_ref[...].astype(BF16), vh.astype(BF16), preferred_element_type=F32)
            k_dec = (kh * jnp.exp(b_last - b)).astype(BF16)
            state_ref[h] = (jnp.exp(b_last) * state_t
                            + lax.dot_general(vh.astype(BF16), k_dec, dn_0, preferred_element_type=F32))
            o = o * lax.rsqrt(jnp.mean(o * o, axis=-1, keepdims=True) + EPS) * gn_ref[...]
            o_ref[pl.ds(r0, c), ls] = (o * _silu(gb_ref[pl.ds(r0, c), ls])).astype(o_ref.dtype)
        return carry

    lax.fori_loop(0, GLA_BLOCK // c, chunk, 0)


def _gla(qb, kb, vb, gb, lr, wlr_pad, blr_pad, gn):
    s = qb.shape[0]
    blk = min(GLA_BLOCK, s)
    assert blk == GLA_BLOCK and s % blk == 0
    tril = jnp.asarray(np.tril(np.ones((GLA_CHUNK, GLA_CHUNK), np.float32)))
    row = lambda wd: pl.BlockSpec((blk, wd), lambda i: (i, 0))
    const = lambda shape: pl.BlockSpec(shape, lambda i: (0,) * len(shape))
    return pl.pallas_call(
        _gla_kernel,
        grid=(s // blk,),
        in_specs=[row(B_HEADS * LANES), row(B_HEADS * LANES), row(B_VW), row(B_VW), row(LANES),
                  const((LANES, B_HEADS * LANES)), const((1, B_HEADS * LANES)), const((1, B_DV)),
                  const((GLA_CHUNK, GLA_CHUNK))],
        out_specs=row(B_VW),
        out_shape=jax.ShapeDtypeStruct((s, B_VW), BF16),
        scratch_shapes=[pltpu.VMEM((B_HEADS, LANES, B_DV), F32),
                        pltpu.VMEM((GLA_CHUNK, B_HEADS * LANES), F32),
                        pltpu.VMEM((GLA_CHUNK, GLA_CHUNK), F32)],
        compiler_params=pltpu.CompilerParams(dimension_semantics=("arbitrary",), vmem_limit_bytes=VMEM_LIMIT),
        name="gla",
    )(qb, kb, vb, gb, lr, wlr_pad, blr_pad, gn, tril)


def _mid_kernel(x_ref, ma_ref, mb_ref, wo_ref, g_ref, wi_ref, x1_ref, u_ref, sg_ref):
    x1 = (x_ref[...]
          + jnp.dot(ma_ref[...], wo_ref[:A_W, :], preferred_element_type=F32)
          + jnp.dot(mb_ref[...], wo_ref[A_W:, :], preferred_element_type=F32))
    x1_ref[...] = x1
    h = (x1 * lax.rsqrt(jnp.mean(x1 * x1, axis=-1, keepdims=True) + EPS) * g_ref[...]).astype(BF16)
    val = jnp.dot(h, wi_ref[:, :D_MODEL], preferred_element_type=F32)
    glu = jnp.dot(h, wi_ref[:, D_MODEL:2 * D_MODEL], preferred_element_type=F32)
    gate = jnp.dot(h, wi_ref[:, 2 * D_MODEL:], preferred_element_type=F32)
    u_ref[...] = val * jax.nn.sigmoid(glu)
    sg_ref[...] = _silu(gate)


def _mid(x2d, mix_a, mix_b, w_out, g, w_in, tm):
    s = x2d.shape[0]
    row = lambda wd: pl.BlockSpec((tm, wd), lambda i: (i, 0))
    const = lambda shape: pl.BlockSpec(shape, lambda i: (0,) * len(shape))
    return pl.pallas_call(
        _mid_kernel,
        grid=(s // tm,),
        in_specs=[row(D_MODEL), row(A_W), row(B_VW), const((A_W + B_VW, D_MODEL)), const((1, D_MODEL)),
                  const((D_MODEL, 3 * D_MODEL))],
        out_specs=[row(D_MODEL), row(D_MODEL), row(D_MODEL)],
        out_shape=[jax.ShapeDtypeStruct((s, D_MODEL), F32)] * 3,
        compiler_params=pltpu.CompilerParams(dimension_semantics=("arbitrary",), vmem_limit_bytes=VMEM_LIMIT),
        name="mid",
    )(x2d, mix_a, mix_b, w_out, g, w_in)


HALO = 32


def _odd_out_kernel(u_ref, halo_ref, sg_ref, x1_ref, cw_ref, cb_ref, lg_ref, lb_ref, wo_ref, nf_ref, o_ref,
                    ubuf_ref, y_ref, *, tm):
    i = pl.program_id(0)
    ubuf_ref[HALO:, :] = u_ref[...]
    ubuf_ref[:HALO, :] = jnp.where(i > 0, halo_ref[...], 0.0)
    base = HALO - (CONV_WIDTH - 1)
    for cs in range(D_MODEL // LANES):
        ls = slice(cs * LANES, (cs + 1) * LANES)
        acc = jnp.zeros((tm, LANES), F32)
        for k in range(CONV_WIDTH):
            acc = acc + ubuf_ref[base + k:base + k + tm, ls] * cw_ref[k:k + 1, ls]
        y_ref[:, ls] = acc + cb_ref[:, ls]
    y = y_ref[...]
    mu = jnp.mean(y, axis=-1, keepdims=True)
    yc = y - mu
    var = jnp.mean(yc * yc, axis=-1, keepdims=True)
    yn = yc * lax.rsqrt(var + EPS) * lg_ref[...] + lb_ref[...]
    mixed = (_silu(yn) * sg_ref[...]).astype(BF16)
    x2 = x1_ref[...] + jnp.dot(mixed, wo_ref[...], preferred_element_type=F32)
    o_ref[...] = x2 * lax.rsqrt(jnp.mean(x2 * x2, axis=-1, keepdims=True) + EPS) * nf_ref[...]


def _odd_out(u, sg, x1, cw_pad, cb, lg, lb, w_out, nf, tm):
    s = u.shape[0]
    row = pl.BlockSpec((tm, D_MODEL), lambda i: (i, 0))
    const = lambda shape: pl.BlockSpec(shape, lambda i: (0,) * len(shape))
    halo = pl.BlockSpec((HALO, D_MODEL), lambda i: (jnp.maximum(i * (tm // HALO) - 1, 0), 0))
    return pl.pallas_call(
        functools.partial(_odd_out_kernel, tm=tm),
        grid=(s // tm,),
        in_specs=[row, halo, row, row, const((HALO, D_MODEL)), const((1, D_MODEL)), const((1, D_MODEL)),
                  const((1, D_MODEL)), const((D_MODEL, D_MODEL)), const((1, D_MODEL))],
        out_specs=row,
        out_shape=jax.ShapeDtypeStruct((s, D_MODEL), F32),
        scratch_shapes=[pltpu.VMEM((HALO + tm, D_MODEL), F32), pltpu.VMEM((tm, D_MODEL), F32)],
        compiler_params=pltpu.CompilerParams(dimension_semantics=("arbitrary",), vmem_limit_bytes=VMEM_LIMIT),
        name="odd_out",
    )(u, u, sg, x1, cw_pad, cb, lg, lb, w_out, nf)


def kernel(x, rel_bias, norm_even, w_in_even, w_gla_lr, b_gla_lr, gla_norm, w_out_even, norm_odd, w_in_odd,
           conv_w, conv_b, conv_ln_g, conv_ln_b, w_out_odd, norm_f):
    bsz, s, d = x.shape
    assert bsz == 1 and d == D_MODEL and s % GLA_BLOCK == 0
    assert norm_even.shape[0] == 1 and norm_odd.shape[0] == 1
    top_k = min(TOPK_MAX, s // 4)
    tm = 256
    x2d = x.reshape(s, d)

    w_pad = _even_in_weights(w_in_even[0])
    qa, ka, va, ga, qi, ki, wi, qb, kb, vb, gb, lr = _even_in(x2d, norm_even[0].reshape(1, d), w_pad, tm)
    mix_a = _dsa(qa, qi, wi, ga, ka, va, ki, _dsa_bias_table(rel_bias), top_k)

    wlr_pad = jnp.zeros((LANES, B_HEADS * LANES), F32)
    blr_pad = jnp.zeros((1, B_HEADS * LANES), F32)
    for h in range(B_HEADS):
        wlr_pad = wlr_pad.at[:GLA_RANK, h * LANES:h * LANES + B_DK].set(w_gla_lr[0][:, h * B_DK:(h + 1) * B_DK])
        blr_pad = blr_pad.at[0, h * LANES:h * LANES + B_DK].set(b_gla_lr[0][h * B_DK:(h + 1) * B_DK])
    mix_b = _gla(qb, kb, vb, gb, lr, wlr_pad, blr_pad, gla_norm[0].reshape(1, B_DV))

    x1, u, sg = _mid(x2d, mix_a, mix_b, w_out_even[0].astype(BF16), norm_odd[0].reshape(1, d),
                     w_in_odd[0].astype(BF16), tm)

    cw_pad = jnp.zeros((HALO, d), F32).at[:CONV_WIDTH].set(conv_w[0])
    out = _odd_out(u, sg, x1, cw_pad, conv_b[0].reshape(1, d), conv_ln_g[0].reshape(1, d),
                   conv_ln_b[0].reshape(1, d), w_out_odd[0].astype(BF16), norm_f.reshape(1, d), tm)
    return out.reshape(bsz, s, d)
```

```python
import functools
import math

import jax
import jax.numpy as jnp
import numpy as np
from jax import lax
from jax.experimental import pallas as pl
from jax.experimental.pallas import tpu as pltpu

D_MODEL = 1024
A_HEADS = 8
A_HEAD_DIM = 64
IDX_HEADS = 4
IDX_DIM = 64
TOPK_MAX = 256
B_HEADS = 4
B_DK = 64
B_DV = 128
GLA_RANK = 16
GLA_TAU = 16.0
GLA_CHUNK = 64
GLA_SUB = 16
CONV_WIDTH = 31
REL_BUCKETS = 32
REL_MAX_EXACT = 16
REL_MAX_DIST = 128
EPS = 1e-6

A_W = A_HEADS * A_HEAD_DIM
B_KW = B_HEADS * B_DK
B_VW = B_HEADS * B_DV

LANES = 128
SUBLANES = 8
VMEM_LIMIT = 56 * 1024 * 1024

F32 = jnp.float32
BF16 = jnp.bfloat16
NEG_INF = float("-inf")
INT_MIN = -(2 ** 31)
NEG_INF_BITS = int(np.array(-np.inf, np.float32).view(np.int32))
NEG_INF_KEY = NEG_INF_BITS ^ 0x7FFFFFFF
M_INIT = -1e30
LOG2E = math.log2(math.e)

C_QA = 0
C_KA = C_QA + A_HEADS * LANES
C_VA = C_KA + A_W
C_GA = C_VA + A_W
C_QI = C_GA + A_W
C_KI = C_QI + IDX_HEADS * LANES
C_WI = C_KI + LANES
C_QB = C_WI + LANES
C_KB = C_QB + B_HEADS * LANES
C_VB = C_KB + B_HEADS * LANES
C_GB = C_VB + B_VW
C_LR = C_GB + B_VW
C_END = C_LR + LANES


def _t5_bucket_np(dist):
    dist = np.maximum(dist, 0)
    d = np.maximum(dist, 1).astype(np.float32)
    large = REL_MAX_EXACT + (np.log(d / REL_MAX_EXACT) / math.log(REL_MAX_DIST / REL_MAX_EXACT)
                             * (REL_BUCKETS - REL_MAX_EXACT)).astype(np.int32)
    large = np.minimum(large, REL_BUCKETS - 1)
    return np.where(dist < REL_MAX_EXACT, dist, large)


def _silu(x):
    return x * jax.nn.sigmoid(x)


def _even_in_kernel(x_ref, g_ref, w_ref, qa_ref, ka_ref, va_ref, ga_ref, qi_ref, ki_ref, wi_ref,
                    qb_ref, kb_ref, vb_ref, gb_ref, lr_ref):
    x = x_ref[...]
    h = (x * lax.rsqrt(jnp.mean(x * x, axis=-1, keepdims=True) + EPS) * g_ref[...]).astype(BF16)

    def proj(lo, hi):
        return jnp.dot(h, w_ref[:, lo:hi], preferred_element_type=F32)

    qa_ref[...] = proj(C_QA, C_KA).astype(BF16)
    ka_ref[...] = proj(C_KA, C_VA).astype(BF16)
    va_ref[...] = proj(C_VA, C_GA).astype(BF16)
    ga_ref[...] = proj(C_GA, C_QI)
    qi_ref[...] = proj(C_QI, C_KI).astype(BF16)
    ki_ref[...] = proj(C_KI, C_WI).astype(BF16)
    wi_ref[...] = proj(C_WI, C_QB)
    qb_ref[...] = proj(C_QB, C_KB)
    kb_ref[...] = proj(C_KB, C_VB)
    vb_ref[...] = proj(C_VB, C_GB)
    gb_ref[...] = proj(C_GB, C_LR)
    lr_ref[...] = proj(C_LR, C_END)


def _even_in_weights(w):
    cols = np.cumsum([0, A_W, A_W, A_W, A_W, IDX_HEADS * IDX_DIM, IDX_DIM, IDX_HEADS,
                      B_KW, B_KW, B_VW, B_VW, GLA_RANK])
    (s_qa, s_ka, s_va, s_ga, s_qi, s_ki, s_wi, s_qb, s_kb, s_vb, s_gb, s_lr, _) = [int(c) for c in cols]
    out = jnp.zeros((D_MODEL, C_END), F32)
    a_scale = (A_HEAD_DIM ** -0.5) * LOG2E
    idx_scale = (IDX_DIM ** -0.5) * (IDX_HEADS ** -0.5)
    b_scale = B_DK ** -0.5
    for h in range(A_HEADS):
        dst = C_QA + h * LANES + (h % 2) * A_HEAD_DIM
        out = out.at[:, dst:dst + A_HEAD_DIM].set(w[:, s_qa + h * A_HEAD_DIM:s_qa + (h + 1) * A_HEAD_DIM] * a_scale)
    out = out.at[:, C_KA:C_KA + A_W].set(w[:, s_ka:s_ka + A_W])
    out = out.at[:, C_VA:C_VA + A_W].set(w[:, s_va:s_va + A_W])
    out = out.at[:, C_GA:C_GA + A_W].set(w[:, s_ga:s_ga + A_W])
    for h in range(IDX_HEADS):
        dst = C_QI + h * LANES
        out = out.at[:, dst:dst + IDX_DIM].set(w[:, s_qi + h * IDX_DIM:s_qi + (h + 1) * IDX_DIM])
    out = out.at[:, C_KI:C_KI + IDX_DIM].set(w[:, s_ki:s_ki + IDX_DIM])
    out = out.at[:, C_WI:C_WI + IDX_HEADS].set(w[:, s_wi:s_wi + IDX_HEADS] * idx_scale)
    for h in range(B_HEADS):
        dq = C_QB + h * LANES
        dk = C_KB + h * LANES
        out = out.at[:, dq:dq + B_DK].set(w[:, s_qb + h * B_DK:s_qb + (h + 1) * B_DK] * b_scale)
        out = out.at[:, dk:dk + B_DK].set(w[:, s_kb + h * B_DK:s_kb + (h + 1) * B_DK])
    out = out.at[:, C_VB:C_VB + B_VW].set(w[:, s_vb:s_vb + B_VW])
    out = out.at[:, C_GB:C_GB + B_VW].set(w[:, s_gb:s_gb + B_VW])
    out = out.at[:, C_LR:C_LR + GLA_RANK].set(w[:, s_lr:s_lr + GLA_RANK])
    return out.astype(BF16)


def _even_in(x2d, g, w_pad, tm):
    s = x2d.shape[0]
    widths = [(C_KA - C_QA, BF16), (A_W, BF16), (A_W, BF16), (A_W, F32), (C_KI - C_QI, BF16), (LANES, BF16),
              (LANES, F32), (C_KB - C_QB, F32), (C_VB - C_KB, F32), (B_VW, F32), (B_VW, F32), (LANES, F32)]
    return pl.pallas_call(
        _even_in_kernel,
        grid=(s // tm,),
        in_specs=[pl.BlockSpec((tm, D_MODEL), lambda i: (i, 0)),
                  pl.BlockSpec((1, D_MODEL), lambda i: (0, 0)),
                  pl.BlockSpec((D_MODEL, C_END), lambda i: (0, 0))],
        out_specs=[pl.BlockSpec((tm, wd), lambda i: (i, 0)) for wd, _ in widths],
        out_shape=[jax.ShapeDtypeStruct((s, wd), dt) for wd, dt in widths],
        compiler_params=pltpu.CompilerParams(dimension_semantics=("arbitrary",), vmem_limit_bytes=VMEM_LIMIT),
        name="even_in",
    )(x2d, g, w_pad)


DSA_Q = 128
DSA_KB = 128
DSA_WIDE = 4
DSA_NACC = 4
DSA_CELLS = 32


def _f32_key(x):
    bits = lax.bitcast_convert_type(x, jnp.int32)
    return bits ^ (lax.shift_right_arithmetic(bits, 31) & 0x7FFFFFFF)


def _fold_keys(x, op):
    n, w = x.shape
    group = DSA_NACC * SUBLANES
    if n % group == 0 and n > group:
        x = op(x.reshape(n // group, DSA_NACC, SUBLANES, w), axis=0)
        return op(x, axis=0)
    return op(x.reshape(n // SUBLANES, SUBLANES, w), axis=0)


def _dsa_kernel(qat_ref, qit_ref, wit_ref, ga_ref, ka_ref, vat_ref, ki_ref, btab_ref, tri_ref, o_ref,
                slab_ref, s_ref, acc_ref, m_ref, l_ref, *, top_k):
    q, kb, wide = DSA_Q, DSA_KB, DSA_WIDE * DSA_KB
    i = pl.program_id(0)
    nkb = i + 1
    nwide = jnp.maximum(i - 1, 0) // DSA_WIDE
    n0 = nwide * DSA_WIDE
    t_row = i * q + lax.broadcasted_iota(jnp.int32, (1, q), 1)
    key_off = lax.broadcasted_iota(jnp.int32, (kb, 1), 0)
    npair = A_HEADS // 2

    def wide_rows(jw):
        return pl.ds(pl.multiple_of(jw * wide, wide), wide)

    def narrow_rows(j):
        return pl.ds(pl.multiple_of(j * kb, kb), kb)

    w_idx = jnp.concatenate([qit_ref[h * LANES:(h + 1) * LANES, :] for h in range(IDX_HEADS)], axis=1)
    w_head = [wit_ref[h:h + 1, :] for h in range(IDX_HEADS)]

    def scores(rows):
        s = jnp.dot(ki_ref[rows, :], w_idx, preferred_element_type=F32)
        sc = w_head[0] * jnp.maximum(s[:, 0:q], 0.0)
        for h in range(1, IDX_HEADS):
            sc = sc + w_head[h] * jnp.maximum(s[:, h * q:(h + 1) * q], 0.0)
        return sc

    def score_wide(jw, c):
        rows = wide_rows(jw)
        slab_ref[rows, :] = _f32_key(scores(rows))
        return c

    def score_narrow(j, c):
        rows = narrow_rows(j)
        sc = jnp.where(j * kb + key_off <= t_row, scores(rows), NEG_INF)
        slab_ref[rows, :] = _f32_key(sc)
        return c

    lax.fori_loop(0, nwide, score_wide, 0)
    lax.fori_loop(n0, nkb, score_narrow, 0)

    def count_ge(trial):
        def cnt(blk):
            return _fold_keys(jnp.where(blk >= trial, 1.0, 0.0), jnp.sum)

        acc = lax.fori_loop(0, nwide, lambda jw, a: a + cnt(slab_ref[wide_rows(jw), :]),
                            jnp.zeros((SUBLANES, q), F32))
        acc = lax.fori_loop(n0, nkb, lambda j, a: a + cnt(slab_ref[narrow_rows(j), :]), acc)
        return jnp.sum(acc, axis=0, keepdims=True).astype(jnp.int32)

    assert DSA_CELLS * SUBLANES >= top_k and wide % (DSA_CELLS * SUBLANES) == 0

    def cells_wide(jw, cm):
        blk = slab_ref[wide_rows(jw), :].reshape(wide // (DSA_CELLS * SUBLANES), DSA_CELLS, SUBLANES, q)
        return jnp.maximum(cm, jnp.max(blk, axis=0))

    def cells_narrow(j, cm):
        blk = slab_ref[narrow_rows(j), :].reshape(kb // SUBLANES, SUBLANES, q)
        return jnp.concatenate([jnp.maximum(cm[:kb // SUBLANES], blk), cm[kb // SUBLANES:]], axis=0)

    cell_max = lax.fori_loop(0, nwide, cells_wide, jnp.full((DSA_CELLS, SUBLANES, q), INT_MIN, jnp.int32))
    cell_max = lax.fori_loop(n0, nkb, cells_narrow, cell_max)
    hi_key = jnp.max(jnp.max(cell_max, axis=0), axis=0, keepdims=True)
    lo_key = jnp.min(jnp.min(cell_max, axis=0), axis=0, keepdims=True)
    nbits = jnp.max(32 - lax.clz(lo_key ^ hi_key))
    prefix_mask = jnp.where(nbits >= 32, 0, lax.shift_left(jnp.int32(-1), jnp.minimum(nbits, 31)))
    thr0 = jnp.where(nbits >= 32, INT_MIN, hi_key & prefix_mask)

    def bisect(b, c):
        trial = c + lax.shift_left(jnp.int32(1), nbits - 1 - b)
        return jnp.where(count_ge(trial) >= top_k, trial, c)

    thr = lax.fori_loop(0, nbits, bisect, thr0)
    tied = jnp.logical_and(count_ge(thr) > top_k, thr > NEG_INF_KEY)
    any_tie = jnp.max(jnp.where(tied, 1, 0)) > 0

    thr_valid = jnp.maximum(thr, NEG_INF_KEY + 1)

    def mask_fast():
        def body(rows):
            slab_ref[rows, :] = jnp.where(slab_ref[rows, :] >= thr_valid, 0, NEG_INF_BITS)

        lax.fori_loop(0, nwide, lambda jw, c: body(wide_rows(jw)), None)
        lax.fori_loop(n0, nkb, lambda j, c: body(narrow_rows(j)), None)

    def mask_tied():
        need = (top_k - count_ge(thr + 1)).astype(F32)

        def body(j, seen):
            rows = narrow_rows(j)
            key = slab_ref[rows, :]
            eq = key == thr
            eq_f = jnp.where(eq, 1.0, 0.0)
            before = jnp.dot(tri_ref[...], eq_f.astype(BF16), preferred_element_type=F32)
            sel = jnp.logical_or(key > thr, jnp.logical_and(eq, seen + before < need))
            sel = jnp.logical_and(sel, key > NEG_INF_KEY)
            slab_ref[rows, :] = jnp.where(sel, 0, NEG_INF_BITS)
            return seen + jnp.sum(eq_f, axis=0, keepdims=True)

        lax.fori_loop(0, nkb, body, jnp.zeros((1, q), F32))

    lax.cond(any_tie, mask_tied, mask_fast)

    m_ref[...] = jnp.full(m_ref.shape, M_INIT, F32)
    l_ref[...] = jnp.zeros(l_ref.shape, F32)
    acc_ref[...] = jnp.zeros(acc_ref.shape, F32)

    def logits(rows, n, buf, rel):
        mask = lax.bitcast_convert_type(slab_ref[rows, :], F32)
        mask2 = jnp.concatenate([mask, mask], axis=1)
        tile_max = []
        for p in range(npair):
            ls = slice(p * LANES, (p + 1) * LANES)
            w_q = jnp.concatenate([qat_ref[(2 * p) * LANES:(2 * p + 1) * LANES, :],
                                   qat_ref[(2 * p + 1) * LANES:(2 * p + 2) * LANES, :]], axis=1)
            s = jnp.dot(ka_ref[rows, ls], w_q, preferred_element_type=F32) + mask2
            if rel is not None:
                s = s + btab_ref[rel, p]
            s_ref[buf, p, 0:n, :] = s
            tile_max.append(_fold_keys(s, jnp.max))
        return tuple(tile_max)

    def attend(rows, n, buf, tile_max):
        for p in range(npair):
            ls = slice(p * LANES, (p + 1) * LANES)
            m_old = m_ref[p]
            m_new = jnp.maximum(m_old, jnp.max(tile_max[p], axis=0, keepdims=True))
            alpha = jnp.exp2(m_old - m_new)
            pexp = jnp.exp2(s_ref[buf, p, 0:n, :] - m_new[0:1])
            m_ref[p] = m_new
            l_ref[p] = alpha * l_ref[p] + _fold_keys(pexp, jnp.sum)
            acc_ref[p] = alpha[0:1] * acc_ref[p] + jnp.dot(vat_ref[ls, rows], pexp.astype(BF16),
                                                           preferred_element_type=F32)

    def sweep(first, last, rows_of, n, rel_of):
        def body(t, carry):
            nxt = jnp.minimum(t + 1, last - 1)
            attend(rows_of(t), n, (t - first) % 2, carry)
            return logits(rows_of(nxt), n, (t + 1 - first) % 2, rel_of(nxt))

        lax.fori_loop(first, last, body, logits(rows_of(first), n, 0, rel_of(first)))

    @pl.when(nwide > 0)
    def _():
        sweep(0, nwide, wide_rows, wide, lambda jw: None)

    sweep(n0, nkb, narrow_rows, kb, lambda j: jnp.minimum(i - j, 2))

    for p in range(npair):
        o_t = acc_ref[p] / jnp.sum(l_ref[p], axis=0, keepdims=True)
        o_pair = jnp.concatenate([o_t[:A_HEAD_DIM, :q], o_t[A_HEAD_DIM:, q:]], axis=0).T
        ls = slice(p * LANES, (p + 1) * LANES)
        o_ref[:, ls] = (o_pair * _silu(ga_ref[:, ls])).astype(o_ref.dtype)


def _dsa_bias_table(rel_bias):
    ii = np.arange(DSA_Q)[:, None]
    jj = np.arange(DSA_KB)[None, :]
    buckets = np.stack([_t5_bucket_np(ii - jj), _t5_bucket_np(DSA_KB + ii - jj),
                        np.full((DSA_Q, DSA_KB), REL_BUCKETS - 1)])
    assert (_t5_bucket_np(np.arange(DSA_KB + 1, 4 * DSA_KB)) == REL_BUCKETS - 1).all()
    rb = rel_bias.astype(F32) * LOG2E
    tab = jnp.transpose(rb[buckets], (0, 3, 2, 1))
    tab = tab - rb[REL_BUCKETS - 1][None, :, None, None]
    tab = tab.reshape(3, A_HEADS // 2, 2, DSA_KB, DSA_Q)
    return jnp.transpose(tab, (0, 1, 3, 2, 4)).reshape(3, A_HEADS // 2, DSA_KB, 2 * DSA_Q)


def _dsa(qa, qi, wi, ga, ka, va, ki, btab, top_k):
    s = qa.shape[0]
    q, kb = DSA_Q, DSA_KB
    tri = jnp.asarray(np.tril(np.ones((kb, kb), np.float32), -1), BF16)
    qa_t, qi_t, va_t = qa.T, qi.T, va.T
    wi_t = wi[:, :SUBLANES].T
    whole = lambda shape: pl.BlockSpec(shape, lambda i: (0,) * len(shape), pipeline_mode=pl.Buffered(1))
    return pl.pallas_call(
        functools.partial(_dsa_kernel, top_k=top_k),
        grid=(s // q,),
        in_specs=[pl.BlockSpec((A_HEADS * LANES, q), lambda i: (0, i)),
                  pl.BlockSpec((IDX_HEADS * LANES, q), lambda i: (0, i)),
                  pl.BlockSpec((SUBLANES, q), lambda i: (0, i)),
                  pl.BlockSpec((q, A_W), lambda i: (i, 0)),
                  whole((s, A_W)), whole((A_W, s)), whole((s, LANES)),
                  whole((3, A_HEADS // 2, kb, 2 * q)), whole((kb, kb))],
        out_specs=pl.BlockSpec((q, A_W), lambda i: (i, 0)),
        out_shape=jax.ShapeDtypeStruct((s, A_W), BF16),
        scratch_shapes=[pltpu.VMEM((s, q), jnp.int32),
                        pltpu.VMEM((2, A_HEADS // 2, DSA_WIDE * kb, 2 * q), F32),
                        pltpu.VMEM((A_HEADS // 2, LANES, 2 * q), F32),
                        pltpu.VMEM((A_HEADS // 2, SUBLANES, 2 * q), F32),
                        pltpu.VMEM((A_HEADS // 2, SUBLANES, 2 * q), F32)],
        compiler_params=pltpu.CompilerParams(dimension_semantics=("arbitrary",), vmem_limit_bytes=VMEM_LIMIT),
        name="dsa",
    )(qa_t, qi_t, wi_t, ga, ka, va_t, ki, btab, tri)


GLA_BLOCK = 512


def _gla_kernel(qb_ref, kb_ref, vb_ref, gb_ref, lr_ref, wlr_ref, blr_ref, gn_ref, tril_ref, o_ref,
                state_ref, b_ref, att_ref):
    c, sub = GLA_CHUNK, GLA_SUB
    nsub = c // sub

    @pl.when(pl.program_id(0) == 0)
    def _():
        state_ref[...] = jnp.zeros(state_ref.shape, F32)

    dn_t = (((1,), (1,)), ((), ()))
    dn_0 = (((0,), (0,)), ((), ()))
    hp = lax.Precision.HIGHEST
    row_i = lax.broadcasted_iota(jnp.int32, (sub, sub), 0)
    col_j = lax.broadcasted_iota(jnp.int32, (sub, sub), 1)

    def chunk(ci, carry):
        r0 = pl.multiple_of(ci * c, c)
        z = jnp.dot(lr_ref[pl.ds(r0, c), :], wlr_ref[...], precision=hp, preferred_element_type=F32) + blr_ref[...]
        log_a = (jnp.minimum(z, 0.0) - jnp.log1p(jnp.exp(-jnp.abs(z)))) * (1.0 / GLA_TAU)
        b_ref[...] = jnp.dot(tril_ref[...], log_a, precision=hp, preferred_element_type=F32)
        for h in range(B_HEADS):
            ls = slice(h * LANES, (h + 1) * LANES)
            b = b_ref[:, ls]
            qh = qb_ref[pl.ds(r0, c), ls]
            kh = kb_ref[pl.ds(r0, c), ls]
            vh = vb_ref[pl.ds(r0, c), ls]
            state_t = state_ref[h]
            b_last = b[c - 1:c, :]
            o = lax.dot_general((qh * jnp.exp(b)).astype(BF16), state_t.astype(BF16), dn_t,
                                preferred_element_type=F32)
            for si in range(nsub):
                rs = slice(si * sub, (si + 1) * sub)
                b_i, q_i = b[rs], qh[rs]
                if si > 0:
                    ref_row = b[si * sub:si * sub + 1, :]
                    q_t = (q_i * jnp.exp(b_i - ref_row)).astype(BF16)
                    k_t = (kh[:si * sub] * jnp.exp(ref_row - b[:si * sub])).astype(BF16)
                    att_ref[rs, :si * sub] = lax.dot_general(q_t, k_t, dn_t, preferred_element_type=F32)
                k_i = kh[rs]
                diag = jnp.zeros((sub, sub), F32)
                for jl in range(sub):
                    diff = jnp.minimum(b_i - b_i[jl:jl + 1, :], 0.0)
                    col = jnp.sum(q_i * k_i[jl:jl + 1, :] * jnp.exp(diff), axis=-1, keepdims=True)
                    diag = jnp.where(col_j == jl, col, diag)
                att_ref[rs, rs] = jnp.where(col_j <= row_i, diag, 0.0)
                if si + 1 < nsub:
                    att_ref[rs, (si + 1) * sub:] = jnp.zeros((sub, c - (si + 1) * sub), F32)
            o = o + jnp.dot(att_ref[...].astype(BF16), vh.astype(BF16), preferred_element_type=F32)
            k_dec = (kh * jnp.exp(b_last - b)).astype(BF16)
            state_ref[h] = (jnp.exp(b_last) * state_t
                            + lax.dot_general(vh.astype(BF16), k_dec, dn_0, preferred_element_type=F32))
            o = o * lax.rsqrt(jnp.mean(o * o, axis=-1, keepdims=True) + EPS) * gn_ref[...]
            o_ref[pl.ds(r0, c), ls] = (o * _silu(gb_ref[pl.ds(r0, c), ls])).astype(o_ref.dtype)
        return carry

    lax.fori_loop(0, GLA_BLOCK // c, chunk, 0)


def _gla(qb, kb, vb, gb, lr, wlr_pad, blr_pad, gn):
    s = qb.shape[0]
    blk = min(GLA_BLOCK, s)
    assert blk == GLA_BLOCK and s % blk == 0
    tril = jnp.asarray(np.tril(np.ones((GLA_CHUNK, GLA_CHUNK), np.float32)))
    row = lambda wd: pl.BlockSpec((blk, wd), lambda i: (i, 0))
    const = lambda shape: pl.BlockSpec(shape, lambda i: (0,) * len(shape))
    return pl.pallas_call(
        _gla_kernel,
        grid=(s // blk,),
        in_specs=[row(B_HEADS * LANES), row(B_HEADS * LANES), row(B_VW), row(B_VW), row(LANES),
                  const((LANES, B_HEADS * LANES)), const((1, B_HEADS * LANES)), const((1, B_DV)),
                  const((GLA_CHUNK, GLA_CHUNK))],
        out_specs=row(B_VW),
        out_shape=jax.ShapeDtypeStruct((s, B_VW), BF16),
        scratch_shapes=[pltpu.VMEM((B_HEADS, LANES, B_DV), F32),
                        pltpu.VMEM((GLA_CHUNK, B_HEADS * LANES), F32),
                        pltpu.VMEM((GLA_CHUNK, GLA_CHUNK), F32)],
        compiler_params=pltpu.CompilerParams(dimension_semantics=("arbitrary",), vmem_limit_bytes=VMEM_LIMIT),
        name="gla",
    )(qb, kb, vb, gb, lr, wlr_pad, blr_pad, gn, tril)


def _mid_kernel(x_ref, ma_ref, mb_ref, wo_ref, g_ref, wi_ref, x1_ref, u_ref, sg_ref):
    x1 = (x_ref[...]
          + jnp.dot(ma_ref[...], wo_ref[:A_W, :], preferred_element_type=F32)
          + jnp.dot(mb_ref[...], wo_ref[A_W:, :], preferred_element_type=F32))
    x1_ref[...] = x1
    h = (x1 * lax.rsqrt(jnp.mean(x1 * x1, axis=-1, keepdims=True) + EPS) * g_ref[...]).astype(BF16)
    val = jnp.dot(h, wi_ref[:, :D_MODEL], preferred_element_type=F32)
    glu = jnp.dot(h, wi_ref[:, D_MODEL:2 * D_MODEL], preferred_element_type=F32)
    gate = jnp.dot(h, wi_ref[:, 2 * D_MODEL:], preferred_element_type=F32)
    u_ref[...] = val * jax.nn.sigmoid(glu)
    sg_ref[...] = _silu(gate)


def _mid(x2d, mix_a, mix_b, w_out, g, w_in, tm):
    s = x2d.shape[0]
    row = lambda wd: pl.BlockSpec((tm, wd), lambda i: (i, 0))
    const = lambda shape: pl.BlockSpec(shape, lambda i: (0,) * len(shape))
    return pl.pallas_call(
        _mid_kernel,
        grid=(s // tm,),
        in_specs=[row(D_MODEL), row(A_W), row(B_VW), const((A_W + B_VW, D_MODEL)), const((1, D_MODEL)),
                  const((D_MODEL, 3 * D_MODEL))],
        out_specs=[row(D_MODEL), row(D_MODEL), row(D_MODEL)],
        out_shape=[jax.ShapeDtypeStruct((s, D_MODEL), F32)] * 3,
        compiler_params=pltpu.CompilerParams(dimension_semantics=("arbitrary",), vmem_limit_bytes=VMEM_LIMIT),
        name="mid",
    )(x2d, mix_a, mix_b, w_out, g, w_in)


HALO = 32


def _odd_out_kernel(u_ref, halo_ref, sg_ref, x1_ref, cw_ref, cb_ref, lg_ref, lb_ref, wo_ref, nf_ref, o_ref,
                    ubuf_ref, y_ref, *, tm):
    i = pl.program_id(0)
    ubuf_ref[HALO:, :] = u_ref[...]
    ubuf_ref[:HALO, :] = jnp.where(i > 0, halo_ref[...], 0.0)
    base = HALO - (CONV_WIDTH - 1)
    for cs in range(D_MODEL // LANES):
        ls = slice(cs * LANES, (cs + 1) * LANES)
        acc = jnp.zeros((tm, LANES), F32)
        for k in range(CONV_WIDTH):
            acc = acc + ubuf_ref[base + k:base + k + tm, ls] * cw_ref[k:k + 1, ls]
        y_ref[:, ls] = acc + cb_ref[:, ls]
    y = y_ref[...]
    mu = jnp.mean(y, axis=-1, keepdims=True)
    yc = y - mu
    var = jnp.mean(yc * yc, axis=-1, keepdims=True)
    yn = yc * lax.rsqrt(var + EPS) * lg_ref[...] + lb_ref[...]
    mixed = (_silu(yn) * sg_ref[...]).astype(BF16)
    x2 = x1_ref[...] + jnp.dot(mixed, wo_ref[...], preferred_element_type=F32)
    o_ref[...] = x2 * lax.rsqrt(jnp.mean(x2 * x2, axis=-1, keepdims=True) + EPS) * nf_ref[...]


def _odd_out(u, sg, x1, cw_pad, cb, lg, lb, w_out, nf, tm):
    s = u.shape[0]
    row = pl.BlockSpec((tm, D_MODEL), lambda i: (i, 0))
    const = lambda shape: pl.BlockSpec(shape, lambda i: (0,) * len(shape))
    halo = pl.BlockSpec((HALO, D_MODEL), lambda i: (jnp.maximum(i * (tm // HALO) - 1, 0), 0))
    return pl.pallas_call(
        functools.partial(_odd_out_kernel, tm=tm),
        grid=(s // tm,),
        in_specs=[row, halo, row, row, const((HALO, D_MODEL)), const((1, D_MODEL)), const((1, D_MODEL)),
                  const((1, D_MODEL)), const((D_MODEL, D_MODEL)), const((1, D_MODEL))],
        out_specs=row,
        out_shape=jax.ShapeDtypeStruct((s, D_MODEL), F32),
        scratch_shapes=[pltpu.VMEM((HALO + tm, D_MODEL), F32), pltpu.VMEM((tm, D_MODEL), F32)],
        compiler_params=pltpu.CompilerParams(dimension_semantics=("arbitrary",), vmem_limit_bytes=VMEM_LIMIT),
        name="odd_out",
    )(u, u, sg, x1, cw_pad, cb, lg, lb, w_out, nf)


def kernel(x, rel_bias, norm_even, w_in_even, w_gla_lr, b_gla_lr, gla_norm, w_out_even, norm_odd, w_in_odd,
           conv_w, conv_b, conv_ln_g, conv_ln_b, w_out_odd, norm_f):
    bsz, s, d = x.shape
    assert bsz == 1 and d == D_MODEL and s % GLA_BLOCK == 0
    assert norm_even.shape[0] == 1 and norm_odd.shape[0] == 1
    top_k = min(TOPK_MAX, s // 4)
    tm = 256
    x2d = x.reshape(s, d)

    w_pad = _even_in_weights(w_in_even[0])
    qa, ka, va, ga, qi, ki, wi, qb, kb, vb, gb, lr = _even_in(x2d, norm_even[0].reshape(1, d), w_pad, tm)
    mix_a = _dsa(qa, qi, wi, ga, ka, va, ki, _dsa_bias_table(rel_bias), top_k)

    wlr_pad = jnp.zeros((LANES, B_HEADS * LANES), F32)
    blr_pad = jnp.zeros((1, B_HEADS * LANES), F32)
    for h in range(B_HEADS):
        wlr_pad = wlr_pad.at[:GLA_RANK, h * LANES:h * LANES + B_DK].set(w_gla_lr[0][:, h * B_DK:(h + 1) * B_DK])
        blr_pad = blr_pad.at[0, h * LANES:h * LANES + B_DK].set(b_gla_lr[0][h * B_DK:(h + 1) * B_DK])
    mix_b = _gla(qb, kb, vb, gb, lr, wlr_pad, blr_pad, gla_norm[0].reshape(1, B_DV))

    x1, u, sg = _mid(x2d, mix_a, mix_b, w_out_even[0].astype(BF16), norm_odd[0].reshape(1, d),
                     w_in_odd[0].astype(BF16), tm)

    cw_pad = jnp.zeros((HALO, d), F32).at[:CONV_WIDTH].set(conv_w[0])
    out = _odd_out(u, sg, x1, cw_pad, conv_b[0].reshape(1, d), conv_ln_g[0].reshape(1, d),
                   conv_ln_b[0].reshape(1, d), w_out_odd[0].astype(BF16), norm_f.reshape(1, d), tm)
    return out.reshape(bsz, s, d)
```

```python
import functools
import math

import jax
import jax.numpy as jnp
import numpy as np
from jax import lax
from jax.experimental import pallas as pl
from jax.experimental.pallas import tpu as pltpu

D_MODEL = 1024
A_HEADS = 8
A_HEAD_DIM = 64
IDX_HEADS = 4
IDX_DIM = 64
TOPK_MAX = 256
B_HEADS = 4
B_DK = 64
B_DV = 128
GLA_RANK = 16
GLA_TAU = 16.0
GLA_CHUNK = 64
GLA_SUB = 16
CONV_WIDTH = 31
REL_BUCKETS = 32
REL_MAX_EXACT = 16
REL_MAX_DIST = 128
EPS = 1e-6

A_W = A_HEADS * A_HEAD_DIM
B_KW = B_HEADS * B_DK
B_VW = B_HEADS * B_DV

LANES = 128
SUBLANES = 8
VMEM_LIMIT = 56 * 1024 * 1024

F32 = jnp.float32
BF16 = jnp.bfloat16
NEG_INF = float("-inf")
INT_MIN = -(2 ** 31)
NEG_INF_BITS = int(np.array(-np.inf, np.float32).view(np.int32))
NEG_INF_KEY = NEG_INF_BITS ^ 0x7FFFFFFF
M_INIT = -1e30
LOG2E = math.log2(math.e)

C_QA = 0
C_KA = C_QA + A_HEADS * LANES
C_VA = C_KA + A_W
C_GA = C_VA + A_W
C_QI = C_GA + A_W
C_KI = C_QI + IDX_HEADS * LANES
C_WI = C_KI + LANES
C_QB = C_WI + LANES
C_KB = C_QB + B_HEADS * LANES
C_VB = C_KB + B_HEADS * LANES
C_GB = C_VB + B_VW
C_LR = C_GB + B_VW
C_END = C_LR + LANES


def _t5_bucket_np(dist):
    dist = np.maximum(dist, 0)
    d = np.maximum(dist, 1).astype(np.float32)
    large = REL_MAX_EXACT + (np.log(d / REL_MAX_EXACT) / math.log(REL_MAX_DIST / REL_MAX_EXACT)
                             * (REL_BUCKETS - REL_MAX_EXACT)).astype(np.int32)
    large = np.minimum(large, REL_BUCKETS - 1)
    return np.where(dist < REL_MAX_EXACT, dist, large)


def _silu(x):
    return x * jax.nn.sigmoid(x)


def _even_in_kernel(x_ref, g_ref, w_ref, qa_ref, ka_ref, va_ref, ga_ref, qi_ref, ki_ref, wi_ref,
                    qb_ref, kb_ref, vb_ref, gb_ref, lr_ref):
    x = x_ref[...]
    h = (x * lax.rsqrt(jnp.mean(x * x, axis=-1, keepdims=True) + EPS) * g_ref[...]).astype(BF16)

    def proj(lo, hi):
        return jnp.dot(h, w_ref[:, lo:hi], preferred_element_type=F32)

    qa_ref[...] = proj(C_QA, C_KA).astype(BF16)
    ka_ref[...] = proj(C_KA, C_VA).astype(BF16)
    va_ref[...] = proj(C_VA, C_GA).astype(BF16)
    ga_ref[...] = proj(C_GA, C_QI)
    qi_ref[...] = proj(C_QI, C_KI).astype(BF16)
    ki_ref[...] = proj(C_KI, C_WI).astype(BF16)
    wi_ref[...] = proj(C_WI, C_QB)
    qb_ref[...] = proj(C_QB, C_KB)
    kb_ref[...] = proj(C_KB, C_VB)
    vb_ref[...] = proj(C_VB, C_GB)
    gb_ref[...] = proj(C_GB, C_LR)
    lr_ref[...] = proj(C_LR, C_END)


def _even_in_weights(w):
    cols = np.cumsum([0, A_W, A_W, A_W, A_W, IDX_HEADS * IDX_DIM, IDX_DIM, IDX_HEADS,
                      B_KW, B_KW, B_VW, B_VW, GLA_RANK])
    (s_qa, s_ka, s_va, s_ga, s_qi, s_ki, s_wi, s_qb, s_kb, s_vb, s_gb, s_lr, _) = [int(c) for c in cols]
    out = jnp.zeros((D_MODEL, C_END), F32)
    a_scale = (A_HEAD_DIM ** -0.5) * LOG2E
    idx_scale = (IDX_DIM ** -0.5) * (IDX_HEADS ** -0.5)
    b_scale = B_DK ** -0.5
    for h in range(A_HEADS):
        dst = C_QA + h * LANES + (h % 2) * A_HEAD_DIM
        out = out.at[:, dst:dst + A_HEAD_DIM].set(w[:, s_qa + h * A_HEAD_DIM:s_qa + (h + 1) * A_HEAD_DIM] * a_scale)
    out = out.at[:, C_KA:C_KA + A_W].set(w[:, s_ka:s_ka + A_W])
    out = out.at[:, C_VA:C_VA + A_W].set(w[:, s_va:s_va + A_W])
    out = out.at[:, C_GA:C_GA + A_W].set(w[:, s_ga:s_ga + A_W])
    for h in range(IDX_HEADS):
        dst = C_QI + h * LANES
        out = out.at[:, dst:dst + IDX_DIM].set(w[:, s_qi + h * IDX_DIM:s_qi + (h + 1) * IDX_DIM])
    out = out.at[:, C_KI:C_KI + IDX_DIM].set(w[:, s_ki:s_ki + IDX_DIM])
    out = out.at[:, C_WI:C_WI + IDX_HEADS].set(w[:, s_wi:s_wi + IDX_HEADS] * idx_scale)
    for h in range(B_HEADS):
        dq = C_QB + h * LANES
        dk = C_KB + h * LANES
        out = out.at[:, dq:dq + B_DK].set(w[:, s_qb + h * B_DK:s_qb + (h + 1) * B_DK] * b_scale)
        out = out.at[:, dk:dk + B_DK].set(w[:, s_kb + h * B_DK:s_kb + (h + 1) * B_DK])
    out = out.at[:, C_VB:C_VB + B_VW].set(w[:, s_vb:s_vb + B_VW])
    out = out.at[:, C_GB:C_GB + B_VW].set(w[:, s_gb:s_gb + B_VW])
    out = out.at[:, C_LR:C_LR + GLA_RANK].set(w[:, s_lr:s_lr + GLA_RANK])
    return out.astype(BF16)


def _even_in(x2d, g, w_pad, tm):
    s = x2d.shape[0]
    widths = [(C_KA - C_QA, BF16), (A_W, BF16), (A_W, BF16), (A_W, F32), (C_KI - C_QI, BF16), (LANES, BF16),
              (LANES, F32), (C_KB - C_QB, F32), (C_VB - C_KB, F32), (B_VW, F32), (B_VW, F32), (LANES, F32)]
    return pl.pallas_call(
        _even_in_kernel,
        grid=(s // tm,),
        in_specs=[pl.BlockSpec((tm, D_MODEL), lambda i: (i, 0)),
                  pl.BlockSpec((1, D_MODEL), lambda i: (0, 0)),
                  pl.BlockSpec((D_MODEL, C_END), lambda i: (0, 0))],
        out_specs=[pl.BlockSpec((tm, wd), lambda i: (i, 0)) for wd, _ in widths],
        out_shape=[jax.ShapeDtypeStruct((s, wd), dt) for wd, dt in widths],
        compiler_params=pltpu.CompilerParams(dimension_semantics=("arbitrary",), vmem_limit_bytes=VMEM_LIMIT),
        name="even_in",
    )(x2d, g, w_pad)


DSA_Q = 128
DSA_KB = 128
DSA_WIDE = 4
DSA_NACC = 4
DSA_CELLS = 32


def _f32_key(x):
    bits = lax.bitcast_convert_type(x, jnp.int32)
    return bits ^ (lax.shift_right_arithmetic(bits, 31) & 0x7FFFFFFF)


def _fold_keys(x, op):
    n, w = x.shape
    group = DSA_NACC * SUBLANES
    if n % group == 0 and n > group:
        x = op(x.reshape(n // group, DSA_NACC, SUBLANES, w), axis=0)
        return op(x, axis=0)
    return op(x.reshape(n // SUBLANES, SUBLANES, w), axis=0)


def _dsa_kernel(qat_ref, qit_ref, wit_ref, ga_ref, ka_ref, vat_ref, ki_ref, btab_ref, tri_ref, o_ref,
                slab_ref, s_ref, acc_ref, m_ref, l_ref, *, top_k):
    q, kb, wide = DSA_Q, DSA_KB, DSA_WIDE * DSA_KB
    i = pl.program_id(0)
    nkb = i + 1
    nwide = jnp.maximum(i - 1, 0) // DSA_WIDE
    n0 = nwide * DSA_WIDE
    t_row = i * q + lax.broadcasted_iota(jnp.int32, (1, q), 1)
    key_off = lax.broadcasted_iota(jnp.int32, (kb, 1), 0)
    npair = A_HEADS // 2

    def wide_rows(jw):
        return pl.ds(pl.multiple_of(jw * wide, wide), wide)

    def narrow_rows(j):
        return pl.ds(pl.multiple_of(j * kb, kb), kb)

    def pipelined(first, last, produce, consume):
        def body(t, carry):
            consume(t, (t - first) % 2, carry)
            return produce(jnp.minimum(t + 1, last - 1), (t + 1 - first) % 2)

        lax.fori_loop(first, last, body, produce(first, 0))

    w_head = [wit_ref[h:h + 1, :] for h in range(IDX_HEADS)]

    def scores(rows):
        k_idx = ki_ref[rows, :]
        sc = None
        for h in range(IDX_HEADS):
            s = jnp.dot(k_idx, qit_ref[h * LANES:(h + 1) * LANES, :], preferred_element_type=F32)
            term = w_head[h] * jnp.maximum(s, 0.0)
            sc = term if sc is None else sc + term
        return sc

    def score_wide(jw, c):
        rows = wide_rows(jw)
        slab_ref[rows, :] = _f32_key(scores(rows))
        return c

    def score_narrow(j, c):
        rows = narrow_rows(j)
        sc = jnp.where(j * kb + key_off <= t_row, scores(rows), NEG_INF)
        slab_ref[rows, :] = _f32_key(sc)
        return c

    lax.fori_loop(0, nwide, score_wide, 0)
    lax.fori_loop(n0, nkb, score_narrow, 0)

    def count_ge(trial):
        def cnt(blk):
            return _fold_keys(jnp.where(blk >= trial, 1.0, 0.0), jnp.sum)

        acc = lax.fori_loop(0, nwide, lambda jw, a: a + cnt(slab_ref[wide_rows(jw), :]),
                            jnp.zeros((SUBLANES, q), F32))
        acc = lax.fori_loop(n0, nkb, lambda j, a: a + cnt(slab_ref[narrow_rows(j), :]), acc)
        return jnp.sum(acc, axis=0, keepdims=True).astype(jnp.int32)

    assert DSA_CELLS * SUBLANES >= top_k and wide % (DSA_CELLS * SUBLANES) == 0

    def cells_wide(jw, cm):
        blk = slab_ref[wide_rows(jw), :].reshape(wide // (DSA_CELLS * SUBLANES), DSA_CELLS, SUBLANES, q)
        return jnp.maximum(cm, jnp.max(blk, axis=0))

    def cells_narrow(j, cm):
        blk = slab_ref[narrow_rows(j), :].reshape(kb // SUBLANES, SUBLANES, q)
        return jnp.concatenate([jnp.maximum(cm[:kb // SUBLANES], blk), cm[kb // SUBLANES:]], axis=0)

    cell_max = lax.fori_loop(0, nwide, cells_wide, jnp.full((DSA_CELLS, SUBLANES, q), INT_MIN, jnp.int32))
    cell_max = lax.fori_loop(n0, nkb, cells_narrow, cell_max)
    hi_key = jnp.max(jnp.max(cell_max, axis=0), axis=0, keepdims=True)
    lo_key = jnp.min(jnp.min(cell_max, axis=0), axis=0, keepdims=True)
    same_sign = jnp.min(lo_key ^ hi_key) >= 0
    base = jnp.where(same_sign, lo_key, INT_MIN)
    span = jnp.where(same_sign, hi_key - lo_key, -1)
    nbits = jnp.where(same_sign, jnp.max(32 - lax.clz(span)), 32)

    def bisect(b, off):
        trial_off = off + lax.shift_left(jnp.int32(1), nbits - 1 - b)
        in_span = (trial_off ^ INT_MIN) <= (span ^ INT_MIN)
        ok = jnp.logical_and(in_span, count_ge(base + trial_off) >= top_k)
        return jnp.where(ok, trial_off, off)

    thr = base + lax.fori_loop(0, nbits, bisect, jnp.zeros((1, q), jnp.int32))
    tied = jnp.logical_and(count_ge(thr) > top_k, thr > NEG_INF_KEY)
    any_tie = jnp.max(jnp.where(tied, 1, 0)) > 0

    thr_valid = jnp.maximum(thr, NEG_INF_KEY + 1)

    def mask_fast():
        def body(rows):
            slab_ref[rows, :] = jnp.where(slab_ref[rows, :] >= thr_valid, 0, NEG_INF_BITS)

        lax.fori_loop(0, nwide, lambda jw, c: body(wide_rows(jw)), None)
        lax.fori_loop(n0, nkb, lambda j, c: body(narrow_rows(j)), None)

    def mask_tied():
        need = (top_k - count_ge(thr + 1)).astype(F32)

        def body(j, seen):
            rows = narrow_rows(j)
            key = slab_ref[rows, :]
            eq = key == thr
            eq_f = jnp.where(eq, 1.0, 0.0)
            before = jnp.dot(tri_ref[...], eq_f.astype(BF16), preferred_element_type=F32)
            sel = jnp.logical_or(key > thr, jnp.logical_and(eq, seen + before < need))
            sel = jnp.logical_and(sel, key > NEG_INF_KEY)
            slab_ref[rows, :] = jnp.where(sel, 0, NEG_INF_BITS)
            return seen + jnp.sum(eq_f, axis=0, keepdims=True)

        lax.fori_loop(0, nkb, body, jnp.zeros((1, q), F32))

    lax.cond(any_tie, mask_tied, mask_fast)

    m_ref[...] = jnp.full(m_ref.shape, M_INIT, F32)
    l_ref[...] = jnp.zeros(l_ref.shape, F32)
    acc_ref[...] = jnp.zeros(acc_ref.shape, F32)

    def logits(rows, n, buf, rel):
        mask = lax.bitcast_convert_type(slab_ref[rows, :], F32)
        mask2 = jnp.concatenate([mask, mask], axis=1)
        tile_max = []
        for p in range(npair):
            ls = slice(p * LANES, (p + 1) * LANES)
            w_q = jnp.concatenate([qat_ref[(2 * p) * LANES:(2 * p + 1) * LANES, :],
                                   qat_ref[(2 * p + 1) * LANES:(2 * p + 2) * LANES, :]], axis=1)
            s = jnp.dot(ka_ref[rows, ls], w_q, preferred_element_type=F32) + mask2
            if rel is not None:
                s = s + btab_ref[rel, p]
            s_ref[buf, p, 0:n, :] = s
            tile_max.append(_fold_keys(s, jnp.max))
        return tuple(tile_max)

    def attend(rows, n, buf, tile_max):
        for p in range(npair):
            ls = slice(p * LANES, (p + 1) * LANES)
            m_old = m_ref[p]
            m_new = jnp.maximum(m_old, jnp.max(tile_max[p], axis=0, keepdims=True))
            alpha = jnp.exp2(m_old - m_new)
            pexp = jnp.exp2(s_ref[buf, p, 0:n, :] - m_new[0:1])
            m_ref[p] = m_new
            l_ref[p] = alpha * l_ref[p] + _fold_keys(pexp, jnp.sum)
            acc_ref[p] = alpha[0:1] * acc_ref[p] + jnp.dot(vat_ref[ls, rows], pexp.astype(BF16),
                                                           preferred_element_type=F32)

    @pl.when(nwide > 0)
    def _():
        pipelined(0, nwide, lambda jw, buf: logits(wide_rows(jw), wide, buf, None),
                  lambda jw, buf, tile_max: attend(wide_rows(jw), wide, buf, tile_max))

    pipelined(n0, nkb, lambda j, buf: logits(narrow_rows(j), kb, buf, jnp.minimum(i - j, 2)),
              lambda j, buf, tile_max: attend(narrow_rows(j), kb, buf, tile_max))

    for p in range(npair):
        o_t = acc_ref[p] / jnp.sum(l_ref[p], axis=0, keepdims=True)
        o_pair = jnp.concatenate([o_t[:A_HEAD_DIM, :q], o_t[A_HEAD_DIM:, q:]], axis=0).T
        ls = slice(p * LANES, (p + 1) * LANES)
        o_ref[:, ls] = (o_pair * _silu(ga_ref[:, ls])).astype(o_ref.dtype)


def _dsa_bias_table(rel_bias):
    ii = np.arange(DSA_Q)[:, None]
    jj = np.arange(DSA_KB)[None, :]
    buckets = np.stack([_t5_bucket_np(ii - jj), _t5_bucket_np(DSA_KB + ii - jj),
                        np.full((DSA_Q, DSA_KB), REL_BUCKETS - 1)])
    assert (_t5_bucket_np(np.arange(DSA_KB + 1, 4 * DSA_KB)) == REL_BUCKETS - 1).all()
    rb = rel_bias.astype(F32) * LOG2E
    tab = jnp.transpose(rb[buckets], (0, 3, 2, 1))
    tab = tab - rb[REL_BUCKETS - 1][None, :, None, None]
    tab = tab.reshape(3, A_HEADS // 2, 2, DSA_KB, DSA_Q)
    return jnp.transpose(tab, (0, 1, 3, 2, 4)).reshape(3, A_HEADS // 2, DSA_KB, 2 * DSA_Q)


def _dsa(qa, qi, wi, ga, ka, va, ki, btab, top_k):
    s = qa.shape[0]
    q, kb = DSA_Q, DSA_KB
    tri = jnp.asarray(np.tril(np.ones((kb, kb), np.float32), -1), BF16)
    qa_t, qi_t, va_t = qa.T, qi.T, va.T
    wi_t = wi[:, :SUBLANES].T
    whole = lambda shape: pl.BlockSpec(shape, lambda i: (0,) * len(shape), pipeline_mode=pl.Buffered(1))
    return pl.pallas_call(
        functools.partial(_dsa_kernel, top_k=top_k),
        grid=(s // q,),
        in_specs=[pl.BlockSpec((A_HEADS * LANES, q), lambda i: (0, i)),
                  pl.BlockSpec((IDX_HEADS * LANES, q), lambda i: (0, i)),
                  pl.BlockSpec((SUBLANES, q), lambda i: (0, i)),
                  pl.BlockSpec((q, A_W), lambda i: (i, 0)),
                  whole((s, A_W)), whole((A_W, s)), whole((s, LANES)),
                  whole((3, A_HEADS // 2, kb, 2 * q)), whole((kb, kb))],
        out_specs=pl.BlockSpec((q, A_W), lambda i: (i, 0)),
        out_shape=jax.ShapeDtypeStruct((s, A_W), BF16),
        scratch_shapes=[pltpu.VMEM((s, q), jnp.int32),
                        pltpu.VMEM((2, A_HEADS // 2, DSA_WIDE * kb, 2 * q), F32),
                        pltpu.VMEM((A_HEADS // 2, LANES, 2 * q), F32),
                        pltpu.VMEM((A_HEADS // 2, SUBLANES, 2 * q), F32),
                        pltpu.VMEM((A_HEADS // 2, SUBLANES, 2 * q), F32)],
        compiler_params=pltpu.CompilerParams(dimension_semantics=("arbitrary",), vmem_limit_bytes=VMEM_LIMIT),
        name="dsa",
    )(qa_t, qi_t, wi_t, ga, ka, va_t, ki, btab, tri)


GLA_BLOCK = 512


def _gla_kernel(qb_ref, kb_ref, vb_ref, gb_ref, lr_ref, wlr_ref, blr_ref, gn_ref, tril_ref, o_ref,
                state_ref, b_ref, att_ref):
    c, sub = GLA_CHUNK, GLA_SUB
    nsub = c // sub

    @pl.when(pl.program_id(0) == 0)
    def _():
        state_ref[...] = jnp.zeros(state_ref.shape, F32)

    dn_t = (((1,), (1,)), ((), ()))
    dn_0 = (((0,), (0,)), ((), ()))
    hp = lax.Precision.HIGHEST
    row_i = lax.broadcasted_iota(jnp.int32, (sub, sub), 0)
    col_j = lax.broadcasted_iota(jnp.int32, (sub, sub), 1)

    def chunk(ci, carry):
        r0 = pl.multiple_of(ci * c, c)
        z = jnp.dot(lr_ref[pl.ds(r0, c), :], wlr_ref[...], precision=hp, preferred_element_type=F32) + blr_ref[...]
        log_a = (jnp.minimum(z, 0.0) - jnp.log1p(jnp.exp(-jnp.abs(z)))) * (1.0 / GLA_TAU)
        b_ref[...] = jnp.dot(tril_ref[...], log_a, precision=hp, preferred_element_type=F32)
        for h in range(B_HEADS):
            ls = slice(h * LANES, (h + 1) * LANES)
            b = b_ref[:, ls]
            qh = qb_ref[pl.ds(r0, c), ls]
            kh = kb_ref[pl.ds(r0, c), ls]
            vh = vb_ref[pl.ds(r0, c), ls]
            state_t = state_ref[h]
            b_last = b[c - 1:c, :]
            o = lax.dot_general((qh * jnp.exp(b)).astype(BF16), state_t.astype(BF16), dn_t,
                                preferred_element_type=F32)
            for si in range(nsub):
                rs = slice(si * sub, (si + 1) * sub)
                b_i, q_i = b[rs], qh[rs]
                if si > 0:
                    ref_row = b[si * sub:si * sub + 1, :]
                    q_t = (q_i * jnp.exp(b_i - ref_row)).astype(BF16)
                    k_t = (kh[:si * sub] * jnp.exp(ref_row - b[:si * sub])).astype(BF16)
                    att_ref[rs, :si * sub] = lax.dot_general(q_t, k_t, dn_t, preferred_element_type=F32)
                k_i = kh[rs]
                diag = jnp.zeros((sub, sub), F32)
                for jl in range(sub):
                    diff = jnp.minimum(b_i - b_i[jl:jl + 1, :], 0.0)
                    col = jnp.sum(q_i * k_i[jl:jl + 1, :] * jnp.exp(diff), axis=-1, keepdims=True)
                    diag = jnp.where(col_j == jl, col, diag)
                att_ref[rs, rs] = jnp.where(col_j <= row_i, diag, 0.0)
                if si + 1 < nsub:
                    att_ref[rs, (si + 1) * sub:] = jnp.zeros((sub, c - (si + 1) * sub), F32)
            o = o + jnp.dot(att_ref[...].astype(BF16), vh.astype(BF16), preferred_element_type=F32)
            k_dec = (kh * jnp.exp(b_last - b)).astype(BF16)
            state_ref[h] = (jnp.exp(b_last) * state_t
                            + lax.dot_general(vh.astype(BF16), k_dec, dn_0, preferred_element_type=F32))
            o = o * lax.rsqrt(jnp.mean(o * o, axis=-1, keepdims=True) + EPS) * gn_ref[...]
            o_ref[pl.ds(r0, c), ls] = (o * _silu(gb_ref[pl.ds(r0, c), ls])).astype(o_ref.dtype)
        return carry

    lax.fori_loop(0, GLA_BLOCK // c, chunk, 0)


def _gla(qb, kb, vb, gb, lr, wlr_pad, blr_pad, gn):
    s = qb.shape[0]
    blk = min(GLA_BLOCK, s)
    assert blk == GLA_BLOCK and s % blk == 0
    tril = jnp.asarray(np.tril(np.ones((GLA_CHUNK, GLA_CHUNK), np.float32)))
    row = lambda wd: pl.BlockSpec((blk, wd), lambda i: (i, 0))
    const = lambda shape: pl.BlockSpec(shape, lambda i: (0,) * len(shape))
    return pl.pallas_call(
        _gla_kernel,
        grid=(s // blk,),
        in_specs=[row(B_HEADS * LANES), row(B_HEADS * LANES), row(B_VW), row(B_VW), row(LANES),
                  const((LANES, B_HEADS * LANES)), const((1, B_HEADS * LANES)), const((1, B_DV)),
                  const((GLA_CHUNK, GLA_CHUNK))],
        out_specs=row(B_VW),
        out_shape=jax.ShapeDtypeStruct((s, B_VW), BF16),
        scratch_shapes=[pltpu.VMEM((B_HEADS, LANES, B_DV), F32),
                        pltpu.VMEM((GLA_CHUNK, B_HEADS * LANES), F32),
                        pltpu.VMEM((GLA_CHUNK, GLA_CHUNK), F32)],
        compiler_params=pltpu.CompilerParams(dimension_semantics=("arbitrary",), vmem_limit_bytes=VMEM_LIMIT),
        name="gla",
    )(qb, kb, vb, gb, lr, wlr_pad, blr_pad, gn, tril)


def _mid_kernel(x_ref, ma_ref, mb_ref, wo_ref, g_ref, wi_ref, x1_ref, u_ref, sg_ref):
    x1 = (x_ref[...]
          + jnp.dot(ma_ref[...], wo_ref[:A_W, :], preferred_element_type=F32)
          + jnp.dot(mb_ref[...], wo_ref[A_W:, :], preferred_element_type=F32))
    x1_ref[...] = x1
    h = (x1 * lax.rsqrt(jnp.mean(x1 * x1, axis=-1, keepdims=True) + EPS) * g_ref[...]).astype(BF16)
    val = jnp.dot(h, wi_ref[:, :D_MODEL], preferred_element_type=F32)
    glu = jnp.dot(h, wi_ref[:, D_MODEL:2 * D_MODEL], preferred_element_type=F32)
    gate = jnp.dot(h, wi_ref[:, 2 * D_MODEL:], preferred_element_type=F32)
    u_ref[...] = val * jax.nn.sigmoid(glu)
    sg_ref[...] = _silu(gate)


def _mid(x2d, mix_a, mix_b, w_out, g, w_in, tm):
    s = x2d.shape[0]
    row = lambda wd: pl.BlockSpec((tm, wd), lambda i: (i, 0))
    const = lambda shape: pl.BlockSpec(shape, lambda i: (0,) * len(shape))
    return pl.pallas_call(
        _mid_kernel,
        grid=(s // tm,),
        in_specs=[row(D_MODEL), row(A_W), row(B_VW), const((A_W + B_VW, D_MODEL)), const((1, D_MODEL)),
                  const((D_MODEL, 3 * D_MODEL))],
        out_specs=[row(D_MODEL), row(D_MODEL), row(D_MODEL)],
        out_shape=[jax.ShapeDtypeStruct((s, D_MODEL), F32)] * 3,
        compiler_params=pltpu.CompilerParams(dimension_semantics=("arbitrary",), vmem_limit_bytes=VMEM_LIMIT),
        name="mid",
    )(x2d, mix_a, mix_b, w_out, g, w_in)


HALO = 32


def _odd_out_kernel(u_ref, halo_ref, sg_ref, x1_ref, cw_ref, cb_ref, lg_ref, lb_ref, wo_ref, nf_ref, o_ref,
                    ubuf_ref, y_ref, *, tm):
    i = pl.program_id(0)
    ubuf_ref[HALO:, :] = u_ref[...]
    ubuf_ref[:HALO, :] = jnp.where(i > 0, halo_ref[...], 0.0)
    base = HALO - (CONV_WIDTH - 1)
    for cs in range(D_MODEL // LANES):
        ls = slice(cs * LANES, (cs + 1) * LANES)
        acc = jnp.zeros((tm, LANES), F32)
        for k in range(CONV_WIDTH):
            acc = acc + ubuf_ref[base + k:base + k + tm, ls] * cw_ref[k:k + 1, ls]
        y_ref[:, ls] = acc + cb_ref[:, ls]
    y = y_ref[...]
    mu = jnp.mean(y, axis=-1, keepdims=True)
    yc = y - mu
    var = jnp.mean(yc * yc, axis=-1, keepdims=True)
    yn = yc * lax.rsqrt(var + EPS) * lg_ref[...] + lb_ref[...]
    mixed = (_silu(yn) * sg_ref[...]).astype(BF16)
    x2 = x1_ref[...] + jnp.dot(mixed, wo_ref[...], preferred_element_type=F32)
    o_ref[...] = x2 * lax.rsqrt(jnp.mean(x2 * x2, axis=-1, keepdims=True) + EPS) * nf_ref[...]


def _odd_out(u, sg, x1, cw_pad, cb, lg, lb, w_out, nf, tm):
    s = u.shape[0]
    row = pl.BlockSpec((tm, D_MODEL), lambda i: (i, 0))
    const = lambda shape: pl.BlockSpec(shape, lambda i: (0,) * len(shape))
    halo = pl.BlockSpec((HALO, D_MODEL), lambda i: (jnp.maximum(i * (tm // HALO) - 1, 0), 0))
    return pl.pallas_call(
        functools.partial(_odd_out_kernel, tm=tm),
        grid=(s // tm,),
        in_specs=[row, halo, row, row, const((HALO, D_MODEL)), const((1, D_MODEL)), const((1, D_MODEL)),
                  const((1, D_MODEL)), const((D_MODEL, D_MODEL)), const((1, D_MODEL))],
        out_specs=row,
        out_shape=jax.ShapeDtypeStruct((s, D_MODEL), F32),
        scratch_shapes=[pltpu.VMEM((HALO + tm, D_MODEL), F32), pltpu.VMEM((tm, D_MODEL), F32)],
        compiler_params=pltpu.CompilerParams(dimension_semantics=("arbitrary",), vmem_limit_bytes=VMEM_LIMIT),
        name="odd_out",
    )(u, u, sg, x1, cw_pad, cb, lg, lb, w_out, nf)


def kernel(x, rel_bias, norm_even, w_in_even, w_gla_lr, b_gla_lr, gla_norm, w_out_even, norm_odd, w_in_odd,
           conv_w, conv_b, conv_ln_g, conv_ln_b, w_out_odd, norm_f):
    bsz, s, d = x.shape
    assert bsz == 1 and d == D_MODEL and s % GLA_BLOCK == 0
    assert norm_even.shape[0] == 1 and norm_odd.shape[0] == 1
    top_k = min(TOPK_MAX, s // 4)
    tm = 256
    x2d = x.reshape(s, d)

    w_pad = _even_in_weights(w_in_even[0])
    qa, ka, va, ga, qi, ki, wi, qb, kb, vb, gb, lr = _even_in(x2d, norm_even[0].reshape(1, d), w_pad, tm)
    mix_a = _dsa(qa, qi, wi, ga, ka, va, ki, _dsa_bias_table(rel_bias), top_k)

    wlr_pad = jnp.zeros((LANES, B_HEADS * LANES), F32)
    blr_pad = jnp.zeros((1, B_HEADS * LANES), F32)
    for h in range(B_HEADS):
        wlr_pad = wlr_pad.at[:GLA_RANK, h * LANES:h * LANES + B_DK].set(w_gla_lr[0][:, h * B_DK:(h + 1) * B_DK])
        blr_pad = blr_pad.at[0, h * LANES:h * LANES + B_DK].set(b_gla_lr[0][h * B_DK:(h + 1) * B_DK])
    mix_b = _gla(qb, kb, vb, gb, lr, wlr_pad, blr_pad, gla_norm[0].reshape(1, B_DV))

    x1, u, sg = _mid(x2d, mix_a, mix_b, w_out_even[0].astype(BF16), norm_odd[0].reshape(1, d),
                     w_in_odd[0].astype(BF16), tm)

    cw_pad = jnp.zeros((HALO, d), F32).at[:CONV_WIDTH].set(conv_w[0])
    out = _odd_out(u, sg, x1, cw_pad, conv_b[0].reshape(1, d), conv_ln_g[0].reshape(1, d),
                   conv_ln_b[0].reshape(1, d), w_out_odd[0].astype(BF16), norm_f.reshape(1, d), tm)
    return out.reshape(bsz, s, d)
```

```python
import functools
import math

import jax
import jax.numpy as jnp
import numpy as np
from jax import lax
from jax.experimental import pallas as pl
from jax.experimental.pallas import tpu as pltpu

D_MODEL = 1024
A_HEADS = 8
A_HEAD_DIM = 64
IDX_HEADS = 4
IDX_DIM = 64
TOPK_MAX = 256
B_HEADS = 4
B_DK = 64
B_DV = 128
GLA_RANK = 16
GLA_TAU = 16.0
GLA_CHUNK = 64
GLA_SUB = 16
CONV_WIDTH = 31
REL_BUCKETS = 32
REL_MAX_EXACT = 16
REL_MAX_DIST = 128
EPS = 1e-6

A_W = A_HEADS * A_HEAD_DIM
B_KW = B_HEADS * B_DK
B_VW = B_HEADS * B_DV

LANES = 128
SUBLANES = 8
VMEM_LIMIT = 56 * 1024 * 1024

F32 = jnp.float32
BF16 = jnp.bfloat16
NEG_INF = float("-inf")
INT_MIN = -(2 ** 31)
NEG_INF_BITS = int(np.array(-np.inf, np.float32).view(np.int32))
NEG_INF_KEY = NEG_INF_BITS ^ 0x7FFFFFFF
M_INIT = -1e30
LOG2E = math.log2(math.e)

C_QA = 0
C_KA = C_QA + A_HEADS * LANES
C_VA = C_KA + A_W
C_GA = C_VA + A_W
C_QI = C_GA + A_W
C_KI = C_QI + IDX_HEADS * LANES
C_WI = C_KI + LANES
C_QB = C_WI + LANES
C_KB = C_QB + B_HEADS * LANES
C_VB = C_KB + B_HEADS * LANES
C_GB = C_VB + B_VW
C_LR = C_GB + B_VW
C_END = C_LR + LANES


def _t5_bucket_np(dist):
    dist = np.maximum(dist, 0)
    d = np.maximum(dist, 1).astype(np.float32)
    large = REL_MAX_EXACT + (np.log(d / REL_MAX_EXACT) / math.log(REL_MAX_DIST / REL_MAX_EXACT)
                             * (REL_BUCKETS - REL_MAX_EXACT)).astype(np.int32)
    large = np.minimum(large, REL_BUCKETS - 1)
    return np.where(dist < REL_MAX_EXACT, dist, large)


def _silu(x):
    return x * jax.nn.sigmoid(x)


def _even_in_kernel(x_ref, g_ref, w_ref, qa_ref, ka_ref, va_ref, ga_ref, qi_ref, ki_ref, wi_ref,
                    qb_ref, kb_ref, vb_ref, gb_ref, lr_ref):
    x = x_ref[...]
    h = (x * lax.rsqrt(jnp.mean(x * x, axis=-1, keepdims=True) + EPS) * g_ref[...]).astype(BF16)

    def proj(lo, hi):
        return jnp.dot(h, w_ref[:, lo:hi], preferred_element_type=F32)

    qa_ref[...] = proj(C_QA, C_KA).astype(BF16)
    ka_ref[...] = proj(C_KA, C_VA).astype(BF16)
    va_ref[...] = proj(C_VA, C_GA).astype(BF16)
    ga_ref[...] = proj(C_GA, C_QI)
    qi_ref[...] = proj(C_QI, C_KI).astype(BF16)
    ki_ref[...] = proj(C_KI, C_WI).astype(BF16)
    wi_ref[...] = proj(C_WI, C_QB)
    qb_ref[...] = proj(C_QB, C_KB)
    kb_ref[...] = proj(C_KB, C_VB)
    vb_ref[...] = proj(C_VB, C_GB)
    gb_ref[...] = proj(C_GB, C_LR)
    lr_ref[...] = proj(C_LR, C_END)


def _even_in_weights(w):
    cols = np.cumsum([0, A_W, A_W, A_W, A_W, IDX_HEADS * IDX_DIM, IDX_DIM, IDX_HEADS,
                      B_KW, B_KW, B_VW, B_VW, GLA_RANK])
    (s_qa, s_ka, s_va, s_ga, s_qi, s_ki, s_wi, s_qb, s_kb, s_vb, s_gb, s_lr, _) = [int(c) for c in cols]
    out = jnp.zeros((D_MODEL, C_END), F32)
    a_scale = (A_HEAD_DIM ** -0.5) * LOG2E
    idx_scale = (IDX_DIM ** -0.5) * (IDX_HEADS ** -0.5)
    b_scale = B_DK ** -0.5
    for h in range(A_HEADS):
        dst = C_QA + h * LANES + (h % 2) * A_HEAD_DIM
        out = out.at[:, dst:dst + A_HEAD_DIM].set(w[:, s_qa + h * A_HEAD_DIM:s_qa + (h + 1) * A_HEAD_DIM] * a_scale)
    out = out.at[:, C_KA:C_KA + A_W].set(w[:, s_ka:s_ka + A_W])
    out = out.at[:, C_VA:C_VA + A_W].set(w[:, s_va:s_va + A_W])
    out = out.at[:, C_GA:C_GA + A_W].set(w[:, s_ga:s_ga + A_W])
    for h in range(IDX_HEADS):
        dst = C_QI + h * LANES
        out = out.at[:, dst:dst + IDX_DIM].set(w[:, s_qi + h * IDX_DIM:s_qi + (h + 1) * IDX_DIM])
    out = out.at[:, C_KI:C_KI + IDX_DIM].set(w[:, s_ki:s_ki + IDX_DIM])
    out = out.at[:, C_WI:C_WI + IDX_HEADS].set(w[:, s_wi:s_wi + IDX_HEADS] * idx_scale)
    for h in range(B_HEADS):
        dq = C_QB + h * LANES
        dk = C_KB + h * LANES
        out = out.at[:, dq:dq + B_DK].set(w[:, s_qb + h * B_DK:s_qb + (h + 1) * B_DK] * b_scale)
        out = out.at[:, dk:dk + B_DK].set(w[:, s_kb + h * B_DK:s_kb + (h + 1) * B_DK])
    out = out.at[:, C_VB:C_VB + B_VW].set(w[:, s_vb:s_vb + B_VW])
    out = out.at[:, C_GB:C_GB + B_VW].set(w[:, s_gb:s_gb + B_VW])
    out = out.at[:, C_LR:C_LR + GLA_RANK].set(w[:, s_lr:s_lr + GLA_RANK])
    return out.astype(BF16)


def _even_in(x2d, g, w_pad, tm):
    s = x2d.shape[0]
    widths = [(C_KA - C_QA, BF16), (A_W, BF16), (A_W, BF16), (A_W, F32), (C_KI - C_QI, BF16), (LANES, BF16),
              (LANES, F32), (C_KB - C_QB, F32), (C_VB - C_KB, F32), (B_VW, F32), (B_VW, F32), (LANES, F32)]
    return pl.pallas_call(
        _even_in_kernel,
        grid=(s // tm,),
        in_specs=[pl.BlockSpec((tm, D_MODEL), lambda i: (i, 0)),
                  pl.BlockSpec((1, D_MODEL), lambda i: (0, 0)),
                  pl.BlockSpec((D_MODEL, C_END), lambda i: (0, 0))],
        out_specs=[pl.BlockSpec((tm, wd), lambda i: (i, 0)) for wd, _ in widths],
        out_shape=[jax.ShapeDtypeStruct((s, wd), dt) for wd, dt in widths],
        compiler_params=pltpu.CompilerParams(dimension_semantics=("arbitrary",), vmem_limit_bytes=VMEM_LIMIT),
        name="even_in",
    )(x2d, g, w_pad)


DSA_Q = 128
DSA_KB = 128
DSA_WIDE = 4
DSA_NACC = 4
DSA_CELLS = 32


def _f32_key(x):
    bits = lax.bitcast_convert_type(x, jnp.int32)
    return bits ^ (lax.shift_right_arithmetic(bits, 31) & 0x7FFFFFFF)


def _fold_keys(x, op):
    n, w = x.shape
    group = DSA_NACC * SUBLANES
    if n % group == 0 and n > group:
        x = op(x.reshape(n // group, DSA_NACC, SUBLANES, w), axis=0)
        return op(x, axis=0)
    return op(x.reshape(n // SUBLANES, SUBLANES, w), axis=0)


def _dsa_kernel(qat_ref, qit_ref, wit_ref, ga_ref, ka_ref, vat_ref, ki_ref, btab_ref, tri_ref, o_ref,
                slab_ref, s_ref, acc_ref, m_ref, l_ref, *, top_k):
    q, kb, wide = DSA_Q, DSA_KB, DSA_WIDE * DSA_KB
    i = pl.program_id(0)
    nkb = i + 1
    nwide = jnp.maximum(i - 1, 0) // DSA_WIDE
    n0 = nwide * DSA_WIDE
    t_row = i * q + lax.broadcasted_iota(jnp.int32, (1, q), 1)
    key_off = lax.broadcasted_iota(jnp.int32, (kb, 1), 0)
    npair = A_HEADS // 2

    def wide_rows(jw):
        return pl.ds(pl.multiple_of(jw * wide, wide), wide)

    def narrow_rows(j):
        return pl.ds(pl.multiple_of(j * kb, kb), kb)

    def pipelined(first, last, produce, consume):
        def body(t, carry):
            consume(t, (t - first) % 2, carry)
            return produce(jnp.minimum(t + 1, last - 1), (t + 1 - first) % 2)

        lax.fori_loop(first, last, body, produce(first, 0))

    w_head = [wit_ref[h:h + 1, :] for h in range(IDX_HEADS)]

    def scores(rows):
        k_idx = ki_ref[rows, :]
        sc = None
        for h in range(IDX_HEADS):
            s = jnp.dot(k_idx, qit_ref[h * LANES:(h + 1) * LANES, :], preferred_element_type=F32)
            term = w_head[h] * jnp.maximum(s, 0.0)
            sc = term if sc is None else sc + term
        return sc

    def score_wide(jw, c):
        rows = wide_rows(jw)
        slab_ref[rows, :] = _f32_key(scores(rows))
        return c

    def score_narrow(j, c):
        rows = narrow_rows(j)
        sc = jnp.where(j * kb + key_off <= t_row, scores(rows), NEG_INF)
        slab_ref[rows, :] = _f32_key(sc)
        return c

    lax.fori_loop(0, nwide, score_wide, 0)
    lax.fori_loop(n0, nkb, score_narrow, 0)

    def count_ge(trial):
        def cnt(blk):
            return _fold_keys(jnp.where(blk >= trial, 1.0, 0.0), jnp.sum)

        acc = lax.fori_loop(0, nwide, lambda jw, a: a + cnt(slab_ref[wide_rows(jw), :]),
                            jnp.zeros((SUBLANES, q), F32))
        acc = lax.fori_loop(n0, nkb, lambda j, a: a + cnt(slab_ref[narrow_rows(j), :]), acc)
        return jnp.sum(acc, axis=0, keepdims=True).astype(jnp.int32)

    assert DSA_CELLS * SUBLANES >= top_k and wide % (DSA_CELLS * SUBLANES) == 0

    def cells_wide(jw, cm):
        blk = slab_ref[wide_rows(jw), :].reshape(wide // (DSA_CELLS * SUBLANES), DSA_CELLS, SUBLANES, q)
        return jnp.maximum(cm, jnp.max(blk, axis=0))

    def cells_narrow(j, cm):
        blk = slab_ref[narrow_rows(j), :].reshape(kb // SUBLANES, SUBLANES, q)
        return jnp.concatenate([jnp.maximum(cm[:kb // SUBLANES], blk), cm[kb // SUBLANES:]], axis=0)

    cell_max = lax.fori_loop(0, nwide, cells_wide, jnp.full((DSA_CELLS, SUBLANES, q), INT_MIN, jnp.int32))
    cell_max = lax.fori_loop(n0, nkb, cells_narrow, cell_max)
    hi_key = jnp.max(jnp.max(cell_max, axis=0), axis=0, keepdims=True)
    lo_key = jnp.min(jnp.min(cell_max, axis=0), axis=0, keepdims=True)
    same_sign = jnp.min(lo_key ^ hi_key) >= 0
    base = jnp.where(same_sign, lo_key, INT_MIN)
    span = jnp.where(same_sign, hi_key - lo_key, -1)
    nbits = jnp.where(same_sign, jnp.max(32 - lax.clz(span)), 32)

    def unsettled(state):
        b, _, cnt = state
        return jnp.logical_and(b < nbits, jnp.max(jnp.where(cnt == top_k, 0, 1)) > 0)

    def bisect(state):
        b, off, cnt = state
        trial_off = off + lax.shift_left(jnp.int32(1), nbits - 1 - b)
        in_span = (trial_off ^ INT_MIN) <= (span ^ INT_MIN)
        c = count_ge(base + trial_off)
        ok = jnp.logical_and(in_span, c >= top_k)
        return b + 1, jnp.where(ok, trial_off, off), jnp.where(ok, c, cnt)

    _, off, cnt_thr = lax.while_loop(unsettled, bisect, (jnp.int32(0), jnp.zeros((1, q), jnp.int32),
                                                         jnp.full((1, q), -1, jnp.int32)))
    thr = base + off

    def tie_check():
        tied = jnp.logical_and(count_ge(thr) > top_k, thr > NEG_INF_KEY)
        return jnp.max(jnp.where(tied, 1, 0))

    all_settled = jnp.max(jnp.where(cnt_thr == top_k, 0, 1)) == 0
    any_tie = lax.cond(all_settled, lambda: jnp.int32(0), tie_check) > 0

    thr_valid = jnp.maximum(thr, NEG_INF_KEY + 1)

    def mask_fast():
        def body(rows):
            slab_ref[rows, :] = jnp.where(slab_ref[rows, :] >= thr_valid, 0, NEG_INF_BITS)

        lax.fori_loop(0, nwide, lambda jw, c: body(wide_rows(jw)), None)
        lax.fori_loop(n0, nkb, lambda j, c: body(narrow_rows(j)), None)

    def mask_tied():
        need = (top_k - count_ge(thr + 1)).astype(F32)

        def body(j, seen):
            rows = narrow_rows(j)
            key = slab_ref[rows, :]
            eq = key == thr
            eq_f = jnp.where(eq, 1.0, 0.0)
            before = jnp.dot(tri_ref[...], eq_f.astype(BF16), preferred_element_type=F32)
            sel = jnp.logical_or(key > thr, jnp.logical_and(eq, seen + before < need))
            sel = jnp.logical_and(sel, key > NEG_INF_KEY)
            slab_ref[rows, :] = jnp.where(sel, 0, NEG_INF_BITS)
            return seen + jnp.sum(eq_f, axis=0, keepdims=True)

        lax.fori_loop(0, nkb, body, jnp.zeros((1, q), F32))

    lax.cond(any_tie, mask_tied, mask_fast)

    m_ref[...] = jnp.full(m_ref.shape, M_INIT, F32)
    l_ref[...] = jnp.zeros(l_ref.shape, F32)
    acc_ref[...] = jnp.zeros(acc_ref.shape, F32)

    def logits(rows, n, buf, rel):
        mask = lax.bitcast_convert_type(slab_ref[rows, :], F32)
        mask2 = jnp.concatenate([mask, mask], axis=1)
        tile_max = []
        for p in range(npair):
            ls = slice(p * LANES, (p + 1) * LANES)
            w_q = jnp.concatenate([qat_ref[(2 * p) * LANES:(2 * p + 1) * LANES, :],
                                   qat_ref[(2 * p + 1) * LANES:(2 * p + 2) * LANES, :]], axis=1)
            s = jnp.dot(ka_ref[rows, ls], w_q, preferred_element_type=F32) + mask2
            if rel is not None:
                s = s + btab_ref[rel, p]
            s_ref[buf, p, 0:n, :] = s
            tile_max.append(_fold_keys(s, jnp.max))
        return tuple(tile_max)

    def attend(rows, n, buf, tile_max):
        for p in range(npair):
            ls = slice(p * LANES, (p + 1) * LANES)
            m_old = m_ref[p]
            m_new = jnp.maximum(m_old, jnp.max(tile_max[p], axis=0, keepdims=True))
            alpha = jnp.exp2(m_old - m_new)
            pexp = jnp.exp2(s_ref[buf, p, 0:n, :] - m_new[0:1])
            m_ref[p] = m_new
            l_ref[p] = alpha * l_ref[p] + _fold_keys(pexp, jnp.sum)
            acc_ref[p] = alpha[0:1] * acc_ref[p] + jnp.dot(vat_ref[ls, rows], pexp.astype(BF16),
                                                           preferred_element_type=F32)

    @pl.when(nwide > 0)
    def _():
        pipelined(0, nwide, lambda jw, buf: logits(wide_rows(jw), wide, buf, None),
                  lambda jw, buf, tile_max: attend(wide_rows(jw), wide, buf, tile_max))

    pipelined(n0, nkb, lambda j, buf: logits(narrow_rows(j), kb, buf, jnp.minimum(i - j, 2)),
              lambda j, buf, tile_max: attend(narrow_rows(j), kb, buf, tile_max))

    for p in range(npair):
        o_t = acc_ref[p] / jnp.sum(l_ref[p], axis=0, keepdims=True)
        o_pair = jnp.concatenate([o_t[:A_HEAD_DIM, :q], o_t[A_HEAD_DIM:, q:]], axis=0).T
        ls = slice(p * LANES, (p + 1) * LANES)
        o_ref[:, ls] = (o_pair * _silu(ga_ref[:, ls])).astype(o_ref.dtype)


def _dsa_bias_table(rel_bias):
    ii = np.arange(DSA_Q)[:, None]
    jj = np.arange(DSA_KB)[None, :]
    buckets = np.stack([_t5_bucket_np(ii - jj), _t5_bucket_np(DSA_KB + ii - jj),
                        np.full((DSA_Q, DSA_KB), REL_BUCKETS - 1)])
    assert (_t5_bucket_np(np.arange(DSA_KB + 1, 4 * DSA_KB)) == REL_BUCKETS - 1).all()
    rb = rel_bias.astype(F32) * LOG2E
    tab = jnp.transpose(rb[buckets], (0, 3, 2, 1))
    tab = tab - rb[REL_BUCKETS - 1][None, :, None, None]
    tab = tab.reshape(3, A_HEADS // 2, 2, DSA_KB, DSA_Q)
    return jnp.transpose(tab, (0, 1, 3, 2, 4)).reshape(3, A_HEADS // 2, DSA_KB, 2 * DSA_Q)


def _dsa(qa, qi, wi, ga, ka, va, ki, btab, top_k):
    s = qa.shape[0]
    q, kb = DSA_Q, DSA_KB
    tri = jnp.asarray(np.tril(np.ones((kb, kb), np.float32), -1), BF16)
    qa_t, qi_t, va_t = qa.T, qi.T, va.T
    wi_t = wi[:, :SUBLANES].T
    whole = lambda shape: pl.BlockSpec(shape, lambda i: (0,) * len(shape), pipeline_mode=pl.Buffered(1))
    return pl.pallas_call(
        functools.partial(_dsa_kernel, top_k=top_k),
        grid=(s // q,),
        in_specs=[pl.BlockSpec((A_HEADS * LANES, q), lambda i: (0, i)),
                  pl.BlockSpec((IDX_HEADS * LANES, q), lambda i: (0, i)),
                  pl.BlockSpec((SUBLANES, q), lambda i: (0, i)),
                  pl.BlockSpec((q, A_W), lambda i: (i, 0)),
                  whole((s, A_W)), whole((A_W, s)), whole((s, LANES)),
                  whole((3, A_HEADS // 2, kb, 2 * q)), whole((kb, kb))],
        out_specs=pl.BlockSpec((q, A_W), lambda i: (i, 0)),
        out_shape=jax.ShapeDtypeStruct((s, A_W), BF16),
        scratch_shapes=[pltpu.VMEM((s, q), jnp.int32),
                        pltpu.VMEM((2, A_HEADS // 2, DSA_WIDE * kb, 2 * q), F32),
                        pltpu.VMEM((A_HEADS // 2, LANES, 2 * q), F32),
                        pltpu.VMEM((A_HEADS // 2, SUBLANES, 2 * q), F32),
                        pltpu.VMEM((A_HEADS // 2, SUBLANES, 2 * q), F32)],
        compiler_params=pltpu.CompilerParams(dimension_semantics=("arbitrary",), vmem_limit_bytes=VMEM_LIMIT),
        name="dsa",
    )(qa_t, qi_t, wi_t, ga, ka, va_t, ki, btab, tri)


GLA_BLOCK = 512


def _gla_kernel(qb_ref, kb_ref, vb_ref, gb_ref, lr_ref, wlr_ref, blr_ref, gn_ref, tril_ref, o_ref,
                state_ref, b_ref, att_ref):
    c, sub = GLA_CHUNK, GLA_SUB
    nsub = c // sub

    @pl.when(pl.program_id(0) == 0)
    def _():
        state_ref[...] = jnp.zeros(state_ref.shape, F32)

    dn_t = (((1,), (1,)), ((), ()))
    dn_0 = (((0,), (0,)), ((), ()))
    hp = lax.Precision.HIGHEST
    row_i = lax.broadcasted_iota(jnp.int32, (sub, sub), 0)
    col_j = lax.broadcasted_iota(jnp.int32, (sub, sub), 1)

    def chunk(ci, carry):
        r0 = pl.multiple_of(ci * c, c)
        z = jnp.dot(lr_ref[pl.ds(r0, c), :], wlr_ref[...], precision=hp, preferred_element_type=F32) + blr_ref[...]
        log_a = (jnp.minimum(z, 0.0) - jnp.log1p(jnp.exp(-jnp.abs(z)))) * (1.0 / GLA_TAU)
        b_ref[...] = jnp.dot(tril_ref[...], log_a, precision=hp, preferred_element_type=F32)
        for h in range(B_HEADS):
            ls = slice(h * LANES, (h + 1) * LANES)
            b = b_ref[:, ls]
            qh = qb_ref[pl.ds(r0, c), ls]
            kh = kb_ref[pl.ds(r0, c), ls]
            vh = vb_ref[pl.ds(r0, c), ls]
            state_t = state_ref[h]
            b_last = b[c - 1:c, :]
            o = lax.dot_general((qh * jnp.exp(b)).astype(BF16), state_t.astype(BF16), dn_t,
                                preferred_element_type=F32)
            for si in range(nsub):
                rs = slice(si * sub, (si + 1) * sub)
                b_i, q_i = b[rs], qh[rs]
                if si > 0:
                    ref_row = b[si * sub:si * sub + 1, :]
                    q_t = (q_i * jnp.exp(b_i - ref_row)).astype(BF16)
                    k_t = (kh[:si * sub] * jnp.exp(ref_row - b[:si * sub])).astype(BF16)
                    att_ref[rs, :si * sub] = lax.dot_general(q_t, k_t, dn_t, preferred_element_type=F32)
                k_i = kh[rs]
                diag = jnp.zeros((sub, sub), F32)
                for jl in range(sub):
                    diff = jnp.minimum(b_i - b_i[jl:jl + 1, :], 0.0)
                    col = jnp.sum(q_i * k_i[jl:jl + 1, :] * jnp.exp(diff), axis=-1, keepdims=True)
                    diag = jnp.where(col_j == jl, col, diag)
                att_ref[rs, rs] = jnp.where(col_j <= row_i, diag, 0.0)
                if si + 1 < nsub:
                    att_ref[rs, (si + 1) * sub:] = jnp.zeros((sub, c - (si + 1) * sub), F32)
            o = o + jnp.dot(att_ref[...].astype(BF16), vh.astype(BF16), preferred_element_type=F32)
            k_dec = (kh * jnp.exp(b_last - b)).astype(BF16)
            state_ref[h] = (jnp.exp(b_last) * state_t
                            + lax.dot_general(vh.astype(BF16), k_dec, dn_0, preferred_element_type=F32))
            o = o * lax.rsqrt(jnp.mean(o * o, axis=-1, keepdims=True) + EPS) * gn_ref[...]
            o_ref[pl.ds(r0, c), ls] = (o * _silu(gb_ref[pl.ds(r0, c), ls])).astype(o_ref.dtype)
        return carry

    lax.fori_loop(0, GLA_BLOCK // c, chunk, 0)


def _gla(qb, kb, vb, gb, lr, wlr_pad, blr_pad, gn):
    s = qb.shape[0]
    blk = min(GLA_BLOCK, s)
    assert blk == GLA_BLOCK and s % blk == 0
    tril = jnp.asarray(np.tril(np.ones((GLA_CHUNK, GLA_CHUNK), np.float32)))
    row = lambda wd: pl.BlockSpec((blk, wd), lambda i: (i, 0))
    const = lambda shape: pl.BlockSpec(shape, lambda i: (0,) * len(shape))
    return pl.pallas_call(
        _gla_kernel,
        grid=(s // blk,),
        in_specs=[row(B_HEADS * LANES), row(B_HEADS * LANES), row(B_VW), row(B_VW), row(LANES),
                  const((LANES, B_HEADS * LANES)), const((1, B_HEADS * LANES)), const((1, B_DV)),
                  const((GLA_CHUNK, GLA_CHUNK))],
        out_specs=row(B_VW),
        out_shape=jax.ShapeDtypeStruct((s, B_VW), BF16),
        scratch_shapes=[pltpu.VMEM((B_HEADS, LANES, B_DV), F32),
                        pltpu.VMEM((GLA_CHUNK, B_HEADS * LANES), F32),
                        pltpu.VMEM((GLA_CHUNK, GLA_CHUNK), F32)],
        compiler_params=pltpu.CompilerParams(dimension_semantics=("arbitrary",), vmem_limit_bytes=VMEM_LIMIT),
        name="gla",
    )(qb, kb, vb, gb, lr, wlr_pad, blr_pad, gn, tril)


def _mid_kernel(x_ref, ma_ref, mb_ref, wo_ref, g_ref, wi_ref, x1_ref, u_ref, sg_ref):
    x1 = (x_ref[...]
          + jnp.dot(ma_ref[...], wo_ref[:A_W, :], preferred_element_type=F32)
          + jnp.dot(mb_ref[...], wo_ref[A_W:, :], preferred_element_type=F32))
    x1_ref[...] = x1
    h = (x1 * lax.rsqrt(jnp.mean(x1 * x1, axis=-1, keepdims=True) + EPS) * g_ref[...]).astype(BF16)
    val = jnp.dot(h, wi_ref[:, :D_MODEL], preferred_element_type=F32)
    glu = jnp.dot(h, wi_ref[:, D_MODEL:2 * D_MODEL], preferred_element_type=F32)
    gate = jnp.dot(h, wi_ref[:, 2 * D_MODEL:], preferred_element_type=F32)
    u_ref[...] = val * jax.nn.sigmoid(glu)
    sg_ref[...] = _silu(gate)


def _mid(x2d, mix_a, mix_b, w_out, g, w_in, tm):
    s = x2d.shape[0]
    row = lambda wd: pl.BlockSpec((tm, wd), lambda i: (i, 0))
    const = lambda shape: pl.BlockSpec(shape, lambda i: (0,) * len(shape))
    return pl.pallas_call(
        _mid_kernel,
        grid=(s // tm,),
        in_specs=[row(D_MODEL), row(A_W), row(B_VW), const((A_W + B_VW, D_MODEL)), const((1, D_MODEL)),
                  const((D_MODEL, 3 * D_MODEL))],
        out_specs=[row(D_MODEL), row(D_MODEL), row(D_MODEL)],
        out_shape=[jax.ShapeDtypeStruct((s, D_MODEL), F32)] * 3,
        compiler_params=pltpu.CompilerParams(dimension_semantics=("arbitrary",), vmem_limit_bytes=VMEM_LIMIT),
        name="mid",
    )(x2d, mix_a, mix_b, w_out, g, w_in)


HALO = 32


def _odd_out_kernel(u_ref, halo_ref, sg_ref, x1_ref, cw_ref, cb_ref, lg_ref, lb_ref, wo_ref, nf_ref, o_ref,
                    ubuf_ref, y_ref, *, tm):
    i = pl.program_id(0)
    ubuf_ref[HALO:, :] = u_ref[...]
    ubuf_ref[:HALO, :] = jnp.where(i > 0, halo_ref[...], 0.0)
    base = HALO - (CONV_WIDTH - 1)
    for cs in range(D_MODEL // LANES):
        ls = slice(cs * LANES, (cs + 1) * LANES)
        acc = jnp.zeros((tm, LANES), F32)
        for k in range(CONV_WIDTH):
            acc = acc + ubuf_ref[base + k:base + k + tm, ls] * cw_ref[k:k + 1, ls]
        y_ref[:, ls] = acc + cb_ref[:, ls]
    y = y_ref[...]
    mu = jnp.mean(y, axis=-1, keepdims=True)
    yc = y - mu
    var = jnp.mean(yc * yc, axis=-1, keepdims=True)
    yn = yc * lax.rsqrt(var + EPS) * lg_ref[...] + lb_ref[...]
    mixed = (_silu(yn) * sg_ref[...]).astype(BF16)
    x2 = x1_ref[...] + jnp.dot(mixed, wo_ref[...], preferred_element_type=F32)
    o_ref[...] = x2 * lax.rsqrt(jnp.mean(x2 * x2, axis=-1, keepdims=True) + EPS) * nf_ref[...]


def _odd_out(u, sg, x1, cw_pad, cb, lg, lb, w_out, nf, tm):
    s = u.shape[0]
    row = pl.BlockSpec((tm, D_MODEL), lambda i: (i, 0))
    const = lambda shape: pl.BlockSpec(shape, lambda i: (0,) * len(shape))
    halo = pl.BlockSpec((HALO, D_MODEL), lambda i: (jnp.maximum(i * (tm // HALO) - 1, 0), 0))
    return pl.pallas_call(
        functools.partial(_odd_out_kernel, tm=tm),
        grid=(s // tm,),
        in_specs=[row, halo, row, row, const((HALO, D_MODEL)), const((1, D_MODEL)), const((1, D_MODEL)),
                  const((1, D_MODEL)), const((D_MODEL, D_MODEL)), const((1, D_MODEL))],
        out_specs=row,
        out_shape=jax.ShapeDtypeStruct((s, D_MODEL), F32),
        scratch_shapes=[pltpu.VMEM((HALO + tm, D_MODEL), F32), pltpu.VMEM((tm, D_MODEL), F32)],
        compiler_params=pltpu.CompilerParams(dimension_semantics=("arbitrary",), vmem_limit_bytes=VMEM_LIMIT),
        name="odd_out",
    )(u, u, sg, x1, cw_pad, cb, lg, lb, w_out, nf)


def kernel(x, rel_bias, norm_even, w_in_even, w_gla_lr, b_gla_lr, gla_norm, w_out_even, norm_odd, w_in_odd,
           conv_w, conv_b, conv_ln_g, conv_ln_b, w_out_odd, norm_f):
    bsz, s, d = x.shape
    assert bsz == 1 and d == D_MODEL and s % GLA_BLOCK == 0
    assert norm_even.shape[0] == 1 and norm_odd.shape[0] == 1
    top_k = min(TOPK_MAX, s // 4)
    tm = 256
    x2d = x.reshape(s, d)

    w_pad = _even_in_weights(w_in_even[0])
    qa, ka, va, ga, qi, ki, wi, qb, kb, vb, gb, lr = _even_in(x2d, norm_even[0].reshape(1, d), w_pad, tm)
    mix_a = _dsa(qa, qi, wi, ga, ka, va, ki, _dsa_bias_table(rel_bias), top_k)

    wlr_pad = jnp.zeros((LANES, B_HEADS * LANES), F32)
    blr_pad = jnp.zeros((1, B_HEADS * LANES), F32)
    for h in range(B_HEADS):
        wlr_pad = wlr_pad.at[:GLA_RANK, h * LANES:h * LANES + B_DK].set(w_gla_lr[0][:, h * B_DK:(h + 1) * B_DK])
        blr_pad = blr_pad.at[0, h * LANES:h * LANES + B_DK].set(b_gla_lr[0][h * B_DK:(h + 1) * B_DK])
    mix_b = _gla(qb, kb, vb, gb, lr, wlr_pad, blr_pad, gla_norm[0].reshape(1, B_DV))

    x1, u, sg = _mid(x2d, mix_a, mix_b, w_out_even[0].astype(BF16), norm_odd[0].reshape(1, d),
                     w_in_odd[0].astype(BF16), tm)

    cw_pad = jnp.zeros((HALO, d), F32).at[:CONV_WIDTH].set(conv_w[0])
    out = _odd_out(u, sg, x1, cw_pad, conv_b[0].reshape(1, d), conv_ln_g[0].reshape(1, d),
                   conv_ln_b[0].reshape(1, d), w_out_odd[0].astype(BF16), norm_f.reshape(1, d), tm)
    return out.reshape(bsz, s, d)
```

```python
import functools
import math

import jax
import jax.numpy as jnp
import numpy as np
from jax import lax
from jax.experimental import pallas as pl
from jax.experimental.pallas import tpu as pltpu

D_MODEL = 1024
A_HEADS = 8
A_HEAD_DIM = 64
IDX_HEADS = 4
IDX_DIM = 64
TOPK_MAX = 256
B_HEADS = 4
B_DK = 64
B_DV = 128
GLA_RANK = 16
GLA_TAU = 16.0
GLA_CHUNK = 64
GLA_SUB = 16
CONV_WIDTH = 31
REL_BUCKETS = 32
REL_MAX_EXACT = 16
REL_MAX_DIST = 128
EPS = 1e-6

A_W = A_HEADS * A_HEAD_DIM
B_KW = B_HEADS * B_DK
B_VW = B_HEADS * B_DV

LANES = 128
SUBLANES = 8
VMEM_LIMIT = 56 * 1024 * 1024

F32 = jnp.float32
BF16 = jnp.bfloat16
NEG_INF = float("-inf")
INT_MIN = -(2 ** 31)
NEG_INF_BITS = int(np.array(-np.inf, np.float32).view(np.int32))
NEG_INF_KEY = NEG_INF_BITS ^ 0x7FFFFFFF
M_INIT = -1e30
LOG2E = math.log2(math.e)

C_QA = 0
C_KA = C_QA + A_HEADS * LANES
C_VA = C_KA + A_W
C_GA = C_VA + A_W
C_QI = C_GA + A_W
C_KI = C_QI + IDX_HEADS * LANES
C_WI = C_KI + LANES
C_QB = C_WI + LANES
C_KB = C_QB + B_HEADS * LANES
C_VB = C_KB + B_HEADS * LANES
C_GB = C_VB + B_VW
C_LR = C_GB + B_VW
C_END = C_LR + LANES


def _t5_bucket_np(dist):
    dist = np.maximum(dist, 0)
    d = np.maximum(dist, 1).astype(np.float32)
    large = REL_MAX_EXACT + (np.log(d / REL_MAX_EXACT) / math.log(REL_MAX_DIST / REL_MAX_EXACT)
                             * (REL_BUCKETS - REL_MAX_EXACT)).astype(np.int32)
    large = np.minimum(large, REL_BUCKETS - 1)
    return np.where(dist < REL_MAX_EXACT, dist, large)


def _silu(x):
    return x * jax.nn.sigmoid(x)


def _even_in_kernel(x_ref, g_ref, wt_ref, qat_ref, ka_ref, vat_ref, ga_ref, qit_ref, ki_ref, wit_ref,
                    qb_ref, kb_ref, vb_ref, gb_ref, lr_ref):
    x = x_ref[...]
    h = (x * lax.rsqrt(jnp.mean(x * x, axis=-1, keepdims=True) + EPS) * g_ref[...]).astype(BF16)
    dn_t = (((1,), (1,)), ((), ()))

    def proj(lo, hi):
        return lax.dot_general(h, wt_ref[lo:hi, :], dn_t, preferred_element_type=F32)

    def proj_t(lo, hi):
        return lax.dot_general(wt_ref[lo:hi, :], h, dn_t, preferred_element_type=F32)

    qat_ref[...] = proj_t(C_QA, C_KA).astype(BF16)
    ka_ref[...] = proj(C_KA, C_VA).astype(BF16)
    vat_ref[...] = proj_t(C_VA, C_GA).astype(BF16)
    ga_ref[...] = proj(C_GA, C_QI)
    qit_ref[...] = proj_t(C_QI, C_KI).astype(BF16)
    ki_ref[...] = proj(C_KI, C_WI).astype(BF16)
    wit_ref[...] = proj_t(C_WI, C_WI + SUBLANES)
    qb_ref[...] = proj(C_QB, C_KB)
    kb_ref[...] = proj(C_KB, C_VB)
    vb_ref[...] = proj(C_VB, C_GB)
    gb_ref[...] = proj(C_GB, C_LR)
    lr_ref[...] = proj(C_LR, C_END)


def _even_in_layout():
    cols = np.cumsum([0, A_W, A_W, A_W, A_W, IDX_HEADS * IDX_DIM, IDX_DIM, IDX_HEADS,
                      B_KW, B_KW, B_VW, B_VW, GLA_RANK])
    (s_qa, s_ka, s_va, s_ga, s_qi, s_ki, s_wi, s_qb, s_kb, s_vb, s_gb, s_lr, _) = [int(c) for c in cols]
    src = np.full((C_END,), -1, np.int32)
    scale = np.ones((C_END,), np.float32)

    def put(dst, start, width, sc=1.0):
        src[dst:dst + width] = np.arange(start, start + width)
        scale[dst:dst + width] = sc

    for h in range(A_HEADS):
        put(C_QA + h * LANES + (h % 2) * A_HEAD_DIM, s_qa + h * A_HEAD_DIM, A_HEAD_DIM, (A_HEAD_DIM ** -0.5) * LOG2E)
    put(C_KA, s_ka, A_W)
    put(C_VA, s_va, A_W)
    put(C_GA, s_ga, A_W)
    for h in range(IDX_HEADS):
        put(C_QI + h * LANES, s_qi + h * IDX_DIM, IDX_DIM)
    put(C_KI, s_ki, IDX_DIM)
    put(C_WI, s_wi, IDX_HEADS, (IDX_DIM ** -0.5) * (IDX_HEADS ** -0.5))
    for h in range(B_HEADS):
        put(C_QB + h * LANES, s_qb + h * B_DK, B_DK, B_DK ** -0.5)
        put(C_KB + h * LANES, s_kb + h * B_DK, B_DK)
    put(C_VB, s_vb, B_VW)
    put(C_GB, s_gb, B_VW)
    put(C_LR, s_lr, GLA_RANK)
    return src, scale


def _even_in_weights(w):
    src, scale = _even_in_layout()
    rows = jnp.take(w.T, jnp.asarray(np.maximum(src, 0)), axis=0)
    return (rows * jnp.asarray(np.where(src >= 0, scale, 0.0))[:, None]).astype(BF16)


def _even_in(x2d, g, wt_pad, tm):
    s = x2d.shape[0]
    row = lambda wd, dt: (pl.BlockSpec((tm, wd), lambda i: (i, 0)), jax.ShapeDtypeStruct((s, wd), dt))
    col = lambda ht, dt: (pl.BlockSpec((ht, tm), lambda i: (0, i)), jax.ShapeDtypeStruct((ht, s), dt))
    outs = [col(C_KA - C_QA, BF16), row(A_W, BF16), col(A_W, BF16), row(A_W, F32), col(C_KI - C_QI, BF16),
            row(LANES, BF16), col(SUBLANES, F32), row(C_KB - C_QB, F32), row(C_VB - C_KB, F32), row(B_VW, F32),
            row(B_VW, F32), row(LANES, F32)]
    return pl.pallas_call(
        _even_in_kernel,
        grid=(s // tm,),
        in_specs=[pl.BlockSpec((tm, D_MODEL), lambda i: (i, 0)),
                  pl.BlockSpec((1, D_MODEL), lambda i: (0, 0)),
                  pl.BlockSpec((C_END, D_MODEL), lambda i: (0, 0))],
        out_specs=[spec for spec, _ in outs],
        out_shape=[shape for _, shape in outs],
        compiler_params=pltpu.CompilerParams(dimension_semantics=("arbitrary",), vmem_limit_bytes=VMEM_LIMIT),
        name="even_in",
    )(x2d, g, wt_pad)


DSA_Q = 128
DSA_KB = 128
DSA_WIDE = 4
DSA_NACC = 4
DSA_CELLS = 32


def _f32_key(x):
    bits = lax.bitcast_convert_type(x, jnp.int32)
    return bits ^ (lax.shift_right_arithmetic(bits, 31) & 0x7FFFFFFF)


def _fold_keys(x, op):
    n, w = x.shape
    group = DSA_NACC * SUBLANES
    if n % group == 0 and n > group:
        x = op(x.reshape(n // group, DSA_NACC, SUBLANES, w), axis=0)
        return op(x, axis=0)
    return op(x.reshape(n // SUBLANES, SUBLANES, w), axis=0)


def _dsa_kernel(qat_ref, qit_ref, wit_ref, ga_ref, ka_ref, vat_ref, ki_ref, btab_ref, tri_ref, o_ref,
                slab_ref, s_ref, acc_ref, m_ref, l_ref, *, top_k):
    q, kb, wide = DSA_Q, DSA_KB, DSA_WIDE * DSA_KB
    i = pl.program_id(0)
    nkb = i + 1
    nwide = jnp.maximum(i - 1, 0) // DSA_WIDE
    n0 = nwide * DSA_WIDE
    t_row = i * q + lax.broadcasted_iota(jnp.int32, (1, q), 1)
    key_off = lax.broadcasted_iota(jnp.int32, (kb, 1), 0)
    npair = A_HEADS // 2

    def wide_rows(jw):
        return pl.ds(pl.multiple_of(jw * wide, wide), wide)

    def narrow_rows(j):
        return pl.ds(pl.multiple_of(j * kb, kb), kb)

    def pipelined(first, last, produce, consume):
        def body(t, carry):
            consume(t, (t - first) % 2, carry)
            return produce(jnp.minimum(t + 1, last - 1), (t + 1 - first) % 2)

        lax.fori_loop(first, last, body, produce(first, 0))

    w_head = [wit_ref[h:h + 1, :] for h in range(IDX_HEADS)]

    def scores(rows):
        k_idx = ki_ref[rows, :]
        sc = None
        for h in range(IDX_HEADS):
            s = jnp.dot(k_idx, qit_ref[h * LANES:(h + 1) * LANES, :], preferred_element_type=F32)
            term = w_head[h] * jnp.maximum(s, 0.0)
            sc = term if sc is None else sc + term
        return sc

    def score_wide(jw, c):
        rows = wide_rows(jw)
        slab_ref[rows, :] = _f32_key(scores(rows))
        return c

    def score_narrow(j, c):
        rows = narrow_rows(j)
        sc = jnp.where(j * kb + key_off <= t_row, scores(rows), NEG_INF)
        slab_ref[rows, :] = _f32_key(sc)
        return c

    lax.fori_loop(0, nwide, score_wide, 0)
    lax.fori_loop(n0, nkb, score_narrow, 0)

    def count_ge(trial):
        def cnt(blk):
            return _fold_keys(jnp.where(blk >= trial, 1.0, 0.0), jnp.sum)

        acc = lax.fori_loop(0, nwide, lambda jw, a: a + cnt(slab_ref[wide_rows(jw), :]),
                            jnp.zeros((SUBLANES, q), F32))
        acc = lax.fori_loop(n0, nkb, lambda j, a: a + cnt(slab_ref[narrow_rows(j), :]), acc)
        return jnp.sum(acc, axis=0, keepdims=True).astype(jnp.int32)

    assert DSA_CELLS * SUBLANES >= top_k and wide % (DSA_CELLS * SUBLANES) == 0

    def cells_wide(jw, cm):
        blk = slab_ref[wide_rows(jw), :].reshape(wide // (DSA_CELLS * SUBLANES), DSA_CELLS, SUBLANES, q)
        return jnp.maximum(cm, jnp.max(blk, axis=0))

    def cells_narrow(j, cm):
        blk = slab_ref[narrow_rows(j), :].reshape(kb // SUBLANES, SUBLANES, q)
        return jnp.concatenate([jnp.maximum(cm[:kb // SUBLANES], blk), cm[kb // SUBLANES:]], axis=0)

    cell_max = lax.fori_loop(0, nwide, cells_wide, jnp.full((DSA_CELLS, SUBLANES, q), INT_MIN, jnp.int32))
    cell_max = lax.fori_loop(n0, nkb, cells_narrow, cell_max)
    hi_key = jnp.max(jnp.max(cell_max, axis=0), axis=0, keepdims=True)
    lo_key = jnp.min(jnp.min(cell_max, axis=0), axis=0, keepdims=True)
    same_sign = jnp.min(lo_key ^ hi_key) >= 0
    base = jnp.where(same_sign, lo_key, INT_MIN)
    span = jnp.where(same_sign, hi_key - lo_key, -1)
    nbits = jnp.where(same_sign, jnp.max(32 - lax.clz(span)), 32)

    def bisect(b, off):
        trial_off = off + lax.shift_left(jnp.int32(1), nbits - 1 - b)
        in_span = (trial_off ^ INT_MIN) <= (span ^ INT_MIN)
        ok = jnp.logical_and(in_span, count_ge(base + trial_off) >= top_k)
        return jnp.where(ok, trial_off, off)

    thr = base + lax.fori_loop(0, nbits, bisect, jnp.zeros((1, q), jnp.int32))
    tied = jnp.logical_and(count_ge(thr) > top_k, thr > NEG_INF_KEY)
    any_tie = jnp.max(jnp.where(tied, 1, 0)) > 0

    thr_valid = jnp.maximum(thr, NEG_INF_KEY + 1)

    def mask_fast():
        def body(rows):
            slab_ref[rows, :] = jnp.where(slab_ref[rows, :] >= thr_valid, 0, NEG_INF_BITS)

        lax.fori_loop(0, nwide, lambda jw, c: body(wide_rows(jw)), None)
        lax.fori_loop(n0, nkb, lambda j, c: body(narrow_rows(j)), None)

    def mask_tied():
        need = (top_k - count_ge(thr + 1)).astype(F32)

        def body(j, seen):
            rows = narrow_rows(j)
            key = slab_ref[rows, :]
            eq = key == thr
            eq_f = jnp.where(eq, 1.0, 0.0)
            before = jnp.dot(tri_ref[...], eq_f.astype(BF16), preferred_element_type=F32)
            sel = jnp.logical_or(key > thr, jnp.logical_and(eq, seen + before < need))
            sel = jnp.logical_and(sel, key > NEG_INF_KEY)
            slab_ref[rows, :] = jnp.where(sel, 0, NEG_INF_BITS)
            return seen + jnp.sum(eq_f, axis=0, keepdims=True)

        lax.fori_loop(0, nkb, body, jnp.zeros((1, q), F32))

    lax.cond(any_tie, mask_tied, mask_fast)

    m_ref[...] = jnp.full(m_ref.shape, M_INIT, F32)
    l_ref[...] = jnp.zeros(l_ref.shape, F32)
    acc_ref[...] = jnp.zeros(acc_ref.shape, F32)

    def logits(rows, n, buf, rel):
        mask = lax.bitcast_convert_type(slab_ref[rows, :], F32)
        mask2 = jnp.concatenate([mask, mask], axis=1)
        tile_max = []
        for p in range(npair):
            ls = slice(p * LANES, (p + 1) * LANES)
            w_q = jnp.concatenate([qat_ref[(2 * p) * LANES:(2 * p + 1) * LANES, :],
                                   qat_ref[(2 * p + 1) * LANES:(2 * p + 2) * LANES, :]], axis=1)
            s = jnp.dot(ka_ref[rows, ls], w_q, preferred_element_type=F32) + mask2
            if rel is not None:
                s = s + btab_ref[rel, p]
            s_ref[buf, p, 0:n, :] = s
            tile_max.append(_fold_keys(s, jnp.max))
        return tuple(tile_max)

    def attend(rows, n, buf, tile_max):
        for p in range(npair):
            ls = slice(p * LANES, (p + 1) * LANES)
            m_old = m_ref[p]
            m_new = jnp.maximum(m_old, jnp.max(tile_max[p], axis=0, keepdims=True))
            alpha = jnp.exp2(m_old - m_new)
            pexp = jnp.exp2(s_ref[buf, p, 0:n, :] - m_new[0:1])
            m_ref[p] = m_new
            l_ref[p] = alpha * l_ref[p] + _fold_keys(pexp, jnp.sum)
            acc_ref[p] = alpha[0:1] * acc_ref[p] + jnp.dot(vat_ref[ls, rows], pexp.astype(BF16),
                                                           preferred_element_type=F32)

    @pl.when(nwide > 0)
    def _():
        pipelined(0, nwide, lambda jw, buf: logits(wide_rows(jw), wide, buf, None),
                  lambda jw, buf, tile_max: attend(wide_rows(jw), wide, buf, tile_max))

    pipelined(n0, nkb, lambda j, buf: logits(narrow_rows(j), kb, buf, jnp.minimum(i - j, 2)),
              lambda j, buf, tile_max: attend(narrow_rows(j), kb, buf, tile_max))

    for p in range(npair):
        o_t = acc_ref[p] / jnp.sum(l_ref[p], axis=0, keepdims=True)
        o_pair = jnp.concatenate([o_t[:A_HEAD_DIM, :q], o_t[A_HEAD_DIM:, q:]], axis=0).T
        ls = slice(p * LANES, (p + 1) * LANES)
        o_ref[:, ls] = (o_pair * _silu(ga_ref[:, ls])).astype(o_ref.dtype)


def _dsa_bias_table(rel_bias):
    ii = np.arange(DSA_Q)[:, None]
    jj = np.arange(DSA_KB)[None, :]
    buckets = np.stack([_t5_bucket_np(ii - jj), _t5_bucket_np(DSA_KB + ii - jj),
                        np.full((DSA_Q, DSA_KB), REL_BUCKETS - 1)])
    assert (_t5_bucket_np(np.arange(DSA_KB + 1, 4 * DSA_KB)) == REL_BUCKETS - 1).all()
    rb = rel_bias.astype(F32) * LOG2E
    rb = rb - rb[REL_BUCKETS - 1]
    onehot = np.zeros((3, DSA_KB, DSA_Q, REL_BUCKETS), np.float32)
    np.put_along_axis(onehot, np.transpose(buckets, (0, 2, 1))[..., None], 1.0, axis=-1)
    tab = jnp.einsum('tsrb,bh->thsr', jnp.asarray(onehot), rb, precision=lax.Precision.HIGHEST)
    tab = tab.reshape(3, A_HEADS // 2, 2, DSA_KB, DSA_Q)
    return jnp.transpose(tab, (0, 1, 3, 2, 4)).reshape(3, A_HEADS // 2, DSA_KB, 2 * DSA_Q)


def _dsa(qa_t, qi_t, wi_t, ga, ka, va_t, ki, btab, top_k):
    s = ga.shape[0]
    q, kb = DSA_Q, DSA_KB
    tri = jnp.asarray(np.tril(np.ones((kb, kb), np.float32), -1), BF16)
    whole = lambda shape: pl.BlockSpec(shape, lambda i: (0,) * len(shape), pipeline_mode=pl.Buffered(1))
    return pl.pallas_call(
        functools.partial(_dsa_kernel, top_k=top_k),
        grid=(s // q,),
        in_specs=[pl.BlockSpec((A_HEADS * LANES, q), lambda i: (0, i)),
                  pl.BlockSpec((IDX_HEADS * LANES, q), lambda i: (0, i)),
                  pl.BlockSpec((SUBLANES, q), lambda i: (0, i)),
                  pl.BlockSpec((q, A_W), lambda i: (i, 0)),
                  whole((s, A_W)), whole((A_W, s)), whole((s, LANES)),
                  whole((3, A_HEADS // 2, kb, 2 * q)), whole((kb, kb))],
        out_specs=pl.BlockSpec((q, A_W), lambda i: (i, 0)),
        out_shape=jax.ShapeDtypeStruct((s, A_W), BF16),
        scratch_shapes=[pltpu.VMEM((s, q), jnp.int32),
                        pltpu.VMEM((2, A_HEADS // 2, DSA_WIDE * kb, 2 * q), F32),
                        pltpu.VMEM((A_HEADS // 2, LANES, 2 * q), F32),
                        pltpu.VMEM((A_HEADS // 2, SUBLANES, 2 * q), F32),
                        pltpu.VMEM((A_HEADS // 2, SUBLANES, 2 * q), F32)],
        compiler_params=pltpu.CompilerParams(dimension_semantics=("arbitrary",), vmem_limit_bytes=VMEM_LIMIT),
        name="dsa",
    )(qa_t, qi_t, wi_t, ga, ka, va_t, ki, btab, tri)


GLA_BLOCK = 512


def _gla_kernel(qb_ref, kb_ref, vb_ref, gb_ref, lr_ref, wlr_ref, blr_ref, gn_ref, tril_ref, o_ref,
                state_ref, b_ref, att_ref):
    c, sub = GLA_CHUNK, GLA_SUB
    nsub = c // sub

    @pl.when(pl.program_id(0) == 0)
    def _():
        state_ref[...] = jnp.zeros(state_ref.shape, F32)

    dn_t = (((1,), (1,)), ((), ()))
    dn_0 = (((0,), (0,)), ((), ()))
    hp = lax.Precision.HIGHEST
    row_i = lax.broadcasted_iota(jnp.int32, (sub, sub), 0)
    col_j = lax.broadcasted_iota(jnp.int32, (sub, sub), 1)

    def chunk(ci, carry):
        r0 = pl.multiple_of(ci * c, c)
        z = jnp.dot(lr_ref[pl.ds(r0, c), :], wlr_ref[...], precision=hp, preferred_element_type=F32) + blr_ref[...]
        log_a = (jnp.minimum(z, 0.0) - jnp.log1p(jnp.exp(-jnp.abs(z)))) * (1.0 / GLA_TAU)
        b_ref[...] = jnp.dot(tril_ref[...], log_a, precision=hp, preferred_element_type=F32)
        for h in range(B_HEADS):
            ls = slice(h * LANES, (h + 1) * LANES)
            b = b_ref[:, ls]
            qh = qb_ref[pl.ds(r0, c), ls]
            kh = kb_ref[pl.ds(r0, c), ls]
            vh = vb_ref[pl.ds(r0, c), ls]
            state_t = state_ref[h]
            b_last = b[c - 1:c, :]
            o = lax.dot_general((qh * jnp.exp(b)).astype(BF16), state_t.astype(BF16), dn_t,
                                preferred_element_type=F32)
            for si in range(nsub):
                rs = slice(si * sub, (si + 1) * sub)
                b_i, q_i = b[rs], qh[rs]
                if si > 0:
                    ref_row = b[si * sub:si * sub + 1, :]
                    q_t = (q_i * jnp.exp(b_i - ref_row)).astype(BF16)
                    k_t = (kh[:si * sub] * jnp.exp(ref_row - b[:si * sub])).astype(BF16)
                    att_ref[rs, :si * sub] = lax.dot_general(q_t, k_t, dn_t, preferred_element_type=F32)
                k_i = kh[rs]
                diag = jnp.zeros((sub, sub), F32)
                for jl in range(sub):
                    diff = jnp.minimum(b_i - b_i[jl:jl + 1, :], 0.0)
                    col = jnp.sum(q_i * k_i[jl:jl + 1, :] * jnp.exp(diff), axis=-1, keepdims=True)
                    diag = jnp.where(col_j == jl, col, diag)
                att_ref[rs, rs] = jnp.where(col_j <= row_i, diag, 0.0)
                if si + 1 < nsub:
                    att_ref[rs, (si + 1) * sub:] = jnp.zeros((sub, c - (si + 1) * sub), F32)
            o = o + jnp.dot(att_ref[...].astype(BF16), vh.astype(BF16), preferred_element_type=F32)
            k_dec = (kh * jnp.exp(b_last - b)).astype(BF16)
            state_ref[h] = (jnp.exp(b_last) * state_t
                            + lax.dot_general(vh.astype(BF16), k_dec, dn_0, preferred_element_type=F32))
            o = o * lax.rsqrt(jnp.mean(o * o, axis=-1, keepdims=True) + EPS) * gn_ref[...]
            o_ref[pl.ds(r0, c), ls] = (o * _silu(gb_ref[pl.ds(r0, c), ls])).astype(o_ref.dtype)
        return carry

    lax.fori_loop(0, GLA_BLOCK // c, chunk, 0)


def _gla(qb, kb, vb, gb, lr, wlr_pad, blr_pad, gn):
    s = qb.shape[0]
    blk = min(GLA_BLOCK, s)
    assert blk == GLA_BLOCK and s % blk == 0
    tril = jnp.asarray(np.tril(np.ones((GLA_CHUNK, GLA_CHUNK), np.float32)))
    row = lambda wd: pl.BlockSpec((blk, wd), lambda i: (i, 0))
    const = lambda shape: pl.BlockSpec(shape, lambda i: (0,) * len(shape))
    return pl.pallas_call(
        _gla_kernel,
        grid=(s // blk,),
        in_specs=[row(B_HEADS * LANES), row(B_HEADS * LANES), row(B_VW), row(B_VW), row(LANES),
                  const((LANES, B_HEADS * LANES)), const((1, B_HEADS * LANES)), const((1, B_DV)),
                  const((GLA_CHUNK, GLA_CHUNK))],
        out_specs=row(B_VW),
        out_shape=jax.ShapeDtypeStruct((s, B_VW), BF16),
        scratch_shapes=[pltpu.VMEM((B_HEADS, LANES, B_DV), F32),
                        pltpu.VMEM((GLA_CHUNK, B_HEADS * LANES), F32),
                        pltpu.VMEM((GLA_CHUNK, GLA_CHUNK), F32)],
        compiler_params=pltpu.CompilerParams(dimension_semantics=("arbitrary",), vmem_limit_bytes=VMEM_LIMIT),
        name="gla",
    )(qb, kb, vb, gb, lr, wlr_pad, blr_pad, gn, tril)


def _mid_kernel(x_ref, ma_ref, mb_ref, wo_ref, g_ref, wi_ref, x1_ref, u_ref, sg_ref):
    x1 = (x_ref[...]
          + jnp.dot(ma_ref[...], wo_ref[:A_W, :], preferred_element_type=F32)
          + jnp.dot(mb_ref[...], wo_ref[A_W:, :], preferred_element_type=F32))
    x1_ref[...] = x1
    h = (x1 * lax.rsqrt(jnp.mean(x1 * x1, axis=-1, keepdims=True) + EPS) * g_ref[...]).astype(BF16)
    val = jnp.dot(h, wi_ref[:, :D_MODEL], preferred_element_type=F32)
    glu = jnp.dot(h, wi_ref[:, D_MODEL:2 * D_MODEL], preferred_element_type=F32)
    gate = jnp.dot(h, wi_ref[:, 2 * D_MODEL:], preferred_element_type=F32)
    u_ref[...] = val * jax.nn.sigmoid(glu)
    sg_ref[...] = _silu(gate)


def _mid(x2d, mix_a, mix_b, w_out, g, w_in, tm):
    s = x2d.shape[0]
    row = lambda wd: pl.BlockSpec((tm, wd), lambda i: (i, 0))
    const = lambda shape: pl.BlockSpec(shape, lambda i: (0,) * len(shape))
    return pl.pallas_call(
        _mid_kernel,
        grid=(s // tm,),
        in_specs=[row(D_MODEL), row(A_W), row(B_VW), const((A_W + B_VW, D_MODEL)), const((1, D_MODEL)),
                  const((D_MODEL, 3 * D_MODEL))],
        out_specs=[row(D_MODEL), row(D_MODEL), row(D_MODEL)],
        out_shape=[jax.ShapeDtypeStruct((s, D_MODEL), F32)] * 3,
        compiler_params=pltpu.CompilerParams(dimension_semantics=("arbitrary",), vmem_limit_bytes=VMEM_LIMIT),
        name="mid",
    )(x2d, mix_a, mix_b, w_out, g, w_in)


HALO = 32


def _odd_out_kernel(u_ref, halo_ref, sg_ref, x1_ref, cw_ref, cb_ref, lg_ref, lb_ref, wo_ref, nf_ref, o_ref,
                    ubuf_ref, y_ref, *, tm):
    i = pl.program_id(0)
    ubuf_ref[HALO:, :] = u_ref[...]
    ubuf_ref[:HALO, :] = jnp.where(i > 0, halo_ref[...], 0.0)
    base = HALO - (CONV_WIDTH - 1)
    for cs in range(D_MODEL // LANES):
        ls = slice(cs * LANES, (cs + 1) * LANES)
        acc = jnp.zeros((tm, LANES), F32)
        for k in range(CONV_WIDTH):
            acc = acc + ubuf_ref[base + k:base + k + tm, ls] * cw_ref[k:k + 1, ls]
        y_ref[:, ls] = acc + cb_ref[:, ls]
    y = y_ref[...]
    mu = jnp.mean(y, axis=-1, keepdims=True)
    yc = y - mu
    var = jnp.mean(yc * yc, axis=-1, keepdims=True)
    yn = yc * lax.rsqrt(var + EPS) * lg_ref[...] + lb_ref[...]
    mixed = (_silu(yn) * sg_ref[...]).astype(BF16)
    x2 = x1_ref[...] + jnp.dot(mixed, wo_ref[...], preferred_element_type=F32)
    o_ref[...] = x2 * lax.rsqrt(jnp.mean(x2 * x2, axis=-1, keepdims=True) + EPS) * nf_ref[...]


def _odd_out(u, sg, x1, cw_pad, cb, lg, lb, w_out, nf, tm):
    s = u.shape[0]
    row = pl.BlockSpec((tm, D_MODEL), lambda i: (i, 0))
    const = lambda shape: pl.BlockSpec(shape, lambda i: (0,) * len(shape))
    halo = pl.BlockSpec((HALO, D_MODEL), lambda i: (jnp.maximum(i * (tm // HALO) - 1, 0), 0))
    return pl.pallas_call(
        functools.partial(_odd_out_kernel, tm=tm),
        grid=(s // tm,),
        in_specs=[row, halo, row, row, const((HALO, D_MODEL)), const((1, D_MODEL)), const((1, D_MODEL)),
                  const((1, D_MODEL)), const((D_MODEL, D_MODEL)), const((1, D_MODEL))],
        out_specs=row,
        out_shape=jax.ShapeDtypeStruct((s, D_MODEL), F32),
        scratch_shapes=[pltpu.VMEM((HALO + tm, D_MODEL), F32), pltpu.VMEM((tm, D_MODEL), F32)],
        compiler_params=pltpu.CompilerParams(dimension_semantics=("arbitrary",), vmem_limit_bytes=VMEM_LIMIT),
        name="odd_out",
    )(u, u, sg, x1, cw_pad, cb, lg, lb, w_out, nf)


def kernel(x, rel_bias, norm_even, w_in_even, w_gla_lr, b_gla_lr, gla_norm, w_out_even, norm_odd, w_in_odd,
           conv_w, conv_b, conv_ln_g, conv_ln_b, w_out_odd, norm_f):
    bsz, s, d = x.shape
    assert bsz == 1 and d == D_MODEL and s % GLA_BLOCK == 0
    assert norm_even.shape[0] == 1 and norm_odd.shape[0] == 1
    top_k = min(TOPK_MAX, s // 4)
    tm = 256
    x2d = x.reshape(s, d)

    wt_pad = _even_in_weights(w_in_even[0])
    qa_t, ka, va_t, ga, qi_t, ki, wi_t, qb, kb, vb, gb, lr = _even_in(x2d, norm_even[0].reshape(1, d), wt_pad, tm)
    mix_a = _dsa(qa_t, qi_t, wi_t, ga, ka, va_t, ki, _dsa_bias_table(rel_bias), top_k)

    head_pad = ((0, 0), (0, 0), (0, LANES - B_DK))
    wlr_pad = jnp.pad(w_gla_lr[0].reshape(GLA_RANK, B_HEADS, B_DK), head_pad).reshape(GLA_RANK, B_HEADS * LANES)
    wlr_pad = jnp.pad(wlr_pad, ((0, LANES - GLA_RANK), (0, 0)))
    blr_pad = jnp.pad(b_gla_lr[0].reshape(1, B_HEADS, B_DK), head_pad).reshape(1, B_HEADS * LANES)
    mix_b = _gla(qb, kb, vb, gb, lr, wlr_pad, blr_pad, gla_norm[0].reshape(1, B_DV))

    x1, u, sg = _mid(x2d, mix_a, mix_b, w_out_even[0].astype(BF16), norm_odd[0].reshape(1, d),
                     w_in_odd[0].astype(BF16), tm)

    cw_pad = jnp.zeros((HALO, d), F32).at[:CONV_WIDTH].set(conv_w[0])
    out = _odd_out(u, sg, x1, cw_pad, conv_b[0].reshape(1, d), conv_ln_g[0].reshape(1, d),
                   conv_ln_b[0].reshape(1, d), w_out_odd[0].astype(BF16), norm_f.reshape(1, d), tm)
    return out.reshape(bsz, s, d)
```

```python
import functools
import math

import jax
import jax.numpy as jnp
import numpy as np
from jax import lax
from jax.experimental import pallas as pl
from jax.experimental.pallas import tpu as pltpu

D_MODEL = 1024
A_HEADS = 8
A_HEAD_DIM = 64
IDX_HEADS = 4
IDX_DIM = 64
TOPK_MAX = 256
B_HEADS = 4
B_DK = 64
B_DV = 128
GLA_RANK = 16
GLA_TAU = 16.0
GLA_CHUNK = 64
GLA_SUB = 16
CONV_WIDTH = 31
REL_BUCKETS = 32
REL_MAX_EXACT = 16
REL_MAX_DIST = 128
EPS = 1e-6

A_W = A_HEADS * A_HEAD_DIM
B_KW = B_HEADS * B_DK
B_VW = B_HEADS * B_DV

LANES = 128
SUBLANES = 8
VMEM_LIMIT = 56 * 1024 * 1024

F32 = jnp.float32
BF16 = jnp.bfloat16
NEG_INF = float("-inf")
INT_MIN = -(2 ** 31)
NEG_INF_BITS = int(np.array(-np.inf, np.float32).view(np.int32))
NEG_INF_KEY = NEG_INF_BITS ^ 0x7FFFFFFF
M_INIT = -1e30
LOG2E = math.log2(math.e)

C_QA = 0
C_KA = C_QA + A_HEADS * LANES
C_VA = C_KA + A_W
C_GA = C_VA + A_W
C_QI = C_GA + A_W
C_KI = C_QI + IDX_HEADS * LANES
C_WI = C_KI + LANES
C_QB = C_WI + LANES
C_KB = C_QB + B_HEADS * LANES
C_VB = C_KB + B_HEADS * LANES
C_GB = C_VB + B_VW
C_LR = C_GB + B_VW
C_END = C_LR + LANES


def _t5_bucket_np(dist):
    dist = np.maximum(dist, 0)
    d = np.maximum(dist, 1).astype(np.float32)
    large = REL_MAX_EXACT + (np.log(d / REL_MAX_EXACT) / math.log(REL_MAX_DIST / REL_MAX_EXACT)
                             * (REL_BUCKETS - REL_MAX_EXACT)).astype(np.int32)
    large = np.minimum(large, REL_BUCKETS - 1)
    return np.where(dist < REL_MAX_EXACT, dist, large)


def _silu(x):
    return x * jax.nn.sigmoid(x)


def _even_in_kernel(x_ref, g_ref, wt_ref, qat_ref, ka_ref, vat_ref, ga_ref, qit_ref, ki_ref, wit_ref,
                    qb_ref, kb_ref, vb_ref, gb_ref, lr_ref):
    x = x_ref[...]
    h = (x * lax.rsqrt(jnp.mean(x * x, axis=-1, keepdims=True) + EPS) * g_ref[...]).astype(BF16)
    dn_t = (((1,), (1,)), ((), ()))

    def proj(lo, hi):
        return lax.dot_general(h, wt_ref[lo:hi, :], dn_t, preferred_element_type=F32)

    def proj_t(lo, hi):
        return lax.dot_general(wt_ref[lo:hi, :], h, dn_t, preferred_element_type=F32)

    qat_ref[...] = proj_t(C_QA, C_KA).astype(BF16)
    ka_ref[...] = proj(C_KA, C_VA).astype(BF16)
    vat_ref[...] = proj_t(C_VA, C_GA).astype(BF16)
    ga_ref[...] = proj(C_GA, C_QI)
    qit_ref[...] = proj_t(C_QI, C_KI).astype(BF16)
    ki_ref[...] = proj(C_KI, C_WI).astype(BF16)
    wit_ref[...] = proj_t(C_WI, C_WI + SUBLANES)
    qb_ref[...] = proj(C_QB, C_KB)
    kb_ref[...] = proj(C_KB, C_VB)
    vb_ref[...] = proj(C_VB, C_GB)
    gb_ref[...] = proj(C_GB, C_LR)
    lr_ref[...] = proj(C_LR, C_END)


def _even_in_layout():
    cols = np.cumsum([0, A_W, A_W, A_W, A_W, IDX_HEADS * IDX_DIM, IDX_DIM, IDX_HEADS,
                      B_KW, B_KW, B_VW, B_VW, GLA_RANK])
    (s_qa, s_ka, s_va, s_ga, s_qi, s_ki, s_wi, s_qb, s_kb, s_vb, s_gb, s_lr, _) = [int(c) for c in cols]
    src = np.full((C_END,), -1, np.int32)
    scale = np.ones((C_END,), np.float32)

    def put(dst, start, width, sc=1.0):
        src[dst:dst + width] = np.arange(start, start + width)
        scale[dst:dst + width] = sc

    for h in range(A_HEADS):
        put(C_QA + h * LANES + (h % 2) * A_HEAD_DIM, s_qa + h * A_HEAD_DIM, A_HEAD_DIM, (A_HEAD_DIM ** -0.5) * LOG2E)
    put(C_KA, s_ka, A_W)
    put(C_VA, s_va, A_W)
    put(C_GA, s_ga, A_W)
    for h in range(IDX_HEADS):
        put(C_QI + h * LANES, s_qi + h * IDX_DIM, IDX_DIM)
    put(C_KI, s_ki, IDX_DIM)
    put(C_WI, s_wi, IDX_HEADS, (IDX_DIM ** -0.5) * (IDX_HEADS ** -0.5))
    for h in range(B_HEADS):
        put(C_QB + h * LANES, s_qb + h * B_DK, B_DK, B_DK ** -0.5)
        put(C_KB + h * LANES, s_kb + h * B_DK, B_DK)
    put(C_VB, s_vb, B_VW)
    put(C_GB, s_gb, B_VW)
    put(C_LR, s_lr, GLA_RANK)
    return src, scale


def _even_in_weights(w):
    src, scale = _even_in_layout()
    rows = jnp.take(w.T, jnp.asarray(np.maximum(src, 0)), axis=0)
    return (rows * jnp.asarray(np.where(src >= 0, scale, 0.0))[:, None]).astype(BF16)


def _even_in(x2d, g, wt_pad, tm):
    s = x2d.shape[0]
    row = lambda wd, dt: (pl.BlockSpec((tm, wd), lambda i: (i, 0)), jax.ShapeDtypeStruct((s, wd), dt))
    col = lambda ht, dt: (pl.BlockSpec((ht, tm), lambda i: (0, i)), jax.ShapeDtypeStruct((ht, s), dt))
    outs = [col(C_KA - C_QA, BF16), row(A_W, BF16), col(A_W, BF16), row(A_W, F32), col(C_KI - C_QI, BF16),
            row(LANES, BF16), col(SUBLANES, F32), row(C_KB - C_QB, F32), row(C_VB - C_KB, F32), row(B_VW, F32),
            row(B_VW, F32), row(LANES, F32)]
    return pl.pallas_call(
        _even_in_kernel,
        grid=(s // tm,),
        in_specs=[pl.BlockSpec((tm, D_MODEL), lambda i: (i, 0)),
                  pl.BlockSpec((1, D_MODEL), lambda i: (0, 0)),
                  pl.BlockSpec((C_END, D_MODEL), lambda i: (0, 0))],
        out_specs=[spec for spec, _ in outs],
        out_shape=[shape for _, shape in outs],
        compiler_params=pltpu.CompilerParams(dimension_semantics=("arbitrary",), vmem_limit_bytes=VMEM_LIMIT),
        name="even_in",
    )(x2d, g, wt_pad)


DSA_Q = 128
DSA_KB = 128
DSA_WIDE = 4
DSA_NACC = 4
DSA_CELLS = 32
DSA_DIGIT_BITS = 15
DIGIT_MIN, DIGIT_MAX = -(2 ** 15), 2 ** 15 - 1
PACKED_SUBLANES = 16
DSA_VMEM_LIMIT = 58 * 1024 * 1024


def _f32_key(x):
    bits = lax.bitcast_convert_type(x, jnp.int32)
    return bits ^ (lax.shift_right_arithmetic(bits, 31) & 0x7FFFFFFF)


def _fold_keys(x, op):
    n, w = x.shape
    group = DSA_NACC * SUBLANES
    if n % group == 0 and n > group:
        x = op(x.reshape(n // group, DSA_NACC, SUBLANES, w), axis=0)
        return op(x, axis=0)
    return op(x.reshape(n // SUBLANES, SUBLANES, w), axis=0)


def _dsa_kernel(qat_ref, qit_ref, wit_ref, ga_ref, ka_ref, vat_ref, ki_ref, btab_ref, tri_ref, o_ref,
                slab_ref, dig_ref, s_ref, acc_ref, m_ref, l_ref, *, top_k):
    q, kb, wide = DSA_Q, DSA_KB, DSA_WIDE * DSA_KB
    i = pl.program_id(0)
    nkb = i + 1
    nwide = jnp.maximum(i - 1, 0) // DSA_WIDE
    n0 = nwide * DSA_WIDE
    t_row = i * q + lax.broadcasted_iota(jnp.int32, (1, q), 1)
    key_off = lax.broadcasted_iota(jnp.int32, (kb, 1), 0)
    npair = A_HEADS // 2

    def wide_rows(jw):
        return pl.ds(pl.multiple_of(jw * wide, wide), wide)

    def narrow_rows(j):
        return pl.ds(pl.multiple_of(j * kb, kb), kb)

    def pipelined(first, last, produce, consume):
        def body(t, carry):
            consume(t, (t - first) % 2, carry)
            return produce(jnp.minimum(t + 1, last - 1), (t + 1 - first) % 2)

        lax.fori_loop(first, last, body, produce(first, 0))

    w_head = [wit_ref[h:h + 1, :] for h in range(IDX_HEADS)]

    def scores(rows):
        k_idx = ki_ref[rows, :]
        sc = None
        for h in range(IDX_HEADS):
            s = jnp.dot(k_idx, qit_ref[h * LANES:(h + 1) * LANES, :], preferred_element_type=F32)
            term = w_head[h] * jnp.maximum(s, 0.0)
            sc = term if sc is None else sc + term
        return sc

    def score_wide(jw, c):
        rows = wide_rows(jw)
        slab_ref[rows, :] = _f32_key(scores(rows))
        return c

    def score_narrow(j, c):
        rows = narrow_rows(j)
        sc = jnp.where(j * kb + key_off <= t_row, scores(rows), NEG_INF)
        slab_ref[rows, :] = _f32_key(sc)
        return c

    lax.fori_loop(0, nwide, score_wide, 0)
    lax.fori_loop(n0, nkb, score_narrow, 0)

    def count_ge(trial):
        def cnt(blk):
            return _fold_keys(jnp.where(blk >= trial, 1.0, 0.0), jnp.sum)

        acc = lax.fori_loop(0, nwide, lambda jw, a: a + cnt(slab_ref[wide_rows(jw), :]),
                            jnp.zeros((SUBLANES, q), F32))
        acc = lax.fori_loop(n0, nkb, lambda j, a: a + cnt(slab_ref[narrow_rows(j), :]), acc)
        return jnp.sum(acc, axis=0, keepdims=True).astype(jnp.int32)

    assert DSA_CELLS * SUBLANES >= top_k and wide % (DSA_CELLS * SUBLANES) == 0

    def cells_wide(jw, cm):
        blk = slab_ref[wide_rows(jw), :].reshape(wide // (DSA_CELLS * SUBLANES), DSA_CELLS, SUBLANES, q)
        return jnp.maximum(cm, jnp.max(blk, axis=0))

    def cells_narrow(j, cm):
        blk = slab_ref[narrow_rows(j), :].reshape(kb // SUBLANES, SUBLANES, q)
        return jnp.concatenate([jnp.maximum(cm[:kb // SUBLANES], blk), cm[kb // SUBLANES:]], axis=0)

    cell_max = lax.fori_loop(0, nwide, cells_wide, jnp.full((DSA_CELLS, SUBLANES, q), INT_MIN, jnp.int32))
    cell_max = lax.fori_loop(n0, nkb, cells_narrow, cell_max)
    hi_key = jnp.max(jnp.max(cell_max, axis=0), axis=0, keepdims=True)
    lo_key = jnp.min(jnp.min(cell_max, axis=0), axis=0, keepdims=True)
    same_sign = jnp.min(lo_key ^ hi_key) >= 0
    base = jnp.where(same_sign, lo_key, INT_MIN)
    span = jnp.where(same_sign, hi_key - lo_key, -1)
    nbits = jnp.where(same_sign, jnp.max(32 - lax.clz(span)), 32)

    def plain_search():
        def bisect(b, off):
            trial_off = off + lax.shift_left(jnp.int32(1), nbits - 1 - b)
            in_span = (trial_off ^ INT_MIN) <= (span ^ INT_MIN)
            ok = jnp.logical_and(in_span, count_ge(base + trial_off) >= top_k)
            return jnp.where(ok, trial_off, off)

        return base + lax.fori_loop(0, nbits, bisect, jnp.zeros((1, q), jnp.int32))

    def digit_pass(floor, digit_of):
        def put(rows):
            key = slab_ref[rows, :]
            dig_ref[rows, :] = jnp.where(key < floor, DIGIT_MIN, digit_of(key)).astype(jnp.int16)

        lax.fori_loop(0, nwide, lambda jw, c: put(wide_rows(jw)), None)
        lax.fori_loop(n0, nkb, lambda j, c: put(narrow_rows(j)), None)

    def digit_count_ge(trial):
        t16 = trial.astype(jnp.int16)

        def cnt(blk):
            parts = [jnp.where(blk[r:r + PACKED_SUBLANES] >= t16, jnp.int16(1), jnp.int16(0))
                     for r in range(0, blk.shape[0], PACKED_SUBLANES)]
            while len(parts) > 1:
                parts = [parts[a] + parts[a + 1] for a in range(0, len(parts), 2)]
            return parts[0].astype(jnp.int32)

        acc = lax.fori_loop(0, nwide, lambda jw, a: a + cnt(dig_ref[wide_rows(jw), :]),
                            jnp.zeros((PACKED_SUBLANES, q), jnp.int32))
        acc = lax.fori_loop(n0, nkb, lambda j, a: a + cnt(dig_ref[narrow_rows(j), :]), acc)
        return jnp.sum(acc, axis=0, keepdims=True)

    def digit_search(nb, start):
        def step(b, v):
            trial = v + lax.shift_left(jnp.int32(1), nb - 1 - b)
            return jnp.where(digit_count_ge(trial) >= top_k, trial, v)

        return lax.fori_loop(0, nb, step, start)

    def packed_search():
        digit_pass(base, lambda key: lax.shift_right_arithmetic(key - base, DSA_DIGIT_BITS))
        high = digit_search(jnp.maximum(nbits - DSA_DIGIT_BITS, 0), jnp.zeros((1, q), jnp.int32))
        base2 = base + lax.shift_left(high, DSA_DIGIT_BITS)
        half = 1 << (DSA_DIGIT_BITS - 1)
        digit_pass(base2, lambda key: jnp.minimum(key - (base2 + half), DIGIT_MAX))
        low = digit_search(DSA_DIGIT_BITS, jnp.full((1, q), -half, jnp.int32))
        return base2 + low + half

    packed_ok = jnp.logical_and(jnp.logical_and(same_sign, jnp.min(lo_key) >= 0), nbits <= 2 * DSA_DIGIT_BITS)
    thr = lax.cond(packed_ok, packed_search, plain_search)
    tied = jnp.logical_and(count_ge(thr) > top_k, thr > NEG_INF_KEY)
    any_tie = jnp.max(jnp.where(tied, 1, 0)) > 0

    thr_valid = jnp.maximum(thr, NEG_INF_KEY + 1)

    def mask_fast():
        def body(rows):
            slab_ref[rows, :] = jnp.where(slab_ref[rows, :] >= thr_valid, 0, NEG_INF_BITS)

        lax.fori_loop(0, nwide, lambda jw, c: body(wide_rows(jw)), None)
        lax.fori_loop(n0, nkb, lambda j, c: body(narrow_rows(j)), None)

    def mask_tied():
        need = (top_k - count_ge(thr + 1)).astype(F32)

        def body(j, seen):
            rows = narrow_rows(j)
            key = slab_ref[rows, :]
            eq = key == thr
            eq_f = jnp.where(eq, 1.0, 0.0)
            before = jnp.dot(tri_ref[...], eq_f.astype(BF16), preferred_element_type=F32)
            sel = jnp.logical_or(key > thr, jnp.logical_and(eq, seen + before < need))
            sel = jnp.logical_and(sel, key > NEG_INF_KEY)
            slab_ref[rows, :] = jnp.where(sel, 0, NEG_INF_BITS)
            return seen + jnp.sum(eq_f, axis=0, keepdims=True)

        lax.fori_loop(0, nkb, body, jnp.zeros((1, q), F32))

    lax.cond(any_tie, mask_tied, mask_fast)

    m_ref[...] = jnp.full(m_ref.shape, M_INIT, F32)
    l_ref[...] = jnp.zeros(l_ref.shape, F32)
    acc_ref[...] = jnp.zeros(acc_ref.shape, F32)

    def logits(rows, n, buf, rel):
        mask = lax.bitcast_convert_type(slab_ref[rows, :], F32)
        mask2 = jnp.concatenate([mask, mask], axis=1)
        tile_max = []
        for p in range(npair):
            ls = slice(p * LANES, (p + 1) * LANES)
            w_q = jnp.concatenate([qat_ref[(2 * p) * LANES:(2 * p + 1) * LANES, :],
                                   qat_ref[(2 * p + 1) * LANES:(2 * p + 2) * LANES, :]], axis=1)
            s = jnp.dot(ka_ref[rows, ls], w_q, preferred_element_type=F32) + mask2
            if rel is not None:
                s = s + btab_ref[rel, p]
            s_ref[buf, p, 0:n, :] = s
            tile_max.append(_fold_keys(s, jnp.max))
        return tuple(tile_max)

    def attend(rows, n, buf, tile_max):
        for p in range(npair):
            ls = slice(p * LANES, (p + 1) * LANES)
            m_old = m_ref[p]
            m_new = jnp.maximum(m_old, jnp.max(tile_max[p], axis=0, keepdims=True))
            alpha = jnp.exp2(m_old - m_new)
            pexp = jnp.exp2(s_ref[buf, p, 0:n, :] - m_new[0:1])
            m_ref[p] = m_new
            l_ref[p] = alpha * l_ref[p] + _fold_keys(pexp, jnp.sum)
            acc_ref[p] = alpha[0:1] * acc_ref[p] + jnp.dot(vat_ref[ls, rows], pexp.astype(BF16),
                                                           preferred_element_type=F32)

    @pl.when(nwide > 0)
    def _():
        pipelined(0, nwide, lambda jw, buf: logits(wide_rows(jw), wide, buf, None),
                  lambda jw, buf, tile_max: attend(wide_rows(jw), wide, buf, tile_max))

    pipelined(n0, nkb, lambda j, buf: logits(narrow_rows(j), kb, buf, jnp.minimum(i - j, 2)),
              lambda j, buf, tile_max: attend(narrow_rows(j), kb, buf, tile_max))

    for p in range(npair):
        o_t = acc_ref[p] / jnp.sum(l_ref[p], axis=0, keepdims=True)
        o_pair = jnp.concatenate([o_t[:A_HEAD_DIM, :q], o_t[A_HEAD_DIM:, q:]], axis=0).T
        ls = slice(p * LANES, (p + 1) * LANES)
        o_ref[:, ls] = (o_pair * _silu(ga_ref[:, ls])).astype(o_ref.dtype)


def _dsa_bias_table(rel_bias):
    ii = np.arange(DSA_Q)[:, None]
    jj = np.arange(DSA_KB)[None, :]
    buckets = np.stack([_t5_bucket_np(ii - jj), _t5_bucket_np(DSA_KB + ii - jj),
                        np.full((DSA_Q, DSA_KB), REL_BUCKETS - 1)])
    assert (_t5_bucket_np(np.arange(DSA_KB + 1, 4 * DSA_KB)) == REL_BUCKETS - 1).all()
    rb = rel_bias.astype(F32) * LOG2E
    rb = rb - rb[REL_BUCKETS - 1]
    onehot = np.zeros((3, DSA_KB, DSA_Q, REL_BUCKETS), np.float32)
    np.put_along_axis(onehot, np.transpose(buckets, (0, 2, 1))[..., None], 1.0, axis=-1)
    tab = jnp.einsum('tsrb,bh->thsr', jnp.asarray(onehot), rb, precision=lax.Precision.HIGHEST)
    tab = tab.reshape(3, A_HEADS // 2, 2, DSA_KB, DSA_Q)
    return jnp.transpose(tab, (0, 1, 3, 2, 4)).reshape(3, A_HEADS // 2, DSA_KB, 2 * DSA_Q)


def _dsa(qa_t, qi_t, wi_t, ga, ka, va_t, ki, btab, top_k):
    s = ga.shape[0]
    q, kb = DSA_Q, DSA_KB
    tri = jnp.asarray(np.tril(np.ones((kb, kb), np.float32), -1), BF16)
    whole = lambda shape: pl.BlockSpec(shape, lambda i: (0,) * len(shape), pipeline_mode=pl.Buffered(1))
    return pl.pallas_call(
        functools.partial(_dsa_kernel, top_k=top_k),
        grid=(s // q,),
        in_specs=[pl.BlockSpec((A_HEADS * LANES, q), lambda i: (0, i)),
                  pl.BlockSpec((IDX_HEADS * LANES, q), lambda i: (0, i)),
                  pl.BlockSpec((SUBLANES, q), lambda i: (0, i)),
                  pl.BlockSpec((q, A_W), lambda i: (i, 0)),
                  whole((s, A_W)), whole((A_W, s)), whole((s, LANES)),
                  whole((3, A_HEADS // 2, kb, 2 * q)), whole((kb, kb))],
        out_specs=pl.BlockSpec((q, A_W), lambda i: (i, 0)),
        out_shape=jax.ShapeDtypeStruct((s, A_W), BF16),
        scratch_shapes=[pltpu.VMEM((s, q), jnp.int32),
                        pltpu.VMEM((s, q), jnp.int16),
                        pltpu.VMEM((2, A_HEADS // 2, DSA_WIDE * kb, 2 * q), F32),
                        pltpu.VMEM((A_HEADS // 2, LANES, 2 * q), F32),
                        pltpu.VMEM((A_HEADS // 2, SUBLANES, 2 * q), F32),
                        pltpu.VMEM((A_HEADS // 2, SUBLANES, 2 * q), F32)],
        compiler_params=pltpu.CompilerParams(dimension_semantics=("arbitrary",), vmem_limit_bytes=DSA_VMEM_LIMIT),
        name="dsa",
    )(qa_t, qi_t, wi_t, ga, ka, va_t, ki, btab, tri)


GLA_BLOCK = 512


def _gla_kernel(qb_ref, kb_ref, vb_ref, gb_ref, lr_ref, wlr_ref, blr_ref, gn_ref, tril_ref, o_ref,
                state_ref, b_ref, att_ref):
    c, sub = GLA_CHUNK, GLA_SUB
    nsub = c // sub

    @pl.when(pl.program_id(0) == 0)
    def _():
        state_ref[...] = jnp.zeros(state_ref.shape, F32)

    dn_t = (((1,), (1,)), ((), ()))
    dn_0 = (((0,), (0,)), ((), ()))
    hp = lax.Precision.HIGHEST
    row_i = lax.broadcasted_iota(jnp.int32, (sub, sub), 0)
    col_j = lax.broadcasted_iota(jnp.int32, (sub, sub), 1)

    def chunk(ci, carry):
        r0 = pl.multiple_of(ci * c, c)
        z = jnp.dot(lr_ref[pl.ds(r0, c), :], wlr_ref[...], precision=hp, preferred_element_type=F32) + blr_ref[...]
        log_a = (jnp.minimum(z, 0.0) - jnp.log1p(jnp.exp(-jnp.abs(z)))) * (1.0 / GLA_TAU)
        b_ref[...] = jnp.dot(tril_ref[...], log_a, precision=hp, preferred_element_type=F32)
        for h in range(B_HEADS):
            ls = slice(h * LANES, (h + 1) * LANES)
            b = b_ref[:, ls]
            qh = qb_ref[pl.ds(r0, c), ls]
            kh = kb_ref[pl.ds(r0, c), ls]
            vh = vb_ref[pl.ds(r0, c), ls]
            state_t = state_ref[h]
            b_last = b[c - 1:c, :]
            o = lax.dot_general((qh * jnp.exp(b)).astype(BF16), state_t.astype(BF16), dn_t,
                                preferred_element_type=F32)
            for si in range(nsub):
                rs = slice(si * sub, (si + 1) * sub)
                b_i, q_i = b[rs], qh[rs]
                if si > 0:
                    ref_row = b[si * sub:si * sub + 1, :]
                    q_t = (q_i * jnp.exp(b_i - ref_row)).astype(BF16)
                    k_t = (kh[:si * sub] * jnp.exp(ref_row - b[:si * sub])).astype(BF16)
                    att_ref[rs, :si * sub] = lax.dot_general(q_t, k_t, dn_t, preferred_element_type=F32)
                k_i = kh[rs]
                diag = jnp.zeros((sub, sub), F32)
                for jl in range(sub):
                    diff = jnp.minimum(b_i - b_i[jl:jl + 1, :], 0.0)
                    col = jnp.sum(q_i * k_i[jl:jl + 1, :] * jnp.exp(diff), axis=-1, keepdims=True)
                    diag = jnp.where(col_j == jl, col, diag)
                att_ref[rs, rs] = jnp.where(col_j <= row_i, diag, 0.0)
                if si + 1 < nsub:
                    att_ref[rs, (si + 1) * sub:] = jnp.zeros((sub, c - (si + 1) * sub), F32)
            o = o + jnp.dot(att_ref[...].astype(BF16), vh.astype(BF16), preferred_element_type=F32)
            k_dec = (kh * jnp.exp(b_last - b)).astype(BF16)
            state_ref[h] = (jnp.exp(b_last) * state_t
                            + lax.dot_general(vh.astype(BF16), k_dec, dn_0, preferred_element_type=F32))
            o = o * lax.rsqrt(jnp.mean(o * o, axis=-1, keepdims=True) + EPS) * gn_ref[...]
            o_ref[pl.ds(r0, c), ls] = (o * _silu(gb_ref[pl.ds(r0, c), ls])).astype(o_ref.dtype)
        return carry

    lax.fori_loop(0, GLA_BLOCK // c, chunk, 0)


def _gla(qb, kb, vb, gb, lr, wlr_pad, blr_pad, gn):
    s = qb.shape[0]
    blk = min(GLA_BLOCK, s)
    assert blk == GLA_BLOCK and s % blk == 0
    tril = jnp.asarray(np.tril(np.ones((GLA_CHUNK, GLA_CHUNK), np.float32)))
    row = lambda wd: pl.BlockSpec((blk, wd), lambda i: (i, 0))
    const = lambda shape: pl.BlockSpec(shape, lambda i: (0,) * len(shape))
    return pl.pallas_call(
        _gla_kernel,
        grid=(s // blk,),
        in_specs=[row(B_HEADS * LANES), row(B_HEADS * LANES), row(B_VW), row(B_VW), row(LANES),
                  const((LANES, B_HEADS * LANES)), const((1, B_HEADS * LANES)), const((1, B_DV)),
                  const((GLA_CHUNK, GLA_CHUNK))],
        out_specs=row(B_VW),
        out_shape=jax.ShapeDtypeStruct((s, B_VW), BF16),
        scratch_shapes=[pltpu.VMEM((B_HEADS, LANES, B_DV), F32),
                        pltpu.VMEM((GLA_CHUNK, B_HEADS * LANES), F32),
                        pltpu.VMEM((GLA_CHUNK, GLA_CHUNK), F32)],
        compiler_params=pltpu.CompilerParams(dimension_semantics=("arbitrary",), vmem_limit_bytes=VMEM_LIMIT),
        name="gla",
    )(qb, kb, vb, gb, lr, wlr_pad, blr_pad, gn, tril)


def _mid_kernel(x_ref, ma_ref, mb_ref, wo_ref, g_ref, wi_ref, x1_ref, u_ref, sg_ref):
    x1 = (x_ref[...]
          + jnp.dot(ma_ref[...], wo_ref[:A_W, :], preferred_element_type=F32)
          + jnp.dot(mb_ref[...], wo_ref[A_W:, :], preferred_element_type=F32))
    x1_ref[...] = x1
    h = (x1 * lax.rsqrt(jnp.mean(x1 * x1, axis=-1, keepdims=True) + EPS) * g_ref[...]).astype(BF16)
    val = jnp.dot(h, wi_ref[:, :D_MODEL], preferred_element_type=F32)
    glu = jnp.dot(h, wi_ref[:, D_MODEL:2 * D_MODEL], preferred_element_type=F32)
    gate = jnp.dot(h, wi_ref[:, 2 * D_MODEL:], preferred_element_type=F32)
    u_ref[...] = val * jax.nn.sigmoid(glu)
    sg_ref[...] = _silu(gate)


def _mid(x2d, mix_a, mix_b, w_out, g, w_in, tm):
    s = x2d.shape[0]
    row = lambda wd: pl.BlockSpec((tm, wd), lambda i: (i, 0))
    const = lambda shape: pl.BlockSpec(shape, lambda i: (0,) * len(shape))
    return pl.pallas_call(
        _mid_kernel,
        grid=(s // tm,),
        in_specs=[row(D_MODEL), row(A_W), row(B_VW), const((A_W + B_VW, D_MODEL)), const((1, D_MODEL)),
                  const((D_MODEL, 3 * D_MODEL))],
        out_specs=[row(D_MODEL), row(D_MODEL), row(D_MODEL)],
        out_shape=[jax.ShapeDtypeStruct((s, D_MODEL), F32)] * 3,
        compiler_params=pltpu.CompilerParams(dimension_semantics=("arbitrary",), vmem_limit_bytes=VMEM_LIMIT),
        name="mid",
    )(x2d, mix_a, mix_b, w_out, g, w_in)


HALO = 32


def _odd_out_kernel(u_ref, halo_ref, sg_ref, x1_ref, cw_ref, cb_ref, lg_ref, lb_ref, wo_ref, nf_ref, o_ref,
                    ubuf_ref, y_ref, *, tm):
    i = pl.program_id(0)
    ubuf_ref[HALO:, :] = u_ref[...]
    ubuf_ref[:HALO, :] = jnp.where(i > 0, halo_ref[...], 0.0)
    base = HALO - (CONV_WIDTH - 1)
    for cs in range(D_MODEL // LANES):
        ls = slice(cs * LANES, (cs + 1) * LANES)
        acc = jnp.zeros((tm, LANES), F32)
        for k in range(CONV_WIDTH):
            acc = acc + ubuf_ref[base + k:base + k + tm, ls] * cw_ref[k:k + 1, ls]
        y_ref[:, ls] = acc + cb_ref[:, ls]
    y = y_ref[...]
    mu = jnp.mean(y, axis=-1, keepdims=True)
    yc = y - mu
    var = jnp.mean(yc * yc, axis=-1, keepdims=True)
    yn = yc * lax.rsqrt(var + EPS) * lg_ref[...] + lb_ref[...]
    mixed = (_silu(yn) * sg_ref[...]).astype(BF16)
    x2 = x1_ref[...] + jnp.dot(mixed, wo_ref[...], preferred_element_type=F32)
    o_ref[...] = x2 * lax.rsqrt(jnp.mean(x2 * x2, axis=-1, keepdims=True) + EPS) * nf_ref[...]


def _odd_out(u, sg, x1, cw_pad, cb, lg, lb, w_out, nf, tm):
    s = u.shape[0]
    row = pl.BlockSpec((tm, D_MODEL), lambda i: (i, 0))
    const = lambda shape: pl.BlockSpec(shape, lambda i: (0,) * len(shape))
    halo = pl.BlockSpec((HALO, D_MODEL), lambda i: (jnp.maximum(i * (tm // HALO) - 1, 0), 0))
    return pl.pallas_call(
        functools.partial(_odd_out_kernel, tm=tm),
        grid=(s // tm,),
        in_specs=[row, halo, row, row, const((HALO, D_MODEL)), const((1, D_MODEL)), const((1, D_MODEL)),
                  const((1, D_MODEL)), const((D_MODEL, D_MODEL)), const((1, D_MODEL))],
        out_specs=row,
        out_shape=jax.ShapeDtypeStruct((s, D_MODEL), F32),
        scratch_shapes=[pltpu.VMEM((HALO + tm, D_MODEL), F32), pltpu.VMEM((tm, D_MODEL), F32)],
        compiler_params=pltpu.CompilerParams(dimension_semantics=("arbitrary",), vmem_limit_bytes=VMEM_LIMIT),
        name="odd_out",
    )(u, u, sg, x1, cw_pad, cb, lg, lb, w_out, nf)


def kernel(x, rel_bias, norm_even, w_in_even, w_gla_lr, b_gla_lr, gla_norm, w_out_even, norm_odd, w_in_odd,
           conv_w, conv_b, conv_ln_g, conv_ln_b, w_out_odd, norm_f):
    bsz, s, d = x.shape
    assert bsz == 1 and d == D_MODEL and s % GLA_BLOCK == 0
    assert norm_even.shape[0] == 1 and norm_odd.shape[0] == 1
    top_k = min(TOPK_MAX, s // 4)
    tm = 256
    x2d = x.reshape(s, d)

    wt_pad = _even_in_weights(w_in_even[0])
    qa_t, ka, va_t, ga, qi_t, ki, wi_t, qb, kb, vb, gb, lr = _even_in(x2d, norm_even[0].reshape(1, d), wt_pad, tm)
    mix_a = _dsa(qa_t, qi_t, wi_t, ga, ka, va_t, ki, _dsa_bias_table(rel_bias), top_k)

    head_pad = ((0, 0), (0, 0), (0, LANES - B_DK))
    wlr_pad = jnp.pad(w_gla_lr[0].reshape(GLA_RANK, B_HEADS, B_DK), head_pad).reshape(GLA_RANK, B_HEADS * LANES)
    wlr_pad = jnp.pad(wlr_pad, ((0, LANES - GLA_RANK), (0, 0)))
    blr_pad = jnp.pad(b_gla_lr[0].reshape(1, B_HEADS, B_DK), head_pad).reshape(1, B_HEADS * LANES)
    mix_b = _gla(qb, kb, vb, gb, lr, wlr_pad, blr_pad, gla_norm[0].reshape(1, B_DV))

    x1, u, sg = _mid(x2d, mix_a, mix_b, w_out_even[0].astype(BF16), norm_odd[0].reshape(1, d),
                     w_in_odd[0].astype(BF16), tm)

    cw_pad = jnp.zeros((HALO, d), F32).at[:CONV_WIDTH].set(conv_w[0])
    out = _odd_out(u, sg, x1, cw_pad, conv_b[0].reshape(1, d), conv_ln_g[0].reshape(1, d),
                   conv_ln_b[0].reshape(1, d), w_out_odd[0].astype(BF16), norm_f.reshape(1, d), tm)
    return out.reshape(bsz, s, d)
```

```python
import functools
import math

import jax
import jax.numpy as jnp
import numpy as np
from jax import lax
from jax.experimental import pallas as pl
from jax.experimental.pallas import tpu as pltpu

D_MODEL = 1024
A_HEADS = 8
A_HEAD_DIM = 64
IDX_HEADS = 4
IDX_DIM = 64
TOPK_MAX = 256
B_HEADS = 4
B_DK = 64
B_DV = 128
GLA_RANK = 16
GLA_TAU = 16.0
GLA_CHUNK = 64
GLA_SUB = 16
CONV_WIDTH = 31
REL_BUCKETS = 32
REL_MAX_EXACT = 16
REL_MAX_DIST = 128
EPS = 1e-6

A_W = A_HEADS * A_HEAD_DIM
B_KW = B_HEADS * B_DK
B_VW = B_HEADS * B_DV

LANES = 128
SUBLANES = 8
VMEM_LIMIT = 56 * 1024 * 1024

F32 = jnp.float32
BF16 = jnp.bfloat16
NEG_INF = float("-inf")
INT_MIN = -(2 ** 31)
NEG_INF_BITS = int(np.array(-np.inf, np.float32).view(np.int32))
NEG_INF_KEY = NEG_INF_BITS ^ 0x7FFFFFFF
M_INIT = -1e30
LOG2E = math.log2(math.e)

C_QA = 0
C_KA = C_QA + A_HEADS * LANES
C_VA = C_KA + A_W
C_GA = C_VA + A_W
C_QI = C_GA + A_W
C_KI = C_QI + IDX_HEADS * LANES
C_WI = C_KI + LANES
C_QB = C_WI + LANES
C_KB = C_QB + B_HEADS * LANES
C_VB = C_KB + B_HEADS * LANES
C_GB = C_VB + B_VW
C_LR = C_GB + B_VW
C_END = C_LR + LANES


def _t5_bucket_np(dist):
    dist = np.maximum(dist, 0)
    d = np.maximum(dist, 1).astype(np.float32)
    large = REL_MAX_EXACT + (np.log(d / REL_MAX_EXACT) / math.log(REL_MAX_DIST / REL_MAX_EXACT)
                             * (REL_BUCKETS - REL_MAX_EXACT)).astype(np.int32)
    large = np.minimum(large, REL_BUCKETS - 1)
    return np.where(dist < REL_MAX_EXACT, dist, large)


def _silu(x):
    return x * jax.nn.sigmoid(x)


def _even_in_kernel(x_ref, g_ref, wt_ref, qat_ref, ka_ref, vat_ref, ga_ref, qit_ref, ki_ref, wit_ref,
                    qb_ref, kb_ref, vb_ref, gb_ref, lr_ref):
    x = x_ref[...]
    h = (x * lax.rsqrt(jnp.mean(x * x, axis=-1, keepdims=True) + EPS) * g_ref[...]).astype(BF16)
    dn_t = (((1,), (1,)), ((), ()))

    def proj(lo, hi):
        return lax.dot_general(h, wt_ref[lo:hi, :], dn_t, preferred_element_type=F32)

    def proj_t(lo, hi):
        return lax.dot_general(wt_ref[lo:hi, :], h, dn_t, preferred_element_type=F32)

    qat_ref[...] = proj_t(C_QA, C_KA).astype(BF16)
    ka_ref[...] = proj(C_KA, C_VA).astype(BF16)
    vat_ref[...] = proj_t(C_VA, C_GA).astype(BF16)
    ga_ref[...] = proj(C_GA, C_QI)
    qit_ref[...] = proj_t(C_QI, C_KI).astype(BF16)
    ki_ref[...] = proj(C_KI, C_WI).astype(BF16)
    wit_ref[...] = proj_t(C_WI, C_WI + SUBLANES)
    qb_ref[...] = proj(C_QB, C_KB)
    kb_ref[...] = proj(C_KB, C_VB)
    vb_ref[...] = proj(C_VB, C_GB)
    gb_ref[...] = proj(C_GB, C_LR)
    lr_ref[...] = proj(C_LR, C_END)


def _even_in_layout():
    cols = np.cumsum([0, A_W, A_W, A_W, A_W, IDX_HEADS * IDX_DIM, IDX_DIM, IDX_HEADS,
                      B_KW, B_KW, B_VW, B_VW, GLA_RANK])
    (s_qa, s_ka, s_va, s_ga, s_qi, s_ki, s_wi, s_qb, s_kb, s_vb, s_gb, s_lr, _) = [int(c) for c in cols]
    src = np.full((C_END,), -1, np.int32)
    scale = np.ones((C_END,), np.float32)

    def put(dst, start, width, sc=1.0):
        src[dst:dst + width] = np.arange(start, start + width)
        scale[dst:dst + width] = sc

    for h in range(A_HEADS):
        put(C_QA + h * LANES + (h % 2) * A_HEAD_DIM, s_qa + h * A_HEAD_DIM, A_HEAD_DIM, (A_HEAD_DIM ** -0.5) * LOG2E)
    put(C_KA, s_ka, A_W)
    put(C_VA, s_va, A_W)
    put(C_GA, s_ga, A_W)
    for h in range(IDX_HEADS):
        put(C_QI + h * LANES, s_qi + h * IDX_DIM, IDX_DIM)
    put(C_KI, s_ki, IDX_DIM)
    put(C_WI, s_wi, IDX_HEADS, (IDX_DIM ** -0.5) * (IDX_HEADS ** -0.5))
    for h in range(B_HEADS):
        put(C_QB + h * LANES, s_qb + h * B_DK, B_DK, B_DK ** -0.5)
        put(C_KB + h * LANES, s_kb + h * B_DK, B_DK)
    put(C_VB, s_vb, B_VW)
    put(C_GB, s_gb, B_VW)
    put(C_LR, s_lr, GLA_RANK)
    return src, scale


def _even_in_weights(w):
    src, scale = _even_in_layout()
    rows = jnp.take(w.T, jnp.asarray(np.maximum(src, 0)), axis=0)
    return (rows * jnp.asarray(np.where(src >= 0, scale, 0.0))[:, None]).astype(BF16)


def _even_in(x2d, g, wt_pad, tm):
    s = x2d.shape[0]
    row = lambda wd, dt: (pl.BlockSpec((tm, wd), lambda i: (i, 0)), jax.ShapeDtypeStruct((s, wd), dt))
    col = lambda ht, dt: (pl.BlockSpec((ht, tm), lambda i: (0, i)), jax.ShapeDtypeStruct((ht, s), dt))
    outs = [col(C_KA - C_QA, BF16), row(A_W, BF16), col(A_W, BF16), row(A_W, F32), col(C_KI - C_QI, BF16),
            row(LANES, BF16), col(SUBLANES, F32), row(C_KB - C_QB, F32), row(C_VB - C_KB, F32), row(B_VW, F32),
            row(B_VW, F32), row(LANES, F32)]
    return pl.pallas_call(
        _even_in_kernel,
        grid=(s // tm,),
        in_specs=[pl.BlockSpec((tm, D_MODEL), lambda i: (i, 0)),
                  pl.BlockSpec((1, D_MODEL), lambda i: (0, 0)),
                  pl.BlockSpec((C_END, D_MODEL), lambda i: (0, 0))],
        out_specs=[spec for spec, _ in outs],
        out_shape=[shape for _, shape in outs],
        compiler_params=pltpu.CompilerParams(dimension_semantics=("arbitrary",), vmem_limit_bytes=VMEM_LIMIT),
        name="even_in",
    )(x2d, g, wt_pad)


DSA_Q = 128
DSA_KB = 128
DSA_WIDE = 4
DSA_NACC = 4
DSA_CELLS = 32
DSA_DIGIT_BITS = 15
DIGIT_MIN, DIGIT_MAX = -(2 ** 15), 2 ** 15 - 1
PACKED_SUBLANES = 16
DSA_VMEM_LIMIT = 58 * 1024 * 1024


def _f32_key(x):
    bits = lax.bitcast_convert_type(x, jnp.int32)
    return bits ^ (lax.shift_right_arithmetic(bits, 31) & 0x7FFFFFFF)


def _fold_keys(x, op):
    n, w = x.shape
    group = DSA_NACC * SUBLANES
    if n % group == 0 and n > group:
        x = op(x.reshape(n // group, DSA_NACC, SUBLANES, w), axis=0)
        return op(x, axis=0)
    return op(x.reshape(n // SUBLANES, SUBLANES, w), axis=0)


def _dsa_kernel(qat_ref, qit_ref, wit_ref, ga_ref, ka_ref, vat_ref, ki_ref, btab_ref, tri_ref, o_ref,
                slab_ref, dig_ref, s_ref, acc_ref, m_ref, l_ref, *, top_k):
    q, kb, wide = DSA_Q, DSA_KB, DSA_WIDE * DSA_KB
    i = pl.program_id(0)
    nkb = i + 1
    nwide = jnp.maximum(i - 1, 0) // DSA_WIDE
    n0 = nwide * DSA_WIDE
    t_row = i * q + lax.broadcasted_iota(jnp.int32, (1, q), 1)
    key_off = lax.broadcasted_iota(jnp.int32, (kb, 1), 0)
    npair = A_HEADS // 2

    def wide_rows(jw):
        return pl.ds(pl.multiple_of(jw * wide, wide), wide)

    def narrow_rows(j):
        return pl.ds(pl.multiple_of(j * kb, kb), kb)

    def pipelined(first, last, produce, consume):
        def body(t, carry):
            consume(t, (t - first) % 2, carry)
            return produce(jnp.minimum(t + 1, last - 1), (t + 1 - first) % 2)

        lax.fori_loop(first, last, body, produce(first, 0))

    w_head = [wit_ref[h:h + 1, :] for h in range(IDX_HEADS)]

    def scores(rows):
        k_idx = ki_ref[rows, :]
        sc = None
        for h in range(IDX_HEADS):
            s = jnp.dot(k_idx, qit_ref[h * LANES:(h + 1) * LANES, :], preferred_element_type=F32)
            term = w_head[h] * jnp.maximum(s, 0.0)
            sc = term if sc is None else sc + term
        return sc

    def score_wide(jw, c):
        rows = wide_rows(jw)
        slab_ref[rows, :] = _f32_key(scores(rows))
        return c

    def score_narrow(j, c):
        rows = narrow_rows(j)
        sc = jnp.where(j * kb + key_off <= t_row, scores(rows), NEG_INF)
        slab_ref[rows, :] = _f32_key(sc)
        return c

    lax.fori_loop(0, nwide, score_wide, 0)
    lax.fori_loop(n0, nkb, score_narrow, 0)

    def count_ge(trial):
        def cnt(blk):
            return _fold_keys(jnp.where(blk >= trial, 1.0, 0.0), jnp.sum)

        acc = lax.fori_loop(0, nwide, lambda jw, a: a + cnt(slab_ref[wide_rows(jw), :]),
                            jnp.zeros((SUBLANES, q), F32))
        acc = lax.fori_loop(n0, nkb, lambda j, a: a + cnt(slab_ref[narrow_rows(j), :]), acc)
        return jnp.sum(acc, axis=0, keepdims=True).astype(jnp.int32)

    assert DSA_CELLS * SUBLANES >= top_k and wide % (DSA_CELLS * SUBLANES) == 0

    def cells_wide(jw, cm):
        blk = slab_ref[wide_rows(jw), :].reshape(wide // (DSA_CELLS * SUBLANES), DSA_CELLS, SUBLANES, q)
        return jnp.maximum(cm, jnp.max(blk, axis=0))

    def cells_narrow(j, cm):
        blk = slab_ref[narrow_rows(j), :].reshape(kb // SUBLANES, SUBLANES, q)
        return jnp.concatenate([jnp.maximum(cm[:kb // SUBLANES], blk), cm[kb // SUBLANES:]], axis=0)

    cell_max = lax.fori_loop(0, nwide, cells_wide, jnp.full((DSA_CELLS, SUBLANES, q), INT_MIN, jnp.int32))
    cell_max = lax.fori_loop(n0, nkb, cells_narrow, cell_max)
    hi_key = jnp.max(jnp.max(cell_max, axis=0), axis=0, keepdims=True)
    lo_key = jnp.min(jnp.min(cell_max, axis=0), axis=0, keepdims=True)
    same_sign = jnp.min(lo_key ^ hi_key) >= 0
    base = jnp.where(same_sign, lo_key, INT_MIN)
    span = jnp.where(same_sign, hi_key - lo_key, -1)
    nbits = jnp.where(same_sign, jnp.max(32 - lax.clz(span)), 32)

    def plain_search():
        def bisect(b, off):
            trial_off = off + lax.shift_left(jnp.int32(1), nbits - 1 - b)
            in_span = (trial_off ^ INT_MIN) <= (span ^ INT_MIN)
            ok = jnp.logical_and(in_span, count_ge(base + trial_off) >= top_k)
            return jnp.where(ok, trial_off, off)

        return base + lax.fori_loop(0, nbits, bisect, jnp.zeros((1, q), jnp.int32))

    def digit_pass(floor, digit_of):
        def put(rows):
            key = slab_ref[rows, :]
            dig_ref[rows, :] = jnp.where(key < floor, DIGIT_MIN, digit_of(key)).astype(jnp.int16)

        lax.fori_loop(0, nwide, lambda jw, c: put(wide_rows(jw)), None)
        lax.fori_loop(n0, nkb, lambda j, c: put(narrow_rows(j)), None)

    def digit_count_ge(trial):
        t16 = trial.astype(jnp.int16)

        def cnt(blk):
            parts = [jnp.where(blk[r:r + PACKED_SUBLANES] >= t16, jnp.int16(1), jnp.int16(0))
                     for r in range(0, blk.shape[0], PACKED_SUBLANES)]
            while len(parts) > 1:
                parts = [parts[a] + parts[a + 1] for a in range(0, len(parts), 2)]
            return parts[0].astype(jnp.int32)

        acc = lax.fori_loop(0, nwide, lambda jw, a: a + cnt(dig_ref[wide_rows(jw), :]),
                            jnp.zeros((PACKED_SUBLANES, q), jnp.int32))
        acc = lax.fori_loop(n0, nkb, lambda j, a: a + cnt(dig_ref[narrow_rows(j), :]), acc)
        return jnp.sum(acc, axis=0, keepdims=True)

    def digit_search(nb, start):
        def step(b, v):
            trial = v + lax.shift_left(jnp.int32(1), nb - 1 - b)
            return jnp.where(digit_count_ge(trial) >= top_k, trial, v)

        return lax.fori_loop(0, nb, step, start)

    def packed_search():
        digit_pass(base, lambda key: lax.shift_right_arithmetic(key - base, DSA_DIGIT_BITS))
        high = digit_search(jnp.maximum(nbits - DSA_DIGIT_BITS, 0), jnp.zeros((1, q), jnp.int32))
        base2 = base + lax.shift_left(high, DSA_DIGIT_BITS)
        half = 1 << (DSA_DIGIT_BITS - 1)
        digit_pass(base2, lambda key: jnp.minimum(key - (base2 + half), DIGIT_MAX))
        low = digit_search(DSA_DIGIT_BITS, jnp.full((1, q), -half, jnp.int32))
        return base2 + low + half

    packed_ok = jnp.logical_and(same_sign, nbits <= 2 * DSA_DIGIT_BITS)
    thr = lax.cond(packed_ok, packed_search, plain_search)
    tied = jnp.logical_and(count_ge(thr) > top_k, thr > NEG_INF_KEY)
    any_tie = jnp.max(jnp.where(tied, 1, 0)) > 0

    thr_valid = jnp.maximum(thr, NEG_INF_KEY + 1)

    def mask_fast():
        def body(rows):
            slab_ref[rows, :] = jnp.where(slab_ref[rows, :] >= thr_valid, 0, NEG_INF_BITS)

        lax.fori_loop(0, nwide, lambda jw, c: body(wide_rows(jw)), None)
        lax.fori_loop(n0, nkb, lambda j, c: body(narrow_rows(j)), None)

    def mask_tied():
        need = (top_k - count_ge(thr + 1)).astype(F32)

        def body(rows, n, seen):
            key = slab_ref[rows, :]
            eq = key == thr
            eq_f = jnp.where(eq, 1.0, 0.0)
            seen_after = seen + jnp.sum(_fold_keys(eq_f, jnp.sum), axis=0, keepdims=True)
            take_all = seen_after <= need
            crossing = jnp.logical_and(jnp.logical_not(take_all), seen < need)

            def rank_inside():
                before = jnp.dot(tri_ref[0:n, 0:n], eq_f.astype(BF16), preferred_element_type=F32)
                return jnp.where(seen + before < need, 1.0, 0.0)

            picked = lax.cond(jnp.max(jnp.where(crossing, 1, 0)) > 0, rank_inside,
                              lambda: jnp.broadcast_to(jnp.where(take_all, 1.0, 0.0), (n, q)))
            sel = jnp.logical_or(key > thr, jnp.logical_and(eq, picked > 0.0))
            sel = jnp.logical_and(sel, key > NEG_INF_KEY)
            slab_ref[rows, :] = jnp.where(sel, 0, NEG_INF_BITS)
            return seen_after

        seen = lax.fori_loop(0, nwide, lambda jw, c: body(wide_rows(jw), wide, c), jnp.zeros((1, q), F32))
        lax.fori_loop(n0, nkb, lambda j, c: body(narrow_rows(j), kb, c), seen)

    lax.cond(any_tie, mask_tied, mask_fast)

    m_ref[...] = jnp.full(m_ref.shape, M_INIT, F32)
    l_ref[...] = jnp.zeros(l_ref.shape, F32)
    acc_ref[...] = jnp.zeros(acc_ref.shape, F32)

    def logits(rows, n, buf, rel):
        mask = lax.bitcast_convert_type(slab_ref[rows, :], F32)
        mask2 = jnp.concatenate([mask, mask], axis=1)
        tile_max = []
        for p in range(npair):
            ls = slice(p * LANES, (p + 1) * LANES)
            w_q = jnp.concatenate([qat_ref[(2 * p) * LANES:(2 * p + 1) * LANES, :],
                                   qat_ref[(2 * p + 1) * LANES:(2 * p + 2) * LANES, :]], axis=1)
            s = jnp.dot(ka_ref[rows, ls], w_q, preferred_element_type=F32) + mask2
            if rel is not None:
                s = s + btab_ref[rel, p]
            s_ref[buf, p, 0:n, :] = s
            tile_max.append(_fold_keys(s, jnp.max))
        return tuple(tile_max)

    def attend(rows, n, buf, tile_max):
        for p in range(npair):
            ls = slice(p * LANES, (p + 1) * LANES)
            m_old = m_ref[p]
            m_new = jnp.maximum(m_old, jnp.max(tile_max[p], axis=0, keepdims=True))
            alpha = jnp.exp2(m_old - m_new)
            pexp = jnp.exp2(s_ref[buf, p, 0:n, :] - m_new[0:1])
            m_ref[p] = m_new
            l_ref[p] = alpha * l_ref[p] + _fold_keys(pexp, jnp.sum)
            acc_ref[p] = alpha[0:1] * acc_ref[p] + jnp.dot(vat_ref[ls, rows], pexp.astype(BF16),
                                                           preferred_element_type=F32)

    @pl.when(nwide > 0)
    def _():
        pipelined(0, nwide, lambda jw, buf: logits(wide_rows(jw), wide, buf, None),
                  lambda jw, buf, tile_max: attend(wide_rows(jw), wide, buf, tile_max))

    pipelined(n0, nkb, lambda j, buf: logits(narrow_rows(j), kb, buf, jnp.minimum(i - j, 2)),
              lambda j, buf, tile_max: attend(narrow_rows(j), kb, buf, tile_max))

    for p in range(npair):
        o_t = acc_ref[p] / jnp.sum(l_ref[p], axis=0, keepdims=True)
        o_pair = jnp.concatenate([o_t[:A_HEAD_DIM, :q], o_t[A_HEAD_DIM:, q:]], axis=0).T
        ls = slice(p * LANES, (p + 1) * LANES)
        o_ref[:, ls] = (o_pair * _silu(ga_ref[:, ls])).astype(o_ref.dtype)


def _dsa_bias_table(rel_bias):
    ii = np.arange(DSA_Q)[:, None]
    jj = np.arange(DSA_KB)[None, :]
    buckets = np.stack([_t5_bucket_np(ii - jj), _t5_bucket_np(DSA_KB + ii - jj),
                        np.full((DSA_Q, DSA_KB), REL_BUCKETS - 1)])
    assert (_t5_bucket_np(np.arange(DSA_KB + 1, 4 * DSA_KB)) == REL_BUCKETS - 1).all()
    rb = rel_bias.astype(F32) * LOG2E
    rb = rb - rb[REL_BUCKETS - 1]
    onehot = np.zeros((3, DSA_KB, DSA_Q, REL_BUCKETS), np.float32)
    np.put_along_axis(onehot, np.transpose(buckets, (0, 2, 1))[..., None], 1.0, axis=-1)
    tab = jnp.einsum('tsrb,bh->thsr', jnp.asarray(onehot), rb, precision=lax.Precision.HIGHEST)
    tab = tab.reshape(3, A_HEADS // 2, 2, DSA_KB, DSA_Q)
    return jnp.transpose(tab, (0, 1, 3, 2, 4)).reshape(3, A_HEADS // 2, DSA_KB, 2 * DSA_Q)


def _dsa(qa_t, qi_t, wi_t, ga, ka, va_t, ki, btab, top_k):
    s = ga.shape[0]
    q, kb = DSA_Q, DSA_KB
    wide = DSA_WIDE * kb
    tri = jnp.asarray(np.tril(np.ones((wide, wide), np.float32), -1), BF16)
    whole = lambda shape: pl.BlockSpec(shape, lambda i: (0,) * len(shape), pipeline_mode=pl.Buffered(1))
    return pl.pallas_call(
        functools.partial(_dsa_kernel, top_k=top_k),
        grid=(s // q,),
        in_specs=[pl.BlockSpec((A_HEADS * LANES, q), lambda i: (0, i)),
                  pl.BlockSpec((IDX_HEADS * LANES, q), lambda i: (0, i)),
                  pl.BlockSpec((SUBLANES, q), lambda i: (0, i)),
                  pl.BlockSpec((q, A_W), lambda i: (i, 0)),
                  whole((s, A_W)), whole((A_W, s)), whole((s, LANES)),
                  whole((3, A_HEADS // 2, kb, 2 * q)), whole((wide, wide))],
        out_specs=pl.BlockSpec((q, A_W), lambda i: (i, 0)),
        out_shape=jax.ShapeDtypeStruct((s, A_W), BF16),
        scratch_shapes=[pltpu.VMEM((s, q), jnp.int32),
                        pltpu.VMEM((s, q), jnp.int16),
                        pltpu.VMEM((2, A_HEADS // 2, DSA_WIDE * kb, 2 * q), F32),
                        pltpu.VMEM((A_HEADS // 2, LANES, 2 * q), F32),
                        pltpu.VMEM((A_HEADS // 2, SUBLANES, 2 * q), F32),
                        pltpu.VMEM((A_HEADS // 2, SUBLANES, 2 * q), F32)],
        compiler_params=pltpu.CompilerParams(dimension_semantics=("arbitrary",), vmem_limit_bytes=DSA_VMEM_LIMIT),
        name="dsa",
    )(qa_t, qi_t, wi_t, ga, ka, va_t, ki, btab, tri)


GLA_BLOCK = 512


def _gla_kernel(qb_ref, kb_ref, vb_ref, gb_ref, lr_ref, wlr_ref, blr_ref, gn_ref, tril_ref, o_ref,
                state_ref, b_ref, att_ref):
    c, sub = GLA_CHUNK, GLA_SUB
    nsub = c // sub

    @pl.when(pl.program_id(0) == 0)
    def _():
        state_ref[...] = jnp.zeros(state_ref.shape, F32)

    dn_t = (((1,), (1,)), ((), ()))
    dn_0 = (((0,), (0,)), ((), ()))
    hp = lax.Precision.HIGHEST
    row_i = lax.broadcasted_iota(jnp.int32, (sub, sub), 0)
    col_j = lax.broadcasted_iota(jnp.int32, (sub, sub), 1)

    def chunk(ci, carry):
        r0 = pl.multiple_of(ci * c, c)
        z = jnp.dot(lr_ref[pl.ds(r0, c), :], wlr_ref[...], precision=hp, preferred_element_type=F32) + blr_ref[...]
        log_a = (jnp.minimum(z, 0.0) - jnp.log1p(jnp.exp(-jnp.abs(z)))) * (1.0 / GLA_TAU)
        b_ref[...] = jnp.dot(tril_ref[...], log_a, precision=hp, preferred_element_type=F32)
        for h in range(B_HEADS):
            ls = slice(h * LANES, (h + 1) * LANES)
            b = b_ref[:, ls]
            qh = qb_ref[pl.ds(r0, c), ls]
            kh = kb_ref[pl.ds(r0, c), ls]
            vh = vb_ref[pl.ds(r0, c), ls]
            state_t = state_ref[h]
            b_last = b[c - 1:c, :]
            o = lax.dot_general((qh * jnp.exp(b)).astype(BF16), state_t.astype(BF16), dn_t,
                                preferred_element_type=F32)
            for si in range(nsub):
                rs = slice(si * sub, (si + 1) * sub)
                b_i, q_i = b[rs], qh[rs]
                if si > 0:
                    ref_row = b[si * sub:si * sub + 1, :]
                    q_t = (q_i * jnp.exp(b_i - ref_row)).astype(BF16)
                    k_t = (kh[:si * sub] * jnp.exp(ref_row - b[:si * sub])).astype(BF16)
                    att_ref[rs, :si * sub] = lax.dot_general(q_t, k_t, dn_t, preferred_element_type=F32)
                k_i = kh[rs]
                diag = jnp.zeros((sub, sub), F32)
                for jl in range(sub):
                    diff = jnp.minimum(b_i - b_i[jl:jl + 1, :], 0.0)
                    col = jnp.sum(q_i * k_i[jl:jl + 1, :] * jnp.exp(diff), axis=-1, keepdims=True)
                    diag = jnp.where(col_j == jl, col, diag)
                att_ref[rs, rs] = jnp.where(col_j <= row_i, diag, 0.0)
                if si + 1 < nsub:
                    att_ref[rs, (si + 1) * sub:] = jnp.zeros((sub, c - (si + 1) * sub), F32)
            o = o + jnp.dot(att_ref[...].astype(BF16), vh.astype(BF16), preferred_element_type=F32)
            k_dec = (kh * jnp.exp(b_last - b)).astype(BF16)
            state_ref[h] = (jnp.exp(b_last) * state_t
                            + lax.dot_general(vh.astype(BF16), k_dec, dn_0, preferred_element_type=F32))
            o = o * lax.rsqrt(jnp.mean(o * o, axis=-1, keepdims=True) + EPS) * gn_ref[...]
            o_ref[pl.ds(r0, c), ls] = (o * _silu(gb_ref[pl.ds(r0, c), ls])).astype(o_ref.dtype)
        return carry

    lax.fori_loop(0, GLA_BLOCK // c, chunk, 0)


def _gla(qb, kb, vb, gb, lr, wlr_pad, blr_pad, gn):
    s = qb.shape[0]
    blk = min(GLA_BLOCK, s)
    assert blk == GLA_BLOCK and s % blk == 0
    tril = jnp.asarray(np.tril(np.ones((GLA_CHUNK, GLA_CHUNK), np.float32)))
    row = lambda wd: pl.BlockSpec((blk, wd), lambda i: (i, 0))
    const = lambda shape: pl.BlockSpec(shape, lambda i: (0,) * len(shape))
    return pl.pallas_call(
        _gla_kernel,
        grid=(s // blk,),
        in_specs=[row(B_HEADS * LANES), row(B_HEADS * LANES), row(B_VW), row(B_VW), row(LANES),
                  const((LANES, B_HEADS * LANES)), const((1, B_HEADS * LANES)), const((1, B_DV)),
                  const((GLA_CHUNK, GLA_CHUNK))],
        out_specs=row(B_VW),
        out_shape=jax.ShapeDtypeStruct((s, B_VW), BF16),
        scratch_shapes=[pltpu.VMEM((B_HEADS, LANES, B_DV), F32),
                        pltpu.VMEM((GLA_CHUNK, B_HEADS * LANES), F32),
                        pltpu.VMEM((GLA_CHUNK, GLA_CHUNK), F32)],
        compiler_params=pltpu.CompilerParams(dimension_semantics=("arbitrary",), vmem_limit_bytes=VMEM_LIMIT),
        name="gla",
    )(qb, kb, vb, gb, lr, wlr_pad, blr_pad, gn, tril)


def _mid_kernel(x_ref, ma_ref, mb_ref, wo_ref, g_ref, wi_ref, x1_ref, u_ref, sg_ref):
    x1 = (x_ref[...]
          + jnp.dot(ma_ref[...], wo_ref[:A_W, :], preferred_element_type=F32)
          + jnp.dot(mb_ref[...], wo_ref[A_W:, :], preferred_element_type=F32))
    x1_ref[...] = x1
    h = (x1 * lax.rsqrt(jnp.mean(x1 * x1, axis=-1, keepdims=True) + EPS) * g_ref[...]).astype(BF16)
    val = jnp.dot(h, wi_ref[:, :D_MODEL], preferred_element_type=F32)
    glu = jnp.dot(h, wi_ref[:, D_MODEL:2 * D_MODEL], preferred_element_type=F32)
    gate = jnp.dot(h, wi_ref[:, 2 * D_MODEL:], preferred_element_type=F32)
    u_ref[...] = val * jax.nn.sigmoid(glu)
    sg_ref[...] = _silu(gate)


def _mid(x2d, mix_a, mix_b, w_out, g, w_in, tm):
    s = x2d.shape[0]
    row = lambda wd: pl.BlockSpec((tm, wd), lambda i: (i, 0))
    const = lambda shape: pl.BlockSpec(shape, lambda i: (0,) * len(shape))
    return pl.pallas_call(
        _mid_kernel,
        grid=(s // tm,),
        in_specs=[row(D_MODEL), row(A_W), row(B_VW), const((A_W + B_VW, D_MODEL)), const((1, D_MODEL)),
                  const((D_MODEL, 3 * D_MODEL))],
        out_specs=[row(D_MODEL), row(D_MODEL), row(D_MODEL)],
        out_shape=[jax.ShapeDtypeStruct((s, D_MODEL), F32)] * 3,
        compiler_params=pltpu.CompilerParams(dimension_semantics=("arbitrary",), vmem_limit_bytes=VMEM_LIMIT),
        name="mid",
    )(x2d, mix_a, mix_b, w_out, g, w_in)


HALO = 32


def _odd_out_kernel(u_ref, halo_ref, sg_ref, x1_ref, cw_ref, cb_ref, lg_ref, lb_ref, wo_ref, nf_ref, o_ref,
                    ubuf_ref, y_ref, *, tm):
    i = pl.program_id(0)
    ubuf_ref[HALO:, :] = u_ref[...]
    ubuf_ref[:HALO, :] = jnp.where(i > 0, halo_ref[...], 0.0)
    base = HALO - (CONV_WIDTH - 1)
    for cs in range(D_MODEL // LANES):
        ls = slice(cs * LANES, (cs + 1) * LANES)
        acc = jnp.zeros((tm, LANES), F32)
        for k in range(CONV_WIDTH):
            acc = acc + ubuf_ref[base + k:base + k + tm, ls] * cw_ref[k:k + 1, ls]
        y_ref[:, ls] = acc + cb_ref[:, ls]
    y = y_ref[...]
    mu = jnp.mean(y, axis=-1, keepdims=True)
    yc = y - mu
    var = jnp.mean(yc * yc, axis=-1, keepdims=True)
    yn = yc * lax.rsqrt(var + EPS) * lg_ref[...] + lb_ref[...]
    mixed = (_silu(yn) * sg_ref[...]).astype(BF16)
    x2 = x1_ref[...] + jnp.dot(mixed, wo_ref[...], preferred_element_type=F32)
    o_ref[...] = x2 * lax.rsqrt(jnp.mean(x2 * x2, axis=-1, keepdims=True) + EPS) * nf_ref[...]


def _odd_out(u, sg, x1, cw_pad, cb, lg, lb, w_out, nf, tm):
    s = u.shape[0]
    row = pl.BlockSpec((tm, D_MODEL), lambda i: (i, 0))
    const = lambda shape: pl.BlockSpec(shape, lambda i: (0,) * len(shape))
    halo = pl.BlockSpec((HALO, D_MODEL), lambda i: (jnp.maximum(i * (tm // HALO) - 1, 0), 0))
    return pl.pallas_call(
        functools.partial(_odd_out_kernel, tm=tm),
        grid=(s // tm,),
        in_specs=[row, halo, row, row, const((HALO, D_MODEL)), const((1, D_MODEL)), const((1, D_MODEL)),
                  const((1, D_MODEL)), const((D_MODEL, D_MODEL)), const((1, D_MODEL))],
        out_specs=row,
        out_shape=jax.ShapeDtypeStruct((s, D_MODEL), F32),
        scratch_shapes=[pltpu.VMEM((HALO + tm, D_MODEL), F32), pltpu.VMEM((tm, D_MODEL), F32)],
        compiler_params=pltpu.CompilerParams(dimension_semantics=("arbitrary",), vmem_limit_bytes=VMEM_LIMIT),
        name="odd_out",
    )(u, u, sg, x1, cw_pad, cb, lg, lb, w_out, nf)


def kernel(x, rel_bias, norm_even, w_in_even, w_gla_lr, b_gla_lr, gla_norm, w_out_even, norm_odd, w_in_odd,
           conv_w, conv_b, conv_ln_g, conv_ln_b, w_out_odd, norm_f):
    bsz, s, d = x.shape
    assert bsz == 1 and d == D_MODEL and s % GLA_BLOCK == 0
    assert norm_even.shape[0] == 1 and norm_odd.shape[0] == 1
    top_k = min(TOPK_MAX, s // 4)
    tm = 256
    x2d = x.reshape(s, d)

    wt_pad = _even_in_weights(w_in_even[0])
    qa_t, ka, va_t, ga, qi_t, ki, wi_t, qb, kb, vb, gb, lr = _even_in(x2d, norm_even[0].reshape(1, d), wt_pad, tm)
    mix_a = _dsa(qa_t, qi_t, wi_t, ga, ka, va_t, ki, _dsa_bias_table(rel_bias), top_k)

    head_pad = ((0, 0), (0, 0), (0, LANES - B_DK))
    wlr_pad = jnp.pad(w_gla_lr[0].reshape(GLA_RANK, B_HEADS, B_DK), head_pad).reshape(GLA_RANK, B_HEADS * LANES)
    wlr_pad = jnp.pad(wlr_pad, ((0, LANES - GLA_RANK), (0, 0)))
    blr_pad = jnp.pad(b_gla_lr[0].reshape(1, B_HEADS, B_DK), head_pad).reshape(1, B_HEADS * LANES)
    mix_b = _gla(qb, kb, vb, gb, lr, wlr_pad, blr_pad, gla_norm[0].reshape(1, B_DV))

    x1, u, sg = _mid(x2d, mix_a, mix_b, w_out_even[0].astype(BF16), norm_odd[0].reshape(1, d),
                     w_in_odd[0].astype(BF16), tm)

    cw_pad = jnp.zeros((HALO, d), F32).at[:CONV_WIDTH].set(conv_w[0])
    out = _odd_out(u, sg, x1, cw_pad, conv_b[0].reshape(1, d), conv_ln_g[0].reshape(1, d),
                   conv_ln_b[0].reshape(1, d), w_out_odd[0].astype(BF16), norm_f.reshape(1, d), tm)
    return out.reshape(bsz, s, d)
```

```python
import functools
import math

import jax
import jax.numpy as jnp
import numpy as np
from jax import lax
from jax.experimental import pallas as pl
from jax.experimental.pallas import tpu as pltpu

D_MODEL = 1024
A_HEADS = 8
A_HEAD_DIM = 64
IDX_HEADS = 4
IDX_DIM = 64
TOPK_MAX = 256
B_HEADS = 4
B_DK = 64
B_DV = 128
GLA_RANK = 16
GLA_TAU = 16.0
GLA_CHUNK = 64
GLA_SUB = 16
CONV_WIDTH = 31
REL_BUCKETS = 32
REL_MAX_EXACT = 16
REL_MAX_DIST = 128
EPS = 1e-6

A_W = A_HEADS * A_HEAD_DIM
B_KW = B_HEADS * B_DK
B_VW = B_HEADS * B_DV

LANES = 128
SUBLANES = 8
VMEM_LIMIT = 56 * 1024 * 1024

F32 = jnp.float32
BF16 = jnp.bfloat16
NEG_INF = float("-inf")
INT_MIN = -(2 ** 31)
NEG_INF_BITS = int(np.array(-np.inf, np.float32).view(np.int32))
NEG_INF_KEY = NEG_INF_BITS ^ 0x7FFFFFFF
M_INIT = -1e30
LOG2E = math.log2(math.e)

C_QA = 0
C_KA = C_QA + A_HEADS * LANES
C_VA = C_KA + A_W
C_GA = C_VA + A_W
C_QI = C_GA + A_W
C_KI = C_QI + IDX_HEADS * LANES
C_WI = C_KI + LANES
C_QB = C_WI + LANES
C_KB = C_QB + B_HEADS * LANES
C_VB = C_KB + B_HEADS * LANES
C_GB = C_VB + B_VW
C_LR = C_GB + B_VW
C_END = C_LR + LANES


def _t5_bucket_np(dist):
    dist = np.maximum(dist, 0)
    d = np.maximum(dist, 1).astype(np.float32)
    large = REL_MAX_EXACT + (np.log(d / REL_MAX_EXACT) / math.log(REL_MAX_DIST / REL_MAX_EXACT)
                             * (REL_BUCKETS - REL_MAX_EXACT)).astype(np.int32)
    large = np.minimum(large, REL_BUCKETS - 1)
    return np.where(dist < REL_MAX_EXACT, dist, large)


def _silu(x):
    return x * jax.nn.sigmoid(x)


def _even_in_kernel(x_ref, g_ref, wt_ref, qat_ref, ka_ref, vat_ref, ga_ref, qit_ref, ki_ref, wit_ref,
                    qb_ref, kb_ref, vb_ref, gb_ref, lr_ref):
    x = x_ref[...]
    h = (x * lax.rsqrt(jnp.mean(x * x, axis=-1, keepdims=True) + EPS) * g_ref[...]).astype(BF16)
    dn_t = (((1,), (1,)), ((), ()))

    def proj(lo, hi):
        return lax.dot_general(h, wt_ref[lo:hi, :], dn_t, preferred_element_type=F32)

    def proj_t(lo, hi):
        return lax.dot_general(wt_ref[lo:hi, :], h, dn_t, preferred_element_type=F32)

    qat_ref[...] = proj_t(C_QA, C_KA).astype(BF16)
    ka_ref[...] = proj(C_KA, C_VA).astype(BF16)
    vat_ref[...] = proj_t(C_VA, C_GA).astype(BF16)
    ga_ref[...] = proj(C_GA, C_QI)
    qit_ref[...] = proj_t(C_QI, C_KI).astype(BF16)
    ki_ref[...] = proj(C_KI, C_WI).astype(BF16)
    wit_ref[...] = proj_t(C_WI, C_WI + SUBLANES)
    qb_ref[...] = proj(C_QB, C_KB)
    kb_ref[...] = proj(C_KB, C_VB)
    vb_ref[...] = proj(C_VB, C_GB)
    gb_ref[...] = proj(C_GB, C_LR)
    lr_ref[...] = proj(C_LR, C_END)


def _even_in_layout():
    cols = np.cumsum([0, A_W, A_W, A_W, A_W, IDX_HEADS * IDX_DIM, IDX_DIM, IDX_HEADS,
                      B_KW, B_KW, B_VW, B_VW, GLA_RANK])
    (s_qa, s_ka, s_va, s_ga, s_qi, s_ki, s_wi, s_qb, s_kb, s_vb, s_gb, s_lr, _) = [int(c) for c in cols]
    src = np.full((C_END,), -1, np.int32)
    scale = np.ones((C_END,), np.float32)

    def put(dst, start, width, sc=1.0):
        src[dst:dst + width] = np.arange(start, start + width)
        scale[dst:dst + width] = sc

    for h in range(A_HEADS):
        put(C_QA + h * LANES + (h % 2) * A_HEAD_DIM, s_qa + h * A_HEAD_DIM, A_HEAD_DIM, (A_HEAD_DIM ** -0.5) * LOG2E)
    put(C_KA, s_ka, A_W)
    put(C_VA, s_va, A_W)
    put(C_GA, s_ga, A_W)
    for h in range(IDX_HEADS):
        put(C_QI + h * LANES, s_qi + h * IDX_DIM, IDX_DIM)
    put(C_KI, s_ki, IDX_DIM)
    put(C_WI, s_wi, IDX_HEADS, (IDX_DIM ** -0.5) * (IDX_HEADS ** -0.5))
    for h in range(B_HEADS):
        put(C_QB + h * LANES, s_qb + h * B_DK, B_DK, B_DK ** -0.5)
        put(C_KB + h * LANES, s_kb + h * B_DK, B_DK)
    put(C_VB, s_vb, B_VW)
    put(C_GB, s_gb, B_VW)
    put(C_LR, s_lr, GLA_RANK)
    return src, scale


def _even_in_weights(w):
    src, scale = _even_in_layout()
    rows = jnp.take(w.T, jnp.asarray(np.maximum(src, 0)), axis=0)
    return (rows * jnp.asarray(np.where(src >= 0, scale, 0.0))[:, None]).astype(BF16)


def _even_in(x2d, g, wt_pad, tm):
    s = x2d.shape[0]
    row = lambda wd, dt: (pl.BlockSpec((tm, wd), lambda i: (i, 0)), jax.ShapeDtypeStruct((s, wd), dt))
    col = lambda ht, dt: (pl.BlockSpec((ht, tm), lambda i: (0, i)), jax.ShapeDtypeStruct((ht, s), dt))
    outs = [col(C_KA - C_QA, BF16), row(A_W, BF16), col(A_W, BF16), row(A_W, F32), col(C_KI - C_QI, BF16),
            row(LANES, BF16), col(SUBLANES, F32), row(C_KB - C_QB, F32), row(C_VB - C_KB, F32), row(B_VW, F32),
            row(B_VW, F32), row(LANES, F32)]
    return pl.pallas_call(
        _even_in_kernel,
        grid=(s // tm,),
        in_specs=[pl.BlockSpec((tm, D_MODEL), lambda i: (i, 0)),
                  pl.BlockSpec((1, D_MODEL), lambda i: (0, 0)),
                  pl.BlockSpec((C_END, D_MODEL), lambda i: (0, 0))],
        out_specs=[spec for spec, _ in outs],
        out_shape=[shape for _, shape in outs],
        compiler_params=pltpu.CompilerParams(dimension_semantics=("arbitrary",), vmem_limit_bytes=VMEM_LIMIT),
        name="even_in",
    )(x2d, g, wt_pad)


DSA_Q = 128
DSA_KB = 128
DSA_WIDE = 4
DSA_NACC = 4
DSA_CELLS = 64
DSA_DIGIT_BITS = 15
DIGIT_MIN, DIGIT_MAX = -(2 ** 15), 2 ** 15 - 1
PACKED_SUBLANES = 16
DSA_VMEM_LIMIT = 58 * 1024 * 1024


def _f32_key(x):
    bits = lax.bitcast_convert_type(x, jnp.int32)
    return bits ^ (lax.shift_right_arithmetic(bits, 31) & 0x7FFFFFFF)


def _fold_keys(x, op):
    n, w = x.shape
    group = DSA_NACC * SUBLANES
    if n % group == 0 and n > group:
        x = op(x.reshape(n // group, DSA_NACC, SUBLANES, w), axis=0)
        return op(x, axis=0)
    return op(x.reshape(n // SUBLANES, SUBLANES, w), axis=0)


def _dsa_kernel(qat_ref, qit_ref, wit_ref, ga_ref, ka_ref, vat_ref, ki_ref, btab_ref, tri_ref, o_ref,
                slab_ref, dig_ref, s_ref, acc_ref, m_ref, l_ref, *, top_k):
    q, kb, wide = DSA_Q, DSA_KB, DSA_WIDE * DSA_KB
    i = pl.program_id(0)
    nkb = i + 1
    nwide = jnp.maximum(i - 1, 0) // DSA_WIDE
    n0 = nwide * DSA_WIDE
    t_row = i * q + lax.broadcasted_iota(jnp.int32, (1, q), 1)
    key_off = lax.broadcasted_iota(jnp.int32, (kb, 1), 0)
    npair = A_HEADS // 2

    def wide_rows(jw):
        return pl.ds(pl.multiple_of(jw * wide, wide), wide)

    def narrow_rows(j):
        return pl.ds(pl.multiple_of(j * kb, kb), kb)

    def pipelined(first, last, produce, consume):
        def body(t, carry):
            consume(t, (t - first) % 2, carry)
            return produce(jnp.minimum(t + 1, last - 1), (t + 1 - first) % 2)

        lax.fori_loop(first, last, body, produce(first, 0))

    w_head = [wit_ref[h:h + 1, :] for h in range(IDX_HEADS)]

    def scores(rows):
        k_idx = ki_ref[rows, :]
        sc = None
        for h in range(IDX_HEADS):
            s = jnp.dot(k_idx, qit_ref[h * LANES:(h + 1) * LANES, :], preferred_element_type=F32)
            term = w_head[h] * jnp.maximum(s, 0.0)
            sc = term if sc is None else sc + term
        return sc

    def score_wide(jw, c):
        rows = wide_rows(jw)
        slab_ref[rows, :] = _f32_key(scores(rows))
        return c

    def score_narrow(j, c):
        rows = narrow_rows(j)
        sc = jnp.where(j * kb + key_off <= t_row, scores(rows), NEG_INF)
        slab_ref[rows, :] = _f32_key(sc)
        return c

    lax.fori_loop(0, nwide, score_wide, 0)
    lax.fori_loop(n0, nkb, score_narrow, 0)

    def count_ge(trial):
        def cnt(blk):
            return _fold_keys(jnp.where(blk >= trial, 1.0, 0.0), jnp.sum)

        acc = lax.fori_loop(0, nwide, lambda jw, a: a + cnt(slab_ref[wide_rows(jw), :]),
                            jnp.zeros((SUBLANES, q), F32))
        acc = lax.fori_loop(n0, nkb, lambda j, a: a + cnt(slab_ref[narrow_rows(j), :]), acc)
        return jnp.sum(acc, axis=0, keepdims=True).astype(jnp.int32)

    assert DSA_CELLS * SUBLANES >= top_k and wide % (DSA_CELLS * SUBLANES) == 0

    def cells_wide(jw, cm):
        blk = slab_ref[wide_rows(jw), :].reshape(wide // (DSA_CELLS * SUBLANES), DSA_CELLS, SUBLANES, q)
        return jnp.maximum(cm, jnp.max(blk, axis=0))

    def cells_narrow(j, cm):
        blk = slab_ref[narrow_rows(j), :].reshape(kb // SUBLANES, SUBLANES, q)
        return jnp.concatenate([jnp.maximum(cm[:kb // SUBLANES], blk), cm[kb // SUBLANES:]], axis=0)

    cell_max = lax.fori_loop(0, nwide, cells_wide, jnp.full((DSA_CELLS, SUBLANES, q), INT_MIN, jnp.int32))
    cell_max = lax.fori_loop(n0, nkb, cells_narrow, cell_max)
    hi_key = jnp.max(jnp.max(cell_max, axis=0), axis=0, keepdims=True)

    def cell_bisect(b, c):
        trial = c + lax.shift_left(jnp.int32(1), 31 - b)
        cnt = jnp.sum(jnp.sum(jnp.where(cell_max >= trial, 1.0, 0.0), axis=0), axis=0, keepdims=True)
        return jnp.where(cnt >= top_k, trial, c)

    lo_key = lax.fori_loop(0, 32, cell_bisect, jnp.full((1, q), INT_MIN, jnp.int32))
    same_sign = jnp.min(lo_key ^ hi_key) >= 0
    base = jnp.where(same_sign, lo_key, INT_MIN)
    span = jnp.where(same_sign, hi_key - lo_key, -1)
    nbits = jnp.where(same_sign, jnp.max(32 - lax.clz(span)), 32)

    def plain_search():
        def bisect(b, off):
            trial_off = off + lax.shift_left(jnp.int32(1), nbits - 1 - b)
            in_span = (trial_off ^ INT_MIN) <= (span ^ INT_MIN)
            ok = jnp.logical_and(in_span, count_ge(base + trial_off) >= top_k)
            return jnp.where(ok, trial_off, off)

        return base + lax.fori_loop(0, nbits, bisect, jnp.zeros((1, q), jnp.int32))

    def digit_pass(floor, digit_of):
        def put(rows):
            key = slab_ref[rows, :]
            dig_ref[rows, :] = jnp.where(key < floor, DIGIT_MIN, digit_of(key)).astype(jnp.int16)

        lax.fori_loop(0, nwide, lambda jw, c: put(wide_rows(jw)), None)
        lax.fori_loop(n0, nkb, lambda j, c: put(narrow_rows(j)), None)

    def digit_count_ge(trial):
        t16 = trial.astype(jnp.int16)

        def cnt(blk):
            parts = [jnp.where(blk[r:r + PACKED_SUBLANES] >= t16, jnp.int16(1), jnp.int16(0))
                     for r in range(0, blk.shape[0], PACKED_SUBLANES)]
            while len(parts) > 1:
                parts = [parts[a] + parts[a + 1] for a in range(0, len(parts), 2)]
            return parts[0].astype(jnp.int32)

        acc = lax.fori_loop(0, nwide, lambda jw, a: a + cnt(dig_ref[wide_rows(jw), :]),
                            jnp.zeros((PACKED_SUBLANES, q), jnp.int32))
        acc = lax.fori_loop(n0, nkb, lambda j, a: a + cnt(dig_ref[narrow_rows(j), :]), acc)
        return jnp.sum(acc, axis=0, keepdims=True)

    def digit_search(nb, start):
        def step(b, v):
            trial = v + lax.shift_left(jnp.int32(1), nb - 1 - b)
            return jnp.where(digit_count_ge(trial) >= top_k, trial, v)

        return lax.fori_loop(0, nb, step, start)

    def packed_search():
        digit_pass(base, lambda key: lax.shift_right_arithmetic(key - base, DSA_DIGIT_BITS))
        high = digit_search(jnp.maximum(nbits - DSA_DIGIT_BITS, 0), jnp.zeros((1, q), jnp.int32))
        base2 = base + lax.shift_left(high, DSA_DIGIT_BITS)
        half = 1 << (DSA_DIGIT_BITS - 1)
        digit_pass(base2, lambda key: jnp.minimum(key - (base2 + half), DIGIT_MAX))
        low = digit_search(DSA_DIGIT_BITS, jnp.full((1, q), -half, jnp.int32))
        return base2 + low + half

    packed_ok = jnp.logical_and(same_sign, nbits <= 2 * DSA_DIGIT_BITS)
    thr = lax.cond(packed_ok, packed_search, plain_search)
    tied = jnp.logical_and(count_ge(thr) > top_k, thr > NEG_INF_KEY)
    any_tie = jnp.max(jnp.where(tied, 1, 0)) > 0

    thr_valid = jnp.maximum(thr, NEG_INF_KEY + 1)

    def mask_fast():
        def body(rows):
            slab_ref[rows, :] = jnp.where(slab_ref[rows, :] >= thr_valid, 0, NEG_INF_BITS)

        lax.fori_loop(0, nwide, lambda jw, c: body(wide_rows(jw)), None)
        lax.fori_loop(n0, nkb, lambda j, c: body(narrow_rows(j)), None)

    def mask_tied():
        need = (top_k - count_ge(thr + 1)).astype(F32)

        def body(rows, n, seen):
            key = slab_ref[rows, :]
            eq = key == thr
            eq_f = jnp.where(eq, 1.0, 0.0)
            seen_after = seen + jnp.sum(_fold_keys(eq_f, jnp.sum), axis=0, keepdims=True)
            take_all = seen_after <= need
            crossing = jnp.logical_and(jnp.logical_not(take_all), seen < need)

            def rank_inside():
                before = jnp.dot(tri_ref[0:n, 0:n], eq_f.astype(BF16), preferred_element_type=F32)
                return jnp.where(seen + before < need, 1.0, 0.0)

            picked = lax.cond(jnp.max(jnp.where(crossing, 1, 0)) > 0, rank_inside,
                              lambda: jnp.broadcast_to(jnp.where(take_all, 1.0, 0.0), (n, q)))
            sel = jnp.logical_or(key > thr, jnp.logical_and(eq, picked > 0.0))
            sel = jnp.logical_and(sel, key > NEG_INF_KEY)
            slab_ref[rows, :] = jnp.where(sel, 0, NEG_INF_BITS)
            return seen_after

        seen = lax.fori_loop(0, nwide, lambda jw, c: body(wide_rows(jw), wide, c), jnp.zeros((1, q), F32))
        lax.fori_loop(n0, nkb, lambda j, c: body(narrow_rows(j), kb, c), seen)

    lax.cond(any_tie, mask_tied, mask_fast)

    m_ref[...] = jnp.full(m_ref.shape, M_INIT, F32)
    l_ref[...] = jnp.zeros(l_ref.shape, F32)
    acc_ref[...] = jnp.zeros(acc_ref.shape, F32)

    def logits(rows, n, buf, rel):
        mask = lax.bitcast_convert_type(slab_ref[rows, :], F32)
        mask2 = jnp.concatenate([mask, mask], axis=1)
        tile_max = []
        for p in range(npair):
            ls = slice(p * LANES, (p + 1) * LANES)
            w_q = jnp.concatenate([qat_ref[(2 * p) * LANES:(2 * p + 1) * LANES, :],
                                   qat_ref[(2 * p + 1) * LANES:(2 * p + 2) * LANES, :]], axis=1)
            s = jnp.dot(ka_ref[rows, ls], w_q, preferred_element_type=F32) + mask2
            if rel is not None:
                s = s + btab_ref[rel, p]
            s_ref[buf, p, 0:n, :] = s
            tile_max.append(_fold_keys(s, jnp.max))
        return tuple(tile_max)

    def attend(rows, n, buf, tile_max):
        for p in range(npair):
            ls = slice(p * LANES, (p + 1) * LANES)
            m_old = m_ref[p]
            m_new = jnp.maximum(m_old, jnp.max(tile_max[p], axis=0, keepdims=True))
            alpha = jnp.exp2(m_old - m_new)
            pexp = jnp.exp2(s_ref[buf, p, 0:n, :] - m_new[0:1])
            m_ref[p] = m_new
            l_ref[p] = alpha * l_ref[p] + _fold_keys(pexp, jnp.sum)
            acc_ref[p] = alpha[0:1] * acc_ref[p] + jnp.dot(vat_ref[ls, rows], pexp.astype(BF16),
                                                           preferred_element_type=F32)

    @pl.when(nwide > 0)
    def _():
        pipelined(0, nwide, lambda jw, buf: logits(wide_rows(jw), wide, buf, None),
                  lambda jw, buf, tile_max: attend(wide_rows(jw), wide, buf, tile_max))

    pipelined(n0, nkb, lambda j, buf: logits(narrow_rows(j), kb, buf, jnp.minimum(i - j, 2)),
              lambda j, buf, tile_max: attend(narrow_rows(j), kb, buf, tile_max))

    for p in range(npair):
        o_t = acc_ref[p] / jnp.sum(l_ref[p], axis=0, keepdims=True)
        o_pair = jnp.concatenate([o_t[:A_HEAD_DIM, :q], o_t[A_HEAD_DIM:, q:]], axis=0).T
        ls = slice(p * LANES, (p + 1) * LANES)
        o_ref[:, ls] = (o_pair * _silu(ga_ref[:, ls])).astype(o_ref.dtype)


def _dsa_bias_table(rel_bias):
    ii = np.arange(DSA_Q)[:, None]
    jj = np.arange(DSA_KB)[None, :]
    buckets = np.stack([_t5_bucket_np(ii - jj), _t5_bucket_np(DSA_KB + ii - jj),
                        np.full((DSA_Q, DSA_KB), REL_BUCKETS - 1)])
    assert (_t5_bucket_np(np.arange(DSA_KB + 1, 4 * DSA_KB)) == REL_BUCKETS - 1).all()
    rb = rel_bias.astype(F32) * LOG2E
    rb = rb - rb[REL_BUCKETS - 1]
    onehot = np.zeros((3, DSA_KB, DSA_Q, REL_BUCKETS), np.float32)
    np.put_along_axis(onehot, np.transpose(buckets, (0, 2, 1))[..., None], 1.0, axis=-1)
    tab = jnp.einsum('tsrb,bh->thsr', jnp.asarray(onehot), rb, precision=lax.Precision.HIGHEST)
    tab = tab.reshape(3, A_HEADS // 2, 2, DSA_KB, DSA_Q)
    return jnp.transpose(tab, (0, 1, 3, 2, 4)).reshape(3, A_HEADS // 2, DSA_KB, 2 * DSA_Q)


def _dsa(qa_t, qi_t, wi_t, ga, ka, va_t, ki, btab, top_k):
    s = ga.shape[0]
    q, kb = DSA_Q, DSA_KB
    wide = DSA_WIDE * kb
    tri = jnp.asarray(np.tril(np.ones((wide, wide), np.float32), -1), BF16)
    whole = lambda shape: pl.BlockSpec(shape, lambda i: (0,) * len(shape), pipeline_mode=pl.Buffered(1))
    return pl.pallas_call(
        functools.partial(_dsa_kernel, top_k=top_k),
        grid=(s // q,),
        in_specs=[pl.BlockSpec((A_HEADS * LANES, q), lambda i: (0, i)),
                  pl.BlockSpec((IDX_HEADS * LANES, q), lambda i: (0, i)),
                  pl.BlockSpec((SUBLANES, q), lambda i: (0, i)),
                  pl.BlockSpec((q, A_W), lambda i: (i, 0)),
                  whole((s, A_W)), whole((A_W, s)), whole((s, LANES)),
                  whole((3, A_HEADS // 2, kb, 2 * q)), whole((wide, wide))],
        out_specs=pl.BlockSpec((q, A_W), lambda i: (i, 0)),
        out_shape=jax.ShapeDtypeStruct((s, A_W), BF16),
        scratch_shapes=[pltpu.VMEM((s, q), jnp.int32),
                        pltpu.VMEM((s, q), jnp.int16),
                        pltpu.VMEM((2, A_HEADS // 2, DSA_WIDE * kb, 2 * q), F32),
                        pltpu.VMEM((A_HEADS // 2, LANES, 2 * q), F32),
                        pltpu.VMEM((A_HEADS // 2, SUBLANES, 2 * q), F32),
                        pltpu.VMEM((A_HEADS // 2, SUBLANES, 2 * q), F32)],
        compiler_params=pltpu.CompilerParams(dimension_semantics=("arbitrary",), vmem_limit_bytes=DSA_VMEM_LIMIT),
        name="dsa",
    )(qa_t, qi_t, wi_t, ga, ka, va_t, ki, btab, tri)


GLA_BLOCK = 512


def _gla_kernel(qb_ref, kb_ref, vb_ref, gb_ref, lr_ref, wlr_ref, blr_ref, gn_ref, tril_ref, o_ref,
                state_ref, b_ref, att_ref):
    c, sub = GLA_CHUNK, GLA_SUB
    nsub = c // sub

    @pl.when(pl.program_id(0) == 0)
    def _():
        state_ref[...] = jnp.zeros(state_ref.shape, F32)

    dn_t = (((1,), (1,)), ((), ()))
    dn_0 = (((0,), (0,)), ((), ()))
    hp = lax.Precision.HIGHEST
    row_i = lax.broadcasted_iota(jnp.int32, (sub, sub), 0)
    col_j = lax.broadcasted_iota(jnp.int32, (sub, sub), 1)

    def chunk(ci, carry):
        r0 = pl.multiple_of(ci * c, c)
        z = jnp.dot(lr_ref[pl.ds(r0, c), :], wlr_ref[...], precision=hp, preferred_element_type=F32) + blr_ref[...]
        log_a = (jnp.minimum(z, 0.0) - jnp.log1p(jnp.exp(-jnp.abs(z)))) * (1.0 / GLA_TAU)
        b_ref[...] = jnp.dot(tril_ref[...], log_a, precision=hp, preferred_element_type=F32)
        for h in range(B_HEADS):
            ls = slice(h * LANES, (h + 1) * LANES)
            b = b_ref[:, ls]
            qh = qb_ref[pl.ds(r0, c), ls]
            kh = kb_ref[pl.ds(r0, c), ls]
            vh = vb_ref[pl.ds(r0, c), ls]
            state_t = state_ref[h]
            b_last = b[c - 1:c, :]
            o = lax.dot_general((qh * jnp.exp(b)).astype(BF16), state_t.astype(BF16), dn_t,
                                preferred_element_type=F32)
            for si in range(nsub):
                rs = slice(si * sub, (si + 1) * sub)
                b_i, q_i = b[rs], qh[rs]
                if si > 0:
                    ref_row = b[si * sub:si * sub + 1, :]
                    q_t = (q_i * jnp.exp(b_i - ref_row)).astype(BF16)
                    k_t = (kh[:si * sub] * jnp.exp(ref_row - b[:si * sub])).astype(BF16)
                    att_ref[rs, :si * sub] = lax.dot_general(q_t, k_t, dn_t, preferred_element_type=F32)
                k_i = kh[rs]
                diag = jnp.zeros((sub, sub), F32)
                for jl in range(sub):
                    diff = jnp.minimum(b_i - b_i[jl:jl + 1, :], 0.0)
                    col = jnp.sum(q_i * k_i[jl:jl + 1, :] * jnp.exp(diff), axis=-1, keepdims=True)
                    diag = jnp.where(col_j == jl, col, diag)
                att_ref[rs, rs] = jnp.where(col_j <= row_i, diag, 0.0)
                if si + 1 < nsub:
                    att_ref[rs, (si + 1) * sub:] = jnp.zeros((sub, c - (si + 1) * sub), F32)
            o = o + jnp.dot(att_ref[...].astype(BF16), vh.astype(BF16), preferred_element_type=F32)
            k_dec = (kh * jnp.exp(b_last - b)).astype(BF16)
            state_ref[h] = (jnp.exp(b_last) * state_t
                            + lax.dot_general(vh.astype(BF16), k_dec, dn_0, preferred_element_type=F32))
            o = o * lax.rsqrt(jnp.mean(o * o, axis=-1, keepdims=True) + EPS) * gn_ref[...]
            o_ref[pl.ds(r0, c), ls] = (o * _silu(gb_ref[pl.ds(r0, c), ls])).astype(o_ref.dtype)
        return carry

    lax.fori_loop(0, GLA_BLOCK // c, chunk, 0)


def _gla(qb, kb, vb, gb, lr, wlr_pad, blr_pad, gn):
    s = qb.shape[0]
    blk = min(GLA_BLOCK, s)
    assert blk == GLA_BLOCK and s % blk == 0
    tril = jnp.asarray(np.tril(np.ones((GLA_CHUNK, GLA_CHUNK), np.float32)))
    row = lambda wd: pl.BlockSpec((blk, wd), lambda i: (i, 0))
    const = lambda shape: pl.BlockSpec(shape, lambda i: (0,) * len(shape))
    return pl.pallas_call(
        _gla_kernel,
        grid=(s // blk,),
        in_specs=[row(B_HEADS * LANES), row(B_HEADS * LANES), row(B_VW), row(B_VW), row(LANES),
                  const((LANES, B_HEADS * LANES)), const((1, B_HEADS * LANES)), const((1, B_DV)),
                  const((GLA_CHUNK, GLA_CHUNK))],
        out_specs=row(B_VW),
        out_shape=jax.ShapeDtypeStruct((s, B_VW), BF16),
        scratch_shapes=[pltpu.VMEM((B_HEADS, LANES, B_DV), F32),
                        pltpu.VMEM((GLA_CHUNK, B_HEADS * LANES), F32),
                        pltpu.VMEM((GLA_CHUNK, GLA_CHUNK), F32)],
        compiler_params=pltpu.CompilerParams(dimension_semantics=("arbitrary",), vmem_limit_bytes=VMEM_LIMIT),
        name="gla",
    )(qb, kb, vb, gb, lr, wlr_pad, blr_pad, gn, tril)


def _mid_kernel(x_ref, ma_ref, mb_ref, wo_ref, g_ref, wi_ref, x1_ref, u_ref, sg_ref):
    x1 = (x_ref[...]
          + jnp.dot(ma_ref[...], wo_ref[:A_W, :], preferred_element_type=F32)
          + jnp.dot(mb_ref[...], wo_ref[A_W:, :], preferred_element_type=F32))
    x1_ref[...] = x1
    h = (x1 * lax.rsqrt(jnp.mean(x1 * x1, axis=-1, keepdims=True) + EPS) * g_ref[...]).astype(BF16)
    val = jnp.dot(h, wi_ref[:, :D_MODEL], preferred_element_type=F32)
    glu = jnp.dot(h, wi_ref[:, D_MODEL:2 * D_MODEL], preferred_element_type=F32)
    gate = jnp.dot(h, wi_ref[:, 2 * D_MODEL:], preferred_element_type=F32)
    u_ref[...] = val * jax.nn.sigmoid(glu)
    sg_ref[...] = _silu(gate)


def _mid(x2d, mix_a, mix_b, w_out, g, w_in, tm):
    s = x2d.shape[0]
    row = lambda wd: pl.BlockSpec((tm, wd), lambda i: (i, 0))
    const = lambda shape: pl.BlockSpec(shape, lambda i: (0,) * len(shape))
    return pl.pallas_call(
        _mid_kernel,
        grid=(s // tm,),
        in_specs=[row(D_MODEL), row(A_W), row(B_VW), const((A_W + B_VW, D_MODEL)), const((1, D_MODEL)),
                  const((D_MODEL, 3 * D_MODEL))],
        out_specs=[row(D_MODEL), row(D_MODEL), row(D_MODEL)],
        out_shape=[jax.ShapeDtypeStruct((s, D_MODEL), F32)] * 3,
        compiler_params=pltpu.CompilerParams(dimension_semantics=("arbitrary",), vmem_limit_bytes=VMEM_LIMIT),
        name="mid",
    )(x2d, mix_a, mix_b, w_out, g, w_in)


HALO = 32


def _odd_out_kernel(u_ref, halo_ref, sg_ref, x1_ref, cw_ref, cb_ref, lg_ref, lb_ref, wo_ref, nf_ref, o_ref,
                    ubuf_ref, y_ref, *, tm):
    i = pl.program_id(0)
    ubuf_ref[HALO:, :] = u_ref[...]
    ubuf_ref[:HALO, :] = jnp.where(i > 0, halo_ref[...], 0.0)
    base = HALO - (CONV_WIDTH - 1)
    for cs in range(D_MODEL // LANES):
        ls = slice(cs * LANES, (cs + 1) * LANES)
        acc = jnp.zeros((tm, LANES), F32)
        for k in range(CONV_WIDTH):
            acc = acc + ubuf_ref[base + k:base + k + tm, ls] * cw_ref[k:k + 1, ls]
        y_ref[:, ls] = acc + cb_ref[:, ls]
    y = y_ref[...]
    mu = jnp.mean(y, axis=-1, keepdims=True)
    yc = y - mu
    var = jnp.mean(yc * yc, axis=-1, keepdims=True)
    yn = yc * lax.rsqrt(var + EPS) * lg_ref[...] + lb_ref[...]
    mixed = (_silu(yn) * sg_ref[...]).astype(BF16)
    x2 = x1_ref[...] + jnp.dot(mixed, wo_ref[...], preferred_element_type=F32)
    o_ref[...] = x2 * lax.rsqrt(jnp.mean(x2 * x2, axis=-1, keepdims=True) + EPS) * nf_ref[...]


def _odd_out(u, sg, x1, cw_pad, cb, lg, lb, w_out, nf, tm):
    s = u.shape[0]
    row = pl.BlockSpec((tm, D_MODEL), lambda i: (i, 0))
    const = lambda shape: pl.BlockSpec(shape, lambda i: (0,) * len(shape))
    halo = pl.BlockSpec((HALO, D_MODEL), lambda i: (jnp.maximum(i * (tm // HALO) - 1, 0), 0))
    return pl.pallas_call(
        functools.partial(_odd_out_kernel, tm=tm),
        grid=(s // tm,),
        in_specs=[row, halo, row, row, const((HALO, D_MODEL)), const((1, D_MODEL)), const((1, D_MODEL)),
                  const((1, D_MODEL)), const((D_MODEL, D_MODEL)), const((1, D_MODEL))],
        out_specs=row,
        out_shape=jax.ShapeDtypeStruct((s, D_MODEL), F32),
        scratch_shapes=[pltpu.VMEM((HALO + tm, D_MODEL), F32), pltpu.VMEM((tm, D_MODEL), F32)],
        compiler_params=pltpu.CompilerParams(dimension_semantics=("arbitrary",), vmem_limit_bytes=VMEM_LIMIT),
        name="odd_out",
    )(u, u, sg, x1, cw_pad, cb, lg, lb, w_out, nf)


def kernel(x, rel_bias, norm_even, w_in_even, w_gla_lr, b_gla_lr, gla_norm, w_out_even, norm_odd, w_in_odd,
           conv_w, conv_b, conv_ln_g, conv_ln_b, w_out_odd, norm_f):
    bsz, s, d = x.shape
    assert bsz == 1 and d == D_MODEL and s % GLA_BLOCK == 0
    assert norm_even.shape[0] == 1 and norm_odd.shape[0] == 1
    top_k = min(TOPK_MAX, s // 4)
    tm = 256
    x2d = x.reshape(s, d)

    wt_pad = _even_in_weights(w_in_even[0])
    qa_t, ka, va_t, ga, qi_t, ki, wi_t, qb, kb, vb, gb, lr = _even_in(x2d, norm_even[0].reshape(1, d), wt_pad, tm)
    mix_a = _dsa(qa_t, qi_t, wi_t, ga, ka, va_t, ki, _dsa_bias_table(rel_bias), top_k)

    head_pad = ((0, 0), (0, 0), (0, LANES - B_DK))
    wlr_pad = jnp.pad(w_gla_lr[0].reshape(GLA_RANK, B_HEADS, B_DK), head_pad).reshape(GLA_RANK, B_HEADS * LANES)
    wlr_pad = jnp.pad(wlr_pad, ((0, LANES - GLA_RANK), (0, 0)))
    blr_pad = jnp.pad(b_gla_lr[0].reshape(1, B_HEADS, B_DK), head_pad).reshape(1, B_HEADS * LANES)
    mix_b = _gla(qb, kb, vb, gb, lr, wlr_pad, blr_pad, gla_norm[0].reshape(1, B_DV))

    x1, u, sg = _mid(x2d, mix_a, mix_b, w_out_even[0].astype(BF16), norm_odd[0].reshape(1, d),
                     w_in_odd[0].astype(BF16), tm)

    cw_pad = jnp.zeros((HALO, d), F32).at[:CONV_WIDTH].set(conv_w[0])
    out = _odd_out(u, sg, x1, cw_pad, conv_b[0].reshape(1, d), conv_ln_g[0].reshape(1, d),
                   conv_ln_b[0].reshape(1, d), w_out_odd[0].astype(BF16), norm_f.reshape(1, d), tm)
    return out.reshape(bsz, s, d)
```

```python
import functools
import math

import jax
import jax.numpy as jnp
import numpy as np
from jax import lax
from jax.experimental import pallas as pl
from jax.experimental.pallas import tpu as pltpu

D_MODEL = 1024
A_HEADS = 8
A_HEAD_DIM = 64
IDX_HEADS = 4
IDX_DIM = 64
TOPK_MAX = 256
B_HEADS = 4
B_DK = 64
B_DV = 128
GLA_RANK = 16
GLA_TAU = 16.0
GLA_CHUNK = 64
GLA_SUB = 16
CONV_WIDTH = 31
REL_BUCKETS = 32
REL_MAX_EXACT = 16
REL_MAX_DIST = 128
EPS = 1e-6

A_W = A_HEADS * A_HEAD_DIM
B_KW = B_HEADS * B_DK
B_VW = B_HEADS * B_DV

LANES = 128
SUBLANES = 8
VMEM_LIMIT = 56 * 1024 * 1024

F32 = jnp.float32
BF16 = jnp.bfloat16
NEG_INF = float("-inf")
INT_MIN = -(2 ** 31)
NEG_INF_BITS = int(np.array(-np.inf, np.float32).view(np.int32))
NEG_INF_KEY = NEG_INF_BITS ^ 0x7FFFFFFF
M_INIT = -1e30
LOG2E = math.log2(math.e)

C_QA = 0
C_KA = C_QA + A_HEADS * LANES
C_VA = C_KA + A_W
C_GA = C_VA + A_W
C_QI = C_GA + A_W
C_KI = C_QI + IDX_HEADS * LANES
C_WI = C_KI + LANES
C_QB = C_WI + LANES
C_KB = C_QB + B_HEADS * LANES
C_VB = C_KB + B_HEADS * LANES
C_GB = C_VB + B_VW
C_LR = C_GB + B_VW
C_END = C_LR + LANES


def _t5_bucket_np(dist):
    dist = np.maximum(dist, 0)
    d = np.maximum(dist, 1).astype(np.float32)
    large = REL_MAX_EXACT + (np.log(d / REL_MAX_EXACT) / math.log(REL_MAX_DIST / REL_MAX_EXACT)
                             * (REL_BUCKETS - REL_MAX_EXACT)).astype(np.int32)
    large = np.minimum(large, REL_BUCKETS - 1)
    return np.where(dist < REL_MAX_EXACT, dist, large)


def _silu(x):
    return x * jax.nn.sigmoid(x)


def _even_in_kernel(x_ref, g_ref, wt_ref, qat_ref, ka_ref, vat_ref, ga_ref, qit_ref, ki_ref, wit_ref,
                    qb_ref, kb_ref, vb_ref, gb_ref, lr_ref):
    x = x_ref[...]
    h = (x * lax.rsqrt(jnp.mean(x * x, axis=-1, keepdims=True) + EPS) * g_ref[...]).astype(BF16)
    dn_t = (((1,), (1,)), ((), ()))

    def proj(lo, hi):
        return lax.dot_general(h, wt_ref[lo:hi, :], dn_t, preferred_element_type=F32)

    def proj_t(lo, hi):
        return lax.dot_general(wt_ref[lo:hi, :], h, dn_t, preferred_element_type=F32)

    qat_ref[...] = proj_t(C_QA, C_KA).astype(BF16)
    ka_ref[...] = proj(C_KA, C_VA).astype(BF16)
    vat_ref[...] = proj_t(C_VA, C_GA).astype(BF16)
    ga_ref[...] = proj(C_GA, C_QI)
    qit_ref[...] = proj_t(C_QI, C_KI).astype(BF16)
    ki_ref[...] = proj(C_KI, C_WI).astype(BF16)
    wit_ref[...] = proj_t(C_WI, C_WI + SUBLANES)
    qb_ref[...] = proj(C_QB, C_KB)
    kb_ref[...] = proj(C_KB, C_VB)
    vb_ref[...] = proj(C_VB, C_GB)
    gb_ref[...] = proj(C_GB, C_LR)
    lr_ref[...] = proj(C_LR, C_END)


def _even_in_layout():
    cols = np.cumsum([0, A_W, A_W, A_W, A_W, IDX_HEADS * IDX_DIM, IDX_DIM, IDX_HEADS,
                      B_KW, B_KW, B_VW, B_VW, GLA_RANK])
    (s_qa, s_ka, s_va, s_ga, s_qi, s_ki, s_wi, s_qb, s_kb, s_vb, s_gb, s_lr, _) = [int(c) for c in cols]
    src = np.full((C_END,), -1, np.int32)
    scale = np.ones((C_END,), np.float32)

    def put(dst, start, width, sc=1.0):
        src[dst:dst + width] = np.arange(start, start + width)
        scale[dst:dst + width] = sc

    for h in range(A_HEADS):
        put(C_QA + h * LANES + (h % 2) * A_HEAD_DIM, s_qa + h * A_HEAD_DIM, A_HEAD_DIM, (A_HEAD_DIM ** -0.5) * LOG2E)
    put(C_KA, s_ka, A_W)
    put(C_VA, s_va, A_W)
    put(C_GA, s_ga, A_W)
    for h in range(IDX_HEADS):
        put(C_QI + h * LANES, s_qi + h * IDX_DIM, IDX_DIM)
    put(C_KI, s_ki, IDX_DIM)
    put(C_WI, s_wi, IDX_HEADS, (IDX_DIM ** -0.5) * (IDX_HEADS ** -0.5))
    for h in range(B_HEADS):
        put(C_QB + h * LANES, s_qb + h * B_DK, B_DK, B_DK ** -0.5)
        put(C_KB + h * LANES, s_kb + h * B_DK, B_DK)
    put(C_VB, s_vb, B_VW)
    put(C_GB, s_gb, B_VW)
    put(C_LR, s_lr, GLA_RANK)
    return src, scale


def _even_in_weights(w):
    src, scale = _even_in_layout()
    rows = jnp.take(w.T, jnp.asarray(np.maximum(src, 0)), axis=0)
    return (rows * jnp.asarray(np.where(src >= 0, scale, 0.0))[:, None]).astype(BF16)


def _even_in(x2d, g, wt_pad, tm):
    s = x2d.shape[0]
    row = lambda wd, dt: (pl.BlockSpec((tm, wd), lambda i: (i, 0)), jax.ShapeDtypeStruct((s, wd), dt))
    col = lambda ht, dt: (pl.BlockSpec((ht, tm), lambda i: (0, i)), jax.ShapeDtypeStruct((ht, s), dt))
    outs = [col(C_KA - C_QA, BF16), row(A_W, BF16), col(A_W, BF16), row(A_W, F32), col(C_KI - C_QI, BF16),
            row(LANES, BF16), col(SUBLANES, F32), row(C_KB - C_QB, F32), row(C_VB - C_KB, F32), row(B_VW, F32),
            row(B_VW, F32), row(LANES, F32)]
    return pl.pallas_call(
        _even_in_kernel,
        grid=(s // tm,),
        in_specs=[pl.BlockSpec((tm, D_MODEL), lambda i: (i, 0)),
                  pl.BlockSpec((1, D_MODEL), lambda i: (0, 0)),
                  pl.BlockSpec((C_END, D_MODEL), lambda i: (0, 0))],
        out_specs=[spec for spec, _ in outs],
        out_shape=[shape for _, shape in outs],
        compiler_params=pltpu.CompilerParams(dimension_semantics=("arbitrary",), vmem_limit_bytes=VMEM_LIMIT),
        name="even_in",
    )(x2d, g, wt_pad)


DSA_Q = 128
DSA_KB = 128
DSA_WIDE = 4
DSA_NACC = 4
DSA_CELLS = 64


def _f32_key(x):
    bits = lax.bitcast_convert_type(x, jnp.int32)
    return bits ^ (lax.shift_right_arithmetic(bits, 31) & 0x7FFFFFFF)


def _fold_keys(x, op):
    n, w = x.shape
    group = DSA_NACC * SUBLANES
    if n % group == 0 and n > group:
        x = op(x.reshape(n // group, DSA_NACC, SUBLANES, w), axis=0)
        return op(x, axis=0)
    return op(x.reshape(n // SUBLANES, SUBLANES, w), axis=0)


def _dsa_kernel(qat_ref, qit_ref, wit_ref, ga_ref, ka_ref, vat_ref, ki_ref, btab_ref, tri_ref, o_ref,
                slab_ref, s_ref, acc_ref, m_ref, l_ref, *, top_k):
    q, kb, wide = DSA_Q, DSA_KB, DSA_WIDE * DSA_KB
    i = pl.program_id(0)
    nkb = i + 1
    nwide = jnp.maximum(i - 1, 0) // DSA_WIDE
    n0 = nwide * DSA_WIDE
    t_row = i * q + lax.broadcasted_iota(jnp.int32, (1, q), 1)
    key_off = lax.broadcasted_iota(jnp.int32, (kb, 1), 0)
    npair = A_HEADS // 2

    def wide_rows(jw):
        return pl.ds(pl.multiple_of(jw * wide, wide), wide)

    def narrow_rows(j):
        return pl.ds(pl.multiple_of(j * kb, kb), kb)

    def pipelined(first, last, produce, consume):
        def body(t, carry):
            consume(t, (t - first) % 2, carry)
            return produce(jnp.minimum(t + 1, last - 1), (t + 1 - first) % 2)

        lax.fori_loop(first, last, body, produce(first, 0))

    w_head = [wit_ref[h:h + 1, :] for h in range(IDX_HEADS)]

    def scores(rows):
        k_idx = ki_ref[rows, :]
        sc = None
        for h in range(IDX_HEADS):
            s = jnp.dot(k_idx, qit_ref[h * LANES:(h + 1) * LANES, :], preferred_element_type=F32)
            term = w_head[h] * jnp.maximum(s, 0.0)
            sc = term if sc is None else sc + term
        return sc

    def score_wide(jw, c):
        rows = wide_rows(jw)
        slab_ref[rows, :] = _f32_key(scores(rows))
        return c

    def score_narrow(j, c):
        rows = narrow_rows(j)
        sc = jnp.where(j * kb + key_off <= t_row, scores(rows), NEG_INF)
        slab_ref[rows, :] = _f32_key(sc)
        return c

    lax.fori_loop(0, nwide, score_wide, 0)
    lax.fori_loop(n0, nkb, score_narrow, 0)

    def count_ge(trial):
        def cnt(blk):
            return _fold_keys(jnp.where(blk >= trial, 1.0, 0.0), jnp.sum)

        acc = lax.fori_loop(0, nwide, lambda jw, a: a + cnt(slab_ref[wide_rows(jw), :]),
                            jnp.zeros((SUBLANES, q), F32))
        acc = lax.fori_loop(n0, nkb, lambda j, a: a + cnt(slab_ref[narrow_rows(j), :]), acc)
        return jnp.sum(acc, axis=0, keepdims=True).astype(jnp.int32)

    assert DSA_CELLS * SUBLANES >= top_k and wide % (DSA_CELLS * SUBLANES) == 0

    def cells_wide(jw, cm):
        blk = slab_ref[wide_rows(jw), :].reshape(wide // (DSA_CELLS * SUBLANES), DSA_CELLS, SUBLANES, q)
        return jnp.maximum(cm, jnp.max(blk, axis=0))

    def cells_narrow(j, cm):
        blk = slab_ref[narrow_rows(j), :].reshape(kb // SUBLANES, SUBLANES, q)
        return jnp.concatenate([jnp.maximum(cm[:kb // SUBLANES], blk), cm[kb // SUBLANES:]], axis=0)

    cell_max = lax.fori_loop(0, nwide, cells_wide, jnp.full((DSA_CELLS, SUBLANES, q), INT_MIN, jnp.int32))
    cell_max = lax.fori_loop(n0, nkb, cells_narrow, cell_max)
    hi_key = jnp.max(jnp.max(cell_max, axis=0), axis=0, keepdims=True)

    def cell_bisect(b, c):
        trial = c + lax.shift_left(jnp.int32(1), 31 - b)
        cnt = jnp.sum(jnp.sum(jnp.where(cell_max >= trial, 1.0, 0.0), axis=0), axis=0, keepdims=True)
        return jnp.where(cnt >= top_k, trial, c)

    lo_key = lax.fori_loop(0, 32, cell_bisect, jnp.full((1, q), INT_MIN, jnp.int32))
    same_sign = jnp.min(lo_key ^ hi_key) >= 0
    base = jnp.where(same_sign, lo_key, INT_MIN)
    span = jnp.where(same_sign, hi_key - lo_key, -1)
    nbits = jnp.where(same_sign, jnp.max(32 - lax.clz(span)), 32)

    def count_ge3(trials):
        def cnt(blk):
            return jnp.stack([_fold_keys(jnp.where(blk >= t, 1.0, 0.0), jnp.sum) for t in trials])

        acc = lax.fori_loop(0, nwide, lambda jw, a: a + cnt(slab_ref[wide_rows(jw), :]),
                            jnp.zeros((3, SUBLANES, q), F32))
        acc = lax.fori_loop(n0, nkb, lambda j, a: a + cnt(slab_ref[narrow_rows(j), :]), acc)
        return [jnp.sum(acc[k], axis=0, keepdims=True).astype(jnp.int32) for k in range(3)]

    ndigits = (nbits + 1) // 2

    def quarter_step(d, off):
        unit = lax.shift_left(jnp.int32(1), 2 * (ndigits - 1 - d))
        offs = [off + unit, off + 2 * unit, off + 3 * unit]
        counts = count_ge3([base + o for o in offs])
        for o, c in zip(offs, counts):
            in_span = (o ^ INT_MIN) <= (span ^ INT_MIN)
            off = jnp.where(jnp.logical_and(in_span, c >= top_k), o, off)
        return off

    thr = base + lax.fori_loop(0, ndigits, quarter_step, jnp.zeros((1, q), jnp.int32))
    tied = jnp.logical_and(count_ge(thr) > top_k, thr > NEG_INF_KEY)
    any_tie = jnp.max(jnp.where(tied, 1, 0)) > 0

    thr_valid = jnp.maximum(thr, NEG_INF_KEY + 1)

    def mask_fast():
        def body(rows):
            slab_ref[rows, :] = jnp.where(slab_ref[rows, :] >= thr_valid, 0, NEG_INF_BITS)

        lax.fori_loop(0, nwide, lambda jw, c: body(wide_rows(jw)), None)
        lax.fori_loop(n0, nkb, lambda j, c: body(narrow_rows(j)), None)

    def mask_tied():
        need = (top_k - count_ge(thr + 1)).astype(F32)

        def body(rows, n, seen):
            key = slab_ref[rows, :]
            eq = key == thr
            eq_f = jnp.where(eq, 1.0, 0.0)
            seen_after = seen + jnp.sum(_fold_keys(eq_f, jnp.sum), axis=0, keepdims=True)
            take_all = seen_after <= need
            crossing = jnp.logical_and(jnp.logical_not(take_all), seen < need)

            def rank_inside():
                before = jnp.dot(tri_ref[0:n, 0:n], eq_f.astype(BF16), preferred_element_type=F32)
                return jnp.where(seen + before < need, 1.0, 0.0)

            picked = lax.cond(jnp.max(jnp.where(crossing, 1, 0)) > 0, rank_inside,
                              lambda: jnp.broadcast_to(jnp.where(take_all, 1.0, 0.0), (n, q)))
            sel = jnp.logical_or(key > thr, jnp.logical_and(eq, picked > 0.0))
            sel = jnp.logical_and(sel, key > NEG_INF_KEY)
            slab_ref[rows, :] = jnp.where(sel, 0, NEG_INF_BITS)
            return seen_after

        seen = lax.fori_loop(0, nwide, lambda jw, c: body(wide_rows(jw), wide, c), jnp.zeros((1, q), F32))
        lax.fori_loop(n0, nkb, lambda j, c: body(narrow_rows(j), kb, c), seen)

    lax.cond(any_tie, mask_tied, mask_fast)

    m_ref[...] = jnp.full(m_ref.shape, M_INIT, F32)
    l_ref[...] = jnp.zeros(l_ref.shape, F32)
    acc_ref[...] = jnp.zeros(acc_ref.shape, F32)

    def logits(rows, n, buf, rel):
        mask = lax.bitcast_convert_type(slab_ref[rows, :], F32)
        mask2 = jnp.concatenate([mask, mask], axis=1)
        tile_max = []
        for p in range(npair):
            ls = slice(p * LANES, (p + 1) * LANES)
            w_q = jnp.concatenate([qat_ref[(2 * p) * LANES:(2 * p + 1) * LANES, :],
                                   qat_ref[(2 * p + 1) * LANES:(2 * p + 2) * LANES, :]], axis=1)
            s = jnp.dot(ka_ref[rows, ls], w_q, preferred_element_type=F32) + mask2
            if rel is not None:
                s = s + btab_ref[rel, p]
            s_ref[buf, p, 0:n, :] = s
            tile_max.append(_fold_keys(s, jnp.max))
        return tuple(tile_max)

    def attend(rows, n, buf, tile_max):
        for p in range(npair):
            ls = slice(p * LANES, (p + 1) * LANES)
            m_old = m_ref[p]
            m_new = jnp.maximum(m_old, jnp.max(tile_max[p], axis=0, keepdims=True))
            alpha = jnp.exp2(m_old - m_new)
            pexp = jnp.exp2(s_ref[buf, p, 0:n, :] - m_new[0:1])
            m_ref[p] = m_new
            l_ref[p] = alpha * l_ref[p] + _fold_keys(pexp, jnp.sum)
            acc_ref[p] = alpha[0:1] * acc_ref[p] + jnp.dot(vat_ref[ls, rows], pexp.astype(BF16),
                                                           preferred_element_type=F32)

    @pl.when(nwide > 0)
    def _():
        pipelined(0, nwide, lambda jw, buf: logits(wide_rows(jw), wide, buf, None),
                  lambda jw, buf, tile_max: attend(wide_rows(jw), wide, buf, tile_max))

    pipelined(n0, nkb, lambda j, buf: logits(narrow_rows(j), kb, buf, jnp.minimum(i - j, 2)),
              lambda j, buf, tile_max: attend(narrow_rows(j), kb, buf, tile_max))

    for p in range(npair):
        o_t = acc_ref[p] / jnp.sum(l_ref[p], axis=0, keepdims=True)
        o_pair = jnp.concatenate([o_t[:A_HEAD_DIM, :q], o_t[A_HEAD_DIM:, q:]], axis=0).T
        ls = slice(p * LANES, (p + 1) * LANES)
        o_ref[:, ls] = (o_pair * _silu(ga_ref[:, ls])).astype(o_ref.dtype)


def _dsa_bias_table(rel_bias):
    ii = np.arange(DSA_Q)[:, None]
    jj = np.arange(DSA_KB)[None, :]
    buckets = np.stack([_t5_bucket_np(ii - jj), _t5_bucket_np(DSA_KB + ii - jj),
                        np.full((DSA_Q, DSA_KB), REL_BUCKETS - 1)])
    assert (_t5_bucket_np(np.arange(DSA_KB + 1, 4 * DSA_KB)) == REL_BUCKETS - 1).all()
    rb = rel_bias.astype(F32) * LOG2E
    rb = rb - rb[REL_BUCKETS - 1]
    onehot = np.zeros((3, DSA_KB, DSA_Q, REL_BUCKETS), np.float32)
    np.put_along_axis(onehot, np.transpose(buckets, (0, 2, 1))[..., None], 1.0, axis=-1)
    tab = jnp.einsum('tsrb,bh->thsr', jnp.asarray(onehot), rb, precision=lax.Precision.HIGHEST)
    tab = tab.reshape(3, A_HEADS // 2, 2, DSA_KB, DSA_Q)
    return jnp.transpose(tab, (0, 1, 3, 2, 4)).reshape(3, A_HEADS // 2, DSA_KB, 2 * DSA_Q)


def _dsa(qa_t, qi_t, wi_t, ga, ka, va_t, ki, btab, top_k):
    s = ga.shape[0]
    q, kb = DSA_Q, DSA_KB
    wide = DSA_WIDE * kb
    tri = jnp.asarray(np.tril(np.ones((wide, wide), np.float32), -1), BF16)
    whole = lambda shape: pl.BlockSpec(shape, lambda i: (0,) * len(shape), pipeline_mode=pl.Buffered(1))
    return pl.pallas_call(
        functools.partial(_dsa_kernel, top_k=top_k),
        grid=(s // q,),
        in_specs=[pl.BlockSpec((A_HEADS * LANES, q), lambda i: (0, i)),
                  pl.BlockSpec((IDX_HEADS * LANES, q), lambda i: (0, i)),
                  pl.BlockSpec((SUBLANES, q), lambda i: (0, i)),
                  pl.BlockSpec((q, A_W), lambda i: (i, 0)),
                  whole((s, A_W)), whole((A_W, s)), whole((s, LANES)),
                  whole((3, A_HEADS // 2, kb, 2 * q)), whole((wide, wide))],
        out_specs=pl.BlockSpec((q, A_W), lambda i: (i, 0)),
        out_shape=jax.ShapeDtypeStruct((s, A_W), BF16),
        scratch_shapes=[pltpu.VMEM((s, q), jnp.int32),
                        pltpu.VMEM((2, A_HEADS // 2, DSA_WIDE * kb, 2 * q), F32),
                        pltpu.VMEM((A_HEADS // 2, LANES, 2 * q), F32),
                        pltpu.VMEM((A_HEADS // 2, SUBLANES, 2 * q), F32),
                        pltpu.VMEM((A_HEADS // 2, SUBLANES, 2 * q), F32)],
        compiler_params=pltpu.CompilerParams(dimension_semantics=("arbitrary",), vmem_limit_bytes=VMEM_LIMIT),
        name="dsa",
    )(qa_t, qi_t, wi_t, ga, ka, va_t, ki, btab, tri)


GLA_BLOCK = 512


def _gla_kernel(qb_ref, kb_ref, vb_ref, gb_ref, lr_ref, wlr_ref, blr_ref, gn_ref, tril_ref, o_ref,
                state_ref, b_ref, att_ref):
    c, sub = GLA_CHUNK, GLA_SUB
    nsub = c // sub

    @pl.when(pl.program_id(0) == 0)
    def _():
        state_ref[...] = jnp.zeros(state_ref.shape, F32)

    dn_t = (((1,), (1,)), ((), ()))
    dn_0 = (((0,), (0,)), ((), ()))
    hp = lax.Precision.HIGHEST
    row_i = lax.broadcasted_iota(jnp.int32, (sub, sub), 0)
    col_j = lax.broadcasted_iota(jnp.int32, (sub, sub), 1)

    def chunk(ci, carry):
        r0 = pl.multiple_of(ci * c, c)
        z = jnp.dot(lr_ref[pl.ds(r0, c), :], wlr_ref[...], precision=hp, preferred_element_type=F32) + blr_ref[...]
        log_a = (jnp.minimum(z, 0.0) - jnp.log1p(jnp.exp(-jnp.abs(z)))) * (1.0 / GLA_TAU)
        b_ref[...] = jnp.dot(tril_ref[...], log_a, precision=hp, preferred_element_type=F32)
        for h in range(B_HEADS):
            ls = slice(h * LANES, (h + 1) * LANES)
            b = b_ref[:, ls]
            qh = qb_ref[pl.ds(r0, c), ls]
            kh = kb_ref[pl.ds(r0, c), ls]
            vh = vb_ref[pl.ds(r0, c), ls]
            state_t = state_ref[h]
            b_last = b[c - 1:c, :]
            o = lax.dot_general((qh * jnp.exp(b)).astype(BF16), state_t.astype(BF16), dn_t,
                                preferred_element_type=F32)
            for si in range(nsub):
                rs = slice(si * sub, (si + 1) * sub)
                b_i, q_i = b[rs], qh[rs]
                if si > 0:
                    ref_row = b[si * sub:si * sub + 1, :]
                    q_t = (q_i * jnp.exp(b_i - ref_row)).astype(BF16)
                    k_t = (kh[:si * sub] * jnp.exp(ref_row - b[:si * sub])).astype(BF16)
                    att_ref[rs, :si * sub] = lax.dot_general(q_t, k_t, dn_t, preferred_element_type=F32)
                k_i = kh[rs]
                diag = jnp.zeros((sub, sub), F32)
                for jl in range(sub):
                    diff = jnp.minimum(b_i - b_i[jl:jl + 1, :], 0.0)
                    col = jnp.sum(q_i * k_i[jl:jl + 1, :] * jnp.exp(diff), axis=-1, keepdims=True)
                    diag = jnp.where(col_j == jl, col, diag)
                att_ref[rs, rs] = jnp.where(col_j <= row_i, diag, 0.0)
                if si + 1 < nsub:
                    att_ref[rs, (si + 1) * sub:] = jnp.zeros((sub, c - (si + 1) * sub), F32)
            o = o + jnp.dot(att_ref[...].astype(BF16), vh.astype(BF16), preferred_element_type=F32)
            k_dec = (kh * jnp.exp(b_last - b)).astype(BF16)
            state_ref[h] = (jnp.exp(b_last) * state_t
                            + lax.dot_general(vh.astype(BF16), k_dec, dn_0, preferred_element_type=F32))
            o = o * lax.rsqrt(jnp.mean(o * o, axis=-1, keepdims=True) + EPS) * gn_ref[...]
            o_ref[pl.ds(r0, c), ls] = (o * _silu(gb_ref[pl.ds(r0, c), ls])).astype(o_ref.dtype)
        return carry

    lax.fori_loop(0, GLA_BLOCK // c, chunk, 0)


def _gla(qb, kb, vb, gb, lr, wlr_pad, blr_pad, gn):
    s = qb.shape[0]
    blk = min(GLA_BLOCK, s)
    assert blk == GLA_BLOCK and s % blk == 0
    tril = jnp.asarray(np.tril(np.ones((GLA_CHUNK, GLA_CHUNK), np.float32)))
    row = lambda wd: pl.BlockSpec((blk, wd), lambda i: (i, 0))
    const = lambda shape: pl.BlockSpec(shape, lambda i: (0,) * len(shape))
    return pl.pallas_call(
        _gla_kernel,
        grid=(s // blk,),
        in_specs=[row(B_HEADS * LANES), row(B_HEADS * LANES), row(B_VW), row(B_VW), row(LANES),
                  const((LANES, B_HEADS * LANES)), const((1, B_HEADS * LANES)), const((1, B_DV)),
                  const((GLA_CHUNK, GLA_CHUNK))],
        out_specs=row(B_VW),
        out_shape=jax.ShapeDtypeStruct((s, B_VW), BF16),
        scratch_shapes=[pltpu.VMEM((B_HEADS, LANES, B_DV), F32),
                        pltpu.VMEM((GLA_CHUNK, B_HEADS * LANES), F32),
                        pltpu.VMEM((GLA_CHUNK, GLA_CHUNK), F32)],
        compiler_params=pltpu.CompilerParams(dimension_semantics=("arbitrary",), vmem_limit_bytes=VMEM_LIMIT),
        name="gla",
    )(qb, kb, vb, gb, lr, wlr_pad, blr_pad, gn, tril)


def _mid_kernel(x_ref, ma_ref, mb_ref, wo_ref, g_ref, wi_ref, x1_ref, u_ref, sg_ref):
    x1 = (x_ref[...]
          + jnp.dot(ma_ref[...], wo_ref[:A_W, :], preferred_element_type=F32)
          + jnp.dot(mb_ref[...], wo_ref[A_W:, :], preferred_element_type=F32))
    x1_ref[...] = x1
    h = (x1 * lax.rsqrt(jnp.mean(x1 * x1, axis=-1, keepdims=True) + EPS) * g_ref[...]).astype(BF16)
    val = jnp.dot(h, wi_ref[:, :D_MODEL], preferred_element_type=F32)
    glu = jnp.dot(h, wi_ref[:, D_MODEL:2 * D_MODEL], preferred_element_type=F32)
    gate = jnp.dot(h, wi_ref[:, 2 * D_MODEL:], preferred_element_type=F32)
    u_ref[...] = val * jax.nn.sigmoid(glu)
    sg_ref[...] = _silu(gate)


def _mid(x2d, mix_a, mix_b, w_out, g, w_in, tm):
    s = x2d.shape[0]
    row = lambda wd: pl.BlockSpec((tm, wd), lambda i: (i, 0))
    const = lambda shape: pl.BlockSpec(shape, lambda i: (0,) * len(shape))
    return pl.pallas_call(
        _mid_kernel,
        grid=(s // tm,),
        in_specs=[row(D_MODEL), row(A_W), row(B_VW), const((A_W + B_VW, D_MODEL)), const((1, D_MODEL)),
                  const((D_MODEL, 3 * D_MODEL))],
        out_specs=[row(D_MODEL), row(D_MODEL), row(D_MODEL)],
        out_shape=[jax.ShapeDtypeStruct((s, D_MODEL), F32)] * 3,
        compiler_params=pltpu.CompilerParams(dimension_semantics=("arbitrary",), vmem_limit_bytes=VMEM_LIMIT),
        name="mid",
    )(x2d, mix_a, mix_b, w_out, g, w_in)


HALO = 32


def _odd_out_kernel(u_ref, halo_ref, sg_ref, x1_ref, cw_ref, cb_ref, lg_ref, lb_ref, wo_ref, nf_ref, o_ref,
                    ubuf_ref, y_ref, *, tm):
    i = pl.program_id(0)
    ubuf_ref[HALO:, :] = u_ref[...]
    ubuf_ref[:HALO, :] = jnp.where(i > 0, halo_ref[...], 0.0)
    base = HALO - (CONV_WIDTH - 1)
    for cs in range(D_MODEL // LANES):
        ls = slice(cs * LANES, (cs + 1) * LANES)
        acc = jnp.zeros((tm, LANES), F32)
        for k in range(CONV_WIDTH):
            acc = acc + ubuf_ref[base + k:base + k + tm, ls] * cw_ref[k:k + 1, ls]
        y_ref[:, ls] = acc + cb_ref[:, ls]
    y = y_ref[...]
    mu = jnp.mean(y, axis=-1, keepdims=True)
    yc = y - mu
    var = jnp.mean(yc * yc, axis=-1, keepdims=True)
    yn = yc * lax.rsqrt(var + EPS) * lg_ref[...] + lb_ref[...]
    mixed = (_silu(yn) * sg_ref[...]).astype(BF16)
    x2 = x1_ref[...] + jnp.dot(mixed, wo_ref[...], preferred_element_type=F32)
    o_ref[...] = x2 * lax.rsqrt(jnp.mean(x2 * x2, axis=-1, keepdims=True) + EPS) * nf_ref[...]


def _odd_out(u, sg, x1, cw_pad, cb, lg, lb, w_out, nf, tm):
    s = u.shape[0]
    row = pl.BlockSpec((tm, D_MODEL), lambda i: (i, 0))
    const = lambda shape: pl.BlockSpec(shape, lambda i: (0,) * len(shape))
    halo = pl.BlockSpec((HALO, D_MODEL), lambda i: (jnp.maximum(i * (tm // HALO) - 1, 0), 0))
    return pl.pallas_call(
        functools.partial(_odd_out_kernel, tm=tm),
        grid=(s // tm,),
        in_specs=[row, halo, row, row, const((HALO, D_MODEL)), const((1, D_MODEL)), const((1, D_MODEL)),
                  const((1, D_MODEL)), const((D_MODEL, D_MODEL)), const((1, D_MODEL))],
        out_specs=row,
        out_shape=jax.ShapeDtypeStruct((s, D_MODEL), F32),
        scratch_shapes=[pltpu.VMEM((HALO + tm, D_MODEL), F32), pltpu.VMEM((tm, D_MODEL), F32)],
        compiler_params=pltpu.CompilerParams(dimension_semantics=("arbitrary",), vmem_limit_bytes=VMEM_LIMIT),
        name="odd_out",
    )(u, u, sg, x1, cw_pad, cb, lg, lb, w_out, nf)


def kernel(x, rel_bias, norm_even, w_in_even, w_gla_lr, b_gla_lr, gla_norm, w_out_even, norm_odd, w_in_odd,
           conv_w, conv_b, conv_ln_g, conv_ln_b, w_out_odd, norm_f):
    bsz, s, d = x.shape
    assert bsz == 1 and d == D_MODEL and s % GLA_BLOCK == 0
    assert norm_even.shape[0] == 1 and norm_odd.shape[0] == 1
    top_k = min(TOPK_MAX, s // 4)
    tm = 256
    x2d = x.reshape(s, d)

    wt_pad = _even_in_weights(w_in_even[0])
    qa_t, ka, va_t, ga, qi_t, ki, wi_t, qb, kb, vb, gb, lr = _even_in(x2d, norm_even[0].reshape(1, d), wt_pad, tm)
    mix_a = _dsa(qa_t, qi_t, wi_t, ga, ka, va_t, ki, _dsa_bias_table(rel_bias), top_k)

    head_pad = ((0, 0), (0, 0), (0, LANES - B_DK))
    wlr_pad = jnp.pad(w_gla_lr[0].reshape(GLA_RANK, B_HEADS, B_DK), head_pad).reshape(GLA_RANK, B_HEADS * LANES)
    wlr_pad = jnp.pad(wlr_pad, ((0, LANES - GLA_RANK), (0, 0)))
    blr_pad = jnp.pad(b_gla_lr[0].reshape(1, B_HEADS, B_DK), head_pad).reshape(1, B_HEADS * LANES)
    mix_b = _gla(qb, kb, vb, gb, lr, wlr_pad, blr_pad, gla_norm[0].reshape(1, B_DV))

    x1, u, sg = _mid(x2d, mix_a, mix_b, w_out_even[0].astype(BF16), norm_odd[0].reshape(1, d),
                     w_in_odd[0].astype(BF16), tm)

    cw_pad = jnp.zeros((HALO, d), F32).at[:CONV_WIDTH].set(conv_w[0])
    out = _odd_out(u, sg, x1, cw_pad, conv_b[0].reshape(1, d), conv_ln_g[0].reshape(1, d),
                   conv_ln_b[0].reshape(1, d), w_out_odd[0].astype(BF16), norm_f.reshape(1, d), tm)
    return out.reshape(bsz, s, d)
```

```python
import functools
import math

import jax
import jax.numpy as jnp
import numpy as np
from jax import lax
from jax.experimental import pallas as pl
from jax.experimental.pallas import tpu as pltpu

D_MODEL = 1024
A_HEADS = 8
A_HEAD_DIM = 64
IDX_HEADS = 4
IDX_DIM = 64
TOPK_MAX = 256
B_HEADS = 4
B_DK = 64
B_DV = 128
GLA_RANK = 16
GLA_TAU = 16.0
GLA_CHUNK = 64
GLA_SUB = 16
CONV_WIDTH = 31
REL_BUCKETS = 32
REL_MAX_EXACT = 16
REL_MAX_DIST = 128
EPS = 1e-6

A_W = A_HEADS * A_HEAD_DIM
B_KW = B_HEADS * B_DK
B_VW = B_HEADS * B_DV

LANES = 128
SUBLANES = 8
VMEM_LIMIT = 56 * 1024 * 1024

F32 = jnp.float32
BF16 = jnp.bfloat16
NEG_INF = float("-inf")
INT_MIN = -(2 ** 31)
NEG_INF_BITS = int(np.array(-np.inf, np.float32).view(np.int32))
NEG_INF_KEY = NEG_INF_BITS ^ 0x7FFFFFFF
M_INIT = -1e30
LOG2E = math.log2(math.e)

C_QA = 0
C_KA = C_QA + A_HEADS * LANES
C_VA = C_KA + A_W
C_GA = C_VA + A_W
C_QI = C_GA + A_W
C_KI = C_QI + IDX_HEADS * LANES
C_WI = C_KI + LANES
C_QB = C_WI + LANES
C_KB = C_QB + B_HEADS * LANES
C_VB = C_KB + B_HEADS * LANES
C_GB = C_VB + B_VW
C_LR = C_GB + B_VW
C_END = C_LR + LANES


def _t5_bucket_np(dist):
    dist = np.maximum(dist, 0)
    d = np.maximum(dist, 1).astype(np.float32)
    large = REL_MAX_EXACT + (np.log(d / REL_MAX_EXACT) / math.log(REL_MAX_DIST / REL_MAX_EXACT)
                             * (REL_BUCKETS - REL_MAX_EXACT)).astype(np.int32)
    large = np.minimum(large, REL_BUCKETS - 1)
    return np.where(dist < REL_MAX_EXACT, dist, large)


def _silu(x):
    return x * jax.nn.sigmoid(x)


def _even_in_kernel(x_ref, g_ref, wt_ref, qat_ref, ka_ref, vat_ref, ga_ref, qit_ref, ki_ref, wit_ref,
                    qb_ref, kb_ref, vb_ref, gb_ref, lr_ref):
    x = x_ref[...]
    h = (x * lax.rsqrt(jnp.mean(x * x, axis=-1, keepdims=True) + EPS) * g_ref[...]).astype(BF16)
    dn_t = (((1,), (1,)), ((), ()))

    def proj(lo, hi):
        return lax.dot_general(h, wt_ref[lo:hi, :], dn_t, preferred_element_type=F32)

    def proj_t(lo, hi):
        return lax.dot_general(wt_ref[lo:hi, :], h, dn_t, preferred_element_type=F32)

    qat_ref[...] = proj_t(C_QA, C_KA).astype(BF16)
    ka_ref[...] = proj(C_KA, C_VA).astype(BF16)
    vat_ref[...] = proj_t(C_VA, C_GA).astype(BF16)
    ga_ref[...] = proj(C_GA, C_QI)
    qit_ref[...] = proj_t(C_QI, C_KI).astype(BF16)
    ki_ref[...] = proj(C_KI, C_WI).astype(BF16)
    wit_ref[...] = proj_t(C_WI, C_WI + SUBLANES)
    qb_ref[...] = proj(C_QB, C_KB)
    kb_ref[...] = proj(C_KB, C_VB)
    vb_ref[...] = proj(C_VB, C_GB)
    gb_ref[...] = proj(C_GB, C_LR)
    lr_ref[...] = proj(C_LR, C_END)


def _even_in_layout():
    cols = np.cumsum([0, A_W, A_W, A_W, A_W, IDX_HEADS * IDX_DIM, IDX_DIM, IDX_HEADS,
                      B_KW, B_KW, B_VW, B_VW, GLA_RANK])
    (s_qa, s_ka, s_va, s_ga, s_qi, s_ki, s_wi, s_qb, s_kb, s_vb, s_gb, s_lr, _) = [int(c) for c in cols]
    src = np.full((C_END,), -1, np.int32)
    scale = np.ones((C_END,), np.float32)

    def put(dst, start, width, sc=1.0):
        src[dst:dst + width] = np.arange(start, start + width)
        scale[dst:dst + width] = sc

    for h in range(A_HEADS):
        put(C_QA + h * LANES + (h % 2) * A_HEAD_DIM, s_qa + h * A_HEAD_DIM, A_HEAD_DIM, (A_HEAD_DIM ** -0.5) * LOG2E)
    put(C_KA, s_ka, A_W)
    put(C_VA, s_va, A_W)
    put(C_GA, s_ga, A_W)
    for h in range(IDX_HEADS):
        put(C_QI + h * LANES, s_qi + h * IDX_DIM, IDX_DIM)
    put(C_KI, s_ki, IDX_DIM)
    put(C_WI, s_wi, IDX_HEADS, (IDX_DIM ** -0.5) * (IDX_HEADS ** -0.5))
    for h in range(B_HEADS):
        put(C_QB + h * LANES, s_qb + h * B_DK, B_DK, B_DK ** -0.5)
        put(C_KB + h * LANES, s_kb + h * B_DK, B_DK)
    put(C_VB, s_vb, B_VW)
    put(C_GB, s_gb, B_VW)
    put(C_LR, s_lr, GLA_RANK)
    return src, scale


def _even_in_weights(w):
    src, scale = _even_in_layout()
    rows = jnp.take(w.T, jnp.asarray(np.maximum(src, 0)), axis=0)
    return (rows * jnp.asarray(np.where(src >= 0, scale, 0.0))[:, None]).astype(BF16)


def _even_in(x2d, g, wt_pad, tm):
    s = x2d.shape[0]
    row = lambda wd, dt: (pl.BlockSpec((tm, wd), lambda i: (i, 0)), jax.ShapeDtypeStruct((s, wd), dt))
    col = lambda ht, dt: (pl.BlockSpec((ht, tm), lambda i: (0, i)), jax.ShapeDtypeStruct((ht, s), dt))
    outs = [col(C_KA - C_QA, BF16), row(A_W, BF16), col(A_W, BF16), row(A_W, F32), col(C_KI - C_QI, BF16),
            row(LANES, BF16), col(SUBLANES, F32), row(C_KB - C_QB, F32), row(C_VB - C_KB, F32), row(B_VW, F32),
            row(B_VW, F32), row(LANES, F32)]
    return pl.pallas_call(
        _even_in_kernel,
        grid=(s // tm,),
        in_specs=[pl.BlockSpec((tm, D_MODEL), lambda i: (i, 0)),
                  pl.BlockSpec((1, D_MODEL), lambda i: (0, 0)),
                  pl.BlockSpec((C_END, D_MODEL), lambda i: (0, 0))],
        out_specs=[spec for spec, _ in outs],
        out_shape=[shape for _, shape in outs],
        compiler_params=pltpu.CompilerParams(dimension_semantics=("arbitrary",), vmem_limit_bytes=VMEM_LIMIT),
        name="even_in",
    )(x2d, g, wt_pad)


DSA_Q = 128
DSA_KB = 128
DSA_WIDE = 4
DSA_NACC = 4
DSA_CELLS = 64


def _f32_key(x):
    bits = lax.bitcast_convert_type(x, jnp.int32)
    return bits ^ (lax.shift_right_arithmetic(bits, 31) & 0x7FFFFFFF)


def _fold_keys(x, op):
    n, w = x.shape
    group = DSA_NACC * SUBLANES
    if n % group == 0 and n > group:
        x = op(x.reshape(n // group, DSA_NACC, SUBLANES, w), axis=0)
        return op(x, axis=0)
    return op(x.reshape(n // SUBLANES, SUBLANES, w), axis=0)


def _dsa_kernel(qat_ref, qit_ref, wit_ref, ga_ref, ka_ref, vat_ref, ki_ref, btab_ref, tri_ref, o_ref,
                slab_ref, s_ref, acc_ref, m_ref, l_ref, *, top_k):
    q, kb, wide = DSA_Q, DSA_KB, DSA_WIDE * DSA_KB
    i = pl.program_id(0)
    nkb = i + 1
    nwide = jnp.maximum(i - 1, 0) // DSA_WIDE
    n0 = nwide * DSA_WIDE
    t_row = i * q + lax.broadcasted_iota(jnp.int32, (1, q), 1)
    key_off = lax.broadcasted_iota(jnp.int32, (kb, 1), 0)
    npair = A_HEADS // 2

    def wide_rows(jw):
        return pl.ds(pl.multiple_of(jw * wide, wide), wide)

    def narrow_rows(j):
        return pl.ds(pl.multiple_of(j * kb, kb), kb)

    def pipelined(first, last, produce, consume):
        def body(t, carry):
            consume(t, (t - first) % 2, carry)
            return produce(jnp.minimum(t + 1, last - 1), (t + 1 - first) % 2)

        lax.fori_loop(first, last, body, produce(first, 0))

    w_head = [wit_ref[h:h + 1, :] for h in range(IDX_HEADS)]

    def scores(rows):
        k_idx = ki_ref[rows, :]
        sc = None
        for h in range(IDX_HEADS):
            s = jnp.dot(k_idx, qit_ref[h * LANES:(h + 1) * LANES, :], preferred_element_type=F32)
            term = w_head[h] * jnp.maximum(s, 0.0)
            sc = term if sc is None else sc + term
        return sc

    def score_wide(jw, c):
        rows = wide_rows(jw)
        slab_ref[rows, :] = _f32_key(scores(rows))
        return c

    def score_narrow(j, c):
        rows = narrow_rows(j)
        sc = jnp.where(j * kb + key_off <= t_row, scores(rows), NEG_INF)
        slab_ref[rows, :] = _f32_key(sc)
        return c

    lax.fori_loop(0, nwide, score_wide, 0)
    lax.fori_loop(n0, nkb, score_narrow, 0)

    def count_ge(trial):
        def cnt(blk):
            return _fold_keys(jnp.where(blk >= trial, 1.0, 0.0), jnp.sum)

        acc = lax.fori_loop(0, nwide, lambda jw, a: a + cnt(slab_ref[wide_rows(jw), :]),
                            jnp.zeros((SUBLANES, q), F32))
        acc = lax.fori_loop(n0, nkb, lambda j, a: a + cnt(slab_ref[narrow_rows(j), :]), acc)
        return jnp.sum(acc, axis=0, keepdims=True).astype(jnp.int32)

    assert DSA_CELLS * SUBLANES >= top_k and wide % (DSA_CELLS * SUBLANES) == 0

    def cells_wide(jw, cm):
        blk = slab_ref[wide_rows(jw), :].reshape(wide // (DSA_CELLS * SUBLANES), DSA_CELLS, SUBLANES, q)
        return jnp.maximum(cm, jnp.max(blk, axis=0))

    def cells_narrow(j, cm):
        blk = slab_ref[narrow_rows(j), :].reshape(kb // SUBLANES, SUBLANES, q)
        return jnp.concatenate([jnp.maximum(cm[:kb // SUBLANES], blk), cm[kb // SUBLANES:]], axis=0)

    cell_max = lax.fori_loop(0, nwide, cells_wide, jnp.full((DSA_CELLS, SUBLANES, q), INT_MIN, jnp.int32))
    cell_max = lax.fori_loop(n0, nkb, cells_narrow, cell_max)
    hi_key = jnp.max(jnp.max(cell_max, axis=0), axis=0, keepdims=True)

    def cell_bisect(b, c):
        trial = c + lax.shift_left(jnp.int32(1), 31 - b)
        cnt = jnp.sum(jnp.sum(jnp.where(cell_max >= trial, 1.0, 0.0), axis=0), axis=0, keepdims=True)
        return jnp.where(cnt >= top_k, trial, c)

    lo_key = lax.fori_loop(0, 32, cell_bisect, jnp.full((1, q), INT_MIN, jnp.int32))
    same_sign = jnp.min(lo_key ^ hi_key) >= 0
    base = jnp.where(same_sign, lo_key, INT_MIN)
    span = jnp.where(same_sign, hi_key - lo_key, -1)
    nbits = jnp.where(same_sign, jnp.max(32 - lax.clz(span)), 32)

    def bisect(b, off):
        trial_off = off + lax.shift_left(jnp.int32(1), nbits - 1 - b)
        in_span = (trial_off ^ INT_MIN) <= (span ^ INT_MIN)
        ok = jnp.logical_and(in_span, count_ge(base + trial_off) >= top_k)
        return jnp.where(ok, trial_off, off)

    thr = base + lax.fori_loop(0, nbits, bisect, jnp.zeros((1, q), jnp.int32))
    tied = jnp.logical_and(count_ge(thr) > top_k, thr > NEG_INF_KEY)
    any_tie = jnp.max(jnp.where(tied, 1, 0)) > 0

    thr_valid = jnp.maximum(thr, NEG_INF_KEY + 1)

    def mask_fast():
        def body(rows):
            slab_ref[rows, :] = jnp.where(slab_ref[rows, :] >= thr_valid, 0, NEG_INF_BITS)

        lax.fori_loop(0, nwide, lambda jw, c: body(wide_rows(jw)), None)
        lax.fori_loop(n0, nkb, lambda j, c: body(narrow_rows(j)), None)

    def mask_tied():
        need = (top_k - count_ge(thr + 1)).astype(F32)

        def body(rows, n, seen):
            key = slab_ref[rows, :]
            eq = key == thr
            eq_f = jnp.where(eq, 1.0, 0.0)
            seen_after = seen + jnp.sum(_fold_keys(eq_f, jnp.sum), axis=0, keepdims=True)
            take_all = seen_after <= need
            crossing = jnp.logical_and(jnp.logical_not(take_all), seen < need)

            def rank_inside():
                before = jnp.dot(tri_ref[0:n, 0:n], eq_f.astype(BF16), preferred_element_type=F32)
                return jnp.where(seen + before < need, 1.0, 0.0)

            picked = lax.cond(jnp.max(jnp.where(crossing, 1, 0)) > 0, rank_inside,
                              lambda: jnp.broadcast_to(jnp.where(take_all, 1.0, 0.0), (n, q)))
            sel = jnp.logical_or(key > thr, jnp.logical_and(eq, picked > 0.0))
            sel = jnp.logical_and(sel, key > NEG_INF_KEY)
            slab_ref[rows, :] = jnp.where(sel, 0, NEG_INF_BITS)
            return seen_after

        seen = lax.fori_loop(0, nwide, lambda jw, c: body(wide_rows(jw), wide, c), jnp.zeros((1, q), F32))
        lax.fori_loop(n0, nkb, lambda j, c: body(narrow_rows(j), kb, c), seen)

    lax.cond(any_tie, mask_tied, mask_fast)

    m_ref[...] = jnp.full(m_ref.shape, M_INIT, F32)
    l_ref[...] = jnp.zeros(l_ref.shape, F32)
    acc_ref[...] = jnp.zeros(acc_ref.shape, F32)

    def logits(rows, n, buf, rel):
        mask = lax.bitcast_convert_type(slab_ref[rows, :], F32)
        mask2 = jnp.concatenate([mask, mask], axis=1)
        tile_max = []
        for p in range(npair):
            ls = slice(p * LANES, (p + 1) * LANES)
            w_q = jnp.concatenate([qat_ref[(2 * p) * LANES:(2 * p + 1) * LANES, :],
                                   qat_ref[(2 * p + 1) * LANES:(2 * p + 2) * LANES, :]], axis=1)
            s = jnp.dot(ka_ref[rows, ls], w_q, preferred_element_type=F32) + mask2
            if rel is not None:
                s = s + btab_ref[rel, p]
            s_ref[buf, p, 0:n, :] = s
            tile_max.append(_fold_keys(s, jnp.max))
        return tuple(tile_max)

    def attend(rows, n, buf, tile_max):
        for p in range(npair):
            ls = slice(p * LANES, (p + 1) * LANES)
            m_old = m_ref[p]
            m_new = jnp.maximum(m_old, jnp.max(tile_max[p], axis=0, keepdims=True))
            alpha = jnp.exp2(m_old - m_new)
            pexp = jnp.exp2(s_ref[buf, p, 0:n, :] - m_new[0:1])
            m_ref[p] = m_new
            l_ref[p] = alpha * l_ref[p] + _fold_keys(pexp, jnp.sum)
            acc_ref[p] = alpha[0:1] * acc_ref[p] + jnp.dot(vat_ref[ls, rows], pexp.astype(BF16),
                                                           preferred_element_type=F32)

    @pl.when(nwide > 0)
    def _():
        pipelined(0, nwide, lambda jw, buf: logits(wide_rows(jw), wide, buf, None),
                  lambda jw, buf, tile_max: attend(wide_rows(jw), wide, buf, tile_max))

    pipelined(n0, nkb, lambda j, buf: logits(narrow_rows(j), kb, buf, jnp.minimum(i - j, 2)),
              lambda j, buf, tile_max: attend(narrow_rows(j), kb, buf, tile_max))

    for p in range(npair):
        o_t = acc_ref[p] / jnp.sum(l_ref[p], axis=0, keepdims=True)
        o_pair = jnp.concatenate([o_t[:A_HEAD_DIM, :q], o_t[A_HEAD_DIM:, q:]], axis=0).T
        ls = slice(p * LANES, (p + 1) * LANES)
        o_ref[:, ls] = (o_pair * _silu(ga_ref[:, ls])).astype(o_ref.dtype)


def _dsa_bias_table(rel_bias):
    ii = np.arange(DSA_Q)[:, None]
    jj = np.arange(DSA_KB)[None, :]
    buckets = np.stack([_t5_bucket_np(ii - jj), _t5_bucket_np(DSA_KB + ii - jj),
                        np.full((DSA_Q, DSA_KB), REL_BUCKETS - 1)])
    assert (_t5_bucket_np(np.arange(DSA_KB + 1, 4 * DSA_KB)) == REL_BUCKETS - 1).all()
    rb = rel_bias.astype(F32) * LOG2E
    rb = rb - rb[REL_BUCKETS - 1]
    onehot = np.zeros((3, DSA_KB, DSA_Q, REL_BUCKETS), np.float32)
    np.put_along_axis(onehot, np.transpose(buckets, (0, 2, 1))[..., None], 1.0, axis=-1)
    tab = jnp.einsum('tsrb,bh->thsr', jnp.asarray(onehot), rb, precision=lax.Precision.HIGHEST)
    tab = tab.reshape(3, A_HEADS // 2, 2, DSA_KB, DSA_Q)
    return jnp.transpose(tab, (0, 1, 3, 2, 4)).reshape(3, A_HEADS // 2, DSA_KB, 2 * DSA_Q)


def _dsa(qa_t, qi_t, wi_t, ga, ka, va_t, ki, btab, top_k):
    s = ga.shape[0]
    q, kb = DSA_Q, DSA_KB
    wide = DSA_WIDE * kb
    tri = jnp.asarray(np.tril(np.ones((wide, wide), np.float32), -1), BF16)
    whole = lambda shape: pl.BlockSpec(shape, lambda i: (0,) * len(shape), pipeline_mode=pl.Buffered(1))
    return pl.pallas_call(
        functools.partial(_dsa_kernel, top_k=top_k),
        grid=(s // q,),
        in_specs=[pl.BlockSpec((A_HEADS * LANES, q), lambda i: (0, i)),
                  pl.BlockSpec((IDX_HEADS * LANES, q), lambda i: (0, i)),
                  pl.BlockSpec((SUBLANES, q), lambda i: (0, i)),
                  pl.BlockSpec((q, A_W), lambda i: (i, 0)),
                  whole((s, A_W)), whole((A_W, s)), whole((s, LANES)),
                  whole((3, A_HEADS // 2, kb, 2 * q)), whole((wide, wide))],
        out_specs=pl.BlockSpec((q, A_W), lambda i: (i, 0)),
        out_shape=jax.ShapeDtypeStruct((s, A_W), BF16),
        scratch_shapes=[pltpu.VMEM((s, q), jnp.int32),
                        pltpu.VMEM((2, A_HEADS // 2, DSA_WIDE * kb, 2 * q), F32),
                        pltpu.VMEM((A_HEADS // 2, LANES, 2 * q), F32),
                        pltpu.VMEM((A_HEADS // 2, SUBLANES, 2 * q), F32),
                        pltpu.VMEM((A_HEADS // 2, SUBLANES, 2 * q), F32)],
        compiler_params=pltpu.CompilerParams(dimension_semantics=("arbitrary",), vmem_limit_bytes=VMEM_LIMIT),
        name="dsa",
    )(qa_t, qi_t, wi_t, ga, ka, va_t, ki, btab, tri)


GLA_BLOCK = 512


def _gla_kernel(qb_ref, kb_ref, vb_ref, gb_ref, lr_ref, wlr_ref, blr_ref, gn_ref, tril_ref, o_ref,
                state_ref, b_ref, att_ref):
    c, sub = GLA_CHUNK, GLA_SUB
    nsub = c // sub

    @pl.when(pl.program_id(0) == 0)
    def _():
        state_ref[...] = jnp.zeros(state_ref.shape, F32)

    dn_t = (((1,), (1,)), ((), ()))
    dn_0 = (((0,), (0,)), ((), ()))
    hp = lax.Precision.HIGHEST
    row_i = lax.broadcasted_iota(jnp.int32, (sub, sub), 0)
    col_j = lax.broadcasted_iota(jnp.int32, (sub, sub), 1)

    def chunk(ci, carry):
        r0 = pl.multiple_of(ci * c, c)
        z = jnp.dot(lr_ref[pl.ds(r0, c), :], wlr_ref[...], precision=hp, preferred_element_type=F32) + blr_ref[...]
        log_a = (jnp.minimum(z, 0.0) - jnp.log1p(jnp.exp(-jnp.abs(z)))) * (1.0 / GLA_TAU)
        b_ref[...] = jnp.dot(tril_ref[...], log_a, precision=hp, preferred_element_type=F32)
        for h in range(B_HEADS):
            ls = slice(h * LANES, (h + 1) * LANES)
            b = b_ref[:, ls]
            qh = qb_ref[pl.ds(r0, c), ls]
            kh = kb_ref[pl.ds(r0, c), ls]
            vh = vb_ref[pl.ds(r0, c), ls]
            state_t = state_ref[h]
            b_last = b[c - 1:c, :]
            o = lax.dot_general((qh * jnp.exp(b)).astype(BF16), state_t.astype(BF16), dn_t,
                                preferred_element_type=F32)
            for si in range(nsub):
                rs = slice(si * sub, (si + 1) * sub)
                b_i, q_i = b[rs], qh[rs]
                if si > 0:
                    ref_row = b[si * sub:si * sub + 1, :]
                    q_t = (q_i * jnp.exp(b_i - ref_row)).astype(BF16)
                    k_t = (kh[:si * sub] * jnp.exp(ref_row - b[:si * sub])).astype(BF16)
                    att_ref[rs, :si * sub] = lax.dot_general(q_t, k_t, dn_t, preferred_element_type=F32)
                k_i = kh[rs]
                diag = jnp.zeros((sub, sub), F32)
                for jl in range(sub):
                    diff = jnp.minimum(b_i - b_i[jl:jl + 1, :], 0.0)
                    col = jnp.sum(q_i * k_i[jl:jl + 1, :] * jnp.exp(diff), axis=-1, keepdims=True)
                    diag = jnp.where(col_j == jl, col, diag)
                att_ref[rs, rs] = jnp.where(col_j <= row_i, diag, 0.0)
                if si + 1 < nsub:
                    att_ref[rs, (si + 1) * sub:] = jnp.zeros((sub, c - (si + 1) * sub), F32)
            o = o + jnp.dot(att_ref[...].astype(BF16), vh.astype(BF16), preferred_element_type=F32)
            k_dec = (kh * jnp.exp(b_last - b)).astype(BF16)
            state_ref[h] = (jnp.exp(b_last) * state_t
                            + lax.dot_general(vh.astype(BF16), k_dec, dn_0, preferred_element_type=F32))
            o = o * lax.rsqrt(jnp.mean(o * o, axis=-1, keepdims=True) + EPS) * gn_ref[...]
            o_ref[pl.ds(r0, c), ls] = (o * _silu(gb_ref[pl.ds(r0, c), ls])).astype(o_ref.dtype)
        return carry

    lax.fori_loop(0, GLA_BLOCK // c, chunk, 0)


def _gla(qb, kb, vb, gb, lr, wlr_pad, blr_pad, gn):
    s = qb.shape[0]
    blk = min(GLA_BLOCK, s)
    assert blk == GLA_BLOCK and s % blk == 0
    tril = jnp.asarray(np.tril(np.ones((GLA_CHUNK, GLA_CHUNK), np.float32)))
    row = lambda wd: pl.BlockSpec((blk, wd), lambda i: (i, 0))
    const = lambda shape: pl.BlockSpec(shape, lambda i: (0,) * len(shape))
    return pl.pallas_call(
        _gla_kernel,
        grid=(s // blk,),
        in_specs=[row(B_HEADS * LANES), row(B_HEADS * LANES), row(B_VW), row(B_VW), row(LANES),
                  const((LANES, B_HEADS * LANES)), const((1, B_HEADS * LANES)), const((1, B_DV)),
                  const((GLA_CHUNK, GLA_CHUNK))],
        out_specs=row(B_VW),
        out_shape=jax.ShapeDtypeStruct((s, B_VW), BF16),
        scratch_shapes=[pltpu.VMEM((B_HEADS, LANES, B_DV), F32),
                        pltpu.VMEM((GLA_CHUNK, B_HEADS * LANES), F32),
                        pltpu.VMEM((GLA_CHUNK, GLA_CHUNK), F32)],
        compiler_params=pltpu.CompilerParams(dimension_semantics=("arbitrary",), vmem_limit_bytes=VMEM_LIMIT),
        name="gla",
    )(qb, kb, vb, gb, lr, wlr_pad, blr_pad, gn, tril)


def _mid_kernel(x_ref, ma_ref, mb_ref, wo_ref, g_ref, wi_ref, x1_ref, u_ref, sg_ref):
    x1 = (x_ref[...]
          + jnp.dot(ma_ref[...], wo_ref[:A_W, :], preferred_element_type=F32)
          + jnp.dot(mb_ref[...], wo_ref[A_W:, :], preferred_element_type=F32))
    x1_ref[...] = x1
    h = (x1 * lax.rsqrt(jnp.mean(x1 * x1, axis=-1, keepdims=True) + EPS) * g_ref[...]).astype(BF16)
    val = jnp.dot(h, wi_ref[:, :D_MODEL], preferred_element_type=F32)
    glu = jnp.dot(h, wi_ref[:, D_MODEL:2 * D_MODEL], preferred_element_type=F32)
    gate = jnp.dot(h, wi_ref[:, 2 * D_MODEL:], preferred_element_type=F32)
    u_ref[...] = val * jax.nn.sigmoid(glu)
    sg_ref[...] = _silu(gate)


def _mid(x2d, mix_a, mix_b, w_out, g, w_in, tm):
    s = x2d.shape[0]
    row = lambda wd: pl.BlockSpec((tm, wd), lambda i: (i, 0))
    const = lambda shape: pl.BlockSpec(shape, lambda i: (0,) * len(shape))
    return pl.pallas_call(
        _mid_kernel,
        grid=(s // tm,),
        in_specs=[row(D_MODEL), row(A_W), row(B_VW), const((A_W + B_VW, D_MODEL)), const((1, D_MODEL)),
                  const((D_MODEL, 3 * D_MODEL))],
        out_specs=[row(D_MODEL), row(D_MODEL), row(D_MODEL)],
        out_shape=[jax.ShapeDtypeStruct((s, D_MODEL), F32)] * 3,
        compiler_params=pltpu.CompilerParams(dimension_semantics=("arbitrary",), vmem_limit_bytes=VMEM_LIMIT),
        name="mid",
    )(x2d, mix_a, mix_b, w_out, g, w_in)


HALO = 32


def _odd_out_kernel(u_ref, halo_ref, sg_ref, x1_ref, cw_ref, cb_ref, lg_ref, lb_ref, wo_ref, nf_ref, o_ref,
                    ubuf_ref, y_ref, *, tm):
    i = pl.program_id(0)
    ubuf_ref[HALO:, :] = u_ref[...]
    ubuf_ref[:HALO, :] = jnp.where(i > 0, halo_ref[...], 0.0)
    base = HALO - (CONV_WIDTH - 1)
    for cs in range(D_MODEL // LANES):
        ls = slice(cs * LANES, (cs + 1) * LANES)
        acc = jnp.zeros((tm, LANES), F32)
        for k in range(CONV_WIDTH):
            acc = acc + ubuf_ref[base + k:base + k + tm, ls] * cw_ref[k:k + 1, ls]
        y_ref[:, ls] = acc + cb_ref[:, ls]
    y = y_ref[...]
    mu = jnp.mean(y, axis=-1, keepdims=True)
    yc = y - mu
    var = jnp.mean(yc * yc, axis=-1, keepdims=True)
    yn = yc * lax.rsqrt(var + EPS) * lg_ref[...] + lb_ref[...]
    mixed = (_silu(yn) * sg_ref[...]).astype(BF16)
    x2 = x1_ref[...] + jnp.dot(mixed, wo_ref[...], preferred_element_type=F32)
    o_ref[...] = x2 * lax.rsqrt(jnp.mean(x2 * x2, axis=-1, keepdims=True) + EPS) * nf_ref[...]


def _odd_out(u, sg, x1, cw_pad, cb, lg, lb, w_out, nf, tm):
    s = u.shape[0]
    row = pl.BlockSpec((tm, D_MODEL), lambda i: (i, 0))
    const = lambda shape: pl.BlockSpec(shape, lambda i: (0,) * len(shape))
    halo = pl.BlockSpec((HALO, D_MODEL), lambda i: (jnp.maximum(i * (tm // HALO) - 1, 0), 0))
    return pl.pallas_call(
        functools.partial(_odd_out_kernel, tm=tm),
        grid=(s // tm,),
        in_specs=[row, halo, row, row, const((HALO, D_MODEL)), const((1, D_MODEL)), const((1, D_MODEL)),
                  const((1, D_MODEL)), const((D_MODEL, D_MODEL)), const((1, D_MODEL))],
        out_specs=row,
        out_shape=jax.ShapeDtypeStruct((s, D_MODEL), F32),
        scratch_shapes=[pltpu.VMEM((HALO + tm, D_MODEL), F32), pltpu.VMEM((tm, D_MODEL), F32)],
        compiler_params=pltpu.CompilerParams(dimension_semantics=("arbitrary",), vmem_limit_bytes=VMEM_LIMIT),
        name="odd_out",
    )(u, u, sg, x1, cw_pad, cb, lg, lb, w_out, nf)


def kernel(x, rel_bias, norm_even, w_in_even, w_gla_lr, b_gla_lr, gla_norm, w_out_even, norm_odd, w_in_odd,
           conv_w, conv_b, conv_ln_g, conv_ln_b, w_out_odd, norm_f):
    bsz, s, d = x.shape
    assert bsz == 1 and d == D_MODEL and s % GLA_BLOCK == 0
    assert norm_even.shape[0] == 1 and norm_odd.shape[0] == 1
    top_k = min(TOPK_MAX, s // 4)
    tm = 256
    x2d = x.reshape(s, d)

    wt_pad = _even_in_weights(w_in_even[0])
    qa_t, ka, va_t, ga, qi_t, ki, wi_t, qb, kb, vb, gb, lr = _even_in(x2d, norm_even[0].reshape(1, d), wt_pad, tm)
    mix_a = _dsa(qa_t, qi_t, wi_t, ga, ka, va_t, ki, _dsa_bias_table(rel_bias), top_k)

    head_pad = ((0, 0), (0, 0), (0, LANES - B_DK))
    wlr_pad = jnp.pad(w_gla_lr[0].reshape(GLA_RANK, B_HEADS, B_DK), head_pad).reshape(GLA_RANK, B_HEADS * LANES)
    wlr_pad = jnp.pad(wlr_pad, ((0, LANES - GLA_RANK), (0, 0)))
    blr_pad = jnp.pad(b_gla_lr[0].reshape(1, B_HEADS, B_DK), head_pad).reshape(1, B_HEADS * LANES)
    mix_b = _gla(qb, kb, vb, gb, lr, wlr_pad, blr_pad, gla_norm[0].reshape(1, B_DV))

    x1, u, sg = _mid(x2d, mix_a, mix_b, w_out_even[0].astype(BF16), norm_odd[0].reshape(1, d),
                     w_in_odd[0].astype(BF16), tm)

    cw_pad = jnp.zeros((HALO, d), F32).at[:CONV_WIDTH].set(conv_w[0])
    out = _odd_out(u, sg, x1, cw_pad, conv_b[0].reshape(1, d), conv_ln_g[0].reshape(1, d),
                   conv_ln_b[0].reshape(1, d), w_out_odd[0].astype(BF16), norm_f.reshape(1, d), tm)
    return out.reshape(bsz, s, d)
```

```python
import functools
import math

import jax
import jax.numpy as jnp
import numpy as np
from jax import lax
from jax.experimental import pallas as pl
from jax.experimental.pallas import tpu as pltpu

D_MODEL = 1024
A_HEADS = 8
A_HEAD_DIM = 64
IDX_HEADS = 4
IDX_DIM = 64
TOPK_MAX = 256
B_HEADS = 4
B_DK = 64
B_DV = 128
GLA_RANK = 16
GLA_TAU = 16.0
GLA_CHUNK = 64
GLA_SUB = 16
CONV_WIDTH = 31
REL_BUCKETS = 32
REL_MAX_EXACT = 16
REL_MAX_DIST = 128
EPS = 1e-6

A_W = A_HEADS * A_HEAD_DIM
B_KW = B_HEADS * B_DK
B_VW = B_HEADS * B_DV

LANES = 128
SUBLANES = 8
VMEM_LIMIT = 56 * 1024 * 1024

F32 = jnp.float32
BF16 = jnp.bfloat16
NEG_INF = float("-inf")
INT_MIN = -(2 ** 31)
NEG_INF_BITS = int(np.array(-np.inf, np.float32).view(np.int32))
NEG_INF_KEY = NEG_INF_BITS ^ 0x7FFFFFFF
M_INIT = -1e30
LOG2E = math.log2(math.e)

C_QA = 0
C_KA = C_QA + A_HEADS * LANES
C_VA = C_KA + A_W
C_GA = C_VA + A_W
C_QI = C_GA + A_W
C_KI = C_QI + IDX_HEADS * LANES
C_WI = C_KI + LANES
C_QB = C_WI + LANES
C_KB = C_QB + B_HEADS * LANES
C_VB = C_KB + B_HEADS * LANES
C_GB = C_VB + B_VW
C_LR = C_GB + B_VW
C_END = C_LR + LANES


def _t5_bucket_np(dist):
    dist = np.maximum(dist, 0)
    d = np.maximum(dist, 1).astype(np.float32)
    large = REL_MAX_EXACT + (np.log(d / REL_MAX_EXACT) / math.log(REL_MAX_DIST / REL_MAX_EXACT)
                             * (REL_BUCKETS - REL_MAX_EXACT)).astype(np.int32)
    large = np.minimum(large, REL_BUCKETS - 1)
    return np.where(dist < REL_MAX_EXACT, dist, large)


def _silu(x):
    return x * jax.nn.sigmoid(x)


def _even_in_kernel(x_ref, g_ref, wt_ref, qat_ref, ka_ref, vat_ref, ga_ref, qit_ref, ki_ref, wit_ref,
                    qb_ref, kb_ref, vb_ref, gb_ref, lr_ref):
    x = x_ref[...]
    h = (x * lax.rsqrt(jnp.mean(x * x, axis=-1, keepdims=True) + EPS) * g_ref[...]).astype(BF16)
    dn_t = (((1,), (1,)), ((), ()))

    def proj(lo, hi):
        return lax.dot_general(h, wt_ref[lo:hi, :], dn_t, preferred_element_type=F32)

    def proj_t(lo, hi):
        return lax.dot_general(wt_ref[lo:hi, :], h, dn_t, preferred_element_type=F32)

    qat_ref[...] = proj_t(C_QA, C_KA).astype(BF16)
    ka_ref[...] = proj(C_KA, C_VA).astype(BF16)
    vat_ref[...] = proj_t(C_VA, C_GA).astype(BF16)
    ga_ref[...] = proj(C_GA, C_QI)
    qit_ref[...] = proj_t(C_QI, C_KI).astype(BF16)
    ki_ref[...] = proj(C_KI, C_WI).astype(BF16)
    wit_ref[...] = proj_t(C_WI, C_WI + SUBLANES)
    qb_ref[...] = proj(C_QB, C_KB)
    kb_ref[...] = proj(C_KB, C_VB)
    vb_ref[...] = proj(C_VB, C_GB)
    gb_ref[...] = proj(C_GB, C_LR)
    lr_ref[...] = proj(C_LR, C_END)


def _even_in_layout():
    cols = np.cumsum([0, A_W, A_W, A_W, A_W, IDX_HEADS * IDX_DIM, IDX_DIM, IDX_HEADS,
                      B_KW, B_KW, B_VW, B_VW, GLA_RANK])
    (s_qa, s_ka, s_va, s_ga, s_qi, s_ki, s_wi, s_qb, s_kb, s_vb, s_gb, s_lr, _) = [int(c) for c in cols]
    src = np.full((C_END,), -1, np.int32)
    scale = np.ones((C_END,), np.float32)

    def put(dst, start, width, sc=1.0):
        src[dst:dst + width] = np.arange(start, start + width)
        scale[dst:dst + width] = sc

    for h in range(A_HEADS):
        put(C_QA + h * LANES + (h % 2) * A_HEAD_DIM, s_qa + h * A_HEAD_DIM, A_HEAD_DIM, (A_HEAD_DIM ** -0.5) * LOG2E)
    put(C_KA, s_ka, A_W)
    put(C_VA, s_va, A_W)
    put(C_GA, s_ga, A_W)
    for h in range(IDX_HEADS):
        put(C_QI + h * LANES, s_qi + h * IDX_DIM, IDX_DIM)
    put(C_KI, s_ki, IDX_DIM)
    put(C_WI, s_wi, IDX_HEADS, (IDX_DIM ** -0.5) * (IDX_HEADS ** -0.5))
    for h in range(B_HEADS):
        put(C_QB + h * LANES, s_qb + h * B_DK, B_DK, B_DK ** -0.5)
        put(C_KB + h * LANES, s_kb + h * B_DK, B_DK)
    put(C_VB, s_vb, B_VW)
    put(C_GB, s_gb, B_VW)
    put(C_LR, s_lr, GLA_RANK)
    return src, scale


def _even_in_weights(w):
    src, scale = _even_in_layout()
    pieces, start = [], 0
    for end in range(1, C_END + 1):
        if end == C_END or (src[end] >= 0) != (src[start] >= 0) or (src[end] >= 0 and src[end] != src[end - 1] + 1):
            if src[start] >= 0:
                pieces.append(w[:, int(src[start]):int(src[end - 1]) + 1])
            else:
                pieces.append(jnp.zeros((D_MODEL, end - start), w.dtype))
            start = end
    w_pad = jnp.concatenate(pieces, axis=1) * jnp.asarray(scale)[None, :]
    return w_pad.T.astype(BF16)


def _even_in(x2d, g, wt_pad, tm):
    s = x2d.shape[0]
    row = lambda wd, dt: (pl.BlockSpec((tm, wd), lambda i: (i, 0)), jax.ShapeDtypeStruct((s, wd), dt))
    col = lambda ht, dt: (pl.BlockSpec((ht, tm), lambda i: (0, i)), jax.ShapeDtypeStruct((ht, s), dt))
    outs = [col(C_KA - C_QA, BF16), row(A_W, BF16), col(A_W, BF16), row(A_W, F32), col(C_KI - C_QI, BF16),
            row(LANES, BF16), col(SUBLANES, F32), row(C_KB - C_QB, F32), row(C_VB - C_KB, F32), row(B_VW, F32),
            row(B_VW, F32), row(LANES, F32)]
    return pl.pallas_call(
        _even_in_kernel,
        grid=(s // tm,),
        in_specs=[pl.BlockSpec((tm, D_MODEL), lambda i: (i, 0)),
                  pl.BlockSpec((1, D_MODEL), lambda i: (0, 0)),
                  pl.BlockSpec((C_END, D_MODEL), lambda i: (0, 0))],
        out_specs=[spec for spec, _ in outs],
        out_shape=[shape for _, shape in outs],
        compiler_params=pltpu.CompilerParams(dimension_semantics=("arbitrary",), vmem_limit_bytes=VMEM_LIMIT),
        name="even_in",
    )(x2d, g, wt_pad)


DSA_Q = 128
DSA_KB = 128
DSA_WIDE = 4
DSA_NACC = 4
DSA_CELLS = 64


def _f32_key(x):
    bits = lax.bitcast_convert_type(x, jnp.int32)
    return bits ^ (lax.shift_right_arithmetic(bits, 31) & 0x7FFFFFFF)


def _fold_keys(x, op):
    n, w = x.shape
    group = DSA_NACC * SUBLANES
    if n % group == 0 and n > group:
        x = op(x.reshape(n // group, DSA_NACC, SUBLANES, w), axis=0)
        return op(x, axis=0)
    return op(x.reshape(n // SUBLANES, SUBLANES, w), axis=0)


def _dsa_kernel(qat_ref, qit_ref, wit_ref, ga_ref, ka_ref, vat_ref, ki_ref, btab_ref, tri_ref, o_ref,
                slab_ref, s_ref, acc_ref, m_ref, l_ref, *, top_k):
    q, kb, wide = DSA_Q, DSA_KB, DSA_WIDE * DSA_KB
    i = pl.program_id(0)
    nkb = i + 1
    nwide = jnp.maximum(i - 1, 0) // DSA_WIDE
    n0 = nwide * DSA_WIDE
    t_row = i * q + lax.broadcasted_iota(jnp.int32, (1, q), 1)
    key_off = lax.broadcasted_iota(jnp.int32, (kb, 1), 0)
    npair = A_HEADS // 2

    def wide_rows(jw):
        return pl.ds(pl.multiple_of(jw * wide, wide), wide)

    def narrow_rows(j):
        return pl.ds(pl.multiple_of(j * kb, kb), kb)

    def pipelined(first, last, produce, consume):
        def body(t, carry):
            consume(t, (t - first) % 2, carry)
            return produce(jnp.minimum(t + 1, last - 1), (t + 1 - first) % 2)

        lax.fori_loop(first, last, body, produce(first, 0))

    w_head = [wit_ref[h:h + 1, :] for h in range(IDX_HEADS)]

    def scores(rows):
        k_idx = ki_ref[rows, :]
        sc = None
        for h in range(IDX_HEADS):
            s = jnp.dot(k_idx, qit_ref[h * LANES:(h + 1) * LANES, :], preferred_element_type=F32)
            term = w_head[h] * jnp.maximum(s, 0.0)
            sc = term if sc is None else sc + term
        return sc

    def score_wide(jw, c):
        rows = wide_rows(jw)
        slab_ref[rows, :] = _f32_key(scores(rows))
        return c

    def score_narrow(j, c):
        rows = narrow_rows(j)
        sc = jnp.where(j * kb + key_off <= t_row, scores(rows), NEG_INF)
        slab_ref[rows, :] = _f32_key(sc)
        return c

    lax.fori_loop(0, nwide, score_wide, 0)
    lax.fori_loop(n0, nkb, score_narrow, 0)

    def count_ge(trial):
        def cnt(blk):
            return _fold_keys(jnp.where(blk >= trial, 1.0, 0.0), jnp.sum)

        acc = lax.fori_loop(0, nwide, lambda jw, a: a + cnt(slab_ref[wide_rows(jw), :]),
                            jnp.zeros((SUBLANES, q), F32))
        acc = lax.fori_loop(n0, nkb, lambda j, a: a + cnt(slab_ref[narrow_rows(j), :]), acc)
        return jnp.sum(acc, axis=0, keepdims=True).astype(jnp.int32)

    assert DSA_CELLS * SUBLANES >= top_k and wide % (DSA_CELLS * SUBLANES) == 0

    def cells_wide(jw, cm):
        blk = slab_ref[wide_rows(jw), :].reshape(wide // (DSA_CELLS * SUBLANES), DSA_CELLS, SUBLANES, q)
        return jnp.maximum(cm, jnp.max(blk, axis=0))

    def cells_narrow(j, cm):
        blk = slab_ref[narrow_rows(j), :].reshape(kb // SUBLANES, SUBLANES, q)
        return jnp.concatenate([jnp.maximum(cm[:kb // SUBLANES], blk), cm[kb // SUBLANES:]], axis=0)

    cell_max = lax.fori_loop(0, nwide, cells_wide, jnp.full((DSA_CELLS, SUBLANES, q), INT_MIN, jnp.int32))
    cell_max = lax.fori_loop(n0, nkb, cells_narrow, cell_max)
    hi_key = jnp.max(jnp.max(cell_max, axis=0), axis=0, keepdims=True)

    def cell_bisect(b, c):
        trial = c + lax.shift_left(jnp.int32(1), 31 - b)
        cnt = jnp.sum(jnp.sum(jnp.where(cell_max >= trial, 1.0, 0.0), axis=0), axis=0, keepdims=True)
        return jnp.where(cnt >= top_k, trial, c)

    lo_key = lax.fori_loop(0, 32, cell_bisect, jnp.full((1, q), INT_MIN, jnp.int32))
    same_sign = jnp.min(lo_key ^ hi_key) >= 0
    base = jnp.where(same_sign, lo_key, INT_MIN)
    span = jnp.where(same_sign, hi_key - lo_key, -1)
    nbits = jnp.where(same_sign, jnp.max(32 - lax.clz(span)), 32)

    def bisect(b, off):
        trial_off = off + lax.shift_left(jnp.int32(1), nbits - 1 - b)
        in_span = (trial_off ^ INT_MIN) <= (span ^ INT_MIN)
        ok = jnp.logical_and(in_span, count_ge(base + trial_off) >= top_k)
        return jnp.where(ok, trial_off, off)

    thr = base + lax.fori_loop(0, nbits, bisect, jnp.zeros((1, q), jnp.int32))
    tied = jnp.logical_and(count_ge(thr) > top_k, thr > NEG_INF_KEY)
    any_tie = jnp.max(jnp.where(tied, 1, 0)) > 0

    thr_valid = jnp.maximum(thr, NEG_INF_KEY + 1)

    def mask_fast():
        def body(rows):
            slab_ref[rows, :] = jnp.where(slab_ref[rows, :] >= thr_valid, 0, NEG_INF_BITS)

        lax.fori_loop(0, nwide, lambda jw, c: body(wide_rows(jw)), None)
        lax.fori_loop(n0, nkb, lambda j, c: body(narrow_rows(j)), None)

    def mask_tied():
        need = (top_k - count_ge(thr + 1)).astype(F32)

        def body(rows, n, seen):
            key = slab_ref[rows, :]
            eq = key == thr
            eq_f = jnp.where(eq, 1.0, 0.0)
            seen_after = seen + jnp.sum(_fold_keys(eq_f, jnp.sum), axis=0, keepdims=True)
            take_all = seen_after <= need
            crossing = jnp.logical_and(jnp.logical_not(take_all), seen < need)

            def rank_inside():
                before = jnp.dot(tri_ref[0:n, 0:n], eq_f.astype(BF16), preferred_element_type=F32)
                return jnp.where(seen + before < need, 1.0, 0.0)

            picked = lax.cond(jnp.max(jnp.where(crossing, 1, 0)) > 0, rank_inside,
                              lambda: jnp.broadcast_to(jnp.where(take_all, 1.0, 0.0), (n, q)))
            sel = jnp.logical_or(key > thr, jnp.logical_and(eq, picked > 0.0))
            sel = jnp.logical_and(sel, key > NEG_INF_KEY)
            slab_ref[rows, :] = jnp.where(sel, 0, NEG_INF_BITS)
            return seen_after

        seen = lax.fori_loop(0, nwide, lambda jw, c: body(wide_rows(jw), wide, c), jnp.zeros((1, q), F32))
        lax.fori_loop(n0, nkb, lambda j, c: body(narrow_rows(j), kb, c), seen)

    lax.cond(any_tie, mask_tied, mask_fast)

    m_ref[...] = jnp.full(m_ref.shape, M_INIT, F32)
    l_ref[...] = jnp.zeros(l_ref.shape, F32)
    acc_ref[...] = jnp.zeros(acc_ref.shape, F32)

    def logits(rows, n, buf, rel):
        mask = lax.bitcast_convert_type(slab_ref[rows, :], F32)
        mask2 = jnp.concatenate([mask, mask], axis=1)
        tile_max = []
        for p in range(npair):
            ls = slice(p * LANES, (p + 1) * LANES)
            w_q = jnp.concatenate([qat_ref[(2 * p) * LANES:(2 * p + 1) * LANES, :],
                                   qat_ref[(2 * p + 1) * LANES:(2 * p + 2) * LANES, :]], axis=1)
            s = jnp.dot(ka_ref[rows, ls], w_q, preferred_element_type=F32) + mask2
            if rel is not None:
                s = s + btab_ref[rel, p]
            s_ref[buf, p, 0:n, :] = s
            tile_max.append(_fold_keys(s, jnp.max))
        return tuple(tile_max)

    def attend(rows, n, buf, tile_max):
        for p in range(npair):
            ls = slice(p * LANES, (p + 1) * LANES)
            m_old = m_ref[p]
            m_new = jnp.maximum(m_old, jnp.max(tile_max[p], axis=0, keepdims=True))
            alpha = jnp.exp2(m_old - m_new)
            pexp = jnp.exp2(s_ref[buf, p, 0:n, :] - m_new[0:1])
            m_ref[p] = m_new
            l_ref[p] = alpha * l_ref[p] + _fold_keys(pexp, jnp.sum)
            acc_ref[p] = alpha[0:1] * acc_ref[p] + jnp.dot(vat_ref[ls, rows], pexp.astype(BF16),
                                                           preferred_element_type=F32)

    @pl.when(nwide > 0)
    def _():
        pipelined(0, nwide, lambda jw, buf: logits(wide_rows(jw), wide, buf, None),
                  lambda jw, buf, tile_max: attend(wide_rows(jw), wide, buf, tile_max))

    pipelined(n0, nkb, lambda j, buf: logits(narrow_rows(j), kb, buf, jnp.minimum(i - j, 2)),
              lambda j, buf, tile_max: attend(narrow_rows(j), kb, buf, tile_max))

    for p in range(npair):
        o_t = acc_ref[p] / jnp.sum(l_ref[p], axis=0, keepdims=True)
        o_pair = jnp.concatenate([o_t[:A_HEAD_DIM, :q], o_t[A_HEAD_DIM:, q:]], axis=0).T
        ls = slice(p * LANES, (p + 1) * LANES)
        o_ref[:, ls] = (o_pair * _silu(ga_ref[:, ls])).astype(o_ref.dtype)


def _dsa_bias_table(rel_bias):
    ii = np.arange(DSA_Q)[:, None]
    jj = np.arange(DSA_KB)[None, :]
    buckets = np.stack([_t5_bucket_np(ii - jj), _t5_bucket_np(DSA_KB + ii - jj),
                        np.full((DSA_Q, DSA_KB), REL_BUCKETS - 1)])
    assert (_t5_bucket_np(np.arange(DSA_KB + 1, 4 * DSA_KB)) == REL_BUCKETS - 1).all()
    rb = rel_bias.astype(F32) * LOG2E
    rb = rb - rb[REL_BUCKETS - 1]
    onehot = np.zeros((3, DSA_KB, DSA_Q, REL_BUCKETS), np.float32)
    np.put_along_axis(onehot, np.transpose(buckets, (0, 2, 1))[..., None], 1.0, axis=-1)
    tab = jnp.einsum('tsrb,bh->thsr', jnp.asarray(onehot), rb, precision=lax.Precision.HIGHEST)
    tab = tab.reshape(3, A_HEADS // 2, 2, DSA_KB, DSA_Q)
    return jnp.transpose(tab, (0, 1, 3, 2, 4)).reshape(3, A_HEADS // 2, DSA_KB, 2 * DSA_Q)


def _dsa(qa_t, qi_t, wi_t, ga, ka, va_t, ki, btab, top_k):
    s = ga.shape[0]
    q, kb = DSA_Q, DSA_KB
    wide = DSA_WIDE * kb
    tri = jnp.asarray(np.tril(np.ones((wide, wide), np.float32), -1), BF16)
    whole = lambda shape: pl.BlockSpec(shape, lambda i: (0,) * len(shape), pipeline_mode=pl.Buffered(1))
    return pl.pallas_call(
        functools.partial(_dsa_kernel, top_k=top_k),
        grid=(s // q,),
        in_specs=[pl.BlockSpec((A_HEADS * LANES, q), lambda i: (0, i)),
                  pl.BlockSpec((IDX_HEADS * LANES, q), lambda i: (0, i)),
                  pl.BlockSpec((SUBLANES, q), lambda i: (0, i)),
                  pl.BlockSpec((q, A_W), lambda i: (i, 0)),
                  whole((s, A_W)), whole((A_W, s)), whole((s, LANES)),
                  whole((3, A_HEADS // 2, kb, 2 * q)), whole((wide, wide))],
        out_specs=pl.BlockSpec((q, A_W), lambda i: (i, 0)),
        out_shape=jax.ShapeDtypeStruct((s, A_W), BF16),
        scratch_shapes=[pltpu.VMEM((s, q), jnp.int32),
                        pltpu.VMEM((2, A_HEADS // 2, DSA_WIDE * kb, 2 * q), F32),
                        pltpu.VMEM((A_HEADS // 2, LANES, 2 * q), F32),
                        pltpu.VMEM((A_HEADS // 2, SUBLANES, 2 * q), F32),
                        pltpu.VMEM((A_HEADS // 2, SUBLANES, 2 * q), F32)],
        compiler_params=pltpu.CompilerParams(dimension_semantics=("arbitrary",), vmem_limit_bytes=VMEM_LIMIT),
        name="dsa",
    )(qa_t, qi_t, wi_t, ga, ka, va_t, ki, btab, tri)


GLA_BLOCK = 512


def _gla_kernel(qb_ref, kb_ref, vb_ref, gb_ref, lr_ref, wlr_ref, blr_ref, gn_ref, tril_ref, o_ref,
                state_ref, b_ref, att_ref):
    c, sub = GLA_CHUNK, GLA_SUB
    nsub = c // sub

    @pl.when(pl.program_id(0) == 0)
    def _():
        state_ref[...] = jnp.zeros(state_ref.shape, F32)

    dn_t = (((1,), (1,)), ((), ()))
    dn_0 = (((0,), (0,)), ((), ()))
    hp = lax.Precision.HIGHEST
    row_i = lax.broadcasted_iota(jnp.int32, (sub, sub), 0)
    col_j = lax.broadcasted_iota(jnp.int32, (sub, sub), 1)

    def chunk(ci, carry):
        r0 = pl.multiple_of(ci * c, c)
        z = jnp.dot(lr_ref[pl.ds(r0, c), :], wlr_ref[...], precision=hp, preferred_element_type=F32) + blr_ref[...]
        log_a = (jnp.minimum(z, 0.0) - jnp.log1p(jnp.exp(-jnp.abs(z)))) * (1.0 / GLA_TAU)
        b_ref[...] = jnp.dot(tril_ref[...], log_a, precision=hp, preferred_element_type=F32)
        for h in range(B_HEADS):
            ls = slice(h * LANES, (h + 1) * LANES)
            b = b_ref[:, ls]
            qh = qb_ref[pl.ds(r0, c), ls]
            kh = kb_ref[pl.ds(r0, c), ls]
            vh = vb_ref[pl.ds(r0, c), ls]
            state_t = state_ref[h]
            b_last = b[c - 1:c, :]
            o = lax.dot_general((qh * jnp.exp(b)).astype(BF16), state_t.astype(BF16), dn_t,
                                preferred_element_type=F32)
            for si in range(nsub):
                rs = slice(si * sub, (si + 1) * sub)
                b_i, q_i = b[rs], qh[rs]
                if si > 0:
                    ref_row = b[si * sub:si * sub + 1, :]
                    q_t = (q_i * jnp.exp(b_i - ref_row)).astype(BF16)
                    k_t = (kh[:si * sub] * jnp.exp(ref_row - b[:si * sub])).astype(BF16)
                    att_ref[rs, :si * sub] = lax.dot_general(q_t, k_t, dn_t, preferred_element_type=F32)
                k_i = kh[rs]
                diag = jnp.zeros((sub, sub), F32)
                for jl in range(sub):
                    diff = jnp.minimum(b_i - b_i[jl:jl + 1, :], 0.0)
                    col = jnp.sum(q_i * k_i[jl:jl + 1, :] * jnp.exp(diff), axis=-1, keepdims=True)
                    diag = jnp.where(col_j == jl, col, diag)
                att_ref[rs, rs] = jnp.where(col_j <= row_i, diag, 0.0)
                if si + 1 < nsub:
                    att_ref[rs, (si + 1) * sub:] = jnp.zeros((sub, c - (si + 1) * sub), F32)
            o = o + jnp.dot(att_ref[...].astype(BF16), vh.astype(BF16), preferred_element_type=F32)
            k_dec = (kh * jnp.exp(b_last - b)).astype(BF16)
            state_ref[h] = (jnp.exp(b_last) * state_t
                            + lax.dot_general(vh.astype(BF16), k_dec, dn_0, preferred_element_type=F32))
            o = o * lax.rsqrt(jnp.mean(o * o, axis=-1, keepdims=True) + EPS) * gn_ref[...]
            o_ref[pl.ds(r0, c), ls] = (o * _silu(gb_ref[pl.ds(r0, c), ls])).astype(o_ref.dtype)
        return carry

    lax.fori_loop(0, GLA_BLOCK // c, chunk, 0)


def _gla(qb, kb, vb, gb, lr, wlr_pad, blr_pad, gn):
    s = qb.shape[0]
    blk = min(GLA_BLOCK, s)
    assert blk == GLA_BLOCK and s % blk == 0
    tril = jnp.asarray(np.tril(np.ones((GLA_CHUNK, GLA_CHUNK), np.float32)))
    row = lambda wd: pl.BlockSpec((blk, wd), lambda i: (i, 0))
    const = lambda shape: pl.BlockSpec(shape, lambda i: (0,) * len(shape))
    return pl.pallas_call(
        _gla_kernel,
        grid=(s // blk,),
        in_specs=[row(B_HEADS * LANES), row(B_HEADS * LANES), row(B_VW), row(B_VW), row(LANES),
                  const((LANES, B_HEADS * LANES)), const((1, B_HEADS * LANES)), const((1, B_DV)),
                  const((GLA_CHUNK, GLA_CHUNK))],
        out_specs=row(B_VW),
        out_shape=jax.ShapeDtypeStruct((s, B_VW), BF16),
        scratch_shapes=[pltpu.VMEM((B_HEADS, LANES, B_DV), F32),
                        pltpu.VMEM((GLA_CHUNK, B_HEADS * LANES), F32),
                        pltpu.VMEM((GLA_CHUNK, GLA_CHUNK), F32)],
        compiler_params=pltpu.CompilerParams(dimension_semantics=("arbitrary",), vmem_limit_bytes=VMEM_LIMIT),
        name="gla",
    )(qb, kb, vb, gb, lr, wlr_pad, blr_pad, gn, tril)


def _mid_kernel(x_ref, ma_ref, mb_ref, wo_ref, g_ref, wi_ref, x1_ref, u_ref, sg_ref):
    x1 = (x_ref[...]
          + jnp.dot(ma_ref[...], wo_ref[:A_W, :], preferred_element_type=F32)
          + jnp.dot(mb_ref[...], wo_ref[A_W:, :], preferred_element_type=F32))
    x1_ref[...] = x1
    h = (x1 * lax.rsqrt(jnp.mean(x1 * x1, axis=-1, keepdims=True) + EPS) * g_ref[...]).astype(BF16)
    val = jnp.dot(h, wi_ref[:, :D_MODEL], preferred_element_type=F32)
    glu = jnp.dot(h, wi_ref[:, D_MODEL:2 * D_MODEL], preferred_element_type=F32)
    gate = jnp.dot(h, wi_ref[:, 2 * D_MODEL:], preferred_element_type=F32)
    u_ref[...] = val * jax.nn.sigmoid(glu)
    sg_ref[...] = _silu(gate)


def _mid(x2d, mix_a, mix_b, w_out, g, w_in, tm):
    s = x2d.shape[0]
    row = lambda wd: pl.BlockSpec((tm, wd), lambda i: (i, 0))
    const = lambda shape: pl.BlockSpec(shape, lambda i: (0,) * len(shape))
    return pl.pallas_call(
        _mid_kernel,
        grid=(s // tm,),
        in_specs=[row(D_MODEL), row(A_W), row(B_VW), const((A_W + B_VW, D_MODEL)), const((1, D_MODEL)),
                  const((D_MODEL, 3 * D_MODEL))],
        out_specs=[row(D_MODEL), row(D_MODEL), row(D_MODEL)],
        out_shape=[jax.ShapeDtypeStruct((s, D_MODEL), F32)] * 3,
        compiler_params=pltpu.CompilerParams(dimension_semantics=("arbitrary",), vmem_limit_bytes=VMEM_LIMIT),
        name="mid",
    )(x2d, mix_a, mix_b, w_out, g, w_in)


HALO = 32
CONV_ROWS = 128


def _odd_out_kernel(u_ref, halo_ref, sg_ref, x1_ref, cw_ref, cb_ref, lg_ref, lb_ref, wo_ref, nf_ref, o_ref,
                    ubuf_ref, y_ref, shift_ref, *, tm):
    i = pl.program_id(0)
    ubuf_ref[HALO:, :] = u_ref[...]
    ubuf_ref[:HALO, :] = jnp.where(i > 0, halo_ref[...], 0.0)
    base = HALO - (CONV_WIDTH - 1)
    for cs in range(D_MODEL // LANES):
        ls = slice(cs * LANES, (cs + 1) * LANES)
        for r0 in range(0, tm, CONV_ROWS):
            acc = jnp.zeros((CONV_ROWS, LANES), F32)
            for res in range(SUBLANES):
                taps = [k for k in range(CONV_WIDTH) if (base + k) % SUBLANES == res]
                first = base + taps[0]
                rows = taps[-1] - taps[0] + CONV_ROWS
                shift_ref[0:rows, :] = ubuf_ref[r0 + first:r0 + first + rows, ls]
                aligned = shift_ref[0:rows, :]
                for k in taps:
                    a = base + k - first
                    acc = acc + aligned[a:a + CONV_ROWS] * cw_ref[k:k + 1, ls]
            y_ref[r0:r0 + CONV_ROWS, ls] = acc + cb_ref[:, ls]
    y = y_ref[...]
    mu = jnp.mean(y, axis=-1, keepdims=True)
    yc = y - mu
    var = jnp.mean(yc * yc, axis=-1, keepdims=True)
    yn = yc * lax.rsqrt(var + EPS) * lg_ref[...] + lb_ref[...]
    mixed = (_silu(yn) * sg_ref[...]).astype(BF16)
    x2 = x1_ref[...] + jnp.dot(mixed, wo_ref[...], preferred_element_type=F32)
    o_ref[...] = x2 * lax.rsqrt(jnp.mean(x2 * x2, axis=-1, keepdims=True) + EPS) * nf_ref[...]


def _odd_out(u, sg, x1, cw_pad, cb, lg, lb, w_out, nf, tm):
    s = u.shape[0]
    row = pl.BlockSpec((tm, D_MODEL), lambda i: (i, 0))
    const = lambda shape: pl.BlockSpec(shape, lambda i: (0,) * len(shape))
    halo = pl.BlockSpec((HALO, D_MODEL), lambda i: (jnp.maximum(i * (tm // HALO) - 1, 0), 0))
    return pl.pallas_call(
        functools.partial(_odd_out_kernel, tm=tm),
        grid=(s // tm,),
        in_specs=[row, halo, row, row, const((HALO, D_MODEL)), const((1, D_MODEL)), const((1, D_MODEL)),
                  const((1, D_MODEL)), const((D_MODEL, D_MODEL)), const((1, D_MODEL))],
        out_specs=row,
        out_shape=jax.ShapeDtypeStruct((s, D_MODEL), F32),
        scratch_shapes=[pltpu.VMEM((HALO + tm, D_MODEL), F32), pltpu.VMEM((tm, D_MODEL), F32),
                        pltpu.VMEM((HALO + CONV_ROWS, LANES), F32)],
        compiler_params=pltpu.CompilerParams(dimension_semantics=("arbitrary",), vmem_limit_bytes=VMEM_LIMIT),
        name="odd_out",
    )(u, u, sg, x1, cw_pad, cb, lg, lb, w_out, nf)


def kernel(x, rel_bias, norm_even, w_in_even, w_gla_lr, b_gla_lr, gla_norm, w_out_even, norm_odd, w_in_odd,
           conv_w, conv_b, conv_ln_g, conv_ln_b, w_out_odd, norm_f):
    bsz, s, d = x.shape
    assert bsz == 1 and d == D_MODEL and s % GLA_BLOCK == 0
    assert norm_even.shape[0] == 1 and norm_odd.shape[0] == 1
    top_k = min(TOPK_MAX, s // 4)
    tm = 256
    x2d = x.reshape(s, d)

    wt_pad = _even_in_weights(w_in_even[0])
    qa_t, ka, va_t, ga, qi_t, ki, wi_t, qb, kb, vb, gb, lr = _even_in(x2d, norm_even[0].reshape(1, d), wt_pad, tm)
    mix_a = _dsa(qa_t, qi_t, wi_t, ga, ka, va_t, ki, _dsa_bias_table(rel_bias), top_k)

    head_pad = ((0, 0), (0, 0), (0, LANES - B_DK))
    wlr_pad = jnp.pad(w_gla_lr[0].reshape(GLA_RANK, B_HEADS, B_DK), head_pad).reshape(GLA_RANK, B_HEADS * LANES)
    wlr_pad = jnp.pad(wlr_pad, ((0, LANES - GLA_RANK), (0, 0)))
    blr_pad = jnp.pad(b_gla_lr[0].reshape(1, B_HEADS, B_DK), head_pad).reshape(1, B_HEADS * LANES)
    mix_b = _gla(qb, kb, vb, gb, lr, wlr_pad, blr_pad, gla_norm[0].reshape(1, B_DV))

    x1, u, sg = _mid(x2d, mix_a, mix_b, w_out_even[0].astype(BF16), norm_odd[0].reshape(1, d),
                     w_in_odd[0].astype(BF16), tm)

    cw_pad = jnp.zeros((HALO, d), F32).at[:CONV_WIDTH].set(conv_w[0])
    out = _odd_out(u, sg, x1, cw_pad, conv_b[0].reshape(1, d), conv_ln_g[0].reshape(1, d),
                   conv_ln_b[0].reshape(1, d), w_out_odd[0].astype(BF16), norm_f.reshape(1, d), tm)
    return out.reshape(bsz, s, d)
```

```python
import functools
import math

import jax
import jax.numpy as jnp
import numpy as np
from jax import lax
from jax.experimental import pallas as pl
from jax.experimental.pallas import tpu as pltpu

D_MODEL = 1024
A_HEADS = 8
A_HEAD_DIM = 64
IDX_HEADS = 4
IDX_DIM = 64
TOPK_MAX = 256
B_HEADS = 4
B_DK = 64
B_DV = 128
GLA_RANK = 16
GLA_TAU = 16.0
GLA_CHUNK = 64
GLA_SUB = 16
CONV_WIDTH = 31
REL_BUCKETS = 32
REL_MAX_EXACT = 16
REL_MAX_DIST = 128
EPS = 1e-6

A_W = A_HEADS * A_HEAD_DIM
B_KW = B_HEADS * B_DK
B_VW = B_HEADS * B_DV

LANES = 128
SUBLANES = 8
VMEM_LIMIT = 56 * 1024 * 1024

F32 = jnp.float32
BF16 = jnp.bfloat16
NEG_INF = float("-inf")
INT_MIN = -(2 ** 31)
NEG_INF_BITS = int(np.array(-np.inf, np.float32).view(np.int32))
NEG_INF_KEY = NEG_INF_BITS ^ 0x7FFFFFFF
M_INIT = -1e30
LOG2E = math.log2(math.e)

C_QA = 0
C_KA = C_QA + A_HEADS * LANES
C_VA = C_KA + A_W
C_GA = C_VA + A_W
C_QI = C_GA + A_W
C_KI = C_QI + IDX_HEADS * LANES
C_WI = C_KI + LANES
C_QB = C_WI + LANES
C_KB = C_QB + B_HEADS * LANES
C_VB = C_KB + B_HEADS * LANES
C_GB = C_VB + B_VW
C_LR = C_GB + B_VW
C_END = C_LR + LANES


def _t5_bucket_np(dist):
    dist = np.maximum(dist, 0)
    d = np.maximum(dist, 1).astype(np.float32)
    large = REL_MAX_EXACT + (np.log(d / REL_MAX_EXACT) / math.log(REL_MAX_DIST / REL_MAX_EXACT)
                             * (REL_BUCKETS - REL_MAX_EXACT)).astype(np.int32)
    large = np.minimum(large, REL_BUCKETS - 1)
    return np.where(dist < REL_MAX_EXACT, dist, large)


def _silu(x):
    return x * jax.nn.sigmoid(x)


def _even_in_kernel(x_ref, g_ref, wt_ref, qat_ref, ka_ref, vat_ref, ga_ref, qit_ref, ki_ref, wit_ref,
                    qb_ref, kb_ref, vb_ref, gb_ref, lr_ref):
    x = x_ref[...]
    h = (x * lax.rsqrt(jnp.mean(x * x, axis=-1, keepdims=True) + EPS) * g_ref[...]).astype(BF16)
    dn_t = (((1,), (1,)), ((), ()))

    def proj(lo, hi):
        return lax.dot_general(h, wt_ref[lo:hi, :], dn_t, preferred_element_type=F32)

    def proj_t(lo, hi):
        return lax.dot_general(wt_ref[lo:hi, :], h, dn_t, preferred_element_type=F32)

    qat_ref[...] = proj_t(C_QA, C_KA).astype(BF16)
    ka_ref[...] = proj(C_KA, C_VA).astype(BF16)
    vat_ref[...] = proj_t(C_VA, C_GA).astype(BF16)
    ga_ref[...] = proj(C_GA, C_QI)
    qit_ref[...] = proj_t(C_QI, C_KI).astype(BF16)
    ki_ref[...] = proj(C_KI, C_WI).astype(BF16)
    wit_ref[...] = proj_t(C_WI, C_WI + SUBLANES)
    qb_ref[...] = proj(C_QB, C_KB)
    kb_ref[...] = proj(C_KB, C_VB)
    vb_ref[...] = proj(C_VB, C_GB)
    gb_ref[...] = proj(C_GB, C_LR)
    lr_ref[...] = proj(C_LR, C_END)


def _even_in_layout():
    cols = np.cumsum([0, A_W, A_W, A_W, A_W, IDX_HEADS * IDX_DIM, IDX_DIM, IDX_HEADS,
                      B_KW, B_KW, B_VW, B_VW, GLA_RANK])
    (s_qa, s_ka, s_va, s_ga, s_qi, s_ki, s_wi, s_qb, s_kb, s_vb, s_gb, s_lr, _) = [int(c) for c in cols]
    src = np.full((C_END,), -1, np.int32)
    scale = np.ones((C_END,), np.float32)

    def put(dst, start, width, sc=1.0):
        src[dst:dst + width] = np.arange(start, start + width)
        scale[dst:dst + width] = sc

    for h in range(A_HEADS):
        put(C_QA + h * LANES + (h % 2) * A_HEAD_DIM, s_qa + h * A_HEAD_DIM, A_HEAD_DIM, (A_HEAD_DIM ** -0.5) * LOG2E)
    put(C_KA, s_ka, A_W)
    put(C_VA, s_va, A_W)
    put(C_GA, s_ga, A_W)
    for h in range(IDX_HEADS):
        put(C_QI + h * LANES, s_qi + h * IDX_DIM, IDX_DIM)
    put(C_KI, s_ki, IDX_DIM)
    put(C_WI, s_wi, IDX_HEADS, (IDX_DIM ** -0.5) * (IDX_HEADS ** -0.5))
    for h in range(B_HEADS):
        put(C_QB + h * LANES, s_qb + h * B_DK, B_DK, B_DK ** -0.5)
        put(C_KB + h * LANES, s_kb + h * B_DK, B_DK)
    put(C_VB, s_vb, B_VW)
    put(C_GB, s_gb, B_VW)
    put(C_LR, s_lr, GLA_RANK)
    return src, scale


def _even_in_weights(w):
    src, scale = _even_in_layout()
    pieces, start = [], 0
    for end in range(1, C_END + 1):
        if end == C_END or (src[end] >= 0) != (src[start] >= 0) or (src[end] >= 0 and src[end] != src[end - 1] + 1):
            if src[start] >= 0:
                pieces.append(w[:, int(src[start]):int(src[end - 1]) + 1])
            else:
                pieces.append(jnp.zeros((D_MODEL, end - start), w.dtype))
            start = end
    w_pad = jnp.concatenate(pieces, axis=1) * jnp.asarray(scale)[None, :]
    return w_pad.T.astype(BF16)


def _even_in(x2d, g, wt_pad, tm):
    s = x2d.shape[0]
    row = lambda wd, dt: (pl.BlockSpec((tm, wd), lambda i: (i, 0)), jax.ShapeDtypeStruct((s, wd), dt))
    col = lambda ht, dt: (pl.BlockSpec((ht, tm), lambda i: (0, i)), jax.ShapeDtypeStruct((ht, s), dt))
    outs = [col(C_KA - C_QA, BF16), row(A_W, BF16), col(A_W, BF16), row(A_W, F32), col(C_KI - C_QI, BF16),
            row(LANES, BF16), col(SUBLANES, F32), row(C_KB - C_QB, F32), row(C_VB - C_KB, F32), row(B_VW, F32),
            row(B_VW, F32), row(LANES, F32)]
    return pl.pallas_call(
        _even_in_kernel,
        grid=(s // tm,),
        in_specs=[pl.BlockSpec((tm, D_MODEL), lambda i: (i, 0)),
                  pl.BlockSpec((1, D_MODEL), lambda i: (0, 0)),
                  pl.BlockSpec((C_END, D_MODEL), lambda i: (0, 0))],
        out_specs=[spec for spec, _ in outs],
        out_shape=[shape for _, shape in outs],
        compiler_params=pltpu.CompilerParams(dimension_semantics=("arbitrary",), vmem_limit_bytes=VMEM_LIMIT),
        name="even_in",
    )(x2d, g, wt_pad)


DSA_Q = 128
DSA_KB = 128
DSA_WIDE = 4
DSA_NACC = 4
DSA_CELLS = 64
DSA_CELL_BITS = 14


def _f32_key(x):
    bits = lax.bitcast_convert_type(x, jnp.int32)
    return bits ^ (lax.shift_right_arithmetic(bits, 31) & 0x7FFFFFFF)


def _fold_keys(x, op):
    n, w = x.shape
    group = DSA_NACC * SUBLANES
    if n % group == 0 and n > group:
        x = op(x.reshape(n // group, DSA_NACC, SUBLANES, w), axis=0)
        return op(x, axis=0)
    return op(x.reshape(n // SUBLANES, SUBLANES, w), axis=0)


def _dsa_kernel(qat_ref, qit_ref, wit_ref, ga_ref, ka_ref, vat_ref, ki_ref, btab_ref, tri_ref, o_ref,
                slab_ref, s_ref, acc_ref, m_ref, l_ref, *, top_k):
    q, kb, wide = DSA_Q, DSA_KB, DSA_WIDE * DSA_KB
    i = pl.program_id(0)
    nkb = i + 1
    nwide = jnp.maximum(i - 1, 0) // DSA_WIDE
    n0 = nwide * DSA_WIDE
    t_row = i * q + lax.broadcasted_iota(jnp.int32, (1, q), 1)
    key_off = lax.broadcasted_iota(jnp.int32, (kb, 1), 0)
    npair = A_HEADS // 2

    def wide_rows(jw):
        return pl.ds(pl.multiple_of(jw * wide, wide), wide)

    def narrow_rows(j):
        return pl.ds(pl.multiple_of(j * kb, kb), kb)

    def pipelined(first, last, produce, consume):
        def body(t, carry):
            consume(t, (t - first) % 2, carry)
            return produce(jnp.minimum(t + 1, last - 1), (t + 1 - first) % 2)

        lax.fori_loop(first, last, body, produce(first, 0))

    w_head = [wit_ref[h:h + 1, :] for h in range(IDX_HEADS)]

    def scores(rows):
        k_idx = ki_ref[rows, :]
        sc = None
        for h in range(IDX_HEADS):
            s = jnp.dot(k_idx, qit_ref[h * LANES:(h + 1) * LANES, :], preferred_element_type=F32)
            term = w_head[h] * jnp.maximum(s, 0.0)
            sc = term if sc is None else sc + term
        return sc

    def score_wide(jw, c):
        rows = wide_rows(jw)
        slab_ref[rows, :] = _f32_key(scores(rows))
        return c

    def score_narrow(j, c):
        rows = narrow_rows(j)
        sc = jnp.where(j * kb + key_off <= t_row, scores(rows), NEG_INF)
        slab_ref[rows, :] = _f32_key(sc)
        return c

    lax.fori_loop(0, nwide, score_wide, 0)
    lax.fori_loop(n0, nkb, score_narrow, 0)

    def count_ge(trial):
        def cnt(blk):
            return _fold_keys(jnp.where(blk >= trial, 1.0, 0.0), jnp.sum)

        acc = lax.fori_loop(0, nwide, lambda jw, a: a + cnt(slab_ref[wide_rows(jw), :]),
                            jnp.zeros((SUBLANES, q), F32))
        acc = lax.fori_loop(n0, nkb, lambda j, a: a + cnt(slab_ref[narrow_rows(j), :]), acc)
        return jnp.sum(acc, axis=0, keepdims=True).astype(jnp.int32)

    assert DSA_CELLS * SUBLANES >= top_k and wide % (DSA_CELLS * SUBLANES) == 0

    def cells_wide(jw, cm):
        blk = slab_ref[wide_rows(jw), :].reshape(wide // (DSA_CELLS * SUBLANES), DSA_CELLS, SUBLANES, q)
        return jnp.maximum(cm, jnp.max(blk, axis=0))

    def cells_narrow(j, cm):
        blk = slab_ref[narrow_rows(j), :].reshape(kb // SUBLANES, SUBLANES, q)
        return jnp.concatenate([jnp.maximum(cm[:kb // SUBLANES], blk), cm[kb // SUBLANES:]], axis=0)

    cell_max = lax.fori_loop(0, nwide, cells_wide, jnp.full((DSA_CELLS, SUBLANES, q), INT_MIN, jnp.int32))
    cell_max = lax.fori_loop(n0, nkb, cells_narrow, cell_max)
    hi_key = jnp.max(jnp.max(cell_max, axis=0), axis=0, keepdims=True)

    def cell_bisect(b, c):
        trial = c + lax.shift_left(jnp.int32(1), 31 - b)
        cnt = jnp.sum(jnp.sum(jnp.where(cell_max >= trial, 1.0, 0.0), axis=0), axis=0, keepdims=True)
        return jnp.where(cnt >= top_k, trial, c)

    lo_key = lax.fori_loop(0, DSA_CELL_BITS, cell_bisect, jnp.full((1, q), INT_MIN, jnp.int32))
    same_sign = jnp.min(lo_key ^ hi_key) >= 0
    base = jnp.where(same_sign, lo_key, INT_MIN)
    span = jnp.where(same_sign, hi_key - lo_key, -1)
    nbits = jnp.where(same_sign, jnp.max(32 - lax.clz(span)), 32)

    def bisect(b, state):
        off, cnt = state
        trial_off = off + lax.shift_left(jnp.int32(1), nbits - 1 - b)
        in_span = (trial_off ^ INT_MIN) <= (span ^ INT_MIN)
        c = count_ge(base + trial_off)
        ok = jnp.logical_and(in_span, c >= top_k)
        return jnp.where(ok, trial_off, off), jnp.where(ok, c, cnt)

    off, cnt_thr = lax.fori_loop(0, nbits, bisect, (jnp.zeros((1, q), jnp.int32), jnp.full((1, q), -1, jnp.int32)))
    thr = base + off
    cnt_thr = lax.cond(jnp.min(cnt_thr) < 0, lambda: count_ge(thr), lambda: cnt_thr)
    tied = jnp.logical_and(cnt_thr > top_k, thr > NEG_INF_KEY)
    any_tie = jnp.max(jnp.where(tied, 1, 0)) > 0

    thr_valid = jnp.maximum(thr, NEG_INF_KEY + 1)

    def mask_fast():
        def body(rows):
            slab_ref[rows, :] = jnp.where(slab_ref[rows, :] >= thr_valid, 0, NEG_INF_BITS)

        lax.fori_loop(0, nwide, lambda jw, c: body(wide_rows(jw)), None)
        lax.fori_loop(n0, nkb, lambda j, c: body(narrow_rows(j)), None)

    def mask_tied():
        need = (top_k - count_ge(thr + 1)).astype(F32)

        def body(rows, n, seen):
            key = slab_ref[rows, :]
            eq = key == thr
            eq_f = jnp.where(eq, 1.0, 0.0)
            seen_after = seen + jnp.sum(_fold_keys(eq_f, jnp.sum), axis=0, keepdims=True)
            take_all = seen_after <= need
            crossing = jnp.logical_and(jnp.logical_not(take_all), seen < need)

            def rank_inside():
                before = jnp.dot(tri_ref[0:n, 0:n], eq_f.astype(BF16), preferred_element_type=F32)
                return jnp.where(seen + before < need, 1.0, 0.0)

            picked = lax.cond(jnp.max(jnp.where(crossing, 1, 0)) > 0, rank_inside,
                              lambda: jnp.broadcast_to(jnp.where(take_all, 1.0, 0.0), (n, q)))
            sel = jnp.logical_or(key > thr, jnp.logical_and(eq, picked > 0.0))
            sel = jnp.logical_and(sel, key > NEG_INF_KEY)
            slab_ref[rows, :] = jnp.where(sel, 0, NEG_INF_BITS)
            return seen_after

        seen = lax.fori_loop(0, nwide, lambda jw, c: body(wide_rows(jw), wide, c), jnp.zeros((1, q), F32))
        lax.fori_loop(n0, nkb, lambda j, c: body(narrow_rows(j), kb, c), seen)

    lax.cond(any_tie, mask_tied, mask_fast)

    m_ref[...] = jnp.full(m_ref.shape, M_INIT, F32)
    l_ref[...] = jnp.zeros(l_ref.shape, F32)
    acc_ref[...] = jnp.zeros(acc_ref.shape, F32)

    def logits(rows, n, buf, rel):
        mask = lax.bitcast_convert_type(slab_ref[rows, :], F32)
        mask2 = jnp.concatenate([mask, mask], axis=1)
        tile_max = []
        for p in range(npair):
            ls = slice(p * LANES, (p + 1) * LANES)
            w_q = jnp.concatenate([qat_ref[(2 * p) * LANES:(2 * p + 1) * LANES, :],
                                   qat_ref[(2 * p + 1) * LANES:(2 * p + 2) * LANES, :]], axis=1)
            s = jnp.dot(ka_ref[rows, ls], w_q, preferred_element_type=F32) + mask2
            if rel is not None:
                s = s + btab_ref[rel, p]
            s_ref[buf, p, 0:n, :] = s
            tile_max.append(_fold_keys(s, jnp.max))
        return tuple(tile_max)

    def attend(rows, n, buf, tile_max):
        for p in range(npair):
            ls = slice(p * LANES, (p + 1) * LANES)
            m_old = m_ref[p]
            m_new = jnp.maximum(m_old, jnp.max(tile_max[p], axis=0, keepdims=True))
            alpha = jnp.exp2(m_old - m_new)
            pexp = jnp.exp2(s_ref[buf, p, 0:n, :] - m_new[0:1])
            m_ref[p] = m_new
            l_ref[p] = alpha * l_ref[p] + _fold_keys(pexp, jnp.sum)
            acc_ref[p] = alpha[0:1] * acc_ref[p] + jnp.dot(vat_ref[ls, rows], pexp.astype(BF16),
                                                           preferred_element_type=F32)

    @pl.when(nwide > 0)
    def _():
        pipelined(0, nwide, lambda jw, buf: logits(wide_rows(jw), wide, buf, None),
                  lambda jw, buf, tile_max: attend(wide_rows(jw), wide, buf, tile_max))

    pipelined(n0, nkb, lambda j, buf: logits(narrow_rows(j), kb, buf, jnp.minimum(i - j, 2)),
              lambda j, buf, tile_max: attend(narrow_rows(j), kb, buf, tile_max))

    for p in range(npair):
        o_t = acc_ref[p] / jnp.sum(l_ref[p], axis=0, keepdims=True)
        o_pair = jnp.concatenate([o_t[:A_HEAD_DIM, :q], o_t[A_HEAD_DIM:, q:]], axis=0).T
        ls = slice(p * LANES, (p + 1) * LANES)
        o_ref[:, ls] = (o_pair * _silu(ga_ref[:, ls])).astype(o_ref.dtype)


def _dsa_bias_table(rel_bias):
    ii = np.arange(DSA_Q)[:, None]
    jj = np.arange(DSA_KB)[None, :]
    buckets = np.stack([_t5_bucket_np(ii - jj), _t5_bucket_np(DSA_KB + ii - jj),
                        np.full((DSA_Q, DSA_KB), REL_BUCKETS - 1)])
    assert (_t5_bucket_np(np.arange(DSA_KB + 1, 4 * DSA_KB)) == REL_BUCKETS - 1).all()
    rb = rel_bias.astype(F32) * LOG2E
    rb = rb - rb[REL_BUCKETS - 1]
    onehot = np.zeros((3, DSA_KB, DSA_Q, REL_BUCKETS), np.float32)
    np.put_along_axis(onehot, np.transpose(buckets, (0, 2, 1))[..., None], 1.0, axis=-1)
    tab = jnp.einsum('tsrb,bh->thsr', jnp.asarray(onehot), rb, precision=lax.Precision.HIGHEST)
    tab = tab.reshape(3, A_HEADS // 2, 2, DSA_KB, DSA_Q)
    return jnp.transpose(tab, (0, 1, 3, 2, 4)).reshape(3, A_HEADS // 2, DSA_KB, 2 * DSA_Q)


def _dsa(qa_t, qi_t, wi_t, ga, ka, va_t, ki, btab, top_k):
    s = ga.shape[0]
    q, kb = DSA_Q, DSA_KB
    wide = DSA_WIDE * kb
    tri = jnp.asarray(np.tril(np.ones((wide, wide), np.float32), -1), BF16)
    whole = lambda shape: pl.BlockSpec(shape, lambda i: (0,) * len(shape), pipeline_mode=pl.Buffered(1))
    return pl.pallas_call(
        functools.partial(_dsa_kernel, top_k=top_k),
        grid=(s // q,),
        in_specs=[pl.BlockSpec((A_HEADS * LANES, q), lambda i: (0, i)),
                  pl.BlockSpec((IDX_HEADS * LANES, q), lambda i: (0, i)),
                  pl.BlockSpec((SUBLANES, q), lambda i: (0, i)),
                  pl.BlockSpec((q, A_W), lambda i: (i, 0)),
                  whole((s, A_W)), whole((A_W, s)), whole((s, LANES)),
                  whole((3, A_HEADS // 2, kb, 2 * q)), whole((wide, wide))],
        out_specs=pl.BlockSpec((q, A_W), lambda i: (i, 0)),
        out_shape=jax.ShapeDtypeStruct((s, A_W), BF16),
        scratch_shapes=[pltpu.VMEM((s, q), jnp.int32),
                        pltpu.VMEM((2, A_HEADS // 2, DSA_WIDE * kb, 2 * q), F32),
                        pltpu.VMEM((A_HEADS // 2, LANES, 2 * q), F32),
                        pltpu.VMEM((A_HEADS // 2, SUBLANES, 2 * q), F32),
                        pltpu.VMEM((A_HEADS // 2, SUBLANES, 2 * q), F32)],
        compiler_params=pltpu.CompilerParams(dimension_semantics=("arbitrary",), vmem_limit_bytes=VMEM_LIMIT),
        name="dsa",
    )(qa_t, qi_t, wi_t, ga, ka, va_t, ki, btab, tri)


GLA_BLOCK = 512


def _gla_kernel(qb_ref, kb_ref, vb_ref, gb_ref, lr_ref, wlr_ref, blr_ref, gn_ref, tril_ref, o_ref,
                state_ref, b_ref, att_ref):
    c, sub = GLA_CHUNK, GLA_SUB
    nsub = c // sub

    @pl.when(pl.program_id(0) == 0)
    def _():
        state_ref[...] = jnp.zeros(state_ref.shape, F32)

    dn_t = (((1,), (1,)), ((), ()))
    dn_0 = (((0,), (0,)), ((), ()))
    hp = lax.Precision.HIGHEST
    row_i = lax.broadcasted_iota(jnp.int32, (sub, sub), 0)
    col_j = lax.broadcasted_iota(jnp.int32, (sub, sub), 1)

    def chunk(ci, carry):
        r0 = pl.multiple_of(ci * c, c)
        z = jnp.dot(lr_ref[pl.ds(r0, c), :], wlr_ref[...], precision=hp, preferred_element_type=F32) + blr_ref[...]
        log_a = (jnp.minimum(z, 0.0) - jnp.log1p(jnp.exp(-jnp.abs(z)))) * (LOG2E / GLA_TAU)
        b_ref[...] = jnp.dot(tril_ref[...], log_a, precision=hp, preferred_element_type=F32)
        for h in range(B_HEADS):
            ls = slice(h * LANES, (h + 1) * LANES)
            b = b_ref[:, ls]
            qh = qb_ref[pl.ds(r0, c), ls]
            kh = kb_ref[pl.ds(r0, c), ls]
            vh = vb_ref[pl.ds(r0, c), ls]
            state_t = state_ref[h]
            b_last = b[c - 1:c, :]
            o = lax.dot_general((qh * jnp.exp2(b)).astype(BF16), state_t.astype(BF16), dn_t,
                                preferred_element_type=F32)
            for si in range(nsub):
                rs = slice(si * sub, (si + 1) * sub)
                b_i, q_i = b[rs], qh[rs]
                if si > 0:
                    ref_row = b[si * sub:si * sub + 1, :]
                    q_t = (q_i * jnp.exp2(b_i - ref_row)).astype(BF16)
                    k_t = (kh[:si * sub] * jnp.exp2(ref_row - b[:si * sub])).astype(BF16)
                    att_ref[rs, :si * sub] = lax.dot_general(q_t, k_t, dn_t, preferred_element_type=F32)
                k_i = kh[rs]
                diag = jnp.zeros((sub, sub), F32)
                for jl in range(sub):
                    diff = jnp.minimum(b_i - b_i[jl:jl + 1, :], 0.0)
                    col = jnp.sum(q_i * k_i[jl:jl + 1, :] * jnp.exp2(diff), axis=-1, keepdims=True)
                    diag = jnp.where(col_j == jl, col, diag)
                att_ref[rs, rs] = jnp.where(col_j <= row_i, diag, 0.0)
                if si + 1 < nsub:
                    att_ref[rs, (si + 1) * sub:] = jnp.zeros((sub, c - (si + 1) * sub), F32)
            o = o + jnp.dot(att_ref[...].astype(BF16), vh.astype(BF16), preferred_element_type=F32)
            k_dec = (kh * jnp.exp2(b_last - b)).astype(BF16)
            state_ref[h] = (jnp.exp2(b_last) * state_t
                            + lax.dot_general(vh.astype(BF16), k_dec, dn_0, preferred_element_type=F32))
            o = o * lax.rsqrt(jnp.mean(o * o, axis=-1, keepdims=True) + EPS) * gn_ref[...]
            o_ref[pl.ds(r0, c), ls] = (o * _silu(gb_ref[pl.ds(r0, c), ls])).astype(o_ref.dtype)
        return carry

    lax.fori_loop(0, GLA_BLOCK // c, chunk, 0)


def _gla(qb, kb, vb, gb, lr, wlr_pad, blr_pad, gn):
    s = qb.shape[0]
    blk = min(GLA_BLOCK, s)
    assert blk == GLA_BLOCK and s % blk == 0
    tril = jnp.asarray(np.tril(np.ones((GLA_CHUNK, GLA_CHUNK), np.float32)))
    row = lambda wd: pl.BlockSpec((blk, wd), lambda i: (i, 0))
    const = lambda shape: pl.BlockSpec(shape, lambda i: (0,) * len(shape))
    return pl.pallas_call(
        _gla_kernel,
        grid=(s // blk,),
        in_specs=[row(B_HEADS * LANES), row(B_HEADS * LANES), row(B_VW), row(B_VW), row(LANES),
                  const((LANES, B_HEADS * LANES)), const((1, B_HEADS * LANES)), const((1, B_DV)),
                  const((GLA_CHUNK, GLA_CHUNK))],
        out_specs=row(B_VW),
        out_shape=jax.ShapeDtypeStruct((s, B_VW), BF16),
        scratch_shapes=[pltpu.VMEM((B_HEADS, LANES, B_DV), F32),
                        pltpu.VMEM((GLA_CHUNK, B_HEADS * LANES), F32),
                        pltpu.VMEM((GLA_CHUNK, GLA_CHUNK), F32)],
        compiler_params=pltpu.CompilerParams(dimension_semantics=("arbitrary",), vmem_limit_bytes=VMEM_LIMIT),
        name="gla",
    )(qb, kb, vb, gb, lr, wlr_pad, blr_pad, gn, tril)


def _mid_kernel(x_ref, ma_ref, mb_ref, wo_ref, g_ref, wi_ref, x1_ref, u_ref, sg_ref):
    x1 = (x_ref[...]
          + jnp.dot(ma_ref[...], wo_ref[:A_W, :], preferred_element_type=F32)
          + jnp.dot(mb_ref[...], wo_ref[A_W:, :], preferred_element_type=F32))
    x1_ref[...] = x1
    h = (x1 * lax.rsqrt(jnp.mean(x1 * x1, axis=-1, keepdims=True) + EPS) * g_ref[...]).astype(BF16)
    val = jnp.dot(h, wi_ref[:, :D_MODEL], preferred_element_type=F32)
    glu = jnp.dot(h, wi_ref[:, D_MODEL:2 * D_MODEL], preferred_element_type=F32)
    gate = jnp.dot(h, wi_ref[:, 2 * D_MODEL:], preferred_element_type=F32)
    u_ref[...] = val * jax.nn.sigmoid(glu)
    sg_ref[...] = _silu(gate)


def _mid(x2d, mix_a, mix_b, w_out, g, w_in, tm):
    s = x2d.shape[0]
    row = lambda wd: pl.BlockSpec((tm, wd), lambda i: (i, 0))
    const = lambda shape: pl.BlockSpec(shape, lambda i: (0,) * len(shape))
    return pl.pallas_call(
        _mid_kernel,
        grid=(s // tm,),
        in_specs=[row(D_MODEL), row(A_W), row(B_VW), const((A_W + B_VW, D_MODEL)), const((1, D_MODEL)),
                  const((D_MODEL, 3 * D_MODEL))],
        out_specs=[row(D_MODEL), row(D_MODEL), row(D_MODEL)],
        out_shape=[jax.ShapeDtypeStruct((s, D_MODEL), F32)] * 3,
        compiler_params=pltpu.CompilerParams(dimension_semantics=("arbitrary",), vmem_limit_bytes=VMEM_LIMIT),
        name="mid",
    )(x2d, mix_a, mix_b, w_out, g, w_in)


HALO = 32
CONV_ROWS = 128


def _odd_out_kernel(u_ref, halo_ref, sg_ref, x1_ref, cw_ref, cb_ref, lg_ref, lb_ref, wo_ref, nf_ref, o_ref,
                    ubuf_ref, y_ref, shift_ref, *, tm):
    i = pl.program_id(0)
    ubuf_ref[HALO:, :] = u_ref[...]
    ubuf_ref[:HALO, :] = jnp.where(i > 0, halo_ref[...], 0.0)
    base = HALO - (CONV_WIDTH - 1)
    for cs in range(D_MODEL // LANES):
        ls = slice(cs * LANES, (cs + 1) * LANES)
        for r0 in range(0, tm, CONV_ROWS):
            acc = jnp.zeros((CONV_ROWS, LANES), F32)
            for res in range(SUBLANES):
                taps = [k for k in range(CONV_WIDTH) if (base + k) % SUBLANES == res]
                first = base + taps[0]
                rows = taps[-1] - taps[0] + CONV_ROWS
                shift_ref[0:rows, :] = ubuf_ref[r0 + first:r0 + first + rows, ls]
                aligned = shift_ref[0:rows, :]
                for k in taps:
                    a = base + k - first
                    acc = acc + aligned[a:a + CONV_ROWS] * cw_ref[k:k + 1, ls]
            y_ref[r0:r0 + CONV_ROWS, ls] = acc + cb_ref[:, ls]
    y = y_ref[...]
    mu = jnp.mean(y, axis=-1, keepdims=True)
    yc = y - mu
    var = jnp.mean(yc * yc, axis=-1, keepdims=True)
    yn = yc * lax.rsqrt(var + EPS) * lg_ref[...] + lb_ref[...]
    mixed = (_silu(yn) * sg_ref[...]).astype(BF16)
    x2 = x1_ref[...] + jnp.dot(mixed, wo_ref[...], preferred_element_type=F32)
    o_ref[...] = x2 * lax.rsqrt(jnp.mean(x2 * x2, axis=-1, keepdims=True) + EPS) * nf_ref[...]


def _odd_out(u, sg, x1, cw_pad, cb, lg, lb, w_out, nf, tm):
    s = u.shape[0]
    row = pl.BlockSpec((tm, D_MODEL), lambda i: (i, 0))
    const = lambda shape: pl.BlockSpec(shape, lambda i: (0,) * len(shape))
    halo = pl.BlockSpec((HALO, D_MODEL), lambda i: (jnp.maximum(i * (tm // HALO) - 1, 0), 0))
    return pl.pallas_call(
        functools.partial(_odd_out_kernel, tm=tm),
        grid=(s // tm,),
        in_specs=[row, halo, row, row, const((HALO, D_MODEL)), const((1, D_MODEL)), const((1, D_MODEL)),
                  const((1, D_MODEL)), const((D_MODEL, D_MODEL)), const((1, D_MODEL))],
        out_specs=row,
        out_shape=jax.ShapeDtypeStruct((s, D_MODEL), F32),
        scratch_shapes=[pltpu.VMEM((HALO + tm, D_MODEL), F32), pltpu.VMEM((tm, D_MODEL), F32),
                        pltpu.VMEM((HALO + CONV_ROWS, LANES), F32)],
        compiler_params=pltpu.CompilerParams(dimension_semantics=("arbitrary",), vmem_limit_bytes=VMEM_LIMIT),
        name="odd_out",
    )(u, u, sg, x1, cw_pad, cb, lg, lb, w_out, nf)


def kernel(x, rel_bias, norm_even, w_in_even, w_gla_lr, b_gla_lr, gla_norm, w_out_even, norm_odd, w_in_odd,
           conv_w, conv_b, conv_ln_g, conv_ln_b, w_out_odd, norm_f):
    bsz, s, d = x.shape
    assert bsz == 1 and d == D_MODEL and s % GLA_BLOCK == 0
    assert norm_even.shape[0] == 1 and norm_odd.shape[0] == 1
    top_k = min(TOPK_MAX, s // 4)
    tm = 256
    x2d = x.reshape(s, d)

    wt_pad = _even_in_weights(w_in_even[0])
    qa_t, ka, va_t, ga, qi_t, ki, wi_t, qb, kb, vb, gb, lr = _even_in(x2d, norm_even[0].reshape(1, d), wt_pad, tm)
    mix_a = _dsa(qa_t, qi_t, wi_t, ga, ka, va_t, ki, _dsa_bias_table(rel_bias), top_k)

    head_pad = ((0, 0), (0, 0), (0, LANES - B_DK))
    wlr_pad = jnp.pad(w_gla_lr[0].reshape(GLA_RANK, B_HEADS, B_DK), head_pad).reshape(GLA_RANK, B_HEADS * LANES)
    wlr_pad = jnp.pad(wlr_pad, ((0, LANES - GLA_RANK), (0, 0)))
    blr_pad = jnp.pad(b_gla_lr[0].reshape(1, B_HEADS, B_DK), head_pad).reshape(1, B_HEADS * LANES)
    mix_b = _gla(qb, kb, vb, gb, lr, wlr_pad, blr_pad, gla_norm[0].reshape(1, B_DV))

    x1, u, sg = _mid(x2d, mix_a, mix_b, w_out_even[0].astype(BF16), norm_odd[0].reshape(1, d),
                     w_in_odd[0].astype(BF16), tm)

    cw_pad = jnp.zeros((HALO, d), F32).at[:CONV_WIDTH].set(conv_w[0])
    out = _odd_out(u, sg, x1, cw_pad, conv_b[0].reshape(1, d), conv_ln_g[0].reshape(1, d),
                   conv_ln_b[0].reshape(1, d), w_out_odd[0].astype(BF16), norm_f.reshape(1, d), tm)
    return out.reshape(bsz, s, d)
```

```python
import functools
import math

import jax
import jax.numpy as jnp
import numpy as np
from jax import lax
from jax.experimental import pallas as pl
from jax.experimental.pallas import tpu as pltpu

D_MODEL = 1024
A_HEADS = 8
A_HEAD_DIM = 64
IDX_HEADS = 4
IDX_DIM = 64
TOPK_MAX = 256
B_HEADS = 4
B_DK = 64
B_DV = 128
GLA_RANK = 16
GLA_TAU = 16.0
GLA_CHUNK = 64
GLA_SUB = 16
CONV_WIDTH = 31
REL_BUCKETS = 32
REL_MAX_EXACT = 16
REL_MAX_DIST = 128
EPS = 1e-6

A_W = A_HEADS * A_HEAD_DIM
B_KW = B_HEADS * B_DK
B_VW = B_HEADS * B_DV

LANES = 128
SUBLANES = 8
VMEM_LIMIT = 56 * 1024 * 1024

F32 = jnp.float32
BF16 = jnp.bfloat16
NEG_INF = float("-inf")
INT_MIN = -(2 ** 31)
NEG_INF_BITS = int(np.array(-np.inf, np.float32).view(np.int32))
NEG_INF_KEY = NEG_INF_BITS ^ 0x7FFFFFFF
M_INIT = -1e30
LOG2E = math.log2(math.e)

C_QA = 0
C_KA = C_QA + A_HEADS * LANES
C_VA = C_KA + A_W
C_GA = C_VA + A_W
C_QI = C_GA + A_W
C_KI = C_QI + IDX_HEADS * LANES
C_WI = C_KI + LANES
C_QB = C_WI + LANES
C_KB = C_QB + B_HEADS * LANES
C_VB = C_KB + B_HEADS * LANES
C_GB = C_VB + B_VW
C_LR = C_GB + B_VW
C_END = C_LR + LANES


def _t5_bucket_np(dist):
    dist = np.maximum(dist, 0)
    d = np.maximum(dist, 1).astype(np.float32)
    large = REL_MAX_EXACT + (np.log(d / REL_MAX_EXACT) / math.log(REL_MAX_DIST / REL_MAX_EXACT)
                             * (REL_BUCKETS - REL_MAX_EXACT)).astype(np.int32)
    large = np.minimum(large, REL_BUCKETS - 1)
    return np.where(dist < REL_MAX_EXACT, dist, large)


def _silu(x):
    return x * jax.nn.sigmoid(x)


def _even_in_kernel(x_ref, g_ref, wt_ref, qat_ref, ka_ref, vat_ref, ga_ref, qit_ref, ki_ref, wit_ref,
                    qb_ref, kb_ref, vb_ref, gb_ref, lr_ref):
    x = x_ref[...]
    h = (x * lax.rsqrt(jnp.mean(x * x, axis=-1, keepdims=True) + EPS) * g_ref[...]).astype(BF16)
    dn_t = (((1,), (1,)), ((), ()))

    def proj(lo, hi):
        return lax.dot_general(h, wt_ref[lo:hi, :], dn_t, preferred_element_type=F32)

    def proj_t(lo, hi):
        return lax.dot_general(wt_ref[lo:hi, :], h, dn_t, preferred_element_type=F32)

    qat_ref[...] = proj_t(C_QA, C_KA).astype(BF16)
    ka_ref[...] = proj(C_KA, C_VA).astype(BF16)
    vat_ref[...] = proj_t(C_VA, C_GA).astype(BF16)
    ga_ref[...] = proj(C_GA, C_QI)
    qit_ref[...] = proj_t(C_QI, C_KI).astype(BF16)
    ki_ref[...] = proj(C_KI, C_WI).astype(BF16)
    wit_ref[...] = proj_t(C_WI, C_WI + SUBLANES)
    qb_ref[...] = proj(C_QB, C_KB)
    kb_ref[...] = proj(C_KB, C_VB)
    vb_ref[...] = proj(C_VB, C_GB)
    gb_ref[...] = proj(C_GB, C_LR)
    lr_ref[...] = proj(C_LR, C_END)


def _even_in_layout():
    cols = np.cumsum([0, A_W, A_W, A_W, A_W, IDX_HEADS * IDX_DIM, IDX_DIM, IDX_HEADS,
                      B_KW, B_KW, B_VW, B_VW, GLA_RANK])
    (s_qa, s_ka, s_va, s_ga, s_qi, s_ki, s_wi, s_qb, s_kb, s_vb, s_gb, s_lr, _) = [int(c) for c in cols]
    src = np.full((C_END,), -1, np.int32)
    scale = np.ones((C_END,), np.float32)

    def put(dst, start, width, sc=1.0):
        src[dst:dst + width] = np.arange(start, start + width)
        scale[dst:dst + width] = sc

    for h in range(A_HEADS):
        put(C_QA + h * LANES + (h % 2) * A_HEAD_DIM, s_qa + h * A_HEAD_DIM, A_HEAD_DIM, (A_HEAD_DIM ** -0.5) * LOG2E)
    put(C_KA, s_ka, A_W)
    put(C_VA, s_va, A_W)
    put(C_GA, s_ga, A_W)
    for h in range(IDX_HEADS):
        put(C_QI + h * LANES, s_qi + h * IDX_DIM, IDX_DIM)
    put(C_KI, s_ki, IDX_DIM)
    put(C_WI, s_wi, IDX_HEADS, (IDX_DIM ** -0.5) * (IDX_HEADS ** -0.5))
    for h in range(B_HEADS):
        put(C_QB + h * LANES, s_qb + h * B_DK, B_DK, B_DK ** -0.5)
        put(C_KB + h * LANES, s_kb + h * B_DK, B_DK)
    put(C_VB, s_vb, B_VW)
    put(C_GB, s_gb, B_VW)
    put(C_LR, s_lr, GLA_RANK)
    return src, scale


def _even_in_weights(w):
    src, scale = _even_in_layout()
    pieces, start = [], 0
    for end in range(1, C_END + 1):
        if end == C_END or (src[end] >= 0) != (src[start] >= 0) or (src[end] >= 0 and src[end] != src[end - 1] + 1):
            if src[start] >= 0:
                pieces.append(w[:, int(src[start]):int(src[end - 1]) + 1])
            else:
                pieces.append(jnp.zeros((D_MODEL, end - start), w.dtype))
            start = end
    w_pad = jnp.concatenate(pieces, axis=1) * jnp.asarray(scale)[None, :]
    return w_pad.T.astype(BF16)


def _even_in(x2d, g, wt_pad, tm):
    s = x2d.shape[0]
    row = lambda wd, dt: (pl.BlockSpec((tm, wd), lambda i: (i, 0)), jax.ShapeDtypeStruct((s, wd), dt))
    col = lambda ht, dt: (pl.BlockSpec((ht, tm), lambda i: (0, i)), jax.ShapeDtypeStruct((ht, s), dt))
    outs = [col(C_KA - C_QA, BF16), row(A_W, BF16), col(A_W, BF16), row(A_W, F32), col(C_KI - C_QI, BF16),
            row(LANES, BF16), col(SUBLANES, F32), row(C_KB - C_QB, F32), row(C_VB - C_KB, F32), row(B_VW, F32),
            row(B_VW, F32), row(LANES, F32)]
    return pl.pallas_call(
        _even_in_kernel,
        grid=(s // tm,),
        in_specs=[pl.BlockSpec((tm, D_MODEL), lambda i: (i, 0)),
                  pl.BlockSpec((1, D_MODEL), lambda i: (0, 0)),
                  pl.BlockSpec((C_END, D_MODEL), lambda i: (0, 0))],
        out_specs=[spec for spec, _ in outs],
        out_shape=[shape for _, shape in outs],
        compiler_params=pltpu.CompilerParams(dimension_semantics=("arbitrary",), vmem_limit_bytes=VMEM_LIMIT),
        name="even_in",
    )(x2d, g, wt_pad)


DSA_Q = 128
DSA_KB = 128
DSA_WIDE = 4
DSA_NACC = 4
DSA_CELLS = 64
DSA_CELL_BITS = 14


def _f32_key(x):
    bits = lax.bitcast_convert_type(x, jnp.int32)
    return bits ^ (lax.shift_right_arithmetic(bits, 31) & 0x7FFFFFFF)


def _fold_keys(x, op):
    n, w = x.shape
    group = DSA_NACC * SUBLANES
    if n % group == 0 and n > group:
        x = op(x.reshape(n // group, DSA_NACC, SUBLANES, w), axis=0)
        return op(x, axis=0)
    return op(x.reshape(n // SUBLANES, SUBLANES, w), axis=0)


def _dsa_kernel(qat_ref, qit_ref, wit_ref, ga_ref, ka_ref, vat_ref, ki_ref, btab_ref, tri_ref, o_ref,
                slab_ref, cell_ref, s_ref, acc_ref, m_ref, l_ref, *, top_k):
    q, kb, wide = DSA_Q, DSA_KB, DSA_WIDE * DSA_KB
    i = pl.program_id(0)
    nkb = i + 1
    nwide = jnp.maximum(i - 1, 0) // DSA_WIDE
    n0 = nwide * DSA_WIDE
    t_row = i * q + lax.broadcasted_iota(jnp.int32, (1, q), 1)
    key_off = lax.broadcasted_iota(jnp.int32, (kb, 1), 0)
    npair = A_HEADS // 2

    def wide_rows(jw):
        return pl.ds(pl.multiple_of(jw * wide, wide), wide)

    def narrow_rows(j):
        return pl.ds(pl.multiple_of(j * kb, kb), kb)

    def pipelined(first, last, produce, consume):
        def body(t, carry):
            consume(t, (t - first) % 2, carry)
            return produce(jnp.minimum(t + 1, last - 1), (t + 1 - first) % 2)

        lax.fori_loop(first, last, body, produce(first, 0))

    w_head = [wit_ref[h:h + 1, :] for h in range(IDX_HEADS)]

    def scores(rows):
        k_idx = ki_ref[rows, :]
        sc = None
        for h in range(IDX_HEADS):
            s = jnp.dot(k_idx, qit_ref[h * LANES:(h + 1) * LANES, :], preferred_element_type=F32)
            term = w_head[h] * jnp.maximum(s, 0.0)
            sc = term if sc is None else sc + term
        return sc

    assert wide % (DSA_CELLS * SUBLANES) == 0
    cell_ref[...] = jnp.full(cell_ref.shape, INT_MIN, jnp.int32)

    def score_wide(jw, c):
        rows = wide_rows(jw)
        key = _f32_key(scores(rows))
        slab_ref[rows, :] = key
        cells = jnp.max(key.reshape(wide // (DSA_CELLS * SUBLANES), DSA_CELLS, SUBLANES, q), axis=0)
        cell_ref[...] = jnp.maximum(cell_ref[...], cells)
        return c

    def score_narrow(j, c):
        rows = narrow_rows(j)
        key = _f32_key(jnp.where(j * kb + key_off <= t_row, scores(rows), NEG_INF))
        slab_ref[rows, :] = key
        ncls = kb // SUBLANES
        cell_ref[0:ncls] = jnp.maximum(cell_ref[0:ncls], key.reshape(ncls, SUBLANES, q))
        return c

    lax.fori_loop(0, nwide, score_wide, 0)
    lax.fori_loop(n0, nkb, score_narrow, 0)

    def count_ge(trial):
        def cnt(blk):
            return _fold_keys(jnp.where(blk >= trial, 1.0, 0.0), jnp.sum)

        acc = lax.fori_loop(0, nwide, lambda jw, a: a + cnt(slab_ref[wide_rows(jw), :]),
                            jnp.zeros((SUBLANES, q), F32))
        acc = lax.fori_loop(n0, nkb, lambda j, a: a + cnt(slab_ref[narrow_rows(j), :]), acc)
        return jnp.sum(acc, axis=0, keepdims=True).astype(jnp.int32)

    assert DSA_CELLS * SUBLANES >= top_k
    cell_max = cell_ref[...]
    hi_key = jnp.max(jnp.max(cell_max, axis=0), axis=0, keepdims=True)

    def cell_bisect(b, c):
        trial = c + lax.shift_left(jnp.int32(1), 31 - b)
        cnt = jnp.sum(jnp.sum(jnp.where(cell_max >= trial, 1.0, 0.0), axis=0), axis=0, keepdims=True)
        return jnp.where(cnt >= top_k, trial, c)

    lo_key = lax.fori_loop(0, DSA_CELL_BITS, cell_bisect, jnp.full((1, q), INT_MIN, jnp.int32))
    same_sign = jnp.min(lo_key ^ hi_key) >= 0
    base = jnp.where(same_sign, lo_key, INT_MIN)
    span = jnp.where(same_sign, hi_key - lo_key, -1)
    nbits = jnp.where(same_sign, jnp.max(32 - lax.clz(span)), 32)

    def bisect(b, state):
        off, cnt, above = state
        trial_off = off + lax.shift_left(jnp.int32(1), nbits - 1 - b)
        in_span = (trial_off ^ INT_MIN) <= (span ^ INT_MIN)
        c = jnp.where(in_span, count_ge(base + trial_off), 0)
        ok = c >= top_k
        return jnp.where(ok, trial_off, off), jnp.where(ok, c, cnt), jnp.where(ok, above, c)

    uncounted = jnp.full((1, q), -1, jnp.int32)
    off, cnt_thr, cnt_above = lax.fori_loop(0, nbits, bisect, (jnp.zeros((1, q), jnp.int32), uncounted, uncounted))
    thr = base + off
    cnt_thr = lax.cond(jnp.min(cnt_thr) < 0, lambda: count_ge(thr), lambda: cnt_thr)
    tied = jnp.logical_and(cnt_thr > top_k, thr > NEG_INF_KEY)
    any_tie = jnp.max(jnp.where(tied, 1, 0)) > 0

    thr_valid = jnp.maximum(thr, NEG_INF_KEY + 1)

    def mask_fast():
        def body(rows):
            slab_ref[rows, :] = jnp.where(slab_ref[rows, :] >= thr_valid, 0, NEG_INF_BITS)

        lax.fori_loop(0, nwide, lambda jw, c: body(wide_rows(jw)), None)
        lax.fori_loop(n0, nkb, lambda j, c: body(narrow_rows(j)), None)

    def mask_tied():
        above = lax.cond(jnp.min(cnt_above) < 0, lambda: count_ge(thr + 1), lambda: cnt_above)
        need = (top_k - above).astype(F32)

        def body(rows, n, seen):
            key = slab_ref[rows, :]
            eq = key == thr
            eq_f = jnp.where(eq, 1.0, 0.0)
            seen_after = seen + jnp.sum(_fold_keys(eq_f, jnp.sum), axis=0, keepdims=True)
            take_all = seen_after <= need
            crossing = jnp.logical_and(jnp.logical_not(take_all), seen < need)

            def rank_inside():
                before = jnp.dot(tri_ref[0:n, 0:n], eq_f.astype(BF16), preferred_element_type=F32)
                return jnp.where(seen + before < need, 1.0, 0.0)

            picked = lax.cond(jnp.max(jnp.where(crossing, 1, 0)) > 0, rank_inside,
                              lambda: jnp.broadcast_to(jnp.where(take_all, 1.0, 0.0), (n, q)))
            sel = jnp.logical_or(key > thr, jnp.logical_and(eq, picked > 0.0))
            sel = jnp.logical_and(sel, key > NEG_INF_KEY)
            slab_ref[rows, :] = jnp.where(sel, 0, NEG_INF_BITS)
            return seen_after

        seen = lax.fori_loop(0, nwide, lambda jw, c: body(wide_rows(jw), wide, c), jnp.zeros((1, q), F32))
        lax.fori_loop(n0, nkb, lambda j, c: body(narrow_rows(j), kb, c), seen)

    lax.cond(any_tie, mask_tied, mask_fast)

    m_ref[...] = jnp.full(m_ref.shape, M_INIT, F32)
    l_ref[...] = jnp.zeros(l_ref.shape, F32)
    acc_ref[...] = jnp.zeros(acc_ref.shape, F32)

    def logits(rows, n, buf, rel):
        mask = lax.bitcast_convert_type(slab_ref[rows, :], F32)
        mask2 = jnp.concatenate([mask, mask], axis=1)
        tile_max = []
        for p in range(npair):
            ls = slice(p * LANES, (p + 1) * LANES)
            w_q = jnp.concatenate([qat_ref[(2 * p) * LANES:(2 * p + 1) * LANES, :],
                                   qat_ref[(2 * p + 1) * LANES:(2 * p + 2) * LANES, :]], axis=1)
            s = jnp.dot(ka_ref[rows, ls], w_q, preferred_element_type=F32) + mask2
            if rel is not None:
                s = s + btab_ref[rel, p]
            s_ref[buf, p, 0:n, :] = s
            tile_max.append(_fold_keys(s, jnp.max))
        return tuple(tile_max)

    def attend(rows, n, buf, tile_max):
        for p in range(npair):
            ls = slice(p * LANES, (p + 1) * LANES)
            m_old = m_ref[p]
            m_new = jnp.maximum(m_old, jnp.max(tile_max[p], axis=0, keepdims=True))
            alpha = jnp.exp2(m_old - m_new)
            pexp = jnp.exp2(s_ref[buf, p, 0:n, :] - m_new[0:1])
            m_ref[p] = m_new
            l_ref[p] = alpha * l_ref[p] + _fold_keys(pexp, jnp.sum)
            acc_ref[p] = alpha[0:1] * acc_ref[p] + jnp.dot(vat_ref[ls, rows], pexp.astype(BF16),
                                                           preferred_element_type=F32)

    @pl.when(nwide > 0)
    def _():
        pipelined(0, nwide, lambda jw, buf: logits(wide_rows(jw), wide, buf, None),
                  lambda jw, buf, tile_max: attend(wide_rows(jw), wide, buf, tile_max))

    pipelined(n0, nkb, lambda j, buf: logits(narrow_rows(j), kb, buf, jnp.minimum(i - j, 2)),
              lambda j, buf, tile_max: attend(narrow_rows(j), kb, buf, tile_max))

    for p in range(npair):
        o_t = acc_ref[p] / jnp.sum(l_ref[p], axis=0, keepdims=True)
        o_pair = jnp.concatenate([o_t[:A_HEAD_DIM, :q], o_t[A_HEAD_DIM:, q:]], axis=0).T
        ls = slice(p * LANES, (p + 1) * LANES)
        o_ref[:, ls] = (o_pair * _silu(ga_ref[:, ls])).astype(o_ref.dtype)


def _dsa_bias_table(rel_bias):
    ii = np.arange(DSA_Q)[:, None]
    jj = np.arange(DSA_KB)[None, :]
    buckets = np.stack([_t5_bucket_np(ii - jj), _t5_bucket_np(DSA_KB + ii - jj),
                        np.full((DSA_Q, DSA_KB), REL_BUCKETS - 1)])
    assert (_t5_bucket_np(np.arange(DSA_KB + 1, 4 * DSA_KB)) == REL_BUCKETS - 1).all()
    rb = rel_bias.astype(F32) * LOG2E
    rb = rb - rb[REL_BUCKETS - 1]
    onehot = np.zeros((3, DSA_KB, DSA_Q, REL_BUCKETS), np.float32)
    np.put_along_axis(onehot, np.transpose(buckets, (0, 2, 1))[..., None], 1.0, axis=-1)
    tab = jnp.einsum('tsrb,bh->thsr', jnp.asarray(onehot), rb, precision=lax.Precision.HIGHEST)
    tab = tab.reshape(3, A_HEADS // 2, 2, DSA_KB, DSA_Q)
    return jnp.transpose(tab, (0, 1, 3, 2, 4)).reshape(3, A_HEADS // 2, DSA_KB, 2 * DSA_Q)


def _dsa(qa_t, qi_t, wi_t, ga, ka, va_t, ki, btab, top_k):
    s = ga.shape[0]
    q, kb = DSA_Q, DSA_KB
    wide = DSA_WIDE * kb
    tri = jnp.asarray(np.tril(np.ones((wide, wide), np.float32), -1), BF16)
    whole = lambda shape: pl.BlockSpec(shape, lambda i: (0,) * len(shape), pipeline_mode=pl.Buffered(1))
    return pl.pallas_call(
        functools.partial(_dsa_kernel, top_k=top_k),
        grid=(s // q,),
        in_specs=[pl.BlockSpec((A_HEADS * LANES, q), lambda i: (0, i)),
                  pl.BlockSpec((IDX_HEADS * LANES, q), lambda i: (0, i)),
                  pl.BlockSpec((SUBLANES, q), lambda i: (0, i)),
                  pl.BlockSpec((q, A_W), lambda i: (i, 0)),
                  whole((s, A_W)), whole((A_W, s)), whole((s, LANES)),
                  whole((3, A_HEADS // 2, kb, 2 * q)), whole((wide, wide))],
        out_specs=pl.BlockSpec((q, A_W), lambda i: (i, 0)),
        out_shape=jax.ShapeDtypeStruct((s, A_W), BF16),
        scratch_shapes=[pltpu.VMEM((s, q), jnp.int32),
                        pltpu.VMEM((DSA_CELLS, SUBLANES, q), jnp.int32),
                        pltpu.VMEM((2, A_HEADS // 2, DSA_WIDE * kb, 2 * q), F32),
                        pltpu.VMEM((A_HEADS // 2, LANES, 2 * q), F32),
                        pltpu.VMEM((A_HEADS // 2, SUBLANES, 2 * q), F32),
                        pltpu.VMEM((A_HEADS // 2, SUBLANES, 2 * q), F32)],
        compiler_params=pltpu.CompilerParams(dimension_semantics=("arbitrary",), vmem_limit_bytes=VMEM_LIMIT),
        name="dsa",
    )(qa_t, qi_t, wi_t, ga, ka, va_t, ki, btab, tri)


GLA_BLOCK = 512


def _gla_kernel(qb_ref, kb_ref, vb_ref, gb_ref, lr_ref, wlr_ref, blr_ref, gn_ref, tril_ref, o_ref,
                state_ref, b_ref, att_ref):
    c, sub = GLA_CHUNK, GLA_SUB
    nsub = c // sub

    @pl.when(pl.program_id(0) == 0)
    def _():
        state_ref[...] = jnp.zeros(state_ref.shape, F32)

    dn_t = (((1,), (1,)), ((), ()))
    dn_0 = (((0,), (0,)), ((), ()))
    hp = lax.Precision.HIGHEST
    row_i = lax.broadcasted_iota(jnp.int32, (sub, sub), 0)
    col_j = lax.broadcasted_iota(jnp.int32, (sub, sub), 1)

    def chunk(ci, carry):
        r0 = pl.multiple_of(ci * c, c)
        z = jnp.dot(lr_ref[pl.ds(r0, c), :], wlr_ref[...], precision=hp, preferred_element_type=F32) + blr_ref[...]
        log_a = (jnp.minimum(z, 0.0) - jnp.log1p(jnp.exp(-jnp.abs(z)))) * (LOG2E / GLA_TAU)
        b_ref[...] = jnp.dot(tril_ref[...], log_a, precision=hp, preferred_element_type=F32)
        for h in range(B_HEADS):
            ls = slice(h * LANES, (h + 1) * LANES)
            b = b_ref[:, ls]
            qh = qb_ref[pl.ds(r0, c), ls]
            kh = kb_ref[pl.ds(r0, c), ls]
            vh = vb_ref[pl.ds(r0, c), ls]
            state_t = state_ref[h]
            b_last = b[c - 1:c, :]
            o = lax.dot_general((qh * jnp.exp2(b)).astype(BF16), state_t.astype(BF16), dn_t,
                                preferred_element_type=F32)
            for si in range(nsub):
                rs = slice(si * sub, (si + 1) * sub)
                b_i, q_i = b[rs], qh[rs]
                if si > 0:
                    ref_row = b[si * sub:si * sub + 1, :]
                    q_t = (q_i * jnp.exp2(b_i - ref_row)).astype(BF16)
                    k_t = (kh[:si * sub] * jnp.exp2(ref_row - b[:si * sub])).astype(BF16)
                    att_ref[rs, :si * sub] = lax.dot_general(q_t, k_t, dn_t, preferred_element_type=F32)
                k_i = kh[rs]
                diag = jnp.zeros((sub, sub), F32)
                for jl in range(sub):
                    diff = jnp.minimum(b_i - b_i[jl:jl + 1, :], 0.0)
                    col = jnp.sum(q_i * k_i[jl:jl + 1, :] * jnp.exp2(diff), axis=-1, keepdims=True)
                    diag = jnp.where(col_j == jl, col, diag)
                att_ref[rs, rs] = jnp.where(col_j <= row_i, diag, 0.0)
                if si + 1 < nsub:
                    att_ref[rs, (si + 1) * sub:] = jnp.zeros((sub, c - (si + 1) * sub), F32)
            o = o + jnp.dot(att_ref[...].astype(BF16), vh.astype(BF16), preferred_element_type=F32)
            k_dec = (kh * jnp.exp2(b_last - b)).astype(BF16)
            state_ref[h] = (jnp.exp2(b_last) * state_t
                            + lax.dot_general(vh.astype(BF16), k_dec, dn_0, preferred_element_type=F32))
            o = o * lax.rsqrt(jnp.mean(o * o, axis=-1, keepdims=True) + EPS) * gn_ref[...]
            o_ref[pl.ds(r0, c), ls] = (o * _silu(gb_ref[pl.ds(r0, c), ls])).astype(o_ref.dtype)
        return carry

    lax.fori_loop(0, GLA_BLOCK // c, chunk, 0)


def _gla(qb, kb, vb, gb, lr, wlr_pad, blr_pad, gn):
    s = qb.shape[0]
    blk = min(GLA_BLOCK, s)
    assert blk == GLA_BLOCK and s % blk == 0
    tril = jnp.asarray(np.tril(np.ones((GLA_CHUNK, GLA_CHUNK), np.float32)))
    row = lambda wd: pl.BlockSpec((blk, wd), lambda i: (i, 0))
    const = lambda shape: pl.BlockSpec(shape, lambda i: (0,) * len(shape))
    return pl.pallas_call(
        _gla_kernel,
        grid=(s // blk,),
        in_specs=[row(B_HEADS * LANES), row(B_HEADS * LANES), row(B_VW), row(B_VW), row(LANES),
                  const((LANES, B_HEADS * LANES)), const((1, B_HEADS * LANES)), const((1, B_DV)),
                  const((GLA_CHUNK, GLA_CHUNK))],
        out_specs=row(B_VW),
        out_shape=jax.ShapeDtypeStruct((s, B_VW), BF16),
        scratch_shapes=[pltpu.VMEM((B_HEADS, LANES, B_DV), F32),
                        pltpu.VMEM((GLA_CHUNK, B_HEADS * LANES), F32),
                        pltpu.VMEM((GLA_CHUNK, GLA_CHUNK), F32)],
        compiler_params=pltpu.CompilerParams(dimension_semantics=("arbitrary",), vmem_limit_bytes=VMEM_LIMIT),
        name="gla",
    )(qb, kb, vb, gb, lr, wlr_pad, blr_pad, gn, tril)


def _mid_kernel(x_ref, ma_ref, mb_ref, wo_ref, g_ref, wi_ref, x1_ref, u_ref, sg_ref):
    x1 = (x_ref[...]
          + jnp.dot(ma_ref[...], wo_ref[:A_W, :], preferred_element_type=F32)
          + jnp.dot(mb_ref[...], wo_ref[A_W:, :], preferred_element_type=F32))
    x1_ref[...] = x1
    h = (x1 * lax.rsqrt(jnp.mean(x1 * x1, axis=-1, keepdims=True) + EPS) * g_ref[...]).astype(BF16)
    val = jnp.dot(h, wi_ref[:, :D_MODEL], preferred_element_type=F32)
    glu = jnp.dot(h, wi_ref[:, D_MODEL:2 * D_MODEL], preferred_element_type=F32)
    gate = jnp.dot(h, wi_ref[:, 2 * D_MODEL:], preferred_element_type=F32)
    u_ref[...] = val * jax.nn.sigmoid(glu)
    sg_ref[...] = _silu(gate)


def _mid(x2d, mix_a, mix_b, w_out, g, w_in, tm):
    s = x2d.shape[0]
    row = lambda wd: pl.BlockSpec((tm, wd), lambda i: (i, 0))
    const = lambda shape: pl.BlockSpec(shape, lambda i: (0,) * len(shape))
    return pl.pallas_call(
        _mid_kernel,
        grid=(s // tm,),
        in_specs=[row(D_MODEL), row(A_W), row(B_VW), const((A_W + B_VW, D_MODEL)), const((1, D_MODEL)),
                  const((D_MODEL, 3 * D_MODEL))],
        out_specs=[row(D_MODEL), row(D_MODEL), row(D_MODEL)],
        out_shape=[jax.ShapeDtypeStruct((s, D_MODEL), F32)] * 3,
        compiler_params=pltpu.CompilerParams(dimension_semantics=("arbitrary",), vmem_limit_bytes=VMEM_LIMIT),
        name="mid",
    )(x2d, mix_a, mix_b, w_out, g, w_in)


HALO = 32
CONV_ROWS = 128


def _odd_out_kernel(u_ref, halo_ref, sg_ref, x1_ref, cw_ref, cb_ref, lg_ref, lb_ref, wo_ref, nf_ref, o_ref,
                    ubuf_ref, y_ref, shift_ref, *, tm):
    i = pl.program_id(0)
    ubuf_ref[HALO:, :] = u_ref[...]
    ubuf_ref[:HALO, :] = jnp.where(i > 0, halo_ref[...], 0.0)
    base = HALO - (CONV_WIDTH - 1)
    for cs in range(D_MODEL // LANES):
        ls = slice(cs * LANES, (cs + 1) * LANES)
        for r0 in range(0, tm, CONV_ROWS):
            acc = jnp.zeros((CONV_ROWS, LANES), F32)
            for res in range(SUBLANES):
                taps = [k for k in range(CONV_WIDTH) if (base + k) % SUBLANES == res]
                first = base + taps[0]
                rows = taps[-1] - taps[0] + CONV_ROWS
                shift_ref[0:rows, :] = ubuf_ref[r0 + first:r0 + first + rows, ls]
                aligned = shift_ref[0:rows, :]
                for k in taps:
                    a = base + k - first
                    acc = acc + aligned[a:a + CONV_ROWS] * cw_ref[k:k + 1, ls]
            y_ref[r0:r0 + CONV_ROWS, ls] = acc + cb_ref[:, ls]
    y = y_ref[...]
    mu = jnp.mean(y, axis=-1, keepdims=True)
    yc = y - mu
    var = jnp.mean(yc * yc, axis=-1, keepdims=True)
    yn = yc * lax.rsqrt(var + EPS) * lg_ref[...] + lb_ref[...]
    mixed = (_silu(yn) * sg_ref[...]).astype(BF16)
    x2 = x1_ref[...] + jnp.dot(mixed, wo_ref[...], preferred_element_type=F32)
    o_ref[...] = x2 * lax.rsqrt(jnp.mean(x2 * x2, axis=-1, keepdims=True) + EPS) * nf_ref[...]


def _odd_out(u, sg, x1, cw_pad, cb, lg, lb, w_out, nf, tm):
    s = u.shape[0]
    row = pl.BlockSpec((tm, D_MODEL), lambda i: (i, 0))
    const = lambda shape: pl.BlockSpec(shape, lambda i: (0,) * len(shape))
    halo = pl.BlockSpec((HALO, D_MODEL), lambda i: (jnp.maximum(i * (tm // HALO) - 1, 0), 0))
    return pl.pallas_call(
        functools.partial(_odd_out_kernel, tm=tm),
        grid=(s // tm,),
        in_specs=[row, halo, row, row, const((HALO, D_MODEL)), const((1, D_MODEL)), const((1, D_MODEL)),
                  const((1, D_MODEL)), const((D_MODEL, D_MODEL)), const((1, D_MODEL))],
        out_specs=row,
        out_shape=jax.ShapeDtypeStruct((s, D_MODEL), F32),
        scratch_shapes=[pltpu.VMEM((HALO + tm, D_MODEL), F32), pltpu.VMEM((tm, D_MODEL), F32),
                        pltpu.VMEM((HALO + CONV_ROWS, LANES), F32)],
        compiler_params=pltpu.CompilerParams(dimension_semantics=("arbitrary",), vmem_limit_bytes=VMEM_LIMIT),
        name="odd_out",
    )(u, u, sg, x1, cw_pad, cb, lg, lb, w_out, nf)


def kernel(x, rel_bias, norm_even, w_in_even, w_gla_lr, b_gla_lr, gla_norm, w_out_even, norm_odd, w_in_odd,
           conv_w, conv_b, conv_ln_g, conv_ln_b, w_out_odd, norm_f):
    bsz, s, d = x.shape
    assert bsz == 1 and d == D_MODEL and s % GLA_BLOCK == 0
    assert norm_even.shape[0] == 1 and norm_odd.shape[0] == 1
    top_k = min(TOPK_MAX, s // 4)
    tm = 256
    x2d = x.reshape(s, d)

    wt_pad = _even_in_weights(w_in_even[0])
    qa_t, ka, va_t, ga, qi_t, ki, wi_t, qb, kb, vb, gb, lr = _even_in(x2d, norm_even[0].reshape(1, d), wt_pad, tm)
    mix_a = _dsa(qa_t, qi_t, wi_t, ga, ka, va_t, ki, _dsa_bias_table(rel_bias), top_k)

    head_pad = ((0, 0), (0, 0), (0, LANES - B_DK))
    wlr_pad = jnp.pad(w_gla_lr[0].reshape(GLA_RANK, B_HEADS, B_DK), head_pad).reshape(GLA_RANK, B_HEADS * LANES)
    wlr_pad = jnp.pad(wlr_pad, ((0, LANES - GLA_RANK), (0, 0)))
    blr_pad = jnp.pad(b_gla_lr[0].reshape(1, B_HEADS, B_DK), head_pad).reshape(1, B_HEADS * LANES)
    mix_b = _gla(qb, kb, vb, gb, lr, wlr_pad, blr_pad, gla_norm[0].reshape(1, B_DV))

    x1, u, sg = _mid(x2d, mix_a, mix_b, w_out_even[0].astype(BF16), norm_odd[0].reshape(1, d),
                     w_in_odd[0].astype(BF16), tm)

    cw_pad = jnp.zeros((HALO, d), F32).at[:CONV_WIDTH].set(conv_w[0])
    out = _odd_out(u, sg, x1, cw_pad, conv_b[0].reshape(1, d), conv_ln_g[0].reshape(1, d),
                   conv_ln_b[0].reshape(1, d), w_out_odd[0].astype(BF16), norm_f.reshape(1, d), tm)
    return out.reshape(bsz, s, d)
```

```python
import functools
import math

import jax
import jax.numpy as jnp
import numpy as np
from jax import lax
from jax.experimental import pallas as pl
from jax.experimental.pallas import tpu as pltpu

D_MODEL = 1024
A_HEADS = 8
A_HEAD_DIM = 64
IDX_HEADS = 4
IDX_DIM = 64
TOPK_MAX = 256
B_HEADS = 4
B_DK = 64
B_DV = 128
GLA_RANK = 16
GLA_TAU = 16.0
GLA_CHUNK = 64
GLA_SUB = 16
CONV_WIDTH = 31
REL_BUCKETS = 32
REL_MAX_EXACT = 16
REL_MAX_DIST = 128
EPS = 1e-6

A_W = A_HEADS * A_HEAD_DIM
B_KW = B_HEADS * B_DK
B_VW = B_HEADS * B_DV

LANES = 128
SUBLANES = 8
VMEM_LIMIT = 56 * 1024 * 1024

F32 = jnp.float32
BF16 = jnp.bfloat16
NEG_INF = float("-inf")
INT_MIN = -(2 ** 31)
NEG_INF_BITS = int(np.array(-np.inf, np.float32).view(np.int32))
NEG_INF_KEY = NEG_INF_BITS ^ 0x7FFFFFFF
M_INIT = -1e30
LOG2E = math.log2(math.e)

C_QA = 0
C_KA = C_QA + A_HEADS * LANES
C_VA = C_KA + A_W
C_GA = C_VA + A_W
C_QI = C_GA + A_W
C_KI = C_QI + IDX_HEADS * LANES
C_WI = C_KI + LANES
C_QB = C_WI + LANES
C_KB = C_QB + B_HEADS * LANES
C_VB = C_KB + B_HEADS * LANES
C_GB = C_VB + B_VW
C_LR = C_GB + B_VW
C_END = C_LR + LANES


def _t5_bucket_np(dist):
    dist = np.maximum(dist, 0)
    d = np.maximum(dist, 1).astype(np.float32)
    large = REL_MAX_EXACT + (np.log(d / REL_MAX_EXACT) / math.log(REL_MAX_DIST / REL_MAX_EXACT)
                             * (REL_BUCKETS - REL_MAX_EXACT)).astype(np.int32)
    large = np.minimum(large, REL_BUCKETS - 1)
    return np.where(dist < REL_MAX_EXACT, dist, large)


def _silu(x):
    return x * jax.nn.sigmoid(x)


def _even_in_kernel(x_ref, g_ref, wt_ref, qat_ref, ka_ref, vat_ref, ga_ref, qit_ref, ki_ref, wit_ref,
                    qb_ref, kb_ref, vb_ref, gb_ref, lr_ref):
    x = x_ref[...]
    h = (x * lax.rsqrt(jnp.mean(x * x, axis=-1, keepdims=True) + EPS) * g_ref[...]).astype(BF16)
    dn_t = (((1,), (1,)), ((), ()))

    def proj(lo, hi):
        return lax.dot_general(h, wt_ref[lo:hi, :], dn_t, preferred_element_type=F32)

    def proj_t(lo, hi):
        return lax.dot_general(wt_ref[lo:hi, :], h, dn_t, preferred_element_type=F32)

    qat_ref[...] = proj_t(C_QA, C_KA).astype(BF16)
    ka_ref[...] = proj(C_KA, C_VA).astype(BF16)
    vat_ref[...] = proj_t(C_VA, C_GA).astype(BF16)
    ga_ref[...] = proj(C_GA, C_QI)
    qit_ref[...] = proj_t(C_QI, C_KI).astype(BF16)
    ki_ref[...] = proj(C_KI, C_WI).astype(BF16)
    wit_ref[...] = proj_t(C_WI, C_WI + SUBLANES)
    qb_ref[...] = proj(C_QB, C_KB)
    kb_ref[...] = proj(C_KB, C_VB)
    vb_ref[...] = proj(C_VB, C_GB)
    gb_ref[...] = proj(C_GB, C_LR)
    lr_ref[...] = proj(C_LR, C_END)


def _even_in_layout():
    cols = np.cumsum([0, A_W, A_W, A_W, A_W, IDX_HEADS * IDX_DIM, IDX_DIM, IDX_HEADS,
                      B_KW, B_KW, B_VW, B_VW, GLA_RANK])
    (s_qa, s_ka, s_va, s_ga, s_qi, s_ki, s_wi, s_qb, s_kb, s_vb, s_gb, s_lr, _) = [int(c) for c in cols]
    src = np.full((C_END,), -1, np.int32)
    scale = np.ones((C_END,), np.float32)

    def put(dst, start, width, sc=1.0):
        src[dst:dst + width] = np.arange(start, start + width)
        scale[dst:dst + width] = sc

    for h in range(A_HEADS):
        put(C_QA + h * LANES + (h % 2) * A_HEAD_DIM, s_qa + h * A_HEAD_DIM, A_HEAD_DIM, (A_HEAD_DIM ** -0.5) * LOG2E)
    put(C_KA, s_ka, A_W)
    put(C_VA, s_va, A_W)
    put(C_GA, s_ga, A_W)
    for h in range(IDX_HEADS):
        put(C_QI + h * LANES, s_qi + h * IDX_DIM, IDX_DIM)
    put(C_KI, s_ki, IDX_DIM)
    put(C_WI, s_wi, IDX_HEADS, (IDX_DIM ** -0.5) * (IDX_HEADS ** -0.5))
    for h in range(B_HEADS):
        put(C_QB + h * LANES, s_qb + h * B_DK, B_DK, B_DK ** -0.5)
        put(C_KB + h * LANES, s_kb + h * B_DK, B_DK)
    put(C_VB, s_vb, B_VW)
    put(C_GB, s_gb, B_VW)
    put(C_LR, s_lr, GLA_RANK)
    return src, scale


def _even_in_weights(w):
    src, scale = _even_in_layout()
    pieces, start = [], 0
    for end in range(1, C_END + 1):
        if end == C_END or (src[end] >= 0) != (src[start] >= 0) or (src[end] >= 0 and src[end] != src[end - 1] + 1):
            if src[start] >= 0:
                pieces.append(w[:, int(src[start]):int(src[end - 1]) + 1])
            else:
                pieces.append(jnp.zeros((D_MODEL, end - start), w.dtype))
            start = end
    w_pad = jnp.concatenate(pieces, axis=1) * jnp.asarray(scale)[None, :]
    return w_pad.T.astype(BF16)


def _even_in(x2d, g, wt_pad, tm):
    s = x2d.shape[0]
    row = lambda wd, dt: (pl.BlockSpec((tm, wd), lambda i: (i, 0)), jax.ShapeDtypeStruct((s, wd), dt))
    col = lambda ht, dt: (pl.BlockSpec((ht, tm), lambda i: (0, i)), jax.ShapeDtypeStruct((ht, s), dt))
    outs = [col(C_KA - C_QA, BF16), row(A_W, BF16), col(A_W, BF16), row(A_W, F32), col(C_KI - C_QI, BF16),
            row(LANES, BF16), col(SUBLANES, F32), row(C_KB - C_QB, F32), row(C_VB - C_KB, F32), row(B_VW, F32),
            row(B_VW, F32), row(LANES, F32)]
    return pl.pallas_call(
        _even_in_kernel,
        grid=(s // tm,),
        in_specs=[pl.BlockSpec((tm, D_MODEL), lambda i: (i, 0)),
                  pl.BlockSpec((1, D_MODEL), lambda i: (0, 0)),
                  pl.BlockSpec((C_END, D_MODEL), lambda i: (0, 0))],
        out_specs=[spec for spec, _ in outs],
        out_shape=[shape for _, shape in outs],
        compiler_params=pltpu.CompilerParams(dimension_semantics=("arbitrary",), vmem_limit_bytes=VMEM_LIMIT),
        name="even_in",
    )(x2d, g, wt_pad)


DSA_Q = 128
DSA_KB = 128
DSA_WIDE = 4
DSA_NACC = 4
DSA_CELLS = 64
DSA_CELL_BITS = 14


def _f32_key(x):
    bits = lax.bitcast_convert_type(x, jnp.int32)
    return bits ^ (lax.shift_right_arithmetic(bits, 31) & 0x7FFFFFFF)


def _fold_keys(x, op):
    n, w = x.shape
    group = DSA_NACC * SUBLANES
    if n % group == 0 and n > group:
        x = op(x.reshape(n // group, DSA_NACC, SUBLANES, w), axis=0)
        return op(x, axis=0)
    return op(x.reshape(n // SUBLANES, SUBLANES, w), axis=0)


def _dsa_kernel(qat_ref, qit_ref, wit_ref, ga_ref, ka_ref, vat_ref, ki_ref, btab_ref, tri_ref, o_ref,
                slab_ref, cell_ref, s_ref, acc_ref, m_ref, l_ref, *, top_k):
    q, kb, wide = DSA_Q, DSA_KB, DSA_WIDE * DSA_KB
    i = pl.program_id(0)
    nkb = i + 1
    nwide = jnp.maximum(i - 1, 0) // DSA_WIDE
    n0 = nwide * DSA_WIDE
    t_row = i * q + lax.broadcasted_iota(jnp.int32, (1, q), 1)
    key_off = lax.broadcasted_iota(jnp.int32, (kb, 1), 0)
    npair = A_HEADS // 2

    def wide_rows(jw):
        return pl.ds(pl.multiple_of(jw * wide, wide), wide)

    def narrow_rows(j):
        return pl.ds(pl.multiple_of(j * kb, kb), kb)

    def pipelined(first, last, produce, consume):
        def body(t, carry):
            consume(t, (t - first) % 2, carry)
            return produce(jnp.minimum(t + 1, last - 1), (t + 1 - first) % 2)

        lax.fori_loop(first, last, body, produce(first, 0))

    w_head = [wit_ref[h:h + 1, :] for h in range(IDX_HEADS)]

    def scores(rows):
        k_idx = ki_ref[rows, :]
        sc = None
        for h in range(IDX_HEADS):
            s = jnp.dot(k_idx, qit_ref[h * LANES:(h + 1) * LANES, :], preferred_element_type=F32)
            term = w_head[h] * jnp.maximum(s, 0.0)
            sc = term if sc is None else sc + term
        return sc

    assert wide % (DSA_CELLS * SUBLANES) == 0
    cell_ref[...] = jnp.full(cell_ref.shape, INT_MIN, jnp.int32)

    def score_wide(jw, c):
        rows = wide_rows(jw)
        key = _f32_key(scores(rows))
        slab_ref[rows, :] = key
        cells = jnp.max(key.reshape(wide // (DSA_CELLS * SUBLANES), DSA_CELLS, SUBLANES, q), axis=0)
        cell_ref[...] = jnp.maximum(cell_ref[...], cells)
        return c

    def score_narrow(j, c):
        rows = narrow_rows(j)
        key = _f32_key(jnp.where(j * kb + key_off <= t_row, scores(rows), NEG_INF))
        slab_ref[rows, :] = key
        ncls = kb // SUBLANES
        cell_ref[0:ncls] = jnp.maximum(cell_ref[0:ncls], key.reshape(ncls, SUBLANES, q))
        return c

    lax.fori_loop(0, nwide, score_wide, 0)
    lax.fori_loop(n0, nkb, score_narrow, 0)

    def count_ge(trial):
        def cnt(blk):
            return _fold_keys(jnp.where(blk >= trial, 1.0, 0.0), jnp.sum)

        def pair_rows(jp):
            return pl.ds(pl.multiple_of(jp * (2 * wide), 2 * wide), 2 * wide)

        acc = lax.fori_loop(0, nwide // 2, lambda jp, a: a + cnt(slab_ref[pair_rows(jp), :]),
                            jnp.zeros((SUBLANES, q), F32))
        acc = lax.fori_loop((nwide // 2) * 2, nwide, lambda jw, a: a + cnt(slab_ref[wide_rows(jw), :]), acc)
        acc = lax.fori_loop(n0, nkb, lambda j, a: a + cnt(slab_ref[narrow_rows(j), :]), acc)
        return jnp.sum(acc, axis=0, keepdims=True).astype(jnp.int32)

    assert DSA_CELLS * SUBLANES >= top_k
    cell_max = cell_ref[...]
    hi_key = jnp.max(jnp.max(cell_max, axis=0), axis=0, keepdims=True)

    def cell_bisect(b, c):
        trial = c + lax.shift_left(jnp.int32(1), 31 - b)
        cnt = jnp.sum(jnp.sum(jnp.where(cell_max >= trial, 1.0, 0.0), axis=0), axis=0, keepdims=True)
        return jnp.where(cnt >= top_k, trial, c)

    lo_key = lax.fori_loop(0, DSA_CELL_BITS, cell_bisect, jnp.full((1, q), INT_MIN, jnp.int32))
    same_sign = jnp.min(lo_key ^ hi_key) >= 0
    base = jnp.where(same_sign, lo_key, INT_MIN)
    span = jnp.where(same_sign, hi_key - lo_key, -1)
    nbits = jnp.where(same_sign, jnp.max(32 - lax.clz(span)), 32)

    def bisect(b, state):
        off, cnt, above = state
        trial_off = off + lax.shift_left(jnp.int32(1), nbits - 1 - b)
        in_span = (trial_off ^ INT_MIN) <= (span ^ INT_MIN)
        c = jnp.where(in_span, count_ge(base + trial_off), 0)
        ok = c >= top_k
        return jnp.where(ok, trial_off, off), jnp.where(ok, c, cnt), jnp.where(ok, above, c)

    uncounted = jnp.full((1, q), -1, jnp.int32)
    off, cnt_thr, cnt_above = lax.fori_loop(0, nbits, bisect, (jnp.zeros((1, q), jnp.int32), uncounted, uncounted))
    thr = base + off
    cnt_thr = lax.cond(jnp.min(cnt_thr) < 0, lambda: count_ge(thr), lambda: cnt_thr)
    tied = jnp.logical_and(cnt_thr > top_k, thr > NEG_INF_KEY)
    any_tie = jnp.max(jnp.where(tied, 1, 0)) > 0

    thr_valid = jnp.maximum(thr, NEG_INF_KEY + 1)

    def mask_fast():
        def body(rows):
            slab_ref[rows, :] = jnp.where(slab_ref[rows, :] >= thr_valid, 0, NEG_INF_BITS)

        lax.fori_loop(0, nwide, lambda jw, c: body(wide_rows(jw)), None)
        lax.fori_loop(n0, nkb, lambda j, c: body(narrow_rows(j)), None)

    def mask_tied():
        above = lax.cond(jnp.min(cnt_above) < 0, lambda: count_ge(thr + 1), lambda: cnt_above)
        need = (top_k - above).astype(F32)

        def body(rows, n, seen):
            key = slab_ref[rows, :]
            eq = key == thr
            eq_f = jnp.where(eq, 1.0, 0.0)
            seen_after = seen + jnp.sum(_fold_keys(eq_f, jnp.sum), axis=0, keepdims=True)
            take_all = seen_after <= need
            crossing = jnp.logical_and(jnp.logical_not(take_all), seen < need)

            def rank_inside():
                before = jnp.dot(tri_ref[0:n, 0:n], eq_f.astype(BF16), preferred_element_type=F32)
                return jnp.where(seen + before < need, 1.0, 0.0)

            picked = lax.cond(jnp.max(jnp.where(crossing, 1, 0)) > 0, rank_inside,
                              lambda: jnp.broadcast_to(jnp.where(take_all, 1.0, 0.0), (n, q)))
            sel = jnp.logical_or(key > thr, jnp.logical_and(eq, picked > 0.0))
            sel = jnp.logical_and(sel, key > NEG_INF_KEY)
            slab_ref[rows, :] = jnp.where(sel, 0, NEG_INF_BITS)
            return seen_after

        seen = lax.fori_loop(0, nwide, lambda jw, c: body(wide_rows(jw), wide, c), jnp.zeros((1, q), F32))
        lax.fori_loop(n0, nkb, lambda j, c: body(narrow_rows(j), kb, c), seen)

    lax.cond(any_tie, mask_tied, mask_fast)

    m_ref[...] = jnp.full(m_ref.shape, M_INIT, F32)
    l_ref[...] = jnp.zeros(l_ref.shape, F32)
    acc_ref[...] = jnp.zeros(acc_ref.shape, F32)

    def logits(rows, n, buf, rel):
        mask = lax.bitcast_convert_type(slab_ref[rows, :], F32)
        mask2 = jnp.concatenate([mask, mask], axis=1)
        tile_max = []
        for p in range(npair):
            ls = slice(p * LANES, (p + 1) * LANES)
            w_q = jnp.concatenate([qat_ref[(2 * p) * LANES:(2 * p + 1) * LANES, :],
                                   qat_ref[(2 * p + 1) * LANES:(2 * p + 2) * LANES, :]], axis=1)
            s = jnp.dot(ka_ref[rows, ls], w_q, preferred_element_type=F32) + mask2
            if rel is not None:
                s = s + btab_ref[rel, p]
            s_ref[buf, p, 0:n, :] = s
            tile_max.append(_fold_keys(s, jnp.max))
        return tuple(tile_max)

    def attend(rows, n, buf, tile_max):
        for p in range(npair):
            ls = slice(p * LANES, (p + 1) * LANES)
            m_old = m_ref[p]
            m_new = jnp.maximum(m_old, jnp.max(tile_max[p], axis=0, keepdims=True))
            alpha = jnp.exp2(m_old - m_new)
            pexp = jnp.exp2(s_ref[buf, p, 0:n, :] - m_new[0:1])
            m_ref[p] = m_new
            l_ref[p] = alpha * l_ref[p] + _fold_keys(pexp, jnp.sum)
            acc_ref[p] = alpha[0:1] * acc_ref[p] + jnp.dot(vat_ref[ls, rows], pexp.astype(BF16),
                                                           preferred_element_type=F32)

    @pl.when(nwide > 0)
    def _():
        pipelined(0, nwide, lambda jw, buf: logits(wide_rows(jw), wide, buf, None),
                  lambda jw, buf, tile_max: attend(wide_rows(jw), wide, buf, tile_max))

    pipelined(n0, nkb, lambda j, buf: logits(narrow_rows(j), kb, buf, jnp.minimum(i - j, 2)),
              lambda j, buf, tile_max: attend(narrow_rows(j), kb, buf, tile_max))

    for p in range(npair):
        o_t = acc_ref[p] / jnp.sum(l_ref[p], axis=0, keepdims=True)
        o_pair = jnp.concatenate([o_t[:A_HEAD_DIM, :q], o_t[A_HEAD_DIM:, q:]], axis=0).T
        ls = slice(p * LANES, (p + 1) * LANES)
        o_ref[:, ls] = (o_pair * _silu(ga_ref[:, ls])).astype(o_ref.dtype)


def _dsa_bias_table(rel_bias):
    ii = np.arange(DSA_Q)[:, None]
    jj = np.arange(DSA_KB)[None, :]
    buckets = np.stack([_t5_bucket_np(ii - jj), _t5_bucket_np(DSA_KB + ii - jj),
                        np.full((DSA_Q, DSA_KB), REL_BUCKETS - 1)])
    assert (_t5_bucket_np(np.arange(DSA_KB + 1, 4 * DSA_KB)) == REL_BUCKETS - 1).all()
    rb = rel_bias.astype(F32) * LOG2E
    rb = rb - rb[REL_BUCKETS - 1]
    onehot = np.zeros((3, DSA_KB, DSA_Q, REL_BUCKETS), np.float32)
    np.put_along_axis(onehot, np.transpose(buckets, (0, 2, 1))[..., None], 1.0, axis=-1)
    tab = jnp.einsum('tsrb,bh->thsr', jnp.asarray(onehot), rb, precision=lax.Precision.HIGHEST)
    tab = tab.reshape(3, A_HEADS // 2, 2, DSA_KB, DSA_Q)
    return jnp.transpose(tab, (0, 1, 3, 2, 4)).reshape(3, A_HEADS // 2, DSA_KB, 2 * DSA_Q)


def _dsa(qa_t, qi_t, wi_t, ga, ka, va_t, ki, btab, top_k):
    s = ga.shape[0]
    q, kb = DSA_Q, DSA_KB
    wide = DSA_WIDE * kb
    tri = jnp.asarray(np.tril(np.ones((wide, wide), np.float32), -1), BF16)
    whole = lambda shape: pl.BlockSpec(shape, lambda i: (0,) * len(shape), pipeline_mode=pl.Buffered(1))
    return pl.pallas_call(
        functools.partial(_dsa_kernel, top_k=top_k),
        grid=(s // q,),
        in_specs=[pl.BlockSpec((A_HEADS * LANES, q), lambda i: (0, i)),
                  pl.BlockSpec((IDX_HEADS * LANES, q), lambda i: (0, i)),
                  pl.BlockSpec((SUBLANES, q), lambda i: (0, i)),
                  pl.BlockSpec((q, A_W), lambda i: (i, 0)),
                  whole((s, A_W)), whole((A_W, s)), whole((s, LANES)),
                  whole((3, A_HEADS // 2, kb, 2 * q)), whole((wide, wide))],
        out_specs=pl.BlockSpec((q, A_W), lambda i: (i, 0)),
        out_shape=jax.ShapeDtypeStruct((s, A_W), BF16),
        scratch_shapes=[pltpu.VMEM((s, q), jnp.int32),
                        pltpu.VMEM((DSA_CELLS, SUBLANES, q), jnp.int32),
                        pltpu.VMEM((2, A_HEADS // 2, DSA_WIDE * kb, 2 * q), F32),
                        pltpu.VMEM((A_HEADS // 2, LANES, 2 * q), F32),
                        pltpu.VMEM((A_HEADS // 2, SUBLANES, 2 * q), F32),
                        pltpu.VMEM((A_HEADS // 2, SUBLANES, 2 * q), F32)],
        compiler_params=pltpu.CompilerParams(dimension_semantics=("arbitrary",), vmem_limit_bytes=VMEM_LIMIT),
        name="dsa",
    )(qa_t, qi_t, wi_t, ga, ka, va_t, ki, btab, tri)


GLA_BLOCK = 512


def _gla_kernel(qb_ref, kb_ref, vb_ref, gb_ref, lr_ref, wlr_ref, blr_ref, gn_ref, tril_ref, o_ref,
                state_ref, b_ref, att_ref):
    c, sub = GLA_CHUNK, GLA_SUB
    nsub = c // sub

    @pl.when(pl.program_id(0) == 0)
    def _():
        state_ref[...] = jnp.zeros(state_ref.shape, F32)

    dn_t = (((1,), (1,)), ((), ()))
    dn_0 = (((0,), (0,)), ((), ()))
    hp = lax.Precision.HIGHEST
    row_i = lax.broadcasted_iota(jnp.int32, (sub, sub), 0)
    col_j = lax.broadcasted_iota(jnp.int32, (sub, sub), 1)

    def chunk(ci, carry):
        r0 = pl.multiple_of(ci * c, c)
        z = jnp.dot(lr_ref[pl.ds(r0, c), :], wlr_ref[...], precision=hp, preferred_element_type=F32) + blr_ref[...]
        log_a = (jnp.minimum(z, 0.0) - jnp.log1p(jnp.exp(-jnp.abs(z)))) * (LOG2E / GLA_TAU)
        b_ref[...] = jnp.dot(tril_ref[...], log_a, precision=hp, preferred_element_type=F32)
        for h in range(B_HEADS):
            ls = slice(h * LANES, (h + 1) * LANES)
            b = b_ref[:, ls]
            qh = qb_ref[pl.ds(r0, c), ls]
            kh = kb_ref[pl.ds(r0, c), ls]
            vh = vb_ref[pl.ds(r0, c), ls]
            state_t = state_ref[h]
            b_last = b[c - 1:c, :]
            o = lax.dot_general((qh * jnp.exp2(b)).astype(BF16), state_t.astype(BF16), dn_t,
                                preferred_element_type=F32)
            for si in range(nsub):
                rs = slice(si * sub, (si + 1) * sub)
                b_i, q_i = b[rs], qh[rs]
                if si > 0:
                    ref_row = b[si * sub:si * sub + 1, :]
                    q_t = (q_i * jnp.exp2(b_i - ref_row)).astype(BF16)
                    k_t = (kh[:si * sub] * jnp.exp2(ref_row - b[:si * sub])).astype(BF16)
                    att_ref[rs, :si * sub] = lax.dot_general(q_t, k_t, dn_t, preferred_element_type=F32)
                k_i = kh[rs]
                diag = jnp.zeros((sub, sub), F32)
                for jl in range(sub):
                    diff = jnp.minimum(b_i - b_i[jl:jl + 1, :], 0.0)
                    col = jnp.sum(q_i * k_i[jl:jl + 1, :] * jnp.exp2(diff), axis=-1, keepdims=True)
                    diag = jnp.where(col_j == jl, col, diag)
                att_ref[rs, rs] = jnp.where(col_j <= row_i, diag, 0.0)
                if si + 1 < nsub:
                    att_ref[rs, (si + 1) * sub:] = jnp.zeros((sub, c - (si + 1) * sub), F32)
            o = o + jnp.dot(att_ref[...].astype(BF16), vh.astype(BF16), preferred_element_type=F32)
            k_dec = (kh * jnp.exp2(b_last - b)).astype(BF16)
            state_ref[h] = (jnp.exp2(b_last) * state_t
                            + lax.dot_general(vh.astype(BF16), k_dec, dn_0, preferred_element_type=F32))
            o = o * lax.rsqrt(jnp.mean(o * o, axis=-1, keepdims=True) + EPS) * gn_ref[...]
            o_ref[pl.ds(r0, c), ls] = (o * _silu(gb_ref[pl.ds(r0, c), ls])).astype(o_ref.dtype)
        return carry

    lax.fori_loop(0, GLA_BLOCK // c, chunk, 0)


def _gla(qb, kb, vb, gb, lr, wlr_pad, blr_pad, gn):
    s = qb.shape[0]
    blk = min(GLA_BLOCK, s)
    assert blk == GLA_BLOCK and s % blk == 0
    tril = jnp.asarray(np.tril(np.ones((GLA_CHUNK, GLA_CHUNK), np.float32)))
    row = lambda wd: pl.BlockSpec((blk, wd), lambda i: (i, 0))
    const = lambda shape: pl.BlockSpec(shape, lambda i: (0,) * len(shape))
    return pl.pallas_call(
        _gla_kernel,
        grid=(s // blk,),
        in_specs=[row(B_HEADS * LANES), row(B_HEADS * LANES), row(B_VW), row(B_VW), row(LANES),
                  const((LANES, B_HEADS * LANES)), const((1, B_HEADS * LANES)), const((1, B_DV)),
                  const((GLA_CHUNK, GLA_CHUNK))],
        out_specs=row(B_VW),
        out_shape=jax.ShapeDtypeStruct((s, B_VW), BF16),
        scratch_shapes=[pltpu.VMEM((B_HEADS, LANES, B_DV), F32),
                        pltpu.VMEM((GLA_CHUNK, B_HEADS * LANES), F32),
                        pltpu.VMEM((GLA_CHUNK, GLA_CHUNK), F32)],
        compiler_params=pltpu.CompilerParams(dimension_semantics=("arbitrary",), vmem_limit_bytes=VMEM_LIMIT),
        name="gla",
    )(qb, kb, vb, gb, lr, wlr_pad, blr_pad, gn, tril)


def _mid_kernel(x_ref, ma_ref, mb_ref, wo_ref, g_ref, wi_ref, x1_ref, u_ref, sg_ref):
    x1 = (x_ref[...]
          + jnp.dot(ma_ref[...], wo_ref[:A_W, :], preferred_element_type=F32)
          + jnp.dot(mb_ref[...], wo_ref[A_W:, :], preferred_element_type=F32))
    x1_ref[...] = x1
    h = (x1 * lax.rsqrt(jnp.mean(x1 * x1, axis=-1, keepdims=True) + EPS) * g_ref[...]).astype(BF16)
    val = jnp.dot(h, wi_ref[:, :D_MODEL], preferred_element_type=F32)
    glu = jnp.dot(h, wi_ref[:, D_MODEL:2 * D_MODEL], preferred_element_type=F32)
    gate = jnp.dot(h, wi_ref[:, 2 * D_MODEL:], preferred_element_type=F32)
    u_ref[...] = val * jax.nn.sigmoid(glu)
    sg_ref[...] = _silu(gate)


def _mid(x2d, mix_a, mix_b, w_out, g, w_in, tm):
    s = x2d.shape[0]
    row = lambda wd: pl.BlockSpec((tm, wd), lambda i: (i, 0))
    const = lambda shape: pl.BlockSpec(shape, lambda i: (0,) * len(shape))
    return pl.pallas_call(
        _mid_kernel,
        grid=(s // tm,),
        in_specs=[row(D_MODEL), row(A_W), row(B_VW), const((A_W + B_VW, D_MODEL)), const((1, D_MODEL)),
                  const((D_MODEL, 3 * D_MODEL))],
        out_specs=[row(D_MODEL), row(D_MODEL), row(D_MODEL)],
        out_shape=[jax.ShapeDtypeStruct((s, D_MODEL), F32)] * 3,
        compiler_params=pltpu.CompilerParams(dimension_semantics=("arbitrary",), vmem_limit_bytes=VMEM_LIMIT),
        name="mid",
    )(x2d, mix_a, mix_b, w_out, g, w_in)


HALO = 32
CONV_ROWS = 128


def _odd_out_kernel(u_ref, halo_ref, sg_ref, x1_ref, cw_ref, cb_ref, lg_ref, lb_ref, wo_ref, nf_ref, o_ref,
                    ubuf_ref, y_ref, shift_ref, *, tm):
    i = pl.program_id(0)
    ubuf_ref[HALO:, :] = u_ref[...]
    ubuf_ref[:HALO, :] = jnp.where(i > 0, halo_ref[...], 0.0)
    base = HALO - (CONV_WIDTH - 1)
    for cs in range(D_MODEL // LANES):
        ls = slice(cs * LANES, (cs + 1) * LANES)
        for r0 in range(0, tm, CONV_ROWS):
            acc = jnp.zeros((CONV_ROWS, LANES), F32)
            for res in range(SUBLANES):
                taps = [k for k in range(CONV_WIDTH) if (base + k) % SUBLANES == res]
                first = base + taps[0]
                rows = taps[-1] - taps[0] + CONV_ROWS
                shift_ref[0:rows, :] = ubuf_ref[r0 + first:r0 + first + rows, ls]
                aligned = shift_ref[0:rows, :]
                for k in taps:
                    a = base + k - first
                    acc = acc + aligned[a:a + CONV_ROWS] * cw_ref[k:k + 1, ls]
            y_ref[r0:r0 + CONV_ROWS, ls] = acc + cb_ref[:, ls]
    y = y_ref[...]
    mu = jnp.mean(y, axis=-1, keepdims=True)
    yc = y - mu
    var = jnp.mean(yc * yc, axis=-1, keepdims=True)
    yn = yc * lax.rsqrt(var + EPS) * lg_ref[...] + lb_ref[...]
    mixed = (_silu(yn) * sg_ref[...]).astype(BF16)
    x2 = x1_ref[...] + jnp.dot(mixed, wo_ref[...], preferred_element_type=F32)
    o_ref[...] = x2 * lax.rsqrt(jnp.mean(x2 * x2, axis=-1, keepdims=True) + EPS) * nf_ref[...]


def _odd_out(u, sg, x1, cw_pad, cb, lg, lb, w_out, nf, tm):
    s = u.shape[0]
    row = pl.BlockSpec((tm, D_MODEL), lambda i: (i, 0))
    const = lambda shape: pl.BlockSpec(shape, lambda i: (0,) * len(shape))
    halo = pl.BlockSpec((HALO, D_MODEL), lambda i: (jnp.maximum(i * (tm // HALO) - 1, 0), 0))
    return pl.pallas_call(
        functools.partial(_odd_out_kernel, tm=tm),
        grid=(s // tm,),
        in_specs=[row, halo, row, row, const((HALO, D_MODEL)), const((1, D_MODEL)), const((1, D_MODEL)),
                  const((1, D_MODEL)), const((D_MODEL, D_MODEL)), const((1, D_MODEL))],
        out_specs=row,
        out_shape=jax.ShapeDtypeStruct((s, D_MODEL), F32),
        scratch_shapes=[pltpu.VMEM((HALO + tm, D_MODEL), F32), pltpu.VMEM((tm, D_MODEL), F32),
                        pltpu.VMEM((HALO + CONV_ROWS, LANES), F32)],
        compiler_params=pltpu.CompilerParams(dimension_semantics=("arbitrary",), vmem_limit_bytes=VMEM_LIMIT),
        name="odd_out",
    )(u, u, sg, x1, cw_pad, cb, lg, lb, w_out, nf)


def kernel(x, rel_bias, norm_even, w_in_even, w_gla_lr, b_gla_lr, gla_norm, w_out_even, norm_odd, w_in_odd,
           conv_w, conv_b, conv_ln_g, conv_ln_b, w_out_odd, norm_f):
    bsz, s, d = x.shape
    assert bsz == 1 and d == D_MODEL and s % GLA_BLOCK == 0
    assert norm_even.shape[0] == 1 and norm_odd.shape[0] == 1
    top_k = min(TOPK_MAX, s // 4)
    tm = 256
    x2d = x.reshape(s, d)

    wt_pad = _even_in_weights(w_in_even[0])
    qa_t, ka, va_t, ga, qi_t, ki, wi_t, qb, kb, vb, gb, lr = _even_in(x2d, norm_even[0].reshape(1, d), wt_pad, tm)
    mix_a = _dsa(qa_t, qi_t, wi_t, ga, ka, va_t, ki, _dsa_bias_table(rel_bias), top_k)

    head_pad = ((0, 0), (0, 0), (0, LANES - B_DK))
    wlr_pad = jnp.pad(w_gla_lr[0].reshape(GLA_RANK, B_HEADS, B_DK), head_pad).reshape(GLA_RANK, B_HEADS * LANES)
    wlr_pad = jnp.pad(wlr_pad, ((0, LANES - GLA_RANK), (0, 0)))
    blr_pad = jnp.pad(b_gla_lr[0].reshape(1, B_HEADS, B_DK), head_pad).reshape(1, B_HEADS * LANES)
    mix_b = _gla(qb, kb, vb, gb, lr, wlr_pad, blr_pad, gla_norm[0].reshape(1, B_DV))

    x1, u, sg = _mid(x2d, mix_a, mix_b, w_out_even[0].astype(BF16), norm_odd[0].reshape(1, d),
                     w_in_odd[0].astype(BF16), tm)

    cw_pad = jnp.zeros((HALO, d), F32).at[:CONV_WIDTH].set(conv_w[0])
    out = _odd_out(u, sg, x1, cw_pad, conv_b[0].reshape(1, d), conv_ln_g[0].reshape(1, d),
                   conv_ln_b[0].reshape(1, d), w_out_odd[0].astype(BF16), norm_f.reshape(1, d), tm)
    return out.reshape(bsz, s, d)
```

```python
import functools
import math

import jax
import jax.numpy as jnp
import numpy as np
from jax import lax
from jax.experimental import pallas as pl
from jax.experimental.pallas import tpu as pltpu

D_MODEL = 1024
A_HEADS = 8
A_HEAD_DIM = 64
IDX_HEADS = 4
IDX_DIM = 64
TOPK_MAX = 256
B_HEADS = 4
B_DK = 64
B_DV = 128
GLA_RANK = 16
GLA_TAU = 16.0
GLA_CHUNK = 64
GLA_SUB = 16
CONV_WIDTH = 31
REL_BUCKETS = 32
REL_MAX_EXACT = 16
REL_MAX_DIST = 128
EPS = 1e-6

A_W = A_HEADS * A_HEAD_DIM
B_KW = B_HEADS * B_DK
B_VW = B_HEADS * B_DV

LANES = 128
SUBLANES = 8
VMEM_LIMIT = 56 * 1024 * 1024

F32 = jnp.float32
BF16 = jnp.bfloat16
NEG_INF = float("-inf")
INT_MIN = -(2 ** 31)
NEG_INF_BITS = int(np.array(-np.inf, np.float32).view(np.int32))
NEG_INF_KEY = NEG_INF_BITS ^ 0x7FFFFFFF
M_INIT = -1e30
LOG2E = math.log2(math.e)

C_QA = 0
C_KA = C_QA + A_HEADS * LANES
C_VA = C_KA + A_W
C_GA = C_VA + A_W
C_QI = C_GA + A_W
C_KI = C_QI + IDX_HEADS * LANES
C_WI = C_KI + LANES
C_QB = C_WI + LANES
C_KB = C_QB + B_HEADS * LANES
C_VB = C_KB + B_HEADS * LANES
C_GB = C_VB + B_VW
C_LR = C_GB + B_VW
C_END = C_LR + LANES


def _t5_bucket_np(dist):
    dist = np.maximum(dist, 0)
    d = np.maximum(dist, 1).astype(np.float32)
    large = REL_MAX_EXACT + (np.log(d / REL_MAX_EXACT) / math.log(REL_MAX_DIST / REL_MAX_EXACT)
                             * (REL_BUCKETS - REL_MAX_EXACT)).astype(np.int32)
    large = np.minimum(large, REL_BUCKETS - 1)
    return np.where(dist < REL_MAX_EXACT, dist, large)


def _silu(x):
    return x * jax.nn.sigmoid(x)


def _even_in_kernel(x_ref, g_ref, wt_ref, qat_ref, ka_ref, vat_ref, ga_ref, qit_ref, ki_ref, wit_ref,
                    qb_ref, kb_ref, vb_ref, gb_ref, lr_ref):
    x = x_ref[...]
    h = (x * lax.rsqrt(jnp.mean(x * x, axis=-1, keepdims=True) + EPS) * g_ref[...]).astype(BF16)
    dn_t = (((1,), (1,)), ((), ()))

    def proj(lo, hi):
        return lax.dot_general(h, wt_ref[lo:hi, :], dn_t, preferred_element_type=F32)

    def proj_t(lo, hi):
        return lax.dot_general(wt_ref[lo:hi, :], h, dn_t, preferred_element_type=F32)

    qat_ref[...] = proj_t(C_QA, C_KA).astype(BF16)
    ka_ref[...] = proj(C_KA, C_VA).astype(BF16)
    vat_ref[...] = proj_t(C_VA, C_GA).astype(BF16)
    ga_ref[...] = proj(C_GA, C_QI)
    qit_ref[...] = proj_t(C_QI, C_KI).astype(BF16)
    ki_ref[...] = proj(C_KI, C_WI).astype(BF16)
    wit_ref[...] = proj_t(C_WI, C_WI + SUBLANES)
    qb_ref[...] = proj(C_QB, C_KB)
    kb_ref[...] = proj(C_KB, C_VB)
    vb_ref[...] = proj(C_VB, C_GB)
    gb_ref[...] = proj(C_GB, C_LR)
    lr_ref[...] = proj(C_LR, C_END)


def _even_in_layout():
    cols = np.cumsum([0, A_W, A_W, A_W, A_W, IDX_HEADS * IDX_DIM, IDX_DIM, IDX_HEADS,
                      B_KW, B_KW, B_VW, B_VW, GLA_RANK])
    (s_qa, s_ka, s_va, s_ga, s_qi, s_ki, s_wi, s_qb, s_kb, s_vb, s_gb, s_lr, _) = [int(c) for c in cols]
    src = np.full((C_END,), -1, np.int32)
    scale = np.ones((C_END,), np.float32)

    def put(dst, start, width, sc=1.0):
        src[dst:dst + width] = np.arange(start, start + width)
        scale[dst:dst + width] = sc

    for h in range(A_HEADS):
        put(C_QA + h * LANES + (h % 2) * A_HEAD_DIM, s_qa + h * A_HEAD_DIM, A_HEAD_DIM, (A_HEAD_DIM ** -0.5) * LOG2E)
    put(C_KA, s_ka, A_W)
    put(C_VA, s_va, A_W)
    put(C_GA, s_ga, A_W)
    for h in range(IDX_HEADS):
        put(C_QI + h * LANES, s_qi + h * IDX_DIM, IDX_DIM)
    put(C_KI, s_ki, IDX_DIM)
    put(C_WI, s_wi, IDX_HEADS, (IDX_DIM ** -0.5) * (IDX_HEADS ** -0.5))
    for h in range(B_HEADS):
        put(C_QB + h * LANES, s_qb + h * B_DK, B_DK, B_DK ** -0.5)
        put(C_KB + h * LANES, s_kb + h * B_DK, B_DK)
    put(C_VB, s_vb, B_VW)
    put(C_GB, s_gb, B_VW)
    put(C_LR, s_lr, GLA_RANK)
    return src, scale


def _even_in_weights(w):
    src, scale = _even_in_layout()
    pieces, start = [], 0
    for end in range(1, C_END + 1):
        if end == C_END or (src[end] >= 0) != (src[start] >= 0) or (src[end] >= 0 and src[end] != src[end - 1] + 1):
            if src[start] >= 0:
                pieces.append(w[:, int(src[start]):int(src[end - 1]) + 1])
            else:
                pieces.append(jnp.zeros((D_MODEL, end - start), w.dtype))
            start = end
    w_pad = jnp.concatenate(pieces, axis=1) * jnp.asarray(scale)[None, :]
    return w_pad.T.astype(BF16)


def _even_in(x2d, g, wt_pad, tm):
    s = x2d.shape[0]
    row = lambda wd, dt: (pl.BlockSpec((tm, wd), lambda i: (i, 0)), jax.ShapeDtypeStruct((s, wd), dt))
    col = lambda ht, dt: (pl.BlockSpec((ht, tm), lambda i: (0, i)), jax.ShapeDtypeStruct((ht, s), dt))
    outs = [col(C_KA - C_QA, BF16), row(A_W, BF16), col(A_W, BF16), row(A_W, F32), col(C_KI - C_QI, BF16),
            row(LANES, BF16), col(SUBLANES, F32), row(C_KB - C_QB, F32), row(C_VB - C_KB, F32), row(B_VW, F32),
            row(B_VW, F32), row(LANES, F32)]
    return pl.pallas_call(
        _even_in_kernel,
        grid=(s // tm,),
        in_specs=[pl.BlockSpec((tm, D_MODEL), lambda i: (i, 0)),
                  pl.BlockSpec((1, D_MODEL), lambda i: (0, 0)),
                  pl.BlockSpec((C_END, D_MODEL), lambda i: (0, 0))],
        out_specs=[spec for spec, _ in outs],
        out_shape=[shape for _, shape in outs],
        compiler_params=pltpu.CompilerParams(dimension_semantics=("arbitrary",), vmem_limit_bytes=VMEM_LIMIT),
        name="even_in",
    )(x2d, g, wt_pad)


DSA_Q = 128
DSA_KB = 128
DSA_WIDE = 4
DSA_NACC = 4
DSA_CELLS = 64
DSA_CELL_BITS = 14


def _f32_key(x):
    bits = lax.bitcast_convert_type(x, jnp.int32)
    return bits ^ (lax.shift_right_arithmetic(bits, 31) & 0x7FFFFFFF)


def _fold_keys(x, op):
    n, w = x.shape
    group = DSA_NACC * SUBLANES
    if n % group == 0 and n > group:
        x = op(x.reshape(n // group, DSA_NACC, SUBLANES, w), axis=0)
        return op(x, axis=0)
    return op(x.reshape(n // SUBLANES, SUBLANES, w), axis=0)


def _dsa_kernel(qat_ref, qit_ref, wit_ref, ga_ref, ka_ref, vat_ref, ki_ref, btab_ref, tri_ref, o_ref,
                slab_ref, cell_ref, s_ref, acc_ref, m_ref, l_ref, *, top_k):
    q, kb, wide = DSA_Q, DSA_KB, DSA_WIDE * DSA_KB
    i = pl.program_id(0)
    nkb = i + 1
    nwide = jnp.maximum(i - 1, 0) // DSA_WIDE
    n0 = nwide * DSA_WIDE
    t_row = i * q + lax.broadcasted_iota(jnp.int32, (1, q), 1)
    key_off = lax.broadcasted_iota(jnp.int32, (kb, 1), 0)
    npair = A_HEADS // 2

    def wide_rows(jw):
        return pl.ds(pl.multiple_of(jw * wide, wide), wide)

    def narrow_rows(j):
        return pl.ds(pl.multiple_of(j * kb, kb), kb)

    def pipelined(first, last, produce, consume):
        def body(t, carry):
            consume(t, (t - first) % 2, carry)
            return produce(jnp.minimum(t + 1, last - 1), (t + 1 - first) % 2)

        lax.fori_loop(first, last, body, produce(first, 0))

    w_head = [wit_ref[h:h + 1, :] for h in range(IDX_HEADS)]

    def scores(rows):
        k_idx = ki_ref[rows, :]
        sc = None
        for h in range(IDX_HEADS):
            s = jnp.dot(k_idx, qit_ref[h * LANES:(h + 1) * LANES, :], preferred_element_type=F32)
            term = w_head[h] * jnp.maximum(s, 0.0)
            sc = term if sc is None else sc + term
        return sc

    assert wide % (DSA_CELLS * SUBLANES) == 0
    cell_ref[...] = jnp.full(cell_ref.shape, INT_MIN, jnp.int32)

    def score_wide(jw, c):
        rows = wide_rows(jw)
        key = _f32_key(scores(rows))
        slab_ref[rows, :] = key
        cells = jnp.max(key.reshape(wide // (DSA_CELLS * SUBLANES), DSA_CELLS, SUBLANES, q), axis=0)
        cell_ref[...] = jnp.maximum(cell_ref[...], cells)
        return c

    def score_narrow(j, c):
        rows = narrow_rows(j)
        key = _f32_key(jnp.where(j * kb + key_off <= t_row, scores(rows), NEG_INF))
        slab_ref[rows, :] = key
        ncls = kb // SUBLANES
        cell_ref[0:ncls] = jnp.maximum(cell_ref[0:ncls], key.reshape(ncls, SUBLANES, q))
        return c

    lax.fori_loop(0, nwide, score_wide, 0)
    lax.fori_loop(n0, nkb, score_narrow, 0)

    def count_ge(trial):
        def cnt(blk):
            return _fold_keys(jnp.where(blk >= trial, 1.0, 0.0), jnp.sum)

        def pair_rows(jp):
            return pl.ds(pl.multiple_of(jp * (2 * wide), 2 * wide), 2 * wide)

        acc = lax.fori_loop(0, nwide // 2, lambda jp, a: a + cnt(slab_ref[pair_rows(jp), :]),
                            jnp.zeros((SUBLANES, q), F32))
        acc = lax.fori_loop((nwide // 2) * 2, nwide, lambda jw, a: a + cnt(slab_ref[wide_rows(jw), :]), acc)
        acc = lax.fori_loop(n0, nkb, lambda j, a: a + cnt(slab_ref[narrow_rows(j), :]), acc)
        return jnp.sum(acc, axis=0, keepdims=True).astype(jnp.int32)

    assert DSA_CELLS * SUBLANES >= top_k
    cell_max = cell_ref[...]
    hi_key = jnp.max(jnp.max(cell_max, axis=0), axis=0, keepdims=True)

    def cell_bisect(b, c):
        trial = c + lax.shift_left(jnp.int32(1), 31 - b)
        cnt = jnp.sum(jnp.sum(jnp.where(cell_max >= trial, 1.0, 0.0), axis=0), axis=0, keepdims=True)
        return jnp.where(cnt >= top_k, trial, c)

    lo_key = lax.fori_loop(0, DSA_CELL_BITS, cell_bisect, jnp.full((1, q), INT_MIN, jnp.int32))
    base = lo_key
    span = hi_key - lo_key
    nbits = jnp.max(32 - lax.clz(span))

    def bisect(b, state):
        off, cnt, above = state
        trial_off = off + lax.shift_left(jnp.int32(1), nbits - 1 - b)
        in_span = (trial_off ^ INT_MIN) <= (span ^ INT_MIN)
        c = jnp.where(in_span, count_ge(base + trial_off), 0)
        ok = c >= top_k
        return jnp.where(ok, trial_off, off), jnp.where(ok, c, cnt), jnp.where(ok, above, c)

    uncounted = jnp.full((1, q), -1, jnp.int32)
    off, cnt_thr, cnt_above = lax.fori_loop(0, nbits, bisect, (jnp.zeros((1, q), jnp.int32), uncounted, uncounted))
    thr = base + off
    cnt_thr = lax.cond(jnp.min(cnt_thr) < 0, lambda: count_ge(thr), lambda: cnt_thr)
    tied = jnp.logical_and(cnt_thr > top_k, thr > NEG_INF_KEY)
    any_tie = jnp.max(jnp.where(tied, 1, 0)) > 0

    thr_valid = jnp.maximum(thr, NEG_INF_KEY + 1)

    def mask_fast():
        def body(rows):
            slab_ref[rows, :] = jnp.where(slab_ref[rows, :] >= thr_valid, 0, NEG_INF_BITS)

        lax.fori_loop(0, nwide, lambda jw, c: body(wide_rows(jw)), None)
        lax.fori_loop(n0, nkb, lambda j, c: body(narrow_rows(j)), None)

    def mask_tied():
        above = lax.cond(jnp.min(cnt_above) < 0, lambda: count_ge(thr + 1), lambda: cnt_above)
        thr_c = jnp.maximum(thr, NEG_INF_KEY)
        need = jnp.where(thr_c > NEG_INF_KEY, (top_k - above).astype(F32), -1.0)

        def body(rows, n, seen):
            for g in range(n // kb):
                blk = pl.ds(rows.start + g * kb, kb)
                key = slab_ref[blk, :]
                eq = key == thr_c
                eq_f = jnp.where(eq, 1.0, 0.0)
                before = jnp.dot(tri_ref[...], eq_f.astype(BF16), preferred_element_type=F32)
                tie_bits = jnp.where(seen + before < need, 0, NEG_INF_BITS)
                slab_ref[blk, :] = jnp.where(key > thr_c, 0, jnp.where(eq, tie_bits, NEG_INF_BITS))
                seen = seen + jnp.sum(_fold_keys(eq_f, jnp.sum), axis=0, keepdims=True)
            return seen

        seen = lax.fori_loop(0, nwide, lambda jw, c: body(wide_rows(jw), wide, c), jnp.zeros((1, q), F32))
        lax.fori_loop(n0, nkb, lambda j, c: body(narrow_rows(j), kb, c), seen)

    lax.cond(any_tie, mask_tied, mask_fast)

    m_ref[...] = jnp.full(m_ref.shape, M_INIT, F32)
    l_ref[...] = jnp.zeros(l_ref.shape, F32)
    acc_ref[...] = jnp.zeros(acc_ref.shape, F32)

    def logits(rows, n, buf, rel):
        mask = lax.bitcast_convert_type(slab_ref[rows, :], F32)
        mask2 = jnp.concatenate([mask, mask], axis=1)
        tile_max = []
        for p in range(npair):
            ls = slice(p * LANES, (p + 1) * LANES)
            w_q = jnp.concatenate([qat_ref[(2 * p) * LANES:(2 * p + 1) * LANES, :],
                                   qat_ref[(2 * p + 1) * LANES:(2 * p + 2) * LANES, :]], axis=1)
            s = jnp.dot(ka_ref[rows, ls], w_q, preferred_element_type=F32) + mask2
            if rel is not None:
                s = s + btab_ref[rel, p]
            s_ref[buf, p, 0:n, :] = s
            tile_max.append(_fold_keys(s, jnp.max))
        return tuple(tile_max)

    def attend(rows, n, buf, tile_max):
        for p in range(npair):
            ls = slice(p * LANES, (p + 1) * LANES)
            m_old = m_ref[p]
            m_new = jnp.maximum(m_old, jnp.max(tile_max[p], axis=0, keepdims=True))
            alpha = jnp.exp2(m_old - m_new)
            pexp = jnp.exp2(s_ref[buf, p, 0:n, :] - m_new[0:1])
            m_ref[p] = m_new
            l_ref[p] = alpha * l_ref[p] + _fold_keys(pexp, jnp.sum)
            acc_ref[p] = alpha[0:1] * acc_ref[p] + jnp.dot(vat_ref[ls, rows], pexp.astype(BF16),
                                                           preferred_element_type=F32)

    @pl.when(nwide > 0)
    def _():
        pipelined(0, nwide, lambda jw, buf: logits(wide_rows(jw), wide, buf, None),
                  lambda jw, buf, tile_max: attend(wide_rows(jw), wide, buf, tile_max))

    pipelined(n0, nkb, lambda j, buf: logits(narrow_rows(j), kb, buf, jnp.minimum(i - j, 2)),
              lambda j, buf, tile_max: attend(narrow_rows(j), kb, buf, tile_max))

    for p in range(npair):
        o_t = acc_ref[p] / jnp.sum(l_ref[p], axis=0, keepdims=True)
        o_pair = jnp.concatenate([o_t[:A_HEAD_DIM, :q], o_t[A_HEAD_DIM:, q:]], axis=0).T
        ls = slice(p * LANES, (p + 1) * LANES)
        o_ref[:, ls] = (o_pair * _silu(ga_ref[:, ls])).astype(o_ref.dtype)


def _dsa_bias_table(rel_bias):
    ii = np.arange(DSA_Q)[:, None]
    jj = np.arange(DSA_KB)[None, :]
    buckets = np.stack([_t5_bucket_np(ii - jj), _t5_bucket_np(DSA_KB + ii - jj),
                        np.full((DSA_Q, DSA_KB), REL_BUCKETS - 1)])
    assert (_t5_bucket_np(np.arange(DSA_KB + 1, 4 * DSA_KB)) == REL_BUCKETS - 1).all()
    rb = rel_bias.astype(F32) * LOG2E
    rb = rb - rb[REL_BUCKETS - 1]
    onehot = np.zeros((3, DSA_KB, DSA_Q, REL_BUCKETS), np.float32)
    np.put_along_axis(onehot, np.transpose(buckets, (0, 2, 1))[..., None], 1.0, axis=-1)
    tab = jnp.einsum('tsrb,bh->thsr', jnp.asarray(onehot), rb, precision=lax.Precision.HIGHEST)
    tab = tab.reshape(3, A_HEADS // 2, 2, DSA_KB, DSA_Q)
    return jnp.transpose(tab, (0, 1, 3, 2, 4)).reshape(3, A_HEADS // 2, DSA_KB, 2 * DSA_Q)


def _dsa(qa_t, qi_t, wi_t, ga, ka, va_t, ki, btab, top_k):
    s = ga.shape[0]
    q, kb = DSA_Q, DSA_KB
    tri = jnp.asarray(np.tril(np.ones((kb, kb), np.float32), -1), BF16)
    whole = lambda shape: pl.BlockSpec(shape, lambda i: (0,) * len(shape), pipeline_mode=pl.Buffered(1))
    return pl.pallas_call(
        functools.partial(_dsa_kernel, top_k=top_k),
        grid=(s // q,),
        in_specs=[pl.BlockSpec((A_HEADS * LANES, q), lambda i: (0, i)),
                  pl.BlockSpec((IDX_HEADS * LANES, q), lambda i: (0, i)),
                  pl.BlockSpec((SUBLANES, q), lambda i: (0, i)),
                  pl.BlockSpec((q, A_W), lambda i: (i, 0)),
                  whole((s, A_W)), whole((A_W, s)), whole((s, LANES)),
                  whole((3, A_HEADS // 2, kb, 2 * q)), whole((kb, kb))],
        out_specs=pl.BlockSpec((q, A_W), lambda i: (i, 0)),
        out_shape=jax.ShapeDtypeStruct((s, A_W), BF16),
        scratch_shapes=[pltpu.VMEM((s, q), jnp.int32),
                        pltpu.VMEM((DSA_CELLS, SUBLANES, q), jnp.int32),
                        pltpu.VMEM((2, A_HEADS // 2, DSA_WIDE * kb, 2 * q), F32),
                        pltpu.VMEM((A_HEADS // 2, LANES, 2 * q), F32),
                        pltpu.VMEM((A_HEADS // 2, SUBLANES, 2 * q), F32),
                        pltpu.VMEM((A_HEADS // 2, SUBLANES, 2 * q), F32)],
        compiler_params=pltpu.CompilerParams(dimension_semantics=("arbitrary",), vmem_limit_bytes=VMEM_LIMIT),
        name="dsa",
    )(qa_t, qi_t, wi_t, ga, ka, va_t, ki, btab, tri)


GLA_BLOCK = 512


def _gla_kernel(qb_ref, kb_ref, vb_ref, gb_ref, lr_ref, wlr_ref, blr_ref, gn_ref, tril_ref, o_ref,
                state_ref, b_ref, att_ref):
    c, sub = GLA_CHUNK, GLA_SUB
    nsub = c // sub

    @pl.when(pl.program_id(0) == 0)
    def _():
        state_ref[...] = jnp.zeros(state_ref.shape, F32)

    dn_t = (((1,), (1,)), ((), ()))
    dn_0 = (((0,), (0,)), ((), ()))
    hp = lax.Precision.HIGHEST
    row_i = lax.broadcasted_iota(jnp.int32, (sub, sub), 0)
    col_j = lax.broadcasted_iota(jnp.int32, (sub, sub), 1)

    def chunk(ci, carry):
        r0 = pl.multiple_of(ci * c, c)
        z = jnp.dot(lr_ref[pl.ds(r0, c), :], wlr_ref[...], precision=hp, preferred_element_type=F32) + blr_ref[...]
        log_a = (jnp.minimum(z, 0.0) - jnp.log1p(jnp.exp(-jnp.abs(z)))) * (LOG2E / GLA_TAU)
        b_ref[...] = jnp.dot(tril_ref[...], log_a, precision=hp, preferred_element_type=F32)
        for h in range(B_HEADS):
            ls = slice(h * LANES, (h + 1) * LANES)
            b = b_ref[:, ls]
            qh = qb_ref[pl.ds(r0, c), ls]
            kh = kb_ref[pl.ds(r0, c), ls]
            vh = vb_ref[pl.ds(r0, c), ls]
            state_t = state_ref[h]
            b_last = b[c - 1:c, :]
            o = lax.dot_general((qh * jnp.exp2(b)).astype(BF16), state_t.astype(BF16), dn_t,
                                preferred_element_type=F32)
            for si in range(nsub):
                rs = slice(si * sub, (si + 1) * sub)
                b_i, q_i = b[rs], qh[rs]
                if si > 0:
                    ref_row = b[si * sub:si * sub + 1, :]
                    q_t = (q_i * jnp.exp2(b_i - ref_row)).astype(BF16)
                    k_t = (kh[:si * sub] * jnp.exp2(ref_row - b[:si * sub])).astype(BF16)
                    att_ref[rs, :si * sub] = lax.dot_general(q_t, k_t, dn_t, preferred_element_type=F32)
                k_i = kh[rs]
                diag = jnp.zeros((sub, sub), F32)
                for jl in range(sub):
                    diff = jnp.minimum(b_i - b_i[jl:jl + 1, :], 0.0)
                    col = jnp.sum(q_i * k_i[jl:jl + 1, :] * jnp.exp2(diff), axis=-1, keepdims=True)
                    diag = jnp.where(col_j == jl, col, diag)
                att_ref[rs, rs] = jnp.where(col_j <= row_i, diag, 0.0)
                if si + 1 < nsub:
                    att_ref[rs, (si + 1) * sub:] = jnp.zeros((sub, c - (si + 1) * sub), F32)
            o = o + jnp.dot(att_ref[...].astype(BF16), vh.astype(BF16), preferred_element_type=F32)
            k_dec = (kh * jnp.exp2(b_last - b)).astype(BF16)
            state_ref[h] = (jnp.exp2(b_last) * state_t
                            + lax.dot_general(vh.astype(BF16), k_dec, dn_0, preferred_element_type=F32))
            o = o * lax.rsqrt(jnp.mean(o * o, axis=-1, keepdims=True) + EPS) * gn_ref[...]
            o_ref[pl.ds(r0, c), ls] = (o * _silu(gb_ref[pl.ds(r0, c), ls])).astype(o_ref.dtype)
        return carry

    lax.fori_loop(0, GLA_BLOCK // c, chunk, 0)


def _gla(qb, kb, vb, gb, lr, wlr_pad, blr_pad, gn):
    s = qb.shape[0]
    blk = min(GLA_BLOCK, s)
    assert blk == GLA_BLOCK and s % blk == 0
    tril = jnp.asarray(np.tril(np.ones((GLA_CHUNK, GLA_CHUNK), np.float32)))
    row = lambda wd: pl.BlockSpec((blk, wd), lambda i: (i, 0))
    const = lambda shape: pl.BlockSpec(shape, lambda i: (0,) * len(shape))
    return pl.pallas_call(
        _gla_kernel,
        grid=(s // blk,),
        in_specs=[row(B_HEADS * LANES), row(B_HEADS * LANES), row(B_VW), row(B_VW), row(LANES),
                  const((LANES, B_HEADS * LANES)), const((1, B_HEADS * LANES)), const((1, B_DV)),
                  const((GLA_CHUNK, GLA_CHUNK))],
        out_specs=row(B_VW),
        out_shape=jax.ShapeDtypeStruct((s, B_VW), BF16),
        scratch_shapes=[pltpu.VMEM((B_HEADS, LANES, B_DV), F32),
                        pltpu.VMEM((GLA_CHUNK, B_HEADS * LANES), F32),
                        pltpu.VMEM((GLA_CHUNK, GLA_CHUNK), F32)],
        compiler_params=pltpu.CompilerParams(dimension_semantics=("arbitrary",), vmem_limit_bytes=VMEM_LIMIT),
        name="gla",
    )(qb, kb, vb, gb, lr, wlr_pad, blr_pad, gn, tril)


def _mid_kernel(x_ref, ma_ref, mb_ref, wo_ref, g_ref, wi_ref, x1_ref, u_ref, sg_ref):
    x1 = (x_ref[...]
          + jnp.dot(ma_ref[...], wo_ref[:A_W, :], preferred_element_type=F32)
          + jnp.dot(mb_ref[...], wo_ref[A_W:, :], preferred_element_type=F32))
    x1_ref[...] = x1
    h = (x1 * lax.rsqrt(jnp.mean(x1 * x1, axis=-1, keepdims=True) + EPS) * g_ref[...]).astype(BF16)
    val = jnp.dot(h, wi_ref[:, :D_MODEL], preferred_element_type=F32)
    glu = jnp.dot(h, wi_ref[:, D_MODEL:2 * D_MODEL], preferred_element_type=F32)
    gate = jnp.dot(h, wi_ref[:, 2 * D_MODEL:], preferred_element_type=F32)
    u_ref[...] = val * jax.nn.sigmoid(glu)
    sg_ref[...] = _silu(gate)


def _mid(x2d, mix_a, mix_b, w_out, g, w_in, tm):
    s = x2d.shape[0]
    row = lambda wd: pl.BlockSpec((tm, wd), lambda i: (i, 0))
    const = lambda shape: pl.BlockSpec(shape, lambda i: (0,) * len(shape))
    return pl.pallas_call(
        _mid_kernel,
        grid=(s // tm,),
        in_specs=[row(D_MODEL), row(A_W), row(B_VW), const((A_W + B_VW, D_MODEL)), const((1, D_MODEL)),
                  const((D_MODEL, 3 * D_MODEL))],
        out_specs=[row(D_MODEL), row(D_MODEL), row(D_MODEL)],
        out_shape=[jax.ShapeDtypeStruct((s, D_MODEL), F32)] * 3,
        compiler_params=pltpu.CompilerParams(dimension_semantics=("arbitrary",), vmem_limit_bytes=VMEM_LIMIT),
        name="mid",
    )(x2d, mix_a, mix_b, w_out, g, w_in)


HALO = 32
CONV_ROWS = 128


def _odd_out_kernel(u_ref, halo_ref, sg_ref, x1_ref, cw_ref, cb_ref, lg_ref, lb_ref, wo_ref, nf_ref, o_ref,
                    ubuf_ref, y_ref, shift_ref, *, tm):
    i = pl.program_id(0)
    ubuf_ref[HALO:, :] = u_ref[...]
    ubuf_ref[:HALO, :] = jnp.where(i > 0, halo_ref[...], 0.0)
    base = HALO - (CONV_WIDTH - 1)
    for cs in range(D_MODEL // LANES):
        ls = slice(cs * LANES, (cs + 1) * LANES)
        for r0 in range(0, tm, CONV_ROWS):
            acc = jnp.zeros((CONV_ROWS, LANES), F32)
            for res in range(SUBLANES):
                taps = [k for k in range(CONV_WIDTH) if (base + k) % SUBLANES == res]
                first = base + taps[0]
                rows = taps[-1] - taps[0] + CONV_ROWS
                shift_ref[0:rows, :] = ubuf_ref[r0 + first:r0 + first + rows, ls]
                aligned = shift_ref[0:rows, :]
                for k in taps:
                    a = base + k - first
                    acc = acc + aligned[a:a + CONV_ROWS] * cw_ref[k:k + 1, ls]
            y_ref[r0:r0 + CONV_ROWS, ls] = acc + cb_ref[:, ls]
    y = y_ref[...]
    mu = jnp.mean(y, axis=-1, keepdims=True)
    yc = y - mu
    var = jnp.mean(yc * yc, axis=-1, keepdims=True)
    yn = yc * lax.rsqrt(var + EPS) * lg_ref[...] + lb_ref[...]
    mixed = (_silu(yn) * sg_ref[...]).astype(BF16)
    x2 = x1_ref[...] + jnp.dot(mixed, wo_ref[...], preferred_element_type=F32)
    o_ref[...] = x2 * lax.rsqrt(jnp.mean(x2 * x2, axis=-1, keepdims=True) + EPS) * nf_ref[...]


def _odd_out(u, sg, x1, cw_pad, cb, lg, lb, w_out, nf, tm):
    s = u.shape[0]
    row = pl.BlockSpec((tm, D_MODEL), lambda i: (i, 0))
    const = lambda shape: pl.BlockSpec(shape, lambda i: (0,) * len(shape))
    halo = pl.BlockSpec((HALO, D_MODEL), lambda i: (jnp.maximum(i * (tm // HALO) - 1, 0), 0))
    return pl.pallas_call(
        functools.partial(_odd_out_kernel, tm=tm),
        grid=(s // tm,),
        in_specs=[row, halo, row, row, const((HALO, D_MODEL)), const((1, D_MODEL)), const((1, D_MODEL)),
                  const((1, D_MODEL)), const((D_MODEL, D_MODEL)), const((1, D_MODEL))],
        out_specs=row,
        out_shape=jax.ShapeDtypeStruct((s, D_MODEL), F32),
        scratch_shapes=[pltpu.VMEM((HALO + tm, D_MODEL), F32), pltpu.VMEM((tm, D_MODEL), F32),
                        pltpu.VMEM((HALO + CONV_ROWS, LANES), F32)],
        compiler_params=pltpu.CompilerParams(dimension_semantics=("arbitrary",), vmem_limit_bytes=VMEM_LIMIT),
        name="odd_out",
    )(u, u, sg, x1, cw_pad, cb, lg, lb, w_out, nf)


def kernel(x, rel_bias, norm_even, w_in_even, w_gla_lr, b_gla_lr, gla_norm, w_out_even, norm_odd, w_in_odd,
           conv_w, conv_b, conv_ln_g, conv_ln_b, w_out_odd, norm_f):
    bsz, s, d = x.shape
    assert bsz == 1 and d == D_MODEL and s % GLA_BLOCK == 0
    assert norm_even.shape[0] == 1 and norm_odd.shape[0] == 1
    top_k = min(TOPK_MAX, s // 4)
    tm = 256
    x2d = x.reshape(s, d)

    wt_pad = _even_in_weights(w_in_even[0])
    qa_t, ka, va_t, ga, qi_t, ki, wi_t, qb, kb, vb, gb, lr = _even_in(x2d, norm_even[0].reshape(1, d), wt_pad, tm)
    mix_a = _dsa(qa_t, qi_t, wi_t, ga, ka, va_t, ki, _dsa_bias_table(rel_bias), top_k)

    head_pad = ((0, 0), (0, 0), (0, LANES - B_DK))
    wlr_pad = jnp.pad(w_gla_lr[0].reshape(GLA_RANK, B_HEADS, B_DK), head_pad).reshape(GLA_RANK, B_HEADS * LANES)
    wlr_pad = jnp.pad(wlr_pad, ((0, LANES - GLA_RANK), (0, 0)))
    blr_pad = jnp.pad(b_gla_lr[0].reshape(1, B_HEADS, B_DK), head_pad).reshape(1, B_HEADS * LANES)
    mix_b = _gla(qb, kb, vb, gb, lr, wlr_pad, blr_pad, gla_norm[0].reshape(1, B_DV))

    x1, u, sg = _mid(x2d, mix_a, mix_b, w_out_even[0].astype(BF16), norm_odd[0].reshape(1, d),
                     w_in_odd[0].astype(BF16), tm)

    cw_pad = jnp.zeros((HALO, d), F32).at[:CONV_WIDTH].set(conv_w[0])
    out = _odd_out(u, sg, x1, cw_pad, conv_b[0].reshape(1, d), conv_ln_g[0].reshape(1, d),
                   conv_ln_b[0].reshape(1, d), w_out_odd[0].astype(BF16), norm_f.reshape(1, d), tm)
    return out.reshape(bsz, s, d)
```

```python
import functools
import math

import jax
import jax.numpy as jnp
import numpy as np
from jax import lax
from jax.experimental import pallas as pl
from jax.experimental.pallas import tpu as pltpu

D_MODEL = 1024
A_HEADS = 8
A_HEAD_DIM = 64
IDX_HEADS = 4
IDX_DIM = 64
TOPK_MAX = 256
B_HEADS = 4
B_DK = 64
B_DV = 128
GLA_RANK = 16
GLA_TAU = 16.0
GLA_CHUNK = 64
GLA_SUB = 16
CONV_WIDTH = 31
REL_BUCKETS = 32
REL_MAX_EXACT = 16
REL_MAX_DIST = 128
EPS = 1e-6

A_W = A_HEADS * A_HEAD_DIM
B_KW = B_HEADS * B_DK
B_VW = B_HEADS * B_DV

LANES = 128
SUBLANES = 8
VMEM_LIMIT = 56 * 1024 * 1024

F32 = jnp.float32
BF16 = jnp.bfloat16
NEG_INF = float("-inf")
INT_MIN = -(2 ** 31)
NEG_INF_BITS = int(np.array(-np.inf, np.float32).view(np.int32))
NEG_INF_KEY = NEG_INF_BITS ^ 0x7FFFFFFF
M_INIT = -1e30
LOG2E = math.log2(math.e)

C_QA = 0
C_KA = C_QA + A_HEADS * LANES
C_VA = C_KA + A_W
C_GA = C_VA + A_W
C_QI = C_GA + A_W
C_KI = C_QI + IDX_HEADS * LANES
C_WI = C_KI + LANES
C_QB = C_WI + LANES
C_KB = C_QB + B_HEADS * LANES
C_VB = C_KB + B_HEADS * LANES
C_GB = C_VB + B_VW
C_LR = C_GB + B_VW
C_END = C_LR + LANES


def _t5_bucket_np(dist):
    dist = np.maximum(dist, 0)
    d = np.maximum(dist, 1).astype(np.float32)
    large = REL_MAX_EXACT + (np.log(d / REL_MAX_EXACT) / math.log(REL_MAX_DIST / REL_MAX_EXACT)
                             * (REL_BUCKETS - REL_MAX_EXACT)).astype(np.int32)
    large = np.minimum(large, REL_BUCKETS - 1)
    return np.where(dist < REL_MAX_EXACT, dist, large)


def _silu(x):
    return x * jax.nn.sigmoid(x)


def _even_in_kernel(x_ref, g_ref, wt_ref, qat_ref, ka_ref, vat_ref, ga_ref, qit_ref, ki_ref, wit_ref,
                    qb_ref, kb_ref, vb_ref, gb_ref, lr_ref):
    x = x_ref[...]
    h = (x * lax.rsqrt(jnp.mean(x * x, axis=-1, keepdims=True) + EPS) * g_ref[...]).astype(BF16)
    dn_t = (((1,), (1,)), ((), ()))

    def proj(lo, hi):
        return lax.dot_general(h, wt_ref[lo:hi, :], dn_t, preferred_element_type=F32)

    def proj_t(lo, hi):
        return lax.dot_general(wt_ref[lo:hi, :], h, dn_t, preferred_element_type=F32)

    qat_ref[...] = proj_t(C_QA, C_KA).astype(BF16)
    ka_ref[...] = proj(C_KA, C_VA).astype(BF16)
    vat_ref[...] = proj_t(C_VA, C_GA).astype(BF16)
    ga_ref[...] = proj(C_GA, C_QI)
    qit_ref[...] = proj_t(C_QI, C_KI).astype(BF16)
    ki_ref[...] = proj(C_KI, C_WI).astype(BF16)
    wit_ref[...] = proj_t(C_WI, C_WI + SUBLANES)
    qb_ref[...] = proj(C_QB, C_KB)
    kb_ref[...] = proj(C_KB, C_VB)
    vb_ref[...] = proj(C_VB, C_GB)
    gb_ref[...] = proj(C_GB, C_LR)
    lr_ref[...] = proj(C_LR, C_END)


def _even_in_layout():
    cols = np.cumsum([0, A_W, A_W, A_W, A_W, IDX_HEADS * IDX_DIM, IDX_DIM, IDX_HEADS,
                      B_KW, B_KW, B_VW, B_VW, GLA_RANK])
    (s_qa, s_ka, s_va, s_ga, s_qi, s_ki, s_wi, s_qb, s_kb, s_vb, s_gb, s_lr, _) = [int(c) for c in cols]
    src = np.full((C_END,), -1, np.int32)
    scale = np.ones((C_END,), np.float32)

    def put(dst, start, width, sc=1.0):
        src[dst:dst + width] = np.arange(start, start + width)
        scale[dst:dst + width] = sc

    for h in range(A_HEADS):
        put(C_QA + h * LANES + (h % 2) * A_HEAD_DIM, s_qa + h * A_HEAD_DIM, A_HEAD_DIM, (A_HEAD_DIM ** -0.5) * LOG2E)
    put(C_KA, s_ka, A_W)
    put(C_VA, s_va, A_W)
    put(C_GA, s_ga, A_W)
    for h in range(IDX_HEADS):
        put(C_QI + h * LANES, s_qi + h * IDX_DIM, IDX_DIM)
    put(C_KI, s_ki, IDX_DIM)
    put(C_WI, s_wi, IDX_HEADS, (IDX_DIM ** -0.5) * (IDX_HEADS ** -0.5))
    for h in range(B_HEADS):
        put(C_QB + h * LANES, s_qb + h * B_DK, B_DK, B_DK ** -0.5)
        put(C_KB + h * LANES, s_kb + h * B_DK, B_DK)
    put(C_VB, s_vb, B_VW)
    put(C_GB, s_gb, B_VW)
    put(C_LR, s_lr, GLA_RANK)
    return src, scale


def _even_in_weights(w):
    src, scale = _even_in_layout()
    pieces, start = [], 0
    for end in range(1, C_END + 1):
        if end == C_END or (src[end] >= 0) != (src[start] >= 0) or (src[end] >= 0 and src[end] != src[end - 1] + 1):
            if src[start] >= 0:
                pieces.append(w[:, int(src[start]):int(src[end - 1]) + 1])
            else:
                pieces.append(jnp.zeros((D_MODEL, end - start), w.dtype))
            start = end
    w_pad = jnp.concatenate(pieces, axis=1) * jnp.asarray(scale)[None, :]
    return w_pad.T.astype(BF16)


def _even_in(x2d, g, wt_pad, tm):
    s = x2d.shape[0]
    row = lambda wd, dt: (pl.BlockSpec((tm, wd), lambda i: (i, 0)), jax.ShapeDtypeStruct((s, wd), dt))
    col = lambda ht, dt: (pl.BlockSpec((ht, tm), lambda i: (0, i)), jax.ShapeDtypeStruct((ht, s), dt))
    outs = [col(C_KA - C_QA, BF16), row(A_W, BF16), col(A_W, BF16), row(A_W, F32), col(C_KI - C_QI, BF16),
            row(LANES, BF16), col(SUBLANES, F32), row(C_KB - C_QB, F32), row(C_VB - C_KB, F32), row(B_VW, F32),
            row(B_VW, F32), row(LANES, F32)]
    return pl.pallas_call(
        _even_in_kernel,
        grid=(s // tm,),
        in_specs=[pl.BlockSpec((tm, D_MODEL), lambda i: (i, 0)),
                  pl.BlockSpec((1, D_MODEL), lambda i: (0, 0)),
                  pl.BlockSpec((C_END, D_MODEL), lambda i: (0, 0))],
        out_specs=[spec for spec, _ in outs],
        out_shape=[shape for _, shape in outs],
        compiler_params=pltpu.CompilerParams(dimension_semantics=("arbitrary",), vmem_limit_bytes=VMEM_LIMIT),
        name="even_in",
    )(x2d, g, wt_pad)


DSA_Q = 128
DSA_KB = 128
DSA_WIDE = 4
DSA_NACC = 4
DSA_CELLS = 64
DSA_CELL_BITS = 14


def _f32_key(x):
    bits = lax.bitcast_convert_type(x, jnp.int32)
    return bits ^ (lax.shift_right_arithmetic(bits, 31) & 0x7FFFFFFF)


def _fold_keys(x, op):
    n, w = x.shape
    group = DSA_NACC * SUBLANES
    if n % group == 0 and n > group:
        x = op(x.reshape(n // group, DSA_NACC, SUBLANES, w), axis=0)
        return op(x, axis=0)
    return op(x.reshape(n // SUBLANES, SUBLANES, w), axis=0)


def _dsa_kernel(qat_ref, qit_ref, wit_ref, ga_ref, ka_ref, vat_ref, ki_ref, btab_ref, tri_ref, o_ref,
                slab_ref, cell_ref, s_ref, acc_ref, m_ref, l_ref, *, top_k):
    q, kb, wide = DSA_Q, DSA_KB, DSA_WIDE * DSA_KB
    i = pl.program_id(0)
    nkb = i + 1
    nwide = jnp.maximum(i - 1, 0) // DSA_WIDE
    n0 = nwide * DSA_WIDE
    t_row = i * q + lax.broadcasted_iota(jnp.int32, (1, q), 1)
    key_off = lax.broadcasted_iota(jnp.int32, (kb, 1), 0)
    npair = A_HEADS // 2

    def wide_rows(jw):
        return pl.ds(pl.multiple_of(jw * wide, wide), wide)

    def narrow_rows(j):
        return pl.ds(pl.multiple_of(j * kb, kb), kb)

    def pipelined(first, last, produce, consume):
        def body(t, carry):
            consume(t, (t - first) % 2, carry)
            return produce(jnp.minimum(t + 1, last - 1), (t + 1 - first) % 2)

        lax.fori_loop(first, last, body, produce(first, 0))

    w_head = [wit_ref[h:h + 1, :] for h in range(IDX_HEADS)]

    def scores(rows):
        k_idx = ki_ref[rows, :]
        sc = None
        for h in range(IDX_HEADS):
            s = jnp.dot(k_idx, qit_ref[h * LANES:(h + 1) * LANES, :], preferred_element_type=F32)
            term = w_head[h] * jnp.maximum(s, 0.0)
            sc = term if sc is None else sc + term
        return sc

    assert wide % (DSA_CELLS * SUBLANES) == 0
    cell_ref[...] = jnp.full(cell_ref.shape, INT_MIN, jnp.int32)

    def score_rows(rows, n):
        key = _f32_key(scores(rows))
        slab_ref[rows, :] = key
        cells = jnp.max(key.reshape(n // (DSA_CELLS * SUBLANES), DSA_CELLS, SUBLANES, q), axis=0)
        cell_ref[...] = jnp.maximum(cell_ref[...], cells)

    def pair_rows(jp):
        return pl.ds(pl.multiple_of(jp * (2 * wide), 2 * wide), 2 * wide)

    def score_narrow(j, c):
        rows = narrow_rows(j)
        key = _f32_key(jnp.where(j * kb + key_off <= t_row, scores(rows), NEG_INF))
        slab_ref[rows, :] = key
        ncls = kb // SUBLANES
        cell_ref[0:ncls] = jnp.maximum(cell_ref[0:ncls], key.reshape(ncls, SUBLANES, q))
        return c

    lax.fori_loop(0, nwide // 2, lambda jp, c: score_rows(pair_rows(jp), 2 * wide), None)
    lax.fori_loop((nwide // 2) * 2, nwide, lambda jw, c: score_rows(wide_rows(jw), wide), None)
    lax.fori_loop(n0, nkb, score_narrow, 0)

    def count_ge(trial):
        def cnt(blk):
            return _fold_keys(jnp.where(blk >= trial, 1.0, 0.0), jnp.sum)

        acc = lax.fori_loop(0, nwide // 2, lambda jp, a: a + cnt(slab_ref[pair_rows(jp), :]),
                            jnp.zeros((SUBLANES, q), F32))
        acc = lax.fori_loop((nwide // 2) * 2, nwide, lambda jw, a: a + cnt(slab_ref[wide_rows(jw), :]), acc)
        acc = lax.fori_loop(n0, nkb, lambda j, a: a + cnt(slab_ref[narrow_rows(j), :]), acc)
        return jnp.sum(acc, axis=0, keepdims=True).astype(jnp.int32)

    assert DSA_CELLS * SUBLANES >= top_k
    cell_max = cell_ref[...]
    hi_key = jnp.max(jnp.max(cell_max, axis=0), axis=0, keepdims=True)

    def cell_bisect(b, c):
        trial = c + lax.shift_left(jnp.int32(1), 31 - b)
        cnt = jnp.sum(jnp.sum(jnp.where(cell_max >= trial, 1.0, 0.0), axis=0), axis=0, keepdims=True)
        return jnp.where(cnt >= top_k, trial, c)

    lo_key = lax.fori_loop(0, DSA_CELL_BITS, cell_bisect, jnp.full((1, q), INT_MIN, jnp.int32))
    base = lo_key
    span = hi_key - lo_key
    nbits = jnp.max(32 - lax.clz(span))

    def bisect(b, state):
        off, cnt, above = state
        trial_off = off + lax.shift_left(jnp.int32(1), nbits - 1 - b)
        in_span = (trial_off ^ INT_MIN) <= (span ^ INT_MIN)
        c = jnp.where(in_span, count_ge(base + trial_off), 0)
        ok = c >= top_k
        return jnp.where(ok, trial_off, off), jnp.where(ok, c, cnt), jnp.where(ok, above, c)

    uncounted = jnp.full((1, q), -1, jnp.int32)
    off, cnt_thr, cnt_above = lax.fori_loop(0, nbits, bisect, (jnp.zeros((1, q), jnp.int32), uncounted, uncounted))
    thr = base + off
    cnt_thr = lax.cond(jnp.min(cnt_thr) < 0, lambda: count_ge(thr), lambda: cnt_thr)
    tied = jnp.logical_and(cnt_thr > top_k, thr > NEG_INF_KEY)
    any_tie = jnp.max(jnp.where(tied, 1, 0)) > 0

    thr_valid = jnp.maximum(thr, NEG_INF_KEY + 1)

    def mask_fast():
        def body(rows):
            slab_ref[rows, :] = jnp.where(slab_ref[rows, :] >= thr_valid, 0, NEG_INF_BITS)

        lax.fori_loop(0, nwide, lambda jw, c: body(wide_rows(jw)), None)
        lax.fori_loop(n0, nkb, lambda j, c: body(narrow_rows(j)), None)

    def mask_tied():
        above = lax.cond(jnp.min(cnt_above) < 0, lambda: count_ge(thr + 1), lambda: cnt_above)
        thr_c = jnp.maximum(thr, NEG_INF_KEY)
        need = jnp.where(thr_c > NEG_INF_KEY, (top_k - above).astype(F32), -1.0)

        def body(rows, n, seen):
            for g in range(n // kb):
                blk = pl.ds(rows.start + g * kb, kb)
                key = slab_ref[blk, :]
                eq = key == thr_c
                eq_f = jnp.where(eq, 1.0, 0.0)
                before = jnp.dot(tri_ref[...], eq_f.astype(BF16), preferred_element_type=F32)
                tie_bits = jnp.where(seen + before < need, 0, NEG_INF_BITS)
                slab_ref[blk, :] = jnp.where(key > thr_c, 0, jnp.where(eq, tie_bits, NEG_INF_BITS))
                seen = seen + jnp.sum(_fold_keys(eq_f, jnp.sum), axis=0, keepdims=True)
            return seen

        seen = lax.fori_loop(0, nwide, lambda jw, c: body(wide_rows(jw), wide, c), jnp.zeros((1, q), F32))
        lax.fori_loop(n0, nkb, lambda j, c: body(narrow_rows(j), kb, c), seen)

    lax.cond(any_tie, mask_tied, mask_fast)

    m_ref[...] = jnp.full(m_ref.shape, M_INIT, F32)
    l_ref[...] = jnp.zeros(l_ref.shape, F32)
    acc_ref[...] = jnp.zeros(acc_ref.shape, F32)

    def logits(rows, n, buf, rel):
        mask = lax.bitcast_convert_type(slab_ref[rows, :], F32)
        mask2 = jnp.concatenate([mask, mask], axis=1)
        tile_max = []
        for p in range(npair):
            ls = slice(p * LANES, (p + 1) * LANES)
            w_q = jnp.concatenate([qat_ref[(2 * p) * LANES:(2 * p + 1) * LANES, :],
                                   qat_ref[(2 * p + 1) * LANES:(2 * p + 2) * LANES, :]], axis=1)
            s = jnp.dot(ka_ref[rows, ls], w_q, preferred_element_type=F32) + mask2
            if rel is not None:
                s = s + btab_ref[rel, p]
            s_ref[buf, p, 0:n, :] = s
            tile_max.append(_fold_keys(s, jnp.max))
        return tuple(tile_max)

    def attend(rows, n, buf, tile_max):
        for p in range(npair):
            ls = slice(p * LANES, (p + 1) * LANES)
            m_old = m_ref[p]
            m_new = jnp.maximum(m_old, jnp.max(tile_max[p], axis=0, keepdims=True))
            alpha = jnp.exp2(m_old - m_new)
            pexp = jnp.exp2(s_ref[buf, p, 0:n, :] - m_new[0:1])
            m_ref[p] = m_new
            l_ref[p] = alpha * l_ref[p] + _fold_keys(pexp, jnp.sum)
            acc_ref[p] = alpha[0:1] * acc_ref[p] + jnp.dot(vat_ref[ls, rows], pexp.astype(BF16),
                                                           preferred_element_type=F32)

    @pl.when(nwide > 0)
    def _():
        pipelined(0, nwide, lambda jw, buf: logits(wide_rows(jw), wide, buf, None),
                  lambda jw, buf, tile_max: attend(wide_rows(jw), wide, buf, tile_max))

    pipelined(n0, nkb, lambda j, buf: logits(narrow_rows(j), kb, buf, jnp.minimum(i - j, 2)),
              lambda j, buf, tile_max: attend(narrow_rows(j), kb, buf, tile_max))

    for p in range(npair):
        o_t = acc_ref[p] / jnp.sum(l_ref[p], axis=0, keepdims=True)
        o_pair = jnp.concatenate([o_t[:A_HEAD_DIM, :q], o_t[A_HEAD_DIM:, q:]], axis=0).T
        ls = slice(p * LANES, (p + 1) * LANES)
        o_ref[:, ls] = (o_pair * _silu(ga_ref[:, ls])).astype(o_ref.dtype)


def _dsa_bias_table(rel_bias):
    ii = np.arange(DSA_Q)[:, None]
    jj = np.arange(DSA_KB)[None, :]
    buckets = np.stack([_t5_bucket_np(ii - jj), _t5_bucket_np(DSA_KB + ii - jj),
                        np.full((DSA_Q, DSA_KB), REL_BUCKETS - 1)])
    assert (_t5_bucket_np(np.arange(DSA_KB + 1, 4 * DSA_KB)) == REL_BUCKETS - 1).all()
    rb = rel_bias.astype(F32) * LOG2E
    rb = rb - rb[REL_BUCKETS - 1]
    onehot = np.zeros((3, DSA_KB, DSA_Q, REL_BUCKETS), np.float32)
    np.put_along_axis(onehot, np.transpose(buckets, (0, 2, 1))[..., None], 1.0, axis=-1)
    tab = jnp.einsum('tsrb,bh->thsr', jnp.asarray(onehot), rb, precision=lax.Precision.HIGHEST)
    tab = tab.reshape(3, A_HEADS // 2, 2, DSA_KB, DSA_Q)
    return jnp.transpose(tab, (0, 1, 3, 2, 4)).reshape(3, A_HEADS // 2, DSA_KB, 2 * DSA_Q)


def _dsa(qa_t, qi_t, wi_t, ga, ka, va_t, ki, btab, top_k):
    s = ga.shape[0]
    q, kb = DSA_Q, DSA_KB
    tri = jnp.asarray(np.tril(np.ones((kb, kb), np.float32), -1), BF16)
    whole = lambda shape: pl.BlockSpec(shape, lambda i: (0,) * len(shape), pipeline_mode=pl.Buffered(1))
    return pl.pallas_call(
        functools.partial(_dsa_kernel, top_k=top_k),
        grid=(s // q,),
        in_specs=[pl.BlockSpec((A_HEADS * LANES, q), lambda i: (0, i)),
                  pl.BlockSpec((IDX_HEADS * LANES, q), lambda i: (0, i)),
                  pl.BlockSpec((SUBLANES, q), lambda i: (0, i)),
                  pl.BlockSpec((q, A_W), lambda i: (i, 0)),
                  whole((s, A_W)), whole((A_W, s)), whole((s, LANES)),
                  whole((3, A_HEADS // 2, kb, 2 * q)), whole((kb, kb))],
        out_specs=pl.BlockSpec((q, A_W), lambda i: (i, 0)),
        out_shape=jax.ShapeDtypeStruct((s, A_W), BF16),
        scratch_shapes=[pltpu.VMEM((s, q), jnp.int32),
                        pltpu.VMEM((DSA_CELLS, SUBLANES, q), jnp.int32),
                        pltpu.VMEM((2, A_HEADS // 2, DSA_WIDE * kb, 2 * q), F32),
                        pltpu.VMEM((A_HEADS // 2, LANES, 2 * q), F32),
                        pltpu.VMEM((A_HEADS // 2, SUBLANES, 2 * q), F32),
                        pltpu.VMEM((A_HEADS // 2, SUBLANES, 2 * q), F32)],
        compiler_params=pltpu.CompilerParams(dimension_semantics=("arbitrary",), vmem_limit_bytes=VMEM_LIMIT),
        name="dsa",
    )(qa_t, qi_t, wi_t, ga, ka, va_t, ki, btab, tri)


GLA_BLOCK = 512


def _gla_kernel(qb_ref, kb_ref, vb_ref, gb_ref, lr_ref, wlr_ref, blr_ref, gn_ref, tril_ref, o_ref,
                state_ref, b_ref, att_ref):
    c, sub = GLA_CHUNK, GLA_SUB
    nsub = c // sub

    @pl.when(pl.program_id(0) == 0)
    def _():
        state_ref[...] = jnp.zeros(state_ref.shape, F32)

    dn_t = (((1,), (1,)), ((), ()))
    dn_0 = (((0,), (0,)), ((), ()))
    hp = lax.Precision.HIGHEST
    row_i = lax.broadcasted_iota(jnp.int32, (sub, sub), 0)
    col_j = lax.broadcasted_iota(jnp.int32, (sub, sub), 1)

    def chunk(ci, carry):
        r0 = pl.multiple_of(ci * c, c)
        z = jnp.dot(lr_ref[pl.ds(r0, c), :], wlr_ref[...], precision=hp, preferred_element_type=F32) + blr_ref[...]
        log_a = (jnp.minimum(z, 0.0) - jnp.log1p(jnp.exp(-jnp.abs(z)))) * (LOG2E / GLA_TAU)
        b_ref[...] = jnp.dot(tril_ref[...], log_a, precision=hp, preferred_element_type=F32)
        for h in range(B_HEADS):
            ls = slice(h * LANES, (h + 1) * LANES)
            b = b_ref[:, ls]
            qh = qb_ref[pl.ds(r0, c), ls]
            kh = kb_ref[pl.ds(r0, c), ls]
            vh = vb_ref[pl.ds(r0, c), ls]
            state_t = state_ref[h]
            b_last = b[c - 1:c, :]
            o = lax.dot_general((qh * jnp.exp2(b)).astype(BF16), state_t.astype(BF16), dn_t,
                                preferred_element_type=F32)
            for si in range(nsub):
                rs = slice(si * sub, (si + 1) * sub)
                b_i, q_i = b[rs], qh[rs]
                if si > 0:
                    ref_row = b[si * sub:si * sub + 1, :]
                    q_t = (q_i * jnp.exp2(b_i - ref_row)).astype(BF16)
                    k_t = (kh[:si * sub] * jnp.exp2(ref_row - b[:si * sub])).astype(BF16)
                    att_ref[rs, :si * sub] = lax.dot_general(q_t, k_t, dn_t, preferred_element_type=F32)
                k_i = kh[rs]
                diag = jnp.zeros((sub, sub), F32)
                for jl in range(sub):
                    diff = jnp.minimum(b_i - b_i[jl:jl + 1, :], 0.0)
                    col = jnp.sum(q_i * k_i[jl:jl + 1, :] * jnp.exp2(diff), axis=-1, keepdims=True)
                    diag = jnp.where(col_j == jl, col, diag)
                att_ref[rs, rs] = jnp.where(col_j <= row_i, diag, 0.0)
                if si + 1 < nsub:
                    att_ref[rs, (si + 1) * sub:] = jnp.zeros((sub, c - (si + 1) * sub), F32)
            o = o + jnp.dot(att_ref[...].astype(BF16), vh.astype(BF16), preferred_element_type=F32)
            k_dec = (kh * jnp.exp2(b_last - b)).astype(BF16)
            state_ref[h] = (jnp.exp2(b_last) * state_t
                            + lax.dot_general(vh.astype(BF16), k_dec, dn_0, preferred_element_type=F32))
            o = o * lax.rsqrt(jnp.mean(o * o, axis=-1, keepdims=True) + EPS) * gn_ref[...]
            o_ref[pl.ds(r0, c), ls] = (o * _silu(gb_ref[pl.ds(r0, c), ls])).astype(o_ref.dtype)
        return carry

    lax.fori_loop(0, GLA_BLOCK // c, chunk, 0)


def _gla(qb, kb, vb, gb, lr, wlr_pad, blr_pad, gn):
    s = qb.shape[0]
    blk = min(GLA_BLOCK, s)
    assert blk == GLA_BLOCK and s % blk == 0
    tril = jnp.asarray(np.tril(np.ones((GLA_CHUNK, GLA_CHUNK), np.float32)))
    row = lambda wd: pl.BlockSpec((blk, wd), lambda i: (i, 0))
    const = lambda shape: pl.BlockSpec(shape, lambda i: (0,) * len(shape))
    return pl.pallas_call(
        _gla_kernel,
        grid=(s // blk,),
        in_specs=[row(B_HEADS * LANES), row(B_HEADS * LANES), row(B_VW), row(B_VW), row(LANES),
                  const((LANES, B_HEADS * LANES)), const((1, B_HEADS * LANES)), const((1, B_DV)),
                  const((GLA_CHUNK, GLA_CHUNK))],
        out_specs=row(B_VW),
        out_shape=jax.ShapeDtypeStruct((s, B_VW), BF16),
        scratch_shapes=[pltpu.VMEM((B_HEADS, LANES, B_DV), F32),
                        pltpu.VMEM((GLA_CHUNK, B_HEADS * LANES), F32),
                        pltpu.VMEM((GLA_CHUNK, GLA_CHUNK), F32)],
        compiler_params=pltpu.CompilerParams(dimension_semantics=("arbitrary",), vmem_limit_bytes=VMEM_LIMIT),
        name="gla",
    )(qb, kb, vb, gb, lr, wlr_pad, blr_pad, gn, tril)


def _mid_kernel(x_ref, ma_ref, mb_ref, wo_ref, g_ref, wi_ref, x1_ref, u_ref, sg_ref):
    x1 = (x_ref[...]
          + jnp.dot(ma_ref[...], wo_ref[:A_W, :], preferred_element_type=F32)
          + jnp.dot(mb_ref[...], wo_ref[A_W:, :], preferred_element_type=F32))
    x1_ref[...] = x1
    h = (x1 * lax.rsqrt(jnp.mean(x1 * x1, axis=-1, keepdims=True) + EPS) * g_ref[...]).astype(BF16)
    val = jnp.dot(h, wi_ref[:, :D_MODEL], preferred_element_type=F32)
    glu = jnp.dot(h, wi_ref[:, D_MODEL:2 * D_MODEL], preferred_element_type=F32)
    gate = jnp.dot(h, wi_ref[:, 2 * D_MODEL:], preferred_element_type=F32)
    u_ref[...] = val * jax.nn.sigmoid(glu)
    sg_ref[...] = _silu(gate)


def _mid(x2d, mix_a, mix_b, w_out, g, w_in, tm):
    s = x2d.shape[0]
    row = lambda wd: pl.BlockSpec((tm, wd), lambda i: (i, 0))
    const = lambda shape: pl.BlockSpec(shape, lambda i: (0,) * len(shape))
    return pl.pallas_call(
        _mid_kernel,
        grid=(s // tm,),
        in_specs=[row(D_MODEL), row(A_W), row(B_VW), const((A_W + B_VW, D_MODEL)), const((1, D_MODEL)),
                  const((D_MODEL, 3 * D_MODEL))],
        out_specs=[row(D_MODEL), row(D_MODEL), row(D_MODEL)],
        out_shape=[jax.ShapeDtypeStruct((s, D_MODEL), F32)] * 3,
        compiler_params=pltpu.CompilerParams(dimension_semantics=("arbitrary",), vmem_limit_bytes=VMEM_LIMIT),
        name="mid",
    )(x2d, mix_a, mix_b, w_out, g, w_in)


HALO = 32
CONV_ROWS = 128


def _odd_out_kernel(u_ref, halo_ref, sg_ref, x1_ref, cw_ref, cb_ref, lg_ref, lb_ref, wo_ref, nf_ref, o_ref,
                    ubuf_ref, y_ref, shift_ref, *, tm):
    i = pl.program_id(0)
    ubuf_ref[HALO:, :] = u_ref[...]
    ubuf_ref[:HALO, :] = jnp.where(i > 0, halo_ref[...], 0.0)
    base = HALO - (CONV_WIDTH - 1)
    for cs in range(D_MODEL // LANES):
        ls = slice(cs * LANES, (cs + 1) * LANES)
        for r0 in range(0, tm, CONV_ROWS):
            acc = jnp.zeros((CONV_ROWS, LANES), F32)
            for res in range(SUBLANES):
                taps = [k for k in range(CONV_WIDTH) if (base + k) % SUBLANES == res]
                first = base + taps[0]
                rows = taps[-1] - taps[0] + CONV_ROWS
                shift_ref[0:rows, :] = ubuf_ref[r0 + first:r0 + first + rows, ls]
                aligned = shift_ref[0:rows, :]
                for k in taps:
                    a = base + k - first
                    acc = acc + aligned[a:a + CONV_ROWS] * cw_ref[k:k + 1, ls]
            y_ref[r0:r0 + CONV_ROWS, ls] = acc + cb_ref[:, ls]
    y = y_ref[...]
    mu = jnp.mean(y, axis=-1, keepdims=True)
    yc = y - mu
    var = jnp.mean(yc * yc, axis=-1, keepdims=True)
    yn = yc * lax.rsqrt(var + EPS) * lg_ref[...] + lb_ref[...]
    mixed = (_silu(yn) * sg_ref[...]).astype(BF16)
    x2 = x1_ref[...] + jnp.dot(mixed, wo_ref[...], preferred_element_type=F32)
    o_ref[...] = x2 * lax.rsqrt(jnp.mean(x2 * x2, axis=-1, keepdims=True) + EPS) * nf_ref[...]


def _odd_out(u, sg, x1, cw_pad, cb, lg, lb, w_out, nf, tm):
    s = u.shape[0]
    row = pl.BlockSpec((tm, D_MODEL), lambda i: (i, 0))
    const = lambda shape: pl.BlockSpec(shape, lambda i: (0,) * len(shape))
    halo = pl.BlockSpec((HALO, D_MODEL), lambda i: (jnp.maximum(i * (tm // HALO) - 1, 0), 0))
    return pl.pallas_call(
        functools.partial(_odd_out_kernel, tm=tm),
        grid=(s // tm,),
        in_specs=[row, halo, row, row, const((HALO, D_MODEL)), const((1, D_MODEL)), const((1, D_MODEL)),
                  const((1, D_MODEL)), const((D_MODEL, D_MODEL)), const((1, D_MODEL))],
        out_specs=row,
        out_shape=jax.ShapeDtypeStruct((s, D_MODEL), F32),
        scratch_shapes=[pltpu.VMEM((HALO + tm, D_MODEL), F32), pltpu.VMEM((tm, D_MODEL), F32),
                        pltpu.VMEM((HALO + CONV_ROWS, LANES), F32)],
        compiler_params=pltpu.CompilerParams(dimension_semantics=("arbitrary",), vmem_limit_bytes=VMEM_LIMIT),
        name="odd_out",
    )(u, u, sg, x1, cw_pad, cb, lg, lb, w_out, nf)


def kernel(x, rel_bias, norm_even, w_in_even, w_gla_lr, b_gla_lr, gla_norm, w_out_even, norm_odd, w_in_odd,
           conv_w, conv_b, conv_ln_g, conv_ln_b, w_out_odd, norm_f):
    bsz, s, d = x.shape
    assert bsz == 1 and d == D_MODEL and s % GLA_BLOCK == 0
    assert norm_even.shape[0] == 1 and norm_odd.shape[0] == 1
    top_k = min(TOPK_MAX, s // 4)
    tm = 256
    x2d = x.reshape(s, d)

    wt_pad = _even_in_weights(w_in_even[0])
    qa_t, ka, va_t, ga, qi_t, ki, wi_t, qb, kb, vb, gb, lr = _even_in(x2d, norm_even[0].reshape(1, d), wt_pad, tm)
    mix_a = _dsa(qa_t, qi_t, wi_t, ga, ka, va_t, ki, _dsa_bias_table(rel_bias), top_k)

    head_pad = ((0, 0), (0, 0), (0, LANES - B_DK))
    wlr_pad = jnp.pad(w_gla_lr[0].reshape(GLA_RANK, B_HEADS, B_DK), head_pad).reshape(GLA_RANK, B_HEADS * LANES)
    wlr_pad = jnp.pad(wlr_pad, ((0, LANES - GLA_RANK), (0, 0)))
    blr_pad = jnp.pad(b_gla_lr[0].reshape(1, B_HEADS, B_DK), head_pad).reshape(1, B_HEADS * LANES)
    mix_b = _gla(qb, kb, vb, gb, lr, wlr_pad, blr_pad, gla_norm[0].reshape(1, B_DV))

    x1, u, sg = _mid(x2d, mix_a, mix_b, w_out_even[0].astype(BF16), norm_odd[0].reshape(1, d),
                     w_in_odd[0].astype(BF16), tm)

    cw_pad = jnp.zeros((HALO, d), F32).at[:CONV_WIDTH].set(conv_w[0])
    out = _odd_out(u, sg, x1, cw_pad, conv_b[0].reshape(1, d), conv_ln_g[0].reshape(1, d),
                   conv_ln_b[0].reshape(1, d), w_out_odd[0].astype(BF16), norm_f.reshape(1, d), tm)
    return out.reshape(bsz, s, d)
```

```python
import functools
import math

import jax
import jax.numpy as jnp
import numpy as np
from jax import lax
from jax.experimental import pallas as pl
from jax.experimental.pallas import tpu as pltpu

D_MODEL = 1024
A_HEADS = 8
A_HEAD_DIM = 64
IDX_HEADS = 4
IDX_DIM = 64
TOPK_MAX = 256
B_HEADS = 4
B_DK = 64
B_DV = 128
GLA_RANK = 16
GLA_TAU = 16.0
GLA_CHUNK = 64
GLA_SUB = 16
CONV_WIDTH = 31
REL_BUCKETS = 32
REL_MAX_EXACT = 16
REL_MAX_DIST = 128
EPS = 1e-6

A_W = A_HEADS * A_HEAD_DIM
B_KW = B_HEADS * B_DK
B_VW = B_HEADS * B_DV

LANES = 128
SUBLANES = 8
VMEM_LIMIT = 56 * 1024 * 1024

F32 = jnp.float32
BF16 = jnp.bfloat16
NEG_INF = float("-inf")
INT_MIN = -(2 ** 31)
NEG_INF_BITS = int(np.array(-np.inf, np.float32).view(np.int32))
NEG_INF_KEY = NEG_INF_BITS ^ 0x7FFFFFFF
M_INIT = -1e30
LOG2E = math.log2(math.e)

C_QA = 0
C_KA = C_QA + A_HEADS * LANES
C_VA = C_KA + A_W
C_GA = C_VA + A_W
C_QI = C_GA + A_W
C_KI = C_QI + IDX_HEADS * LANES
C_WI = C_KI + LANES
C_QB = C_WI + LANES
C_KB = C_QB + B_HEADS * LANES
C_VB = C_KB + B_HEADS * LANES
C_GB = C_VB + B_VW
C_LR = C_GB + B_VW
C_END = C_LR + LANES


def _t5_bucket_np(dist):
    dist = np.maximum(dist, 0)
    d = np.maximum(dist, 1).astype(np.float32)
    large = REL_MAX_EXACT + (np.log(d / REL_MAX_EXACT) / math.log(REL_MAX_DIST / REL_MAX_EXACT)
                             * (REL_BUCKETS - REL_MAX_EXACT)).astype(np.int32)
    large = np.minimum(large, REL_BUCKETS - 1)
    return np.where(dist < REL_MAX_EXACT, dist, large)


def _silu(x):
    return x * jax.nn.sigmoid(x)


def _even_in_kernel(x_ref, g_ref, wt_ref, qat_ref, ka_ref, vat_ref, ga_ref, qit_ref, ki_ref, wit_ref,
                    qb_ref, kb_ref, vb_ref, gb_ref, lr_ref):
    x = x_ref[...]
    h = (x * lax.rsqrt(jnp.mean(x * x, axis=-1, keepdims=True) + EPS) * g_ref[...]).astype(BF16)
    dn_t = (((1,), (1,)), ((), ()))

    def proj(lo, hi):
        return lax.dot_general(h, wt_ref[lo:hi, :], dn_t, preferred_element_type=F32)

    def proj_t(lo, hi):
        return lax.dot_general(wt_ref[lo:hi, :], h, dn_t, preferred_element_type=F32)

    qat_ref[...] = proj_t(C_QA, C_KA).astype(BF16)
    ka_ref[...] = proj(C_KA, C_VA).astype(BF16)
    vat_ref[...] = proj_t(C_VA, C_GA).astype(BF16)
    ga_ref[...] = proj(C_GA, C_QI)
    qit_ref[...] = proj_t(C_QI, C_KI).astype(BF16)
    ki_ref[...] = proj(C_KI, C_WI).astype(BF16)
    wit_ref[...] = proj_t(C_WI, C_WI + SUBLANES)
    qb_ref[...] = proj(C_QB, C_KB)
    kb_ref[...] = proj(C_KB, C_VB)
    vb_ref[...] = proj(C_VB, C_GB)
    gb_ref[...] = proj(C_GB, C_LR)
    lr_ref[...] = proj(C_LR, C_END)


def _even_in_layout():
    cols = np.cumsum([0, A_W, A_W, A_W, A_W, IDX_HEADS * IDX_DIM, IDX_DIM, IDX_HEADS,
                      B_KW, B_KW, B_VW, B_VW, GLA_RANK])
    (s_qa, s_ka, s_va, s_ga, s_qi, s_ki, s_wi, s_qb, s_kb, s_vb, s_gb, s_lr, _) = [int(c) for c in cols]
    src = np.full((C_END,), -1, np.int32)
    scale = np.ones((C_END,), np.float32)

    def put(dst, start, width, sc=1.0):
        src[dst:dst + width] = np.arange(start, start + width)
        scale[dst:dst + width] = sc

    for h in range(A_HEADS):
        put(C_QA + h * LANES + (h % 2) * A_HEAD_DIM, s_qa + h * A_HEAD_DIM, A_HEAD_DIM, (A_HEAD_DIM ** -0.5) * LOG2E)
    put(C_KA, s_ka, A_W)
    put(C_VA, s_va, A_W)
    put(C_GA, s_ga, A_W)
    for h in range(IDX_HEADS):
        put(C_QI + h * LANES, s_qi + h * IDX_DIM, IDX_DIM)
    put(C_KI, s_ki, IDX_DIM)
    put(C_WI, s_wi, IDX_HEADS, (IDX_DIM ** -0.5) * (IDX_HEADS ** -0.5))
    for h in range(B_HEADS):
        put(C_QB + h * LANES, s_qb + h * B_DK, B_DK, B_DK ** -0.5)
        put(C_KB + h * LANES, s_kb + h * B_DK, B_DK)
    put(C_VB, s_vb, B_VW)
    put(C_GB, s_gb, B_VW)
    put(C_LR, s_lr, GLA_RANK)
    return src, scale


def _even_in_weights(w):
    src, scale = _even_in_layout()
    pieces, start = [], 0
    for end in range(1, C_END + 1):
        if end == C_END or (src[end] >= 0) != (src[start] >= 0) or (src[end] >= 0 and src[end] != src[end - 1] + 1):
            if src[start] >= 0:
                pieces.append(w[:, int(src[start]):int(src[end - 1]) + 1])
            else:
                pieces.append(jnp.zeros((D_MODEL, end - start), w.dtype))
            start = end
    w_pad = jnp.concatenate(pieces, axis=1) * jnp.asarray(scale)[None, :]
    return w_pad.T.astype(BF16)


def _even_in(x2d, g, wt_pad, tm):
    s = x2d.shape[0]
    row = lambda wd, dt: (pl.BlockSpec((tm, wd), lambda i: (i, 0)), jax.ShapeDtypeStruct((s, wd), dt))
    col = lambda ht, dt: (pl.BlockSpec((ht, tm), lambda i: (0, i)), jax.ShapeDtypeStruct((ht, s), dt))
    outs = [col(C_KA - C_QA, BF16), row(A_W, BF16), col(A_W, BF16), row(A_W, F32), col(C_KI - C_QI, BF16),
            row(LANES, BF16), col(SUBLANES, F32), row(C_KB - C_QB, F32), row(C_VB - C_KB, F32), row(B_VW, F32),
            row(B_VW, F32), row(LANES, F32)]
    return pl.pallas_call(
        _even_in_kernel,
        grid=(s // tm,),
        in_specs=[pl.BlockSpec((tm, D_MODEL), lambda i: (i, 0)),
                  pl.BlockSpec((1, D_MODEL), lambda i: (0, 0)),
                  pl.BlockSpec((C_END, D_MODEL), lambda i: (0, 0))],
        out_specs=[spec for spec, _ in outs],
        out_shape=[shape for _, shape in outs],
        compiler_params=pltpu.CompilerParams(dimension_semantics=("arbitrary",), vmem_limit_bytes=VMEM_LIMIT),
        name="even_in",
    )(x2d, g, wt_pad)


DSA_Q = 128
DSA_KB = 128
DSA_WIDE = 4
DSA_NACC = 4
DSA_CELLS = 64
DSA_CELL_BITS = 14


def _f32_key(x):
    bits = lax.bitcast_convert_type(x, jnp.int32)
    return bits ^ (lax.shift_right_arithmetic(bits, 31) & 0x7FFFFFFF)


def _fold_keys(x, op):
    n, w = x.shape
    group = DSA_NACC * SUBLANES
    if n % group == 0 and n > group:
        x = op(x.reshape(n // group, DSA_NACC, SUBLANES, w), axis=0)
        return op(x, axis=0)
    return op(x.reshape(n // SUBLANES, SUBLANES, w), axis=0)


def _dsa_kernel(qat_ref, qit_ref, wit_ref, ga_ref, ka_ref, vat_ref, ki_ref, btab_ref, tri_ref, o_ref,
                slab_ref, cell_ref, s_ref, acc_ref, m_ref, l_ref, *, top_k):
    q, kb, wide = DSA_Q, DSA_KB, DSA_WIDE * DSA_KB
    i = pl.program_id(0)
    nkb = i + 1
    nwide = jnp.maximum(i - 1, 0) // DSA_WIDE
    n0 = nwide * DSA_WIDE
    t_row = i * q + lax.broadcasted_iota(jnp.int32, (1, q), 1)
    key_off = lax.broadcasted_iota(jnp.int32, (kb, 1), 0)
    npair = A_HEADS // 2

    def wide_rows(jw):
        return pl.ds(pl.multiple_of(jw * wide, wide), wide)

    def narrow_rows(j):
        return pl.ds(pl.multiple_of(j * kb, kb), kb)

    def pipelined(first, last, produce, consume):
        def body(t, carry):
            consume(t, (t - first) % 2, carry)
            return produce(jnp.minimum(t + 1, last - 1), (t + 1 - first) % 2)

        lax.fori_loop(first, last, body, produce(first, 0))

    w_head = [wit_ref[h:h + 1, :] for h in range(IDX_HEADS)]

    def scores(rows):
        k_idx = ki_ref[rows, :]
        sc = None
        for h in range(IDX_HEADS):
            s = jnp.dot(k_idx, qit_ref[h * LANES:(h + 1) * LANES, :], preferred_element_type=F32)
            term = w_head[h] * jnp.maximum(s, 0.0)
            sc = term if sc is None else sc + term
        return sc

    assert wide % (DSA_CELLS * SUBLANES) == 0
    cell_ref[...] = jnp.full(cell_ref.shape, INT_MIN, jnp.int32)

    def score_rows(rows, n):
        key = _f32_key(scores(rows))
        slab_ref[rows, :] = key
        cells = jnp.max(key.reshape(n // (DSA_CELLS * SUBLANES), DSA_CELLS, SUBLANES, q), axis=0)
        cell_ref[...] = jnp.maximum(cell_ref[...], cells)

    def pair_rows(jp):
        return pl.ds(pl.multiple_of(jp * (2 * wide), 2 * wide), 2 * wide)

    def score_narrow(j, c):
        rows = narrow_rows(j)
        key = _f32_key(jnp.where(j * kb + key_off <= t_row, scores(rows), NEG_INF))
        slab_ref[rows, :] = key
        ncls = kb // SUBLANES
        cell_ref[0:ncls] = jnp.maximum(cell_ref[0:ncls], key.reshape(ncls, SUBLANES, q))
        return c

    lax.fori_loop(0, nwide // 2, lambda jp, c: score_rows(pair_rows(jp), 2 * wide), None)
    lax.fori_loop((nwide // 2) * 2, nwide, lambda jw, c: score_rows(wide_rows(jw), wide), None)
    lax.fori_loop(n0, nkb, score_narrow, 0)

    def count_ge(trial):
        def cnt(blk):
            return _fold_keys(jnp.where(blk >= trial, 1.0, 0.0), jnp.sum)

        acc = lax.fori_loop(0, nwide // 2, lambda jp, a: a + cnt(slab_ref[pair_rows(jp), :]),
                            jnp.zeros((SUBLANES, q), F32))
        acc = lax.fori_loop((nwide // 2) * 2, nwide, lambda jw, a: a + cnt(slab_ref[wide_rows(jw), :]), acc)
        acc = lax.fori_loop(n0, nkb, lambda j, a: a + cnt(slab_ref[narrow_rows(j), :]), acc)
        return jnp.sum(acc, axis=0, keepdims=True).astype(jnp.int32)

    assert DSA_CELLS * SUBLANES >= top_k
    cell_max = cell_ref[...]
    hi_key = jnp.max(jnp.max(cell_max, axis=0), axis=0, keepdims=True)

    def cell_bisect(b, c):
        trial = c + lax.shift_left(jnp.int32(1), 31 - b)
        cnt = jnp.sum(jnp.sum(jnp.where(cell_max >= trial, 1.0, 0.0), axis=0), axis=0, keepdims=True)
        return jnp.where(cnt >= top_k, trial, c)

    lo_key = lax.fori_loop(0, DSA_CELL_BITS, cell_bisect, jnp.full((1, q), INT_MIN, jnp.int32))
    base = lo_key
    span = hi_key - lo_key
    nbits = jnp.max(32 - lax.clz(span))

    def bisect(b, state):
        off, cnt, above = state
        trial_off = off + lax.shift_left(jnp.int32(1), nbits - 1 - b)
        in_span = (trial_off ^ INT_MIN) <= (span ^ INT_MIN)
        c = jnp.where(in_span, count_ge(base + trial_off), 0)
        ok = c >= top_k
        return jnp.where(ok, trial_off, off), jnp.where(ok, c, cnt), jnp.where(ok, above, c)

    uncounted = jnp.full((1, q), -1, jnp.int32)
    off, cnt_thr, cnt_above = lax.fori_loop(0, nbits, bisect, (jnp.zeros((1, q), jnp.int32), uncounted, uncounted))
    thr = base + off
    cnt_thr = lax.cond(jnp.min(cnt_thr) < 0, lambda: count_ge(thr), lambda: cnt_thr)
    tied = jnp.logical_and(cnt_thr > top_k, thr > NEG_INF_KEY)
    any_tie = jnp.max(jnp.where(tied, 1, 0)) > 0

    thr_valid = jnp.maximum(thr, NEG_INF_KEY + 1)

    def mask_fast():
        def body(rows):
            slab_ref[rows, :] = jnp.where(slab_ref[rows, :] >= thr_valid, 0, NEG_INF_BITS)

        lax.fori_loop(0, nwide, lambda jw, c: body(wide_rows(jw)), None)
        lax.fori_loop(n0, nkb, lambda j, c: body(narrow_rows(j)), None)

    def mask_tied():
        above = lax.cond(jnp.min(cnt_above) < 0, lambda: count_ge(thr + 1), lambda: cnt_above)
        thr_c = jnp.maximum(thr, NEG_INF_KEY)
        need = jnp.where(thr_c > NEG_INF_KEY, (top_k - above).astype(F32), -1.0)

        def body(rows, n, seen):
            for g in range(n // kb):
                blk = pl.ds(rows.start + g * kb, kb)
                key = slab_ref[blk, :]
                eq = key == thr_c
                eq_f = jnp.where(eq, 1.0, 0.0)
                before = jnp.dot(tri_ref[...], eq_f.astype(BF16), preferred_element_type=F32)
                tie_bits = jnp.where(seen + before < need, 0, NEG_INF_BITS)
                slab_ref[blk, :] = jnp.where(key > thr_c, 0, jnp.where(eq, tie_bits, NEG_INF_BITS))
                seen = seen + jnp.sum(_fold_keys(eq_f, jnp.sum), axis=0, keepdims=True)
            return seen

        seen = lax.fori_loop(0, nwide // 2, lambda jp, c: body(pair_rows(jp), 2 * wide, c), jnp.zeros((1, q), F32))
        seen = lax.fori_loop((nwide // 2) * 2, nwide, lambda jw, c: body(wide_rows(jw), wide, c), seen)
        lax.fori_loop(n0, nkb, lambda j, c: body(narrow_rows(j), kb, c), seen)

    lax.cond(any_tie, mask_tied, mask_fast)

    m_ref[...] = jnp.full(m_ref.shape, M_INIT, F32)
    l_ref[...] = jnp.zeros(l_ref.shape, F32)
    acc_ref[...] = jnp.zeros(acc_ref.shape, F32)

    def logits(rows, n, buf, rel):
        mask = lax.bitcast_convert_type(slab_ref[rows, :], F32)
        mask2 = jnp.concatenate([mask, mask], axis=1)
        tile_max = []
        for p in range(npair):
            ls = slice(p * LANES, (p + 1) * LANES)
            w_q = jnp.concatenate([qat_ref[(2 * p) * LANES:(2 * p + 1) * LANES, :],
                                   qat_ref[(2 * p + 1) * LANES:(2 * p + 2) * LANES, :]], axis=1)
            s = jnp.dot(ka_ref[rows, ls], w_q, preferred_element_type=F32) + mask2
            if rel is not None:
                s = s + btab_ref[rel, p]
            s_ref[buf, p, 0:n, :] = s
            tile_max.append(_fold_keys(s, jnp.max))
        return tuple(tile_max)

    def attend(rows, n, buf, tile_max):
        for p in range(npair):
            ls = slice(p * LANES, (p + 1) * LANES)
            m_old = m_ref[p]
            m_new = jnp.maximum(m_old, jnp.max(tile_max[p], axis=0, keepdims=True))
            alpha = jnp.exp2(m_old - m_new)
            pexp = jnp.exp2(s_ref[buf, p, 0:n, :] - m_new[0:1])
            m_ref[p] = m_new
            l_ref[p] = alpha * l_ref[p] + _fold_keys(pexp, jnp.sum)
            acc_ref[p] = alpha[0:1] * acc_ref[p] + jnp.dot(vat_ref[ls, rows], pexp.astype(BF16),
                                                           preferred_element_type=F32)

    @pl.when(nwide > 0)
    def _():
        pipelined(0, nwide, lambda jw, buf: logits(wide_rows(jw), wide, buf, None),
                  lambda jw, buf, tile_max: attend(wide_rows(jw), wide, buf, tile_max))

    pipelined(n0, nkb, lambda j, buf: logits(narrow_rows(j), kb, buf, jnp.minimum(i - j, 2)),
              lambda j, buf, tile_max: attend(narrow_rows(j), kb, buf, tile_max))

    for p in range(npair):
        o_t = acc_ref[p] / jnp.sum(l_ref[p], axis=0, keepdims=True)
        o_pair = jnp.concatenate([o_t[:A_HEAD_DIM, :q], o_t[A_HEAD_DIM:, q:]], axis=0).T
        ls = slice(p * LANES, (p + 1) * LANES)
        o_ref[:, ls] = (o_pair * _silu(ga_ref[:, ls])).astype(o_ref.dtype)


def _dsa_bias_table(rel_bias):
    ii = np.arange(DSA_Q)[:, None]
    jj = np.arange(DSA_KB)[None, :]
    buckets = np.stack([_t5_bucket_np(ii - jj), _t5_bucket_np(DSA_KB + ii - jj),
                        np.full((DSA_Q, DSA_KB), REL_BUCKETS - 1)])
    assert (_t5_bucket_np(np.arange(DSA_KB + 1, 4 * DSA_KB)) == REL_BUCKETS - 1).all()
    rb = rel_bias.astype(F32) * LOG2E
    rb = rb - rb[REL_BUCKETS - 1]
    onehot = np.zeros((3, DSA_KB, DSA_Q, REL_BUCKETS), np.float32)
    np.put_along_axis(onehot, np.transpose(buckets, (0, 2, 1))[..., None], 1.0, axis=-1)
    tab = jnp.einsum('tsrb,bh->thsr', jnp.asarray(onehot), rb, precision=lax.Precision.HIGHEST)
    tab = tab.reshape(3, A_HEADS // 2, 2, DSA_KB, DSA_Q)
    return jnp.transpose(tab, (0, 1, 3, 2, 4)).reshape(3, A_HEADS // 2, DSA_KB, 2 * DSA_Q)


def _dsa(qa_t, qi_t, wi_t, ga, ka, va_t, ki, btab, top_k):
    s = ga.shape[0]
    q, kb = DSA_Q, DSA_KB
    tri = jnp.asarray(np.tril(np.ones((kb, kb), np.float32), -1), BF16)
    whole = lambda shape: pl.BlockSpec(shape, lambda i: (0,) * len(shape), pipeline_mode=pl.Buffered(1))
    return pl.pallas_call(
        functools.partial(_dsa_kernel, top_k=top_k),
        grid=(s // q,),
        in_specs=[pl.BlockSpec((A_HEADS * LANES, q), lambda i: (0, i)),
                  pl.BlockSpec((IDX_HEADS * LANES, q), lambda i: (0, i)),
                  pl.BlockSpec((SUBLANES, q), lambda i: (0, i)),
                  pl.BlockSpec((q, A_W), lambda i: (i, 0)),
                  whole((s, A_W)), whole((A_W, s)), whole((s, LANES)),
                  whole((3, A_HEADS // 2, kb, 2 * q)), whole((kb, kb))],
        out_specs=pl.BlockSpec((q, A_W), lambda i: (i, 0)),
        out_shape=jax.ShapeDtypeStruct((s, A_W), BF16),
        scratch_shapes=[pltpu.VMEM((s, q), jnp.int32),
                        pltpu.VMEM((DSA_CELLS, SUBLANES, q), jnp.int32),
                        pltpu.VMEM((2, A_HEADS // 2, DSA_WIDE * kb, 2 * q), F32),
                        pltpu.VMEM((A_HEADS // 2, LANES, 2 * q), F32),
                        pltpu.VMEM((A_HEADS // 2, SUBLANES, 2 * q), F32),
                        pltpu.VMEM((A_HEADS // 2, SUBLANES, 2 * q), F32)],
        compiler_params=pltpu.CompilerParams(dimension_semantics=("arbitrary",), vmem_limit_bytes=VMEM_LIMIT),
        name="dsa",
    )(qa_t, qi_t, wi_t, ga, ka, va_t, ki, btab, tri)


GLA_BLOCK = 512


def _gla_kernel(qb_ref, kb_ref, vb_ref, gb_ref, lr_ref, wlr_ref, blr_ref, gn_ref, tril_ref, o_ref,
                state_ref, b_ref, att_ref):
    c, sub = GLA_CHUNK, GLA_SUB
    nsub = c // sub

    @pl.when(pl.program_id(0) == 0)
    def _():
        state_ref[...] = jnp.zeros(state_ref.shape, F32)

    dn_t = (((1,), (1,)), ((), ()))
    dn_0 = (((0,), (0,)), ((), ()))
    hp = lax.Precision.HIGHEST
    row_i = lax.broadcasted_iota(jnp.int32, (sub, sub), 0)
    col_j = lax.broadcasted_iota(jnp.int32, (sub, sub), 1)

    def chunk(ci, carry):
        r0 = pl.multiple_of(ci * c, c)
        z = jnp.dot(lr_ref[pl.ds(r0, c), :], wlr_ref[...], precision=hp, preferred_element_type=F32) + blr_ref[...]
        log_a = (jnp.minimum(z, 0.0) - jnp.log1p(jnp.exp(-jnp.abs(z)))) * (LOG2E / GLA_TAU)
        b_ref[...] = jnp.dot(tril_ref[...], log_a, precision=hp, preferred_element_type=F32)
        for h in range(B_HEADS):
            ls = slice(h * LANES, (h + 1) * LANES)
            b = b_ref[:, ls]
            qh = qb_ref[pl.ds(r0, c), ls]
            kh = kb_ref[pl.ds(r0, c), ls]
            vh = vb_ref[pl.ds(r0, c), ls]
            state_t = state_ref[h]
            b_last = b[c - 1:c, :]
            o = lax.dot_general((qh * jnp.exp2(b)).astype(BF16), state_t.astype(BF16), dn_t,
                                preferred_element_type=F32)
            for si in range(nsub):
                rs = slice(si * sub, (si + 1) * sub)
                b_i, q_i = b[rs], qh[rs]
                if si > 0:
                    ref_row = b[si * sub:si * sub + 1, :]
                    q_t = (q_i * jnp.exp2(b_i - ref_row)).astype(BF16)
                    k_t = (kh[:si * sub] * jnp.exp2(ref_row - b[:si * sub])).astype(BF16)
                    att_ref[rs, :si * sub] = lax.dot_general(q_t, k_t, dn_t, preferred_element_type=F32)
                k_i = kh[rs]
                diag = jnp.zeros((sub, sub), F32)
                for jl in range(sub):
                    diff = jnp.minimum(b_i - b_i[jl:jl + 1, :], 0.0)
                    col = jnp.sum(q_i * k_i[jl:jl + 1, :] * jnp.exp2(diff), axis=-1, keepdims=True)
                    diag = jnp.where(col_j == jl, col, diag)
                att_ref[rs, rs] = jnp.where(col_j <= row_i, diag, 0.0)
                if si + 1 < nsub:
                    att_ref[rs, (si + 1) * sub:] = jnp.zeros((sub, c - (si + 1) * sub), F32)
            o = o + jnp.dot(att_ref[...].astype(BF16), vh.astype(BF16), preferred_element_type=F32)
            k_dec = (kh * jnp.exp2(b_last - b)).astype(BF16)
            state_ref[h] = (jnp.exp2(b_last) * state_t
                            + lax.dot_general(vh.astype(BF16), k_dec, dn_0, preferred_element_type=F32))
            o = o * lax.rsqrt(jnp.mean(o * o, axis=-1, keepdims=True) + EPS) * gn_ref[...]
            o_ref[pl.ds(r0, c), ls] = (o * _silu(gb_ref[pl.ds(r0, c), ls])).astype(o_ref.dtype)
        return carry

    lax.fori_loop(0, GLA_BLOCK // c, chunk, 0)


def _gla(qb, kb, vb, gb, lr, wlr_pad, blr_pad, gn):
    s = qb.shape[0]
    blk = min(GLA_BLOCK, s)
    assert blk == GLA_BLOCK and s % blk == 0
    tril = jnp.asarray(np.tril(np.ones((GLA_CHUNK, GLA_CHUNK), np.float32)))
    row = lambda wd: pl.BlockSpec((blk, wd), lambda i: (i, 0))
    const = lambda shape: pl.BlockSpec(shape, lambda i: (0,) * len(shape))
    return pl.pallas_call(
        _gla_kernel,
        grid=(s // blk,),
        in_specs=[row(B_HEADS * LANES), row(B_HEADS * LANES), row(B_VW), row(B_VW), row(LANES),
                  const((LANES, B_HEADS * LANES)), const((1, B_HEADS * LANES)), const((1, B_DV)),
                  const((GLA_CHUNK, GLA_CHUNK))],
        out_specs=row(B_VW),
        out_shape=jax.ShapeDtypeStruct((s, B_VW), BF16),
        scratch_shapes=[pltpu.VMEM((B_HEADS, LANES, B_DV), F32),
                        pltpu.VMEM((GLA_CHUNK, B_HEADS * LANES), F32),
                        pltpu.VMEM((GLA_CHUNK, GLA_CHUNK), F32)],
        compiler_params=pltpu.CompilerParams(dimension_semantics=("arbitrary",), vmem_limit_bytes=VMEM_LIMIT),
        name="gla",
    )(qb, kb, vb, gb, lr, wlr_pad, blr_pad, gn, tril)


def _mid_kernel(x_ref, ma_ref, mb_ref, wo_ref, g_ref, wi_ref, x1_ref, u_ref, sg_ref):
    x1 = (x_ref[...]
          + jnp.dot(ma_ref[...], wo_ref[:A_W, :], preferred_element_type=F32)
          + jnp.dot(mb_ref[...], wo_ref[A_W:, :], preferred_element_type=F32))
    x1_ref[...] = x1
    h = (x1 * lax.rsqrt(jnp.mean(x1 * x1, axis=-1, keepdims=True) + EPS) * g_ref[...]).astype(BF16)
    val = jnp.dot(h, wi_ref[:, :D_MODEL], preferred_element_type=F32)
    glu = jnp.dot(h, wi_ref[:, D_MODEL:2 * D_MODEL], preferred_element_type=F32)
    gate = jnp.dot(h, wi_ref[:, 2 * D_MODEL:], preferred_element_type=F32)
    u_ref[...] = val * jax.nn.sigmoid(glu)
    sg_ref[...] = _silu(gate)


def _mid(x2d, mix_a, mix_b, w_out, g, w_in, tm):
    s = x2d.shape[0]
    row = lambda wd: pl.BlockSpec((tm, wd), lambda i: (i, 0))
    const = lambda shape: pl.BlockSpec(shape, lambda i: (0,) * len(shape))
    return pl.pallas_call(
        _mid_kernel,
        grid=(s // tm,),
        in_specs=[row(D_MODEL), row(A_W), row(B_VW), const((A_W + B_VW, D_MODEL)), const((1, D_MODEL)),
                  const((D_MODEL, 3 * D_MODEL))],
        out_specs=[row(D_MODEL), row(D_MODEL), row(D_MODEL)],
        out_shape=[jax.ShapeDtypeStruct((s, D_MODEL), F32)] * 3,
        compiler_params=pltpu.CompilerParams(dimension_semantics=("arbitrary",), vmem_limit_bytes=VMEM_LIMIT),
        name="mid",
    )(x2d, mix_a, mix_b, w_out, g, w_in)


HALO = 32
CONV_ROWS = 128


def _odd_out_kernel(u_ref, halo_ref, sg_ref, x1_ref, cw_ref, cb_ref, lg_ref, lb_ref, wo_ref, nf_ref, o_ref,
                    ubuf_ref, y_ref, shift_ref, *, tm):
    i = pl.program_id(0)
    ubuf_ref[HALO:, :] = u_ref[...]
    ubuf_ref[:HALO, :] = jnp.where(i > 0, halo_ref[...], 0.0)
    base = HALO - (CONV_WIDTH - 1)
    for cs in range(D_MODEL // LANES):
        ls = slice(cs * LANES, (cs + 1) * LANES)
        for r0 in range(0, tm, CONV_ROWS):
            acc = jnp.zeros((CONV_ROWS, LANES), F32)
            for res in range(SUBLANES):
                taps = [k for k in range(CONV_WIDTH) if (base + k) % SUBLANES == res]
                first = base + taps[0]
                rows = taps[-1] - taps[0] + CONV_ROWS
                shift_ref[0:rows, :] = ubuf_ref[r0 + first:r0 + first + rows, ls]
                aligned = shift_ref[0:rows, :]
                for k in taps:
                    a = base + k - first
                    acc = acc + aligned[a:a + CONV_ROWS] * cw_ref[k:k + 1, ls]
            y_ref[r0:r0 + CONV_ROWS, ls] = acc + cb_ref[:, ls]
    y = y_ref[...]
    mu = jnp.mean(y, axis=-1, keepdims=True)
    yc = y - mu
    var = jnp.mean(yc * yc, axis=-1, keepdims=True)
    yn = yc * lax.rsqrt(var + EPS) * lg_ref[...] + lb_ref[...]
    mixed = (_silu(yn) * sg_ref[...]).astype(BF16)
    x2 = x1_ref[...] + jnp.dot(mixed, wo_ref[...], preferred_element_type=F32)
    o_ref[...] = x2 * lax.rsqrt(jnp.mean(x2 * x2, axis=-1, keepdims=True) + EPS) * nf_ref[...]


def _odd_out(u, sg, x1, cw_pad, cb, lg, lb, w_out, nf, tm):
    s = u.shape[0]
    row = pl.BlockSpec((tm, D_MODEL), lambda i: (i, 0))
    const = lambda shape: pl.BlockSpec(shape, lambda i: (0,) * len(shape))
    halo = pl.BlockSpec((HALO, D_MODEL), lambda i: (jnp.maximum(i * (tm // HALO) - 1, 0), 0))
    return pl.pallas_call(
        functools.partial(_odd_out_kernel, tm=tm),
        grid=(s // tm,),
        in_specs=[row, halo, row, row, const((HALO, D_MODEL)), const((1, D_MODEL)), const((1, D_MODEL)),
                  const((1, D_MODEL)), const((D_MODEL, D_MODEL)), const((1, D_MODEL))],
        out_specs=row,
        out_shape=jax.ShapeDtypeStruct((s, D_MODEL), F32),
        scratch_shapes=[pltpu.VMEM((HALO + tm, D_MODEL), F32), pltpu.VMEM((tm, D_MODEL), F32),
                        pltpu.VMEM((HALO + CONV_ROWS, LANES), F32)],
        compiler_params=pltpu.CompilerParams(dimension_semantics=("arbitrary",), vmem_limit_bytes=VMEM_LIMIT),
        name="odd_out",
    )(u, u, sg, x1, cw_pad, cb, lg, lb, w_out, nf)


def kernel(x, rel_bias, norm_even, w_in_even, w_gla_lr, b_gla_lr, gla_norm, w_out_even, norm_odd, w_in_odd,
           conv_w, conv_b, conv_ln_g, conv_ln_b, w_out_odd, norm_f):
    bsz, s, d = x.shape
    assert bsz == 1 and d == D_MODEL and s % GLA_BLOCK == 0
    assert norm_even.shape[0] == 1 and norm_odd.shape[0] == 1
    top_k = min(TOPK_MAX, s // 4)
    tm = 256
    x2d = x.reshape(s, d)

    wt_pad = _even_in_weights(w_in_even[0])
    qa_t, ka, va_t, ga, qi_t, ki, wi_t, qb, kb, vb, gb, lr = _even_in(x2d, norm_even[0].reshape(1, d), wt_pad, tm)
    mix_a = _dsa(qa_t, qi_t, wi_t, ga, ka, va_t, ki, _dsa_bias_table(rel_bias), top_k)

    head_pad = ((0, 0), (0, 0), (0, LANES - B_DK))
    wlr_pad = jnp.pad(w_gla_lr[0].reshape(GLA_RANK, B_HEADS, B_DK), head_pad).reshape(GLA_RANK, B_HEADS * LANES)
    wlr_pad = jnp.pad(wlr_pad, ((0, LANES - GLA_RANK), (0, 0)))
    blr_pad = jnp.pad(b_gla_lr[0].reshape(1, B_HEADS, B_DK), head_pad).reshape(1, B_HEADS * LANES)
    mix_b = _gla(qb, kb, vb, gb, lr, wlr_pad, blr_pad, gla_norm[0].reshape(1, B_DV))

    x1, u, sg = _mid(x2d, mix_a, mix_b, w_out_even[0].astype(BF16), norm_odd[0].reshape(1, d),
                     w_in_odd[0].astype(BF16), tm)

    cw_pad = jnp.zeros((HALO, d), F32).at[:CONV_WIDTH].set(conv_w[0])
    out = _odd_out(u, sg, x1, cw_pad, conv_b[0].reshape(1, d), conv_ln_g[0].reshape(1, d),
                   conv_ln_b[0].reshape(1, d), w_out_odd[0].astype(BF16), norm_f.reshape(1, d), tm)
    return out.reshape(bsz, s, d)
```

```python
import functools
import math

import jax
import jax.numpy as jnp
import numpy as np
from jax import lax
from jax.experimental import pallas as pl
from jax.experimental.pallas import tpu as pltpu

D_MODEL = 1024
A_HEADS = 8
A_HEAD_DIM = 64
IDX_HEADS = 4
IDX_DIM = 64
TOPK_MAX = 256
B_HEADS = 4
B_DK = 64
B_DV = 128
GLA_RANK = 16
GLA_TAU = 16.0
GLA_CHUNK = 64
GLA_SUB = 16
CONV_WIDTH = 31
REL_BUCKETS = 32
REL_MAX_EXACT = 16
REL_MAX_DIST = 128
EPS = 1e-6

A_W = A_HEADS * A_HEAD_DIM
B_KW = B_HEADS * B_DK
B_VW = B_HEADS * B_DV

LANES = 128
SUBLANES = 8
VMEM_LIMIT = 56 * 1024 * 1024

F32 = jnp.float32
BF16 = jnp.bfloat16
NEG_INF = float("-inf")
INT_MIN = -(2 ** 31)
NEG_INF_BITS = int(np.array(-np.inf, np.float32).view(np.int32))
NEG_INF_KEY = NEG_INF_BITS ^ 0x7FFFFFFF
M_INIT = -1e30
LOG2E = math.log2(math.e)

C_QA = 0
C_KA = C_QA + A_HEADS * LANES
C_VA = C_KA + A_W
C_GA = C_VA + A_W
C_QI = C_GA + A_W
C_KI = C_QI + IDX_HEADS * LANES
C_WI = C_KI + LANES
C_QB = C_WI + LANES
C_KB = C_QB + B_HEADS * LANES
C_VB = C_KB + B_HEADS * LANES
C_GB = C_VB + B_VW
C_LR = C_GB + B_VW
C_END = C_LR + LANES


def _t5_bucket_np(dist):
    dist = np.maximum(dist, 0)
    d = np.maximum(dist, 1).astype(np.float32)
    large = REL_MAX_EXACT + (np.log(d / REL_MAX_EXACT) / math.log(REL_MAX_DIST / REL_MAX_EXACT)
                             * (REL_BUCKETS - REL_MAX_EXACT)).astype(np.int32)
    large = np.minimum(large, REL_BUCKETS - 1)
    return np.where(dist < REL_MAX_EXACT, dist, large)


def _silu(x):
    return x * jax.nn.sigmoid(x)


def _even_in_kernel(x_ref, g_ref, wt_ref, qat_ref, ka_ref, vat_ref, ga_ref, qit_ref, ki_ref, wit_ref,
                    qb_ref, kb_ref, vb_ref, gb_ref, lr_ref):
    x = x_ref[...]
    h = (x * lax.rsqrt(jnp.mean(x * x, axis=-1, keepdims=True) + EPS) * g_ref[...]).astype(BF16)
    dn_t = (((1,), (1,)), ((), ()))

    def proj(lo, hi):
        return lax.dot_general(h, wt_ref[lo:hi, :], dn_t, preferred_element_type=F32)

    def proj_t(lo, hi):
        return lax.dot_general(wt_ref[lo:hi, :], h, dn_t, preferred_element_type=F32)

    qat_ref[...] = proj_t(C_QA, C_KA).astype(BF16)
    ka_ref[...] = proj(C_KA, C_VA).astype(BF16)
    vat_ref[...] = proj_t(C_VA, C_GA).astype(BF16)
    ga_ref[...] = proj(C_GA, C_QI)
    qit_ref[...] = proj_t(C_QI, C_KI).astype(BF16)
    ki_ref[...] = proj(C_KI, C_WI).astype(BF16)
    wit_ref[...] = proj_t(C_WI, C_WI + SUBLANES)
    qb_ref[...] = proj(C_QB, C_KB)
    kb_ref[...] = proj(C_KB, C_VB)
    vb_ref[...] = proj(C_VB, C_GB)
    gb_ref[...] = proj(C_GB, C_LR)
    lr_ref[...] = proj(C_LR, C_END)


def _even_in_layout():
    cols = np.cumsum([0, A_W, A_W, A_W, A_W, IDX_HEADS * IDX_DIM, IDX_DIM, IDX_HEADS,
                      B_KW, B_KW, B_VW, B_VW, GLA_RANK])
    (s_qa, s_ka, s_va, s_ga, s_qi, s_ki, s_wi, s_qb, s_kb, s_vb, s_gb, s_lr, _) = [int(c) for c in cols]
    src = np.full((C_END,), -1, np.int32)
    scale = np.ones((C_END,), np.float32)

    def put(dst, start, width, sc=1.0):
        src[dst:dst + width] = np.arange(start, start + width)
        scale[dst:dst + width] = sc

    for h in range(A_HEADS):
        put(C_QA + h * LANES + (h % 2) * A_HEAD_DIM, s_qa + h * A_HEAD_DIM, A_HEAD_DIM, (A_HEAD_DIM ** -0.5) * LOG2E)
    put(C_KA, s_ka, A_W)
    put(C_VA, s_va, A_W)
    put(C_GA, s_ga, A_W)
    for h in range(IDX_HEADS):
        put(C_QI + h * LANES, s_qi + h * IDX_DIM, IDX_DIM)
    put(C_KI, s_ki, IDX_DIM)
    put(C_WI, s_wi, IDX_HEADS, (IDX_DIM ** -0.5) * (IDX_HEADS ** -0.5))
    for h in range(B_HEADS):
        put(C_QB + h * LANES, s_qb + h * B_DK, B_DK, B_DK ** -0.5)
        put(C_KB + h * LANES, s_kb + h * B_DK, B_DK)
    put(C_VB, s_vb, B_VW)
    put(C_GB, s_gb, B_VW)
    put(C_LR, s_lr, GLA_RANK)
    return src, scale


def _even_in_weights(w):
    src, scale = _even_in_layout()
    pieces, start = [], 0
    for end in range(1, C_END + 1):
        if end == C_END or (src[end] >= 0) != (src[start] >= 0) or (src[end] >= 0 and src[end] != src[end - 1] + 1):
            if src[start] >= 0:
                pieces.append(w[:, int(src[start]):int(src[end - 1]) + 1])
            else:
                pieces.append(jnp.zeros((D_MODEL, end - start), w.dtype))
            start = end
    w_pad = jnp.concatenate(pieces, axis=1) * jnp.asarray(scale)[None, :]
    return w_pad.T.astype(BF16)


def _even_in(x2d, g, wt_pad, tm):
    s = x2d.shape[0]
    row = lambda wd, dt: (pl.BlockSpec((tm, wd), lambda i: (i, 0)), jax.ShapeDtypeStruct((s, wd), dt))
    col = lambda ht, dt: (pl.BlockSpec((ht, tm), lambda i: (0, i)), jax.ShapeDtypeStruct((ht, s), dt))
    outs = [col(C_KA - C_QA, BF16), row(A_W, BF16), col(A_W, BF16), row(A_W, F32), col(C_KI - C_QI, BF16),
            row(LANES, BF16), col(SUBLANES, F32), row(C_KB - C_QB, F32), row(C_VB - C_KB, F32), row(B_VW, F32),
            row(B_VW, F32), row(LANES, F32)]
    return pl.pallas_call(
        _even_in_kernel,
        grid=(s // tm,),
        in_specs=[pl.BlockSpec((tm, D_MODEL), lambda i: (i, 0)),
                  pl.BlockSpec((1, D_MODEL), lambda i: (0, 0)),
                  pl.BlockSpec((C_END, D_MODEL), lambda i: (0, 0))],
        out_specs=[spec for spec, _ in outs],
        out_shape=[shape for _, shape in outs],
        compiler_params=pltpu.CompilerParams(dimension_semantics=("arbitrary",), vmem_limit_bytes=VMEM_LIMIT),
        name="even_in",
    )(x2d, g, wt_pad)


DSA_Q = 128
DSA_KB = 128
DSA_WIDE = 4
DSA_NACC = 4
DSA_CELLS = 64
DSA_CELL_BITS = 14


def _f32_key(x):
    bits = lax.bitcast_convert_type(x, jnp.int32)
    return bits ^ (lax.shift_right_arithmetic(bits, 31) & 0x7FFFFFFF)


def _fold_keys(x, op):
    n, w = x.shape
    group = DSA_NACC * SUBLANES
    if n % group == 0 and n > group:
        x = op(x.reshape(n // group, DSA_NACC, SUBLANES, w), axis=0)
        return op(x, axis=0)
    return op(x.reshape(n // SUBLANES, SUBLANES, w), axis=0)


def _dsa_kernel(qat_ref, qit_ref, wit_ref, ga_ref, ka_ref, vat_ref, ki_ref, btab_ref, tri_ref, o_ref,
                slab_ref, cell_ref, s_ref, acc_ref, m_ref, l_ref, *, top_k):
    q, kb, wide = DSA_Q, DSA_KB, DSA_WIDE * DSA_KB
    i = pl.program_id(0)
    nkb = i + 1
    nwide = jnp.maximum(i - 1, 0) // DSA_WIDE
    n0 = nwide * DSA_WIDE
    t_row = i * q + lax.broadcasted_iota(jnp.int32, (1, q), 1)
    key_off = lax.broadcasted_iota(jnp.int32, (kb, 1), 0)
    npair = A_HEADS // 2

    def wide_rows(jw):
        return pl.ds(pl.multiple_of(jw * wide, wide), wide)

    def narrow_rows(j):
        return pl.ds(pl.multiple_of(j * kb, kb), kb)

    def pipelined(first, last, produce, consume):
        def body(t, carry):
            consume(t, (t - first) % 2, carry)
            return produce(t + 1, (t + 1 - first) % 2)

        carry = lax.fori_loop(first, last - 1, body, produce(first, 0))
        consume(last - 1, (last - 1 - first) % 2, carry)

    w_head = [wit_ref[h:h + 1, :] for h in range(IDX_HEADS)]

    def scores(rows):
        k_idx = ki_ref[rows, :]
        sc = None
        for h in range(IDX_HEADS):
            s = jnp.dot(k_idx, qit_ref[h * LANES:(h + 1) * LANES, :], preferred_element_type=F32)
            term = w_head[h] * jnp.maximum(s, 0.0)
            sc = term if sc is None else sc + term
        return sc

    assert wide % (DSA_CELLS * SUBLANES) == 0
    cell_ref[...] = jnp.full(cell_ref.shape, INT_MIN, jnp.int32)

    def score_rows(rows, n):
        key = _f32_key(scores(rows))
        slab_ref[rows, :] = key
        cells = jnp.max(key.reshape(n // (DSA_CELLS * SUBLANES), DSA_CELLS, SUBLANES, q), axis=0)
        cell_ref[...] = jnp.maximum(cell_ref[...], cells)

    def pair_rows(jp):
        return pl.ds(pl.multiple_of(jp * (2 * wide), 2 * wide), 2 * wide)

    def score_narrow(j, c):
        rows = narrow_rows(j)
        key = _f32_key(jnp.where(j * kb + key_off <= t_row, scores(rows), NEG_INF))
        slab_ref[rows, :] = key
        ncls = kb // SUBLANES
        cell_ref[0:ncls] = jnp.maximum(cell_ref[0:ncls], key.reshape(ncls, SUBLANES, q))
        return c

    lax.fori_loop(0, nwide // 2, lambda jp, c: score_rows(pair_rows(jp), 2 * wide), None)
    lax.fori_loop((nwide // 2) * 2, nwide, lambda jw, c: score_rows(wide_rows(jw), wide), None)
    lax.fori_loop(n0, nkb, score_narrow, 0)

    def count_ge(trial):
        def cnt(blk):
            return _fold_keys(jnp.where(blk >= trial, 1.0, 0.0), jnp.sum)

        acc = lax.fori_loop(0, nwide // 2, lambda jp, a: a + cnt(slab_ref[pair_rows(jp), :]),
                            jnp.zeros((SUBLANES, q), F32))
        acc = lax.fori_loop((nwide // 2) * 2, nwide, lambda jw, a: a + cnt(slab_ref[wide_rows(jw), :]), acc)
        acc = lax.fori_loop(n0, nkb, lambda j, a: a + cnt(slab_ref[narrow_rows(j), :]), acc)
        return jnp.sum(acc, axis=0, keepdims=True).astype(jnp.int32)

    assert DSA_CELLS * SUBLANES >= top_k
    cell_max = cell_ref[...]
    hi_key = jnp.max(jnp.max(cell_max, axis=0), axis=0, keepdims=True)

    def cell_bisect(b, c):
        trial = c + lax.shift_left(jnp.int32(1), 31 - b)
        cnt = jnp.sum(jnp.sum(jnp.where(cell_max >= trial, 1.0, 0.0), axis=0), axis=0, keepdims=True)
        return jnp.where(cnt >= top_k, trial, c)

    lo_key = lax.fori_loop(0, DSA_CELL_BITS, cell_bisect, jnp.full((1, q), INT_MIN, jnp.int32))
    base = lo_key
    span = hi_key - lo_key
    nbits = jnp.max(32 - lax.clz(span))

    def bisect(b, state):
        off, cnt, above = state
        trial_off = off + lax.shift_left(jnp.int32(1), nbits - 1 - b)
        in_span = (trial_off ^ INT_MIN) <= (span ^ INT_MIN)
        c = jnp.where(in_span, count_ge(base + trial_off), 0)
        ok = c >= top_k
        return jnp.where(ok, trial_off, off), jnp.where(ok, c, cnt), jnp.where(ok, above, c)

    uncounted = jnp.full((1, q), -1, jnp.int32)
    off, cnt_thr, cnt_above = lax.fori_loop(0, nbits, bisect, (jnp.zeros((1, q), jnp.int32), uncounted, uncounted))
    thr = base + off
    cnt_thr = lax.cond(jnp.min(cnt_thr) < 0, lambda: count_ge(thr), lambda: cnt_thr)
    tied = jnp.logical_and(cnt_thr > top_k, thr > NEG_INF_KEY)
    any_tie = jnp.max(jnp.where(tied, 1, 0)) > 0

    thr_valid = jnp.maximum(thr, NEG_INF_KEY + 1)

    def mask_fast():
        def body(rows):
            slab_ref[rows, :] = jnp.where(slab_ref[rows, :] >= thr_valid, 0, NEG_INF_BITS)

        lax.fori_loop(0, nwide, lambda jw, c: body(wide_rows(jw)), None)
        lax.fori_loop(n0, nkb, lambda j, c: body(narrow_rows(j)), None)

    def mask_tied():
        above = lax.cond(jnp.min(cnt_above) < 0, lambda: count_ge(thr + 1), lambda: cnt_above)
        thr_c = jnp.maximum(thr, NEG_INF_KEY)
        need = jnp.where(thr_c > NEG_INF_KEY, (top_k - above).astype(F32), -1.0)

        def body(rows, n, seen):
            for g in range(n // kb):
                blk = pl.ds(rows.start + g * kb, kb)
                key = slab_ref[blk, :]
                eq = key == thr_c
                eq_f = jnp.where(eq, 1.0, 0.0)
                before = jnp.dot(tri_ref[...], eq_f.astype(BF16), preferred_element_type=F32)
                tie_bits = jnp.where(seen + before < need, 0, NEG_INF_BITS)
                slab_ref[blk, :] = jnp.where(key > thr_c, 0, jnp.where(eq, tie_bits, NEG_INF_BITS))
                seen = seen + jnp.sum(_fold_keys(eq_f, jnp.sum), axis=0, keepdims=True)
            return seen

        seen = lax.fori_loop(0, nwide // 2, lambda jp, c: body(pair_rows(jp), 2 * wide, c), jnp.zeros((1, q), F32))
        seen = lax.fori_loop((nwide // 2) * 2, nwide, lambda jw, c: body(wide_rows(jw), wide, c), seen)
        lax.fori_loop(n0, nkb, lambda j, c: body(narrow_rows(j), kb, c), seen)

    lax.cond(any_tie, mask_tied, mask_fast)

    m_ref[...] = jnp.full(m_ref.shape, M_INIT, F32)
    l_ref[...] = jnp.zeros(l_ref.shape, F32)
    acc_ref[...] = jnp.zeros(acc_ref.shape, F32)

    def logits(rows, n, buf, rel):
        mask = lax.bitcast_convert_type(slab_ref[rows, :], F32)
        mask2 = jnp.concatenate([mask, mask], axis=1)
        tile_max = []
        for p in range(npair):
            ls = slice(p * LANES, (p + 1) * LANES)
            w_q = jnp.concatenate([qat_ref[(2 * p) * LANES:(2 * p + 1) * LANES, :],
                                   qat_ref[(2 * p + 1) * LANES:(2 * p + 2) * LANES, :]], axis=1)
            s = jnp.dot(ka_ref[rows, ls], w_q, preferred_element_type=F32) + mask2
            if rel is not None:
                s = s + btab_ref[rel, p]
            s_ref[buf, p, 0:n, :] = s
            tile_max.append(_fold_keys(s, jnp.max))
        return tuple(tile_max)

    def attend(rows, n, buf, tile_max):
        for p in range(npair):
            ls = slice(p * LANES, (p + 1) * LANES)
            m_old = m_ref[p]
            m_new = jnp.maximum(m_old, jnp.max(tile_max[p], axis=0, keepdims=True))
            alpha = jnp.exp2(m_old - m_new)
            pexp = jnp.exp2(s_ref[buf, p, 0:n, :] - m_new[0:1])
            m_ref[p] = m_new
            l_ref[p] = alpha * l_ref[p] + _fold_keys(pexp, jnp.sum)
            acc_ref[p] = alpha[0:1] * acc_ref[p] + jnp.dot(vat_ref[ls, rows], pexp.astype(BF16),
                                                           preferred_element_type=F32)

    @pl.when(nwide > 0)
    def _():
        pipelined(0, nwide, lambda jw, buf: logits(wide_rows(jw), wide, buf, None),
                  lambda jw, buf, tile_max: attend(wide_rows(jw), wide, buf, tile_max))

    pipelined(n0, nkb, lambda j, buf: logits(narrow_rows(j), kb, buf, jnp.minimum(i - j, 2)),
              lambda j, buf, tile_max: attend(narrow_rows(j), kb, buf, tile_max))

    for p in range(npair):
        o_t = acc_ref[p] / jnp.sum(l_ref[p], axis=0, keepdims=True)
        o_pair = jnp.concatenate([o_t[:A_HEAD_DIM, :q], o_t[A_HEAD_DIM:, q:]], axis=0).T
        ls = slice(p * LANES, (p + 1) * LANES)
        o_ref[:, ls] = (o_pair * _silu(ga_ref[:, ls])).astype(o_ref.dtype)


def _dsa_bias_table(rel_bias):
    ii = np.arange(DSA_Q)[:, None]
    jj = np.arange(DSA_KB)[None, :]
    buckets = np.stack([_t5_bucket_np(ii - jj), _t5_bucket_np(DSA_KB + ii - jj),
                        np.full((DSA_Q, DSA_KB), REL_BUCKETS - 1)])
    assert (_t5_bucket_np(np.arange(DSA_KB + 1, 4 * DSA_KB)) == REL_BUCKETS - 1).all()
    rb = rel_bias.astype(F32) * LOG2E
    rb = rb - rb[REL_BUCKETS - 1]
    onehot = np.zeros((3, DSA_KB, DSA_Q, REL_BUCKETS), np.float32)
    np.put_along_axis(onehot, np.transpose(buckets, (0, 2, 1))[..., None], 1.0, axis=-1)
    tab = jnp.einsum('tsrb,bh->thsr', jnp.asarray(onehot), rb, precision=lax.Precision.HIGHEST)
    tab = tab.reshape(3, A_HEADS // 2, 2, DSA_KB, DSA_Q)
    return jnp.transpose(tab, (0, 1, 3, 2, 4)).reshape(3, A_HEADS // 2, DSA_KB, 2 * DSA_Q)


def _dsa(qa_t, qi_t, wi_t, ga, ka, va_t, ki, btab, top_k):
    s = ga.shape[0]
    q, kb = DSA_Q, DSA_KB
    tri = jnp.asarray(np.tril(np.ones((kb, kb), np.float32), -1), BF16)
    whole = lambda shape: pl.BlockSpec(shape, lambda i: (0,) * len(shape), pipeline_mode=pl.Buffered(1))
    return pl.pallas_call(
        functools.partial(_dsa_kernel, top_k=top_k),
        grid=(s // q,),
        in_specs=[pl.BlockSpec((A_HEADS * LANES, q), lambda i: (0, i)),
                  pl.BlockSpec((IDX_HEADS * LANES, q), lambda i: (0, i)),
                  pl.BlockSpec((SUBLANES, q), lambda i: (0, i)),
                  pl.BlockSpec((q, A_W), lambda i: (i, 0)),
                  whole((s, A_W)), whole((A_W, s)), whole((s, LANES)),
                  whole((3, A_HEADS // 2, kb, 2 * q)), whole((kb, kb))],
        out_specs=pl.BlockSpec((q, A_W), lambda i: (i, 0)),
        out_shape=jax.ShapeDtypeStruct((s, A_W), BF16),
        scratch_shapes=[pltpu.VMEM((s, q), jnp.int32),
                        pltpu.VMEM((DSA_CELLS, SUBLANES, q), jnp.int32),
                        pltpu.VMEM((2, A_HEADS // 2, DSA_WIDE * kb, 2 * q), F32),
                        pltpu.VMEM((A_HEADS // 2, LANES, 2 * q), F32),
                        pltpu.VMEM((A_HEADS // 2, SUBLANES, 2 * q), F32),
                        pltpu.VMEM((A_HEADS // 2, SUBLANES, 2 * q), F32)],
        compiler_params=pltpu.CompilerParams(dimension_semantics=("arbitrary",), vmem_limit_bytes=VMEM_LIMIT),
        name="dsa",
    )(qa_t, qi_t, wi_t, ga, ka, va_t, ki, btab, tri)


GLA_BLOCK = 512


def _gla_kernel(qb_ref, kb_ref, vb_ref, gb_ref, lr_ref, wlr_ref, blr_ref, gn_ref, tril_ref, o_ref,
                state_ref, b_ref, att_ref):
    c, sub = GLA_CHUNK, GLA_SUB
    nsub = c // sub

    @pl.when(pl.program_id(0) == 0)
    def _():
        state_ref[...] = jnp.zeros(state_ref.shape, F32)

    dn_t = (((1,), (1,)), ((), ()))
    dn_0 = (((0,), (0,)), ((), ()))
    hp = lax.Precision.HIGHEST
    row_i = lax.broadcasted_iota(jnp.int32, (sub, sub), 0)
    col_j = lax.broadcasted_iota(jnp.int32, (sub, sub), 1)

    def chunk(ci, carry):
        r0 = pl.multiple_of(ci * c, c)
        z = jnp.dot(lr_ref[pl.ds(r0, c), :], wlr_ref[...], precision=hp, preferred_element_type=F32) + blr_ref[...]
        log_a = (jnp.minimum(z, 0.0) - jnp.log1p(jnp.exp(-jnp.abs(z)))) * (LOG2E / GLA_TAU)
        b_ref[...] = jnp.dot(tril_ref[...], log_a, precision=hp, preferred_element_type=F32)
        for h in range(B_HEADS):
            ls = slice(h * LANES, (h + 1) * LANES)
            b = b_ref[:, ls]
            qh = qb_ref[pl.ds(r0, c), ls]
            kh = kb_ref[pl.ds(r0, c), ls]
            vh = vb_ref[pl.ds(r0, c), ls]
            state_t = state_ref[h]
            b_last = b[c - 1:c, :]
            o = lax.dot_general((qh * jnp.exp2(b)).astype(BF16), state_t.astype(BF16), dn_t,
                                preferred_element_type=F32)
            for si in range(nsub):
                rs = slice(si * sub, (si + 1) * sub)
                b_i, q_i = b[rs], qh[rs]
                if si > 0:
                    ref_row = b[si * sub:si * sub + 1, :]
                    q_t = (q_i * jnp.exp2(b_i - ref_row)).astype(BF16)
                    k_t = (kh[:si * sub] * jnp.exp2(ref_row - b[:si * sub])).astype(BF16)
                    att_ref[rs, :si * sub] = lax.dot_general(q_t, k_t, dn_t, preferred_element_type=F32)
                k_i = kh[rs]
                diag = jnp.zeros((sub, sub), F32)
                for jl in range(sub):
                    diff = jnp.minimum(b_i - b_i[jl:jl + 1, :], 0.0)
                    col = jnp.sum(q_i * k_i[jl:jl + 1, :] * jnp.exp2(diff), axis=-1, keepdims=True)
                    diag = jnp.where(col_j == jl, col, diag)
                att_ref[rs, rs] = jnp.where(col_j <= row_i, diag, 0.0)
                if si + 1 < nsub:
                    att_ref[rs, (si + 1) * sub:] = jnp.zeros((sub, c - (si + 1) * sub), F32)
            o = o + jnp.dot(att_ref[...].astype(BF16), vh.astype(BF16), preferred_element_type=F32)
            k_dec = (kh * jnp.exp2(b_last - b)).astype(BF16)
            state_ref[h] = (jnp.exp2(b_last) * state_t
                            + lax.dot_general(vh.astype(BF16), k_dec, dn_0, preferred_element_type=F32))
            o = o * lax.rsqrt(jnp.mean(o * o, axis=-1, keepdims=True) + EPS) * gn_ref[...]
            o_ref[pl.ds(r0, c), ls] = (o * _silu(gb_ref[pl.ds(r0, c), ls])).astype(o_ref.dtype)
        return carry

    lax.fori_loop(0, GLA_BLOCK // c, chunk, 0)


def _gla(qb, kb, vb, gb, lr, wlr_pad, blr_pad, gn):
    s = qb.shape[0]
    blk = min(GLA_BLOCK, s)
    assert blk == GLA_BLOCK and s % blk == 0
    tril = jnp.asarray(np.tril(np.ones((GLA_CHUNK, GLA_CHUNK), np.float32)))
    row = lambda wd: pl.BlockSpec((blk, wd), lambda i: (i, 0))
    const = lambda shape: pl.BlockSpec(shape, lambda i: (0,) * len(shape))
    return pl.pallas_call(
        _gla_kernel,
        grid=(s // blk,),
        in_specs=[row(B_HEADS * LANES), row(B_HEADS * LANES), row(B_VW), row(B_VW), row(LANES),
                  const((LANES, B_HEADS * LANES)), const((1, B_HEADS * LANES)), const((1, B_DV)),
                  const((GLA_CHUNK, GLA_CHUNK))],
        out_specs=row(B_VW),
        out_shape=jax.ShapeDtypeStruct((s, B_VW), BF16),
        scratch_shapes=[pltpu.VMEM((B_HEADS, LANES, B_DV), F32),
                        pltpu.VMEM((GLA_CHUNK, B_HEADS * LANES), F32),
                        pltpu.VMEM((GLA_CHUNK, GLA_CHUNK), F32)],
        compiler_params=pltpu.CompilerParams(dimension_semantics=("arbitrary",), vmem_limit_bytes=VMEM_LIMIT),
        name="gla",
    )(qb, kb, vb, gb, lr, wlr_pad, blr_pad, gn, tril)


def _mid_kernel(x_ref, ma_ref, mb_ref, wo_ref, g_ref, wi_ref, x1_ref, u_ref, sg_ref):
    x1 = (x_ref[...]
          + jnp.dot(ma_ref[...], wo_ref[:A_W, :], preferred_element_type=F32)
          + jnp.dot(mb_ref[...], wo_ref[A_W:, :], preferred_element_type=F32))
    x1_ref[...] = x1
    h = (x1 * lax.rsqrt(jnp.mean(x1 * x1, axis=-1, keepdims=True) + EPS) * g_ref[...]).astype(BF16)
    val = jnp.dot(h, wi_ref[:, :D_MODEL], preferred_element_type=F32)
    glu = jnp.dot(h, wi_ref[:, D_MODEL:2 * D_MODEL], preferred_element_type=F32)
    gate = jnp.dot(h, wi_ref[:, 2 * D_MODEL:], preferred_element_type=F32)
    u_ref[...] = val * jax.nn.sigmoid(glu)
    sg_ref[...] = _silu(gate)


def _mid(x2d, mix_a, mix_b, w_out, g, w_in, tm):
    s = x2d.shape[0]
    row = lambda wd: pl.BlockSpec((tm, wd), lambda i: (i, 0))
    const = lambda shape: pl.BlockSpec(shape, lambda i: (0,) * len(shape))
    return pl.pallas_call(
        _mid_kernel,
        grid=(s // tm,),
        in_specs=[row(D_MODEL), row(A_W), row(B_VW), const((A_W + B_VW, D_MODEL)), const((1, D_MODEL)),
                  const((D_MODEL, 3 * D_MODEL))],
        out_specs=[row(D_MODEL), row(D_MODEL), row(D_MODEL)],
        out_shape=[jax.ShapeDtypeStruct((s, D_MODEL), F32)] * 3,
        compiler_params=pltpu.CompilerParams(dimension_semantics=("arbitrary",), vmem_limit_bytes=VMEM_LIMIT),
        name="mid",
    )(x2d, mix_a, mix_b, w_out, g, w_in)


HALO = 32
CONV_ROWS = 128


def _odd_out_kernel(u_ref, halo_ref, sg_ref, x1_ref, cw_ref, cb_ref, lg_ref, lb_ref, wo_ref, nf_ref, o_ref,
                    ubuf_ref, y_ref, shift_ref, *, tm):
    i = pl.program_id(0)
    ubuf_ref[HALO:, :] = u_ref[...]
    ubuf_ref[:HALO, :] = jnp.where(i > 0, halo_ref[...], 0.0)
    base = HALO - (CONV_WIDTH - 1)
    for cs in range(D_MODEL // LANES):
        ls = slice(cs * LANES, (cs + 1) * LANES)
        for r0 in range(0, tm, CONV_ROWS):
            acc = jnp.zeros((CONV_ROWS, LANES), F32)
            for res in range(SUBLANES):
                taps = [k for k in range(CONV_WIDTH) if (base + k) % SUBLANES == res]
                first = base + taps[0]
                rows = taps[-1] - taps[0] + CONV_ROWS
                shift_ref[0:rows, :] = ubuf_ref[r0 + first:r0 + first + rows, ls]
                aligned = shift_ref[0:rows, :]
                for k in taps:
                    a = base + k - first
                    acc = acc + aligned[a:a + CONV_ROWS] * cw_ref[k:k + 1, ls]
            y_ref[r0:r0 + CONV_ROWS, ls] = acc + cb_ref[:, ls]
    y = y_ref[...]
    mu = jnp.mean(y, axis=-1, keepdims=True)
    yc = y - mu
    var = jnp.mean(yc * yc, axis=-1, keepdims=True)
    yn = yc * lax.rsqrt(var + EPS) * lg_ref[...] + lb_ref[...]
    mixed = (_silu(yn) * sg_ref[...]).astype(BF16)
    x2 = x1_ref[...] + jnp.dot(mixed, wo_ref[...], preferred_element_type=F32)
    o_ref[...] = x2 * lax.rsqrt(jnp.mean(x2 * x2, axis=-1, keepdims=True) + EPS) * nf_ref[...]


def _odd_out(u, sg, x1, cw_pad, cb, lg, lb, w_out, nf, tm):
    s = u.shape[0]
    row = pl.BlockSpec((tm, D_MODEL), lambda i: (i, 0))
    const = lambda shape: pl.BlockSpec(shape, lambda i: (0,) * len(shape))
    halo = pl.BlockSpec((HALO, D_MODEL), lambda i: (jnp.maximum(i * (tm // HALO) - 1, 0), 0))
    return pl.pallas_call(
        functools.partial(_odd_out_kernel, tm=tm),
        grid=(s // tm,),
        in_specs=[row, halo, row, row, const((HALO, D_MODEL)), const((1, D_MODEL)), const((1, D_MODEL)),
                  const((1, D_MODEL)), const((D_MODEL, D_MODEL)), const((1, D_MODEL))],
        out_specs=row,
        out_shape=jax.ShapeDtypeStruct((s, D_MODEL), F32),
        scratch_shapes=[pltpu.VMEM((HALO + tm, D_MODEL), F32), pltpu.VMEM((tm, D_MODEL), F32),
                        pltpu.VMEM((HALO + CONV_ROWS, LANES), F32)],
        compiler_params=pltpu.CompilerParams(dimension_semantics=("arbitrary",), vmem_limit_bytes=VMEM_LIMIT),
        name="odd_out",
    )(u, u, sg, x1, cw_pad, cb, lg, lb, w_out, nf)


def kernel(x, rel_bias, norm_even, w_in_even, w_gla_lr, b_gla_lr, gla_norm, w_out_even, norm_odd, w_in_odd,
           conv_w, conv_b, conv_ln_g, conv_ln_b, w_out_odd, norm_f):
    bsz, s, d = x.shape
    assert bsz == 1 and d == D_MODEL and s % GLA_BLOCK == 0
    assert norm_even.shape[0] == 1 and norm_odd.shape[0] == 1
    top_k = min(TOPK_MAX, s // 4)
    tm = 512
    x2d = x.reshape(s, d)

    wt_pad = _even_in_weights(w_in_even[0])
    qa_t, ka, va_t, ga, qi_t, ki, wi_t, qb, kb, vb, gb, lr = _even_in(x2d, norm_even[0].reshape(1, d), wt_pad, tm)
    mix_a = _dsa(qa_t, qi_t, wi_t, ga, ka, va_t, ki, _dsa_bias_table(rel_bias), top_k)

    head_pad = ((0, 0), (0, 0), (0, LANES - B_DK))
    wlr_pad = jnp.pad(w_gla_lr[0].reshape(GLA_RANK, B_HEADS, B_DK), head_pad).reshape(GLA_RANK, B_HEADS * LANES)
    wlr_pad = jnp.pad(wlr_pad, ((0, LANES - GLA_RANK), (0, 0)))
    blr_pad = jnp.pad(b_gla_lr[0].reshape(1, B_HEADS, B_DK), head_pad).reshape(1, B_HEADS * LANES)
    mix_b = _gla(qb, kb, vb, gb, lr, wlr_pad, blr_pad, gla_norm[0].reshape(1, B_DV))

    x1, u, sg = _mid(x2d, mix_a, mix_b, w_out_even[0].astype(BF16), norm_odd[0].reshape(1, d),
                     w_in_odd[0].astype(BF16), tm)

    cw_pad = jnp.zeros((HALO, d), F32).at[:CONV_WIDTH].set(conv_w[0])
    out = _odd_out(u, sg, x1, cw_pad, conv_b[0].reshape(1, d), conv_ln_g[0].reshape(1, d),
                   conv_ln_b[0].reshape(1, d), w_out_odd[0].astype(BF16), norm_f.reshape(1, d), tm)
    return out.reshape(bsz, s, d)
```

```python
import functools
import math

import jax
import jax.numpy as jnp
import numpy as np
from jax import lax
from jax.experimental import pallas as pl
from jax.experimental.pallas import tpu as pltpu

D_MODEL = 1024
A_HEADS = 8
A_HEAD_DIM = 64
IDX_HEADS = 4
IDX_DIM = 64
TOPK_MAX = 256
B_HEADS = 4
B_DK = 64
B_DV = 128
GLA_RANK = 16
GLA_TAU = 16.0
GLA_CHUNK = 64
GLA_SUB = 16
CONV_WIDTH = 31
REL_BUCKETS = 32
REL_MAX_EXACT = 16
REL_MAX_DIST = 128
EPS = 1e-6

A_W = A_HEADS * A_HEAD_DIM
B_KW = B_HEADS * B_DK
B_VW = B_HEADS * B_DV

LANES = 128
SUBLANES = 8
VMEM_LIMIT = 56 * 1024 * 1024

F32 = jnp.float32
BF16 = jnp.bfloat16
NEG_INF = float("-inf")
INT_MIN = -(2 ** 31)
NEG_INF_BITS = int(np.array(-np.inf, np.float32).view(np.int32))
NEG_INF_KEY = NEG_INF_BITS ^ 0x7FFFFFFF
M_INIT = -1e30
LOG2E = math.log2(math.e)

C_QA = 0
C_KA = C_QA + A_HEADS * LANES
C_VA = C_KA + A_W
C_GA = C_VA + A_W
C_QI = C_GA + A_W
C_KI = C_QI + IDX_HEADS * LANES
C_WI = C_KI + LANES
C_QB = C_WI + LANES
C_KB = C_QB + B_HEADS * LANES
C_VB = C_KB + B_HEADS * LANES
C_GB = C_VB + B_VW
C_LR = C_GB + B_VW
C_END = C_LR + LANES


def _t5_bucket_np(dist):
    dist = np.maximum(dist, 0)
    d = np.maximum(dist, 1).astype(np.float32)
    large = REL_MAX_EXACT + (np.log(d / REL_MAX_EXACT) / math.log(REL_MAX_DIST / REL_MAX_EXACT)
                             * (REL_BUCKETS - REL_MAX_EXACT)).astype(np.int32)
    large = np.minimum(large, REL_BUCKETS - 1)
    return np.where(dist < REL_MAX_EXACT, dist, large)


def _silu(x):
    return x * jax.nn.sigmoid(x)


def _even_in_kernel(x_ref, g_ref, wt_ref, qat_ref, ka_ref, vat_ref, ga_ref, qit_ref, ki_ref, wit_ref,
                    qb_ref, kb_ref, vb_ref, gb_ref, lr_ref):
    x = x_ref[...]
    h = (x * lax.rsqrt(jnp.mean(x * x, axis=-1, keepdims=True) + EPS) * g_ref[...]).astype(BF16)
    dn_t = (((1,), (1,)), ((), ()))

    def proj(lo, hi):
        return lax.dot_general(h, wt_ref[lo:hi, :], dn_t, preferred_element_type=F32)

    def proj_t(lo, hi):
        return lax.dot_general(wt_ref[lo:hi, :], h, dn_t, preferred_element_type=F32)

    qat_ref[...] = proj_t(C_QA, C_KA).astype(BF16)
    ka_ref[...] = proj(C_KA, C_VA).astype(BF16)
    vat_ref[...] = proj_t(C_VA, C_GA).astype(BF16)
    ga_ref[...] = proj(C_GA, C_QI)
    qit_ref[...] = proj_t(C_QI, C_KI).astype(BF16)
    ki_ref[...] = proj(C_KI, C_WI).astype(BF16)
    wit_ref[...] = proj_t(C_WI, C_WI + SUBLANES)
    qb_ref[...] = proj(C_QB, C_KB)
    kb_ref[...] = proj(C_KB, C_VB)
    vb_ref[...] = proj(C_VB, C_GB)
    gb_ref[...] = proj(C_GB, C_LR)
    lr_ref[...] = proj(C_LR, C_END)


def _even_in_layout():
    cols = np.cumsum([0, A_W, A_W, A_W, A_W, IDX_HEADS * IDX_DIM, IDX_DIM, IDX_HEADS,
                      B_KW, B_KW, B_VW, B_VW, GLA_RANK])
    (s_qa, s_ka, s_va, s_ga, s_qi, s_ki, s_wi, s_qb, s_kb, s_vb, s_gb, s_lr, _) = [int(c) for c in cols]
    src = np.full((C_END,), -1, np.int32)
    scale = np.ones((C_END,), np.float32)

    def put(dst, start, width, sc=1.0):
        src[dst:dst + width] = np.arange(start, start + width)
        scale[dst:dst + width] = sc

    for h in range(A_HEADS):
        put(C_QA + h * LANES + (h % 2) * A_HEAD_DIM, s_qa + h * A_HEAD_DIM, A_HEAD_DIM, (A_HEAD_DIM ** -0.5) * LOG2E)
    put(C_KA, s_ka, A_W)
    put(C_VA, s_va, A_W)
    put(C_GA, s_ga, A_W)
    for h in range(IDX_HEADS):
        put(C_QI + h * LANES, s_qi + h * IDX_DIM, IDX_DIM)
    put(C_KI, s_ki, IDX_DIM)
    put(C_WI, s_wi, IDX_HEADS, (IDX_DIM ** -0.5) * (IDX_HEADS ** -0.5))
    for h in range(B_HEADS):
        put(C_QB + h * LANES, s_qb + h * B_DK, B_DK, B_DK ** -0.5)
        put(C_KB + h * LANES, s_kb + h * B_DK, B_DK)
    put(C_VB, s_vb, B_VW)
    put(C_GB, s_gb, B_VW)
    put(C_LR, s_lr, GLA_RANK)
    return src, scale


def _even_in_weights(w):
    src, scale = _even_in_layout()
    pieces, start = [], 0
    for end in range(1, C_END + 1):
        if end == C_END or (src[end] >= 0) != (src[start] >= 0) or (src[end] >= 0 and src[end] != src[end - 1] + 1):
            if src[start] >= 0:
                pieces.append(w[:, int(src[start]):int(src[end - 1]) + 1])
            else:
                pieces.append(jnp.zeros((D_MODEL, end - start), w.dtype))
            start = end
    w_pad = jnp.concatenate(pieces, axis=1) * jnp.asarray(scale)[None, :]
    return w_pad.T.astype(BF16)


def _even_in(x2d, g, wt_pad, tm):
    s = x2d.shape[0]
    row = lambda wd, dt: (pl.BlockSpec((tm, wd), lambda i: (i, 0)), jax.ShapeDtypeStruct((s, wd), dt))
    col = lambda ht, dt: (pl.BlockSpec((ht, tm), lambda i: (0, i)), jax.ShapeDtypeStruct((ht, s), dt))
    outs = [col(C_KA - C_QA, BF16), row(A_W, BF16), col(A_W, BF16), row(A_W, F32), col(C_KI - C_QI, BF16),
            row(LANES, BF16), col(SUBLANES, F32), row(C_KB - C_QB, F32), row(C_VB - C_KB, F32), row(B_VW, F32),
            row(B_VW, F32), row(LANES, F32)]
    return pl.pallas_call(
        _even_in_kernel,
        grid=(s // tm,),
        in_specs=[pl.BlockSpec((tm, D_MODEL), lambda i: (i, 0)),
                  pl.BlockSpec((1, D_MODEL), lambda i: (0, 0)),
                  pl.BlockSpec((C_END, D_MODEL), lambda i: (0, 0))],
        out_specs=[spec for spec, _ in outs],
        out_shape=[shape for _, shape in outs],
        compiler_params=pltpu.CompilerParams(dimension_semantics=("arbitrary",), vmem_limit_bytes=VMEM_LIMIT),
        name="even_in",
    )(x2d, g, wt_pad)


DSA_Q = 128
DSA_KB = 128
DSA_WIDE = 4
DSA_NACC = 4
DSA_CELLS = 64
DSA_CELL_BITS = 14


def _f32_key(x):
    bits = lax.bitcast_convert_type(x, jnp.int32)
    return bits ^ (lax.shift_right_arithmetic(bits, 31) & 0x7FFFFFFF)


def _fold_keys(x, op):
    n, w = x.shape
    group = DSA_NACC * SUBLANES
    if n % group == 0 and n > group:
        x = op(x.reshape(n // group, DSA_NACC, SUBLANES, w), axis=0)
        return op(x, axis=0)
    return op(x.reshape(n // SUBLANES, SUBLANES, w), axis=0)


def _dsa_kernel(qat_ref, qit_ref, wit_ref, ga_ref, ka_ref, vat_ref, ki_ref, btab_ref, tri_ref, o_ref,
                slab_ref, cell_ref, s_ref, acc_ref, m_ref, l_ref, *, top_k):
    q, kb, wide = DSA_Q, DSA_KB, DSA_WIDE * DSA_KB
    i = pl.program_id(0)
    nkb = i + 1
    nwide = jnp.maximum(i - 1, 0) // DSA_WIDE
    n0 = nwide * DSA_WIDE
    t_row = i * q + lax.broadcasted_iota(jnp.int32, (1, q), 1)
    key_off = lax.broadcasted_iota(jnp.int32, (kb, 1), 0)
    npair = A_HEADS // 2

    def wide_rows(jw):
        return pl.ds(pl.multiple_of(jw * wide, wide), wide)

    def narrow_rows(j):
        return pl.ds(pl.multiple_of(j * kb, kb), kb)

    def pipelined(first, last, produce, consume):
        def body(u, carry):
            t = first + 2 * u
            consume(t, 0, carry)
            consume(t + 1, 1, produce(t + 1, 1))
            return produce(t + 2, 0)

        npairs = (last - first - 1) // 2
        carry = lax.fori_loop(0, npairs, body, produce(first, 0))
        t = first + 2 * npairs
        consume(t, 0, carry)

        @pl.when(t + 1 < last)
        def _():
            consume(t + 1, 1, produce(t + 1, 1))

    w_head = [wit_ref[h:h + 1, :] for h in range(IDX_HEADS)]

    def scores(rows):
        k_idx = ki_ref[rows, :]
        sc = None
        for h in range(IDX_HEADS):
            s = jnp.dot(k_idx, qit_ref[h * LANES:(h + 1) * LANES, :], preferred_element_type=F32)
            term = w_head[h] * jnp.maximum(s, 0.0)
            sc = term if sc is None else sc + term
        return sc

    assert wide % (DSA_CELLS * SUBLANES) == 0
    cell_ref[...] = jnp.full(cell_ref.shape, INT_MIN, jnp.int32)

    def score_rows(rows, n):
        key = _f32_key(scores(rows))
        slab_ref[rows, :] = key
        cells = jnp.max(key.reshape(n // (DSA_CELLS * SUBLANES), DSA_CELLS, SUBLANES, q), axis=0)
        cell_ref[...] = jnp.maximum(cell_ref[...], cells)

    def pair_rows(jp):
        return pl.ds(pl.multiple_of(jp * (2 * wide), 2 * wide), 2 * wide)

    def score_narrow(j, c):
        rows = narrow_rows(j)
        key = _f32_key(jnp.where(j * kb + key_off <= t_row, scores(rows), NEG_INF))
        slab_ref[rows, :] = key
        ncls = kb // SUBLANES
        cell_ref[0:ncls] = jnp.maximum(cell_ref[0:ncls], key.reshape(ncls, SUBLANES, q))
        return c

    lax.fori_loop(0, nwide // 2, lambda jp, c: score_rows(pair_rows(jp), 2 * wide), None)
    lax.fori_loop((nwide // 2) * 2, nwide, lambda jw, c: score_rows(wide_rows(jw), wide), None)
    lax.fori_loop(n0, nkb, score_narrow, 0)

    def count_ge(trial):
        def cnt(blk):
            return _fold_keys(jnp.where(blk >= trial, 1.0, 0.0), jnp.sum)

        acc = lax.fori_loop(0, nwide // 2, lambda jp, a: a + cnt(slab_ref[pair_rows(jp), :]),
                            jnp.zeros((SUBLANES, q), F32))
        acc = lax.fori_loop((nwide // 2) * 2, nwide, lambda jw, a: a + cnt(slab_ref[wide_rows(jw), :]), acc)
        acc = lax.fori_loop(n0, nkb, lambda j, a: a + cnt(slab_ref[narrow_rows(j), :]), acc)
        return jnp.sum(acc, axis=0, keepdims=True).astype(jnp.int32)

    assert DSA_CELLS * SUBLANES >= top_k
    cell_max = cell_ref[...]
    hi_key = jnp.max(jnp.max(cell_max, axis=0), axis=0, keepdims=True)

    def cell_bisect(b, c):
        trial = c + lax.shift_left(jnp.int32(1), 31 - b)
        cnt = jnp.sum(jnp.sum(jnp.where(cell_max >= trial, 1.0, 0.0), axis=0), axis=0, keepdims=True)
        return jnp.where(cnt >= top_k, trial, c)

    lo_key = lax.fori_loop(0, DSA_CELL_BITS, cell_bisect, jnp.full((1, q), INT_MIN, jnp.int32))
    base = lo_key
    span = hi_key - lo_key
    nbits = jnp.max(32 - lax.clz(span))

    def bisect(b, state):
        off, cnt, above = state
        trial_off = off + lax.shift_left(jnp.int32(1), nbits - 1 - b)
        in_span = (trial_off ^ INT_MIN) <= (span ^ INT_MIN)
        c = jnp.where(in_span, count_ge(base + trial_off), 0)
        ok = c >= top_k
        return jnp.where(ok, trial_off, off), jnp.where(ok, c, cnt), jnp.where(ok, above, c)

    uncounted = jnp.full((1, q), -1, jnp.int32)
    off, cnt_thr, cnt_above = lax.fori_loop(0, nbits, bisect, (jnp.zeros((1, q), jnp.int32), uncounted, uncounted))
    thr = base + off
    cnt_thr = lax.cond(jnp.min(cnt_thr) < 0, lambda: count_ge(thr), lambda: cnt_thr)
    tied = jnp.logical_and(cnt_thr > top_k, thr > NEG_INF_KEY)
    any_tie = jnp.max(jnp.where(tied, 1, 0)) > 0

    thr_valid = jnp.maximum(thr, NEG_INF_KEY + 1)

    def mask_fast():
        def body(rows):
            slab_ref[rows, :] = jnp.where(slab_ref[rows, :] >= thr_valid, 0, NEG_INF_BITS)

        lax.fori_loop(0, nwide, lambda jw, c: body(wide_rows(jw)), None)
        lax.fori_loop(n0, nkb, lambda j, c: body(narrow_rows(j)), None)

    def mask_tied():
        above = lax.cond(jnp.min(cnt_above) < 0, lambda: count_ge(thr + 1), lambda: cnt_above)
        thr_c = jnp.maximum(thr, NEG_INF_KEY)
        need = jnp.where(thr_c > NEG_INF_KEY, (top_k - above).astype(F32), -1.0)

        def body(rows, n, seen):
            for g in range(n // kb):
                blk = pl.ds(rows.start + g * kb, kb)
                key = slab_ref[blk, :]
                eq = key == thr_c
                eq_f = jnp.where(eq, 1.0, 0.0)
                before = jnp.dot(tri_ref[...], eq_f.astype(BF16), preferred_element_type=F32)
                tie_bits = jnp.where(seen + before < need, 0, NEG_INF_BITS)
                slab_ref[blk, :] = jnp.where(key > thr_c, 0, jnp.where(eq, tie_bits, NEG_INF_BITS))
                seen = seen + jnp.sum(_fold_keys(eq_f, jnp.sum), axis=0, keepdims=True)
            return seen

        seen = lax.fori_loop(0, nwide // 2, lambda jp, c: body(pair_rows(jp), 2 * wide, c), jnp.zeros((1, q), F32))
        seen = lax.fori_loop((nwide // 2) * 2, nwide, lambda jw, c: body(wide_rows(jw), wide, c), seen)
        lax.fori_loop(n0, nkb, lambda j, c: body(narrow_rows(j), kb, c), seen)

    lax.cond(any_tie, mask_tied, mask_fast)

    m_ref[...] = jnp.full(m_ref.shape, M_INIT, F32)
    l_ref[...] = jnp.zeros(l_ref.shape, F32)
    acc_ref[...] = jnp.zeros(acc_ref.shape, F32)

    def logits(rows, n, buf, rel):
        mask = lax.bitcast_convert_type(slab_ref[rows, :], F32)
        mask2 = jnp.concatenate([mask, mask], axis=1)
        tile_max = []
        for p in range(npair):
            ls = slice(p * LANES, (p + 1) * LANES)
            w_q = jnp.concatenate([qat_ref[(2 * p) * LANES:(2 * p + 1) * LANES, :],
                                   qat_ref[(2 * p + 1) * LANES:(2 * p + 2) * LANES, :]], axis=1)
            s = jnp.dot(ka_ref[rows, ls], w_q, preferred_element_type=F32) + mask2
            if rel is not None:
                s = s + btab_ref[rel, p]
            s_ref[buf, p, 0:n, :] = s
            tile_max.append(_fold_keys(s, jnp.max))
        return tuple(tile_max)

    def attend(rows, n, buf, tile_max):
        for p in range(npair):
            ls = slice(p * LANES, (p + 1) * LANES)
            m_old = m_ref[p]
            m_new = jnp.maximum(m_old, jnp.max(tile_max[p], axis=0, keepdims=True))
            alpha = jnp.exp2(m_old - m_new)
            pexp = jnp.exp2(s_ref[buf, p, 0:n, :] - m_new[0:1])
            m_ref[p] = m_new
            l_ref[p] = alpha * l_ref[p] + _fold_keys(pexp, jnp.sum)
            acc_ref[p] = alpha[0:1] * acc_ref[p] + jnp.dot(vat_ref[ls, rows], pexp.astype(BF16),
                                                           preferred_element_type=F32)

    @pl.when(nwide > 0)
    def _():
        pipelined(0, nwide, lambda jw, buf: logits(wide_rows(jw), wide, buf, None),
                  lambda jw, buf, tile_max: attend(wide_rows(jw), wide, buf, tile_max))

    pipelined(n0, nkb, lambda j, buf: logits(narrow_rows(j), kb, buf, jnp.minimum(i - j, 2)),
              lambda j, buf, tile_max: attend(narrow_rows(j), kb, buf, tile_max))

    for p in range(npair):
        o_t = acc_ref[p] / jnp.sum(l_ref[p], axis=0, keepdims=True)
        o_pair = jnp.concatenate([o_t[:A_HEAD_DIM, :q], o_t[A_HEAD_DIM:, q:]], axis=0).T
        ls = slice(p * LANES, (p + 1) * LANES)
        o_ref[:, ls] = (o_pair * _silu(ga_ref[:, ls])).astype(o_ref.dtype)


def _dsa_bias_table(rel_bias):
    ii = np.arange(DSA_Q)[:, None]
    jj = np.arange(DSA_KB)[None, :]
    buckets = np.stack([_t5_bucket_np(ii - jj), _t5_bucket_np(DSA_KB + ii - jj),
                        np.full((DSA_Q, DSA_KB), REL_BUCKETS - 1)])
    assert (_t5_bucket_np(np.arange(DSA_KB + 1, 4 * DSA_KB)) == REL_BUCKETS - 1).all()
    rb = rel_bias.astype(F32) * LOG2E
    rb = rb - rb[REL_BUCKETS - 1]
    onehot = np.zeros((3, DSA_KB, DSA_Q, REL_BUCKETS), np.float32)
    np.put_along_axis(onehot, np.transpose(buckets, (0, 2, 1))[..., None], 1.0, axis=-1)
    tab = jnp.einsum('tsrb,bh->thsr', jnp.asarray(onehot), rb, precision=lax.Precision.HIGHEST)
    tab = tab.reshape(3, A_HEADS // 2, 2, DSA_KB, DSA_Q)
    return jnp.transpose(tab, (0, 1, 3, 2, 4)).reshape(3, A_HEADS // 2, DSA_KB, 2 * DSA_Q)


def _dsa(qa_t, qi_t, wi_t, ga, ka, va_t, ki, btab, top_k):
    s = ga.shape[0]
    q, kb = DSA_Q, DSA_KB
    tri = jnp.asarray(np.tril(np.ones((kb, kb), np.float32), -1), BF16)
    whole = lambda shape: pl.BlockSpec(shape, lambda i: (0,) * len(shape), pipeline_mode=pl.Buffered(1))
    return pl.pallas_call(
        functools.partial(_dsa_kernel, top_k=top_k),
        grid=(s // q,),
        in_specs=[pl.BlockSpec((A_HEADS * LANES, q), lambda i: (0, i)),
                  pl.BlockSpec((IDX_HEADS * LANES, q), lambda i: (0, i)),
                  pl.BlockSpec((SUBLANES, q), lambda i: (0, i)),
                  pl.BlockSpec((q, A_W), lambda i: (i, 0)),
                  whole((s, A_W)), whole((A_W, s)), whole((s, LANES)),
                  whole((3, A_HEADS // 2, kb, 2 * q)), whole((kb, kb))],
        out_specs=pl.BlockSpec((q, A_W), lambda i: (i, 0)),
        out_shape=jax.ShapeDtypeStruct((s, A_W), BF16),
        scratch_shapes=[pltpu.VMEM((s, q), jnp.int32),
                        pltpu.VMEM((DSA_CELLS, SUBLANES, q), jnp.int32),
                        pltpu.VMEM((2, A_HEADS // 2, DSA_WIDE * kb, 2 * q), F32),
                        pltpu.VMEM((A_HEADS // 2, LANES, 2 * q), F32),
                        pltpu.VMEM((A_HEADS // 2, SUBLANES, 2 * q), F32),
                        pltpu.VMEM((A_HEADS // 2, SUBLANES, 2 * q), F32)],
        compiler_params=pltpu.CompilerParams(dimension_semantics=("arbitrary",), vmem_limit_bytes=VMEM_LIMIT),
        name="dsa",
    )(qa_t, qi_t, wi_t, ga, ka, va_t, ki, btab, tri)


GLA_BLOCK = 512


def _gla_kernel(qb_ref, kb_ref, vb_ref, gb_ref, lr_ref, wlr_ref, blr_ref, gn_ref, tril_ref, o_ref,
                state_ref, b_ref, att_ref):
    c, sub = GLA_CHUNK, GLA_SUB
    nsub = c // sub

    @pl.when(pl.program_id(0) == 0)
    def _():
        state_ref[...] = jnp.zeros(state_ref.shape, F32)

    dn_t = (((1,), (1,)), ((), ()))
    dn_0 = (((0,), (0,)), ((), ()))
    hp = lax.Precision.HIGHEST
    row_i = lax.broadcasted_iota(jnp.int32, (sub, sub), 0)
    col_j = lax.broadcasted_iota(jnp.int32, (sub, sub), 1)

    def chunk(ci, carry):
        r0 = pl.multiple_of(ci * c, c)
        z = jnp.dot(lr_ref[pl.ds(r0, c), :], wlr_ref[...], precision=hp, preferred_element_type=F32) + blr_ref[...]
        log_a = (jnp.minimum(z, 0.0) - jnp.log1p(jnp.exp(-jnp.abs(z)))) * (LOG2E / GLA_TAU)
        b_ref[...] = jnp.dot(tril_ref[...], log_a, precision=hp, preferred_element_type=F32)
        for h in range(B_HEADS):
            ls = slice(h * LANES, (h + 1) * LANES)
            b = b_ref[:, ls]
            qh = qb_ref[pl.ds(r0, c), ls]
            kh = kb_ref[pl.ds(r0, c), ls]
            vh = vb_ref[pl.ds(r0, c), ls]
            state_t = state_ref[h]
            b_last = b[c - 1:c, :]
            o = lax.dot_general((qh * jnp.exp2(b)).astype(BF16), state_t.astype(BF16), dn_t,
                                preferred_element_type=F32)
            for si in range(nsub):
                rs = slice(si * sub, (si + 1) * sub)
                b_i, q_i = b[rs], qh[rs]
                if si > 0:
                    ref_row = b[si * sub:si * sub + 1, :]
                    q_t = (q_i * jnp.exp2(b_i - ref_row)).astype(BF16)
                    k_t = (kh[:si * sub] * jnp.exp2(ref_row - b[:si * sub])).astype(BF16)
                    att_ref[rs, :si * sub] = lax.dot_general(q_t, k_t, dn_t, preferred_element_type=F32)
                k_i = kh[rs]
                diag = jnp.zeros((sub, sub), F32)
                for jl in range(sub):
                    diff = jnp.minimum(b_i - b_i[jl:jl + 1, :], 0.0)
                    col = jnp.sum(q_i * k_i[jl:jl + 1, :] * jnp.exp2(diff), axis=-1, keepdims=True)
                    diag = jnp.where(col_j == jl, col, diag)
                att_ref[rs, rs] = jnp.where(col_j <= row_i, diag, 0.0)
                if si + 1 < nsub:
                    att_ref[rs, (si + 1) * sub:] = jnp.zeros((sub, c - (si + 1) * sub), F32)
            o = o + jnp.dot(att_ref[...].astype(BF16), vh.astype(BF16), preferred_element_type=F32)
            k_dec = (kh * jnp.exp2(b_last - b)).astype(BF16)
            state_ref[h] = (jnp.exp2(b_last) * state_t
                            + lax.dot_general(vh.astype(BF16), k_dec, dn_0, preferred_element_type=F32))
            o = o * lax.rsqrt(jnp.mean(o * o, axis=-1, keepdims=True) + EPS) * gn_ref[...]
            o_ref[pl.ds(r0, c), ls] = (o * _silu(gb_ref[pl.ds(r0, c), ls])).astype(o_ref.dtype)
        return carry

    lax.fori_loop(0, GLA_BLOCK // c, chunk, 0)


def _gla(qb, kb, vb, gb, lr, wlr_pad, blr_pad, gn):
    s = qb.shape[0]
    blk = min(GLA_BLOCK, s)
    assert blk == GLA_BLOCK and s % blk == 0
    tril = jnp.asarray(np.tril(np.ones((GLA_CHUNK, GLA_CHUNK), np.float32)))
    row = lambda wd: pl.BlockSpec((blk, wd), lambda i: (i, 0))
    const = lambda shape: pl.BlockSpec(shape, lambda i: (0,) * len(shape))
    return pl.pallas_call(
        _gla_kernel,
        grid=(s // blk,),
        in_specs=[row(B_HEADS * LANES), row(B_HEADS * LANES), row(B_VW), row(B_VW), row(LANES),
                  const((LANES, B_HEADS * LANES)), const((1, B_HEADS * LANES)), const((1, B_DV)),
                  const((GLA_CHUNK, GLA_CHUNK))],
        out_specs=row(B_VW),
        out_shape=jax.ShapeDtypeStruct((s, B_VW), BF16),
        scratch_shapes=[pltpu.VMEM((B_HEADS, LANES, B_DV), F32),
                        pltpu.VMEM((GLA_CHUNK, B_HEADS * LANES), F32),
                        pltpu.VMEM((GLA_CHUNK, GLA_CHUNK), F32)],
        compiler_params=pltpu.CompilerParams(dimension_semantics=("arbitrary",), vmem_limit_bytes=VMEM_LIMIT),
        name="gla",
    )(qb, kb, vb, gb, lr, wlr_pad, blr_pad, gn, tril)


def _mid_kernel(x_ref, ma_ref, mb_ref, wo_ref, g_ref, wi_ref, x1_ref, u_ref, sg_ref):
    x1 = (x_ref[...]
          + jnp.dot(ma_ref[...], wo_ref[:A_W, :], preferred_element_type=F32)
          + jnp.dot(mb_ref[...], wo_ref[A_W:, :], preferred_element_type=F32))
    x1_ref[...] = x1
    h = (x1 * lax.rsqrt(jnp.mean(x1 * x1, axis=-1, keepdims=True) + EPS) * g_ref[...]).astype(BF16)
    val = jnp.dot(h, wi_ref[:, :D_MODEL], preferred_element_type=F32)
    glu = jnp.dot(h, wi_ref[:, D_MODEL:2 * D_MODEL], preferred_element_type=F32)
    gate = jnp.dot(h, wi_ref[:, 2 * D_MODEL:], preferred_element_type=F32)
    u_ref[...] = val * jax.nn.sigmoid(glu)
    sg_ref[...] = _silu(gate)


def _mid(x2d, mix_a, mix_b, w_out, g, w_in, tm):
    s = x2d.shape[0]
    row = lambda wd: pl.BlockSpec((tm, wd), lambda i: (i, 0))
    const = lambda shape: pl.BlockSpec(shape, lambda i: (0,) * len(shape))
    return pl.pallas_call(
        _mid_kernel,
        grid=(s // tm,),
        in_specs=[row(D_MODEL), row(A_W), row(B_VW), const((A_W + B_VW, D_MODEL)), const((1, D_MODEL)),
                  const((D_MODEL, 3 * D_MODEL))],
        out_specs=[row(D_MODEL), row(D_MODEL), row(D_MODEL)],
        out_shape=[jax.ShapeDtypeStruct((s, D_MODEL), F32)] * 3,
        compiler_params=pltpu.CompilerParams(dimension_semantics=("arbitrary",), vmem_limit_bytes=VMEM_LIMIT),
        name="mid",
    )(x2d, mix_a, mix_b, w_out, g, w_in)


HALO = 32
CONV_ROWS = 128


def _odd_out_kernel(u_ref, halo_ref, sg_ref, x1_ref, cw_ref, cb_ref, lg_ref, lb_ref, wo_ref, nf_ref, o_ref,
                    ubuf_ref, y_ref, shift_ref, *, tm):
    i = pl.program_id(0)
    ubuf_ref[HALO:, :] = u_ref[...]
    ubuf_ref[:HALO, :] = jnp.where(i > 0, halo_ref[...], 0.0)
    base = HALO - (CONV_WIDTH - 1)
    for cs in range(D_MODEL // LANES):
        ls = slice(cs * LANES, (cs + 1) * LANES)
        for r0 in range(0, tm, CONV_ROWS):
            acc = jnp.zeros((CONV_ROWS, LANES), F32)
            for res in range(SUBLANES):
                taps = [k for k in range(CONV_WIDTH) if (base + k) % SUBLANES == res]
                first = base + taps[0]
                rows = taps[-1] - taps[0] + CONV_ROWS
                shift_ref[0:rows, :] = ubuf_ref[r0 + first:r0 + first + rows, ls]
                aligned = shift_ref[0:rows, :]
                for k in taps:
                    a = base + k - first
                    acc = acc + aligned[a:a + CONV_ROWS] * cw_ref[k:k + 1, ls]
            y_ref[r0:r0 + CONV_ROWS, ls] = acc + cb_ref[:, ls]
    y = y_ref[...]
    mu = jnp.mean(y, axis=-1, keepdims=True)
    yc = y - mu
    var = jnp.mean(yc * yc, axis=-1, keepdims=True)
    yn = yc * lax.rsqrt(var + EPS) * lg_ref[...] + lb_ref[...]
    mixed = (_silu(yn) * sg_ref[...]).astype(BF16)
    x2 = x1_ref[...] + jnp.dot(mixed, wo_ref[...], preferred_element_type=F32)
    o_ref[...] = x2 * lax.rsqrt(jnp.mean(x2 * x2, axis=-1, keepdims=True) + EPS) * nf_ref[...]


def _odd_out(u, sg, x1, cw_pad, cb, lg, lb, w_out, nf, tm):
    s = u.shape[0]
    row = pl.BlockSpec((tm, D_MODEL), lambda i: (i, 0))
    const = lambda shape: pl.BlockSpec(shape, lambda i: (0,) * len(shape))
    halo = pl.BlockSpec((HALO, D_MODEL), lambda i: (jnp.maximum(i * (tm // HALO) - 1, 0), 0))
    return pl.pallas_call(
        functools.partial(_odd_out_kernel, tm=tm),
        grid=(s // tm,),
        in_specs=[row, halo, row, row, const((HALO, D_MODEL)), const((1, D_MODEL)), const((1, D_MODEL)),
                  const((1, D_MODEL)), const((D_MODEL, D_MODEL)), const((1, D_MODEL))],
        out_specs=row,
        out_shape=jax.ShapeDtypeStruct((s, D_MODEL), F32),
        scratch_shapes=[pltpu.VMEM((HALO + tm, D_MODEL), F32), pltpu.VMEM((tm, D_MODEL), F32),
                        pltpu.VMEM((HALO + CONV_ROWS, LANES), F32)],
        compiler_params=pltpu.CompilerParams(dimension_semantics=("arbitrary",), vmem_limit_bytes=VMEM_LIMIT),
        name="odd_out",
    )(u, u, sg, x1, cw_pad, cb, lg, lb, w_out, nf)


def kernel(x, rel_bias, norm_even, w_in_even, w_gla_lr, b_gla_lr, gla_norm, w_out_even, norm_odd, w_in_odd,
           conv_w, conv_b, conv_ln_g, conv_ln_b, w_out_odd, norm_f):
    bsz, s, d = x.shape
    assert bsz == 1 and d == D_MODEL and s % GLA_BLOCK == 0
    assert norm_even.shape[0] == 1 and norm_odd.shape[0] == 1
    top_k = min(TOPK_MAX, s // 4)
    tm = 512
    x2d = x.reshape(s, d)

    wt_pad = _even_in_weights(w_in_even[0])
    qa_t, ka, va_t, ga, qi_t, ki, wi_t, qb, kb, vb, gb, lr = _even_in(x2d, norm_even[0].reshape(1, d), wt_pad, tm)
    mix_a = _dsa(qa_t, qi_t, wi_t, ga, ka, va_t, ki, _dsa_bias_table(rel_bias), top_k)

    head_pad = ((0, 0), (0, 0), (0, LANES - B_DK))
    wlr_pad = jnp.pad(w_gla_lr[0].reshape(GLA_RANK, B_HEADS, B_DK), head_pad).reshape(GLA_RANK, B_HEADS * LANES)
    wlr_pad = jnp.pad(wlr_pad, ((0, LANES - GLA_RANK), (0, 0)))
    blr_pad = jnp.pad(b_gla_lr[0].reshape(1, B_HEADS, B_DK), head_pad).reshape(1, B_HEADS * LANES)
    mix_b = _gla(qb, kb, vb, gb, lr, wlr_pad, blr_pad, gla_norm[0].reshape(1, B_DV))

    x1, u, sg = _mid(x2d, mix_a, mix_b, w_out_even[0].astype(BF16), norm_odd[0].reshape(1, d),
                     w_in_odd[0].astype(BF16), tm)

    cw_pad = jnp.zeros((HALO, d), F32).at[:CONV_WIDTH].set(conv_w[0])
    out = _odd_out(u, sg, x1, cw_pad, conv_b[0].reshape(1, d), conv_ln_g[0].reshape(1, d),
                   conv_ln_b[0].reshape(1, d), w_out_odd[0].astype(BF16), norm_f.reshape(1, d), tm)
    return out.reshape(bsz, s, d)
```

```python
import functools
import math

import jax
import jax.numpy as jnp
import numpy as np
from jax import lax
from jax.experimental import pallas as pl
from jax.experimental.pallas import tpu as pltpu

D_MODEL = 1024
A_HEADS = 8
A_HEAD_DIM = 64
IDX_HEADS = 4
IDX_DIM = 64
TOPK_MAX = 256
B_HEADS = 4
B_DK = 64
B_DV = 128
GLA_RANK = 16
GLA_TAU = 16.0
GLA_CHUNK = 64
GLA_SUB = 16
CONV_WIDTH = 31
REL_BUCKETS = 32
REL_MAX_EXACT = 16
REL_MAX_DIST = 128
EPS = 1e-6

A_W = A_HEADS * A_HEAD_DIM
B_KW = B_HEADS * B_DK
B_VW = B_HEADS * B_DV

LANES = 128
SUBLANES = 8
VMEM_LIMIT = 56 * 1024 * 1024
ROW_BLOCK = 512

F32 = jnp.float32
BF16 = jnp.bfloat16
NEG_INF = float("-inf")
INT_MIN = -(2 ** 31)
NEG_INF_BITS = int(np.array(-np.inf, np.float32).view(np.int32))
NEG_INF_KEY = NEG_INF_BITS ^ 0x7FFFFFFF
M_INIT = -1e30
LOG2E = math.log2(math.e)

C_QA = 0
C_KA = C_QA + A_HEADS * LANES
C_VA = C_KA + A_W
C_GA = C_VA + A_W
C_QI = C_GA + A_W
C_KI = C_QI + IDX_HEADS * LANES
C_WI = C_KI + LANES
C_QB = C_WI + LANES
C_KB = C_QB + B_HEADS * LANES
C_VB = C_KB + B_HEADS * LANES
C_GB = C_VB + B_VW
C_LR = C_GB + B_VW
C_END = C_LR + LANES


def _t5_bucket_np(dist):
    dist = np.maximum(dist, 0)
    d = np.maximum(dist, 1).astype(np.float32)
    large = REL_MAX_EXACT + (np.log(d / REL_MAX_EXACT) / math.log(REL_MAX_DIST / REL_MAX_EXACT)
                             * (REL_BUCKETS - REL_MAX_EXACT)).astype(np.int32)
    large = np.minimum(large, REL_BUCKETS - 1)
    return np.where(dist < REL_MAX_EXACT, dist, large)


def _silu(x):
    return x * jax.nn.sigmoid(x)


def _even_in_kernel(x_ref, g_ref, wt_ref, qat_ref, ka_ref, vat_ref, ga_ref, qit_ref, ki_ref, wit_ref,
                    qb_ref, kb_ref, vb_ref, gb_ref, lr_ref):
    x = x_ref[...]
    h = (x * lax.rsqrt(jnp.mean(x * x, axis=-1, keepdims=True) + EPS) * g_ref[...]).astype(BF16)
    dn_t = (((1,), (1,)), ((), ()))

    def proj(lo, hi):
        return lax.dot_general(h, wt_ref[lo:hi, :], dn_t, preferred_element_type=F32)

    def proj_t(lo, hi):
        return lax.dot_general(wt_ref[lo:hi, :], h, dn_t, preferred_element_type=F32)

    qat_ref[...] = proj_t(C_QA, C_KA).astype(BF16)
    ka_ref[...] = proj(C_KA, C_VA).astype(BF16)
    vat_ref[...] = proj_t(C_VA, C_GA).astype(BF16)
    ga_ref[...] = proj(C_GA, C_QI)
    qit_ref[...] = proj_t(C_QI, C_KI).astype(BF16)
    ki_ref[...] = proj(C_KI, C_WI).astype(BF16)
    wit_ref[...] = proj_t(C_WI, C_WI + SUBLANES)
    qb_ref[...] = proj(C_QB, C_KB)
    kb_ref[...] = proj(C_KB, C_VB)
    vb_ref[...] = proj(C_VB, C_GB)
    gb_ref[...] = proj(C_GB, C_LR)
    lr_ref[...] = proj(C_LR, C_END)


def _even_in_layout():
    cols = np.cumsum([0, A_W, A_W, A_W, A_W, IDX_HEADS * IDX_DIM, IDX_DIM, IDX_HEADS,
                      B_KW, B_KW, B_VW, B_VW, GLA_RANK])
    (s_qa, s_ka, s_va, s_ga, s_qi, s_ki, s_wi, s_qb, s_kb, s_vb, s_gb, s_lr, _) = [int(c) for c in cols]
    src = np.full((C_END,), -1, np.int32)
    scale = np.ones((C_END,), np.float32)

    def put(dst, start, width, sc=1.0):
        src[dst:dst + width] = np.arange(start, start + width)
        scale[dst:dst + width] = sc

    for h in range(A_HEADS):
        put(C_QA + h * LANES + (h % 2) * A_HEAD_DIM, s_qa + h * A_HEAD_DIM, A_HEAD_DIM, (A_HEAD_DIM ** -0.5) * LOG2E)
    put(C_KA, s_ka, A_W)
    put(C_VA, s_va, A_W)
    put(C_GA, s_ga, A_W)
    for h in range(IDX_HEADS):
        put(C_QI + h * LANES, s_qi + h * IDX_DIM, IDX_DIM)
    put(C_KI, s_ki, IDX_DIM)
    put(C_WI, s_wi, IDX_HEADS, (IDX_DIM ** -0.5) * (IDX_HEADS ** -0.5))
    for h in range(B_HEADS):
        put(C_QB + h * LANES, s_qb + h * B_DK, B_DK, B_DK ** -0.5)
        put(C_KB + h * LANES, s_kb + h * B_DK, B_DK)
    put(C_VB, s_vb, B_VW)
    put(C_GB, s_gb, B_VW)
    put(C_LR, s_lr, GLA_RANK)
    return src, scale


def _even_in_weights(w):
    src, scale = _even_in_layout()
    pieces, start = [], 0
    for end in range(1, C_END + 1):
        if end == C_END or (src[end] >= 0) != (src[start] >= 0) or (src[end] >= 0 and src[end] != src[end - 1] + 1):
            if src[start] >= 0:
                pieces.append(w[:, int(src[start]):int(src[end - 1]) + 1])
            else:
                pieces.append(jnp.zeros((D_MODEL, end - start), w.dtype))
            start = end
    w_pad = jnp.concatenate(pieces, axis=1) * jnp.asarray(scale)[None, :]
    return w_pad.T.astype(BF16)


def _even_in(x2d, g, wt_pad, tm):
    s = x2d.shape[0]
    row = lambda wd, dt: (pl.BlockSpec((tm, wd), lambda i: (i, 0)), jax.ShapeDtypeStruct((s, wd), dt))
    col = lambda ht, dt: (pl.BlockSpec((ht, tm), lambda i: (0, i)), jax.ShapeDtypeStruct((ht, s), dt))
    outs = [col(C_KA - C_QA, BF16), row(A_W, BF16), col(A_W, BF16), row(A_W, F32), col(C_KI - C_QI, BF16),
            row(LANES, BF16), col(SUBLANES, F32), row(C_KB - C_QB, F32), row(C_VB - C_KB, F32), row(B_VW, F32),
            row(B_VW, F32), row(LANES, F32)]
    return pl.pallas_call(
        _even_in_kernel,
        grid=(s // tm,),
        in_specs=[pl.BlockSpec((tm, D_MODEL), lambda i: (i, 0)),
                  pl.BlockSpec((1, D_MODEL), lambda i: (0, 0)),
                  pl.BlockSpec((C_END, D_MODEL), lambda i: (0, 0))],
        out_specs=[spec for spec, _ in outs],
        out_shape=[shape for _, shape in outs],
        compiler_params=pltpu.CompilerParams(dimension_semantics=("arbitrary",), vmem_limit_bytes=VMEM_LIMIT),
        name="even_in",
    )(x2d, g, wt_pad)


DSA_Q = 128
DSA_KB = 128
DSA_WIDE = 4
DSA_NACC = 4
DSA_CELLS = 64
DSA_CELL_BITS = 14


def _f32_key(x):
    bits = lax.bitcast_convert_type(x, jnp.int32)
    return bits ^ (lax.shift_right_arithmetic(bits, 31) & 0x7FFFFFFF)


def _fold_keys(x, op):
    n, w = x.shape
    group = DSA_NACC * SUBLANES
    if n % group == 0 and n > group:
        x = op(x.reshape(n // group, DSA_NACC, SUBLANES, w), axis=0)
        return op(x, axis=0)
    return op(x.reshape(n // SUBLANES, SUBLANES, w), axis=0)


def _dsa_kernel(qat_ref, qit_ref, wit_ref, ga_ref, ka_ref, vat_ref, ki_ref, btab_ref, tri_ref, o_ref,
                slab_ref, cell_ref, s_ref, acc_ref, m_ref, l_ref, *, top_k):
    q, kb, wide = DSA_Q, DSA_KB, DSA_WIDE * DSA_KB
    i = pl.program_id(0)
    nkb = i + 1
    nwide = jnp.maximum(i - 1, 0) // DSA_WIDE
    n0 = nwide * DSA_WIDE
    t_row = i * q + lax.broadcasted_iota(jnp.int32, (1, q), 1)
    key_off = lax.broadcasted_iota(jnp.int32, (kb, 1), 0)
    npair = A_HEADS // 2

    def wide_rows(jw):
        return pl.ds(pl.multiple_of(jw * wide, wide), wide)

    def narrow_rows(j):
        return pl.ds(pl.multiple_of(j * kb, kb), kb)

    def pipelined(first, last, produce, consume):
        def body(u, carry):
            t = first + 2 * u
            consume(t, 0, carry)
            consume(t + 1, 1, produce(t + 1, 1))
            return produce(t + 2, 0)

        npairs = (last - first - 1) // 2
        carry = lax.fori_loop(0, npairs, body, produce(first, 0))
        t = first + 2 * npairs
        consume(t, 0, carry)

        @pl.when(t + 1 < last)
        def _():
            consume(t + 1, 1, produce(t + 1, 1))

    w_head = [wit_ref[h:h + 1, :] for h in range(IDX_HEADS)]

    def scores(rows):
        k_idx = ki_ref[rows, :]
        sc = None
        for h in range(IDX_HEADS):
            s = jnp.dot(k_idx, qit_ref[h * LANES:(h + 1) * LANES, :], preferred_element_type=F32)
            term = w_head[h] * jnp.maximum(s, 0.0)
            sc = term if sc is None else sc + term
        return sc

    assert wide % (DSA_CELLS * SUBLANES) == 0
    cell_ref[...] = jnp.full(cell_ref.shape, INT_MIN, jnp.int32)

    def score_rows(rows, n):
        key = _f32_key(scores(rows))
        slab_ref[rows, :] = key
        cells = jnp.max(key.reshape(n // (DSA_CELLS * SUBLANES), DSA_CELLS, SUBLANES, q), axis=0)
        cell_ref[...] = jnp.maximum(cell_ref[...], cells)

    def pair_rows(jp):
        return pl.ds(pl.multiple_of(jp * (2 * wide), 2 * wide), 2 * wide)

    def score_narrow(j, c):
        rows = narrow_rows(j)
        key = _f32_key(jnp.where(j * kb + key_off <= t_row, scores(rows), NEG_INF))
        slab_ref[rows, :] = key
        ncls = kb // SUBLANES
        cell_ref[0:ncls] = jnp.maximum(cell_ref[0:ncls], key.reshape(ncls, SUBLANES, q))
        return c

    lax.fori_loop(0, nwide // 2, lambda jp, c: score_rows(pair_rows(jp), 2 * wide), None)
    lax.fori_loop((nwide // 2) * 2, nwide, lambda jw, c: score_rows(wide_rows(jw), wide), None)
    lax.fori_loop(n0, nkb, score_narrow, 0)

    def count_ge(trial):
        def cnt(blk):
            return _fold_keys(jnp.where(blk >= trial, 1.0, 0.0), jnp.sum)

        ncw = nkb // DSA_WIDE
        acc = lax.fori_loop(0, ncw // 2, lambda jp, a: a + cnt(slab_ref[pair_rows(jp), :]),
                            jnp.zeros((SUBLANES, q), F32))
        acc = lax.fori_loop((ncw // 2) * 2, ncw, lambda jw, a: a + cnt(slab_ref[wide_rows(jw), :]), acc)
        acc = lax.fori_loop(ncw * DSA_WIDE, nkb, lambda j, a: a + cnt(slab_ref[narrow_rows(j), :]), acc)
        return jnp.sum(acc, axis=0, keepdims=True).astype(jnp.int32)

    assert DSA_CELLS * SUBLANES >= top_k
    cell_max = cell_ref[...]
    hi_key = jnp.max(jnp.max(cell_max, axis=0), axis=0, keepdims=True)

    def cell_bisect(b, c):
        trial = c + lax.shift_left(jnp.int32(1), 31 - b)
        cnt = jnp.sum(jnp.sum(jnp.where(cell_max >= trial, 1.0, 0.0), axis=0), axis=0, keepdims=True)
        return jnp.where(cnt >= top_k, trial, c)

    lo_key = lax.fori_loop(0, DSA_CELL_BITS, cell_bisect, jnp.full((1, q), INT_MIN, jnp.int32))
    base = lo_key
    span = hi_key - lo_key
    nbits = jnp.max(32 - lax.clz(span))

    def bisect(b, state):
        off, cnt, above = state
        trial_off = off + lax.shift_left(jnp.int32(1), nbits - 1 - b)
        in_span = (trial_off ^ INT_MIN) <= (span ^ INT_MIN)
        c = jnp.where(in_span, count_ge(base + trial_off), 0)
        ok = c >= top_k
        return jnp.where(ok, trial_off, off), jnp.where(ok, c, cnt), jnp.where(ok, above, c)

    uncounted = jnp.full((1, q), -1, jnp.int32)
    off, cnt_thr, cnt_above = lax.fori_loop(0, nbits, bisect, (jnp.zeros((1, q), jnp.int32), uncounted, uncounted))
    thr = base + off
    cnt_thr = lax.cond(jnp.min(cnt_thr) < 0, lambda: count_ge(thr), lambda: cnt_thr)
    tied = jnp.logical_and(cnt_thr > top_k, thr > NEG_INF_KEY)
    any_tie = jnp.max(jnp.where(tied, 1, 0)) > 0

    thr_valid = jnp.maximum(thr, NEG_INF_KEY + 1)

    def mask_fast():
        def body(rows):
            slab_ref[rows, :] = jnp.where(slab_ref[rows, :] >= thr_valid, 0, NEG_INF_BITS)

        lax.fori_loop(0, nwide, lambda jw, c: body(wide_rows(jw)), None)
        lax.fori_loop(n0, nkb, lambda j, c: body(narrow_rows(j)), None)

    def mask_tied():
        above = lax.cond(jnp.min(cnt_above) < 0, lambda: count_ge(thr + 1), lambda: cnt_above)
        thr_c = jnp.maximum(thr, NEG_INF_KEY)
        need = jnp.where(thr_c > NEG_INF_KEY, (top_k - above).astype(F32), -1.0)

        def body(rows, n, seen):
            for g in range(n // kb):
                blk = pl.ds(rows.start + g * kb, kb)
                key = slab_ref[blk, :]
                eq = key == thr_c
                eq_f = jnp.where(eq, 1.0, 0.0)
                before = jnp.dot(tri_ref[...], eq_f.astype(BF16), preferred_element_type=F32)
                tie_bits = jnp.where(seen + before < need, 0, NEG_INF_BITS)
                slab_ref[blk, :] = jnp.where(key > thr_c, 0, jnp.where(eq, tie_bits, NEG_INF_BITS))
                seen = seen + jnp.sum(_fold_keys(eq_f, jnp.sum), axis=0, keepdims=True)
            return seen

        seen = lax.fori_loop(0, nwide // 2, lambda jp, c: body(pair_rows(jp), 2 * wide, c), jnp.zeros((1, q), F32))
        seen = lax.fori_loop((nwide // 2) * 2, nwide, lambda jw, c: body(wide_rows(jw), wide, c), seen)
        lax.fori_loop(n0, nkb, lambda j, c: body(narrow_rows(j), kb, c), seen)

    lax.cond(any_tie, mask_tied, mask_fast)

    m_ref[...] = jnp.full(m_ref.shape, M_INIT, F32)
    l_ref[...] = jnp.zeros(l_ref.shape, F32)
    acc_ref[...] = jnp.zeros(acc_ref.shape, F32)

    def logits(rows, n, buf, rel):
        mask = lax.bitcast_convert_type(slab_ref[rows, :], F32)
        mask2 = jnp.concatenate([mask, mask], axis=1)
        tile_max = []
        for p in range(npair):
            ls = slice(p * LANES, (p + 1) * LANES)
            w_q = jnp.concatenate([qat_ref[(2 * p) * LANES:(2 * p + 1) * LANES, :],
                                   qat_ref[(2 * p + 1) * LANES:(2 * p + 2) * LANES, :]], axis=1)
            s = jnp.dot(ka_ref[rows, ls], w_q, preferred_element_type=F32) + mask2
            if rel is not None:
                s = s + btab_ref[rel, p]
            s_ref[buf, p, 0:n, :] = s
            tile_max.append(_fold_keys(s, jnp.max))
        return tuple(tile_max)

    def attend(rows, n, buf, tile_max):
        for p in range(npair):
            ls = slice(p * LANES, (p + 1) * LANES)
            m_old = m_ref[p]
            m_new = jnp.maximum(m_old, jnp.max(tile_max[p], axis=0, keepdims=True))
            alpha = jnp.exp2(m_old - m_new)
            pexp = jnp.exp2(s_ref[buf, p, 0:n, :] - m_new[0:1])
            m_ref[p] = m_new
            l_ref[p] = alpha * l_ref[p] + _fold_keys(pexp, jnp.sum)
            acc_ref[p] = alpha[0:1] * acc_ref[p] + jnp.dot(vat_ref[ls, rows], pexp.astype(BF16),
                                                           preferred_element_type=F32)

    @pl.when(nwide > 0)
    def _():
        pipelined(0, nwide, lambda jw, buf: logits(wide_rows(jw), wide, buf, None),
                  lambda jw, buf, tile_max: attend(wide_rows(jw), wide, buf, tile_max))

    pipelined(n0, nkb, lambda j, buf: logits(narrow_rows(j), kb, buf, jnp.minimum(i - j, 2)),
              lambda j, buf, tile_max: attend(narrow_rows(j), kb, buf, tile_max))

    for p in range(npair):
        o_t = acc_ref[p] / jnp.sum(l_ref[p], axis=0, keepdims=True)
        o_pair = jnp.concatenate([o_t[:A_HEAD_DIM, :q], o_t[A_HEAD_DIM:, q:]], axis=0).T
        ls = slice(p * LANES, (p + 1) * LANES)
        o_ref[:, ls] = (o_pair * _silu(ga_ref[:, ls])).astype(o_ref.dtype)


def _dsa_bias_table(rel_bias):
    ii = np.arange(DSA_Q)[:, None]
    jj = np.arange(DSA_KB)[None, :]
    buckets = np.stack([_t5_bucket_np(ii - jj), _t5_bucket_np(DSA_KB + ii - jj),
                        np.full((DSA_Q, DSA_KB), REL_BUCKETS - 1)])
    assert (_t5_bucket_np(np.arange(DSA_KB + 1, 4 * DSA_KB)) == REL_BUCKETS - 1).all()
    rb = rel_bias.astype(F32) * LOG2E
    rb = rb - rb[REL_BUCKETS - 1]
    onehot = np.zeros((3, DSA_KB, DSA_Q, REL_BUCKETS), np.float32)
    np.put_along_axis(onehot, np.transpose(buckets, (0, 2, 1))[..., None], 1.0, axis=-1)
    tab = jnp.einsum('tsrb,bh->thsr', jnp.asarray(onehot), rb, precision=lax.Precision.HIGHEST)
    tab = tab.reshape(3, A_HEADS // 2, 2, DSA_KB, DSA_Q)
    return jnp.transpose(tab, (0, 1, 3, 2, 4)).reshape(3, A_HEADS // 2, DSA_KB, 2 * DSA_Q)


def _dsa(qa_t, qi_t, wi_t, ga, ka, va_t, ki, btab, top_k):
    s = ga.shape[0]
    q, kb = DSA_Q, DSA_KB
    tri = jnp.asarray(np.tril(np.ones((kb, kb), np.float32), -1), BF16)
    whole = lambda shape: pl.BlockSpec(shape, lambda i: (0,) * len(shape), pipeline_mode=pl.Buffered(1))
    return pl.pallas_call(
        functools.partial(_dsa_kernel, top_k=top_k),
        grid=(s // q,),
        in_specs=[pl.BlockSpec((A_HEADS * LANES, q), lambda i: (0, i)),
                  pl.BlockSpec((IDX_HEADS * LANES, q), lambda i: (0, i)),
                  pl.BlockSpec((SUBLANES, q), lambda i: (0, i)),
                  pl.BlockSpec((q, A_W), lambda i: (i, 0)),
                  whole((s, A_W)), whole((A_W, s)), whole((s, LANES)),
                  whole((3, A_HEADS // 2, kb, 2 * q)), whole((kb, kb))],
        out_specs=pl.BlockSpec((q, A_W), lambda i: (i, 0)),
        out_shape=jax.ShapeDtypeStruct((s, A_W), BF16),
        scratch_shapes=[pltpu.VMEM((s, q), jnp.int32),
                        pltpu.VMEM((DSA_CELLS, SUBLANES, q), jnp.int32),
                        pltpu.VMEM((2, A_HEADS // 2, DSA_WIDE * kb, 2 * q), F32),
                        pltpu.VMEM((A_HEADS // 2, LANES, 2 * q), F32),
                        pltpu.VMEM((A_HEADS // 2, SUBLANES, 2 * q), F32),
                        pltpu.VMEM((A_HEADS // 2, SUBLANES, 2 * q), F32)],
        compiler_params=pltpu.CompilerParams(dimension_semantics=("arbitrary",), vmem_limit_bytes=VMEM_LIMIT),
        name="dsa",
    )(qa_t, qi_t, wi_t, ga, ka, va_t, ki, btab, tri)


GLA_BLOCK = 512


def _gla_kernel(qb_ref, kb_ref, vb_ref, gb_ref, lr_ref, wlr_ref, blr_ref, gn_ref, tril_ref, o_ref,
                state_ref, b_ref, att_ref):
    c, sub = GLA_CHUNK, GLA_SUB
    nsub = c // sub

    @pl.when(pl.program_id(0) == 0)
    def _():
        state_ref[...] = jnp.zeros(state_ref.shape, F32)

    dn_t = (((1,), (1,)), ((), ()))
    dn_0 = (((0,), (0,)), ((), ()))
    hp = lax.Precision.HIGHEST
    row_i = lax.broadcasted_iota(jnp.int32, (sub, sub), 0)
    col_j = lax.broadcasted_iota(jnp.int32, (sub, sub), 1)

    def chunk(ci, carry):
        r0 = pl.multiple_of(ci * c, c)
        z = jnp.dot(lr_ref[pl.ds(r0, c), :], wlr_ref[...], precision=hp, preferred_element_type=F32) + blr_ref[...]
        log_a = (jnp.minimum(z, 0.0) - jnp.log1p(jnp.exp(-jnp.abs(z)))) * (LOG2E / GLA_TAU)
        b_ref[...] = jnp.dot(tril_ref[...], log_a, precision=hp, preferred_element_type=F32)
        for h in range(B_HEADS):
            ls = slice(h * LANES, (h + 1) * LANES)
            b = b_ref[:, ls]
            qh = qb_ref[pl.ds(r0, c), ls]
            kh = kb_ref[pl.ds(r0, c), ls]
            vh = vb_ref[pl.ds(r0, c), ls]
            state_t = state_ref[h]
            b_last = b[c - 1:c, :]
            o = lax.dot_general((qh * jnp.exp2(b)).astype(BF16), state_t.astype(BF16), dn_t,
                                preferred_element_type=F32)
            for si in range(nsub):
                rs = slice(si * sub, (si + 1) * sub)
                b_i, q_i = b[rs], qh[rs]
                if si > 0:
                    ref_row = b[si * sub:si * sub + 1, :]
                    q_t = (q_i * jnp.exp2(b_i - ref_row)).astype(BF16)
                    k_t = (kh[:si * sub] * jnp.exp2(ref_row - b[:si * sub])).astype(BF16)
                    att_ref[rs, :si * sub] = lax.dot_general(q_t, k_t, dn_t, preferred_element_type=F32)
                k_i = kh[rs]
                diag = jnp.zeros((sub, sub), F32)
                for jl in range(sub):
                    diff = jnp.minimum(b_i - b_i[jl:jl + 1, :], 0.0)
                    col = jnp.sum(q_i * k_i[jl:jl + 1, :] * jnp.exp2(diff), axis=-1, keepdims=True)
                    diag = jnp.where(col_j == jl, col, diag)
                att_ref[rs, rs] = jnp.where(col_j <= row_i, diag, 0.0)
                if si + 1 < nsub:
                    att_ref[rs, (si + 1) * sub:] = jnp.zeros((sub, c - (si + 1) * sub), F32)
            o = o + jnp.dot(att_ref[...].astype(BF16), vh.astype(BF16), preferred_element_type=F32)
            k_dec = (kh * jnp.exp2(b_last - b)).astype(BF16)
            state_ref[h] = (jnp.exp2(b_last) * state_t
                            + lax.dot_general(vh.astype(BF16), k_dec, dn_0, preferred_element_type=F32))
            o = o * lax.rsqrt(jnp.mean(o * o, axis=-1, keepdims=True) + EPS) * gn_ref[...]
            o_ref[pl.ds(r0, c), ls] = (o * _silu(gb_ref[pl.ds(r0, c), ls])).astype(o_ref.dtype)
        return carry

    lax.fori_loop(0, GLA_BLOCK // c, chunk, 0)


def _gla(qb, kb, vb, gb, lr, wlr_pad, blr_pad, gn):
    s = qb.shape[0]
    blk = min(GLA_BLOCK, s)
    assert blk == GLA_BLOCK and s % blk == 0
    tril = jnp.asarray(np.tril(np.ones((GLA_CHUNK, GLA_CHUNK), np.float32)))
    row = lambda wd: pl.BlockSpec((blk, wd), lambda i: (i, 0))
    const = lambda shape: pl.BlockSpec(shape, lambda i: (0,) * len(shape))
    return pl.pallas_call(
        _gla_kernel,
        grid=(s // blk,),
        in_specs=[row(B_HEADS * LANES), row(B_HEADS * LANES), row(B_VW), row(B_VW), row(LANES),
                  const((LANES, B_HEADS * LANES)), const((1, B_HEADS * LANES)), const((1, B_DV)),
                  const((GLA_CHUNK, GLA_CHUNK))],
        out_specs=row(B_VW),
        out_shape=jax.ShapeDtypeStruct((s, B_VW), BF16),
        scratch_shapes=[pltpu.VMEM((B_HEADS, LANES, B_DV), F32),
                        pltpu.VMEM((GLA_CHUNK, B_HEADS * LANES), F32),
                        pltpu.VMEM((GLA_CHUNK, GLA_CHUNK), F32)],
        compiler_params=pltpu.CompilerParams(dimension_semantics=("arbitrary",), vmem_limit_bytes=VMEM_LIMIT),
        name="gla",
    )(qb, kb, vb, gb, lr, wlr_pad, blr_pad, gn, tril)


def _mid_kernel(x_ref, ma_ref, mb_ref, wo_ref, g_ref, wi_ref, x1_ref, u_ref, sg_ref):
    x1 = (x_ref[...]
          + jnp.dot(ma_ref[...], wo_ref[:A_W, :], preferred_element_type=F32)
          + jnp.dot(mb_ref[...], wo_ref[A_W:, :], preferred_element_type=F32))
    x1_ref[...] = x1
    h = (x1 * lax.rsqrt(jnp.mean(x1 * x1, axis=-1, keepdims=True) + EPS) * g_ref[...]).astype(BF16)
    val = jnp.dot(h, wi_ref[:, :D_MODEL], preferred_element_type=F32)
    glu = jnp.dot(h, wi_ref[:, D_MODEL:2 * D_MODEL], preferred_element_type=F32)
    gate = jnp.dot(h, wi_ref[:, 2 * D_MODEL:], preferred_element_type=F32)
    u_ref[...] = val * jax.nn.sigmoid(glu)
    sg_ref[...] = _silu(gate)


def _mid(x2d, mix_a, mix_b, w_out, g, w_in, tm):
    s = x2d.shape[0]
    row = lambda wd: pl.BlockSpec((tm, wd), lambda i: (i, 0))
    const = lambda shape: pl.BlockSpec(shape, lambda i: (0,) * len(shape))
    return pl.pallas_call(
        _mid_kernel,
        grid=(s // tm,),
        in_specs=[row(D_MODEL), row(A_W), row(B_VW), const((A_W + B_VW, D_MODEL)), const((1, D_MODEL)),
                  const((D_MODEL, 3 * D_MODEL))],
        out_specs=[row(D_MODEL), row(D_MODEL), row(D_MODEL)],
        out_shape=[jax.ShapeDtypeStruct((s, D_MODEL), F32)] * 3,
        compiler_params=pltpu.CompilerParams(dimension_semantics=("arbitrary",), vmem_limit_bytes=VMEM_LIMIT),
        name="mid",
    )(x2d, mix_a, mix_b, w_out, g, w_in)


HALO = 32
CONV_ROWS = 128


def _odd_out_kernel(u_ref, halo_ref, sg_ref, x1_ref, cw_ref, cb_ref, lg_ref, lb_ref, wo_ref, nf_ref, o_ref,
                    ubuf_ref, y_ref, shift_ref, *, tm):
    i = pl.program_id(0)
    ubuf_ref[HALO:, :] = u_ref[...]
    ubuf_ref[:HALO, :] = jnp.where(i > 0, halo_ref[...], 0.0)
    base = HALO - (CONV_WIDTH - 1)
    for cs in range(D_MODEL // LANES):
        ls = slice(cs * LANES, (cs + 1) * LANES)
        for r0 in range(0, tm, CONV_ROWS):
            acc = jnp.zeros((CONV_ROWS, LANES), F32)
            for res in range(SUBLANES):
                taps = [k for k in range(CONV_WIDTH) if (base + k) % SUBLANES == res]
                first = base + taps[0]
                rows = taps[-1] - taps[0] + CONV_ROWS
                shift_ref[0:rows, :] = ubuf_ref[r0 + first:r0 + first + rows, ls]
                aligned = shift_ref[0:rows, :]
                for k in taps:
                    a = base + k - first
                    acc = acc + aligned[a:a + CONV_ROWS] * cw_ref[k:k + 1, ls]
            y_ref[r0:r0 + CONV_ROWS, ls] = acc + cb_ref[:, ls]
    y = y_ref[...]
    mu = jnp.mean(y, axis=-1, keepdims=True)
    yc = y - mu
    var = jnp.mean(yc * yc, axis=-1, keepdims=True)
    yn = yc * lax.rsqrt(var + EPS) * lg_ref[...] + lb_ref[...]
    mixed = (_silu(yn) * sg_ref[...]).astype(BF16)
    x2 = x1_ref[...] + jnp.dot(mixed, wo_ref[...], preferred_element_type=F32)
    o_ref[...] = x2 * lax.rsqrt(jnp.mean(x2 * x2, axis=-1, keepdims=True) + EPS) * nf_ref[...]


def _odd_out(u, sg, x1, cw_pad, cb, lg, lb, w_out, nf, tm):
    s = u.shape[0]
    row = pl.BlockSpec((tm, D_MODEL), lambda i: (i, 0))
    const = lambda shape: pl.BlockSpec(shape, lambda i: (0,) * len(shape))
    halo = pl.BlockSpec((HALO, D_MODEL), lambda i: (jnp.maximum(i * (tm // HALO) - 1, 0), 0))
    return pl.pallas_call(
        functools.partial(_odd_out_kernel, tm=tm),
        grid=(s // tm,),
        in_specs=[row, halo, row, row, const((HALO, D_MODEL)), const((1, D_MODEL)), const((1, D_MODEL)),
                  const((1, D_MODEL)), const((D_MODEL, D_MODEL)), const((1, D_MODEL))],
        out_specs=row,
        out_shape=jax.ShapeDtypeStruct((s, D_MODEL), F32),
        scratch_shapes=[pltpu.VMEM((HALO + tm, D_MODEL), F32), pltpu.VMEM((tm, D_MODEL), F32),
                        pltpu.VMEM((HALO + CONV_ROWS, LANES), F32)],
        compiler_params=pltpu.CompilerParams(dimension_semantics=("arbitrary",), vmem_limit_bytes=VMEM_LIMIT),
        name="odd_out",
    )(u, u, sg, x1, cw_pad, cb, lg, lb, w_out, nf)


def kernel(x, rel_bias, norm_even, w_in_even, w_gla_lr, b_gla_lr, gla_norm, w_out_even, norm_odd, w_in_odd,
           conv_w, conv_b, conv_ln_g, conv_ln_b, w_out_odd, norm_f):
    bsz, s, d = x.shape
    assert bsz == 1 and d == D_MODEL and s % GLA_BLOCK == 0
    assert norm_even.shape[0] == 1 and norm_odd.shape[0] == 1
    top_k = min(TOPK_MAX, s // 4)
    tm = ROW_BLOCK
    x2d = x.reshape(s, d)

    wt_pad = _even_in_weights(w_in_even[0])
    qa_t, ka, va_t, ga, qi_t, ki, wi_t, qb, kb, vb, gb, lr = _even_in(x2d, norm_even[0].reshape(1, d), wt_pad, tm)
    mix_a = _dsa(qa_t, qi_t, wi_t, ga, ka, va_t, ki, _dsa_bias_table(rel_bias), top_k)

    head_pad = ((0, 0), (0, 0), (0, LANES - B_DK))
    wlr_pad = jnp.pad(w_gla_lr[0].reshape(GLA_RANK, B_HEADS, B_DK), head_pad).reshape(GLA_RANK, B_HEADS * LANES)
    wlr_pad = jnp.pad(wlr_pad, ((0, LANES - GLA_RANK), (0, 0)))
    blr_pad = jnp.pad(b_gla_lr[0].reshape(1, B_HEADS, B_DK), head_pad).reshape(1, B_HEADS * LANES)
    mix_b = _gla(qb, kb, vb, gb, lr, wlr_pad, blr_pad, gla_norm[0].reshape(1, B_DV))

    x1, u, sg = _mid(x2d, mix_a, mix_b, w_out_even[0].astype(BF16), norm_odd[0].reshape(1, d),
                     w_in_odd[0].astype(BF16), tm)

    cw_pad = jnp.zeros((HALO, d), F32).at[:CONV_WIDTH].set(conv_w[0])
    out = _odd_out(u, sg, x1, cw_pad, conv_b[0].reshape(1, d), conv_ln_g[0].reshape(1, d),
                   conv_ln_b[0].reshape(1, d), w_out_odd[0].astype(BF16), norm_f.reshape(1, d), tm)
    return out.reshape(bsz, s, d)
```

```python
import functools
import math

import jax
import jax.numpy as jnp
import numpy as np
from jax import lax
from jax.experimental import pallas as pl
from jax.experimental.pallas import tpu as pltpu

D_MODEL = 1024
A_HEADS = 8
A_HEAD_DIM = 64
IDX_HEADS = 4
IDX_DIM = 64
TOPK_MAX = 256
B_HEADS = 4
B_DK = 64
B_DV = 128
GLA_RANK = 16
GLA_TAU = 16.0
GLA_CHUNK = 64
GLA_SUB = 16
CONV_WIDTH = 31
REL_BUCKETS = 32
REL_MAX_EXACT = 16
REL_MAX_DIST = 128
EPS = 1e-6

A_W = A_HEADS * A_HEAD_DIM
B_KW = B_HEADS * B_DK
B_VW = B_HEADS * B_DV

LANES = 128
SUBLANES = 8
VMEM_LIMIT = 56 * 1024 * 1024
ROW_BLOCK = 512

F32 = jnp.float32
BF16 = jnp.bfloat16
NEG_INF = float("-inf")
INT_MIN = -(2 ** 31)
NEG_INF_BITS = int(np.array(-np.inf, np.float32).view(np.int32))
NEG_INF_KEY = NEG_INF_BITS ^ 0x7FFFFFFF
M_INIT = -1e30
LOG2E = math.log2(math.e)

C_QA = 0
C_KA = C_QA + A_HEADS * LANES
C_VA = C_KA + A_W
C_GA = C_VA + A_W
C_QI = C_GA + A_W
C_KI = C_QI + IDX_HEADS * LANES
C_WI = C_KI + LANES
C_QB = C_WI + LANES
C_KB = C_QB + B_HEADS * LANES
C_VB = C_KB + B_HEADS * LANES
C_GB = C_VB + B_VW
C_LR = C_GB + B_VW
C_END = C_LR + LANES


def _t5_bucket_np(dist):
    dist = np.maximum(dist, 0)
    d = np.maximum(dist, 1).astype(np.float32)
    large = REL_MAX_EXACT + (np.log(d / REL_MAX_EXACT) / math.log(REL_MAX_DIST / REL_MAX_EXACT)
                             * (REL_BUCKETS - REL_MAX_EXACT)).astype(np.int32)
    large = np.minimum(large, REL_BUCKETS - 1)
    return np.where(dist < REL_MAX_EXACT, dist, large)


def _silu(x):
    return x * jax.nn.sigmoid(x)


def _even_in_kernel(x_ref, g_ref, wt_ref, qat_ref, ka_ref, vat_ref, ga_ref, qit_ref, ki_ref, wit_ref,
                    qb_ref, kb_ref, vb_ref, gb_ref, lr_ref):
    x = x_ref[...]
    h = (x * lax.rsqrt(jnp.mean(x * x, axis=-1, keepdims=True) + EPS) * g_ref[...]).astype(BF16)
    dn_t = (((1,), (1,)), ((), ()))

    def proj(lo, hi):
        return lax.dot_general(h, wt_ref[lo:hi, :], dn_t, preferred_element_type=F32)

    def proj_t(lo, hi):
        return lax.dot_general(wt_ref[lo:hi, :], h, dn_t, preferred_element_type=F32)

    qat_ref[...] = proj_t(C_QA, C_KA).astype(BF16)
    ka_ref[...] = proj(C_KA, C_VA).astype(BF16)
    vat_ref[...] = proj_t(C_VA, C_GA).astype(BF16)
    ga_ref[...] = proj(C_GA, C_QI)
    qit_ref[...] = proj_t(C_QI, C_KI).astype(BF16)
    ki_ref[...] = proj(C_KI, C_WI).astype(BF16)
    wit_ref[...] = proj_t(C_WI, C_WI + SUBLANES)
    qb_ref[...] = proj(C_QB, C_KB)
    kb_ref[...] = proj(C_KB, C_VB)
    vb_ref[...] = proj(C_VB, C_GB)
    gb_ref[...] = proj(C_GB, C_LR)
    lr_ref[...] = proj(C_LR, C_END)


def _even_in_layout():
    cols = np.cumsum([0, A_W, A_W, A_W, A_W, IDX_HEADS * IDX_DIM, IDX_DIM, IDX_HEADS,
                      B_KW, B_KW, B_VW, B_VW, GLA_RANK])
    (s_qa, s_ka, s_va, s_ga, s_qi, s_ki, s_wi, s_qb, s_kb, s_vb, s_gb, s_lr, _) = [int(c) for c in cols]
    src = np.full((C_END,), -1, np.int32)
    scale = np.ones((C_END,), np.float32)

    def put(dst, start, width, sc=1.0):
        src[dst:dst + width] = np.arange(start, start + width)
        scale[dst:dst + width] = sc

    for h in range(A_HEADS):
        put(C_QA + h * LANES + (h % 2) * A_HEAD_DIM, s_qa + h * A_HEAD_DIM, A_HEAD_DIM, (A_HEAD_DIM ** -0.5) * LOG2E)
    put(C_KA, s_ka, A_W)
    put(C_VA, s_va, A_W)
    put(C_GA, s_ga, A_W)
    for h in range(IDX_HEADS):
        put(C_QI + h * LANES, s_qi + h * IDX_DIM, IDX_DIM)
    put(C_KI, s_ki, IDX_DIM)
    put(C_WI, s_wi, IDX_HEADS, (IDX_DIM ** -0.5) * (IDX_HEADS ** -0.5))
    for h in range(B_HEADS):
        put(C_QB + h * LANES, s_qb + h * B_DK, B_DK, B_DK ** -0.5)
        put(C_KB + h * LANES, s_kb + h * B_DK, B_DK)
    put(C_VB, s_vb, B_VW)
    put(C_GB, s_gb, B_VW)
    put(C_LR, s_lr, GLA_RANK)
    return src, scale


def _even_in_weights(w):
    src, scale = _even_in_layout()
    pieces, start = [], 0
    for end in range(1, C_END + 1):
        if end == C_END or (src[end] >= 0) != (src[start] >= 0) or (src[end] >= 0 and src[end] != src[end - 1] + 1):
            if src[start] >= 0:
                pieces.append(w[:, int(src[start]):int(src[end - 1]) + 1])
            else:
                pieces.append(jnp.zeros((D_MODEL, end - start), w.dtype))
            start = end
    w_pad = jnp.concatenate(pieces, axis=1) * jnp.asarray(scale)[None, :]
    return w_pad.T.astype(BF16)


def _even_in(x2d, g, wt_pad, tm):
    s = x2d.shape[0]
    row = lambda wd, dt: (pl.BlockSpec((tm, wd), lambda i: (i, 0)), jax.ShapeDtypeStruct((s, wd), dt))
    col = lambda ht, dt: (pl.BlockSpec((ht, tm), lambda i: (0, i)), jax.ShapeDtypeStruct((ht, s), dt))
    outs = [col(C_KA - C_QA, BF16), row(A_W, BF16), col(A_W, BF16), row(A_W, F32), col(C_KI - C_QI, BF16),
            row(LANES, BF16), col(SUBLANES, F32), row(C_KB - C_QB, F32), row(C_VB - C_KB, F32), row(B_VW, F32),
            row(B_VW, F32), row(LANES, F32)]
    return pl.pallas_call(
        _even_in_kernel,
        grid=(s // tm,),
        in_specs=[pl.BlockSpec((tm, D_MODEL), lambda i: (i, 0)),
                  pl.BlockSpec((1, D_MODEL), lambda i: (0, 0)),
                  pl.BlockSpec((C_END, D_MODEL), lambda i: (0, 0))],
        out_specs=[spec for spec, _ in outs],
        out_shape=[shape for _, shape in outs],
        compiler_params=pltpu.CompilerParams(dimension_semantics=("arbitrary",), vmem_limit_bytes=VMEM_LIMIT),
        name="even_in",
    )(x2d, g, wt_pad)


DSA_Q = 128
DSA_KB = 128
DSA_WIDE = 4
DSA_NACC = 4
DSA_CELLS = 64
DSA_CELL_BITS = 14


def _f32_key(x):
    bits = lax.bitcast_convert_type(x, jnp.int32)
    return bits ^ (lax.shift_right_arithmetic(bits, 31) & 0x7FFFFFFF)


def _fold_keys(x, op):
    n, w = x.shape
    group = DSA_NACC * SUBLANES
    if n % group == 0 and n > group:
        x = op(x.reshape(n // group, DSA_NACC, SUBLANES, w), axis=0)
        return op(x, axis=0)
    return op(x.reshape(n // SUBLANES, SUBLANES, w), axis=0)


def _dsa_kernel(qat_ref, qit_ref, wit_ref, ga_ref, ka_ref, vat_ref, ki_ref, btab_ref, tri_ref, o_ref,
                slab_ref, cell_ref, s_ref, acc_ref, m_ref, l_ref, *, top_k):
    q, kb, wide = DSA_Q, DSA_KB, DSA_WIDE * DSA_KB
    i = pl.program_id(0)
    nkb = i + 1
    nwide = jnp.maximum(i - 1, 0) // DSA_WIDE
    n0 = nwide * DSA_WIDE
    t_row = i * q + lax.broadcasted_iota(jnp.int32, (1, q), 1)
    key_off = lax.broadcasted_iota(jnp.int32, (kb, 1), 0)
    npair = A_HEADS // 2

    def wide_rows(jw):
        return pl.ds(pl.multiple_of(jw * wide, wide), wide)

    def narrow_rows(j):
        return pl.ds(pl.multiple_of(j * kb, kb), kb)

    def pipelined(first, last, produce, consume):
        def body(u, carry):
            t = first + 2 * u
            consume(t, 0, carry)
            consume(t + 1, 1, produce(t + 1, 1))
            return produce(t + 2, 0)

        npairs = (last - first - 1) // 2
        carry = lax.fori_loop(0, npairs, body, produce(first, 0))
        t = first + 2 * npairs
        consume(t, 0, carry)

        @pl.when(t + 1 < last)
        def _():
            consume(t + 1, 1, produce(t + 1, 1))

    w_head = [wit_ref[h:h + 1, :] for h in range(IDX_HEADS)]

    def scores(rows):
        k_idx = ki_ref[rows, :]
        sc = None
        for h in range(IDX_HEADS):
            s = jnp.dot(k_idx, qit_ref[h * LANES:(h + 1) * LANES, :], preferred_element_type=F32)
            term = w_head[h] * jnp.maximum(s, 0.0)
            sc = term if sc is None else sc + term
        return sc

    assert wide % (DSA_CELLS * SUBLANES) == 0
    cell_ref[...] = jnp.full(cell_ref.shape, INT_MIN, jnp.int32)

    def score_rows(rows, n):
        key = _f32_key(scores(rows))
        slab_ref[rows, :] = key
        cells = jnp.max(key.reshape(n // (DSA_CELLS * SUBLANES), DSA_CELLS, SUBLANES, q), axis=0)
        cell_ref[...] = jnp.maximum(cell_ref[...], cells)

    def pair_rows(jp):
        return pl.ds(pl.multiple_of(jp * (2 * wide), 2 * wide), 2 * wide)

    def score_narrow(j, c):
        rows = narrow_rows(j)
        key = _f32_key(jnp.where(j * kb + key_off <= t_row, scores(rows), NEG_INF))
        slab_ref[rows, :] = key
        ncls = kb // SUBLANES
        cell_ref[0:ncls] = jnp.maximum(cell_ref[0:ncls], key.reshape(ncls, SUBLANES, q))
        return c

    def quad_rows(jq):
        return pl.ds(pl.multiple_of(jq * (4 * wide), 4 * wide), 4 * wide)

    lax.fori_loop(0, nwide // 4, lambda jq, c: score_rows(quad_rows(jq), 4 * wide), None)
    lax.fori_loop((nwide // 4) * 2, nwide // 2, lambda jp, c: score_rows(pair_rows(jp), 2 * wide), None)
    lax.fori_loop((nwide // 2) * 2, nwide, lambda jw, c: score_rows(wide_rows(jw), wide), None)
    lax.fori_loop(n0, nkb, score_narrow, 0)

    def count_ge(trial):
        def cnt(blk):
            return _fold_keys(jnp.where(blk >= trial, 1.0, 0.0), jnp.sum)

        ncw = nkb // DSA_WIDE
        acc = lax.fori_loop(0, ncw // 2, lambda jp, a: a + cnt(slab_ref[pair_rows(jp), :]),
                            jnp.zeros((SUBLANES, q), F32))
        acc = lax.fori_loop((ncw // 2) * 2, ncw, lambda jw, a: a + cnt(slab_ref[wide_rows(jw), :]), acc)
        acc = lax.fori_loop(ncw * DSA_WIDE, nkb, lambda j, a: a + cnt(slab_ref[narrow_rows(j), :]), acc)
        return jnp.sum(acc, axis=0, keepdims=True).astype(jnp.int32)

    assert DSA_CELLS * SUBLANES >= top_k
    cell_max = cell_ref[...]
    hi_key = jnp.max(jnp.max(cell_max, axis=0), axis=0, keepdims=True)

    def cell_bisect(b, c):
        trial = c + lax.shift_left(jnp.int32(1), 31 - b)
        cnt = jnp.sum(jnp.sum(jnp.where(cell_max >= trial, 1.0, 0.0), axis=0), axis=0, keepdims=True)
        return jnp.where(cnt >= top_k, trial, c)

    lo_key = lax.fori_loop(0, DSA_CELL_BITS, cell_bisect, jnp.full((1, q), INT_MIN, jnp.int32))
    base = lo_key
    span = hi_key - lo_key
    nbits = jnp.max(32 - lax.clz(span))

    def bisect(b, state):
        off, cnt, above = state
        trial_off = off + lax.shift_left(jnp.int32(1), nbits - 1 - b)
        in_span = (trial_off ^ INT_MIN) <= (span ^ INT_MIN)
        c = jnp.where(in_span, count_ge(base + trial_off), 0)
        ok = c >= top_k
        return jnp.where(ok, trial_off, off), jnp.where(ok, c, cnt), jnp.where(ok, above, c)

    uncounted = jnp.full((1, q), -1, jnp.int32)
    off, cnt_thr, cnt_above = lax.fori_loop(0, nbits, bisect, (jnp.zeros((1, q), jnp.int32), uncounted, uncounted))
    thr = base + off
    cnt_thr = lax.cond(jnp.min(cnt_thr) < 0, lambda: count_ge(thr), lambda: cnt_thr)
    tied = jnp.logical_and(cnt_thr > top_k, thr > NEG_INF_KEY)
    any_tie = jnp.max(jnp.where(tied, 1, 0)) > 0

    thr_valid = jnp.maximum(thr, NEG_INF_KEY + 1)

    def mask_fast():
        def body(rows):
            slab_ref[rows, :] = jnp.where(slab_ref[rows, :] >= thr_valid, 0, NEG_INF_BITS)

        lax.fori_loop(0, nwide, lambda jw, c: body(wide_rows(jw)), None)
        lax.fori_loop(n0, nkb, lambda j, c: body(narrow_rows(j)), None)

    def mask_tied():
        above = lax.cond(jnp.min(cnt_above) < 0, lambda: count_ge(thr + 1), lambda: cnt_above)
        thr_c = jnp.maximum(thr, NEG_INF_KEY)
        need = jnp.where(thr_c > NEG_INF_KEY, (top_k - above).astype(F32), -1.0)

        def body(rows, n, seen):
            for g in range(n // kb):
                blk = pl.ds(rows.start + g * kb, kb)
                key = slab_ref[blk, :]
                eq = key == thr_c
                eq_f = jnp.where(eq, 1.0, 0.0)
                before = jnp.dot(tri_ref[...], eq_f.astype(BF16), preferred_element_type=F32)
                tie_bits = jnp.where(seen + before < need, 0, NEG_INF_BITS)
                slab_ref[blk, :] = jnp.where(key > thr_c, 0, jnp.where(eq, tie_bits, NEG_INF_BITS))
                seen = seen + jnp.sum(_fold_keys(eq_f, jnp.sum), axis=0, keepdims=True)
            return seen

        seen = lax.fori_loop(0, nwide // 2, lambda jp, c: body(pair_rows(jp), 2 * wide, c), jnp.zeros((1, q), F32))
        seen = lax.fori_loop((nwide // 2) * 2, nwide, lambda jw, c: body(wide_rows(jw), wide, c), seen)
        lax.fori_loop(n0, nkb, lambda j, c: body(narrow_rows(j), kb, c), seen)

    lax.cond(any_tie, mask_tied, mask_fast)

    m_ref[...] = jnp.full(m_ref.shape, M_INIT, F32)
    l_ref[...] = jnp.zeros(l_ref.shape, F32)
    acc_ref[...] = jnp.zeros(acc_ref.shape, F32)

    def logits(rows, n, buf, rel):
        mask = lax.bitcast_convert_type(slab_ref[rows, :], F32)
        mask2 = jnp.concatenate([mask, mask], axis=1)
        tile_max = []
        for p in range(npair):
            ls = slice(p * LANES, (p + 1) * LANES)
            w_q = jnp.concatenate([qat_ref[(2 * p) * LANES:(2 * p + 1) * LANES, :],
                                   qat_ref[(2 * p + 1) * LANES:(2 * p + 2) * LANES, :]], axis=1)
            s = jnp.dot(ka_ref[rows, ls], w_q, preferred_element_type=F32) + mask2
            if rel is not None:
                s = s + btab_ref[rel, p]
            s_ref[buf, p, 0:n, :] = s
            tile_max.append(_fold_keys(s, jnp.max))
        return tuple(tile_max)

    def attend(rows, n, buf, tile_max):
        for p in range(npair):
            ls = slice(p * LANES, (p + 1) * LANES)
            m_old = m_ref[p]
            m_new = jnp.maximum(m_old, jnp.max(tile_max[p], axis=0, keepdims=True))
            alpha = jnp.exp2(m_old - m_new)
            pexp = jnp.exp2(s_ref[buf, p, 0:n, :] - m_new[0:1])
            m_ref[p] = m_new
            l_ref[p] = alpha * l_ref[p] + _fold_keys(pexp, jnp.sum)
            acc_ref[p] = alpha[0:1] * acc_ref[p] + jnp.dot(vat_ref[ls, rows], pexp.astype(BF16),
                                                           preferred_element_type=F32)

    @pl.when(nwide > 0)
    def _():
        pipelined(0, nwide, lambda jw, buf: logits(wide_rows(jw), wide, buf, None),
                  lambda jw, buf, tile_max: attend(wide_rows(jw), wide, buf, tile_max))

    pipelined(n0, nkb, lambda j, buf: logits(narrow_rows(j), kb, buf, jnp.minimum(i - j, 2)),
              lambda j, buf, tile_max: attend(narrow_rows(j), kb, buf, tile_max))

    for p in range(npair):
        o_t = acc_ref[p] / jnp.sum(l_ref[p], axis=0, keepdims=True)
        o_pair = jnp.concatenate([o_t[:A_HEAD_DIM, :q], o_t[A_HEAD_DIM:, q:]], axis=0).T
        ls = slice(p * LANES, (p + 1) * LANES)
        o_ref[:, ls] = (o_pair * _silu(ga_ref[:, ls])).astype(o_ref.dtype)


def _dsa_bias_table(rel_bias):
    ii = np.arange(DSA_Q)[:, None]
    jj = np.arange(DSA_KB)[None, :]
    buckets = np.stack([_t5_bucket_np(ii - jj), _t5_bucket_np(DSA_KB + ii - jj),
                        np.full((DSA_Q, DSA_KB), REL_BUCKETS - 1)])
    assert (_t5_bucket_np(np.arange(DSA_KB + 1, 4 * DSA_KB)) == REL_BUCKETS - 1).all()
    rb = rel_bias.astype(F32) * LOG2E
    rb = rb - rb[REL_BUCKETS - 1]
    onehot = np.zeros((3, DSA_KB, DSA_Q, REL_BUCKETS), np.float32)
    np.put_along_axis(onehot, np.transpose(buckets, (0, 2, 1))[..., None], 1.0, axis=-1)
    tab = jnp.einsum('tsrb,bh->thsr', jnp.asarray(onehot), rb, precision=lax.Precision.HIGHEST)
    tab = tab.reshape(3, A_HEADS // 2, 2, DSA_KB, DSA_Q)
    return jnp.transpose(tab, (0, 1, 3, 2, 4)).reshape(3, A_HEADS // 2, DSA_KB, 2 * DSA_Q)


def _dsa(qa_t, qi_t, wi_t, ga, ka, va_t, ki, btab, top_k):
    s = ga.shape[0]
    q, kb = DSA_Q, DSA_KB
    tri = jnp.asarray(np.tril(np.ones((kb, kb), np.float32), -1), BF16)
    whole = lambda shape: pl.BlockSpec(shape, lambda i: (0,) * len(shape), pipeline_mode=pl.Buffered(1))
    return pl.pallas_call(
        functools.partial(_dsa_kernel, top_k=top_k),
        grid=(s // q,),
        in_specs=[pl.BlockSpec((A_HEADS * LANES, q), lambda i: (0, i)),
                  pl.BlockSpec((IDX_HEADS * LANES, q), lambda i: (0, i)),
                  pl.BlockSpec((SUBLANES, q), lambda i: (0, i)),
                  pl.BlockSpec((q, A_W), lambda i: (i, 0)),
                  whole((s, A_W)), whole((A_W, s)), whole((s, LANES)),
                  whole((3, A_HEADS // 2, kb, 2 * q)), whole((kb, kb))],
        out_specs=pl.BlockSpec((q, A_W), lambda i: (i, 0)),
        out_shape=jax.ShapeDtypeStruct((s, A_W), BF16),
        scratch_shapes=[pltpu.VMEM((s, q), jnp.int32),
                        pltpu.VMEM((DSA_CELLS, SUBLANES, q), jnp.int32),
                        pltpu.VMEM((2, A_HEADS // 2, DSA_WIDE * kb, 2 * q), F32),
                        pltpu.VMEM((A_HEADS // 2, LANES, 2 * q), F32),
                        pltpu.VMEM((A_HEADS // 2, SUBLANES, 2 * q), F32),
                        pltpu.VMEM((A_HEADS // 2, SUBLANES, 2 * q), F32)],
        compiler_params=pltpu.CompilerParams(dimension_semantics=("arbitrary",), vmem_limit_bytes=VMEM_LIMIT),
        name="dsa",
    )(qa_t, qi_t, wi_t, ga, ka, va_t, ki, btab, tri)


GLA_BLOCK = 512


def _gla_kernel(qb_ref, kb_ref, vb_ref, gb_ref, lr_ref, wlr_ref, blr_ref, gn_ref, tril_ref, o_ref,
                state_ref, b_ref, att_ref):
    c, sub = GLA_CHUNK, GLA_SUB
    nsub = c // sub

    @pl.when(pl.program_id(0) == 0)
    def _():
        state_ref[...] = jnp.zeros(state_ref.shape, F32)

    dn_t = (((1,), (1,)), ((), ()))
    dn_0 = (((0,), (0,)), ((), ()))
    hp = lax.Precision.HIGHEST
    row_i = lax.broadcasted_iota(jnp.int32, (sub, sub), 0)
    col_j = lax.broadcasted_iota(jnp.int32, (sub, sub), 1)

    def chunk(ci, carry):
        r0 = pl.multiple_of(ci * c, c)
        z = jnp.dot(lr_ref[pl.ds(r0, c), :], wlr_ref[...], precision=hp, preferred_element_type=F32) + blr_ref[...]
        log_a = (jnp.minimum(z, 0.0) - jnp.log1p(jnp.exp(-jnp.abs(z)))) * (LOG2E / GLA_TAU)
        b_ref[...] = jnp.dot(tril_ref[...], log_a, precision=hp, preferred_element_type=F32)
        for h in range(B_HEADS):
            ls = slice(h * LANES, (h + 1) * LANES)
            b = b_ref[:, ls]
            qh = qb_ref[pl.ds(r0, c), ls]
            kh = kb_ref[pl.ds(r0, c), ls]
            vh = vb_ref[pl.ds(r0, c), ls]
            state_t = state_ref[h]
            b_last = b[c - 1:c, :]
            o = lax.dot_general((qh * jnp.exp2(b)).astype(BF16), state_t.astype(BF16), dn_t,
                                preferred_element_type=F32)
            for si in range(nsub):
                rs = slice(si * sub, (si + 1) * sub)
                b_i, q_i = b[rs], qh[rs]
                if si > 0:
                    ref_row = b[si * sub:si * sub + 1, :]
                    q_t = (q_i * jnp.exp2(b_i - ref_row)).astype(BF16)
                    k_t = (kh[:si * sub] * jnp.exp2(ref_row - b[:si * sub])).astype(BF16)
                    att_ref[rs, :si * sub] = lax.dot_general(q_t, k_t, dn_t, preferred_element_type=F32)
                k_i = kh[rs]
                diag = jnp.zeros((sub, sub), F32)
                for jl in range(sub):
                    diff = jnp.minimum(b_i - b_i[jl:jl + 1, :], 0.0)
                    col = jnp.sum(q_i * k_i[jl:jl + 1, :] * jnp.exp2(diff), axis=-1, keepdims=True)
                    diag = jnp.where(col_j == jl, col, diag)
                att_ref[rs, rs] = jnp.where(col_j <= row_i, diag, 0.0)
                if si + 1 < nsub:
                    att_ref[rs, (si + 1) * sub:] = jnp.zeros((sub, c - (si + 1) * sub), F32)
            o = o + jnp.dot(att_ref[...].astype(BF16), vh.astype(BF16), preferred_element_type=F32)
            k_dec = (kh * jnp.exp2(b_last - b)).astype(BF16)
            state_ref[h] = (jnp.exp2(b_last) * state_t
                            + lax.dot_general(vh.astype(BF16), k_dec, dn_0, preferred_element_type=F32))
            o = o * lax.rsqrt(jnp.mean(o * o, axis=-1, keepdims=True) + EPS) * gn_ref[...]
            o_ref[pl.ds(r0, c), ls] = (o * _silu(gb_ref[pl.ds(r0, c), ls])).astype(o_ref.dtype)
        return carry

    lax.fori_loop(0, GLA_BLOCK // c, chunk, 0)


def _gla(qb, kb, vb, gb, lr, wlr_pad, blr_pad, gn):
    s = qb.shape[0]
    blk = min(GLA_BLOCK, s)
    assert blk == GLA_BLOCK and s % blk == 0
    tril = jnp.asarray(np.tril(np.ones((GLA_CHUNK, GLA_CHUNK), np.float32)))
    row = lambda wd: pl.BlockSpec((blk, wd), lambda i: (i, 0))
    const = lambda shape: pl.BlockSpec(shape, lambda i: (0,) * len(shape))
    return pl.pallas_call(
        _gla_kernel,
        grid=(s // blk,),
        in_specs=[row(B_HEADS * LANES), row(B_HEADS * LANES), row(B_VW), row(B_VW), row(LANES),
                  const((LANES, B_HEADS * LANES)), const((1, B_HEADS * LANES)), const((1, B_DV)),
                  const((GLA_CHUNK, GLA_CHUNK))],
        out_specs=row(B_VW),
        out_shape=jax.ShapeDtypeStruct((s, B_VW), BF16),
        scratch_shapes=[pltpu.VMEM((B_HEADS, LANES, B_DV), F32),
                        pltpu.VMEM((GLA_CHUNK, B_HEADS * LANES), F32),
                        pltpu.VMEM((GLA_CHUNK, GLA_CHUNK), F32)],
        compiler_params=pltpu.CompilerParams(dimension_semantics=("arbitrary",), vmem_limit_bytes=VMEM_LIMIT),
        name="gla",
    )(qb, kb, vb, gb, lr, wlr_pad, blr_pad, gn, tril)


def _mid_kernel(x_ref, ma_ref, mb_ref, wo_ref, g_ref, wi_ref, x1_ref, u_ref, sg_ref):
    x1 = (x_ref[...]
          + jnp.dot(ma_ref[...], wo_ref[:A_W, :], preferred_element_type=F32)
          + jnp.dot(mb_ref[...], wo_ref[A_W:, :], preferred_element_type=F32))
    x1_ref[...] = x1
    h = (x1 * lax.rsqrt(jnp.mean(x1 * x1, axis=-1, keepdims=True) + EPS) * g_ref[...]).astype(BF16)
    val = jnp.dot(h, wi_ref[:, :D_MODEL], preferred_element_type=F32)
    glu = jnp.dot(h, wi_ref[:, D_MODEL:2 * D_MODEL], preferred_element_type=F32)
    gate = jnp.dot(h, wi_ref[:, 2 * D_MODEL:], preferred_element_type=F32)
    u_ref[...] = val * jax.nn.sigmoid(glu)
    sg_ref[...] = _silu(gate)


def _mid(x2d, mix_a, mix_b, w_out, g, w_in, tm):
    s = x2d.shape[0]
    row = lambda wd: pl.BlockSpec((tm, wd), lambda i: (i, 0))
    const = lambda shape: pl.BlockSpec(shape, lambda i: (0,) * len(shape))
    return pl.pallas_call(
        _mid_kernel,
        grid=(s // tm,),
        in_specs=[row(D_MODEL), row(A_W), row(B_VW), const((A_W + B_VW, D_MODEL)), const((1, D_MODEL)),
                  const((D_MODEL, 3 * D_MODEL))],
        out_specs=[row(D_MODEL), row(D_MODEL), row(D_MODEL)],
        out_shape=[jax.ShapeDtypeStruct((s, D_MODEL), F32)] * 3,
        compiler_params=pltpu.CompilerParams(dimension_semantics=("arbitrary",), vmem_limit_bytes=VMEM_LIMIT),
        name="mid",
    )(x2d, mix_a, mix_b, w_out, g, w_in)


HALO = 32
CONV_ROWS = 128


def _odd_out_kernel(u_ref, halo_ref, sg_ref, x1_ref, cw_ref, cb_ref, lg_ref, lb_ref, wo_ref, nf_ref, o_ref,
                    ubuf_ref, y_ref, shift_ref, *, tm):
    i = pl.program_id(0)
    ubuf_ref[HALO:, :] = u_ref[...]
    ubuf_ref[:HALO, :] = jnp.where(i > 0, halo_ref[...], 0.0)
    base = HALO - (CONV_WIDTH - 1)
    for cs in range(D_MODEL // LANES):
        ls = slice(cs * LANES, (cs + 1) * LANES)
        for r0 in range(0, tm, CONV_ROWS):
            acc = jnp.zeros((CONV_ROWS, LANES), F32)
            for res in range(SUBLANES):
                taps = [k for k in range(CONV_WIDTH) if (base + k) % SUBLANES == res]
                first = base + taps[0]
                rows = taps[-1] - taps[0] + CONV_ROWS
                shift_ref[0:rows, :] = ubuf_ref[r0 + first:r0 + first + rows, ls]
                aligned = shift_ref[0:rows, :]
                for k in taps:
                    a = base + k - first
                    acc = acc + aligned[a:a + CONV_ROWS] * cw_ref[k:k + 1, ls]
            y_ref[r0:r0 + CONV_ROWS, ls] = acc + cb_ref[:, ls]
    y = y_ref[...]
    mu = jnp.mean(y, axis=-1, keepdims=True)
    yc = y - mu
    var = jnp.mean(yc * yc, axis=-1, keepdims=True)
    yn = yc * lax.rsqrt(var + EPS) * lg_ref[...] + lb_ref[...]
    mixed = (_silu(yn) * sg_ref[...]).astype(BF16)
    x2 = x1_ref[...] + jnp.dot(mixed, wo_ref[...], preferred_element_type=F32)
    o_ref[...] = x2 * lax.rsqrt(jnp.mean(x2 * x2, axis=-1, keepdims=True) + EPS) * nf_ref[...]


def _odd_out(u, sg, x1, cw_pad, cb, lg, lb, w_out, nf, tm):
    s = u.shape[0]
    row = pl.BlockSpec((tm, D_MODEL), lambda i: (i, 0))
    const = lambda shape: pl.BlockSpec(shape, lambda i: (0,) * len(shape))
    halo = pl.BlockSpec((HALO, D_MODEL), lambda i: (jnp.maximum(i * (tm // HALO) - 1, 0), 0))
    return pl.pallas_call(
        functools.partial(_odd_out_kernel, tm=tm),
        grid=(s // tm,),
        in_specs=[row, halo, row, row, const((HALO, D_MODEL)), const((1, D_MODEL)), const((1, D_MODEL)),
                  const((1, D_MODEL)), const((D_MODEL, D_MODEL)), const((1, D_MODEL))],
        out_specs=row,
        out_shape=jax.ShapeDtypeStruct((s, D_MODEL), F32),
        scratch_shapes=[pltpu.VMEM((HALO + tm, D_MODEL), F32), pltpu.VMEM((tm, D_MODEL), F32),
                        pltpu.VMEM((HALO + CONV_ROWS, LANES), F32)],
        compiler_params=pltpu.CompilerParams(dimension_semantics=("arbitrary",), vmem_limit_bytes=VMEM_LIMIT),
        name="odd_out",
    )(u, u, sg, x1, cw_pad, cb, lg, lb, w_out, nf)


def kernel(x, rel_bias, norm_even, w_in_even, w_gla_lr, b_gla_lr, gla_norm, w_out_even, norm_odd, w_in_odd,
           conv_w, conv_b, conv_ln_g, conv_ln_b, w_out_odd, norm_f):
    bsz, s, d = x.shape
    assert bsz == 1 and d == D_MODEL and s % GLA_BLOCK == 0
    assert norm_even.shape[0] == 1 and norm_odd.shape[0] == 1
    top_k = min(TOPK_MAX, s // 4)
    tm = ROW_BLOCK
    x2d = x.reshape(s, d)

    wt_pad = _even_in_weights(w_in_even[0])
    qa_t, ka, va_t, ga, qi_t, ki, wi_t, qb, kb, vb, gb, lr = _even_in(x2d, norm_even[0].reshape(1, d), wt_pad, tm)
    mix_a = _dsa(qa_t, qi_t, wi_t, ga, ka, va_t, ki, _dsa_bias_table(rel_bias), top_k)

    head_pad = ((0, 0), (0, 0), (0, LANES - B_DK))
    wlr_pad = jnp.pad(w_gla_lr[0].reshape(GLA_RANK, B_HEADS, B_DK), head_pad).reshape(GLA_RANK, B_HEADS * LANES)
    wlr_pad = jnp.pad(wlr_pad, ((0, LANES - GLA_RANK), (0, 0)))
    blr_pad = jnp.pad(b_gla_lr[0].reshape(1, B_HEADS, B_DK), head_pad).reshape(1, B_HEADS * LANES)
    mix_b = _gla(qb, kb, vb, gb, lr, wlr_pad, blr_pad, gla_norm[0].reshape(1, B_DV))

    x1, u, sg = _mid(x2d, mix_a, mix_b, w_out_even[0].astype(BF16), norm_odd[0].reshape(1, d),
                     w_in_odd[0].astype(BF16), tm)

    cw_pad = jnp.zeros((HALO, d), F32).at[:CONV_WIDTH].set(conv_w[0])
    out = _odd_out(u, sg, x1, cw_pad, conv_b[0].reshape(1, d), conv_ln_g[0].reshape(1, d),
                   conv_ln_b[0].reshape(1, d), w_out_odd[0].astype(BF16), norm_f.reshape(1, d), tm)
    return out.reshape(bsz, s, d)
```

```python
import functools
import math

import jax
import jax.numpy as jnp
import numpy as np
from jax import lax
from jax.experimental import pallas as pl
from jax.experimental.pallas import tpu as pltpu

D_MODEL = 1024
A_HEADS = 8
A_HEAD_DIM = 64
IDX_HEADS = 4
IDX_DIM = 64
TOPK_MAX = 256
B_HEADS = 4
B_DK = 64
B_DV = 128
GLA_RANK = 16
GLA_TAU = 16.0
GLA_CHUNK = 64
GLA_SUB = 16
CONV_WIDTH = 31
REL_BUCKETS = 32
REL_MAX_EXACT = 16
REL_MAX_DIST = 128
EPS = 1e-6

A_W = A_HEADS * A_HEAD_DIM
B_KW = B_HEADS * B_DK
B_VW = B_HEADS * B_DV

LANES = 128
SUBLANES = 8
VMEM_LIMIT = 56 * 1024 * 1024
ROW_BLOCK = 512

F32 = jnp.float32
BF16 = jnp.bfloat16
NEG_INF = float("-inf")
INT_MIN = -(2 ** 31)
NEG_INF_BITS = int(np.array(-np.inf, np.float32).view(np.int32))
NEG_INF_KEY = NEG_INF_BITS ^ 0x7FFFFFFF
M_INIT = -1e30
LOG2E = math.log2(math.e)

C_QA = 0
C_KA = C_QA + A_HEADS * LANES
C_VA = C_KA + A_W
C_GA = C_VA + A_W
C_QI = C_GA + A_W
C_KI = C_QI + IDX_HEADS * LANES
C_WI = C_KI + LANES
C_QB = C_WI + LANES
C_KB = C_QB + B_HEADS * LANES
C_VB = C_KB + B_HEADS * LANES
C_GB = C_VB + B_VW
C_LR = C_GB + B_VW
C_END = C_LR + LANES


def _t5_bucket_np(dist):
    dist = np.maximum(dist, 0)
    d = np.maximum(dist, 1).astype(np.float32)
    large = REL_MAX_EXACT + (np.log(d / REL_MAX_EXACT) / math.log(REL_MAX_DIST / REL_MAX_EXACT)
                             * (REL_BUCKETS - REL_MAX_EXACT)).astype(np.int32)
    large = np.minimum(large, REL_BUCKETS - 1)
    return np.where(dist < REL_MAX_EXACT, dist, large)


def _silu(x):
    return x * jax.nn.sigmoid(x)


def _even_in_kernel(x_ref, g_ref, wt_ref, qat_ref, ka_ref, vat_ref, ga_ref, qit_ref, ki_ref, wit_ref,
                    qb_ref, kb_ref, vb_ref, gb_ref, lr_ref):
    x = x_ref[...]
    h = (x * lax.rsqrt(jnp.mean(x * x, axis=-1, keepdims=True) + EPS) * g_ref[...]).astype(BF16)
    dn_t = (((1,), (1,)), ((), ()))

    def proj(lo, hi):
        return lax.dot_general(h, wt_ref[lo:hi, :], dn_t, preferred_element_type=F32)

    def proj_t(lo, hi):
        return lax.dot_general(wt_ref[lo:hi, :], h, dn_t, preferred_element_type=F32)

    qat_ref[...] = proj_t(C_QA, C_KA).astype(BF16)
    ka_ref[...] = proj(C_KA, C_VA).astype(BF16)
    vat_ref[...] = proj_t(C_VA, C_GA).astype(BF16)
    ga_ref[...] = proj(C_GA, C_QI)
    qit_ref[...] = proj_t(C_QI, C_KI).astype(BF16)
    ki_ref[...] = proj(C_KI, C_WI).astype(BF16)
    wit_ref[...] = proj_t(C_WI, C_WI + SUBLANES)
    qb_ref[...] = proj(C_QB, C_KB)
    kb_ref[...] = proj(C_KB, C_VB)
    vb_ref[...] = proj(C_VB, C_GB)
    gb_ref[...] = proj(C_GB, C_LR)
    lr_ref[...] = proj(C_LR, C_END)


def _even_in_layout():
    cols = np.cumsum([0, A_W, A_W, A_W, A_W, IDX_HEADS * IDX_DIM, IDX_DIM, IDX_HEADS,
                      B_KW, B_KW, B_VW, B_VW, GLA_RANK])
    (s_qa, s_ka, s_va, s_ga, s_qi, s_ki, s_wi, s_qb, s_kb, s_vb, s_gb, s_lr, _) = [int(c) for c in cols]
    src = np.full((C_END,), -1, np.int32)
    scale = np.ones((C_END,), np.float32)

    def put(dst, start, width, sc=1.0):
        src[dst:dst + width] = np.arange(start, start + width)
        scale[dst:dst + width] = sc

    for h in range(A_HEADS):
        put(C_QA + h * LANES + (h % 2) * A_HEAD_DIM, s_qa + h * A_HEAD_DIM, A_HEAD_DIM, (A_HEAD_DIM ** -0.5) * LOG2E)
    put(C_KA, s_ka, A_W)
    put(C_VA, s_va, A_W)
    put(C_GA, s_ga, A_W)
    for h in range(IDX_HEADS):
        put(C_QI + h * LANES, s_qi + h * IDX_DIM, IDX_DIM)
    put(C_KI, s_ki, IDX_DIM)
    put(C_WI, s_wi, IDX_HEADS, (IDX_DIM ** -0.5) * (IDX_HEADS ** -0.5))
    for h in range(B_HEADS):
        put(C_QB + h * LANES, s_qb + h * B_DK, B_DK, B_DK ** -0.5)
        put(C_KB + h * LANES, s_kb + h * B_DK, B_DK)
    put(C_VB, s_vb, B_VW)
    put(C_GB, s_gb, B_VW)
    put(C_LR, s_lr, GLA_RANK)
    return src, scale


def _even_in_weights(w):
    src, scale = _even_in_layout()
    pieces, start = [], 0
    for end in range(1, C_END + 1):
        if end == C_END or (src[end] >= 0) != (src[start] >= 0) or (src[end] >= 0 and src[end] != src[end - 1] + 1):
            if src[start] >= 0:
                pieces.append(w[:, int(src[start]):int(src[end - 1]) + 1])
            else:
                pieces.append(jnp.zeros((D_MODEL, end - start), w.dtype))
            start = end
    w_pad = jnp.concatenate(pieces, axis=1) * jnp.asarray(scale)[None, :]
    return w_pad.T.astype(BF16)


def _even_in(x2d, g, wt_pad, tm):
    s = x2d.shape[0]
    row = lambda wd, dt: (pl.BlockSpec((tm, wd), lambda i: (i, 0)), jax.ShapeDtypeStruct((s, wd), dt))
    col = lambda ht, dt: (pl.BlockSpec((ht, tm), lambda i: (0, i)), jax.ShapeDtypeStruct((ht, s), dt))
    outs = [col(C_KA - C_QA, BF16), row(A_W, BF16), col(A_W, BF16), row(A_W, F32), col(C_KI - C_QI, BF16),
            row(LANES, BF16), col(SUBLANES, F32), row(C_KB - C_QB, F32), row(C_VB - C_KB, F32), row(B_VW, F32),
            row(B_VW, F32), row(LANES, F32)]
    return pl.pallas_call(
        _even_in_kernel,
        grid=(s // tm,),
        in_specs=[pl.BlockSpec((tm, D_MODEL), lambda i: (i, 0)),
                  pl.BlockSpec((1, D_MODEL), lambda i: (0, 0)),
                  pl.BlockSpec((C_END, D_MODEL), lambda i: (0, 0))],
        out_specs=[spec for spec, _ in outs],
        out_shape=[shape for _, shape in outs],
        compiler_params=pltpu.CompilerParams(dimension_semantics=("arbitrary",), vmem_limit_bytes=VMEM_LIMIT),
        name="even_in",
    )(x2d, g, wt_pad)


DSA_Q = 128
DSA_KB = 128
DSA_WIDE = 4
DSA_NACC = 4
DSA_CELLS = 64
DSA_CELL_BITS = 14


def _f32_key(x):
    bits = lax.bitcast_convert_type(x, jnp.int32)
    return bits ^ (lax.shift_right_arithmetic(bits, 31) & 0x7FFFFFFF)


def _fold_keys(x, op):
    n, w = x.shape
    group = DSA_NACC * SUBLANES
    if n % group == 0 and n > group:
        x = op(x.reshape(n // group, DSA_NACC, SUBLANES, w), axis=0)
        return op(x, axis=0)
    return op(x.reshape(n // SUBLANES, SUBLANES, w), axis=0)


def _dsa_kernel(qat_ref, qit_ref, wit_ref, ga_ref, ka_ref, vat_ref, ki_ref, btab_ref, tri_ref, o_ref,
                slab_ref, cell_ref, s_ref, acc_ref, m_ref, l_ref, *, top_k):
    q, kb, wide = DSA_Q, DSA_KB, DSA_WIDE * DSA_KB
    i = pl.program_id(0)
    nkb = i + 1
    nwide = jnp.maximum(i - 1, 0) // DSA_WIDE
    n0 = nwide * DSA_WIDE
    t_row = i * q + lax.broadcasted_iota(jnp.int32, (1, q), 1)
    key_off = lax.broadcasted_iota(jnp.int32, (kb, 1), 0)
    npair = A_HEADS // 2

    def wide_rows(jw):
        return pl.ds(pl.multiple_of(jw * wide, wide), wide)

    def narrow_rows(j):
        return pl.ds(pl.multiple_of(j * kb, kb), kb)

    def pipelined(first, last, produce, consume):
        def body(u, carry):
            t = first + 2 * u
            consume(t, 0, carry)
            consume(t + 1, 1, produce(t + 1, 1))
            return produce(t + 2, 0)

        npairs = (last - first - 1) // 2
        carry = lax.fori_loop(0, npairs, body, produce(first, 0))
        t = first + 2 * npairs
        consume(t, 0, carry)

        @pl.when(t + 1 < last)
        def _():
            consume(t + 1, 1, produce(t + 1, 1))

    w_head = [wit_ref[h:h + 1, :] for h in range(IDX_HEADS)]

    def scores(rows):
        k_idx = ki_ref[rows, :]
        sc = None
        for h in range(IDX_HEADS):
            s = jnp.dot(k_idx, qit_ref[h * LANES:(h + 1) * LANES, :], preferred_element_type=F32)
            term = w_head[h] * jnp.maximum(s, 0.0)
            sc = term if sc is None else sc + term
        return sc

    assert wide % (DSA_CELLS * SUBLANES) == 0
    cell_ref[...] = jnp.full(cell_ref.shape, INT_MIN, jnp.int32)

    def score_rows(rows, n):
        key = _f32_key(scores(rows))
        slab_ref[rows, :] = key
        cells = jnp.max(key.reshape(n // (DSA_CELLS * SUBLANES), DSA_CELLS, SUBLANES, q), axis=0)
        cell_ref[...] = jnp.maximum(cell_ref[...], cells)

    def pair_rows(jp):
        return pl.ds(pl.multiple_of(jp * (2 * wide), 2 * wide), 2 * wide)

    def score_narrow(j, c):
        rows = narrow_rows(j)
        key = _f32_key(jnp.where(j * kb + key_off <= t_row, scores(rows), NEG_INF))
        slab_ref[rows, :] = key
        ncls = kb // SUBLANES
        cell_ref[0:ncls] = jnp.maximum(cell_ref[0:ncls], key.reshape(ncls, SUBLANES, q))
        return c

    def quad_rows(jq):
        return pl.ds(pl.multiple_of(jq * (4 * wide), 4 * wide), 4 * wide)

    lax.fori_loop(0, nwide // 4, lambda jq, c: score_rows(quad_rows(jq), 4 * wide), None)
    lax.fori_loop((nwide // 4) * 2, nwide // 2, lambda jp, c: score_rows(pair_rows(jp), 2 * wide), None)
    lax.fori_loop((nwide // 2) * 2, nwide, lambda jw, c: score_rows(wide_rows(jw), wide), None)
    lax.fori_loop(n0, nkb, score_narrow, 0)

    def count_ge(trial):
        def cnt(blk):
            return _fold_keys(jnp.where(blk >= trial, 1.0, 0.0), jnp.sum)

        ncw = nkb // DSA_WIDE
        acc = lax.fori_loop(0, ncw // 4, lambda jq, a: a + cnt(slab_ref[quad_rows(jq), :]),
                            jnp.zeros((SUBLANES, q), F32))
        acc = lax.fori_loop((ncw // 4) * 2, ncw // 2, lambda jp, a: a + cnt(slab_ref[pair_rows(jp), :]), acc)
        acc = lax.fori_loop((ncw // 2) * 2, ncw, lambda jw, a: a + cnt(slab_ref[wide_rows(jw), :]), acc)
        acc = lax.fori_loop(ncw * DSA_WIDE, nkb, lambda j, a: a + cnt(slab_ref[narrow_rows(j), :]), acc)
        return jnp.sum(acc, axis=0, keepdims=True).astype(jnp.int32)

    assert DSA_CELLS * SUBLANES >= top_k
    cell_max = cell_ref[...]
    hi_key = jnp.max(jnp.max(cell_max, axis=0), axis=0, keepdims=True)

    def cell_bisect(b, c):
        trial = c + lax.shift_left(jnp.int32(1), 31 - b)
        cnt = jnp.sum(jnp.sum(jnp.where(cell_max >= trial, 1.0, 0.0), axis=0), axis=0, keepdims=True)
        return jnp.where(cnt >= top_k, trial, c)

    lo_key = lax.fori_loop(0, DSA_CELL_BITS, cell_bisect, jnp.full((1, q), INT_MIN, jnp.int32))
    base = lo_key
    span = hi_key - lo_key
    nbits = jnp.max(32 - lax.clz(span))

    def bisect(b, state):
        off, cnt, above = state
        trial_off = off + lax.shift_left(jnp.int32(1), nbits - 1 - b)
        in_span = (trial_off ^ INT_MIN) <= (span ^ INT_MIN)
        c = jnp.where(in_span, count_ge(base + trial_off), 0)
        ok = c >= top_k
        return jnp.where(ok, trial_off, off), jnp.where(ok, c, cnt), jnp.where(ok, above, c)

    uncounted = jnp.full((1, q), -1, jnp.int32)
    off, cnt_thr, cnt_above = lax.fori_loop(0, nbits, bisect, (jnp.zeros((1, q), jnp.int32), uncounted, uncounted))
    thr = base + off
    cnt_thr = lax.cond(jnp.min(cnt_thr) < 0, lambda: count_ge(thr), lambda: cnt_thr)
    tied = jnp.logical_and(cnt_thr > top_k, thr > NEG_INF_KEY)
    any_tie = jnp.max(jnp.where(tied, 1, 0)) > 0

    thr_valid = jnp.maximum(thr, NEG_INF_KEY + 1)

    def mask_fast():
        def body(rows):
            slab_ref[rows, :] = jnp.where(slab_ref[rows, :] >= thr_valid, 0, NEG_INF_BITS)

        lax.fori_loop(0, nwide, lambda jw, c: body(wide_rows(jw)), None)
        lax.fori_loop(n0, nkb, lambda j, c: body(narrow_rows(j)), None)

    def mask_tied():
        above = lax.cond(jnp.min(cnt_above) < 0, lambda: count_ge(thr + 1), lambda: cnt_above)
        thr_c = jnp.maximum(thr, NEG_INF_KEY)
        need = jnp.where(thr_c > NEG_INF_KEY, (top_k - above).astype(F32), -1.0)

        def body(rows, n, seen):
            for g in range(n // kb):
                blk = pl.ds(rows.start + g * kb, kb)
                key = slab_ref[blk, :]
                eq = key == thr_c
                eq_f = jnp.where(eq, 1.0, 0.0)
                before = jnp.dot(tri_ref[...], eq_f.astype(BF16), preferred_element_type=F32)
                tie_bits = jnp.where(seen + before < need, 0, NEG_INF_BITS)
                slab_ref[blk, :] = jnp.where(key > thr_c, 0, jnp.where(eq, tie_bits, NEG_INF_BITS))
                seen = seen + jnp.sum(_fold_keys(eq_f, jnp.sum), axis=0, keepdims=True)
            return seen

        seen = lax.fori_loop(0, nwide // 4, lambda jq, c: body(quad_rows(jq), 4 * wide, c), jnp.zeros((1, q), F32))
        seen = lax.fori_loop((nwide // 4) * 2, nwide // 2, lambda jp, c: body(pair_rows(jp), 2 * wide, c), seen)
        seen = lax.fori_loop((nwide // 2) * 2, nwide, lambda jw, c: body(wide_rows(jw), wide, c), seen)
        lax.fori_loop(n0, nkb, lambda j, c: body(narrow_rows(j), kb, c), seen)

    lax.cond(any_tie, mask_tied, mask_fast)

    m_ref[...] = jnp.full(m_ref.shape, M_INIT, F32)
    l_ref[...] = jnp.zeros(l_ref.shape, F32)
    acc_ref[...] = jnp.zeros(acc_ref.shape, F32)

    def logits(rows, n, buf, rel):
        mask = lax.bitcast_convert_type(slab_ref[rows, :], F32)
        mask2 = jnp.concatenate([mask, mask], axis=1)
        tile_max = []
        for p in range(npair):
            ls = slice(p * LANES, (p + 1) * LANES)
            w_q = jnp.concatenate([qat_ref[(2 * p) * LANES:(2 * p + 1) * LANES, :],
                                   qat_ref[(2 * p + 1) * LANES:(2 * p + 2) * LANES, :]], axis=1)
            s = jnp.dot(ka_ref[rows, ls], w_q, preferred_element_type=F32) + mask2
            if rel is not None:
                s = s + btab_ref[rel, p]
            s_ref[buf, p, 0:n, :] = s
            tile_max.append(_fold_keys(s, jnp.max))
        return tuple(tile_max)

    def attend(rows, n, buf, tile_max):
        for p in range(npair):
            ls = slice(p * LANES, (p + 1) * LANES)
            m_old = m_ref[p]
            m_new = jnp.maximum(m_old, jnp.max(tile_max[p], axis=0, keepdims=True))
            alpha = jnp.exp2(m_old - m_new)
            pexp = jnp.exp2(s_ref[buf, p, 0:n, :] - m_new[0:1])
            m_ref[p] = m_new
            l_ref[p] = alpha * l_ref[p] + _fold_keys(pexp, jnp.sum)
            acc_ref[p] = alpha[0:1] * acc_ref[p] + jnp.dot(vat_ref[ls, rows], pexp.astype(BF16),
                                                           preferred_element_type=F32)

    @pl.when(nwide > 0)
    def _():
        pipelined(0, nwide, lambda jw, buf: logits(wide_rows(jw), wide, buf, None),
                  lambda jw, buf, tile_max: attend(wide_rows(jw), wide, buf, tile_max))

    pipelined(n0, nkb, lambda j, buf: logits(narrow_rows(j), kb, buf, jnp.minimum(i - j, 2)),
              lambda j, buf, tile_max: attend(narrow_rows(j), kb, buf, tile_max))

    for p in range(npair):
        o_t = acc_ref[p] / jnp.sum(l_ref[p], axis=0, keepdims=True)
        o_pair = jnp.concatenate([o_t[:A_HEAD_DIM, :q], o_t[A_HEAD_DIM:, q:]], axis=0).T
        ls = slice(p * LANES, (p + 1) * LANES)
        o_ref[:, ls] = (o_pair * _silu(ga_ref[:, ls])).astype(o_ref.dtype)


def _dsa_bias_table(rel_bias):
    ii = np.arange(DSA_Q)[:, None]
    jj = np.arange(DSA_KB)[None, :]
    buckets = np.stack([_t5_bucket_np(ii - jj), _t5_bucket_np(DSA_KB + ii - jj),
                        np.full((DSA_Q, DSA_KB), REL_BUCKETS - 1)])
    assert (_t5_bucket_np(np.arange(DSA_KB + 1, 4 * DSA_KB)) == REL_BUCKETS - 1).all()
    rb = rel_bias.astype(F32) * LOG2E
    rb = rb - rb[REL_BUCKETS - 1]
    onehot = np.zeros((3, DSA_KB, DSA_Q, REL_BUCKETS), np.float32)
    np.put_along_axis(onehot, np.transpose(buckets, (0, 2, 1))[..., None], 1.0, axis=-1)
    tab = jnp.einsum('tsrb,bh->thsr', jnp.asarray(onehot), rb, precision=lax.Precision.HIGHEST)
    tab = tab.reshape(3, A_HEADS // 2, 2, DSA_KB, DSA_Q)
    return jnp.transpose(tab, (0, 1, 3, 2, 4)).reshape(3, A_HEADS // 2, DSA_KB, 2 * DSA_Q)


def _dsa(qa_t, qi_t, wi_t, ga, ka, va_t, ki, btab, top_k):
    s = ga.shape[0]
    q, kb = DSA_Q, DSA_KB
    tri = jnp.asarray(np.tril(np.ones((kb, kb), np.float32), -1), BF16)
    whole = lambda shape: pl.BlockSpec(shape, lambda i: (0,) * len(shape), pipeline_mode=pl.Buffered(1))
    return pl.pallas_call(
        functools.partial(_dsa_kernel, top_k=top_k),
        grid=(s // q,),
        in_specs=[pl.BlockSpec((A_HEADS * LANES, q), lambda i: (0, i)),
                  pl.BlockSpec((IDX_HEADS * LANES, q), lambda i: (0, i)),
                  pl.BlockSpec((SUBLANES, q), lambda i: (0, i)),
                  pl.BlockSpec((q, A_W), lambda i: (i, 0)),
                  whole((s, A_W)), whole((A_W, s)), whole((s, LANES)),
                  whole((3, A_HEADS // 2, kb, 2 * q)), whole((kb, kb))],
        out_specs=pl.BlockSpec((q, A_W), lambda i: (i, 0)),
        out_shape=jax.ShapeDtypeStruct((s, A_W), BF16),
        scratch_shapes=[pltpu.VMEM((s, q), jnp.int32),
                        pltpu.VMEM((DSA_CELLS, SUBLANES, q), jnp.int32),
                        pltpu.VMEM((2, A_HEADS // 2, DSA_WIDE * kb, 2 * q), F32),
                        pltpu.VMEM((A_HEADS // 2, LANES, 2 * q), F32),
                        pltpu.VMEM((A_HEADS // 2, SUBLANES, 2 * q), F32),
                        pltpu.VMEM((A_HEADS // 2, SUBLANES, 2 * q), F32)],
        compiler_params=pltpu.CompilerParams(dimension_semantics=("arbitrary",), vmem_limit_bytes=VMEM_LIMIT),
        name="dsa",
    )(qa_t, qi_t, wi_t, ga, ka, va_t, ki, btab, tri)


GLA_BLOCK = 512


def _gla_kernel(qb_ref, kb_ref, vb_ref, gb_ref, lr_ref, wlr_ref, blr_ref, gn_ref, tril_ref, o_ref,
                state_ref, b_ref, att_ref):
    c, sub = GLA_CHUNK, GLA_SUB
    nsub = c // sub

    @pl.when(pl.program_id(0) == 0)
    def _():
        state_ref[...] = jnp.zeros(state_ref.shape, F32)

    dn_t = (((1,), (1,)), ((), ()))
    dn_0 = (((0,), (0,)), ((), ()))
    hp = lax.Precision.HIGHEST
    row_i = lax.broadcasted_iota(jnp.int32, (sub, sub), 0)
    col_j = lax.broadcasted_iota(jnp.int32, (sub, sub), 1)

    def chunk(ci, carry):
        r0 = pl.multiple_of(ci * c, c)
        z = jnp.dot(lr_ref[pl.ds(r0, c), :], wlr_ref[...], precision=hp, preferred_element_type=F32) + blr_ref[...]
        log_a = (jnp.minimum(z, 0.0) - jnp.log1p(jnp.exp(-jnp.abs(z)))) * (LOG2E / GLA_TAU)
        b_ref[...] = jnp.dot(tril_ref[...], log_a, precision=hp, preferred_element_type=F32)
        for h in range(B_HEADS):
            ls = slice(h * LANES, (h + 1) * LANES)
            b = b_ref[:, ls]
            qh = qb_ref[pl.ds(r0, c), ls]
            kh = kb_ref[pl.ds(r0, c), ls]
            vh = vb_ref[pl.ds(r0, c), ls]
            state_t = state_ref[h]
            b_last = b[c - 1:c, :]
            o = lax.dot_general((qh * jnp.exp2(b)).astype(BF16), state_t.astype(BF16), dn_t,
                                preferred_element_type=F32)
            for si in range(nsub):
                rs = slice(si * sub, (si + 1) * sub)
                b_i, q_i = b[rs], qh[rs]
                if si > 0:
                    ref_row = b[si * sub:si * sub + 1, :]
                    q_t = (q_i * jnp.exp2(b_i - ref_row)).astype(BF16)
                    k_t = (kh[:si * sub] * jnp.exp2(ref_row - b[:si * sub])).astype(BF16)
                    att_ref[rs, :si * sub] = lax.dot_general(q_t, k_t, dn_t, preferred_element_type=F32)
                k_i = kh[rs]
                diag = jnp.zeros((sub, sub), F32)
                for jl in range(sub):
                    diff = jnp.minimum(b_i - b_i[jl:jl + 1, :], 0.0)
                    col = jnp.sum(q_i * k_i[jl:jl + 1, :] * jnp.exp2(diff), axis=-1, keepdims=True)
                    diag = jnp.where(col_j == jl, col, diag)
                att_ref[rs, rs] = jnp.where(col_j <= row_i, diag, 0.0)
                if si + 1 < nsub:
                    att_ref[rs, (si + 1) * sub:] = jnp.zeros((sub, c - (si + 1) * sub), F32)
            o = o + jnp.dot(att_ref[...].astype(BF16), vh.astype(BF16), preferred_element_type=F32)
            k_dec = (kh * jnp.exp2(b_last - b)).astype(BF16)
            state_ref[h] = (jnp.exp2(b_last) * state_t
                            + lax.dot_general(vh.astype(BF16), k_dec, dn_0, preferred_element_type=F32))
            o = o * lax.rsqrt(jnp.mean(o * o, axis=-1, keepdims=True) + EPS) * gn_ref[...]
            o_ref[pl.ds(r0, c), ls] = (o * _silu(gb_ref[pl.ds(r0, c), ls])).astype(o_ref.dtype)
        return carry

    lax.fori_loop(0, GLA_BLOCK // c, chunk, 0)


def _gla(qb, kb, vb, gb, lr, wlr_pad, blr_pad, gn):
    s = qb.shape[0]
    blk = min(GLA_BLOCK, s)
    assert blk == GLA_BLOCK and s % blk == 0
    tril = jnp.asarray(np.tril(np.ones((GLA_CHUNK, GLA_CHUNK), np.float32)))
    row = lambda wd: pl.BlockSpec((blk, wd), lambda i: (i, 0))
    const = lambda shape: pl.BlockSpec(shape, lambda i: (0,) * len(shape))
    return pl.pallas_call(
        _gla_kernel,
        grid=(s // blk,),
        in_specs=[row(B_HEADS * LANES), row(B_HEADS * LANES), row(B_VW), row(B_VW), row(LANES),
                  const((LANES, B_HEADS * LANES)), const((1, B_HEADS * LANES)), const((1, B_DV)),
                  const((GLA_CHUNK, GLA_CHUNK))],
        out_specs=row(B_VW),
        out_shape=jax.ShapeDtypeStruct((s, B_VW), BF16),
        scratch_shapes=[pltpu.VMEM((B_HEADS, LANES, B_DV), F32),
                        pltpu.VMEM((GLA_CHUNK, B_HEADS * LANES), F32),
                        pltpu.VMEM((GLA_CHUNK, GLA_CHUNK), F32)],
        compiler_params=pltpu.CompilerParams(dimension_semantics=("arbitrary",), vmem_limit_bytes=VMEM_LIMIT),
        name="gla",
    )(qb, kb, vb, gb, lr, wlr_pad, blr_pad, gn, tril)


def _mid_kernel(x_ref, ma_ref, mb_ref, wo_ref, g_ref, wi_ref, x1_ref, u_ref, sg_ref):
    x1 = (x_ref[...]
          + jnp.dot(ma_ref[...], wo_ref[:A_W, :], preferred_element_type=F32)
          + jnp.dot(mb_ref[...], wo_ref[A_W:, :], preferred_element_type=F32))
    x1_ref[...] = x1
    h = (x1 * lax.rsqrt(jnp.mean(x1 * x1, axis=-1, keepdims=True) + EPS) * g_ref[...]).astype(BF16)
    val = jnp.dot(h, wi_ref[:, :D_MODEL], preferred_element_type=F32)
    glu = jnp.dot(h, wi_ref[:, D_MODEL:2 * D_MODEL], preferred_element_type=F32)
    gate = jnp.dot(h, wi_ref[:, 2 * D_MODEL:], preferred_element_type=F32)
    u_ref[...] = val * jax.nn.sigmoid(glu)
    sg_ref[...] = _silu(gate)


def _mid(x2d, mix_a, mix_b, w_out, g, w_in, tm):
    s = x2d.shape[0]
    row = lambda wd: pl.BlockSpec((tm, wd), lambda i: (i, 0))
    const = lambda shape: pl.BlockSpec(shape, lambda i: (0,) * len(shape))
    return pl.pallas_call(
        _mid_kernel,
        grid=(s // tm,),
        in_specs=[row(D_MODEL), row(A_W), row(B_VW), const((A_W + B_VW, D_MODEL)), const((1, D_MODEL)),
                  const((D_MODEL, 3 * D_MODEL))],
        out_specs=[row(D_MODEL), row(D_MODEL), row(D_MODEL)],
        out_shape=[jax.ShapeDtypeStruct((s, D_MODEL), F32)] * 3,
        compiler_params=pltpu.CompilerParams(dimension_semantics=("arbitrary",), vmem_limit_bytes=VMEM_LIMIT),
        name="mid",
    )(x2d, mix_a, mix_b, w_out, g, w_in)


HALO = 32
CONV_ROWS = 128


def _odd_out_kernel(u_ref, halo_ref, sg_ref, x1_ref, cw_ref, cb_ref, lg_ref, lb_ref, wo_ref, nf_ref, o_ref,
                    ubuf_ref, y_ref, shift_ref, *, tm):
    i = pl.program_id(0)
    ubuf_ref[HALO:, :] = u_ref[...]
    ubuf_ref[:HALO, :] = jnp.where(i > 0, halo_ref[...], 0.0)
    base = HALO - (CONV_WIDTH - 1)
    for cs in range(D_MODEL // LANES):
        ls = slice(cs * LANES, (cs + 1) * LANES)
        for r0 in range(0, tm, CONV_ROWS):
            acc = jnp.zeros((CONV_ROWS, LANES), F32)
            for res in range(SUBLANES):
                taps = [k for k in range(CONV_WIDTH) if (base + k) % SUBLANES == res]
                first = base + taps[0]
                rows = taps[-1] - taps[0] + CONV_ROWS
                shift_ref[0:rows, :] = ubuf_ref[r0 + first:r0 + first + rows, ls]
                aligned = shift_ref[0:rows, :]
                for k in taps:
                    a = base + k - first
                    acc = acc + aligned[a:a + CONV_ROWS] * cw_ref[k:k + 1, ls]
            y_ref[r0:r0 + CONV_ROWS, ls] = acc + cb_ref[:, ls]
    y = y_ref[...]
    mu = jnp.mean(y, axis=-1, keepdims=True)
    yc = y - mu
    var = jnp.mean(yc * yc, axis=-1, keepdims=True)
    yn = yc * lax.rsqrt(var + EPS) * lg_ref[...] + lb_ref[...]
    mixed = (_silu(yn) * sg_ref[...]).astype(BF16)
    x2 = x1_ref[...] + jnp.dot(mixed, wo_ref[...], preferred_element_type=F32)
    o_ref[...] = x2 * lax.rsqrt(jnp.mean(x2 * x2, axis=-1, keepdims=True) + EPS) * nf_ref[...]


def _odd_out(u, sg, x1, cw_pad, cb, lg, lb, w_out, nf, tm):
    s = u.shape[0]
    row = pl.BlockSpec((tm, D_MODEL), lambda i: (i, 0))
    const = lambda shape: pl.BlockSpec(shape, lambda i: (0,) * len(shape))
    halo = pl.BlockSpec((HALO, D_MODEL), lambda i: (jnp.maximum(i * (tm // HALO) - 1, 0), 0))
    return pl.pallas_call(
        functools.partial(_odd_out_kernel, tm=tm),
        grid=(s // tm,),
        in_specs=[row, halo, row, row, const((HALO, D_MODEL)), const((1, D_MODEL)), const((1, D_MODEL)),
                  const((1, D_MODEL)), const((D_MODEL, D_MODEL)), const((1, D_MODEL))],
        out_specs=row,
        out_shape=jax.ShapeDtypeStruct((s, D_MODEL), F32),
        scratch_shapes=[pltpu.VMEM((HALO + tm, D_MODEL), F32), pltpu.VMEM((tm, D_MODEL), F32),
                        pltpu.VMEM((HALO + CONV_ROWS, LANES), F32)],
        compiler_params=pltpu.CompilerParams(dimension_semantics=("arbitrary",), vmem_limit_bytes=VMEM_LIMIT),
        name="odd_out",
    )(u, u, sg, x1, cw_pad, cb, lg, lb, w_out, nf)


def kernel(x, rel_bias, norm_even, w_in_even, w_gla_lr, b_gla_lr, gla_norm, w_out_even, norm_odd, w_in_odd,
           conv_w, conv_b, conv_ln_g, conv_ln_b, w_out_odd, norm_f):
    bsz, s, d = x.shape
    assert bsz == 1 and d == D_MODEL and s % GLA_BLOCK == 0
    assert norm_even.shape[0] == 1 and norm_odd.shape[0] == 1
    top_k = min(TOPK_MAX, s // 4)
    tm = ROW_BLOCK
    x2d = x.reshape(s, d)

    wt_pad = _even_in_weights(w_in_even[0])
    qa_t, ka, va_t, ga, qi_t, ki, wi_t, qb, kb, vb, gb, lr = _even_in(x2d, norm_even[0].reshape(1, d), wt_pad, tm)
    mix_a = _dsa(qa_t, qi_t, wi_t, ga, ka, va_t, ki, _dsa_bias_table(rel_bias), top_k)

    head_pad = ((0, 0), (0, 0), (0, LANES - B_DK))
    wlr_pad = jnp.pad(w_gla_lr[0].reshape(GLA_RANK, B_HEADS, B_DK), head_pad).reshape(GLA_RANK, B_HEADS * LANES)
    wlr_pad = jnp.pad(wlr_pad, ((0, LANES - GLA_RANK), (0, 0)))
    blr_pad = jnp.pad(b_gla_lr[0].reshape(1, B_HEADS, B_DK), head_pad).reshape(1, B_HEADS * LANES)
    mix_b = _gla(qb, kb, vb, gb, lr, wlr_pad, blr_pad, gla_norm[0].reshape(1, B_DV))

    x1, u, sg = _mid(x2d, mix_a, mix_b, w_out_even[0].astype(BF16), norm_odd[0].reshape(1, d),
                     w_in_odd[0].astype(BF16), tm)

    cw_pad = jnp.zeros((HALO, d), F32).at[:CONV_WIDTH].set(conv_w[0])
    out = _odd_out(u, sg, x1, cw_pad, conv_b[0].reshape(1, d), conv_ln_g[0].reshape(1, d),
                   conv_ln_b[0].reshape(1, d), w_out_odd[0].astype(BF16), norm_f.reshape(1, d), tm)
    return out.reshape(bsz, s, d)
```

```python
import functools
import math

import jax
import jax.numpy as jnp
import numpy as np
from jax import lax
from jax.experimental import pallas as pl
from jax.experimental.pallas import tpu as pltpu

D_MODEL = 1024
A_HEADS = 8
A_HEAD_DIM = 64
IDX_HEADS = 4
IDX_DIM = 64
TOPK_MAX = 256
B_HEADS = 4
B_DK = 64
B_DV = 128
GLA_RANK = 16
GLA_TAU = 16.0
GLA_CHUNK = 64
GLA_SUB = 16
CONV_WIDTH = 31
REL_BUCKETS = 32
REL_MAX_EXACT = 16
REL_MAX_DIST = 128
EPS = 1e-6

A_W = A_HEADS * A_HEAD_DIM
B_KW = B_HEADS * B_DK
B_VW = B_HEADS * B_DV

LANES = 128
SUBLANES = 8
VMEM_LIMIT = 56 * 1024 * 1024
ROW_BLOCK = 512

F32 = jnp.float32
BF16 = jnp.bfloat16
NEG_INF = float("-inf")
INT_MIN = -(2 ** 31)
NEG_INF_BITS = int(np.array(-np.inf, np.float32).view(np.int32))
NEG_INF_KEY = NEG_INF_BITS ^ 0x7FFFFFFF
M_INIT = -1e30
LOG2E = math.log2(math.e)

C_QA = 0
C_KA = C_QA + A_HEADS * LANES
C_VA = C_KA + A_W
C_GA = C_VA + A_W
C_QI = C_GA + A_W
C_KI = C_QI + IDX_HEADS * LANES
C_WI = C_KI + LANES
C_QB = C_WI + LANES
C_KB = C_QB + B_HEADS * LANES
C_VB = C_KB + B_HEADS * LANES
C_GB = C_VB + B_VW
C_LR = C_GB + B_VW
C_END = C_LR + LANES


def _t5_bucket_np(dist):
    dist = np.maximum(dist, 0)
    d = np.maximum(dist, 1).astype(np.float32)
    large = REL_MAX_EXACT + (np.log(d / REL_MAX_EXACT) / math.log(REL_MAX_DIST / REL_MAX_EXACT)
                             * (REL_BUCKETS - REL_MAX_EXACT)).astype(np.int32)
    large = np.minimum(large, REL_BUCKETS - 1)
    return np.where(dist < REL_MAX_EXACT, dist, large)


def _silu(x):
    return x * jax.nn.sigmoid(x)


def _even_in_kernel(x_ref, g_ref, wt_ref, qat_ref, ka_ref, vat_ref, ga_ref, qit_ref, ki_ref, wit_ref,
                    qb_ref, kb_ref, vb_ref, gb_ref, lr_ref):
    x = x_ref[...]
    h = (x * lax.rsqrt(jnp.mean(x * x, axis=-1, keepdims=True) + EPS) * g_ref[...]).astype(BF16)
    dn_t = (((1,), (1,)), ((), ()))

    def proj(lo, hi):
        return lax.dot_general(h, wt_ref[lo:hi, :], dn_t, preferred_element_type=F32)

    def proj_t(lo, hi):
        return lax.dot_general(wt_ref[lo:hi, :], h, dn_t, preferred_element_type=F32)

    qat_ref[...] = proj_t(C_QA, C_KA).astype(BF16)
    ka_ref[...] = proj(C_KA, C_VA).astype(BF16)
    vat_ref[...] = proj_t(C_VA, C_GA).astype(BF16)
    ga_ref[...] = proj(C_GA, C_QI)
    qit_ref[...] = proj_t(C_QI, C_KI).astype(BF16)
    ki_ref[...] = proj(C_KI, C_WI).astype(BF16)
    wit_ref[...] = proj_t(C_WI, C_WI + SUBLANES)
    qb_ref[...] = proj(C_QB, C_KB)
    kb_ref[...] = proj(C_KB, C_VB)
    vb_ref[...] = proj(C_VB, C_GB)
    gb_ref[...] = proj(C_GB, C_LR)
    lr_ref[...] = proj(C_LR, C_END)


def _even_in_layout():
    cols = np.cumsum([0, A_W, A_W, A_W, A_W, IDX_HEADS * IDX_DIM, IDX_DIM, IDX_HEADS,
                      B_KW, B_KW, B_VW, B_VW, GLA_RANK])
    (s_qa, s_ka, s_va, s_ga, s_qi, s_ki, s_wi, s_qb, s_kb, s_vb, s_gb, s_lr, _) = [int(c) for c in cols]
    src = np.full((C_END,), -1, np.int32)
    scale = np.ones((C_END,), np.float32)

    def put(dst, start, width, sc=1.0):
        src[dst:dst + width] = np.arange(start, start + width)
        scale[dst:dst + width] = sc

    for h in range(A_HEADS):
        put(C_QA + h * LANES + (h % 2) * A_HEAD_DIM, s_qa + h * A_HEAD_DIM, A_HEAD_DIM, (A_HEAD_DIM ** -0.5) * LOG2E)
    put(C_KA, s_ka, A_W)
    put(C_VA, s_va, A_W)
    put(C_GA, s_ga, A_W)
    for h in range(IDX_HEADS):
        put(C_QI + h * LANES, s_qi + h * IDX_DIM, IDX_DIM)
    put(C_KI, s_ki, IDX_DIM)
    put(C_WI, s_wi, IDX_HEADS, (IDX_DIM ** -0.5) * (IDX_HEADS ** -0.5))
    for h in range(B_HEADS):
        put(C_QB + h * LANES, s_qb + h * B_DK, B_DK, B_DK ** -0.5)
        put(C_KB + h * LANES, s_kb + h * B_DK, B_DK)
    put(C_VB, s_vb, B_VW)
    put(C_GB, s_gb, B_VW)
    put(C_LR, s_lr, GLA_RANK)
    return src, scale


def _even_in_weights(w):
    src, scale = _even_in_layout()
    pieces, start = [], 0
    for end in range(1, C_END + 1):
        if end == C_END or (src[end] >= 0) != (src[start] >= 0) or (src[end] >= 0 and src[end] != src[end - 1] + 1):
            if src[start] >= 0:
                piece = w[:, int(src[start]):int(src[end - 1]) + 1] * jnp.asarray(scale[start:end])[None, :]
                pieces.append(piece.astype(BF16))
            else:
                pieces.append(jnp.zeros((D_MODEL, end - start), BF16))
            start = end
    return jnp.concatenate(pieces, axis=1).T


def _even_in(x2d, g, wt_pad, tm):
    s = x2d.shape[0]
    row = lambda wd, dt: (pl.BlockSpec((tm, wd), lambda i: (i, 0)), jax.ShapeDtypeStruct((s, wd), dt))
    col = lambda ht, dt: (pl.BlockSpec((ht, tm), lambda i: (0, i)), jax.ShapeDtypeStruct((ht, s), dt))
    outs = [col(C_KA - C_QA, BF16), row(A_W, BF16), col(A_W, BF16), row(A_W, F32), col(C_KI - C_QI, BF16),
            row(LANES, BF16), col(SUBLANES, F32), row(C_KB - C_QB, F32), row(C_VB - C_KB, F32), row(B_VW, F32),
            row(B_VW, F32), row(LANES, F32)]
    return pl.pallas_call(
        _even_in_kernel,
        grid=(s // tm,),
        in_specs=[pl.BlockSpec((tm, D_MODEL), lambda i: (i, 0)),
                  pl.BlockSpec((1, D_MODEL), lambda i: (0, 0)),
                  pl.BlockSpec((C_END, D_MODEL), lambda i: (0, 0))],
        out_specs=[spec for spec, _ in outs],
        out_shape=[shape for _, shape in outs],
        compiler_params=pltpu.CompilerParams(dimension_semantics=("arbitrary",), vmem_limit_bytes=VMEM_LIMIT),
        name="even_in",
    )(x2d, g, wt_pad)


DSA_Q = 128
DSA_KB = 128
DSA_WIDE = 4
DSA_NACC = 4
DSA_CELLS = 64
DSA_CELL_BITS = 14


def _f32_key(x):
    bits = lax.bitcast_convert_type(x, jnp.int32)
    return bits ^ (lax.shift_right_arithmetic(bits, 31) & 0x7FFFFFFF)


def _fold_keys(x, op):
    n, w = x.shape
    group = DSA_NACC * SUBLANES
    if n % group == 0 and n > group:
        x = op(x.reshape(n // group, DSA_NACC, SUBLANES, w), axis=0)
        return op(x, axis=0)
    return op(x.reshape(n // SUBLANES, SUBLANES, w), axis=0)


def _dsa_kernel(qat_ref, qit_ref, wit_ref, ga_ref, ka_ref, vat_ref, ki_ref, btab_ref, tri_ref, o_ref,
                slab_ref, cell_ref, s_ref, acc_ref, m_ref, l_ref, *, top_k):
    q, kb, wide = DSA_Q, DSA_KB, DSA_WIDE * DSA_KB
    i = pl.program_id(0)
    nkb = i + 1
    nwide = jnp.maximum(i - 1, 0) // DSA_WIDE
    n0 = nwide * DSA_WIDE
    t_row = i * q + lax.broadcasted_iota(jnp.int32, (1, q), 1)
    key_off = lax.broadcasted_iota(jnp.int32, (kb, 1), 0)
    npair = A_HEADS // 2

    def wide_rows(jw):
        return pl.ds(pl.multiple_of(jw * wide, wide), wide)

    def narrow_rows(j):
        return pl.ds(pl.multiple_of(j * kb, kb), kb)

    def pipelined(first, last, produce, consume):
        def body(u, carry):
            t = first + 2 * u
            consume(t, 0, carry)
            consume(t + 1, 1, produce(t + 1, 1))
            return produce(t + 2, 0)

        npairs = (last - first - 1) // 2
        carry = lax.fori_loop(0, npairs, body, produce(first, 0))
        t = first + 2 * npairs
        consume(t, 0, carry)

        @pl.when(t + 1 < last)
        def _():
            consume(t + 1, 1, produce(t + 1, 1))

    w_head = [wit_ref[h:h + 1, :] for h in range(IDX_HEADS)]

    def scores(rows):
        k_idx = ki_ref[rows, :]
        sc = None
        for h in range(IDX_HEADS):
            s = jnp.dot(k_idx, qit_ref[h * LANES:(h + 1) * LANES, :], preferred_element_type=F32)
            term = w_head[h] * jnp.maximum(s, 0.0)
            sc = term if sc is None else sc + term
        return sc

    assert wide % (DSA_CELLS * SUBLANES) == 0
    cell_ref[...] = jnp.full(cell_ref.shape, INT_MIN, jnp.int32)

    def score_rows(rows, n):
        key = _f32_key(scores(rows))
        slab_ref[rows, :] = key
        cells = jnp.max(key.reshape(n // (DSA_CELLS * SUBLANES), DSA_CELLS, SUBLANES, q), axis=0)
        cell_ref[...] = jnp.maximum(cell_ref[...], cells)

    def pair_rows(jp):
        return pl.ds(pl.multiple_of(jp * (2 * wide), 2 * wide), 2 * wide)

    def score_narrow(j, c):
        rows = narrow_rows(j)
        key = _f32_key(jnp.where(j * kb + key_off <= t_row, scores(rows), NEG_INF))
        slab_ref[rows, :] = key
        ncls = kb // SUBLANES
        cell_ref[0:ncls] = jnp.maximum(cell_ref[0:ncls], key.reshape(ncls, SUBLANES, q))
        return c

    def quad_rows(jq):
        return pl.ds(pl.multiple_of(jq * (4 * wide), 4 * wide), 4 * wide)

    lax.fori_loop(0, nwide // 4, lambda jq, c: score_rows(quad_rows(jq), 4 * wide), None)
    lax.fori_loop((nwide // 4) * 2, nwide // 2, lambda jp, c: score_rows(pair_rows(jp), 2 * wide), None)
    lax.fori_loop((nwide // 2) * 2, nwide, lambda jw, c: score_rows(wide_rows(jw), wide), None)
    lax.fori_loop(n0, nkb, score_narrow, 0)

    def count_ge(trial):
        def cnt(blk):
            return _fold_keys(jnp.where(blk >= trial, 1.0, 0.0), jnp.sum)

        ncw = nkb // DSA_WIDE
        acc = lax.fori_loop(0, ncw // 4, lambda jq, a: a + cnt(slab_ref[quad_rows(jq), :]),
                            jnp.zeros((SUBLANES, q), F32))
        acc = lax.fori_loop((ncw // 4) * 2, ncw // 2, lambda jp, a: a + cnt(slab_ref[pair_rows(jp), :]), acc)
        acc = lax.fori_loop((ncw // 2) * 2, ncw, lambda jw, a: a + cnt(slab_ref[wide_rows(jw), :]), acc)
        acc = lax.fori_loop(ncw * DSA_WIDE, nkb, lambda j, a: a + cnt(slab_ref[narrow_rows(j), :]), acc)
        return jnp.sum(acc, axis=0, keepdims=True).astype(jnp.int32)

    assert DSA_CELLS * SUBLANES >= top_k
    cell_max = cell_ref[...]
    hi_key = jnp.max(jnp.max(cell_max, axis=0), axis=0, keepdims=True)

    def cell_bisect(b, c):
        trial = c + lax.shift_left(jnp.int32(1), 31 - b)
        cnt = jnp.sum(jnp.sum(jnp.where(cell_max >= trial, 1.0, 0.0), axis=0), axis=0, keepdims=True)
        return jnp.where(cnt >= top_k, trial, c)

    lo_key = lax.fori_loop(0, DSA_CELL_BITS, cell_bisect, jnp.full((1, q), INT_MIN, jnp.int32))
    base = lo_key
    span = hi_key - lo_key
    nbits = jnp.max(32 - lax.clz(span))

    def bisect(b, state):
        off, cnt, above = state
        trial_off = off + lax.shift_left(jnp.int32(1), nbits - 1 - b)
        in_span = (trial_off ^ INT_MIN) <= (span ^ INT_MIN)
        c = jnp.where(in_span, count_ge(base + trial_off), 0)
        ok = c >= top_k
        return jnp.where(ok, trial_off, off), jnp.where(ok, c, cnt), jnp.where(ok, above, c)

    uncounted = jnp.full((1, q), -1, jnp.int32)
    off, cnt_thr, cnt_above = lax.fori_loop(0, nbits, bisect, (jnp.zeros((1, q), jnp.int32), uncounted, uncounted))
    thr = base + off
    cnt_thr = lax.cond(jnp.min(cnt_thr) < 0, lambda: count_ge(thr), lambda: cnt_thr)
    tied = jnp.logical_and(cnt_thr > top_k, thr > NEG_INF_KEY)
    any_tie = jnp.max(jnp.where(tied, 1, 0)) > 0

    thr_valid = jnp.maximum(thr, NEG_INF_KEY + 1)

    def mask_fast():
        def body(rows):
            slab_ref[rows, :] = jnp.where(slab_ref[rows, :] >= thr_valid, 0, NEG_INF_BITS)

        lax.fori_loop(0, nwide, lambda jw, c: body(wide_rows(jw)), None)
        lax.fori_loop(n0, nkb, lambda j, c: body(narrow_rows(j)), None)

    def mask_tied():
        above = lax.cond(jnp.min(cnt_above) < 0, lambda: count_ge(thr + 1), lambda: cnt_above)
        thr_c = jnp.maximum(thr, NEG_INF_KEY)
        need = jnp.where(thr_c > NEG_INF_KEY, (top_k - above).astype(F32), -1.0)

        def body(rows, n, seen):
            for g in range(n // kb):
                blk = pl.ds(rows.start + g * kb, kb)
                key = slab_ref[blk, :]
                eq = key == thr_c
                eq_f = jnp.where(eq, 1.0, 0.0)
                before = jnp.dot(tri_ref[...], eq_f.astype(BF16), preferred_element_type=F32)
                tie_bits = jnp.where(seen + before < need, 0, NEG_INF_BITS)
                slab_ref[blk, :] = jnp.where(key > thr_c, 0, jnp.where(eq, tie_bits, NEG_INF_BITS))
                seen = seen + jnp.sum(_fold_keys(eq_f, jnp.sum), axis=0, keepdims=True)
            return seen

        seen = lax.fori_loop(0, nwide // 4, lambda jq, c: body(quad_rows(jq), 4 * wide, c), jnp.zeros((1, q), F32))
        seen = lax.fori_loop((nwide // 4) * 2, nwide // 2, lambda jp, c: body(pair_rows(jp), 2 * wide, c), seen)
        seen = lax.fori_loop((nwide // 2) * 2, nwide, lambda jw, c: body(wide_rows(jw), wide, c), seen)
        lax.fori_loop(n0, nkb, lambda j, c: body(narrow_rows(j), kb, c), seen)

    lax.cond(any_tie, mask_tied, mask_fast)

    m_ref[...] = jnp.full(m_ref.shape, M_INIT, F32)
    l_ref[...] = jnp.zeros(l_ref.shape, F32)
    acc_ref[...] = jnp.zeros(acc_ref.shape, F32)

    def logits(rows, n, buf, rel):
        mask = lax.bitcast_convert_type(slab_ref[rows, :], F32)
        mask2 = jnp.concatenate([mask, mask], axis=1)
        tile_max = []
        for p in range(npair):
            ls = slice(p * LANES, (p + 1) * LANES)
            w_q = jnp.concatenate([qat_ref[(2 * p) * LANES:(2 * p + 1) * LANES, :],
                                   qat_ref[(2 * p + 1) * LANES:(2 * p + 2) * LANES, :]], axis=1)
            s = jnp.dot(ka_ref[rows, ls], w_q, preferred_element_type=F32) + mask2
            if rel is not None:
                s = s + btab_ref[rel, p]
            s_ref[buf, p, 0:n, :] = s
            tile_max.append(_fold_keys(s, jnp.max))
        return tuple(tile_max)

    def attend(rows, n, buf, tile_max):
        for p in range(npair):
            ls = slice(p * LANES, (p + 1) * LANES)
            m_old = m_ref[p]
            m_new = jnp.maximum(m_old, jnp.max(tile_max[p], axis=0, keepdims=True))
            alpha = jnp.exp2(m_old - m_new)
            pexp = jnp.exp2(s_ref[buf, p, 0:n, :] - m_new[0:1])
            m_ref[p] = m_new
            l_ref[p] = alpha * l_ref[p] + _fold_keys(pexp, jnp.sum)
            acc_ref[p] = alpha[0:1] * acc_ref[p] + jnp.dot(vat_ref[ls, rows], pexp.astype(BF16),
                                                           preferred_element_type=F32)

    @pl.when(nwide > 0)
    def _():
        pipelined(0, nwide, lambda jw, buf: logits(wide_rows(jw), wide, buf, None),
                  lambda jw, buf, tile_max: attend(wide_rows(jw), wide, buf, tile_max))

    pipelined(n0, nkb, lambda j, buf: logits(narrow_rows(j), kb, buf, jnp.minimum(i - j, 2)),
              lambda j, buf, tile_max: attend(narrow_rows(j), kb, buf, tile_max))

    for p in range(npair):
        o_t = acc_ref[p] / jnp.sum(l_ref[p], axis=0, keepdims=True)
        o_pair = jnp.concatenate([o_t[:A_HEAD_DIM, :q], o_t[A_HEAD_DIM:, q:]], axis=0).T
        ls = slice(p * LANES, (p + 1) * LANES)
        o_ref[:, ls] = (o_pair * _silu(ga_ref[:, ls])).astype(o_ref.dtype)


def _dsa_bias_table(rel_bias):
    ii = np.arange(DSA_Q)[:, None]
    jj = np.arange(DSA_KB)[None, :]
    buckets = np.stack([_t5_bucket_np(ii - jj), _t5_bucket_np(DSA_KB + ii - jj),
                        np.full((DSA_Q, DSA_KB), REL_BUCKETS - 1)])
    assert (_t5_bucket_np(np.arange(DSA_KB + 1, 4 * DSA_KB)) == REL_BUCKETS - 1).all()
    rb = rel_bias.astype(F32) * LOG2E
    rb = rb - rb[REL_BUCKETS - 1]
    onehot = np.zeros((3, DSA_KB, DSA_Q, REL_BUCKETS), np.float32)
    np.put_along_axis(onehot, np.transpose(buckets, (0, 2, 1))[..., None], 1.0, axis=-1)
    tab = jnp.einsum('tsrb,bh->thsr', jnp.asarray(onehot), rb, precision=lax.Precision.HIGHEST)
    tab = tab.reshape(3, A_HEADS // 2, 2, DSA_KB, DSA_Q)
    return jnp.transpose(tab, (0, 1, 3, 2, 4)).reshape(3, A_HEADS // 2, DSA_KB, 2 * DSA_Q)


def _dsa(qa_t, qi_t, wi_t, ga, ka, va_t, ki, btab, top_k):
    s = ga.shape[0]
    q, kb = DSA_Q, DSA_KB
    tri = jnp.asarray(np.tril(np.ones((kb, kb), np.float32), -1), BF16)
    whole = lambda shape: pl.BlockSpec(shape, lambda i: (0,) * len(shape), pipeline_mode=pl.Buffered(1))
    return pl.pallas_call(
        functools.partial(_dsa_kernel, top_k=top_k),
        grid=(s // q,),
        in_specs=[pl.BlockSpec((A_HEADS * LANES, q), lambda i: (0, i)),
                  pl.BlockSpec((IDX_HEADS * LANES, q), lambda i: (0, i)),
                  pl.BlockSpec((SUBLANES, q), lambda i: (0, i)),
                  pl.BlockSpec((q, A_W), lambda i: (i, 0)),
                  whole((s, A_W)), whole((A_W, s)), whole((s, LANES)),
                  whole((3, A_HEADS // 2, kb, 2 * q)), whole((kb, kb))],
        out_specs=pl.BlockSpec((q, A_W), lambda i: (i, 0)),
        out_shape=jax.ShapeDtypeStruct((s, A_W), BF16),
        scratch_shapes=[pltpu.VMEM((s, q), jnp.int32),
                        pltpu.VMEM((DSA_CELLS, SUBLANES, q), jnp.int32),
                        pltpu.VMEM((2, A_HEADS // 2, DSA_WIDE * kb, 2 * q), F32),
                        pltpu.VMEM((A_HEADS // 2, LANES, 2 * q), F32),
                        pltpu.VMEM((A_HEADS // 2, SUBLANES, 2 * q), F32),
                        pltpu.VMEM((A_HEADS // 2, SUBLANES, 2 * q), F32)],
        compiler_params=pltpu.CompilerParams(dimension_semantics=("arbitrary",), vmem_limit_bytes=VMEM_LIMIT),
        name="dsa",
    )(qa_t, qi_t, wi_t, ga, ka, va_t, ki, btab, tri)


GLA_BLOCK = 512


def _gla_kernel(qb_ref, kb_ref, vb_ref, gb_ref, lr_ref, wlr_ref, blr_ref, gn_ref, tril_ref, o_ref,
                state_ref, b_ref, att_ref):
    c, sub = GLA_CHUNK, GLA_SUB
    nsub = c // sub

    @pl.when(pl.program_id(0) == 0)
    def _():
        state_ref[...] = jnp.zeros(state_ref.shape, F32)

    dn_t = (((1,), (1,)), ((), ()))
    dn_0 = (((0,), (0,)), ((), ()))
    hp = lax.Precision.HIGHEST
    row_i = lax.broadcasted_iota(jnp.int32, (sub, sub), 0)
    col_j = lax.broadcasted_iota(jnp.int32, (sub, sub), 1)

    def chunk(ci, carry):
        r0 = pl.multiple_of(ci * c, c)
        z = jnp.dot(lr_ref[pl.ds(r0, c), :], wlr_ref[...], precision=hp, preferred_element_type=F32) + blr_ref[...]
        log_a = (jnp.minimum(z, 0.0) - jnp.log1p(jnp.exp(-jnp.abs(z)))) * (LOG2E / GLA_TAU)
        b_ref[...] = jnp.dot(tril_ref[...], log_a, precision=hp, preferred_element_type=F32)
        for h in range(B_HEADS):
            ls = slice(h * LANES, (h + 1) * LANES)
            b = b_ref[:, ls]
            qh = qb_ref[pl.ds(r0, c), ls]
            kh = kb_ref[pl.ds(r0, c), ls]
            vh = vb_ref[pl.ds(r0, c), ls]
            state_t = state_ref[h]
            b_last = b[c - 1:c, :]
            o = lax.dot_general((qh * jnp.exp2(b)).astype(BF16), state_t.astype(BF16), dn_t,
                                preferred_element_type=F32)
            for si in range(nsub):
                rs = slice(si * sub, (si + 1) * sub)
                b_i, q_i = b[rs], qh[rs]
                if si > 0:
                    ref_row = b[si * sub:si * sub + 1, :]
                    q_t = (q_i * jnp.exp2(b_i - ref_row)).astype(BF16)
                    k_t = (kh[:si * sub] * jnp.exp2(ref_row - b[:si * sub])).astype(BF16)
                    att_ref[rs, :si * sub] = lax.dot_general(q_t, k_t, dn_t, preferred_element_type=F32)
                k_i = kh[rs]
                diag = jnp.zeros((sub, sub), F32)
                for jl in range(sub):
                    diff = jnp.minimum(b_i - b_i[jl:jl + 1, :], 0.0)
                    col = jnp.sum(q_i * k_i[jl:jl + 1, :] * jnp.exp2(diff), axis=-1, keepdims=True)
                    diag = jnp.where(col_j == jl, col, diag)
                att_ref[rs, rs] = jnp.where(col_j <= row_i, diag, 0.0)
                if si + 1 < nsub:
                    att_ref[rs, (si + 1) * sub:] = jnp.zeros((sub, c - (si + 1) * sub), F32)
            o = o + jnp.dot(att_ref[...].astype(BF16), vh.astype(BF16), preferred_element_type=F32)
            k_dec = (kh * jnp.exp2(b_last - b)).astype(BF16)
            state_ref[h] = (jnp.exp2(b_last) * state_t
                            + lax.dot_general(vh.astype(BF16), k_dec, dn_0, preferred_element_type=F32))
            o = o * lax.rsqrt(jnp.mean(o * o, axis=-1, keepdims=True) + EPS) * gn_ref[...]
            o_ref[pl.ds(r0, c), ls] = (o * _silu(gb_ref[pl.ds(r0, c), ls])).astype(o_ref.dtype)
        return carry

    lax.fori_loop(0, GLA_BLOCK // c, chunk, 0)


def _gla(qb, kb, vb, gb, lr, wlr_pad, blr_pad, gn):
    s = qb.shape[0]
    blk = min(GLA_BLOCK, s)
    assert blk == GLA_BLOCK and s % blk == 0
    tril = jnp.asarray(np.tril(np.ones((GLA_CHUNK, GLA_CHUNK), np.float32)))
    row = lambda wd: pl.BlockSpec((blk, wd), lambda i: (i, 0))
    const = lambda shape: pl.BlockSpec(shape, lambda i: (0,) * len(shape))
    return pl.pallas_call(
        _gla_kernel,
        grid=(s // blk,),
        in_specs=[row(B_HEADS * LANES), row(B_HEADS * LANES), row(B_VW), row(B_VW), row(LANES),
                  const((LANES, B_HEADS * LANES)), const((1, B_HEADS * LANES)), const((1, B_DV)),
                  const((GLA_CHUNK, GLA_CHUNK))],
        out_specs=row(B_VW),
        out_shape=jax.ShapeDtypeStruct((s, B_VW), BF16),
        scratch_shapes=[pltpu.VMEM((B_HEADS, LANES, B_DV), F32),
                        pltpu.VMEM((GLA_CHUNK, B_HEADS * LANES), F32),
                        pltpu.VMEM((GLA_CHUNK, GLA_CHUNK), F32)],
        compiler_params=pltpu.CompilerParams(dimension_semantics=("arbitrary",), vmem_limit_bytes=VMEM_LIMIT),
        name="gla",
    )(qb, kb, vb, gb, lr, wlr_pad, blr_pad, gn, tril)


def _mid_kernel(x_ref, ma_ref, mb_ref, wo_ref, g_ref, wi_ref, x1_ref, u_ref, sg_ref):
    x1 = (x_ref[...]
          + jnp.dot(ma_ref[...], wo_ref[:A_W, :], preferred_element_type=F32)
          + jnp.dot(mb_ref[...], wo_ref[A_W:, :], preferred_element_type=F32))
    x1_ref[...] = x1
    h = (x1 * lax.rsqrt(jnp.mean(x1 * x1, axis=-1, keepdims=True) + EPS) * g_ref[...]).astype(BF16)
    val = jnp.dot(h, wi_ref[:, :D_MODEL], preferred_element_type=F32)
    glu = jnp.dot(h, wi_ref[:, D_MODEL:2 * D_MODEL], preferred_element_type=F32)
    gate = jnp.dot(h, wi_ref[:, 2 * D_MODEL:], preferred_element_type=F32)
    u_ref[...] = val * jax.nn.sigmoid(glu)
    sg_ref[...] = _silu(gate)


def _mid(x2d, mix_a, mix_b, w_out, g, w_in, tm):
    s = x2d.shape[0]
    row = lambda wd: pl.BlockSpec((tm, wd), lambda i: (i, 0))
    const = lambda shape: pl.BlockSpec(shape, lambda i: (0,) * len(shape))
    return pl.pallas_call(
        _mid_kernel,
        grid=(s // tm,),
        in_specs=[row(D_MODEL), row(A_W), row(B_VW), const((A_W + B_VW, D_MODEL)), const((1, D_MODEL)),
                  const((D_MODEL, 3 * D_MODEL))],
        out_specs=[row(D_MODEL), row(D_MODEL), row(D_MODEL)],
        out_shape=[jax.ShapeDtypeStruct((s, D_MODEL), F32)] * 3,
        compiler_params=pltpu.CompilerParams(dimension_semantics=("arbitrary",), vmem_limit_bytes=VMEM_LIMIT),
        name="mid",
    )(x2d, mix_a, mix_b, w_out, g, w_in)


HALO = 32
CONV_ROWS = 128


def _odd_out_kernel(u_ref, halo_ref, sg_ref, x1_ref, cw_ref, cb_ref, lg_ref, lb_ref, wo_ref, nf_ref, o_ref,
                    ubuf_ref, y_ref, shift_ref, *, tm):
    i = pl.program_id(0)
    ubuf_ref[HALO:, :] = u_ref[...]
    ubuf_ref[:HALO, :] = jnp.where(i > 0, halo_ref[...], 0.0)
    base = HALO - (CONV_WIDTH - 1)
    for cs in range(D_MODEL // LANES):
        ls = slice(cs * LANES, (cs + 1) * LANES)
        for r0 in range(0, tm, CONV_ROWS):
            acc = jnp.zeros((CONV_ROWS, LANES), F32)
            for res in range(SUBLANES):
                taps = [k for k in range(CONV_WIDTH) if (base + k) % SUBLANES == res]
                first = base + taps[0]
                rows = taps[-1] - taps[0] + CONV_ROWS
                shift_ref[0:rows, :] = ubuf_ref[r0 + first:r0 + first + rows, ls]
                aligned = shift_ref[0:rows, :]
                for k in taps:
                    a = base + k - first
                    acc = acc + aligned[a:a + CONV_ROWS] * cw_ref[k:k + 1, ls]
            y_ref[r0:r0 + CONV_ROWS, ls] = acc + cb_ref[:, ls]
    y = y_ref[...]
    mu = jnp.mean(y, axis=-1, keepdims=True)
    yc = y - mu
    var = jnp.mean(yc * yc, axis=-1, keepdims=True)
    yn = yc * lax.rsqrt(var + EPS) * lg_ref[...] + lb_ref[...]
    mixed = (_silu(yn) * sg_ref[...]).astype(BF16)
    x2 = x1_ref[...] + jnp.dot(mixed, wo_ref[...], preferred_element_type=F32)
    o_ref[...] = x2 * lax.rsqrt(jnp.mean(x2 * x2, axis=-1, keepdims=True) + EPS) * nf_ref[...]


def _odd_out(u, sg, x1, cw_pad, cb, lg, lb, w_out, nf, tm):
    s = u.shape[0]
    row = pl.BlockSpec((tm, D_MODEL), lambda i: (i, 0))
    const = lambda shape: pl.BlockSpec(shape, lambda i: (0,) * len(shape))
    halo = pl.BlockSpec((HALO, D_MODEL), lambda i: (jnp.maximum(i * (tm // HALO) - 1, 0), 0))
    return pl.pallas_call(
        functools.partial(_odd_out_kernel, tm=tm),
        grid=(s // tm,),
        in_specs=[row, halo, row, row, const((HALO, D_MODEL)), const((1, D_MODEL)), const((1, D_MODEL)),
                  const((1, D_MODEL)), const((D_MODEL, D_MODEL)), const((1, D_MODEL))],
        out_specs=row,
        out_shape=jax.ShapeDtypeStruct((s, D_MODEL), F32),
        scratch_shapes=[pltpu.VMEM((HALO + tm, D_MODEL), F32), pltpu.VMEM((tm, D_MODEL), F32),
                        pltpu.VMEM((HALO + CONV_ROWS, LANES), F32)],
        compiler_params=pltpu.CompilerParams(dimension_semantics=("arbitrary",), vmem_limit_bytes=VMEM_LIMIT),
        name="odd_out",
    )(u, u, sg, x1, cw_pad, cb, lg, lb, w_out, nf)


def kernel(x, rel_bias, norm_even, w_in_even, w_gla_lr, b_gla_lr, gla_norm, w_out_even, norm_odd, w_in_odd,
           conv_w, conv_b, conv_ln_g, conv_ln_b, w_out_odd, norm_f):
    bsz, s, d = x.shape
    assert bsz == 1 and d == D_MODEL and s % GLA_BLOCK == 0
    assert norm_even.shape[0] == 1 and norm_odd.shape[0] == 1
    top_k = min(TOPK_MAX, s // 4)
    tm = ROW_BLOCK
    x2d = x.reshape(s, d)

    wt_pad = _even_in_weights(w_in_even[0])
    qa_t, ka, va_t, ga, qi_t, ki, wi_t, qb, kb, vb, gb, lr = _even_in(x2d, norm_even[0].reshape(1, d), wt_pad, tm)
    mix_a = _dsa(qa_t, qi_t, wi_t, ga, ka, va_t, ki, _dsa_bias_table(rel_bias), top_k)

    head_pad = ((0, 0), (0, 0), (0, LANES - B_DK))
    wlr_pad = jnp.pad(w_gla_lr[0].reshape(GLA_RANK, B_HEADS, B_DK), head_pad).reshape(GLA_RANK, B_HEADS * LANES)
    wlr_pad = jnp.pad(wlr_pad, ((0, LANES - GLA_RANK), (0, 0)))
    blr_pad = jnp.pad(b_gla_lr[0].reshape(1, B_HEADS, B_DK), head_pad).reshape(1, B_HEADS * LANES)
    mix_b = _gla(qb, kb, vb, gb, lr, wlr_pad, blr_pad, gla_norm[0].reshape(1, B_DV))

    x1, u, sg = _mid(x2d, mix_a, mix_b, w_out_even[0].astype(BF16), norm_odd[0].reshape(1, d),
                     w_in_odd[0].astype(BF16), tm)

    cw_pad = jnp.zeros((HALO, d), F32).at[:CONV_WIDTH].set(conv_w[0])
    out = _odd_out(u, sg, x1, cw_pad, conv_b[0].reshape(1, d), conv_ln_g[0].reshape(1, d),
                   conv_ln_b[0].reshape(1, d), w_out_odd[0].astype(BF16), norm_f.reshape(1, d), tm)
    return out.reshape(bsz, s, d)
```

```python
import functools
import math

import jax
import jax.numpy as jnp
import numpy as np
from jax import lax
from jax.experimental import pallas as pl
from jax.experimental.pallas import tpu as pltpu

D_MODEL = 1024
A_HEADS = 8
A_HEAD_DIM = 64
IDX_HEADS = 4
IDX_DIM = 64
TOPK_MAX = 256
B_HEADS = 4
B_DK = 64
B_DV = 128
GLA_RANK = 16
GLA_TAU = 16.0
GLA_CHUNK = 64
GLA_SUB = 16
CONV_WIDTH = 31
REL_BUCKETS = 32
REL_MAX_EXACT = 16
REL_MAX_DIST = 128
EPS = 1e-6

A_W = A_HEADS * A_HEAD_DIM
B_KW = B_HEADS * B_DK
B_VW = B_HEADS * B_DV

LANES = 128
SUBLANES = 8
VMEM_LIMIT = 56 * 1024 * 1024
ROW_BLOCK = 512

F32 = jnp.float32
BF16 = jnp.bfloat16
NEG_INF = float("-inf")
INT_MIN = -(2 ** 31)
NEG_INF_BITS = int(np.array(-np.inf, np.float32).view(np.int32))
NEG_INF_KEY = NEG_INF_BITS ^ 0x7FFFFFFF
M_INIT = -1e30
LOG2E = math.log2(math.e)

C_QA = 0
C_KA = C_QA + A_HEADS * LANES
C_VA = C_KA + A_W
C_GA = C_VA + A_W
C_QI = C_GA + A_W
C_KI = C_QI + IDX_HEADS * LANES
C_WI = C_KI + LANES
C_QB = C_WI + LANES
C_KB = C_QB + B_HEADS * LANES
C_VB = C_KB + B_HEADS * LANES
C_GB = C_VB + B_VW
C_LR = C_GB + B_VW
C_END = C_LR + LANES


def _t5_bucket_np(dist):
    dist = np.maximum(dist, 0)
    d = np.maximum(dist, 1).astype(np.float32)
    large = REL_MAX_EXACT + (np.log(d / REL_MAX_EXACT) / math.log(REL_MAX_DIST / REL_MAX_EXACT)
                             * (REL_BUCKETS - REL_MAX_EXACT)).astype(np.int32)
    large = np.minimum(large, REL_BUCKETS - 1)
    return np.where(dist < REL_MAX_EXACT, dist, large)


def _silu(x):
    return x * jax.nn.sigmoid(x)


def _even_in_kernel(x_ref, g_ref, wt_ref, qat_ref, ka_ref, vat_ref, ga_ref, qit_ref, ki_ref, wit_ref,
                    qb_ref, kb_ref, vb_ref, gb_ref, lr_ref):
    x = x_ref[...]
    h = (x * lax.rsqrt(jnp.mean(x * x, axis=-1, keepdims=True) + EPS) * g_ref[...]).astype(BF16)
    dn_t = (((1,), (1,)), ((), ()))

    def proj(lo, hi):
        return lax.dot_general(h, wt_ref[lo:hi, :], dn_t, preferred_element_type=F32)

    def proj_t(lo, hi):
        return lax.dot_general(wt_ref[lo:hi, :], h, dn_t, preferred_element_type=F32)

    qat_ref[...] = proj_t(C_QA, C_KA).astype(BF16)
    ka_ref[...] = proj(C_KA, C_VA).astype(BF16)
    vat_ref[...] = proj_t(C_VA, C_GA).astype(BF16)
    ga_ref[...] = proj(C_GA, C_QI)
    qit_ref[...] = proj_t(C_QI, C_KI).astype(BF16)
    ki_ref[...] = proj(C_KI, C_WI).astype(BF16)
    wit_ref[...] = proj_t(C_WI, C_WI + SUBLANES)
    qb_ref[...] = proj(C_QB, C_KB)
    kb_ref[...] = proj(C_KB, C_VB)
    vb_ref[...] = proj(C_VB, C_GB)
    gb_ref[...] = proj(C_GB, C_LR)
    lr_ref[...] = proj(C_LR, C_END)


def _even_in_layout():
    cols = np.cumsum([0, A_W, A_W, A_W, A_W, IDX_HEADS * IDX_DIM, IDX_DIM, IDX_HEADS,
                      B_KW, B_KW, B_VW, B_VW, GLA_RANK])
    (s_qa, s_ka, s_va, s_ga, s_qi, s_ki, s_wi, s_qb, s_kb, s_vb, s_gb, s_lr, _) = [int(c) for c in cols]
    src = np.full((C_END,), -1, np.int32)
    scale = np.ones((C_END,), np.float32)

    def put(dst, start, width, sc=1.0):
        src[dst:dst + width] = np.arange(start, start + width)
        scale[dst:dst + width] = sc

    for h in range(A_HEADS):
        put(C_QA + h * LANES + (h % 2) * A_HEAD_DIM, s_qa + h * A_HEAD_DIM, A_HEAD_DIM, (A_HEAD_DIM ** -0.5) * LOG2E)
    put(C_KA, s_ka, A_W)
    put(C_VA, s_va, A_W)
    put(C_GA, s_ga, A_W)
    for h in range(IDX_HEADS):
        put(C_QI + h * LANES, s_qi + h * IDX_DIM, IDX_DIM)
    put(C_KI, s_ki, IDX_DIM)
    put(C_WI, s_wi, IDX_HEADS, (IDX_DIM ** -0.5) * (IDX_HEADS ** -0.5))
    for h in range(B_HEADS):
        put(C_QB + h * LANES, s_qb + h * B_DK, B_DK, B_DK ** -0.5)
        put(C_KB + h * LANES, s_kb + h * B_DK, B_DK)
    put(C_VB, s_vb, B_VW)
    put(C_GB, s_gb, B_VW)
    put(C_LR, s_lr, GLA_RANK)
    return src, scale


def _even_in_weights(w):
    src, scale = _even_in_layout()
    pieces, start = [], 0
    for end in range(1, C_END + 1):
        if end == C_END or (src[end] >= 0) != (src[start] >= 0) or (src[end] >= 0 and src[end] != src[end - 1] + 1):
            if src[start] >= 0:
                pieces.append(w[:, int(src[start]):int(src[end - 1]) + 1])
            else:
                pieces.append(jnp.zeros((D_MODEL, end - start), w.dtype))
            start = end
    w_pad = jnp.concatenate(pieces, axis=1) * jnp.asarray(scale)[None, :]
    return w_pad.T.astype(BF16)


def _even_in(x2d, g, wt_pad, tm):
    s = x2d.shape[0]
    row = lambda wd, dt: (pl.BlockSpec((tm, wd), lambda i: (i, 0)), jax.ShapeDtypeStruct((s, wd), dt))
    col = lambda ht, dt: (pl.BlockSpec((ht, tm), lambda i: (0, i)), jax.ShapeDtypeStruct((ht, s), dt))
    outs = [col(C_KA - C_QA, BF16), row(A_W, BF16), col(A_W, BF16), row(A_W, F32), col(C_KI - C_QI, BF16),
            row(LANES, BF16), col(SUBLANES, F32), row(C_KB - C_QB, F32), row(C_VB - C_KB, F32), row(B_VW, F32),
            row(B_VW, F32), row(LANES, F32)]
    return pl.pallas_call(
        _even_in_kernel,
        grid=(s // tm,),
        in_specs=[pl.BlockSpec((tm, D_MODEL), lambda i: (i, 0)),
                  pl.BlockSpec((1, D_MODEL), lambda i: (0, 0)),
                  pl.BlockSpec((C_END, D_MODEL), lambda i: (0, 0))],
        out_specs=[spec for spec, _ in outs],
        out_shape=[shape for _, shape in outs],
        compiler_params=pltpu.CompilerParams(dimension_semantics=("arbitrary",), vmem_limit_bytes=VMEM_LIMIT),
        name="even_in",
    )(x2d, g, wt_pad)


DSA_Q = 128
DSA_KB = 128
DSA_WIDE = 4
DSA_NACC = 4
DSA_CELLS = 64
DSA_CELL_BITS = 14
DSA_VMEM_LIMIT = 60 * 1024 * 1024


def _f32_key(x):
    bits = lax.bitcast_convert_type(x, jnp.int32)
    return bits ^ (lax.shift_right_arithmetic(bits, 31) & 0x7FFFFFFF)


def _fold_keys(x, op):
    n, w = x.shape
    group = DSA_NACC * SUBLANES
    if n % group == 0 and n > group:
        x = op(x.reshape(n // group, DSA_NACC, SUBLANES, w), axis=0)
        return op(x, axis=0)
    return op(x.reshape(n // SUBLANES, SUBLANES, w), axis=0)


def _dsa_kernel(qat_ref, qit_ref, wit_ref, ga_ref, ka_ref, vat_ref, ki_ref, btab_ref, tri_ref, o_ref,
                slab_ref, cell_ref, s_ref, acc_ref, m_ref, l_ref, *, top_k):
    q, kb, wide = DSA_Q, DSA_KB, DSA_WIDE * DSA_KB
    i = pl.program_id(0)
    nkb = i + 1
    nwide = jnp.maximum(i - 1, 0) // DSA_WIDE
    n0 = nwide * DSA_WIDE
    t_row = i * q + lax.broadcasted_iota(jnp.int32, (1, q), 1)
    key_off = lax.broadcasted_iota(jnp.int32, (kb, 1), 0)
    npair = A_HEADS // 2

    def wide_rows(jw):
        return pl.ds(pl.multiple_of(jw * wide, wide), wide)

    def narrow_rows(j):
        return pl.ds(pl.multiple_of(j * kb, kb), kb)

    def pipelined(first, last, produce, consume):
        def body(u, carry):
            t = first + 2 * u
            consume(t, 0, carry)
            consume(t + 1, 1, produce(t + 1, 1))
            return produce(t + 2, 0)

        npairs = (last - first - 1) // 2
        carry = lax.fori_loop(0, npairs, body, produce(first, 0))
        t = first + 2 * npairs
        consume(t, 0, carry)

        @pl.when(t + 1 < last)
        def _():
            consume(t + 1, 1, produce(t + 1, 1))

    w_head = [wit_ref[h:h + 1, :] for h in range(IDX_HEADS)]

    def scores(rows):
        k_idx = ki_ref[rows, :]
        sc = None
        for h in range(IDX_HEADS):
            s = jnp.dot(k_idx, qit_ref[h * LANES:(h + 1) * LANES, :], preferred_element_type=F32)
            term = w_head[h] * jnp.maximum(s, 0.0)
            sc = term if sc is None else sc + term
        return sc

    assert wide % (DSA_CELLS * SUBLANES) == 0
    cell_ref[...] = jnp.full(cell_ref.shape, INT_MIN, jnp.int32)

    def score_rows(rows, n):
        key = _f32_key(scores(rows))
        slab_ref[rows, :] = key
        cells = jnp.max(key.reshape(n // (DSA_CELLS * SUBLANES), DSA_CELLS, SUBLANES, q), axis=0)
        cell_ref[...] = jnp.maximum(cell_ref[...], cells)

    def pair_rows(jp):
        return pl.ds(pl.multiple_of(jp * (2 * wide), 2 * wide), 2 * wide)

    def score_narrow(j, c):
        rows = narrow_rows(j)
        key = _f32_key(jnp.where(j * kb + key_off <= t_row, scores(rows), NEG_INF))
        slab_ref[rows, :] = key
        ncls = kb // SUBLANES
        cell_ref[0:ncls] = jnp.maximum(cell_ref[0:ncls], key.reshape(ncls, SUBLANES, q))
        return c

    def quad_rows(jq):
        return pl.ds(pl.multiple_of(jq * (4 * wide), 4 * wide), 4 * wide)

    lax.fori_loop(0, nwide // 4, lambda jq, c: score_rows(quad_rows(jq), 4 * wide), None)
    lax.fori_loop((nwide // 4) * 2, nwide // 2, lambda jp, c: score_rows(pair_rows(jp), 2 * wide), None)
    lax.fori_loop((nwide // 2) * 2, nwide, lambda jw, c: score_rows(wide_rows(jw), wide), None)
    lax.fori_loop(n0, nkb, score_narrow, 0)

    def count_ge(trial):
        def cnt(blk):
            return _fold_keys(jnp.where(blk >= trial, 1.0, 0.0), jnp.sum)

        ncw = nkb // DSA_WIDE
        acc = lax.fori_loop(0, ncw // 4, lambda jq, a: a + cnt(slab_ref[quad_rows(jq), :]),
                            jnp.zeros((SUBLANES, q), F32))
        acc = lax.fori_loop((ncw // 4) * 2, ncw // 2, lambda jp, a: a + cnt(slab_ref[pair_rows(jp), :]), acc)
        acc = lax.fori_loop((ncw // 2) * 2, ncw, lambda jw, a: a + cnt(slab_ref[wide_rows(jw), :]), acc)
        acc = lax.fori_loop(ncw * DSA_WIDE, nkb, lambda j, a: a + cnt(slab_ref[narrow_rows(j), :]), acc)
        return jnp.sum(acc, axis=0, keepdims=True).astype(jnp.int32)

    assert DSA_CELLS * SUBLANES >= top_k
    cell_max = cell_ref[...]
    hi_key = jnp.max(jnp.max(cell_max, axis=0), axis=0, keepdims=True)

    def cell_bisect(b, c):
        trial = c + lax.shift_left(jnp.int32(1), 31 - b)
        cnt = jnp.sum(jnp.sum(jnp.where(cell_max >= trial, 1.0, 0.0), axis=0), axis=0, keepdims=True)
        return jnp.where(cnt >= top_k, trial, c)

    lo_key = lax.fori_loop(0, DSA_CELL_BITS, cell_bisect, jnp.full((1, q), INT_MIN, jnp.int32))
    base = lo_key
    span = hi_key - lo_key
    nbits = jnp.max(32 - lax.clz(span))

    def bisect(b, state):
        off, cnt, above = state
        trial_off = off + lax.shift_left(jnp.int32(1), nbits - 1 - b)
        in_span = (trial_off ^ INT_MIN) <= (span ^ INT_MIN)
        c = jnp.where(in_span, count_ge(base + trial_off), 0)
        ok = c >= top_k
        return jnp.where(ok, trial_off, off), jnp.where(ok, c, cnt), jnp.where(ok, above, c)

    uncounted = jnp.full((1, q), -1, jnp.int32)
    off, cnt_thr, cnt_above = lax.fori_loop(0, nbits, bisect, (jnp.zeros((1, q), jnp.int32), uncounted, uncounted))
    thr = base + off
    cnt_thr = lax.cond(jnp.min(cnt_thr) < 0, lambda: count_ge(thr), lambda: cnt_thr)
    tied = jnp.logical_and(cnt_thr > top_k, thr > NEG_INF_KEY)
    any_tie = jnp.max(jnp.where(tied, 1, 0)) > 0

    thr_valid = jnp.maximum(thr, NEG_INF_KEY + 1)

    def mask_fast():
        def body(rows):
            slab_ref[rows, :] = jnp.where(slab_ref[rows, :] >= thr_valid, 0, NEG_INF_BITS)

        lax.fori_loop(0, nwide, lambda jw, c: body(wide_rows(jw)), None)
        lax.fori_loop(n0, nkb, lambda j, c: body(narrow_rows(j)), None)

    def mask_tied():
        above = lax.cond(jnp.min(cnt_above) < 0, lambda: count_ge(thr + 1), lambda: cnt_above)
        thr_c = jnp.maximum(thr, NEG_INF_KEY)
        need = jnp.where(thr_c > NEG_INF_KEY, (top_k - above).astype(F32), -1.0)

        def body(rows, n, seen):
            for g in range(n // kb):
                blk = pl.ds(rows.start + g * kb, kb)
                key = slab_ref[blk, :]
                eq = key == thr_c
                eq_f = jnp.where(eq, 1.0, 0.0)
                before = jnp.dot(tri_ref[...], eq_f.astype(BF16), preferred_element_type=F32)
                tie_bits = jnp.where(seen + before < need, 0, NEG_INF_BITS)
                slab_ref[blk, :] = jnp.where(key > thr_c, 0, jnp.where(eq, tie_bits, NEG_INF_BITS))
                seen = seen + jnp.sum(_fold_keys(eq_f, jnp.sum), axis=0, keepdims=True)
            return seen

        seen = lax.fori_loop(0, nwide // 4, lambda jq, c: body(quad_rows(jq), 4 * wide, c), jnp.zeros((1, q), F32))
        seen = lax.fori_loop((nwide // 4) * 2, nwide // 2, lambda jp, c: body(pair_rows(jp), 2 * wide, c), seen)
        seen = lax.fori_loop((nwide // 2) * 2, nwide, lambda jw, c: body(wide_rows(jw), wide, c), seen)
        lax.fori_loop(n0, nkb, lambda j, c: body(narrow_rows(j), kb, c), seen)

    lax.cond(any_tie, mask_tied, mask_fast)

    m_ref[...] = jnp.full(m_ref.shape, M_INIT, F32)
    l_ref[...] = jnp.zeros(l_ref.shape, F32)
    acc_ref[...] = jnp.zeros(acc_ref.shape, F32)

    def logits(rows, n, buf, rel):
        mask = lax.bitcast_convert_type(slab_ref[rows, :], F32)
        mask2 = jnp.concatenate([mask, mask], axis=1)
        tile_max = []
        for p in range(npair):
            ls = slice(p * LANES, (p + 1) * LANES)
            w_q = jnp.concatenate([qat_ref[(2 * p) * LANES:(2 * p + 1) * LANES, :],
                                   qat_ref[(2 * p + 1) * LANES:(2 * p + 2) * LANES, :]], axis=1)
            s = jnp.dot(ka_ref[rows, ls], w_q, preferred_element_type=F32) + mask2
            if rel is not None:
                s = s + btab_ref[rel, p]
            s_ref[buf, p, 0:n, :] = s
            tile_max.append(_fold_keys(s, jnp.max))
        return tuple(tile_max)

    def attend(rows, n, buf, tile_max):
        for p in range(npair):
            ls = slice(p * LANES, (p + 1) * LANES)
            m_old = m_ref[p]
            m_new = jnp.maximum(m_old, jnp.max(tile_max[p], axis=0, keepdims=True))
            alpha = jnp.exp2(m_old - m_new)
            pexp = jnp.exp2(s_ref[buf, p, 0:n, :] - m_new[0:1])
            m_ref[p] = m_new
            l_ref[p] = alpha * l_ref[p] + _fold_keys(pexp, jnp.sum)
            acc_ref[p] = alpha[0:1] * acc_ref[p] + jnp.dot(vat_ref[ls, rows], pexp.astype(BF16),
                                                           preferred_element_type=F32)

    @pl.when(nwide >= 2)
    def _():
        pipelined(0, nwide // 2, lambda jp, buf: logits(pair_rows(jp), 2 * wide, buf, None),
                  lambda jp, buf, tile_max: attend(pair_rows(jp), 2 * wide, buf, tile_max))

    @pl.when(nwide % 2 == 1)
    def _():
        attend(wide_rows(nwide - 1), wide, 0, logits(wide_rows(nwide - 1), wide, 0, None))

    pipelined(n0, nkb, lambda j, buf: logits(narrow_rows(j), kb, buf, jnp.minimum(i - j, 2)),
              lambda j, buf, tile_max: attend(narrow_rows(j), kb, buf, tile_max))

    for p in range(npair):
        o_t = acc_ref[p] / jnp.sum(l_ref[p], axis=0, keepdims=True)
        o_pair = jnp.concatenate([o_t[:A_HEAD_DIM, :q], o_t[A_HEAD_DIM:, q:]], axis=0).T
        ls = slice(p * LANES, (p + 1) * LANES)
        o_ref[:, ls] = (o_pair * _silu(ga_ref[:, ls])).astype(o_ref.dtype)


def _dsa_bias_table(rel_bias):
    ii = np.arange(DSA_Q)[:, None]
    jj = np.arange(DSA_KB)[None, :]
    buckets = np.stack([_t5_bucket_np(ii - jj), _t5_bucket_np(DSA_KB + ii - jj),
                        np.full((DSA_Q, DSA_KB), REL_BUCKETS - 1)])
    assert (_t5_bucket_np(np.arange(DSA_KB + 1, 4 * DSA_KB)) == REL_BUCKETS - 1).all()
    rb = rel_bias.astype(F32) * LOG2E
    rb = rb - rb[REL_BUCKETS - 1]
    onehot = np.zeros((3, DSA_KB, DSA_Q, REL_BUCKETS), np.float32)
    np.put_along_axis(onehot, np.transpose(buckets, (0, 2, 1))[..., None], 1.0, axis=-1)
    tab = jnp.einsum('tsrb,bh->thsr', jnp.asarray(onehot), rb, precision=lax.Precision.HIGHEST)
    tab = tab.reshape(3, A_HEADS // 2, 2, DSA_KB, DSA_Q)
    return jnp.transpose(tab, (0, 1, 3, 2, 4)).reshape(3, A_HEADS // 2, DSA_KB, 2 * DSA_Q)


def _dsa(qa_t, qi_t, wi_t, ga, ka, va_t, ki, btab, top_k):
    s = ga.shape[0]
    q, kb = DSA_Q, DSA_KB
    tri = jnp.asarray(np.tril(np.ones((kb, kb), np.float32), -1), BF16)
    whole = lambda shape: pl.BlockSpec(shape, lambda i: (0,) * len(shape), pipeline_mode=pl.Buffered(1))
    return pl.pallas_call(
        functools.partial(_dsa_kernel, top_k=top_k),
        grid=(s // q,),
        in_specs=[pl.BlockSpec((A_HEADS * LANES, q), lambda i: (0, i)),
                  pl.BlockSpec((IDX_HEADS * LANES, q), lambda i: (0, i)),
                  pl.BlockSpec((SUBLANES, q), lambda i: (0, i)),
                  pl.BlockSpec((q, A_W), lambda i: (i, 0)),
                  whole((s, A_W)), whole((A_W, s)), whole((s, LANES)),
                  whole((3, A_HEADS // 2, kb, 2 * q)), whole((kb, kb))],
        out_specs=pl.BlockSpec((q, A_W), lambda i: (i, 0)),
        out_shape=jax.ShapeDtypeStruct((s, A_W), BF16),
        scratch_shapes=[pltpu.VMEM((s, q), jnp.int32),
                        pltpu.VMEM((DSA_CELLS, SUBLANES, q), jnp.int32),
                        pltpu.VMEM((2, A_HEADS // 2, 2 * DSA_WIDE * kb, 2 * q), F32),
                        pltpu.VMEM((A_HEADS // 2, LANES, 2 * q), F32),
                        pltpu.VMEM((A_HEADS // 2, SUBLANES, 2 * q), F32),
                        pltpu.VMEM((A_HEADS // 2, SUBLANES, 2 * q), F32)],
        compiler_params=pltpu.CompilerParams(dimension_semantics=("arbitrary",), vmem_limit_bytes=DSA_VMEM_LIMIT),
        name="dsa",
    )(qa_t, qi_t, wi_t, ga, ka, va_t, ki, btab, tri)


GLA_BLOCK = 512


def _gla_kernel(qb_ref, kb_ref, vb_ref, gb_ref, lr_ref, wlr_ref, blr_ref, gn_ref, tril_ref, o_ref,
                state_ref, b_ref, att_ref):
    c, sub = GLA_CHUNK, GLA_SUB
    nsub = c // sub

    @pl.when(pl.program_id(0) == 0)
    def _():
        state_ref[...] = jnp.zeros(state_ref.shape, F32)

    dn_t = (((1,), (1,)), ((), ()))
    dn_0 = (((0,), (0,)), ((), ()))
    hp = lax.Precision.HIGHEST
    row_i = lax.broadcasted_iota(jnp.int32, (sub, sub), 0)
    col_j = lax.broadcasted_iota(jnp.int32, (sub, sub), 1)

    def chunk(ci, carry):
        r0 = pl.multiple_of(ci * c, c)
        z = jnp.dot(lr_ref[pl.ds(r0, c), :], wlr_ref[...], precision=hp, preferred_element_type=F32) + blr_ref[...]
        log_a = (jnp.minimum(z, 0.0) - jnp.log1p(jnp.exp(-jnp.abs(z)))) * (LOG2E / GLA_TAU)
        b_ref[...] = jnp.dot(tril_ref[...], log_a, precision=hp, preferred_element_type=F32)
        for h in range(B_HEADS):
            ls = slice(h * LANES, (h + 1) * LANES)
            b = b_ref[:, ls]
            qh = qb_ref[pl.ds(r0, c), ls]
            kh = kb_ref[pl.ds(r0, c), ls]
            vh = vb_ref[pl.ds(r0, c), ls]
            state_t = state_ref[h]
            b_last = b[c - 1:c, :]
            o = lax.dot_general((qh * jnp.exp2(b)).astype(BF16), state_t.astype(BF16), dn_t,
                                preferred_element_type=F32)
            for si in range(nsub):
                rs = slice(si * sub, (si + 1) * sub)
                b_i, q_i = b[rs], qh[rs]
                if si > 0:
                    ref_row = b[si * sub:si * sub + 1, :]
                    q_t = (q_i * jnp.exp2(b_i - ref_row)).astype(BF16)
                    k_t = (kh[:si * sub] * jnp.exp2(ref_row - b[:si * sub])).astype(BF16)
                    att_ref[rs, :si * sub] = lax.dot_general(q_t, k_t, dn_t, preferred_element_type=F32)
                k_i = kh[rs]
                diag = jnp.zeros((sub, sub), F32)
                for jl in range(sub):
                    diff = jnp.minimum(b_i - b_i[jl:jl + 1, :], 0.0)
                    col = jnp.sum(q_i * k_i[jl:jl + 1, :] * jnp.exp2(diff), axis=-1, keepdims=True)
                    diag = jnp.where(col_j == jl, col, diag)
                att_ref[rs, rs] = jnp.where(col_j <= row_i, diag, 0.0)
                if si + 1 < nsub:
                    att_ref[rs, (si + 1) * sub:] = jnp.zeros((sub, c - (si + 1) * sub), F32)
            o = o + jnp.dot(att_ref[...].astype(BF16), vh.astype(BF16), preferred_element_type=F32)
            k_dec = (kh * jnp.exp2(b_last - b)).astype(BF16)
            state_ref[h] = (jnp.exp2(b_last) * state_t
                            + lax.dot_general(vh.astype(BF16), k_dec, dn_0, preferred_element_type=F32))
            o = o * lax.rsqrt(jnp.mean(o * o, axis=-1, keepdims=True) + EPS) * gn_ref[...]
            o_ref[pl.ds(r0, c), ls] = (o * _silu(gb_ref[pl.ds(r0, c), ls])).astype(o_ref.dtype)
        return carry

    lax.fori_loop(0, GLA_BLOCK // c, chunk, 0)


def _gla(qb, kb, vb, gb, lr, wlr_pad, blr_pad, gn):
    s = qb.shape[0]
    blk = min(GLA_BLOCK, s)
    assert blk == GLA_BLOCK and s % blk == 0
    tril = jnp.asarray(np.tril(np.ones((GLA_CHUNK, GLA_CHUNK), np.float32)))
    row = lambda wd: pl.BlockSpec((blk, wd), lambda i: (i, 0))
    const = lambda shape: pl.BlockSpec(shape, lambda i: (0,) * len(shape))
    return pl.pallas_call(
        _gla_kernel,
        grid=(s // blk,),
        in_specs=[row(B_HEADS * LANES), row(B_HEADS * LANES), row(B_VW), row(B_VW), row(LANES),
                  const((LANES, B_HEADS * LANES)), const((1, B_HEADS * LANES)), const((1, B_DV)),
                  const((GLA_CHUNK, GLA_CHUNK))],
        out_specs=row(B_VW),
        out_shape=jax.ShapeDtypeStruct((s, B_VW), BF16),
        scratch_shapes=[pltpu.VMEM((B_HEADS, LANES, B_DV), F32),
                        pltpu.VMEM((GLA_CHUNK, B_HEADS * LANES), F32),
                        pltpu.VMEM((GLA_CHUNK, GLA_CHUNK), F32)],
        compiler_params=pltpu.CompilerParams(dimension_semantics=("arbitrary",), vmem_limit_bytes=VMEM_LIMIT),
        name="gla",
    )(qb, kb, vb, gb, lr, wlr_pad, blr_pad, gn, tril)


def _mid_kernel(x_ref, ma_ref, mb_ref, wo_ref, g_ref, wi_ref, x1_ref, u_ref, sg_ref):
    x1 = (x_ref[...]
          + jnp.dot(ma_ref[...], wo_ref[:A_W, :], preferred_element_type=F32)
          + jnp.dot(mb_ref[...], wo_ref[A_W:, :], preferred_element_type=F32))
    x1_ref[...] = x1
    h = (x1 * lax.rsqrt(jnp.mean(x1 * x1, axis=-1, keepdims=True) + EPS) * g_ref[...]).astype(BF16)
    val = jnp.dot(h, wi_ref[:, :D_MODEL], preferred_element_type=F32)
    glu = jnp.dot(h, wi_ref[:, D_MODEL:2 * D_MODEL], preferred_element_type=F32)
    gate = jnp.dot(h, wi_ref[:, 2 * D_MODEL:], preferred_element_type=F32)
    u_ref[...] = val * jax.nn.sigmoid(glu)
    sg_ref[...] = _silu(gate)


def _mid(x2d, mix_a, mix_b, w_out, g, w_in, tm):
    s = x2d.shape[0]
    row = lambda wd: pl.BlockSpec((tm, wd), lambda i: (i, 0))
    const = lambda shape: pl.BlockSpec(shape, lambda i: (0,) * len(shape))
    return pl.pallas_call(
        _mid_kernel,
        grid=(s // tm,),
        in_specs=[row(D_MODEL), row(A_W), row(B_VW), const((A_W + B_VW, D_MODEL)), const((1, D_MODEL)),
                  const((D_MODEL, 3 * D_MODEL))],
        out_specs=[row(D_MODEL), row(D_MODEL), row(D_MODEL)],
        out_shape=[jax.ShapeDtypeStruct((s, D_MODEL), F32)] * 3,
        compiler_params=pltpu.CompilerParams(dimension_semantics=("arbitrary",), vmem_limit_bytes=VMEM_LIMIT),
        name="mid",
    )(x2d, mix_a, mix_b, w_out, g, w_in)


HALO = 32
CONV_ROWS = 128


def _odd_out_kernel(u_ref, halo_ref, sg_ref, x1_ref, cw_ref, cb_ref, lg_ref, lb_ref, wo_ref, nf_ref, o_ref,
                    ubuf_ref, y_ref, shift_ref, *, tm):
    i = pl.program_id(0)
    ubuf_ref[HALO:, :] = u_ref[...]
    ubuf_ref[:HALO, :] = jnp.where(i > 0, halo_ref[...], 0.0)
    base = HALO - (CONV_WIDTH - 1)
    for cs in range(D_MODEL // LANES):
        ls = slice(cs * LANES, (cs + 1) * LANES)
        for r0 in range(0, tm, CONV_ROWS):
            acc = jnp.zeros((CONV_ROWS, LANES), F32)
            for res in range(SUBLANES):
                taps = [k for k in range(CONV_WIDTH) if (base + k) % SUBLANES == res]
                first = base + taps[0]
                rows = taps[-1] - taps[0] + CONV_ROWS
                shift_ref[0:rows, :] = ubuf_ref[r0 + first:r0 + first + rows, ls]
                aligned = shift_ref[0:rows, :]
                for k in taps:
                    a = base + k - first
                    acc = acc + aligned[a:a + CONV_ROWS] * cw_ref[k:k + 1, ls]
            y_ref[r0:r0 + CONV_ROWS, ls] = acc + cb_ref[:, ls]
    y = y_ref[...]
    mu = jnp.mean(y, axis=-1, keepdims=True)
    yc = y - mu
    var = jnp.mean(yc * yc, axis=-1, keepdims=True)
    yn = yc * lax.rsqrt(var + EPS) * lg_ref[...] + lb_ref[...]
    mixed = (_silu(yn) * sg_ref[...]).astype(BF16)
    x2 = x1_ref[...] + jnp.dot(mixed, wo_ref[...], preferred_element_type=F32)
    o_ref[...] = x2 * lax.rsqrt(jnp.mean(x2 * x2, axis=-1, keepdims=True) + EPS) * nf_ref[...]


def _odd_out(u, sg, x1, cw_pad, cb, lg, lb, w_out, nf, tm):
    s = u.shape[0]
    row = pl.BlockSpec((tm, D_MODEL), lambda i: (i, 0))
    const = lambda shape: pl.BlockSpec(shape, lambda i: (0,) * len(shape))
    halo = pl.BlockSpec((HALO, D_MODEL), lambda i: (jnp.maximum(i * (tm // HALO) - 1, 0), 0))
    return pl.pallas_call(
        functools.partial(_odd_out_kernel, tm=tm),
        grid=(s // tm,),
        in_specs=[row, halo, row, row, const((HALO, D_MODEL)), const((1, D_MODEL)), const((1, D_MODEL)),
                  const((1, D_MODEL)), const((D_MODEL, D_MODEL)), const((1, D_MODEL))],
        out_specs=row,
        out_shape=jax.ShapeDtypeStruct((s, D_MODEL), F32),
        scratch_shapes=[pltpu.VMEM((HALO + tm, D_MODEL), F32), pltpu.VMEM((tm, D_MODEL), F32),
                        pltpu.VMEM((HALO + CONV_ROWS, LANES), F32)],
        compiler_params=pltpu.CompilerParams(dimension_semantics=("arbitrary",), vmem_limit_bytes=VMEM_LIMIT),
        name="odd_out",
    )(u, u, sg, x1, cw_pad, cb, lg, lb, w_out, nf)


def kernel(x, rel_bias, norm_even, w_in_even, w_gla_lr, b_gla_lr, gla_norm, w_out_even, norm_odd, w_in_odd,
           conv_w, conv_b, conv_ln_g, conv_ln_b, w_out_odd, norm_f):
    bsz, s, d = x.shape
    assert bsz == 1 and d == D_MODEL and s % GLA_BLOCK == 0
    assert norm_even.shape[0] == 1 and norm_odd.shape[0] == 1
    top_k = min(TOPK_MAX, s // 4)
    tm = ROW_BLOCK
    x2d = x.reshape(s, d)

    wt_pad = _even_in_weights(w_in_even[0])
    qa_t, ka, va_t, ga, qi_t, ki, wi_t, qb, kb, vb, gb, lr = _even_in(x2d, norm_even[0].reshape(1, d), wt_pad, tm)
    mix_a = _dsa(qa_t, qi_t, wi_t, ga, ka, va_t, ki, _dsa_bias_table(rel_bias), top_k)

    head_pad = ((0, 0), (0, 0), (0, LANES - B_DK))
    wlr_pad = jnp.pad(w_gla_lr[0].reshape(GLA_RANK, B_HEADS, B_DK), head_pad).reshape(GLA_RANK, B_HEADS * LANES)
    wlr_pad = jnp.pad(wlr_pad, ((0, LANES - GLA_RANK), (0, 0)))
    blr_pad = jnp.pad(b_gla_lr[0].reshape(1, B_HEADS, B_DK), head_pad).reshape(1, B_HEADS * LANES)
    mix_b = _gla(qb, kb, vb, gb, lr, wlr_pad, blr_pad, gla_norm[0].reshape(1, B_DV))

    x1, u, sg = _mid(x2d, mix_a, mix_b, w_out_even[0].astype(BF16), norm_odd[0].reshape(1, d),
                     w_in_odd[0].astype(BF16), tm)

    cw_pad = jnp.zeros((HALO, d), F32).at[:CONV_WIDTH].set(conv_w[0])
    out = _odd_out(u, sg, x1, cw_pad, conv_b[0].reshape(1, d), conv_ln_g[0].reshape(1, d),
                   conv_ln_b[0].reshape(1, d), w_out_odd[0].astype(BF16), norm_f.reshape(1, d), tm)
    return out.reshape(bsz, s, d)
```

```python
import functools
import math

import jax
import jax.numpy as jnp
import numpy as np
from jax import lax
from jax.experimental import pallas as pl
from jax.experimental.pallas import tpu as pltpu

D_MODEL = 1024
A_HEADS = 8
A_HEAD_DIM = 64
IDX_HEADS = 4
IDX_DIM = 64
TOPK_MAX = 256
B_HEADS = 4
B_DK = 64
B_DV = 128
GLA_RANK = 16
GLA_TAU = 16.0
GLA_CHUNK = 64
GLA_SUB = 16
CONV_WIDTH = 31
REL_BUCKETS = 32
REL_MAX_EXACT = 16
REL_MAX_DIST = 128
EPS = 1e-6

A_W = A_HEADS * A_HEAD_DIM
B_KW = B_HEADS * B_DK
B_VW = B_HEADS * B_DV

LANES = 128
SUBLANES = 8
VMEM_LIMIT = 56 * 1024 * 1024
ROW_BLOCK = 512

F32 = jnp.float32
BF16 = jnp.bfloat16
NEG_INF = float("-inf")
INT_MIN = -(2 ** 31)
NEG_INF_BITS = int(np.array(-np.inf, np.float32).view(np.int32))
NEG_INF_KEY = NEG_INF_BITS ^ 0x7FFFFFFF
M_INIT = -1e30
LOG2E = math.log2(math.e)

C_QA = 0
C_KA = C_QA + A_HEADS * LANES
C_VA = C_KA + A_W
C_GA = C_VA + A_W
C_QI = C_GA + A_W
C_KI = C_QI + IDX_HEADS * LANES
C_WI = C_KI + LANES
C_QB = C_WI + LANES
C_KB = C_QB + B_HEADS * LANES
C_VB = C_KB + B_HEADS * LANES
C_GB = C_VB + B_VW
C_LR = C_GB + B_VW
C_END = C_LR + LANES


def _t5_bucket_np(dist):
    dist = np.maximum(dist, 0)
    d = np.maximum(dist, 1).astype(np.float32)
    large = REL_MAX_EXACT + (np.log(d / REL_MAX_EXACT) / math.log(REL_MAX_DIST / REL_MAX_EXACT)
                             * (REL_BUCKETS - REL_MAX_EXACT)).astype(np.int32)
    large = np.minimum(large, REL_BUCKETS - 1)
    return np.where(dist < REL_MAX_EXACT, dist, large)


def _silu(x):
    return x * jax.nn.sigmoid(x)


def _even_in_kernel(x_ref, g_ref, wt_ref, qat_ref, ka_ref, vat_ref, ga_ref, qit_ref, ki_ref, wit_ref,
                    qb_ref, kb_ref, vb_ref, gb_ref, lr_ref):
    x = x_ref[...]
    h = (x * lax.rsqrt(jnp.mean(x * x, axis=-1, keepdims=True) + EPS) * g_ref[...]).astype(BF16)
    dn_t = (((1,), (1,)), ((), ()))

    def proj(lo, hi):
        return lax.dot_general(h, wt_ref[lo:hi, :], dn_t, preferred_element_type=F32)

    def proj_t(lo, hi):
        return lax.dot_general(wt_ref[lo:hi, :], h, dn_t, preferred_element_type=F32)

    qat_ref[...] = proj_t(C_QA, C_KA).astype(BF16)
    ka_ref[...] = proj(C_KA, C_VA).astype(BF16)
    vat_ref[...] = proj_t(C_VA, C_GA).astype(BF16)
    ga_ref[...] = proj(C_GA, C_QI)
    qit_ref[...] = proj_t(C_QI, C_KI).astype(BF16)
    ki_ref[...] = proj(C_KI, C_WI).astype(BF16)
    wit_ref[...] = proj_t(C_WI, C_WI + SUBLANES)
    qb_ref[...] = proj(C_QB, C_KB)
    kb_ref[...] = proj(C_KB, C_VB)
    vb_ref[...] = proj(C_VB, C_GB)
    gb_ref[...] = proj(C_GB, C_LR)
    lr_ref[...] = proj(C_LR, C_END)


def _even_in_layout():
    cols = np.cumsum([0, A_W, A_W, A_W, A_W, IDX_HEADS * IDX_DIM, IDX_DIM, IDX_HEADS,
                      B_KW, B_KW, B_VW, B_VW, GLA_RANK])
    (s_qa, s_ka, s_va, s_ga, s_qi, s_ki, s_wi, s_qb, s_kb, s_vb, s_gb, s_lr, _) = [int(c) for c in cols]
    src = np.full((C_END,), -1, np.int32)
    scale = np.ones((C_END,), np.float32)

    def put(dst, start, width, sc=1.0):
        src[dst:dst + width] = np.arange(start, start + width)
        scale[dst:dst + width] = sc

    for h in range(A_HEADS):
        put(C_QA + h * LANES + (h % 2) * A_HEAD_DIM, s_qa + h * A_HEAD_DIM, A_HEAD_DIM, (A_HEAD_DIM ** -0.5) * LOG2E)
    put(C_KA, s_ka, A_W)
    put(C_VA, s_va, A_W)
    put(C_GA, s_ga, A_W)
    for h in range(IDX_HEADS):
        put(C_QI + h * LANES, s_qi + h * IDX_DIM, IDX_DIM)
    put(C_KI, s_ki, IDX_DIM)
    put(C_WI, s_wi, IDX_HEADS, (IDX_DIM ** -0.5) * (IDX_HEADS ** -0.5))
    for h in range(B_HEADS):
        put(C_QB + h * LANES, s_qb + h * B_DK, B_DK, B_DK ** -0.5)
        put(C_KB + h * LANES, s_kb + h * B_DK, B_DK)
    put(C_VB, s_vb, B_VW)
    put(C_GB, s_gb, B_VW)
    put(C_LR, s_lr, GLA_RANK)
    return src, scale


def _even_in_weights(w):
    src, scale = _even_in_layout()
    pieces, start = [], 0
    for end in range(1, C_END + 1):
        if end == C_END or (src[end] >= 0) != (src[start] >= 0) or (src[end] >= 0 and src[end] != src[end - 1] + 1):
            if src[start] >= 0:
                pieces.append(w[:, int(src[start]):int(src[end - 1]) + 1])
            else:
                pieces.append(jnp.zeros((D_MODEL, end - start), w.dtype))
            start = end
    w_pad = jnp.concatenate(pieces, axis=1) * jnp.asarray(scale)[None, :]
    return w_pad.T.astype(BF16)


def _even_in(x2d, g, wt_pad, tm):
    s = x2d.shape[0]
    row = lambda wd, dt: (pl.BlockSpec((tm, wd), lambda i: (i, 0)), jax.ShapeDtypeStruct((s, wd), dt))
    col = lambda ht, dt: (pl.BlockSpec((ht, tm), lambda i: (0, i)), jax.ShapeDtypeStruct((ht, s), dt))
    outs = [col(C_KA - C_QA, BF16), row(A_W, BF16), col(A_W, BF16), row(A_W, F32), col(C_KI - C_QI, BF16),
            row(LANES, BF16), col(SUBLANES, F32), row(C_KB - C_QB, F32), row(C_VB - C_KB, F32), row(B_VW, F32),
            row(B_VW, F32), row(LANES, F32)]
    return pl.pallas_call(
        _even_in_kernel,
        grid=(s // tm,),
        in_specs=[pl.BlockSpec((tm, D_MODEL), lambda i: (i, 0)),
                  pl.BlockSpec((1, D_MODEL), lambda i: (0, 0)),
                  pl.BlockSpec((C_END, D_MODEL), lambda i: (0, 0))],
        out_specs=[spec for spec, _ in outs],
        out_shape=[shape for _, shape in outs],
        compiler_params=pltpu.CompilerParams(dimension_semantics=("arbitrary",), vmem_limit_bytes=VMEM_LIMIT),
        name="even_in",
    )(x2d, g, wt_pad)


DSA_Q = 128
DSA_KB = 128
DSA_WIDE = 4
DSA_NACC = 4
DSA_CELLS = 64
DSA_CELL_BITS = 14
DSA_VMEM_LIMIT = 60 * 1024 * 1024


def _f32_key(x):
    bits = lax.bitcast_convert_type(x, jnp.int32)
    return bits ^ (lax.shift_right_arithmetic(bits, 31) & 0x7FFFFFFF)


def _fold_keys(x, op):
    n, w = x.shape
    group = DSA_NACC * SUBLANES
    if n % group == 0 and n > group:
        x = op(x.reshape(n // group, DSA_NACC, SUBLANES, w), axis=0)
        return op(x, axis=0)
    return op(x.reshape(n // SUBLANES, SUBLANES, w), axis=0)


def _dsa_kernel(qat_ref, qit_ref, wit_ref, ga_ref, ka_ref, vat_ref, ki_ref, btab_ref, tri_ref, o_ref,
                slab_ref, cell_ref, s_ref, acc_ref, m_ref, l_ref, *, top_k):
    q, kb, wide = DSA_Q, DSA_KB, DSA_WIDE * DSA_KB
    i = pl.program_id(0)
    nkb = i + 1
    nwide = jnp.maximum(i - 1, 0) // DSA_WIDE
    n0 = nwide * DSA_WIDE
    t_row = i * q + lax.broadcasted_iota(jnp.int32, (1, q), 1)
    key_off = lax.broadcasted_iota(jnp.int32, (kb, 1), 0)
    npair = A_HEADS // 2

    def wide_rows(jw):
        return pl.ds(pl.multiple_of(jw * wide, wide), wide)

    def narrow_rows(j):
        return pl.ds(pl.multiple_of(j * kb, kb), kb)

    def pipelined(first, last, produce, consume):
        def body(u, carry):
            t = first + 2 * u
            consume(t, 0, carry)
            consume(t + 1, 1, produce(t + 1, 1))
            return produce(t + 2, 0)

        npairs = (last - first - 1) // 2
        carry = lax.fori_loop(0, npairs, body, produce(first, 0))
        t = first + 2 * npairs
        consume(t, 0, carry)

        @pl.when(t + 1 < last)
        def _():
            consume(t + 1, 1, produce(t + 1, 1))

    w_head = [wit_ref[h:h + 1, :] for h in range(IDX_HEADS)]

    def scores(rows):
        k_idx = ki_ref[rows, :]
        sc = None
        for h in range(IDX_HEADS):
            s = jnp.dot(k_idx, qit_ref[h * LANES:(h + 1) * LANES, :], preferred_element_type=F32)
            term = w_head[h] * jnp.maximum(s, 0.0)
            sc = term if sc is None else sc + term
        return sc

    assert wide % (DSA_CELLS * SUBLANES) == 0
    cell_ref[...] = jnp.full(cell_ref.shape, INT_MIN, jnp.int32)

    def score_rows(rows, n):
        key = _f32_key(scores(rows))
        slab_ref[rows, :] = key
        cells = jnp.max(key.reshape(n // (DSA_CELLS * SUBLANES), DSA_CELLS, SUBLANES, q), axis=0)
        cell_ref[...] = jnp.maximum(cell_ref[...], cells)

    def pair_rows(jp):
        return pl.ds(pl.multiple_of(jp * (2 * wide), 2 * wide), 2 * wide)

    def score_narrow(j, c):
        rows = narrow_rows(j)
        key = _f32_key(jnp.where(j * kb + key_off <= t_row, scores(rows), NEG_INF))
        slab_ref[rows, :] = key
        ncls = kb // SUBLANES
        cell_ref[0:ncls] = jnp.maximum(cell_ref[0:ncls], key.reshape(ncls, SUBLANES, q))
        return c

    def quad_rows(jq):
        return pl.ds(pl.multiple_of(jq * (4 * wide), 4 * wide), 4 * wide)

    def oct_rows(jo):
        return pl.ds(pl.multiple_of(jo * (8 * wide), 8 * wide), 8 * wide)

    lax.fori_loop(0, nwide // 4, lambda jq, c: score_rows(quad_rows(jq), 4 * wide), None)
    lax.fori_loop((nwide // 4) * 2, nwide // 2, lambda jp, c: score_rows(pair_rows(jp), 2 * wide), None)
    lax.fori_loop((nwide // 2) * 2, nwide, lambda jw, c: score_rows(wide_rows(jw), wide), None)
    lax.fori_loop(n0, nkb, score_narrow, 0)

    def count_ge(trial):
        def cnt(blk):
            return _fold_keys(jnp.where(blk >= trial, 1.0, 0.0), jnp.sum)

        ncw = nkb // DSA_WIDE
        acc = lax.fori_loop(0, ncw // 8, lambda jo, a: a + cnt(slab_ref[oct_rows(jo), :]),
                            jnp.zeros((SUBLANES, q), F32))
        acc = lax.fori_loop((ncw // 8) * 2, ncw // 4, lambda jq, a: a + cnt(slab_ref[quad_rows(jq), :]), acc)
        acc = lax.fori_loop((ncw // 4) * 2, ncw // 2, lambda jp, a: a + cnt(slab_ref[pair_rows(jp), :]), acc)
        acc = lax.fori_loop((ncw // 2) * 2, ncw, lambda jw, a: a + cnt(slab_ref[wide_rows(jw), :]), acc)
        acc = lax.fori_loop(ncw * DSA_WIDE, nkb, lambda j, a: a + cnt(slab_ref[narrow_rows(j), :]), acc)
        return jnp.sum(acc, axis=0, keepdims=True).astype(jnp.int32)

    assert DSA_CELLS * SUBLANES >= top_k
    cell_max = cell_ref[...]
    hi_key = jnp.max(jnp.max(cell_max, axis=0), axis=0, keepdims=True)

    def cell_bisect(b, c):
        trial = c + lax.shift_left(jnp.int32(1), 31 - b)
        cnt = jnp.sum(jnp.sum(jnp.where(cell_max >= trial, 1.0, 0.0), axis=0), axis=0, keepdims=True)
        return jnp.where(cnt >= top_k, trial, c)

    lo_key = lax.fori_loop(0, DSA_CELL_BITS, cell_bisect, jnp.full((1, q), INT_MIN, jnp.int32))
    base = lo_key
    span = hi_key - lo_key
    nbits = jnp.max(32 - lax.clz(span))

    def bisect(b, state):
        off, cnt, above = state
        trial_off = off + lax.shift_left(jnp.int32(1), nbits - 1 - b)
        in_span = (trial_off ^ INT_MIN) <= (span ^ INT_MIN)
        c = jnp.where(in_span, count_ge(base + trial_off), 0)
        ok = c >= top_k
        return jnp.where(ok, trial_off, off), jnp.where(ok, c, cnt), jnp.where(ok, above, c)

    uncounted = jnp.full((1, q), -1, jnp.int32)
    off, cnt_thr, cnt_above = lax.fori_loop(0, nbits, bisect, (jnp.zeros((1, q), jnp.int32), uncounted, uncounted))
    thr = base + off
    cnt_thr = lax.cond(jnp.min(cnt_thr) < 0, lambda: count_ge(thr), lambda: cnt_thr)
    tied = jnp.logical_and(cnt_thr > top_k, thr > NEG_INF_KEY)
    any_tie = jnp.max(jnp.where(tied, 1, 0)) > 0

    thr_valid = jnp.maximum(thr, NEG_INF_KEY + 1)

    def mask_fast():
        def body(rows):
            slab_ref[rows, :] = jnp.where(slab_ref[rows, :] >= thr_valid, 0, NEG_INF_BITS)

        lax.fori_loop(0, nwide, lambda jw, c: body(wide_rows(jw)), None)
        lax.fori_loop(n0, nkb, lambda j, c: body(narrow_rows(j)), None)

    def mask_tied():
        above = lax.cond(jnp.min(cnt_above) < 0, lambda: count_ge(thr + 1), lambda: cnt_above)
        thr_c = jnp.maximum(thr, NEG_INF_KEY)
        need = jnp.where(thr_c > NEG_INF_KEY, (top_k - above).astype(F32), -1.0)

        def body(rows, n, seen):
            for g in range(n // kb):
                blk = pl.ds(rows.start + g * kb, kb)
                key = slab_ref[blk, :]
                eq = key == thr_c
                eq_f = jnp.where(eq, 1.0, 0.0)
                before = jnp.dot(tri_ref[...], eq_f.astype(BF16), preferred_element_type=F32)
                tie_bits = jnp.where(seen + before < need, 0, NEG_INF_BITS)
                slab_ref[blk, :] = jnp.where(key > thr_c, 0, jnp.where(eq, tie_bits, NEG_INF_BITS))
                seen = seen + jnp.sum(_fold_keys(eq_f, jnp.sum), axis=0, keepdims=True)
            return seen

        seen = lax.fori_loop(0, nwide // 4, lambda jq, c: body(quad_rows(jq), 4 * wide, c), jnp.zeros((1, q), F32))
        seen = lax.fori_loop((nwide // 4) * 2, nwide // 2, lambda jp, c: body(pair_rows(jp), 2 * wide, c), seen)
        seen = lax.fori_loop((nwide // 2) * 2, nwide, lambda jw, c: body(wide_rows(jw), wide, c), seen)
        lax.fori_loop(n0, nkb, lambda j, c: body(narrow_rows(j), kb, c), seen)

    lax.cond(any_tie, mask_tied, mask_fast)

    m_ref[...] = jnp.full(m_ref.shape, M_INIT, F32)
    l_ref[...] = jnp.zeros(l_ref.shape, F32)
    acc_ref[...] = jnp.zeros(acc_ref.shape, F32)

    def logits(rows, n, buf, rel):
        mask = lax.bitcast_convert_type(slab_ref[rows, :], F32)
        mask2 = jnp.concatenate([mask, mask], axis=1)
        tile_max = []
        for p in range(npair):
            ls = slice(p * LANES, (p + 1) * LANES)
            w_q = jnp.concatenate([qat_ref[(2 * p) * LANES:(2 * p + 1) * LANES, :],
                                   qat_ref[(2 * p + 1) * LANES:(2 * p + 2) * LANES, :]], axis=1)
            s = jnp.dot(ka_ref[rows, ls], w_q, preferred_element_type=F32) + mask2
            if rel is not None:
                s = s + btab_ref[rel, p]
            s_ref[buf, p, 0:n, :] = s
            tile_max.append(_fold_keys(s, jnp.max))
        return tuple(tile_max)

    def attend(rows, n, buf, tile_max):
        for p in range(npair):
            ls = slice(p * LANES, (p + 1) * LANES)
            m_old = m_ref[p]
            m_new = jnp.maximum(m_old, jnp.max(tile_max[p], axis=0, keepdims=True))
            alpha = jnp.exp2(m_old - m_new)
            pexp = jnp.exp2(s_ref[buf, p, 0:n, :] - m_new[0:1])
            m_ref[p] = m_new
            l_ref[p] = alpha * l_ref[p] + _fold_keys(pexp, jnp.sum)
            acc_ref[p] = alpha[0:1] * acc_ref[p] + jnp.dot(vat_ref[ls, rows], pexp.astype(BF16),
                                                           preferred_element_type=F32)

    @pl.when(nwide >= 2)
    def _():
        pipelined(0, nwide // 2, lambda jp, buf: logits(pair_rows(jp), 2 * wide, buf, None),
                  lambda jp, buf, tile_max: attend(pair_rows(jp), 2 * wide, buf, tile_max))

    @pl.when(nwide % 2 == 1)
    def _():
        attend(wide_rows(nwide - 1), wide, 0, logits(wide_rows(nwide - 1), wide, 0, None))

    pipelined(n0, nkb, lambda j, buf: logits(narrow_rows(j), kb, buf, jnp.minimum(i - j, 2)),
              lambda j, buf, tile_max: attend(narrow_rows(j), kb, buf, tile_max))

    for p in range(npair):
        o_t = acc_ref[p] / jnp.sum(l_ref[p], axis=0, keepdims=True)
        o_pair = jnp.concatenate([o_t[:A_HEAD_DIM, :q], o_t[A_HEAD_DIM:, q:]], axis=0).T
        ls = slice(p * LANES, (p + 1) * LANES)
        o_ref[:, ls] = (o_pair * _silu(ga_ref[:, ls])).astype(o_ref.dtype)


def _dsa_bias_table(rel_bias):
    ii = np.arange(DSA_Q)[:, None]
    jj = np.arange(DSA_KB)[None, :]
    buckets = np.stack([_t5_bucket_np(ii - jj), _t5_bucket_np(DSA_KB + ii - jj),
                        np.full((DSA_Q, DSA_KB), REL_BUCKETS - 1)])
    assert (_t5_bucket_np(np.arange(DSA_KB + 1, 4 * DSA_KB)) == REL_BUCKETS - 1).all()
    rb = rel_bias.astype(F32) * LOG2E
    rb = rb - rb[REL_BUCKETS - 1]
    onehot = np.zeros((3, DSA_KB, DSA_Q, REL_BUCKETS), np.float32)
    np.put_along_axis(onehot, np.transpose(buckets, (0, 2, 1))[..., None], 1.0, axis=-1)
    tab = jnp.einsum('tsrb,bh->thsr', jnp.asarray(onehot), rb, precision=lax.Precision.HIGHEST)
    tab = tab.reshape(3, A_HEADS // 2, 2, DSA_KB, DSA_Q)
    return jnp.transpose(tab, (0, 1, 3, 2, 4)).reshape(3, A_HEADS // 2, DSA_KB, 2 * DSA_Q)


def _dsa(qa_t, qi_t, wi_t, ga, ka, va_t, ki, btab, top_k):
    s = ga.shape[0]
    q, kb = DSA_Q, DSA_KB
    tri = jnp.asarray(np.tril(np.ones((kb, kb), np.float32), -1), BF16)
    whole = lambda shape: pl.BlockSpec(shape, lambda i: (0,) * len(shape), pipeline_mode=pl.Buffered(1))
    return pl.pallas_call(
        functools.partial(_dsa_kernel, top_k=top_k),
        grid=(s // q,),
        in_specs=[pl.BlockSpec((A_HEADS * LANES, q), lambda i: (0, i)),
                  pl.BlockSpec((IDX_HEADS * LANES, q), lambda i: (0, i)),
                  pl.BlockSpec((SUBLANES, q), lambda i: (0, i)),
                  pl.BlockSpec((q, A_W), lambda i: (i, 0)),
                  whole((s, A_W)), whole((A_W, s)), whole((s, LANES)),
                  whole((3, A_HEADS // 2, kb, 2 * q)), whole((kb, kb))],
        out_specs=pl.BlockSpec((q, A_W), lambda i: (i, 0)),
        out_shape=jax.ShapeDtypeStruct((s, A_W), BF16),
        scratch_shapes=[pltpu.VMEM((s, q), jnp.int32),
                        pltpu.VMEM((DSA_CELLS, SUBLANES, q), jnp.int32),
                        pltpu.VMEM((2, A_HEADS // 2, 2 * DSA_WIDE * kb, 2 * q), F32),
                        pltpu.VMEM((A_HEADS // 2, LANES, 2 * q), F32),
                        pltpu.VMEM((A_HEADS // 2, SUBLANES, 2 * q), F32),
                        pltpu.VMEM((A_HEADS // 2, SUBLANES, 2 * q), F32)],
        compiler_params=pltpu.CompilerParams(dimension_semantics=("arbitrary",), vmem_limit_bytes=DSA_VMEM_LIMIT),
        name="dsa",
    )(qa_t, qi_t, wi_t, ga, ka, va_t, ki, btab, tri)


GLA_BLOCK = 512


def _gla_kernel(qb_ref, kb_ref, vb_ref, gb_ref, lr_ref, wlr_ref, blr_ref, gn_ref, tril_ref, o_ref,
                state_ref, b_ref, att_ref):
    c, sub = GLA_CHUNK, GLA_SUB
    nsub = c // sub

    @pl.when(pl.program_id(0) == 0)
    def _():
        state_ref[...] = jnp.zeros(state_ref.shape, F32)

    dn_t = (((1,), (1,)), ((), ()))
    dn_0 = (((0,), (0,)), ((), ()))
    hp = lax.Precision.HIGHEST
    row_i = lax.broadcasted_iota(jnp.int32, (sub, sub), 0)
    col_j = lax.broadcasted_iota(jnp.int32, (sub, sub), 1)

    def chunk(ci, carry):
        r0 = pl.multiple_of(ci * c, c)
        z = jnp.dot(lr_ref[pl.ds(r0, c), :], wlr_ref[...], precision=hp, preferred_element_type=F32) + blr_ref[...]
        log_a = (jnp.minimum(z, 0.0) - jnp.log1p(jnp.exp(-jnp.abs(z)))) * (LOG2E / GLA_TAU)
        b_ref[...] = jnp.dot(tril_ref[...], log_a, precision=hp, preferred_element_type=F32)
        for h in range(B_HEADS):
            ls = slice(h * LANES, (h + 1) * LANES)
            b = b_ref[:, ls]
            qh = qb_ref[pl.ds(r0, c), ls]
            kh = kb_ref[pl.ds(r0, c), ls]
            vh = vb_ref[pl.ds(r0, c), ls]
            state_t = state_ref[h]
            b_last = b[c - 1:c, :]
            o = lax.dot_general((qh * jnp.exp2(b)).astype(BF16), state_t.astype(BF16), dn_t,
                                preferred_element_type=F32)
            for si in range(nsub):
                rs = slice(si * sub, (si + 1) * sub)
                b_i, q_i = b[rs], qh[rs]
                if si > 0:
                    ref_row = b[si * sub:si * sub + 1, :]
                    q_t = (q_i * jnp.exp2(b_i - ref_row)).astype(BF16)
                    k_t = (kh[:si * sub] * jnp.exp2(ref_row - b[:si * sub])).astype(BF16)
                    att_ref[rs, :si * sub] = lax.dot_general(q_t, k_t, dn_t, preferred_element_type=F32)
                k_i = kh[rs]
                diag = jnp.zeros((sub, sub), F32)
                for jl in range(sub):
                    diff = jnp.minimum(b_i - b_i[jl:jl + 1, :], 0.0)
                    col = jnp.sum(q_i * k_i[jl:jl + 1, :] * jnp.exp2(diff), axis=-1, keepdims=True)
                    diag = jnp.where(col_j == jl, col, diag)
                att_ref[rs, rs] = jnp.where(col_j <= row_i, diag, 0.0)
                if si + 1 < nsub:
                    att_ref[rs, (si + 1) * sub:] = jnp.zeros((sub, c - (si + 1) * sub), F32)
            o = o + jnp.dot(att_ref[...].astype(BF16), vh.astype(BF16), preferred_element_type=F32)
            k_dec = (kh * jnp.exp2(b_last - b)).astype(BF16)
            state_ref[h] = (jnp.exp2(b_last) * state_t
                            + lax.dot_general(vh.astype(BF16), k_dec, dn_0, preferred_element_type=F32))
            o = o * lax.rsqrt(jnp.mean(o * o, axis=-1, keepdims=True) + EPS) * gn_ref[...]
            o_ref[pl.ds(r0, c), ls] = (o * _silu(gb_ref[pl.ds(r0, c), ls])).astype(o_ref.dtype)
        return carry

    lax.fori_loop(0, GLA_BLOCK // c, chunk, 0)


def _gla(qb, kb, vb, gb, lr, wlr_pad, blr_pad, gn):
    s = qb.shape[0]
    blk = min(GLA_BLOCK, s)
    assert blk == GLA_BLOCK and s % blk == 0
    tril = jnp.asarray(np.tril(np.ones((GLA_CHUNK, GLA_CHUNK), np.float32)))
    row = lambda wd: pl.BlockSpec((blk, wd), lambda i: (i, 0))
    const = lambda shape: pl.BlockSpec(shape, lambda i: (0,) * len(shape))
    return pl.pallas_call(
        _gla_kernel,
        grid=(s // blk,),
        in_specs=[row(B_HEADS * LANES), row(B_HEADS * LANES), row(B_VW), row(B_VW), row(LANES),
                  const((LANES, B_HEADS * LANES)), const((1, B_HEADS * LANES)), const((1, B_DV)),
                  const((GLA_CHUNK, GLA_CHUNK))],
        out_specs=row(B_VW),
        out_shape=jax.ShapeDtypeStruct((s, B_VW), BF16),
        scratch_shapes=[pltpu.VMEM((B_HEADS, LANES, B_DV), F32),
                        pltpu.VMEM((GLA_CHUNK, B_HEADS * LANES), F32),
                        pltpu.VMEM((GLA_CHUNK, GLA_CHUNK), F32)],
        compiler_params=pltpu.CompilerParams(dimension_semantics=("arbitrary",), vmem_limit_bytes=VMEM_LIMIT),
        name="gla",
    )(qb, kb, vb, gb, lr, wlr_pad, blr_pad, gn, tril)


def _mid_kernel(x_ref, ma_ref, mb_ref, wo_ref, g_ref, wi_ref, x1_ref, u_ref, sg_ref):
    x1 = (x_ref[...]
          + jnp.dot(ma_ref[...], wo_ref[:A_W, :], preferred_element_type=F32)
          + jnp.dot(mb_ref[...], wo_ref[A_W:, :], preferred_element_type=F32))
    x1_ref[...] = x1
    h = (x1 * lax.rsqrt(jnp.mean(x1 * x1, axis=-1, keepdims=True) + EPS) * g_ref[...]).astype(BF16)
    val = jnp.dot(h, wi_ref[:, :D_MODEL], preferred_element_type=F32)
    glu = jnp.dot(h, wi_ref[:, D_MODEL:2 * D_MODEL], preferred_element_type=F32)
    gate = jnp.dot(h, wi_ref[:, 2 * D_MODEL:], preferred_element_type=F32)
    u_ref[...] = val * jax.nn.sigmoid(glu)
    sg_ref[...] = _silu(gate)


def _mid(x2d, mix_a, mix_b, w_out, g, w_in, tm):
    s = x2d.shape[0]
    row = lambda wd: pl.BlockSpec((tm, wd), lambda i: (i, 0))
    const = lambda shape: pl.BlockSpec(shape, lambda i: (0,) * len(shape))
    return pl.pallas_call(
        _mid_kernel,
        grid=(s // tm,),
        in_specs=[row(D_MODEL), row(A_W), row(B_VW), const((A_W + B_VW, D_MODEL)), const((1, D_MODEL)),
                  const((D_MODEL, 3 * D_MODEL))],
        out_specs=[row(D_MODEL), row(D_MODEL), row(D_MODEL)],
        out_shape=[jax.ShapeDtypeStruct((s, D_MODEL), F32)] * 3,
        compiler_params=pltpu.CompilerParams(dimension_semantics=("arbitrary",), vmem_limit_bytes=VMEM_LIMIT),
        name="mid",
    )(x2d, mix_a, mix_b, w_out, g, w_in)


HALO = 32
CONV_ROWS = 128


def _odd_out_kernel(u_ref, halo_ref, sg_ref, x1_ref, cw_ref, cb_ref, lg_ref, lb_ref, wo_ref, nf_ref, o_ref,
                    ubuf_ref, y_ref, shift_ref, *, tm):
    i = pl.program_id(0)
    ubuf_ref[HALO:, :] = u_ref[...]
    ubuf_ref[:HALO, :] = jnp.where(i > 0, halo_ref[...], 0.0)
    base = HALO - (CONV_WIDTH - 1)
    for cs in range(D_MODEL // LANES):
        ls = slice(cs * LANES, (cs + 1) * LANES)
        for r0 in range(0, tm, CONV_ROWS):
            acc = jnp.zeros((CONV_ROWS, LANES), F32)
            for res in range(SUBLANES):
                taps = [k for k in range(CONV_WIDTH) if (base + k) % SUBLANES == res]
                first = base + taps[0]
                rows = taps[-1] - taps[0] + CONV_ROWS
                shift_ref[0:rows, :] = ubuf_ref[r0 + first:r0 + first + rows, ls]
                aligned = shift_ref[0:rows, :]
                for k in taps:
                    a = base + k - first
                    acc = acc + aligned[a:a + CONV_ROWS] * cw_ref[k:k + 1, ls]
            y_ref[r0:r0 + CONV_ROWS, ls] = acc + cb_ref[:, ls]
    y = y_ref[...]
    mu = jnp.mean(y, axis=-1, keepdims=True)
    yc = y - mu
    var = jnp.mean(yc * yc, axis=-1, keepdims=True)
    yn = yc * lax.rsqrt(var + EPS) * lg_ref[...] + lb_ref[...]
    mixed = (_silu(yn) * sg_ref[...]).astype(BF16)
    x2 = x1_ref[...] + jnp.dot(mixed, wo_ref[...], preferred_element_type=F32)
    o_ref[...] = x2 * lax.rsqrt(jnp.mean(x2 * x2, axis=-1, keepdims=True) + EPS) * nf_ref[...]


def _odd_out(u, sg, x1, cw_pad, cb, lg, lb, w_out, nf, tm):
    s = u.shape[0]
    row = pl.BlockSpec((tm, D_MODEL), lambda i: (i, 0))
    const = lambda shape: pl.BlockSpec(shape, lambda i: (0,) * len(shape))
    halo = pl.BlockSpec((HALO, D_MODEL), lambda i: (jnp.maximum(i * (tm // HALO) - 1, 0), 0))
    return pl.pallas_call(
        functools.partial(_odd_out_kernel, tm=tm),
        grid=(s // tm,),
        in_specs=[row, halo, row, row, const((HALO, D_MODEL)), const((1, D_MODEL)), const((1, D_MODEL)),
                  const((1, D_MODEL)), const((D_MODEL, D_MODEL)), const((1, D_MODEL))],
        out_specs=row,
        out_shape=jax.ShapeDtypeStruct((s, D_MODEL), F32),
        scratch_shapes=[pltpu.VMEM((HALO + tm, D_MODEL), F32), pltpu.VMEM((tm, D_MODEL), F32),
                        pltpu.VMEM((HALO + CONV_ROWS, LANES), F32)],
        compiler_params=pltpu.CompilerParams(dimension_semantics=("arbitrary",), vmem_limit_bytes=VMEM_LIMIT),
        name="odd_out",
    )(u, u, sg, x1, cw_pad, cb, lg, lb, w_out, nf)


def kernel(x, rel_bias, norm_even, w_in_even, w_gla_lr, b_gla_lr, gla_norm, w_out_even, norm_odd, w_in_odd,
           conv_w, conv_b, conv_ln_g, conv_ln_b, w_out_odd, norm_f):
    bsz, s, d = x.shape
    assert bsz == 1 and d == D_MODEL and s % GLA_BLOCK == 0
    assert norm_even.shape[0] == 1 and norm_odd.shape[0] == 1
    top_k = min(TOPK_MAX, s // 4)
    tm = ROW_BLOCK
    x2d = x.reshape(s, d)

    wt_pad = _even_in_weights(w_in_even[0])
    qa_t, ka, va_t, ga, qi_t, ki, wi_t, qb, kb, vb, gb, lr = _even_in(x2d, norm_even[0].reshape(1, d), wt_pad, tm)
    mix_a = _dsa(qa_t, qi_t, wi_t, ga, ka, va_t, ki, _dsa_bias_table(rel_bias), top_k)

    head_pad = ((0, 0), (0, 0), (0, LANES - B_DK))
    wlr_pad = jnp.pad(w_gla_lr[0].reshape(GLA_RANK, B_HEADS, B_DK), head_pad).reshape(GLA_RANK, B_HEADS * LANES)
    wlr_pad = jnp.pad(wlr_pad, ((0, LANES - GLA_RANK), (0, 0)))
    blr_pad = jnp.pad(b_gla_lr[0].reshape(1, B_HEADS, B_DK), head_pad).reshape(1, B_HEADS * LANES)
    mix_b = _gla(qb, kb, vb, gb, lr, wlr_pad, blr_pad, gla_norm[0].reshape(1, B_DV))

    x1, u, sg = _mid(x2d, mix_a, mix_b, w_out_even[0].astype(BF16), norm_odd[0].reshape(1, d),
                     w_in_odd[0].astype(BF16), tm)

    cw_pad = jnp.zeros((HALO, d), F32).at[:CONV_WIDTH].set(conv_w[0])
    out = _odd_out(u, sg, x1, cw_pad, conv_b[0].reshape(1, d), conv_ln_g[0].reshape(1, d),
                   conv_ln_b[0].reshape(1, d), w_out_odd[0].astype(BF16), norm_f.reshape(1, d), tm)
    return out.reshape(bsz, s, d)
```
